```python
import math
import jax, jax.numpy as jnp
from jax import lax
import numpy as np

D_MODEL = 1024
BATCH = 8
SEQ = 2048
DEPTH = 1
DEC_BATCH = 128
DEC_SEQ = 1
PAST_LEN = 16384
PAGE_SIZE = 128

RET_HEADS = 4
RET_DK = 128
RET_DV = 256
RET_QK = RET_HEADS * RET_DK
RET_V = RET_HEADS * RET_DV
GDN_HEADS = 8
GDN_DK = 128
GDN_DV = 128
GDN_QK = GDN_HEADS * GDN_DK
GDN_V = GDN_HEADS * GDN_DV
CONV_W = 4
CONV_CH = 2 * GDN_QK + GDN_V
CHUNK = 64
N_MEM = 256
X_HEADS = 4
X_HD = D_MODEL // X_HEADS
D_FF = ((8 * D_MODEL + 3 * 256 - 1) // (3 * 256)) * 256
ROPE_BASE = 10000.0
EPS = 1e-6
IN_SIZES = (RET_QK, RET_QK, RET_V, RET_V, CONV_CH, GDN_V, GDN_HEADS, GDN_HEADS, D_MODEL, D_MODEL)
D_IN = RET_QK * 2 + RET_V * 2 + CONV_CH + GDN_V + 2 * GDN_HEADS + 2 * D_MODEL

kernel_name = 'hybrid_retention_gdn_memxattn_step'


def _offsets(sizes):
    out, acc = [], 0
    for s in sizes[:-1]:
        acc += s
        out.append(acc)
    return out


def rmsnorm(x, g):
    xf = x.astype(jnp.float32)
    return xf * lax.rsqrt(jnp.mean(xf * xf, axis=-1, keepdims=True) + EPS) * g


def l2norm(x):
    return x * lax.rsqrt(jnp.sum(x * x, axis=-1, keepdims=True) + EPS)


def rotary(x, pos):
    half = x.shape[-1] // 2
    inv = ROPE_BASE ** (-jnp.arange(half, dtype=jnp.float32) / half)
    ang = pos.astype(jnp.float32)[:, None] * inv[None, :]
    cos, sin = jnp.cos(ang)[None, :, None, :], jnp.sin(ang)[None, :, None, :]
    x1, x2 = x[..., :half], x[..., half:]
    return jnp.concatenate([x1 * cos - x2 * sin, x1 * sin + x2 * cos], axis=-1)


def chunk_size(t):
    return t if t <= CHUNK else math.gcd(t, CHUNK)


def to_chunks(x, c):
    b, t = x.shape[:2]
    return jnp.moveaxis(x.reshape((b, t // c, c) + x.shape[2:]), 1, 0)


def from_chunks(o):
    n, b, c = o.shape[:3]
    return jnp.moveaxis(o, 0, 1).reshape((b, n * c) + o.shape[3:])


def retention_chunked(q, k, v, s0):
    h = q.shape[2]
    c = chunk_size(q.shape[1])
    log_g = jnp.log1p(-jnp.exp2(-5.0 - jnp.arange(h, dtype=jnp.float32)))
    idx = jnp.arange(c, dtype=jnp.float32)
    diff = idx[:, None] - idx[None, :]
    causal = diff >= 0
    intra_decay = jnp.where(causal[None], jnp.exp(jnp.maximum(diff, 0.0)[None] * log_g[:, None, None]), 0.0)
    q_decay = jnp.exp((idx + 1.0)[:, None] * log_g[None, :])
    k_decay = jnp.exp((c - 1.0 - idx)[:, None] * log_g[None, :])
    chunk_decay = jnp.exp(c * log_g)

    def step(s, xs):
        qc, kc, vc = xs
        scores = jnp.einsum('bihk,bjhk->bhij', qc, kc) * intra_decay
        o = jnp.einsum('bhij,bjhv->bihv', scores, vc) + jnp.einsum('bihk,bhkv->bihv', qc * q_decay[None, :, :, None], s)
        s = chunk_decay[None, :, None, None] * s + jnp.einsum('bjhk,bjhv->bhkv', kc * k_decay[None, :, :, None], vc)
        return s, o

    s, o = lax.scan(step, s0, (to_chunks(q, c), to_chunks(k, c), to_chunks(v, c)))
    return from_chunks(o), s


def gated_delta_chunked(q, k, v, beta, g, s0):
    dv = v.shape[-1]
    c = chunk_size(q.shape[1])
    idx = jnp.arange(c)
    incl = idx[:, None] >= idx[None, :]
    strict = idx[:, None] > idx[None, :]
    eye = jnp.eye(c, dtype=jnp.float32)

    def step(s, xs):
        qc, kc, vc, bc, gc = xs
        gcum = jnp.cumsum(gc, axis=1)
        gh = jnp.transpose(gcum, (0, 2, 1))
        dg = gh[:, :, :, None] - gh[:, :, None, :]
        gamma = jnp.where(incl, jnp.exp(jnp.where(incl, dg, 0.0)), 0.0)
        kk = jnp.einsum('bihk,bjhk->bhij', kc, kc)
        a = jnp.where(strict, kk * gamma * jnp.transpose(bc, (0, 2, 1))[..., None], 0.0)
        rhs = jnp.concatenate([vc * bc[..., None], kc * (bc * jnp.exp(gcum))[..., None]], axis=-1)
        rhs = jnp.transpose(rhs, (0, 2, 1, 3))
        sol = lax.linalg.triangular_solve(a + eye, rhs, left_side=True, lower=True, unit_diagonal=True)
        u, w = sol[..., :dv], sol[..., dv:]
        v_new = u - jnp.einsum('bhik,bhkv->bhiv', w, s)
        qk = jnp.where(incl, jnp.einsum('bihk,bjhk->bhij', qc, kc) * gamma, 0.0)
        qh = jnp.transpose(qc, (0, 2, 1, 3)) * jnp.exp(gh)[..., None]
        o = jnp.einsum('bhik,bhkv->bhiv', qh, s) + jnp.einsum('bhij,bhjv->bhiv', qk, v_new)
        g_last = gh[:, :, -1]
        kh = jnp.transpose(kc, (0, 2, 1, 3)) * jnp.exp(g_last[..., None] - gh)[..., None]
        s = jnp.exp(g_last)[..., None, None] * s + jnp.einsum('bhjk,bhjv->bhkv', kh, v_new)
        return s, jnp.transpose(o, (0, 2, 1, 3))

    xs = (to_chunks(q, c), to_chunks(k, c), to_chunks(v, c), to_chunks(beta, c), to_chunks(g, c))
    s, o = lax.scan(step, s0, xs)
    return from_chunks(o), s


def causal_conv(xc, buf, w):
    t = xc.shape[1]
    full = jnp.concatenate([buf.astype(jnp.float32), xc], axis=1)
    out = full[:, 0:t] * w[0]
    for i in range(1, CONV_W):
        out = out + full[:, i:i + t] * w[i]
    return jax.nn.silu(out), full[:, t:]


def mixer_block(h, s_ret, s_gdn, conv_buf, pos0, w_in, ret_gn_g, w_branch_a, gdn_conv_w, gdn_a_log,
                gdn_dt_bias, gdn_norm_g, w_branch_b, w_out):
    b, t, _ = h.shape
    f32 = jnp.float32
    proj = (h @ w_in).astype(f32)
    rq, rk, rv, rg, qkv, z, bb, aa, ga, gb = jnp.split(proj, _offsets(IN_SIZES), axis=-1)
    pos = pos0 + jnp.arange(t)
    rq = rotary(rq.reshape(b, t, RET_HEADS, RET_DK), pos)
    rk = rotary(rk.reshape(b, t, RET_HEADS, RET_DK), pos) * (RET_DK ** -0.5)
    rv = rv.reshape(b, t, RET_HEADS, RET_DV)
    o_r, s_ret_new = retention_chunked(rq, rk, rv, s_ret.astype(f32))
    mu = jnp.mean(o_r, axis=-1, keepdims=True)
    var = jnp.mean(jnp.square(o_r - mu), axis=-1, keepdims=True)
    o_r = ((o_r - mu) * lax.rsqrt(var + EPS)).reshape(b, t, RET_V) * ret_gn_g
    y_a = (jax.nn.silu(rg) * o_r) @ w_branch_a
    qkv, conv_new = causal_conv(qkv, conv_buf, gdn_conv_w.astype(f32))
    gq, gk, gv = jnp.split(qkv, [GDN_QK, 2 * GDN_QK], axis=-1)
    gq = l2norm(gq.reshape(b, t, GDN_HEADS, GDN_DK)) * (GDN_DK ** -0.5)
    gk = l2norm(gk.reshape(b, t, GDN_HEADS, GDN_DK))
    gv = gv.reshape(b, t, GDN_HEADS, GDN_DV)
    beta = jax.nn.sigmoid(bb)
    g = -jnp.exp(gdn_a_log.astype(f32)) * jax.nn.softplus(aa + gdn_dt_bias)
    o_g, s_gdn_new = gated_delta_chunked(gq, gk, gv, beta, g, s_gdn.astype(f32))
    o_g = rmsnorm(o_g, gdn_norm_g) * jax.nn.silu(z.reshape(b, t, GDN_HEADS, GDN_DV))
    y_b = o_g.reshape(b, t, GDN_V) @ w_branch_b
    merged = jax.nn.sigmoid(ga) * y_a + jax.nn.sigmoid(gb) * y_b
    return merged @ w_out, s_ret_new, s_gdn_new, conv_new


def mem_kv(mem, mem_norm_g, w_xk, w_xv):
    b, m, _ = mem.shape
    mn = rmsnorm(mem, mem_norm_g)
    return (mn @ w_xk).reshape(b, m, X_HEADS, X_HD), (mn @ w_xv).reshape(b, m, X_HEADS, X_HD)


def cross_attn(h, mk, mv, w_xq, w_xo):
    b, t, _ = h.shape
    q = (h @ w_xq).reshape(b, t, X_HEADS, X_HD).astype(jnp.float32)
    s = jnp.einsum('bthd,bmhd->bhtm', q, mk.astype(jnp.float32)) * (X_HD ** -0.5)
    p = jax.nn.softmax(s, axis=-1)
    o = jnp.einsum('bhtm,bmhd->bthd', p, mv.astype(jnp.float32)).reshape(b, t, D_MODEL)
    return o @ w_xo


def swiglu(h, w_gate, w_up, w_down):
    return (jax.nn.silu(h @ w_gate) * (h @ w_up)) @ w_down


def layer(x, s_ret, s_gdn, conv_buf, mk, mv, pos0, norm_mix_g, w_in, ret_gn_g, w_branch_a, gdn_conv_w,
          gdn_a_log, gdn_dt_bias, gdn_norm_g, w_branch_b, w_out, norm_x_g, w_xq, w_xo, norm_ffn_g,
          w_gate, w_up, w_down):
    y, s_ret, s_gdn, conv_buf = mixer_block(rmsnorm(x, norm_mix_g), s_ret, s_gdn, conv_buf, pos0, w_in,
                                            ret_gn_g, w_branch_a, gdn_conv_w, gdn_a_log, gdn_dt_bias,
                                            gdn_norm_g, w_branch_b, w_out)
    x = x + y
    x = x + cross_attn(rmsnorm(x, norm_x_g), mk, mv, w_xq, w_xo)
    x = x + swiglu(rmsnorm(x, norm_ffn_g), w_gate, w_up, w_down)
    return x, s_ret, s_gdn, conv_buf


def setup_inputs(seed: int = 0) -> dict:
    key = jax.random.key(seed)
    ks = jax.random.split(key, 40)
    f32 = jnp.float32
    L = DEPTH

    def nrm(k, shape, scale):
        return jax.random.normal(k, shape, f32) * scale

    def gain(k, shape):
        return 1.0 + 0.01 * jax.random.normal(k, shape, f32)

    dt = jnp.exp(jax.random.uniform(ks[0], (L, GDN_HEADS), f32, math.log(1e-3), math.log(1e-1)))
    dt_bias = dt + jnp.log(-jnp.expm1(-dt))
    a_log = jnp.log(jax.random.uniform(ks[1], (L, GDN_HEADS), f32, 1.0, 16.0))
    return {
        'x_prompt': nrm(ks[2], (BATCH, SEQ, D_MODEL), 1.0),
        'x_sample': nrm(ks[3], (DEC_BATCH, DEC_SEQ, D_MODEL), 1.0),
        'state_ret': nrm(ks[4], (L, DEC_BATCH, RET_HEADS, RET_DK, RET_DV), 0.1),
        'state_gdn': nrm(ks[5], (L, DEC_BATCH, GDN_HEADS, GDN_DK, GDN_DV), 0.1),
        'state_conv': nrm(ks[6], (L, DEC_BATCH, CONV_W - 1, CONV_CH), 1.0),
        'cache_mem_k': nrm(ks[7], (L, DEC_BATCH, N_MEM, X_HEADS, X_HD), 1.0),
        'cache_mem_v': nrm(ks[8], (L, DEC_BATCH, N_MEM, X_HEADS, X_HD), 1.0),
        'mem_prompt': nrm(ks[9], (BATCH, N_MEM, D_MODEL), 1.0),
        'norm_mix_g': gain(ks[10], (L, D_MODEL)),
        'w_in': nrm(ks[11], (L, D_MODEL, D_IN), D_MODEL ** -0.5),
        'ret_gn_g': gain(ks[12], (L, RET_V)),
        'w_branch_a': nrm(ks[13], (L, RET_V, D_MODEL), RET_V ** -0.5),
        'gdn_conv_w': nrm(ks[14], (L, CONV_W, CONV_CH), CONV_W ** -0.5),
        'gdn_a_log': a_log,
        'gdn_dt_bias': dt_bias,
        'gdn_norm_g': gain(ks[15], (L, GDN_DV)),
        'w_branch_b': nrm(ks[16], (L, GDN_V, D_MODEL), GDN_V ** -0.5),
        'w_out': nrm(ks[17], (L, D_MODEL, D_MODEL), D_MODEL ** -0.5),
        'norm_x_g': gain(ks[18], (L, D_MODEL)),
        'mem_norm_g': gain(ks[19], (L, D_MODEL)),
        'w_xq': nrm(ks[20], (L, D_MODEL, D_MODEL), D_MODEL ** -0.5),
        'w_xk': nrm(ks[21], (L, D_MODEL, D_MODEL), D_MODEL ** -0.5),
        'w_xv': nrm(ks[22], (L, D_MODEL, D_MODEL), D_MODEL ** -0.5),
        'w_xo': nrm(ks[23], (L, D_MODEL, D_MODEL), D_MODEL ** -0.5),
        'norm_ffn_g': gain(ks[24], (L, D_MODEL)),
        'w_gate': nrm(ks[25], (L, D_MODEL, D_FF), D_MODEL ** -0.5),
        'w_up': nrm(ks[26], (L, D_MODEL, D_FF), D_MODEL ** -0.5),
        'w_down': nrm(ks[27], (L, D_FF, D_MODEL), D_FF ** -0.5),
        'norm_final_g': gain(ks[28], (D_MODEL,)),
    }


def reference(x_prompt, x_sample, state_ret, state_gdn, state_conv, cache_mem_k, cache_mem_v, mem_prompt,
              norm_mix_g, w_in, ret_gn_g, w_branch_a, gdn_conv_w, gdn_a_log, gdn_dt_bias, gdn_norm_g,
              w_branch_b, w_out, norm_x_g, mem_norm_g, w_xq, w_xk, w_xv, w_xo, norm_ffn_g, w_gate, w_up,
              w_down, norm_final_g):
    f32 = jnp.float32
    bp = x_prompt.shape[0]
    xp, xs = x_prompt, x_sample
    sr_p_l, sg_p_l, sc_p_l, mk_p_l, mv_p_l, sr_s_l, sg_s_l, sc_s_l = [], [], [], [], [], [], [], []
    for l in range(DEPTH):
        lw = (norm_mix_g[l], w_in[l], ret_gn_g[l], w_branch_a[l], gdn_conv_w[l], gdn_a_log[l], gdn_dt_bias[l],
              gdn_norm_g[l], w_branch_b[l], w_out[l], norm_x_g[l], w_xq[l], w_xo[l], norm_ffn_g[l], w_gate[l],
              w_up[l], w_down[l])
        mk_p, mv_p = mem_kv(mem_prompt, mem_norm_g[l], w_xk[l], w_xv[l])
        xp, sr_p, sg_p, sc_p = layer(xp, jnp.zeros((bp, RET_HEADS, RET_DK, RET_DV), f32),
                                     jnp.zeros((bp, GDN_HEADS, GDN_DK, GDN_DV), f32),
                                     jnp.zeros((bp, CONV_W - 1, CONV_CH), f32), mk_p, mv_p, 0, *lw)
        xs, sr_s, sg_s, sc_s = layer(xs, state_ret[l], state_gdn[l], state_conv[l], cache_mem_k[l],
                                     cache_mem_v[l], PAST_LEN, *lw)
        sr_p_l.append(sr_p); sg_p_l.append(sg_p); sc_p_l.append(sc_p)
        mk_p_l.append(mk_p); mv_p_l.append(mv_p)
        sr_s_l.append(sr_s); sg_s_l.append(sg_s); sc_s_l.append(sc_s)
    y_prompt = rmsnorm(xp, norm_final_g)
    y_sample = rmsnorm(xs, norm_final_g)
    return (y_prompt, y_sample, jnp.stack(sr_p_l), jnp.stack(sg_p_l), jnp.stack(sc_p_l), jnp.stack(mk_p_l),
            jnp.stack(mv_p_l), jnp.stack(sr_s_l), jnp.stack(sg_s_l), jnp.stack(sc_s_l))
```

```python
import functools

import numpy as np
import jax
import jax.numpy as jnp
from jax import lax
from jax.experimental import pallas as pl
from jax.experimental.pallas import tpu as pltpu

F32 = jnp.float32
BF16 = jnp.bfloat16

D_MODEL = 1024
RET_HEADS, RET_DK, RET_DV = 4, 128, 256
GDN_HEADS, GDN_DK, GDN_DV = 8, 128, 128
CONV_W = 4
CONV_CH = 3 * GDN_HEADS * GDN_DK
N_MEM, X_HEADS, X_HD = 256, 4, 256
PAST_LEN = 16384
ROPE_BASE = 10000.0
EPS = 1e-6
GDN_CHUNK = 64

COL_RET = 0
COL_GDN = 3072
COL_GATE = 7168
N_MAIN = 9216
BA_PAD = 128

VMEM_LIMIT = 56 * 1024 * 1024

NT_DIMS = (((1,), (1,)), ((), ()))
TN_DIMS = (((0,), (0,)), ((), ()))


def _mm(a, b):
    return jnp.dot(a.astype(BF16), b.astype(BF16), preferred_element_type=F32)


def _mm_nt(a, b):
    return lax.dot_general(a.astype(BF16), b.astype(BF16), NT_DIMS, preferred_element_type=F32)


def _mm_tn(a, b):
    return lax.dot_general(a.astype(BF16), b.astype(BF16), TN_DIMS, preferred_element_type=F32)


def _mm_f32(a, b):
    return jnp.dot(a, b, preferred_element_type=F32, precision=lax.Precision.HIGHEST)


def _rms(x, g):
    return x * lax.rsqrt(jnp.mean(x * x, axis=-1, keepdims=True) + EPS) * g


def _silu(x):
    return x * jax.nn.sigmoid(x)


def _softplus(x):
    return jnp.maximum(x, 0.0) + jnp.log1p(jnp.exp(-jnp.abs(x)))


def _params(*sem):
    return pltpu.CompilerParams(dimension_semantics=sem, vmem_limit_bytes=VMEM_LIMIT)


def _inproj_body(x_ref, g_ref, w_ref, wba_ref, wbat_ref, o_ref, oba_ref, obat_ref, h_scr):
    @pl.when(pl.program_id(1) == 0)
    def _():
        hb = _rms(x_ref[...], g_ref[...]).astype(BF16)
        h_scr[...] = hb
        oba_ref[...] = jnp.dot(hb, wba_ref[...], preferred_element_type=F32)
        obat_ref[...] = lax.dot_general(wbat_ref[...], hb, NT_DIMS, preferred_element_type=F32)

    o_ref[...] = jnp.dot(h_scr[...], w_ref[...], preferred_element_type=F32)


def _inproj(x, g, w_main, w_ba, w_bat, tm, tn=1024):
    m = x.shape[0]
    return pl.pallas_call(
        _inproj_body,
        grid=(m // tm, N_MAIN // tn),
        in_specs=[
            pl.BlockSpec((tm, D_MODEL), lambda i, j: (i, 0)),
            pl.BlockSpec((1, D_MODEL), lambda i, j: (0, 0)),
            pl.BlockSpec((D_MODEL, tn), lambda i, j: (0, j)),
            pl.BlockSpec((D_MODEL, BA_PAD), lambda i, j: (0, 0)),
            pl.BlockSpec((BA_PAD, D_MODEL), lambda i, j: (0, 0)),
        ],
        out_specs=[
            pl.BlockSpec((tm, tn), lambda i, j: (i, j)),
            pl.BlockSpec((tm, BA_PAD), lambda i, j: (i, 0)),
            pl.BlockSpec((BA_PAD, tm), lambda i, j: (0, i)),
        ],
        out_shape=[
            jax.ShapeDtypeStruct((m, N_MAIN), F32),
            jax.ShapeDtypeStruct((m, BA_PAD), F32),
            jax.ShapeDtypeStruct((BA_PAD, m), F32),
        ],
        scratch_shapes=[pltpu.VMEM((tm, D_MODEL), BF16)],
        compiler_params=_params("parallel", "arbitrary"),
        name="inproj",
    )(x, g, w_main, w_ba, w_bat)


_RET_LOG_G = np.log1p(-np.exp2(-5.0 - np.arange(RET_HEADS, dtype=np.float64)))


def _ret_tables(c):
    idx = np.arange(c, dtype=np.float64)
    diff = idx[:, None] - idx[None, :]
    dmat = np.where(diff >= 0, np.exp(np.maximum(diff, 0.0)[None] * _RET_LOG_G[:, None, None]), 0.0)
    qdec = np.exp((idx + 1.0)[None, :] * _RET_LOG_G[:, None])
    kdec = np.exp((c - 1.0 - idx)[None, :] * _RET_LOG_G[:, None])
    lane = np.ones((1, 1, RET_DK))
    return (jnp.asarray(dmat, F32), jnp.asarray(qdec[:, :, None] * lane, F32),
            jnp.asarray(kdec[:, :, None] * lane, F32), [float(v) for v in np.exp(c * _RET_LOG_G)])


def _rot(x, cos, sin):
    return x * cos + pltpu.roll(x, RET_DK // 2, 1) * sin


def _group_norm_gate(o, gate, gn):
    mu = jnp.mean(o, axis=-1, keepdims=True)
    d = o - mu
    var = jnp.mean(d * d, axis=-1, keepdims=True)
    return _silu(gate) * (d * lax.rsqrt(var + EPS) * gn)


def _ret_body(cdec, q_ref, k_ref, v_ref, g_ref, cos_ref, sin_ref, dmat_ref, qdec_ref, kdec_ref, gn_ref,
              o_ref, s_out_ref, s_scr):
    t = pl.program_id(1)

    @pl.when(t == 0)
    def _():
        s_scr[...] = jnp.zeros_like(s_scr)

    cos, sin = cos_ref[...], sin_ref[...]
    for h in range(RET_HEADS):
        qk = slice(h * RET_DK, (h + 1) * RET_DK)
        vv = slice(h * RET_DV, (h + 1) * RET_DV)
        q = _rot(q_ref[0, :, qk], cos, sin)
        k = _rot(k_ref[0, :, qk], cos, sin) * (RET_DK ** -0.5)
        v = v_ref[0, :, vv]
        s = s_scr[h]
        scores = _mm_nt(q, k) * dmat_ref[h]
        o = _mm(scores, v) + _mm(q * qdec_ref[h], s)
        s_scr[h] = cdec[h] * s + _mm_tn(k * kdec_ref[h], v)
        o_ref[0, :, vv] = _group_norm_gate(o, g_ref[0, :, vv], gn_ref[:, vv]).astype(BF16)

    @pl.when(t == pl.num_programs(1) - 1)
    def _():
        s_out_ref[0] = s_scr[...]


def _retention_prompt(proj3, cos, sin, gn, tb=256):
    b, t, _ = proj3.shape
    dmat, qdec, kdec, cdec = _ret_tables(tb)
    qw = RET_HEADS * RET_DK
    vw = RET_HEADS * RET_DV
    const3 = lambda i, j: (0, 0, 0)
    return pl.pallas_call(
        functools.partial(_ret_body, cdec),
        grid=(b, t // tb),
        in_specs=[
            pl.BlockSpec((1, tb, qw), lambda i, j: (i, j, 0)),
            pl.BlockSpec((1, tb, qw), lambda i, j: (i, j, 1)),
            pl.BlockSpec((1, tb, vw), lambda i, j: (i, j, 1)),
            pl.BlockSpec((1, tb, vw), lambda i, j: (i, j, 2)),
            pl.BlockSpec((tb, RET_DK), lambda i, j: (j, 0)),
            pl.BlockSpec((tb, RET_DK), lambda i, j: (j, 0)),
            pl.BlockSpec((RET_HEADS, tb, tb), const3),
            pl.BlockSpec((RET_HEADS, tb, RET_DK), const3),
            pl.BlockSpec((RET_HEADS, tb, RET_DK), const3),
            pl.BlockSpec((1, vw), lambda i, j: (0, 0)),
        ],
        out_specs=[
            pl.BlockSpec((1, tb, vw), lambda i, j: (i, j, 0)),
            pl.BlockSpec((1, RET_HEADS, RET_DK, RET_DV), lambda i, j: (i, 0, 0, 0)),
        ],
        out_shape=[
            jax.ShapeDtypeStruct((b, t, vw), BF16),
            jax.ShapeDtypeStruct((b, RET_HEADS, RET_DK, RET_DV), F32),
        ],
        scratch_shapes=[pltpu.VMEM((RET_HEADS, RET_DK, RET_DV), F32)],
        compiler_params=_params("parallel", "arbitrary"),
        name="retention_prompt",
    )(proj3, proj3, proj3, proj3, cos, sin, dmat, qdec, kdec, gn)


def _gdn_tables(tb):
    idx = np.arange(tb)
    same = (idx[:, None] // GDN_CHUNK) == (idx[None, :] // GDN_CHUNK)
    lower = same & (idx[:, None] >= idx[None, :])
    nchunk = tb // GDN_CHUNK
    chunk_sel = np.repeat((idx[:, None] // GDN_CHUNK) == np.arange(nchunk)[None, :], 128, axis=1)
    return (jnp.asarray(lower, F32), jnp.asarray(lower.T, F32), jnp.asarray(same, F32),
            jnp.asarray(chunk_sel, F32))


def _gdn_body(tb, qkv_ref, z_ref, ba_ref, bat_ref, cw_ref, alog_r_ref, dt_r_ref, alog_c_ref, dt_c_ref,
              ng_ref, lbd_ref, ubd_ref, obd_ref, csel_ref,
              o_ref, s_out_ref, conv_out_ref, s_scr, x_scr):
    t = pl.program_id(1)
    nchunk = tb // GDN_CHUNK
    hk = GDN_HEADS * GDN_DK

    @pl.when(t == 0)
    def _():
        s_scr[...] = jnp.zeros_like(s_scr)
        x_scr[0:8, :] = jnp.zeros((8, CONV_CH), F32)

    x_scr[8:8 + tb, :] = qkv_ref[0]
    conv = x_scr[5:5 + tb, :] * cw_ref[0:1, :]
    for i in range(1, CONV_W):
        conv = conv + x_scr[5 + i:5 + i + tb, :] * cw_ref[i:i + 1, :]
    conv = _silu(conv)

    @pl.when(t == pl.num_programs(1) - 1)
    def _():
        conv_out_ref[0] = x_scr[tb + 5:tb + 8, :]

    x_scr[0:8, :] = x_scr[tb:tb + 8, :]

    ba = ba_ref[0]
    beta_c = jax.nn.sigmoid(ba)
    g_c = -jnp.exp(alog_r_ref[...]) * _softplus(ba + dt_r_ref[...])
    g_r = -jnp.exp(alog_c_ref[...]) * _softplus(bat_ref[...] + dt_c_ref[...])
    gc_c = _mm_f32(lbd_ref[...], g_c)
    gt_c = _mm_f32(obd_ref[...], g_c)
    gc_r = _mm_f32(g_r, ubd_ref[...])
    gt_l = _mm_f32(g_r, csel_ref[...])

    ri = lax.broadcasted_iota(jnp.int32, (tb, tb), 0)
    ci = lax.broadcasted_iota(jnp.int32, (tb, tb), 1)
    same = (ri // GDN_CHUNK) == (ci // GDN_CHUNK)
    incl = same & (ri >= ci)
    strict = same & (ri > ci)

    for h in range(GDN_HEADS):
        hs = slice(h * GDN_DK, (h + 1) * GDN_DK)
        q = conv[:, h * GDN_DK:(h + 1) * GDN_DK]
        k = conv[:, hk + h * GDN_DK:hk + (h + 1) * GDN_DK]
        v = conv[:, 2 * hk + h * GDN_DV:2 * hk + (h + 1) * GDN_DV]
        q = q * lax.rsqrt(jnp.sum(q * q, axis=-1, keepdims=True) + EPS) * (GDN_DK ** -0.5)
        k = k * lax.rsqrt(jnp.sum(k * k, axis=-1, keepdims=True) + EPS)
        beta = beta_c[:, h:h + 1]
        gcc = gc_c[:, 8 + h:9 + h]
        gtc = gt_c[:, 8 + h:9 + h]
        gcr = gc_r[8 + h:9 + h, :]

        dg = gcc - gcr
        gamma = jnp.where(incl, jnp.exp(jnp.where(incl, dg, 0.0)), 0.0)
        kb = k.astype(BF16)
        a = jnp.where(strict, _mm_nt(kb, kb) * gamma, 0.0) * beta
        pk = -a
        tp = pk
        pk = _mm(pk, pk)
        for _ in range(4):
            x2 = _mm(jnp.concatenate([tp, pk], axis=0), pk)
            tp = tp + pk + x2[:tb]
            pk = x2[tb:]
        tp = tp + pk + _mm(tp, pk)

        eg = jnp.exp(gcc)
        rhs = jnp.concatenate([v * beta, k * (beta * eg)], axis=1)
        uw = rhs + _mm(tp, rhs)
        u, w = uw[:, :GDN_DV], uw[:, GDN_DV:]
        qk = jnp.where(incl, _mm_nt(q, kb) * gamma, 0.0)
        qg = q * eg
        kh = k * jnp.exp(gtc - gcc)

        s = s_scr[h]
        vn_parts, qs_parts = [], []
        for c in range(nchunk):
            rows = slice(c * GDN_CHUNK, (c + 1) * GDN_CHUNK)
            wq = _mm(jnp.concatenate([w[rows], qg[rows]], axis=0), s)
            vn = u[rows] - wq[:GDN_CHUNK]
            qs_parts.append(wq[GDN_CHUNK:])
            vn_parts.append(vn)
            decay = jnp.exp(gt_l[8 + h:9 + h, c * 128:(c + 1) * 128])
            s = decay * s + _mm_tn(kh[rows], vn)
        s_scr[h] = s
        o = jnp.concatenate(qs_parts, axis=0) + _mm(qk, jnp.concatenate(vn_parts, axis=0))
        o_ref[0, :, hs] = (_rms(o, ng_ref[...]) * _silu(z_ref[0, :, hs])).astype(BF16)

    @pl.when(t == pl.num_programs(1) - 1)
    def _():
        s_out_ref[0] = s_scr[...]


def _gdn_prompt(proj3, ba3, bat, conv_w, alog_r, dt_r, alog_c, dt_c, norm_g, tb=256):
    b, t, _ = proj3.shape
    nt = t // tb
    lbd, ubd, obd, csel = _gdn_tables(tb)
    vw = GDN_HEADS * GDN_DV
    c2 = lambda i, j: (0, 0)
    return pl.pallas_call(
        functools.partial(_gdn_body, tb),
        grid=(b, nt),
        in_specs=[
            pl.BlockSpec((1, tb, CONV_CH), lambda i, j: (i, j, COL_GDN // CONV_CH)),
            pl.BlockSpec((1, tb, vw), lambda i, j: (i, j, (COL_GDN + CONV_CH) // vw)),
            pl.BlockSpec((1, tb, BA_PAD), lambda i, j: (i, j, 0)),
            pl.BlockSpec((BA_PAD, tb), lambda i, j: (0, i * nt + j)),
            pl.BlockSpec((CONV_W, CONV_CH), c2),
            pl.BlockSpec((1, BA_PAD), c2),
            pl.BlockSpec((1, BA_PAD), c2),
            pl.BlockSpec((BA_PAD, 1), c2),
            pl.BlockSpec((BA_PAD, 1), c2),
            pl.BlockSpec((1, GDN_DV), c2),
            pl.BlockSpec((tb, tb), c2),
            pl.BlockSpec((tb, tb), c2),
            pl.BlockSpec((tb, tb), c2),
            pl.BlockSpec((tb, (tb // GDN_CHUNK) * 128), c2),
        ],
        out_specs=[
            pl.BlockSpec((1, tb, vw), lambda i, j: (i, j, 0)),
            pl.BlockSpec((1, GDN_HEADS, GDN_DK, GDN_DV), lambda i, j: (i, 0, 0, 0)),
            pl.BlockSpec((1, CONV_W - 1, CONV_CH), lambda i, j: (i, 0, 0)),
        ],
        out_shape=[
            jax.ShapeDtypeStruct((b, t, vw), BF16),
            jax.ShapeDtypeStruct((b, GDN_HEADS, GDN_DK, GDN_DV), F32),
            jax.ShapeDtypeStruct((b, CONV_W - 1, CONV_CH), F32),
        ],
        scratch_shapes=[pltpu.VMEM((GDN_HEADS, GDN_DK, GDN_DV), F32), pltpu.VMEM((tb + 8, CONV_CH), F32)],
        compiler_params=_params("parallel", "arbitrary"),
        name="gdn_prompt",
    )(proj3, proj3, ba3, bat, conv_w, alog_r, dt_r, alog_c, dt_c, norm_g, lbd, ubd, obd, csel)


def _sample_mix_body(cdec, ret_ref, gdn_ref, z_ref, ba_ref, sr_ref, sg_ref, sc_ref, cos_ref, sin_ref, cw_ref,
                     alog_ref, dt_ref, gn_ref, ng_ref,
                     oa_ref, ob_ref, sr_out_ref, sg_out_ref, sc_out_ref, col_scr):
    hk = GDN_HEADS * GDN_DK
    cos, sin = cos_ref[...], sin_ref[...]
    ret = ret_ref[0]
    x_new = gdn_ref[0]
    z = z_ref[0]
    buf = sc_ref[0]
    conv = x_new * cw_ref[CONV_W - 1:CONV_W, :]
    for i in range(CONV_W - 1):
        conv = conv + buf[i:i + 1, :] * cw_ref[i:i + 1, :]
    conv = _silu(conv)
    sc_out_ref[0, 0:2, :] = buf[1:3, :]
    sc_out_ref[0, 2:3, :] = x_new

    ba = ba_ref[0]
    beta = jax.nn.sigmoid(ba)
    eg = jnp.exp(-jnp.exp(alog_ref[...]) * _softplus(ba + dt_ref[...]))

    col_scr[...] = jnp.zeros_like(col_scr)
    nr = RET_HEADS
    for h in range(RET_HEADS):
        sl = slice(h * RET_DK, (h + 1) * RET_DK)
        col_scr[h:h + 1, :] = _rot(ret[:, sl], cos, sin)
        col_scr[nr + h:nr + h + 1, :] = _rot(ret[:, 512 + h * RET_DK:512 + (h + 1) * RET_DK], cos, sin) * (RET_DK ** -0.5)
    for h in range(GDN_HEADS):
        q = conv[:, h * GDN_DK:(h + 1) * GDN_DK]
        k = conv[:, hk + h * GDN_DK:hk + (h + 1) * GDN_DK]
        q = q * lax.rsqrt(jnp.sum(q * q, axis=-1, keepdims=True) + EPS) * (GDN_DK ** -0.5)
        k = k * lax.rsqrt(jnp.sum(k * k, axis=-1, keepdims=True) + EPS)
        col_scr[2 * nr + h:2 * nr + h + 1, :] = q
        col_scr[2 * nr + GDN_HEADS + h:2 * nr + GDN_HEADS + h + 1, :] = k
    cols = col_scr[...].T

    for h in range(RET_HEADS):
        vv = slice(h * RET_DV, (h + 1) * RET_DV)
        qc = cols[:, h:h + 1]
        kc = cols[:, nr + h:nr + h + 1]
        v = ret[:, 1024 + h * RET_DV:1024 + (h + 1) * RET_DV]
        s_new = cdec[h] * sr_ref[0, h] + kc * v
        sr_out_ref[0, h] = s_new
        o = jnp.sum(qc * s_new, axis=0, keepdims=True)
        oa_ref[0, :, vv] = _group_norm_gate(o, ret[:, 2048 + h * RET_DV:2048 + (h + 1) * RET_DV],
                                            gn_ref[:, vv]).astype(BF16)

    for h in range(GDN_HEADS):
        hs = slice(h * GDN_DV, (h + 1) * GDN_DV)
        qc = cols[:, 2 * nr + h:2 * nr + h + 1]
        kc = cols[:, 2 * nr + GDN_HEADS + h:2 * nr + GDN_HEADS + h + 1]
        v = conv[:, 2 * hk + h * GDN_DV:2 * hk + (h + 1) * GDN_DV]
        egh = eg[:, 8 + h:9 + h]
        s = sg_ref[0, h]
        ks = jnp.sum(kc * s, axis=0, keepdims=True)
        vn = beta[:, h:h + 1] * (v - egh * ks)
        s_new = egh * s + kc * vn
        sg_out_ref[0, h] = s_new
        o = jnp.sum(qc * s_new, axis=0, keepdims=True)
        ob_ref[0, :, hs] = (_rms(o, ng_ref[...]) * _silu(z[:, hs])).astype(BF16)


def _sample_mix(proj, ba, state_ret, state_gdn, state_conv, cos, sin, conv_w, alog_r, dt_r, gn, norm_g):
    ns = proj.shape[0]
    proj3 = proj.reshape(ns, 1, N_MAIN)
    ba3 = ba.reshape(ns, 1, BA_PAD)
    cdec = [float(v) for v in np.exp(_RET_LOG_G)]
    vw = RET_HEADS * RET_DV
    gw = GDN_HEADS * GDN_DV
    c2 = lambda i: (0, 0)
    return pl.pallas_call(
        functools.partial(_sample_mix_body, cdec),
        grid=(ns,),
        in_specs=[
            pl.BlockSpec((1, 1, 3072), lambda i: (i, 0, 0)),
            pl.BlockSpec((1, 1, CONV_CH), lambda i: (i, 0, COL_GDN // CONV_CH)),
            pl.BlockSpec((1, 1, gw), lambda i: (i, 0, (COL_GDN + CONV_CH) // gw)),
            pl.BlockSpec((1, 1, BA_PAD), lambda i: (i, 0, 0)),
            pl.BlockSpec((1, RET_HEADS, RET_DK, RET_DV), lambda i: (i, 0, 0, 0)),
            pl.BlockSpec((1, GDN_HEADS, GDN_DK, GDN_DV), lambda i: (i, 0, 0, 0)),
            pl.BlockSpec((1, CONV_W - 1, CONV_CH), lambda i: (i, 0, 0)),
            pl.BlockSpec((1, RET_DK), c2),
            pl.BlockSpec((1, RET_DK), c2),
            pl.BlockSpec((CONV_W, CONV_CH), c2),
            pl.BlockSpec((1, BA_PAD), c2),
            pl.BlockSpec((1, BA_PAD), c2),
            pl.BlockSpec((1, vw), c2),
            pl.BlockSpec((1, GDN_DV), c2),
        ],
        out_specs=[
            pl.BlockSpec((1, 1, vw), lambda i: (i, 0, 0)),
            pl.BlockSpec((1, 1, gw), lambda i: (i, 0, 0)),
            pl.BlockSpec((1, RET_HEADS, RET_DK, RET_DV), lambda i: (i, 0, 0, 0)),
            pl.BlockSpec((1, GDN_HEADS, GDN_DK, GDN_DV), lambda i: (i, 0, 0, 0)),
            pl.BlockSpec((1, CONV_W - 1, CONV_CH), lambda i: (i, 0, 0)),
        ],
        out_shape=[
            jax.ShapeDtypeStruct((ns, 1, vw), BF16),
            jax.ShapeDtypeStruct((ns, 1, gw), BF16),
            jax.ShapeDtypeStruct(state_ret.shape, F32),
            jax.ShapeDtypeStruct(state_gdn.shape, F32),
            jax.ShapeDtypeStruct(state_conv.shape, F32),
        ],
        scratch_shapes=[pltpu.VMEM((128, 128), F32)],
        compiler_params=_params("parallel"),
        name="sample_mix",
    )(proj3, proj3, proj3, ba3, state_ret, state_gdn, state_conv, cos, sin, conv_w, alog_r, dt_r, gn, norm_g)


def _merge_body(x_ref, ga_ref, gb_ref, oa_ref, ob_ref, wa_ref, wb_ref, wo_ref, ng_ref, wq_ref, x1_ref, q_ref):
    ya = jnp.dot(oa_ref[...], wa_ref[...], preferred_element_type=F32)
    yb = jnp.dot(ob_ref[...], wb_ref[...], preferred_element_type=F32)
    merged = jax.nn.sigmoid(ga_ref[...]) * ya + jax.nn.sigmoid(gb_ref[...]) * yb
    x1 = x_ref[...] + _mm(merged, wo_ref[...])
    x1_ref[...] = x1
    q_ref[...] = _mm(_rms(x1, ng_ref[...]), wq_ref[...]).astype(BF16)


def _merge(x, proj, oa, ob, wa, wb, wo, ng, wq, tm):
    m = x.shape[0]
    row = lambda i: (i, 0)
    c2 = lambda i: (0, 0)
    wspec = pl.BlockSpec((D_MODEL, D_MODEL), c2)
    return pl.pallas_call(
        _merge_body,
        grid=(m // tm,),
        in_specs=[
            pl.BlockSpec((tm, D_MODEL), row),
            pl.BlockSpec((tm, D_MODEL), lambda i: (i, COL_GATE // D_MODEL)),
            pl.BlockSpec((tm, D_MODEL), lambda i: (i, COL_GATE // D_MODEL + 1)),
            pl.BlockSpec((tm, D_MODEL), row),
            pl.BlockSpec((tm, D_MODEL), row),
            wspec, wspec, wspec,
            pl.BlockSpec((1, D_MODEL), c2),
            wspec,
        ],
        out_specs=[pl.BlockSpec((tm, D_MODEL), row), pl.BlockSpec((tm, D_MODEL), row)],
        out_shape=[jax.ShapeDtypeStruct((m, D_MODEL), F32), jax.ShapeDtypeStruct((m, D_MODEL), BF16)],
        compiler_params=_params("parallel"),
        name="merge",
    )(x, proj, proj, oa, ob, wa, wb, wo, ng, wq)


def _memkv_body(m_ref, g_ref, wk_ref, wv_ref, k_ref, v_ref):
    mn = _rms(m_ref[...], g_ref[...]).astype(BF16)
    k_ref[...] = jnp.dot(mn, wk_ref[...], preferred_element_type=F32)
    v_ref[...] = jnp.dot(mn, wv_ref[...], preferred_element_type=F32)


def _memkv(mem, g, wk, wv, tm=512):
    m = mem.shape[0]
    row = lambda i: (i, 0)
    c2 = lambda i: (0, 0)
    return pl.pallas_call(
        _memkv_body,
        grid=(m // tm,),
        in_specs=[pl.BlockSpec((tm, D_MODEL), row), pl.BlockSpec((1, D_MODEL), c2),
                  pl.BlockSpec((D_MODEL, D_MODEL), c2), pl.BlockSpec((D_MODEL, D_MODEL), c2)],
        out_specs=[pl.BlockSpec((tm, D_MODEL), row), pl.BlockSpec((tm, D_MODEL), row)],
        out_shape=[jax.ShapeDtypeStruct((m, D_MODEL), F32)] * 2,
        compiler_params=_params("parallel"),
        name="memkv",
    )(mem, g, wk, wv)


def _xattn_body(q_ref, mk_ref, mv_ref, x1_ref, wo_ref, x2_ref):
    parts = []
    for h in range(X_HEADS):
        hs = slice(h * X_HD, (h + 1) * X_HD)
        s = _mm_nt(q_ref[0, :, hs], mk_ref[0, :, hs]) * (X_HD ** -0.5)
        p = jnp.exp(s - jnp.max(s, axis=-1, keepdims=True))
        o = _mm(p, mv_ref[0, :, hs]) / jnp.sum(p, axis=-1, keepdims=True)
        parts.append(o.astype(BF16))
    x2_ref[0] = x1_ref[0] + jnp.dot(jnp.concatenate(parts, axis=1), wo_ref[...], preferred_element_type=F32)


def _xattn_prompt(q3, mk3, mv3, x13, wo, tq=512):
    b, t, _ = q3.shape
    tok = lambda i, j: (i, j, 0)
    mem = lambda i, j: (i, 0, 0)
    return pl.pallas_call(
        _xattn_body,
        grid=(b, t // tq),
        in_specs=[pl.BlockSpec((1, tq, D_MODEL), tok), pl.BlockSpec((1, N_MEM, D_MODEL), mem),
                  pl.BlockSpec((1, N_MEM, D_MODEL), mem), pl.BlockSpec((1, tq, D_MODEL), tok),
                  pl.BlockSpec((D_MODEL, D_MODEL), lambda i, j: (0, 0))],
        out_specs=pl.BlockSpec((1, tq, D_MODEL), tok),
        out_shape=jax.ShapeDtypeStruct((b, t, D_MODEL), F32),
        compiler_params=_params("parallel", "parallel"),
        name="xattn_prompt",
    )(q3, mk3, mv3, x13, wo)


def _xattn_sample_body(q_ref, mk_ref, mv_ref, o_ref):
    q8 = jnp.broadcast_to(q_ref[0], (8, D_MODEL))
    for h in range(X_HEADS):
        hs = slice(h * X_HD, (h + 1) * X_HD)
        s = _mm_nt(mk_ref[0, :, hs], q8[:, hs])[:, 0:1] * (X_HD ** -0.5)
        p = jnp.exp(s - jnp.max(s, axis=0, keepdims=True))
        o = jnp.sum(p * mv_ref[0, :, hs], axis=0, keepdims=True) / jnp.sum(p, axis=0, keepdims=True)
        o_ref[0, :, hs] = o.astype(BF16)


def _xattn_sample(q, mk3, mv3):
    ns = q.shape[0]
    q3 = q.reshape(ns, 1, D_MODEL)
    row = lambda i: (i, 0, 0)
    return pl.pallas_call(
        _xattn_sample_body,
        grid=(ns,),
        in_specs=[pl.BlockSpec((1, 1, D_MODEL), row), pl.BlockSpec((1, N_MEM, D_MODEL), row),
                  pl.BlockSpec((1, N_MEM, D_MODEL), row)],
        out_specs=pl.BlockSpec((1, 1, D_MODEL), row),
        out_shape=jax.ShapeDtypeStruct((ns, 1, D_MODEL), BF16),
        compiler_params=_params("parallel"),
        name="xattn_sample",
    )(q3, mk3, mv3)


def _resid_mm_body(x_ref, a_ref, w_ref, o_ref):
    o_ref[...] = x_ref[...] + jnp.dot(a_ref[...], w_ref[...], preferred_element_type=F32)


def _resid_mm(x, a, w):
    m = x.shape[0]
    return pl.pallas_call(
        _resid_mm_body,
        out_shape=jax.ShapeDtypeStruct((m, D_MODEL), F32),
        compiler_params=pltpu.CompilerParams(vmem_limit_bytes=VMEM_LIMIT),
        name="resid_mm",
    )(x, a, w)


def _ffn_body(x_ref, ng_ref, wg_ref, wu_ref, wd_ref, nf_ref, y_ref):
    x = x_ref[...]
    h = _rms(x, ng_ref[...]).astype(BF16)
    gate = jnp.dot(h, wg_ref[...], preferred_element_type=F32)
    up = jnp.dot(h, wu_ref[...], preferred_element_type=F32)
    x3 = x + _mm(_silu(gate) * up, wd_ref[...])
    y_ref[...] = _rms(x3, nf_ref[...])


def _ffn(x, ng, wg, wu, wd, nf, tm):
    m = x.shape[0]
    dff = wg.shape[1]
    row = lambda i: (i, 0)
    c2 = lambda i: (0, 0)
    single = pl.Buffered(1)
    return pl.pallas_call(
        _ffn_body,
        grid=(m // tm,),
        in_specs=[pl.BlockSpec((tm, D_MODEL), row), pl.BlockSpec((1, D_MODEL), c2),
                  pl.BlockSpec((D_MODEL, dff), c2, pipeline_mode=single),
                  pl.BlockSpec((D_MODEL, dff), c2, pipeline_mode=single),
                  pl.BlockSpec((dff, D_MODEL), c2, pipeline_mode=single),
                  pl.BlockSpec((1, D_MODEL), c2)],
        out_specs=pl.BlockSpec((tm, D_MODEL), row),
        out_shape=jax.ShapeDtypeStruct((m, D_MODEL), F32),
        compiler_params=_params("parallel"),
        name="ffn",
    )(x, ng, wg, wu, wd, nf)


def _rope_tables(pos):
    half = RET_DK // 2
    inv = ROPE_BASE ** (-jnp.arange(half, dtype=F32) / half)
    ang = pos.astype(F32)[:, None] * inv[None, :]
    cos, sin = jnp.cos(ang), jnp.sin(ang)
    return jnp.concatenate([cos, cos], axis=-1), jnp.concatenate([-sin, sin], axis=-1)


def _pad_lanes(v, offset):
    return jnp.zeros((BA_PAD,), F32).at[offset:offset + v.shape[0]].set(v)


def kernel(x_prompt, x_sample, state_ret, state_gdn, state_conv, cache_mem_k, cache_mem_v, mem_prompt,
           norm_mix_g, w_in, ret_gn_g, w_branch_a, gdn_conv_w, gdn_a_log, gdn_dt_bias, gdn_norm_g,
           w_branch_b, w_out, norm_x_g, mem_norm_g, w_xq, w_xk, w_xv, w_xo, norm_ffn_g, w_gate, w_up,
           w_down, norm_final_g):
    depth = w_in.shape[0]
    assert depth == 1, "single-layer kernel"
    b, t, _ = x_prompt.shape
    ns = x_sample.shape[0]
    l = 0

    w = w_in[l]
    ba0 = 7168
    w_main = jnp.concatenate([w[:, :ba0], w[:, ba0 + 2 * GDN_HEADS:]], axis=1).astype(BF16)
    w_ba = jnp.pad(w[:, ba0:ba0 + 2 * GDN_HEADS], ((0, 0), (0, BA_PAD - 2 * GDN_HEADS))).astype(BF16)
    w_bat = w_ba.T
    row = lambda v: v.reshape(1, -1)
    wa, wb, wo = w_branch_a[l].astype(BF16), w_branch_b[l].astype(BF16), w_out[l].astype(BF16)
    wq, wk, wv, wxo = w_xq[l].astype(BF16), w_xk[l].astype(BF16), w_xv[l].astype(BF16), w_xo[l].astype(BF16)
    wg, wu, wd = w_gate[l].astype(BF16), w_up[l].astype(BF16), w_down[l].astype(BF16)
    alog_r = _pad_lanes(gdn_a_log[l], GDN_HEADS).reshape(1, BA_PAD)
    dt_r = _pad_lanes(gdn_dt_bias[l], GDN_HEADS).reshape(1, BA_PAD)
    alog_c, dt_c = alog_r.reshape(BA_PAD, 1), dt_r.reshape(BA_PAD, 1)
    cos_p, sin_p = _rope_tables(jnp.arange(t))
    cos_s, sin_s = _rope_tables(PAST_LEN + jnp.arange(1))

    xp = x_prompt.reshape(b * t, D_MODEL)
    proj_p, ba_p, bat_p = _inproj(xp, row(norm_mix_g[l]), w_main, w_ba, w_bat, tm=1024)
    proj_p3 = proj_p.reshape(b, t, N_MAIN)
    oa_p, sr_p = _retention_prompt(proj_p3, cos_p, sin_p, row(ret_gn_g[l]))
    ob_p, sg_p, sc_p = _gdn_prompt(proj_p3, ba_p.reshape(b, t, BA_PAD), bat_p, gdn_conv_w[l], alog_r, dt_r,
                                   alog_c, dt_c, row(gdn_norm_g[l]))
    x1_p, q_p = _merge(xp, proj_p, oa_p.reshape(b * t, -1), ob_p.reshape(b * t, -1), wa, wb, wo,
                       row(norm_x_g[l]), wq, tm=512)
    mk_p, mv_p = _memkv(mem_prompt.reshape(b * N_MEM, D_MODEL), row(mem_norm_g[l]), wk, wv)
    x2_p = _xattn_prompt(q_p.reshape(b, t, D_MODEL), mk_p.reshape(b, N_MEM, D_MODEL),
                         mv_p.reshape(b, N_MEM, D_MODEL), x1_p.reshape(b, t, D_MODEL), wxo)
    y_p = _ffn(x2_p.reshape(b * t, D_MODEL), row(norm_ffn_g[l]), wg, wu, wd, row(norm_final_g), tm=256)

    xs = x_sample.reshape(ns, D_MODEL)
    proj_s, ba_s, _ = _inproj(xs, row(norm_mix_g[l]), w_main, w_ba, w_bat, tm=ns)
    oa_s, ob_s, sr_s, sg_s, sc_s = _sample_mix(proj_s, ba_s, state_ret[l], state_gdn[l], state_conv[l],
                                               cos_s, sin_s, gdn_conv_w[l], alog_r, dt_r, row(ret_gn_g[l]),
                                               row(gdn_norm_g[l]))
    x1_s, q_s = _merge(xs, proj_s, oa_s.reshape(ns, -1), ob_s.reshape(ns, -1), wa, wb, wo,
                       row(norm_x_g[l]), wq, tm=ns)
    o_s = _xattn_sample(q_s, cache_mem_k[l].reshape(ns, N_MEM, D_MODEL), cache_mem_v[l].reshape(ns, N_MEM, D_MODEL))
    x2_s = _resid_mm(x1_s, o_s.reshape(ns, D_MODEL), wxo)
    y_s = _ffn(x2_s, row(norm_ffn_g[l]), wg, wu, wd, row(norm_final_g), tm=ns)

    return (y_p.reshape(b, t, D_MODEL), y_s.reshape(ns, 1, D_MODEL),
            sr_p[None], sg_p[None], sc_p[None],
            mk_p.reshape(1, b, N_MEM, X_HEADS, X_HD), mv_p.reshape(1, b, N_MEM, X_HEADS, X_HD),
            sr_s[None], sg_s[None], sc_s[None])
```

```python
import functools

import numpy as np
import jax
import jax.numpy as jnp
from jax import lax
from jax.experimental import pallas as pl
from jax.experimental.pallas import tpu as pltpu

F32 = jnp.float32
BF16 = jnp.bfloat16

D_MODEL = 1024
RET_HEADS, RET_DK, RET_DV = 4, 128, 256
GDN_HEADS, GDN_DK, GDN_DV = 8, 128, 128
CONV_W = 4
CONV_CH = 3 * GDN_HEADS * GDN_DK
N_MEM, X_HEADS, X_HD = 256, 4, 256
PAST_LEN = 16384
ROPE_BASE = 10000.0
EPS = 1e-6
GDN_CHUNK = 64

COL_RET = 0
COL_GDN = 3072
COL_GATE = 7168
N_MAIN = 9216
BA_PAD = 128

VMEM_LIMIT = 56 * 1024 * 1024

NT_DIMS = (((1,), (1,)), ((), ()))
TN_DIMS = (((0,), (0,)), ((), ()))


def _mm(a, b):
    return jnp.dot(a.astype(BF16), b.astype(BF16), preferred_element_type=F32)


def _mm_nt(a, b):
    return lax.dot_general(a.astype(BF16), b.astype(BF16), NT_DIMS, preferred_element_type=F32)


def _mm_tn(a, b):
    return lax.dot_general(a.astype(BF16), b.astype(BF16), TN_DIMS, preferred_element_type=F32)


def _mm_f32(a, b):
    return jnp.dot(a, b, preferred_element_type=F32, precision=lax.Precision.HIGHEST)


def _rms(x, g):
    return x * lax.rsqrt(jnp.mean(x * x, axis=-1, keepdims=True) + EPS) * g


def _silu(x):
    return x * jax.nn.sigmoid(x)


def _softplus(x):
    return jnp.maximum(x, 0.0) + jnp.log1p(jnp.exp(-jnp.abs(x)))


def _params(*sem):
    return pltpu.CompilerParams(dimension_semantics=sem, vmem_limit_bytes=VMEM_LIMIT)


def _inproj_body(x_ref, g_ref, w_ref, wba_ref, wbat_ref, o_ref, oba_ref, obat_ref, h_scr):
    @pl.when(pl.program_id(1) == 0)
    def _():
        hb = _rms(x_ref[...], g_ref[...]).astype(BF16)
        h_scr[...] = hb
        oba_ref[...] = jnp.dot(hb, wba_ref[...], preferred_element_type=F32)
        obat_ref[...] = lax.dot_general(wbat_ref[...], hb, NT_DIMS, preferred_element_type=F32)

    o_ref[...] = jnp.dot(h_scr[...], w_ref[...], preferred_element_type=F32)


def _inproj(x, g, w_main, w_ba, w_bat, tm, tn=1024):
    m = x.shape[0]
    return pl.pallas_call(
        _inproj_body,
        grid=(m // tm, N_MAIN // tn),
        in_specs=[
            pl.BlockSpec((tm, D_MODEL), lambda i, j: (i, 0)),
            pl.BlockSpec((1, D_MODEL), lambda i, j: (0, 0)),
            pl.BlockSpec((D_MODEL, tn), lambda i, j: (0, j)),
            pl.BlockSpec((D_MODEL, BA_PAD), lambda i, j: (0, 0)),
            pl.BlockSpec((BA_PAD, D_MODEL), lambda i, j: (0, 0)),
        ],
        out_specs=[
            pl.BlockSpec((tm, tn), lambda i, j: (i, j)),
            pl.BlockSpec((tm, BA_PAD), lambda i, j: (i, 0)),
            pl.BlockSpec((BA_PAD, tm), lambda i, j: (0, i)),
        ],
        out_shape=[
            jax.ShapeDtypeStruct((m, N_MAIN), F32),
            jax.ShapeDtypeStruct((m, BA_PAD), F32),
            jax.ShapeDtypeStruct((BA_PAD, m), F32),
        ],
        scratch_shapes=[pltpu.VMEM((tm, D_MODEL), BF16)],
        compiler_params=_params("parallel", "arbitrary"),
        name="inproj",
    )(x, g, w_main, w_ba, w_bat)


_RET_LOG_G = np.log1p(-np.exp2(-5.0 - np.arange(RET_HEADS, dtype=np.float64)))


def _ret_tables(c):
    idx = np.arange(c, dtype=np.float64)
    diff = idx[:, None] - idx[None, :]
    dmat = np.where(diff >= 0, np.exp(np.maximum(diff, 0.0)[None] * _RET_LOG_G[:, None, None]), 0.0)
    qdec = np.exp((idx + 1.0)[None, :] * _RET_LOG_G[:, None])
    kdec = np.exp((c - 1.0 - idx)[None, :] * _RET_LOG_G[:, None])
    lane = np.ones((1, 1, RET_DK))
    return (jnp.asarray(dmat, F32), jnp.asarray(qdec[:, :, None] * lane, F32),
            jnp.asarray(kdec[:, :, None] * lane, F32), [float(v) for v in np.exp(c * _RET_LOG_G)])


def _rot(x, cos, sin):
    return x * cos + pltpu.roll(x, RET_DK // 2, 1) * sin


def _group_norm_gate(o, gate, gn):
    mu = jnp.mean(o, axis=-1, keepdims=True)
    d = o - mu
    var = jnp.mean(d * d, axis=-1, keepdims=True)
    return _silu(gate) * (d * lax.rsqrt(var + EPS) * gn)


def _ret_body(cdec, q_ref, k_ref, v_ref, g_ref, cos_ref, sin_ref, dmat_ref, qdec_ref, kdec_ref, gn_ref,
              o_ref, s_out_ref, s_scr):
    t = pl.program_id(1)

    @pl.when(t == 0)
    def _():
        s_scr[...] = jnp.zeros_like(s_scr)

    cos, sin = cos_ref[...], sin_ref[...]
    for h in range(RET_HEADS):
        qk = slice(h * RET_DK, (h + 1) * RET_DK)
        vv = slice(h * RET_DV, (h + 1) * RET_DV)
        q = _rot(q_ref[0, :, qk], cos, sin)
        k = _rot(k_ref[0, :, qk], cos, sin) * (RET_DK ** -0.5)
        v = v_ref[0, :, vv]
        s = s_scr[h]
        scores = _mm_nt(q, k) * dmat_ref[h]
        o = _mm(scores, v) + _mm(q * qdec_ref[h], s)
        s_scr[h] = cdec[h] * s + _mm_tn(k * kdec_ref[h], v)
        o_ref[0, :, vv] = _group_norm_gate(o, g_ref[0, :, vv], gn_ref[:, vv]).astype(BF16)

    @pl.when(t == pl.num_programs(1) - 1)
    def _():
        s_out_ref[0] = s_scr[...]


def _retention_prompt(proj3, cos, sin, gn, tb=256):
    b, t, _ = proj3.shape
    dmat, qdec, kdec, cdec = _ret_tables(tb)
    qw = RET_HEADS * RET_DK
    vw = RET_HEADS * RET_DV
    const3 = lambda i, j: (0, 0, 0)
    return pl.pallas_call(
        functools.partial(_ret_body, cdec),
        grid=(b, t // tb),
        in_specs=[
            pl.BlockSpec((1, tb, qw), lambda i, j: (i, j, 0)),
            pl.BlockSpec((1, tb, qw), lambda i, j: (i, j, 1)),
            pl.BlockSpec((1, tb, vw), lambda i, j: (i, j, 1)),
            pl.BlockSpec((1, tb, vw), lambda i, j: (i, j, 2)),
            pl.BlockSpec((tb, RET_DK), lambda i, j: (j, 0)),
            pl.BlockSpec((tb, RET_DK), lambda i, j: (j, 0)),
            pl.BlockSpec((RET_HEADS, tb, tb), const3),
            pl.BlockSpec((RET_HEADS, tb, RET_DK), const3),
            pl.BlockSpec((RET_HEADS, tb, RET_DK), const3),
            pl.BlockSpec((1, vw), lambda i, j: (0, 0)),
        ],
        out_specs=[
            pl.BlockSpec((1, tb, vw), lambda i, j: (i, j, 0)),
            pl.BlockSpec((1, RET_HEADS, RET_DK, RET_DV), lambda i, j: (i, 0, 0, 0)),
        ],
        out_shape=[
            jax.ShapeDtypeStruct((b, t, vw), BF16),
            jax.ShapeDtypeStruct((b, RET_HEADS, RET_DK, RET_DV), F32),
        ],
        scratch_shapes=[pltpu.VMEM((RET_HEADS, RET_DK, RET_DV), F32)],
        compiler_params=_params("parallel", "arbitrary"),
        name="retention_prompt",
    )(proj3, proj3, proj3, proj3, cos, sin, dmat, qdec, kdec, gn)


def _gdn_tables(tb):
    idx = np.arange(tb)
    same = (idx[:, None] // GDN_CHUNK) == (idx[None, :] // GDN_CHUNK)
    lower = same & (idx[:, None] >= idx[None, :])
    nchunk = tb // GDN_CHUNK
    chunk_sel = np.repeat((idx[:, None] // GDN_CHUNK) == np.arange(nchunk)[None, :], 128, axis=1)
    bias = np.where(lower, 0.0, -1e30)
    offdiag = 1.0 - np.eye(tb)
    return (jnp.asarray(lower, F32), jnp.asarray(lower.T, F32), jnp.asarray(same, F32),
            jnp.asarray(chunk_sel, F32), jnp.asarray(bias, F32), jnp.asarray(offdiag, F32))


def _gdn_body(tb, qkv_ref, z_ref, ba_ref, bat_ref, cw_ref, alog_r_ref, dt_r_ref, alog_c_ref, dt_c_ref,
              ng_ref, lbd_ref, ubd_ref, obd_ref, csel_ref, bias_ref, offd_ref,
              o_ref, s_out_ref, conv_out_ref, s_scr, x_scr):
    t = pl.program_id(1)
    nchunk = tb // GDN_CHUNK
    hk = GDN_HEADS * GDN_DK

    @pl.when(t == 0)
    def _():
        s_scr[...] = jnp.zeros_like(s_scr)
        x_scr[0:8, :] = jnp.zeros((8, CONV_CH), F32)

    x_scr[8:8 + tb, :] = qkv_ref[0]
    conv = x_scr[5:5 + tb, :] * cw_ref[0:1, :]
    for i in range(1, CONV_W):
        conv = conv + x_scr[5 + i:5 + i + tb, :] * cw_ref[i:i + 1, :]
    conv = _silu(conv)

    @pl.when(t == pl.num_programs(1) - 1)
    def _():
        conv_out_ref[0] = x_scr[tb + 5:tb + 8, :]

    x_scr[0:8, :] = x_scr[tb:tb + 8, :]

    ba = ba_ref[0]
    beta_c = jax.nn.sigmoid(ba)
    g_c = -jnp.exp(alog_r_ref[...]) * _softplus(ba + dt_r_ref[...])
    g_r = -jnp.exp(alog_c_ref[...]) * _softplus(bat_ref[...] + dt_c_ref[...])
    gc_c = _mm_f32(lbd_ref[...], g_c)
    gt_c = _mm_f32(obd_ref[...], g_c)
    gc_r = _mm_f32(g_r, ubd_ref[...])
    gt_l = _mm_f32(g_r, csel_ref[...])

    heads = range(GDN_HEADS)
    bias = bias_ref[...]
    offdiag = offd_ref[...]
    qs, ks, gammas, ps, rhss, qgs, khs = [], [], [], [], [], [], []
    for h in heads:
        q = conv[:, h * GDN_DK:(h + 1) * GDN_DK]
        k = conv[:, hk + h * GDN_DK:hk + (h + 1) * GDN_DK]
        v = conv[:, 2 * hk + h * GDN_DV:2 * hk + (h + 1) * GDN_DV]
        q = q * lax.rsqrt(jnp.sum(q * q, axis=-1, keepdims=True) + EPS) * (GDN_DK ** -0.5)
        k = k * lax.rsqrt(jnp.sum(k * k, axis=-1, keepdims=True) + EPS)
        beta = beta_c[:, h:h + 1]
        gcc = gc_c[:, 8 + h:9 + h]
        gtc = gt_c[:, 8 + h:9 + h]
        gcr = gc_r[8 + h:9 + h, :]
        gamma = jnp.exp((gcc - gcr) + bias)
        kbeta = k * beta
        kb = k.astype(BF16)
        ps.append(lax.dot_general((-kbeta).astype(BF16), kb, NT_DIMS, preferred_element_type=F32)
                  * (gamma * offdiag))
        eg = jnp.exp(gcc)
        rhss.append(jnp.concatenate([v * beta, kbeta * eg], axis=1))
        qs.append(q.astype(BF16))
        ks.append(kb)
        gammas.append(gamma)
        qgs.append(q * eg)
        khs.append(k * jnp.exp(gtc - gcc))

    tp = list(ps)
    pkb = [p.astype(BF16) for p in ps]
    pk = [jnp.dot(pkb[h], pkb[h], preferred_element_type=F32) for h in heads]
    for _ in range(4):
        pkb = [pk[h].astype(BF16) for h in heads]
        xt = [jnp.dot(tp[h].astype(BF16), pkb[h], preferred_element_type=F32) for h in heads]
        xp = [jnp.dot(pkb[h], pkb[h], preferred_element_type=F32) for h in heads]
        tp = [tp[h] + pk[h] + xt[h] for h in heads]
        pk = xp
    tp = [tp[h] + pk[h] + _mm(tp[h], pk[h]) for h in heads]

    us, ws, qks = [], [], []
    for h in heads:
        uw = rhss[h] + _mm(tp[h], rhss[h])
        us.append(uw[:, :GDN_DV])
        ws.append(uw[:, GDN_DV:])
        qks.append(lax.dot_general(qs[h], ks[h], NT_DIMS, preferred_element_type=F32) * gammas[h])

    s = [s_scr[h] for h in heads]
    vn_parts = [[] for _ in heads]
    qs_parts = [[] for _ in heads]
    for c in range(nchunk):
        rows = slice(c * GDN_CHUNK, (c + 1) * GDN_CHUNK)
        for h in heads:
            wq = _mm(jnp.concatenate([ws[h][rows], qgs[h][rows]], axis=0), s[h])
            vn = us[h][rows] - wq[:GDN_CHUNK]
            qs_parts[h].append(wq[GDN_CHUNK:])
            vn_parts[h].append(vn)
            decay = jnp.exp(gt_l[8 + h:9 + h, c * 128:(c + 1) * 128])
            s[h] = decay * s[h] + _mm_tn(khs[h][rows], vn)
    for h in heads:
        hs = slice(h * GDN_DV, (h + 1) * GDN_DV)
        s_scr[h] = s[h]
        o = jnp.concatenate(qs_parts[h], axis=0) + _mm(qks[h], jnp.concatenate(vn_parts[h], axis=0))
        o_ref[0, :, hs] = (_rms(o, ng_ref[...]) * _silu(z_ref[0, :, hs])).astype(BF16)

    @pl.when(t == pl.num_programs(1) - 1)
    def _():
        s_out_ref[0] = s_scr[...]


def _gdn_prompt(proj3, ba3, bat, conv_w, alog_r, dt_r, alog_c, dt_c, norm_g, tb=256):
    b, t, _ = proj3.shape
    nt = t // tb
    lbd, ubd, obd, csel, bias, offdiag = _gdn_tables(tb)
    vw = GDN_HEADS * GDN_DV
    c2 = lambda i, j: (0, 0)
    return pl.pallas_call(
        functools.partial(_gdn_body, tb),
        grid=(b, nt),
        in_specs=[
            pl.BlockSpec((1, tb, CONV_CH), lambda i, j: (i, j, COL_GDN // CONV_CH)),
            pl.BlockSpec((1, tb, vw), lambda i, j: (i, j, (COL_GDN + CONV_CH) // vw)),
            pl.BlockSpec((1, tb, BA_PAD), lambda i, j: (i, j, 0)),
            pl.BlockSpec((BA_PAD, tb), lambda i, j: (0, i * nt + j)),
            pl.BlockSpec((CONV_W, CONV_CH), c2),
            pl.BlockSpec((1, BA_PAD), c2),
            pl.BlockSpec((1, BA_PAD), c2),
            pl.BlockSpec((BA_PAD, 1), c2),
            pl.BlockSpec((BA_PAD, 1), c2),
            pl.BlockSpec((1, GDN_DV), c2),
            pl.BlockSpec((tb, tb), c2),
            pl.BlockSpec((tb, tb), c2),
            pl.BlockSpec((tb, tb), c2),
            pl.BlockSpec((tb, (tb // GDN_CHUNK) * 128), c2),
            pl.BlockSpec((tb, tb), c2),
            pl.BlockSpec((tb, tb), c2),
        ],
        out_specs=[
            pl.BlockSpec((1, tb, vw), lambda i, j: (i, j, 0)),
            pl.BlockSpec((1, GDN_HEADS, GDN_DK, GDN_DV), lambda i, j: (i, 0, 0, 0)),
            pl.BlockSpec((1, CONV_W - 1, CONV_CH), lambda i, j: (i, 0, 0)),
        ],
        out_shape=[
            jax.ShapeDtypeStruct((b, t, vw), BF16),
            jax.ShapeDtypeStruct((b, GDN_HEADS, GDN_DK, GDN_DV), F32),
            jax.ShapeDtypeStruct((b, CONV_W - 1, CONV_CH), F32),
        ],
        scratch_shapes=[pltpu.VMEM((GDN_HEADS, GDN_DK, GDN_DV), F32), pltpu.VMEM((tb + 8, CONV_CH), F32)],
        compiler_params=_params("parallel", "arbitrary"),
        name="gdn_prompt",
    )(proj3, proj3, ba3, bat, conv_w, alog_r, dt_r, alog_c, dt_c, norm_g, lbd, ubd, obd, csel, bias, offdiag)


def _sample_mix_body(cdec, ret_ref, gdn_ref, z_ref, ba_ref, sr_ref, sg_ref, sc_ref, cos_ref, sin_ref, cw_ref,
                     alog_ref, dt_ref, gn_ref, ng_ref,
                     oa_ref, ob_ref, sr_out_ref, sg_out_ref, sc_out_ref, col_scr):
    hk = GDN_HEADS * GDN_DK
    cos, sin = cos_ref[...], sin_ref[...]
    ret = ret_ref[0]
    x_new = gdn_ref[0]
    z = z_ref[0]
    buf = sc_ref[0]
    conv = x_new * cw_ref[CONV_W - 1:CONV_W, :]
    for i in range(CONV_W - 1):
        conv = conv + buf[i:i + 1, :] * cw_ref[i:i + 1, :]
    conv = _silu(conv)
    sc_out_ref[0, 0:2, :] = buf[1:3, :]
    sc_out_ref[0, 2:3, :] = x_new

    ba = ba_ref[0]
    beta = jax.nn.sigmoid(ba)
    eg = jnp.exp(-jnp.exp(alog_ref[...]) * _softplus(ba + dt_ref[...]))

    col_scr[...] = jnp.zeros_like(col_scr)
    nr = RET_HEADS
    for h in range(RET_HEADS):
        sl = slice(h * RET_DK, (h + 1) * RET_DK)
        col_scr[h:h + 1, :] = _rot(ret[:, sl], cos, sin)
        col_scr[nr + h:nr + h + 1, :] = _rot(ret[:, 512 + h * RET_DK:512 + (h + 1) * RET_DK], cos, sin) * (RET_DK ** -0.5)
    for h in range(GDN_HEADS):
        q = conv[:, h * GDN_DK:(h + 1) * GDN_DK]
        k = conv[:, hk + h * GDN_DK:hk + (h + 1) * GDN_DK]
        q = q * lax.rsqrt(jnp.sum(q * q, axis=-1, keepdims=True) + EPS) * (GDN_DK ** -0.5)
        k = k * lax.rsqrt(jnp.sum(k * k, axis=-1, keepdims=True) + EPS)
        col_scr[2 * nr + h:2 * nr + h + 1, :] = q
        col_scr[2 * nr + GDN_HEADS + h:2 * nr + GDN_HEADS + h + 1, :] = k
    cols = col_scr[...].T

    for h in range(RET_HEADS):
        vv = slice(h * RET_DV, (h + 1) * RET_DV)
        qc = cols[:, h:h + 1]
        kc = cols[:, nr + h:nr + h + 1]
        v = ret[:, 1024 + h * RET_DV:1024 + (h + 1) * RET_DV]
        s_new = cdec[h] * sr_ref[0, h] + kc * v
        sr_out_ref[0, h] = s_new
        o = jnp.sum(qc * s_new, axis=0, keepdims=True)
        oa_ref[0, :, vv] = _group_norm_gate(o, ret[:, 2048 + h * RET_DV:2048 + (h + 1) * RET_DV],
                                            gn_ref[:, vv]).astype(BF16)

    for h in range(GDN_HEADS):
        hs = slice(h * GDN_DV, (h + 1) * GDN_DV)
        qc = cols[:, 2 * nr + h:2 * nr + h + 1]
        kc = cols[:, 2 * nr + GDN_HEADS + h:2 * nr + GDN_HEADS + h + 1]
        v = conv[:, 2 * hk + h * GDN_DV:2 * hk + (h + 1) * GDN_DV]
        egh = eg[:, 8 + h:9 + h]
        s = sg_ref[0, h]
        ks = jnp.sum(kc * s, axis=0, keepdims=True)
        vn = beta[:, h:h + 1] * (v - egh * ks)
        s_new = egh * s + kc * vn
        sg_out_ref[0, h] = s_new
        o = jnp.sum(qc * s_new, axis=0, keepdims=True)
        ob_ref[0, :, hs] = (_rms(o, ng_ref[...]) * _silu(z[:, hs])).astype(BF16)


def _sample_mix(proj, ba, state_ret, state_gdn, state_conv, cos, sin, conv_w, alog_r, dt_r, gn, norm_g):
    ns = proj.shape[0]
    proj3 = proj.reshape(ns, 1, N_MAIN)
    ba3 = ba.reshape(ns, 1, BA_PAD)
    cdec = [float(v) for v in np.exp(_RET_LOG_G)]
    vw = RET_HEADS * RET_DV
    gw = GDN_HEADS * GDN_DV
    c2 = lambda i: (0, 0)
    return pl.pallas_call(
        functools.partial(_sample_mix_body, cdec),
        grid=(ns,),
        in_specs=[
            pl.BlockSpec((1, 1, 3072), lambda i: (i, 0, 0)),
            pl.BlockSpec((1, 1, CONV_CH), lambda i: (i, 0, COL_GDN // CONV_CH)),
            pl.BlockSpec((1, 1, gw), lambda i: (i, 0, (COL_GDN + CONV_CH) // gw)),
            pl.BlockSpec((1, 1, BA_PAD), lambda i: (i, 0, 0)),
            pl.BlockSpec((1, RET_HEADS, RET_DK, RET_DV), lambda i: (i, 0, 0, 0)),
            pl.BlockSpec((1, GDN_HEADS, GDN_DK, GDN_DV), lambda i: (i, 0, 0, 0)),
            pl.BlockSpec((1, CONV_W - 1, CONV_CH), lambda i: (i, 0, 0)),
            pl.BlockSpec((1, RET_DK), c2),
            pl.BlockSpec((1, RET_DK), c2),
            pl.BlockSpec((CONV_W, CONV_CH), c2),
            pl.BlockSpec((1, BA_PAD), c2),
            pl.BlockSpec((1, BA_PAD), c2),
            pl.BlockSpec((1, vw), c2),
            pl.BlockSpec((1, GDN_DV), c2),
        ],
        out_specs=[
            pl.BlockSpec((1, 1, vw), lambda i: (i, 0, 0)),
            pl.BlockSpec((1, 1, gw), lambda i: (i, 0, 0)),
            pl.BlockSpec((1, RET_HEADS, RET_DK, RET_DV), lambda i: (i, 0, 0, 0)),
            pl.BlockSpec((1, GDN_HEADS, GDN_DK, GDN_DV), lambda i: (i, 0, 0, 0)),
            pl.BlockSpec((1, CONV_W - 1, CONV_CH), lambda i: (i, 0, 0)),
        ],
        out_shape=[
            jax.ShapeDtypeStruct((ns, 1, vw), BF16),
            jax.ShapeDtypeStruct((ns, 1, gw), BF16),
            jax.ShapeDtypeStruct(state_ret.shape, F32),
            jax.ShapeDtypeStruct(state_gdn.shape, F32),
            jax.ShapeDtypeStruct(state_conv.shape, F32),
        ],
        scratch_shapes=[pltpu.VMEM((128, 128), F32)],
        compiler_params=_params("parallel"),
        name="sample_mix",
    )(proj3, proj3, proj3, ba3, state_ret, state_gdn, state_conv, cos, sin, conv_w, alog_r, dt_r, gn, norm_g)


def _merge_body(x_ref, ga_ref, gb_ref, oa_ref, ob_ref, wa_ref, wb_ref, wo_ref, ng_ref, wq_ref, x1_ref, q_ref):
    ya = jnp.dot(oa_ref[...], wa_ref[...], preferred_element_type=F32)
    yb = jnp.dot(ob_ref[...], wb_ref[...], preferred_element_type=F32)
    merged = jax.nn.sigmoid(ga_ref[...]) * ya + jax.nn.sigmoid(gb_ref[...]) * yb
    x1 = x_ref[...] + _mm(merged, wo_ref[...])
    x1_ref[...] = x1
    q_ref[...] = _mm(_rms(x1, ng_ref[...]), wq_ref[...]).astype(BF16)


def _merge(x, proj, oa, ob, wa, wb, wo, ng, wq, tm):
    m = x.shape[0]
    row = lambda i: (i, 0)
    c2 = lambda i: (0, 0)
    wspec = pl.BlockSpec((D_MODEL, D_MODEL), c2)
    return pl.pallas_call(
        _merge_body,
        grid=(m // tm,),
        in_specs=[
            pl.BlockSpec((tm, D_MODEL), row),
            pl.BlockSpec((tm, D_MODEL), lambda i: (i, COL_GATE // D_MODEL)),
            pl.BlockSpec((tm, D_MODEL), lambda i: (i, COL_GATE // D_MODEL + 1)),
            pl.BlockSpec((tm, D_MODEL), row),
            pl.BlockSpec((tm, D_MODEL), row),
            wspec, wspec, wspec,
            pl.BlockSpec((1, D_MODEL), c2),
            wspec,
        ],
        out_specs=[pl.BlockSpec((tm, D_MODEL), row), pl.BlockSpec((tm, D_MODEL), row)],
        out_shape=[jax.ShapeDtypeStruct((m, D_MODEL), F32), jax.ShapeDtypeStruct((m, D_MODEL), BF16)],
        compiler_params=_params("parallel"),
        name="merge",
    )(x, proj, proj, oa, ob, wa, wb, wo, ng, wq)


def _memkv_body(m_ref, g_ref, wk_ref, wv_ref, k_ref, v_ref):
    mn = _rms(m_ref[...], g_ref[...]).astype(BF16)
    k_ref[...] = jnp.dot(mn, wk_ref[...], preferred_element_type=F32)
    v_ref[...] = jnp.dot(mn, wv_ref[...], preferred_element_type=F32)


def _memkv(mem, g, wk, wv, tm=512):
    m = mem.shape[0]
    row = lambda i: (i, 0)
    c2 = lambda i: (0, 0)
    return pl.pallas_call(
        _memkv_body,
        grid=(m // tm,),
        in_specs=[pl.BlockSpec((tm, D_MODEL), row), pl.BlockSpec((1, D_MODEL), c2),
                  pl.BlockSpec((D_MODEL, D_MODEL), c2), pl.BlockSpec((D_MODEL, D_MODEL), c2)],
        out_specs=[pl.BlockSpec((tm, D_MODEL), row), pl.BlockSpec((tm, D_MODEL), row)],
        out_shape=[jax.ShapeDtypeStruct((m, D_MODEL), F32)] * 2,
        compiler_params=_params("parallel"),
        name="memkv",
    )(mem, g, wk, wv)


def _xattn_body(q_ref, mk_ref, mv_ref, x1_ref, wo_ref, x2_ref):
    parts = []
    for h in range(X_HEADS):
        hs = slice(h * X_HD, (h + 1) * X_HD)
        s = _mm_nt(q_ref[0, :, hs], mk_ref[0, :, hs]) * (X_HD ** -0.5)
        p = jnp.exp(s - jnp.max(s, axis=-1, keepdims=True))
        o = _mm(p, mv_ref[0, :, hs]) / jnp.sum(p, axis=-1, keepdims=True)
        parts.append(o.astype(BF16))
    x2_ref[0] = x1_ref[0] + jnp.dot(jnp.concatenate(parts, axis=1), wo_ref[...], preferred_element_type=F32)


def _xattn_prompt(q3, mk3, mv3, x13, wo, tq=512):
    b, t, _ = q3.shape
    tok = lambda i, j: (i, j, 0)
    mem = lambda i, j: (i, 0, 0)
    return pl.pallas_call(
        _xattn_body,
        grid=(b, t // tq),
        in_specs=[pl.BlockSpec((1, tq, D_MODEL), tok), pl.BlockSpec((1, N_MEM, D_MODEL), mem),
                  pl.BlockSpec((1, N_MEM, D_MODEL), mem), pl.BlockSpec((1, tq, D_MODEL), tok),
                  pl.BlockSpec((D_MODEL, D_MODEL), lambda i, j: (0, 0))],
        out_specs=pl.BlockSpec((1, tq, D_MODEL), tok),
        out_shape=jax.ShapeDtypeStruct((b, t, D_MODEL), F32),
        compiler_params=_params("parallel", "parallel"),
        name="xattn_prompt",
    )(q3, mk3, mv3, x13, wo)


def _xattn_sample_body(q_ref, mk_ref, mv_ref, o_ref):
    q8 = jnp.broadcast_to(q_ref[0], (8, D_MODEL))
    for h in range(X_HEADS):
        hs = slice(h * X_HD, (h + 1) * X_HD)
        s = _mm_nt(mk_ref[0, :, h, :], q8[:, hs])[:, 0:1] * (X_HD ** -0.5)
        p = jnp.exp(s - jnp.max(s, axis=0, keepdims=True))
        o = jnp.sum(p * mv_ref[0, :, h, :], axis=0, keepdims=True) / jnp.sum(p, axis=0, keepdims=True)
        o_ref[0, :, hs] = o.astype(BF16)


def _xattn_sample(q, mk4, mv4):
    ns = q.shape[0]
    q3 = q.reshape(ns, 1, D_MODEL)
    row = lambda i: (i, 0, 0)
    mem = lambda i: (i, 0, 0, 0)
    return pl.pallas_call(
        _xattn_sample_body,
        grid=(ns,),
        in_specs=[pl.BlockSpec((1, 1, D_MODEL), row), pl.BlockSpec((1, N_MEM, X_HEADS, X_HD), mem),
                  pl.BlockSpec((1, N_MEM, X_HEADS, X_HD), mem)],
        out_specs=pl.BlockSpec((1, 1, D_MODEL), row),
        out_shape=jax.ShapeDtypeStruct((ns, 1, D_MODEL), BF16),
        compiler_params=_params("parallel"),
        name="xattn_sample",
    )(q3, mk4, mv4)


def _resid_mm_body(x_ref, a_ref, w_ref, o_ref):
    o_ref[...] = x_ref[...] + jnp.dot(a_ref[...], w_ref[...], preferred_element_type=F32)


def _resid_mm(x, a, w):
    m = x.shape[0]
    return pl.pallas_call(
        _resid_mm_body,
        out_shape=jax.ShapeDtypeStruct((m, D_MODEL), F32),
        compiler_params=pltpu.CompilerParams(vmem_limit_bytes=VMEM_LIMIT),
        name="resid_mm",
    )(x, a, w)


def _ffn_body(x_ref, ng_ref, wg_ref, wu_ref, wd_ref, nf_ref, y_ref):
    x = x_ref[...]
    h = _rms(x, ng_ref[...]).astype(BF16)
    gate = jnp.dot(h, wg_ref[...], preferred_element_type=F32)
    up = jnp.dot(h, wu_ref[...], preferred_element_type=F32)
    x3 = x + _mm(_silu(gate) * up, wd_ref[...])
    y_ref[...] = _rms(x3, nf_ref[...])


def _ffn(x, ng, wg, wu, wd, nf, tm):
    m = x.shape[0]
    dff = wg.shape[1]
    row = lambda i: (i, 0)
    c2 = lambda i: (0, 0)
    single = pl.Buffered(1)
    return pl.pallas_call(
        _ffn_body,
        grid=(m // tm,),
        in_specs=[pl.BlockSpec((tm, D_MODEL), row), pl.BlockSpec((1, D_MODEL), c2),
                  pl.BlockSpec((D_MODEL, dff), c2, pipeline_mode=single),
                  pl.BlockSpec((D_MODEL, dff), c2, pipeline_mode=single),
                  pl.BlockSpec((dff, D_MODEL), c2, pipeline_mode=single),
                  pl.BlockSpec((1, D_MODEL), c2)],
        out_specs=pl.BlockSpec((tm, D_MODEL), row),
        out_shape=jax.ShapeDtypeStruct((m, D_MODEL), F32),
        compiler_params=_params("parallel"),
        name="ffn",
    )(x, ng, wg, wu, wd, nf)


def _rope_tables(pos):
    half = RET_DK // 2
    inv = ROPE_BASE ** (-jnp.arange(half, dtype=F32) / half)
    ang = pos.astype(F32)[:, None] * inv[None, :]
    cos, sin = jnp.cos(ang), jnp.sin(ang)
    return jnp.concatenate([cos, cos], axis=-1), jnp.concatenate([-sin, sin], axis=-1)


def _pad_lanes(v, offset):
    return jnp.zeros((BA_PAD,), F32).at[offset:offset + v.shape[0]].set(v)


def kernel(x_prompt, x_sample, state_ret, state_gdn, state_conv, cache_mem_k, cache_mem_v, mem_prompt,
           norm_mix_g, w_in, ret_gn_g, w_branch_a, gdn_conv_w, gdn_a_log, gdn_dt_bias, gdn_norm_g,
           w_branch_b, w_out, norm_x_g, mem_norm_g, w_xq, w_xk, w_xv, w_xo, norm_ffn_g, w_gate, w_up,
           w_down, norm_final_g):
    depth = w_in.shape[0]
    assert depth == 1, "single-layer kernel"
    b, t, _ = x_prompt.shape
    ns = x_sample.shape[0]
    l = 0

    w = w_in[l]
    ba0 = 7168
    w_main = jnp.concatenate([w[:, :ba0], w[:, ba0 + 2 * GDN_HEADS:]], axis=1).astype(BF16)
    w_ba = jnp.pad(w[:, ba0:ba0 + 2 * GDN_HEADS], ((0, 0), (0, BA_PAD - 2 * GDN_HEADS))).astype(BF16)
    w_bat = w_ba.T
    row = lambda v: v.reshape(1, -1)
    wa, wb, wo = w_branch_a[l].astype(BF16), w_branch_b[l].astype(BF16), w_out[l].astype(BF16)
    wq, wk, wv, wxo = w_xq[l].astype(BF16), w_xk[l].astype(BF16), w_xv[l].astype(BF16), w_xo[l].astype(BF16)
    wg, wu, wd = w_gate[l].astype(BF16), w_up[l].astype(BF16), w_down[l].astype(BF16)
    alog_r = _pad_lanes(gdn_a_log[l], GDN_HEADS).reshape(1, BA_PAD)
    dt_r = _pad_lanes(gdn_dt_bias[l], GDN_HEADS).reshape(1, BA_PAD)
    alog_c, dt_c = alog_r.reshape(BA_PAD, 1), dt_r.reshape(BA_PAD, 1)
    cos_p, sin_p = _rope_tables(jnp.arange(t))
    cos_s, sin_s = _rope_tables(PAST_LEN + jnp.arange(1))

    xp = x_prompt.reshape(b * t, D_MODEL)
    proj_p, ba_p, bat_p = _inproj(xp, row(norm_mix_g[l]), w_main, w_ba, w_bat, tm=1024)
    proj_p3 = proj_p.reshape(b, t, N_MAIN)
    oa_p, sr_p = _retention_prompt(proj_p3, cos_p, sin_p, row(ret_gn_g[l]))
    ob_p, sg_p, sc_p = _gdn_prompt(proj_p3, ba_p.reshape(b, t, BA_PAD), bat_p, gdn_conv_w[l], alog_r, dt_r,
                                   alog_c, dt_c, row(gdn_norm_g[l]))
    x1_p, q_p = _merge(xp, proj_p, oa_p.reshape(b * t, -1), ob_p.reshape(b * t, -1), wa, wb, wo,
                       row(norm_x_g[l]), wq, tm=512)
    mk_p, mv_p = _memkv(mem_prompt.reshape(b * N_MEM, D_MODEL), row(mem_norm_g[l]), wk, wv)
    x2_p = _xattn_prompt(q_p.reshape(b, t, D_MODEL), mk_p.reshape(b, N_MEM, D_MODEL),
                         mv_p.reshape(b, N_MEM, D_MODEL), x1_p.reshape(b, t, D_MODEL), wxo)
    y_p = _ffn(x2_p.reshape(b * t, D_MODEL), row(norm_ffn_g[l]), wg, wu, wd, row(norm_final_g), tm=256)

    xs = x_sample.reshape(ns, D_MODEL)
    proj_s, ba_s, _ = _inproj(xs, row(norm_mix_g[l]), w_main, w_ba, w_bat, tm=ns)
    oa_s, ob_s, sr_s, sg_s, sc_s = _sample_mix(proj_s, ba_s, state_ret[l], state_gdn[l], state_conv[l],
                                               cos_s, sin_s, gdn_conv_w[l], alog_r, dt_r, row(ret_gn_g[l]),
                                               row(gdn_norm_g[l]))
    x1_s, q_s = _merge(xs, proj_s, oa_s.reshape(ns, -1), ob_s.reshape(ns, -1), wa, wb, wo,
                       row(norm_x_g[l]), wq, tm=ns)
    o_s = _xattn_sample(q_s, cache_mem_k[l], cache_mem_v[l])
    x2_s = _resid_mm(x1_s, o_s.reshape(ns, D_MODEL), wxo)
    y_s = _ffn(x2_s, row(norm_ffn_g[l]), wg, wu, wd, row(norm_final_g), tm=ns)

    return (y_p.reshape(b, t, D_MODEL), y_s.reshape(ns, 1, D_MODEL),
            sr_p[None], sg_p[None], sc_p[None],
            mk_p.reshape(1, b, N_MEM, X_HEADS, X_HD), mv_p.reshape(1, b, N_MEM, X_HEADS, X_HD),
            sr_s[None], sg_s[None], sc_s[None])
```

```python
import functools

import numpy as np
import jax
import jax.numpy as jnp
from jax import lax
from jax.experimental import pallas as pl
from jax.experimental.pallas import tpu as pltpu

F32 = jnp.float32
BF16 = jnp.bfloat16

D_MODEL = 1024
RET_HEADS, RET_DK, RET_DV = 4, 128, 256
GDN_HEADS, GDN_DK, GDN_DV = 8, 128, 128
CONV_W = 4
CONV_CH = 3 * GDN_HEADS * GDN_DK
N_MEM, X_HEADS, X_HD = 256, 4, 256
PAST_LEN = 16384
ROPE_BASE = 10000.0
EPS = 1e-6
GDN_CHUNK = 64

COL_RET = 0
COL_GDN = 3072
COL_GATE = 7168
N_MAIN = 9216
BA_PAD = 128

VMEM_LIMIT = 56 * 1024 * 1024

NT_DIMS = (((1,), (1,)), ((), ()))
TN_DIMS = (((0,), (0,)), ((), ()))


def _mm(a, b):
    return jnp.dot(a.astype(BF16), b.astype(BF16), preferred_element_type=F32)


def _mm_nt(a, b):
    return lax.dot_general(a.astype(BF16), b.astype(BF16), NT_DIMS, preferred_element_type=F32)


def _mm_tn(a, b):
    return lax.dot_general(a.astype(BF16), b.astype(BF16), TN_DIMS, preferred_element_type=F32)


def _mm_f32(a, b):
    return jnp.dot(a, b, preferred_element_type=F32, precision=lax.Precision.HIGHEST)


def _rms(x, g):
    return x * lax.rsqrt(jnp.mean(x * x, axis=-1, keepdims=True) + EPS) * g


def _silu(x):
    return x * jax.nn.sigmoid(x)


def _softplus(x):
    return jnp.maximum(x, 0.0) + jnp.log1p(jnp.exp(-jnp.abs(x)))


def _params(*sem):
    return pltpu.CompilerParams(dimension_semantics=sem, vmem_limit_bytes=VMEM_LIMIT)


def _inproj_body(x_ref, g_ref, w_ref, wba_ref, wbat_ref, o_ref, oba_ref, obat_ref, h_scr):
    @pl.when(pl.program_id(1) == 0)
    def _():
        hb = _rms(x_ref[...], g_ref[...]).astype(BF16)
        h_scr[...] = hb
        oba_ref[...] = jnp.dot(hb, wba_ref[...], preferred_element_type=F32)
        obat_ref[...] = lax.dot_general(wbat_ref[...], hb, NT_DIMS, preferred_element_type=F32)

    o_ref[...] = jnp.dot(h_scr[...], w_ref[...], preferred_element_type=F32)


def _inproj(x, g, w_main, w_ba, w_bat, tm, tn=1024):
    m = x.shape[0]
    return pl.pallas_call(
        _inproj_body,
        grid=(m // tm, N_MAIN // tn),
        in_specs=[
            pl.BlockSpec((tm, D_MODEL), lambda i, j: (i, 0)),
            pl.BlockSpec((1, D_MODEL), lambda i, j: (0, 0)),
            pl.BlockSpec((D_MODEL, tn), lambda i, j: (0, j)),
            pl.BlockSpec((D_MODEL, BA_PAD), lambda i, j: (0, 0)),
            pl.BlockSpec((BA_PAD, D_MODEL), lambda i, j: (0, 0)),
        ],
        out_specs=[
            pl.BlockSpec((tm, tn), lambda i, j: (i, j)),
            pl.BlockSpec((tm, BA_PAD), lambda i, j: (i, 0)),
            pl.BlockSpec((BA_PAD, tm), lambda i, j: (0, i)),
        ],
        out_shape=[
            jax.ShapeDtypeStruct((m, N_MAIN), F32),
            jax.ShapeDtypeStruct((m, BA_PAD), F32),
            jax.ShapeDtypeStruct((BA_PAD, m), F32),
        ],
        scratch_shapes=[pltpu.VMEM((tm, D_MODEL), BF16)],
        compiler_params=_params("parallel", "arbitrary"),
        name="inproj",
    )(x, g, w_main, w_ba, w_bat)


_RET_LOG_G = np.log1p(-np.exp2(-5.0 - np.arange(RET_HEADS, dtype=np.float64)))


def _ret_tables(c):
    idx = np.arange(c, dtype=np.float64)
    diff = idx[:, None] - idx[None, :]
    dmat = np.where(diff >= 0, np.exp(np.maximum(diff, 0.0)[None] * _RET_LOG_G[:, None, None]), 0.0)
    qdec = np.exp((idx + 1.0)[None, :] * _RET_LOG_G[:, None])
    kdec = np.exp((c - 1.0 - idx)[None, :] * _RET_LOG_G[:, None])
    lane = np.ones((1, 1, RET_DK))
    return (jnp.asarray(dmat, F32), jnp.asarray(qdec[:, :, None] * lane, F32),
            jnp.asarray(kdec[:, :, None] * lane, F32), [float(v) for v in np.exp(c * _RET_LOG_G)])


def _rot(x, cos, sin):
    return x * cos + pltpu.roll(x, RET_DK // 2, 1) * sin


def _group_norm_gate(o, gate, gn):
    mu = jnp.mean(o, axis=-1, keepdims=True)
    d = o - mu
    var = jnp.mean(d * d, axis=-1, keepdims=True)
    return _silu(gate) * (d * lax.rsqrt(var + EPS) * gn)


def _ret_body(cdec, q_ref, k_ref, v_ref, g_ref, cos_ref, sin_ref, dmat_ref, qdec_ref, kdec_ref, gn_ref,
              o_ref, s_out_ref, s_scr):
    t = pl.program_id(1)

    @pl.when(t == 0)
    def _():
        s_scr[...] = jnp.zeros_like(s_scr)

    cos, sin = cos_ref[...], sin_ref[...]
    for h in range(RET_HEADS):
        qk = slice(h * RET_DK, (h + 1) * RET_DK)
        vv = slice(h * RET_DV, (h + 1) * RET_DV)
        q = _rot(q_ref[0, :, qk], cos, sin)
        k = _rot(k_ref[0, :, qk], cos, sin) * (RET_DK ** -0.5)
        v = v_ref[0, :, vv]
        s = s_scr[h]
        scores = _mm_nt(q, k) * dmat_ref[h]
        o = _mm(scores, v) + _mm(q * qdec_ref[h], s)
        s_scr[h] = cdec[h] * s + _mm_tn(k * kdec_ref[h], v)
        o_ref[0, :, vv] = _group_norm_gate(o, g_ref[0, :, vv], gn_ref[:, vv]).astype(BF16)

    @pl.when(t == pl.num_programs(1) - 1)
    def _():
        s_out_ref[0] = s_scr[...]


def _retention_prompt(proj3, cos, sin, gn, tb=256):
    b, t, _ = proj3.shape
    dmat, qdec, kdec, cdec = _ret_tables(tb)
    qw = RET_HEADS * RET_DK
    vw = RET_HEADS * RET_DV
    const3 = lambda i, j: (0, 0, 0)
    return pl.pallas_call(
        functools.partial(_ret_body, cdec),
        grid=(b, t // tb),
        in_specs=[
            pl.BlockSpec((1, tb, qw), lambda i, j: (i, j, 0)),
            pl.BlockSpec((1, tb, qw), lambda i, j: (i, j, 1)),
            pl.BlockSpec((1, tb, vw), lambda i, j: (i, j, 1)),
            pl.BlockSpec((1, tb, vw), lambda i, j: (i, j, 2)),
            pl.BlockSpec((tb, RET_DK), lambda i, j: (j, 0)),
            pl.BlockSpec((tb, RET_DK), lambda i, j: (j, 0)),
            pl.BlockSpec((RET_HEADS, tb, tb), const3),
            pl.BlockSpec((RET_HEADS, tb, RET_DK), const3),
            pl.BlockSpec((RET_HEADS, tb, RET_DK), const3),
            pl.BlockSpec((1, vw), lambda i, j: (0, 0)),
        ],
        out_specs=[
            pl.BlockSpec((1, tb, vw), lambda i, j: (i, j, 0)),
            pl.BlockSpec((1, RET_HEADS, RET_DK, RET_DV), lambda i, j: (i, 0, 0, 0)),
        ],
        out_shape=[
            jax.ShapeDtypeStruct((b, t, vw), BF16),
            jax.ShapeDtypeStruct((b, RET_HEADS, RET_DK, RET_DV), F32),
        ],
        scratch_shapes=[pltpu.VMEM((RET_HEADS, RET_DK, RET_DV), F32)],
        compiler_params=_params("parallel", "arbitrary"),
        name="retention_prompt",
    )(proj3, proj3, proj3, proj3, cos, sin, dmat, qdec, kdec, gn)


def _gdn_tables(tb):
    idx = np.arange(tb)
    same = (idx[:, None] // GDN_CHUNK) == (idx[None, :] // GDN_CHUNK)
    lower = same & (idx[:, None] >= idx[None, :])
    nchunk = tb // GDN_CHUNK
    chunk_sel = np.repeat((idx[:, None] // GDN_CHUNK) == np.arange(nchunk)[None, :], 128, axis=1)
    grp = 2 * GDN_CHUNK
    il = (idx % grp)[:, None]
    jl = np.arange(grp)[None, :]
    bias = np.where((il // GDN_CHUNK == jl // GDN_CHUNK) & (il >= jl), 0.0, -1e30)
    eye = (il == jl).astype(np.float64)
    return (jnp.asarray(lower, F32), jnp.asarray(lower.T, F32), jnp.asarray(same, F32),
            jnp.asarray(chunk_sel, F32), jnp.asarray(bias, F32), jnp.asarray(1.0 - eye, F32),
            jnp.asarray(eye, BF16))


def _gdn_body(tb, qkv_ref, z_ref, ba_ref, bat_ref, cw_ref, alog_r_ref, dt_r_ref, alog_c_ref, dt_c_ref,
              ng_ref, lbd_ref, ubd_ref, obd_ref, csel_ref, bias_ref, offd_ref, eye_ref,
              o_ref, s_out_ref, conv_out_ref, s_scr, x_scr, c_scr):
    t = pl.program_id(1)
    nchunk = tb // GDN_CHUNK
    hk = GDN_HEADS * GDN_DK

    @pl.when(t == 0)
    def _():
        s_scr[...] = jnp.zeros_like(s_scr)
        x_scr[0:8, :] = jnp.zeros((8, CONV_CH), F32)

    x_scr[8:8 + tb, :] = qkv_ref[0]
    for c0 in range(0, CONV_CH, GDN_DK):
        cs = slice(c0, c0 + GDN_DK)
        acc = x_scr[5:5 + tb, cs] * cw_ref[0:1, cs]
        for i in range(1, CONV_W):
            acc = acc + x_scr[5 + i:5 + i + tb, cs] * cw_ref[i:i + 1, cs]
        c_scr[:, cs] = _silu(acc)

    @pl.when(t == pl.num_programs(1) - 1)
    def _():
        conv_out_ref[0] = x_scr[tb + 5:tb + 8, :]

    x_scr[0:8, :] = x_scr[tb:tb + 8, :]

    ba = ba_ref[0]
    beta_c = jax.nn.sigmoid(ba)
    g_c = -jnp.exp(alog_r_ref[...]) * _softplus(ba + dt_r_ref[...])
    g_r = -jnp.exp(alog_c_ref[...]) * _softplus(bat_ref[...] + dt_c_ref[...])
    gc_c = _mm_f32(lbd_ref[...], g_c)
    gt_c = _mm_f32(obd_ref[...], g_c)
    gc_r = _mm_f32(g_r, ubd_ref[...])
    gt_l = _mm_f32(g_r, csel_ref[...])

    heads = range(GDN_HEADS)
    grp = 2 * GDN_CHUNK
    groups = [slice(p * grp, (p + 1) * grp) for p in range(tb // grp)]

    def grp_dot(a, b):
        return jnp.concatenate([jnp.dot(a[g], b[g], preferred_element_type=F32) for g in groups], axis=0)

    def grp_dot_nt(a, b):
        return jnp.concatenate([lax.dot_general(a[g], b[g], NT_DIMS, preferred_element_type=F32)
                                for g in groups], axis=0)

    bias = bias_ref[...]
    offdiag = offd_ref[...]
    eye_b = eye_ref[...]
    qs, ks, gammas, pbs, rhss, qgs, khs = [], [], [], [], [], [], []
    for h in heads:
        q = c_scr[:, h * GDN_DK:(h + 1) * GDN_DK]
        k = c_scr[:, hk + h * GDN_DK:hk + (h + 1) * GDN_DK]
        v = c_scr[:, 2 * hk + h * GDN_DV:2 * hk + (h + 1) * GDN_DV]
        q = q * lax.rsqrt(jnp.sum(q * q, axis=-1, keepdims=True) + EPS) * (GDN_DK ** -0.5)
        k = k * lax.rsqrt(jnp.sum(k * k, axis=-1, keepdims=True) + EPS)
        beta = beta_c[:, h:h + 1]
        gcc = gc_c[:, 8 + h:9 + h]
        gtc = gt_c[:, 8 + h:9 + h]
        gcr = gc_r[8 + h:9 + h, :]
        dg = jnp.concatenate([gcc[g] - gcr[:, g] for g in groups], axis=0)
        gamma = jnp.exp(dg + bias)
        kbeta = k * beta
        kb = k.astype(BF16)
        pbs.append((grp_dot_nt((-kbeta).astype(BF16), kb) * (gamma * offdiag)).astype(BF16))
        eg = jnp.exp(gcc)
        rhss.append(jnp.concatenate([v * beta, kbeta * eg], axis=1))
        qs.append(q.astype(BF16))
        ks.append(kb)
        gammas.append(gamma)
        qgs.append(q * eg)
        khs.append(k * jnp.exp(gtc - gcc))

    tbs = [pbs[h] + eye_b for h in heads]
    for lvl in range(5):
        pbs = [grp_dot(pbs[h], pbs[h]).astype(BF16) for h in heads]
        tnew = [grp_dot(tbs[h], pbs[h] + eye_b) for h in heads]
        tbs = [x.astype(BF16) for x in tnew]

    us, ws, qks = [], [], []
    for h in heads:
        uw = rhss[h] + grp_dot(tbs[h] - eye_b, rhss[h].astype(BF16))
        us.append(uw[:, :GDN_DV])
        ws.append(uw[:, GDN_DV:])
        qks.append((grp_dot_nt(qs[h], ks[h]) * gammas[h]).astype(BF16))

    s = [s_scr[h] for h in heads]
    vn_parts = [[] for _ in heads]
    qs_parts = [[] for _ in heads]
    for c in range(nchunk):
        rows = slice(c * GDN_CHUNK, (c + 1) * GDN_CHUNK)
        for h in heads:
            wq = _mm(jnp.concatenate([ws[h][rows], qgs[h][rows]], axis=0), s[h])
            vn = us[h][rows] - wq[:GDN_CHUNK]
            qs_parts[h].append(wq[GDN_CHUNK:])
            vn_parts[h].append(vn)
            decay = jnp.exp(gt_l[8 + h:9 + h, c * 128:(c + 1) * 128])
            s[h] = decay * s[h] + _mm_tn(khs[h][rows], vn)
    for h in heads:
        hs = slice(h * GDN_DV, (h + 1) * GDN_DV)
        s_scr[h] = s[h]
        vn = jnp.concatenate(vn_parts[h], axis=0).astype(BF16)
        o = jnp.concatenate(qs_parts[h], axis=0) + grp_dot(qks[h], vn)
        o_ref[0, :, hs] = (_rms(o, ng_ref[...]) * _silu(z_ref[0, :, hs])).astype(BF16)

    @pl.when(t == pl.num_programs(1) - 1)
    def _():
        s_out_ref[0] = s_scr[...]


def _gdn_prompt(proj3, ba3, bat, conv_w, alog_r, dt_r, alog_c, dt_c, norm_g, tb=256):
    b, t, _ = proj3.shape
    nt = t // tb
    lbd, ubd, obd, csel, bias, offdiag, eye = _gdn_tables(tb)
    vw = GDN_HEADS * GDN_DV
    c2 = lambda i, j: (0, 0)
    return pl.pallas_call(
        functools.partial(_gdn_body, tb),
        grid=(b, nt),
        in_specs=[
            pl.BlockSpec((1, tb, CONV_CH), lambda i, j: (i, j, COL_GDN // CONV_CH)),
            pl.BlockSpec((1, tb, vw), lambda i, j: (i, j, (COL_GDN + CONV_CH) // vw)),
            pl.BlockSpec((1, tb, BA_PAD), lambda i, j: (i, j, 0)),
            pl.BlockSpec((BA_PAD, tb), lambda i, j: (0, i * nt + j)),
            pl.BlockSpec((CONV_W, CONV_CH), c2),
            pl.BlockSpec((1, BA_PAD), c2),
            pl.BlockSpec((1, BA_PAD), c2),
            pl.BlockSpec((BA_PAD, 1), c2),
            pl.BlockSpec((BA_PAD, 1), c2),
            pl.BlockSpec((1, GDN_DV), c2),
            pl.BlockSpec((tb, tb), c2),
            pl.BlockSpec((tb, tb), c2),
            pl.BlockSpec((tb, tb), c2),
            pl.BlockSpec((tb, (tb // GDN_CHUNK) * 128), c2),
            pl.BlockSpec((tb, 2 * GDN_CHUNK), c2),
            pl.BlockSpec((tb, 2 * GDN_CHUNK), c2),
            pl.BlockSpec((tb, 2 * GDN_CHUNK), c2),
        ],
        out_specs=[
            pl.BlockSpec((1, tb, vw), lambda i, j: (i, j, 0)),
            pl.BlockSpec((1, GDN_HEADS, GDN_DK, GDN_DV), lambda i, j: (i, 0, 0, 0)),
            pl.BlockSpec((1, CONV_W - 1, CONV_CH), lambda i, j: (i, 0, 0)),
        ],
        out_shape=[
            jax.ShapeDtypeStruct((b, t, vw), BF16),
            jax.ShapeDtypeStruct((b, GDN_HEADS, GDN_DK, GDN_DV), F32),
            jax.ShapeDtypeStruct((b, CONV_W - 1, CONV_CH), F32),
        ],
        scratch_shapes=[pltpu.VMEM((GDN_HEADS, GDN_DK, GDN_DV), F32), pltpu.VMEM((tb + 8, CONV_CH), F32),
                        pltpu.VMEM((tb, CONV_CH), F32)],
        compiler_params=_params("parallel", "arbitrary"),
        name="gdn_prompt",
    )(proj3, proj3, ba3, bat, conv_w, alog_r, dt_r, alog_c, dt_c, norm_g, lbd, ubd, obd, csel, bias, offdiag, eye)


def _sample_mix_body(cdec, ret_ref, gdn_ref, z_ref, ba_ref, sr_ref, sg_ref, sc_ref, cos_ref, sin_ref, cw_ref,
                     alog_ref, dt_ref, gn_ref, ng_ref,
                     oa_ref, ob_ref, sr_out_ref, sg_out_ref, sc_out_ref, col_scr):
    hk = GDN_HEADS * GDN_DK
    cos, sin = cos_ref[...], sin_ref[...]
    ret = ret_ref[0]
    x_new = gdn_ref[0]
    z = z_ref[0]
    buf = sc_ref[0]
    conv = x_new * cw_ref[CONV_W - 1:CONV_W, :]
    for i in range(CONV_W - 1):
        conv = conv + buf[i:i + 1, :] * cw_ref[i:i + 1, :]
    conv = _silu(conv)
    sc_out_ref[0, 0:2, :] = buf[1:3, :]
    sc_out_ref[0, 2:3, :] = x_new

    ba = ba_ref[0]
    beta = jax.nn.sigmoid(ba)
    eg = jnp.exp(-jnp.exp(alog_ref[...]) * _softplus(ba + dt_ref[...]))

    col_scr[...] = jnp.zeros_like(col_scr)
    nr = RET_HEADS
    for h in range(RET_HEADS):
        sl = slice(h * RET_DK, (h + 1) * RET_DK)
        col_scr[h:h + 1, :] = _rot(ret[:, sl], cos, sin)
        col_scr[nr + h:nr + h + 1, :] = _rot(ret[:, 512 + h * RET_DK:512 + (h + 1) * RET_DK], cos, sin) * (RET_DK ** -0.5)
    for h in range(GDN_HEADS):
        q = conv[:, h * GDN_DK:(h + 1) * GDN_DK]
        k = conv[:, hk + h * GDN_DK:hk + (h + 1) * GDN_DK]
        q = q * lax.rsqrt(jnp.sum(q * q, axis=-1, keepdims=True) + EPS) * (GDN_DK ** -0.5)
        k = k * lax.rsqrt(jnp.sum(k * k, axis=-1, keepdims=True) + EPS)
        col_scr[2 * nr + h:2 * nr + h + 1, :] = q
        col_scr[2 * nr + GDN_HEADS + h:2 * nr + GDN_HEADS + h + 1, :] = k
    cols = col_scr[...].T

    for h in range(RET_HEADS):
        vv = slice(h * RET_DV, (h + 1) * RET_DV)
        qc = cols[:, h:h + 1]
        kc = cols[:, nr + h:nr + h + 1]
        v = ret[:, 1024 + h * RET_DV:1024 + (h + 1) * RET_DV]
        s_new = cdec[h] * sr_ref[0, h] + kc * v
        sr_out_ref[0, h] = s_new
        o = jnp.sum(qc * s_new, axis=0, keepdims=True)
        oa_ref[0, :, vv] = _group_norm_gate(o, ret[:, 2048 + h * RET_DV:2048 + (h + 1) * RET_DV],
                                            gn_ref[:, vv]).astype(BF16)

    for h in range(GDN_HEADS):
        hs = slice(h * GDN_DV, (h + 1) * GDN_DV)
        qc = cols[:, 2 * nr + h:2 * nr + h + 1]
        kc = cols[:, 2 * nr + GDN_HEADS + h:2 * nr + GDN_HEADS + h + 1]
        v = conv[:, 2 * hk + h * GDN_DV:2 * hk + (h + 1) * GDN_DV]
        egh = eg[:, 8 + h:9 + h]
        s = sg_ref[0, h]
        ks = jnp.sum(kc * s, axis=0, keepdims=True)
        vn = beta[:, h:h + 1] * (v - egh * ks)
        s_new = egh * s + kc * vn
        sg_out_ref[0, h] = s_new
        o = jnp.sum(qc * s_new, axis=0, keepdims=True)
        ob_ref[0, :, hs] = (_rms(o, ng_ref[...]) * _silu(z[:, hs])).astype(BF16)


def _sample_mix(proj, ba, state_ret, state_gdn, state_conv, cos, sin, conv_w, alog_r, dt_r, gn, norm_g):
    ns = proj.shape[0]
    proj3 = proj.reshape(ns, 1, N_MAIN)
    ba3 = ba.reshape(ns, 1, BA_PAD)
    cdec = [float(v) for v in np.exp(_RET_LOG_G)]
    vw = RET_HEADS * RET_DV
    gw = GDN_HEADS * GDN_DV
    c2 = lambda i: (0, 0)
    return pl.pallas_call(
        functools.partial(_sample_mix_body, cdec),
        grid=(ns,),
        in_specs=[
            pl.BlockSpec((1, 1, 3072), lambda i: (i, 0, 0)),
            pl.BlockSpec((1, 1, CONV_CH), lambda i: (i, 0, COL_GDN // CONV_CH)),
            pl.BlockSpec((1, 1, gw), lambda i: (i, 0, (COL_GDN + CONV_CH) // gw)),
            pl.BlockSpec((1, 1, BA_PAD), lambda i: (i, 0, 0)),
            pl.BlockSpec((1, RET_HEADS, RET_DK, RET_DV), lambda i: (i, 0, 0, 0)),
            pl.BlockSpec((1, GDN_HEADS, GDN_DK, GDN_DV), lambda i: (i, 0, 0, 0)),
            pl.BlockSpec((1, CONV_W - 1, CONV_CH), lambda i: (i, 0, 0)),
            pl.BlockSpec((1, RET_DK), c2),
            pl.BlockSpec((1, RET_DK), c2),
            pl.BlockSpec((CONV_W, CONV_CH), c2),
            pl.BlockSpec((1, BA_PAD), c2),
            pl.BlockSpec((1, BA_PAD), c2),
            pl.BlockSpec((1, vw), c2),
            pl.BlockSpec((1, GDN_DV), c2),
        ],
        out_specs=[
            pl.BlockSpec((1, 1, vw), lambda i: (i, 0, 0)),
            pl.BlockSpec((1, 1, gw), lambda i: (i, 0, 0)),
            pl.BlockSpec((1, RET_HEADS, RET_DK, RET_DV), lambda i: (i, 0, 0, 0)),
            pl.BlockSpec((1, GDN_HEADS, GDN_DK, GDN_DV), lambda i: (i, 0, 0, 0)),
            pl.BlockSpec((1, CONV_W - 1, CONV_CH), lambda i: (i, 0, 0)),
        ],
        out_shape=[
            jax.ShapeDtypeStruct((ns, 1, vw), BF16),
            jax.ShapeDtypeStruct((ns, 1, gw), BF16),
            jax.ShapeDtypeStruct(state_ret.shape, F32),
            jax.ShapeDtypeStruct(state_gdn.shape, F32),
            jax.ShapeDtypeStruct(state_conv.shape, F32),
        ],
        scratch_shapes=[pltpu.VMEM((128, 128), F32)],
        compiler_params=_params("parallel"),
        name="sample_mix",
    )(proj3, proj3, proj3, ba3, state_ret, state_gdn, state_conv, cos, sin, conv_w, alog_r, dt_r, gn, norm_g)


def _merge_body(x_ref, ga_ref, gb_ref, oa_ref, ob_ref, wa_ref, wb_ref, wo_ref, ng_ref, wq_ref, x1_ref, q_ref):
    ya = jnp.dot(oa_ref[...], wa_ref[...], preferred_element_type=F32)
    yb = jnp.dot(ob_ref[...], wb_ref[...], preferred_element_type=F32)
    merged = jax.nn.sigmoid(ga_ref[...]) * ya + jax.nn.sigmoid(gb_ref[...]) * yb
    x1 = x_ref[...] + _mm(merged, wo_ref[...])
    x1_ref[...] = x1
    q_ref[...] = _mm(_rms(x1, ng_ref[...]), wq_ref[...]).astype(BF16)


def _merge(x, proj, oa, ob, wa, wb, wo, ng, wq, tm):
    m = x.shape[0]
    row = lambda i: (i, 0)
    c2 = lambda i: (0, 0)
    wspec = pl.BlockSpec((D_MODEL, D_MODEL), c2)
    return pl.pallas_call(
        _merge_body,
        grid=(m // tm,),
        in_specs=[
            pl.BlockSpec((tm, D_MODEL), row),
            pl.BlockSpec((tm, D_MODEL), lambda i: (i, COL_GATE // D_MODEL)),
            pl.BlockSpec((tm, D_MODEL), lambda i: (i, COL_GATE // D_MODEL + 1)),
            pl.BlockSpec((tm, D_MODEL), row),
            pl.BlockSpec((tm, D_MODEL), row),
            wspec, wspec, wspec,
            pl.BlockSpec((1, D_MODEL), c2),
            wspec,
        ],
        out_specs=[pl.BlockSpec((tm, D_MODEL), row), pl.BlockSpec((tm, D_MODEL), row)],
        out_shape=[jax.ShapeDtypeStruct((m, D_MODEL), F32), jax.ShapeDtypeStruct((m, D_MODEL), BF16)],
        compiler_params=_params("parallel"),
        name="merge",
    )(x, proj, proj, oa, ob, wa, wb, wo, ng, wq)


def _memkv_body(m_ref, g_ref, wk_ref, wv_ref, k_ref, v_ref):
    mn = _rms(m_ref[...], g_ref[...]).astype(BF16)
    k_ref[...] = jnp.dot(mn, wk_ref[...], preferred_element_type=F32)
    v_ref[...] = jnp.dot(mn, wv_ref[...], preferred_element_type=F32)


def _memkv(mem, g, wk, wv, tm=512):
    m = mem.shape[0]
    row = lambda i: (i, 0)
    c2 = lambda i: (0, 0)
    return pl.pallas_call(
        _memkv_body,
        grid=(m // tm,),
        in_specs=[pl.BlockSpec((tm, D_MODEL), row), pl.BlockSpec((1, D_MODEL), c2),
                  pl.BlockSpec((D_MODEL, D_MODEL), c2), pl.BlockSpec((D_MODEL, D_MODEL), c2)],
        out_specs=[pl.BlockSpec((tm, D_MODEL), row), pl.BlockSpec((tm, D_MODEL), row)],
        out_shape=[jax.ShapeDtypeStruct((m, D_MODEL), F32)] * 2,
        compiler_params=_params("parallel"),
        name="memkv",
    )(mem, g, wk, wv)


def _xattn_body(q_ref, mk_ref, mv_ref, x1_ref, wo_ref, x2_ref):
    parts = []
    for h in range(X_HEADS):
        hs = slice(h * X_HD, (h + 1) * X_HD)
        s = _mm_nt(q_ref[0, :, hs], mk_ref[0, :, hs]) * (X_HD ** -0.5)
        p = jnp.exp(s - jnp.max(s, axis=-1, keepdims=True))
        o = _mm(p, mv_ref[0, :, hs]) / jnp.sum(p, axis=-1, keepdims=True)
        parts.append(o.astype(BF16))
    x2_ref[0] = x1_ref[0] + jnp.dot(jnp.concatenate(parts, axis=1), wo_ref[...], preferred_element_type=F32)


def _xattn_prompt(q3, mk3, mv3, x13, wo, tq=512):
    b, t, _ = q3.shape
    tok = lambda i, j: (i, j, 0)
    mem = lambda i, j: (i, 0, 0)
    return pl.pallas_call(
        _xattn_body,
        grid=(b, t // tq),
        in_specs=[pl.BlockSpec((1, tq, D_MODEL), tok), pl.BlockSpec((1, N_MEM, D_MODEL), mem),
                  pl.BlockSpec((1, N_MEM, D_MODEL), mem), pl.BlockSpec((1, tq, D_MODEL), tok),
                  pl.BlockSpec((D_MODEL, D_MODEL), lambda i, j: (0, 0))],
        out_specs=pl.BlockSpec((1, tq, D_MODEL), tok),
        out_shape=jax.ShapeDtypeStruct((b, t, D_MODEL), F32),
        compiler_params=_params("parallel", "parallel"),
        name="xattn_prompt",
    )(q3, mk3, mv3, x13, wo)


def _xattn_sample_body(q_ref, mk_ref, mv_ref, o_ref):
    q = q_ref[0]
    s = jnp.sum(mk_ref[0] * q[None], axis=-1, keepdims=True) * (X_HD ** -0.5)
    p = jnp.exp(s - jnp.max(s, axis=0, keepdims=True))
    o = jnp.sum(p * mv_ref[0], axis=0) / jnp.sum(p, axis=0)
    o_ref[0] = o


def _xattn_sample(q, mk4, mv4):
    ns = q.shape[0]
    q3 = q.astype(F32).reshape(ns, X_HEADS, X_HD)
    row = lambda i: (i, 0, 0)
    mem = lambda i: (i, 0, 0, 0)
    return pl.pallas_call(
        _xattn_sample_body,
        grid=(ns,),
        in_specs=[pl.BlockSpec((1, X_HEADS, X_HD), row), pl.BlockSpec((1, N_MEM, X_HEADS, X_HD), mem),
                  pl.BlockSpec((1, N_MEM, X_HEADS, X_HD), mem)],
        out_specs=pl.BlockSpec((1, X_HEADS, X_HD), row),
        out_shape=jax.ShapeDtypeStruct((ns, X_HEADS, X_HD), F32),
        compiler_params=_params("parallel"),
        name="xattn_sample",
    )(q3, mk4, mv4)


def _resid_mm_body(x_ref, a_ref, w_ref, o_ref):
    o_ref[...] = x_ref[...] + jnp.dot(a_ref[...], w_ref[...], preferred_element_type=F32)


def _resid_mm(x, a, w):
    m = x.shape[0]
    return pl.pallas_call(
        _resid_mm_body,
        out_shape=jax.ShapeDtypeStruct((m, D_MODEL), F32),
        compiler_params=pltpu.CompilerParams(vmem_limit_bytes=VMEM_LIMIT),
        name="resid_mm",
    )(x, a, w)


def _ffn_body(x_ref, ng_ref, wg_ref, wu_ref, wd_ref, nf_ref, y_ref):
    x = x_ref[...]
    h = _rms(x, ng_ref[...]).astype(BF16)
    gate = jnp.dot(h, wg_ref[...], preferred_element_type=F32)
    up = jnp.dot(h, wu_ref[...], preferred_element_type=F32)
    x3 = x + _mm(_silu(gate) * up, wd_ref[...])
    y_ref[...] = _rms(x3, nf_ref[...])


def _ffn(x, ng, wg, wu, wd, nf, tm):
    m = x.shape[0]
    dff = wg.shape[1]
    row = lambda i: (i, 0)
    c2 = lambda i: (0, 0)
    single = pl.Buffered(1)
    return pl.pallas_call(
        _ffn_body,
        grid=(m // tm,),
        in_specs=[pl.BlockSpec((tm, D_MODEL), row), pl.BlockSpec((1, D_MODEL), c2),
                  pl.BlockSpec((D_MODEL, dff), c2, pipeline_mode=single),
                  pl.BlockSpec((D_MODEL, dff), c2, pipeline_mode=single),
                  pl.BlockSpec((dff, D_MODEL), c2, pipeline_mode=single),
                  pl.BlockSpec((1, D_MODEL), c2)],
        out_specs=pl.BlockSpec((tm, D_MODEL), row),
        out_shape=jax.ShapeDtypeStruct((m, D_MODEL), F32),
        compiler_params=_params("parallel"),
        name="ffn",
    )(x, ng, wg, wu, wd, nf)


def _rope_tables(pos):
    half = RET_DK // 2
    inv = ROPE_BASE ** (-jnp.arange(half, dtype=F32) / half)
    ang = pos.astype(F32)[:, None] * inv[None, :]
    cos, sin = jnp.cos(ang), jnp.sin(ang)
    return jnp.concatenate([cos, cos], axis=-1), jnp.concatenate([-sin, sin], axis=-1)


def _pad_lanes(v, offset):
    return jnp.zeros((BA_PAD,), F32).at[offset:offset + v.shape[0]].set(v)


def kernel(x_prompt, x_sample, state_ret, state_gdn, state_conv, cache_mem_k, cache_mem_v, mem_prompt,
           norm_mix_g, w_in, ret_gn_g, w_branch_a, gdn_conv_w, gdn_a_log, gdn_dt_bias, gdn_norm_g,
           w_branch_b, w_out, norm_x_g, mem_norm_g, w_xq, w_xk, w_xv, w_xo, norm_ffn_g, w_gate, w_up,
           w_down, norm_final_g):
    depth = w_in.shape[0]
    assert depth == 1, "single-layer kernel"
    b, t, _ = x_prompt.shape
    ns = x_sample.shape[0]
    l = 0

    w = w_in[l]
    ba0 = 7168
    w_main = jnp.concatenate([w[:, :ba0], w[:, ba0 + 2 * GDN_HEADS:]], axis=1).astype(BF16)
    w_ba = jnp.pad(w[:, ba0:ba0 + 2 * GDN_HEADS], ((0, 0), (0, BA_PAD - 2 * GDN_HEADS))).astype(BF16)
    w_bat = w_ba.T
    row = lambda v: v.reshape(1, -1)
    wa, wb, wo = w_branch_a[l].astype(BF16), w_branch_b[l].astype(BF16), w_out[l].astype(BF16)
    wq, wk, wv, wxo = w_xq[l].astype(BF16), w_xk[l].astype(BF16), w_xv[l].astype(BF16), w_xo[l].astype(BF16)
    wg, wu, wd = w_gate[l].astype(BF16), w_up[l].astype(BF16), w_down[l].astype(BF16)
    alog_r = _pad_lanes(gdn_a_log[l], GDN_HEADS).reshape(1, BA_PAD)
    dt_r = _pad_lanes(gdn_dt_bias[l], GDN_HEADS).reshape(1, BA_PAD)
    alog_c, dt_c = alog_r.reshape(BA_PAD, 1), dt_r.reshape(BA_PAD, 1)
    cos_p, sin_p = _rope_tables(jnp.arange(t))
    cos_s, sin_s = _rope_tables(PAST_LEN + jnp.arange(1))

    xp = x_prompt.reshape(b * t, D_MODEL)
    proj_p, ba_p, bat_p = _inproj(xp, row(norm_mix_g[l]), w_main, w_ba, w_bat, tm=1024)
    proj_p3 = proj_p.reshape(b, t, N_MAIN)
    oa_p, sr_p = _retention_prompt(proj_p3, cos_p, sin_p, row(ret_gn_g[l]))
    ob_p, sg_p, sc_p = _gdn_prompt(proj_p3, ba_p.reshape(b, t, BA_PAD), bat_p, gdn_conv_w[l], alog_r, dt_r,
                                   alog_c, dt_c, row(gdn_norm_g[l]))
    x1_p, q_p = _merge(xp, proj_p, oa_p.reshape(b * t, -1), ob_p.reshape(b * t, -1), wa, wb, wo,
                       row(norm_x_g[l]), wq, tm=512)
    mk_p, mv_p = _memkv(mem_prompt.reshape(b * N_MEM, D_MODEL), row(mem_norm_g[l]), wk, wv)
    x2_p = _xattn_prompt(q_p.reshape(b, t, D_MODEL), mk_p.reshape(b, N_MEM, D_MODEL),
                         mv_p.reshape(b, N_MEM, D_MODEL), x1_p.reshape(b, t, D_MODEL), wxo)
    y_p = _ffn(x2_p.reshape(b * t, D_MODEL), row(norm_ffn_g[l]), wg, wu, wd, row(norm_final_g), tm=256)

    xs = x_sample.reshape(ns, D_MODEL)
    proj_s, ba_s, _ = _inproj(xs, row(norm_mix_g[l]), w_main, w_ba, w_bat, tm=ns)
    oa_s, ob_s, sr_s, sg_s, sc_s = _sample_mix(proj_s, ba_s, state_ret[l], state_gdn[l], state_conv[l],
                                               cos_s, sin_s, gdn_conv_w[l], alog_r, dt_r, row(ret_gn_g[l]),
                                               row(gdn_norm_g[l]))
    x1_s, q_s = _merge(xs, proj_s, oa_s.reshape(ns, -1), ob_s.reshape(ns, -1), wa, wb, wo,
                       row(norm_x_g[l]), wq, tm=ns)
    o_s = _xattn_sample(q_s, cache_mem_k[l], cache_mem_v[l])
    x2_s = _resid_mm(x1_s, o_s.reshape(ns, D_MODEL).astype(BF16), wxo)
    y_s = _ffn(x2_s, row(norm_ffn_g[l]), wg, wu, wd, row(norm_final_g), tm=ns)

    return (y_p.reshape(b, t, D_MODEL), y_s.reshape(ns, 1, D_MODEL),
            sr_p[None], sg_p[None], sc_p[None],
            mk_p.reshape(1, b, N_MEM, X_HEADS, X_HD), mv_p.reshape(1, b, N_MEM, X_HEADS, X_HD),
            sr_s[None], sg_s[None], sc_s[None])
```

```python
import functools

import numpy as np
import jax
import jax.numpy as jnp
from jax import lax
from jax.experimental import pallas as pl
from jax.experimental.pallas import tpu as pltpu

F32 = jnp.float32
BF16 = jnp.bfloat16

D_MODEL = 1024
RET_HEADS, RET_DK, RET_DV = 4, 128, 256
GDN_HEADS, GDN_DK, GDN_DV = 8, 128, 128
CONV_W = 4
CONV_CH = 3 * GDN_HEADS * GDN_DK
N_MEM, X_HEADS, X_HD = 256, 4, 256
PAST_LEN = 16384
ROPE_BASE = 10000.0
EPS = 1e-6
GDN_CHUNK = 64

COL_RET = 0
COL_GDN = 3072
COL_GATE = 7168
N_MAIN = 9216
BA_PAD = 128

VMEM_LIMIT = 56 * 1024 * 1024

NT_DIMS = (((1,), (1,)), ((), ()))
TN_DIMS = (((0,), (0,)), ((), ()))


def _mm(a, b):
    return jnp.dot(a.astype(BF16), b.astype(BF16), preferred_element_type=F32)


def _mm_nt(a, b):
    return lax.dot_general(a.astype(BF16), b.astype(BF16), NT_DIMS, preferred_element_type=F32)


def _mm_tn(a, b):
    return lax.dot_general(a.astype(BF16), b.astype(BF16), TN_DIMS, preferred_element_type=F32)


def _mm_f32(a, b):
    return jnp.dot(a, b, preferred_element_type=F32, precision=lax.Precision.HIGHEST)


def _rms(x, g):
    return x * lax.rsqrt(jnp.mean(x * x, axis=-1, keepdims=True) + EPS) * g


def _silu(x):
    return x * jax.nn.sigmoid(x)


def _softplus(x):
    return jnp.maximum(x, 0.0) + jnp.log1p(jnp.exp(-jnp.abs(x)))


def _params(*sem):
    return pltpu.CompilerParams(dimension_semantics=sem, vmem_limit_bytes=VMEM_LIMIT)


def _inproj_body(x_ref, g_ref, w_ref, wba_ref, wbat_ref, o_ref, oba_ref, obat_ref, h_scr):
    @pl.when(pl.program_id(1) == 0)
    def _():
        hb = _rms(x_ref[...], g_ref[...]).astype(BF16)
        h_scr[...] = hb
        oba_ref[...] = jnp.dot(hb, wba_ref[...], preferred_element_type=F32)
        obat_ref[...] = lax.dot_general(wbat_ref[...], hb, NT_DIMS, preferred_element_type=F32)

    o_ref[...] = jnp.dot(h_scr[...], w_ref[...], preferred_element_type=F32)


def _inproj(x, g, w_main, w_ba, w_bat, tm, tn=1024):
    m = x.shape[0]
    return pl.pallas_call(
        _inproj_body,
        grid=(m // tm, N_MAIN // tn),
        in_specs=[
            pl.BlockSpec((tm, D_MODEL), lambda i, j: (i, 0)),
            pl.BlockSpec((1, D_MODEL), lambda i, j: (0, 0)),
            pl.BlockSpec((D_MODEL, tn), lambda i, j: (0, j)),
            pl.BlockSpec((D_MODEL, BA_PAD), lambda i, j: (0, 0)),
            pl.BlockSpec((BA_PAD, D_MODEL), lambda i, j: (0, 0)),
        ],
        out_specs=[
            pl.BlockSpec((tm, tn), lambda i, j: (i, j)),
            pl.BlockSpec((tm, BA_PAD), lambda i, j: (i, 0)),
            pl.BlockSpec((BA_PAD, tm), lambda i, j: (0, i)),
        ],
        out_shape=[
            jax.ShapeDtypeStruct((m, N_MAIN), F32),
            jax.ShapeDtypeStruct((m, BA_PAD), F32),
            jax.ShapeDtypeStruct((BA_PAD, m), F32),
        ],
        scratch_shapes=[pltpu.VMEM((tm, D_MODEL), BF16)],
        compiler_params=_params("parallel", "arbitrary"),
        name="inproj",
    )(x, g, w_main, w_ba, w_bat)


_RET_LOG_G = np.log1p(-np.exp2(-5.0 - np.arange(RET_HEADS, dtype=np.float64)))


def _ret_tables(c):
    idx = np.arange(c, dtype=np.float64)
    diff = idx[:, None] - idx[None, :]
    dmat = np.where(diff >= 0, np.exp(np.maximum(diff, 0.0)[None] * _RET_LOG_G[:, None, None]), 0.0)
    qdec = np.exp((idx + 1.0)[None, :] * _RET_LOG_G[:, None])
    kdec = np.exp((c - 1.0 - idx)[None, :] * _RET_LOG_G[:, None])
    lane = np.ones((1, 1, RET_DK))
    return (jnp.asarray(dmat, F32), jnp.asarray(qdec[:, :, None] * lane, F32),
            jnp.asarray(kdec[:, :, None] * lane, F32), [float(v) for v in np.exp(c * _RET_LOG_G)])


def _rot(x, cos, sin):
    return x * cos + pltpu.roll(x, RET_DK // 2, 1) * sin


def _group_norm_gate(o, gate, gn):
    mu = jnp.mean(o, axis=-1, keepdims=True)
    d = o - mu
    var = jnp.mean(d * d, axis=-1, keepdims=True)
    return _silu(gate) * (d * lax.rsqrt(var + EPS) * gn)


def _ret_body(cdec, q_ref, k_ref, v_ref, g_ref, cos_ref, sin_ref, dmat_ref, qdec_ref, kdec_ref, gn_ref,
              o_ref, s_out_ref, s_scr):
    t = pl.program_id(1)

    @pl.when(t == 0)
    def _():
        s_scr[...] = jnp.zeros_like(s_scr)

    cos, sin = cos_ref[...], sin_ref[...]
    for h in range(RET_HEADS):
        qk = slice(h * RET_DK, (h + 1) * RET_DK)
        vv = slice(h * RET_DV, (h + 1) * RET_DV)
        q = _rot(q_ref[0, :, qk], cos, sin)
        k = _rot(k_ref[0, :, qk], cos, sin) * (RET_DK ** -0.5)
        v = v_ref[0, :, vv]
        s = s_scr[h]
        scores = _mm_nt(q, k) * dmat_ref[h]
        o = _mm(scores, v) + _mm(q * qdec_ref[h], s)
        s_scr[h] = cdec[h] * s + _mm_tn(k * kdec_ref[h], v)
        o_ref[0, :, vv] = _group_norm_gate(o, g_ref[0, :, vv], gn_ref[:, vv]).astype(BF16)

    @pl.when(t == pl.num_programs(1) - 1)
    def _():
        s_out_ref[0] = s_scr[...]


def _retention_prompt(proj3, cos, sin, gn, tb=256):
    b, t, _ = proj3.shape
    dmat, qdec, kdec, cdec = _ret_tables(tb)
    qw = RET_HEADS * RET_DK
    vw = RET_HEADS * RET_DV
    const3 = lambda i, j: (0, 0, 0)
    return pl.pallas_call(
        functools.partial(_ret_body, cdec),
        grid=(b, t // tb),
        in_specs=[
            pl.BlockSpec((1, tb, qw), lambda i, j: (i, j, 0)),
            pl.BlockSpec((1, tb, qw), lambda i, j: (i, j, 1)),
            pl.BlockSpec((1, tb, vw), lambda i, j: (i, j, 1)),
            pl.BlockSpec((1, tb, vw), lambda i, j: (i, j, 2)),
            pl.BlockSpec((tb, RET_DK), lambda i, j: (j, 0)),
            pl.BlockSpec((tb, RET_DK), lambda i, j: (j, 0)),
            pl.BlockSpec((RET_HEADS, tb, tb), const3),
            pl.BlockSpec((RET_HEADS, tb, RET_DK), const3),
            pl.BlockSpec((RET_HEADS, tb, RET_DK), const3),
            pl.BlockSpec((1, vw), lambda i, j: (0, 0)),
        ],
        out_specs=[
            pl.BlockSpec((1, tb, vw), lambda i, j: (i, j, 0)),
            pl.BlockSpec((1, RET_HEADS, RET_DK, RET_DV), lambda i, j: (i, 0, 0, 0)),
        ],
        out_shape=[
            jax.ShapeDtypeStruct((b, t, vw), BF16),
            jax.ShapeDtypeStruct((b, RET_HEADS, RET_DK, RET_DV), F32),
        ],
        scratch_shapes=[pltpu.VMEM((RET_HEADS, RET_DK, RET_DV), F32)],
        compiler_params=_params("parallel", "arbitrary"),
        name="retention_prompt",
    )(proj3, proj3, proj3, proj3, cos, sin, dmat, qdec, kdec, gn)


def _gdn_tables(tb):
    idx = np.arange(tb)
    same = (idx[:, None] // GDN_CHUNK) == (idx[None, :] // GDN_CHUNK)
    lower = same & (idx[:, None] >= idx[None, :])
    nchunk = tb // GDN_CHUNK
    chunk_sel = np.repeat((idx[:, None] // GDN_CHUNK) == np.arange(nchunk)[None, :], 128, axis=1)
    grp = 2 * GDN_CHUNK
    il = (idx % grp)[:, None]
    jl = np.arange(grp)[None, :]
    bias = np.where((il // GDN_CHUNK == jl // GDN_CHUNK) & (il >= jl), 0.0, -1e30)
    eye = (il == jl).astype(np.float64)
    return (jnp.asarray(lower, F32), jnp.asarray(lower.T, F32), jnp.asarray(same, F32),
            jnp.asarray(chunk_sel, F32), jnp.asarray(bias, F32), jnp.asarray(1.0 - eye, F32),
            jnp.asarray(eye, BF16))


def _gdn_body(tb, qkv_ref, z_ref, ba_ref, bat_ref, cw_ref, alog_r_ref, dt_r_ref, alog_c_ref, dt_c_ref,
              ng_ref, lbd_ref, ubd_ref, obd_ref, csel_ref, bias_ref, offd_ref, eye_ref,
              o_ref, s_out_ref, conv_out_ref, s_scr, x_scr, c_scr):
    t = pl.program_id(1)
    nchunk = tb // GDN_CHUNK
    hk = GDN_HEADS * GDN_DK

    @pl.when(t == 0)
    def _():
        s_scr[...] = jnp.zeros_like(s_scr)
        x_scr[0:8, :] = jnp.zeros((8, CONV_CH), F32)

    x_scr[8:8 + tb, :] = qkv_ref[0]
    for c0 in range(0, CONV_CH, GDN_DK):
        cs = slice(c0, c0 + GDN_DK)
        acc = x_scr[5:5 + tb, cs] * cw_ref[0:1, cs]
        for i in range(1, CONV_W):
            acc = acc + x_scr[5 + i:5 + i + tb, cs] * cw_ref[i:i + 1, cs]
        c_scr[:, cs] = _silu(acc)

    @pl.when(t == pl.num_programs(1) - 1)
    def _():
        conv_out_ref[0] = x_scr[tb + 5:tb + 8, :]

    x_scr[0:8, :] = x_scr[tb:tb + 8, :]

    ba = ba_ref[0]
    beta_c = jax.nn.sigmoid(ba)
    g_c = -jnp.exp(alog_r_ref[...]) * _softplus(ba + dt_r_ref[...])
    g_r = -jnp.exp(alog_c_ref[...]) * _softplus(bat_ref[...] + dt_c_ref[...])
    gc_c = _mm_f32(lbd_ref[...], g_c)
    gt_c = _mm_f32(obd_ref[...], g_c)
    gc_r = _mm_f32(g_r, ubd_ref[...])
    gt_l = _mm_f32(g_r, csel_ref[...])

    heads = range(GDN_HEADS)
    grp = 2 * GDN_CHUNK
    groups = [slice(p * grp, (p + 1) * grp) for p in range(tb // grp)]

    def grp_dot(a, b):
        return jnp.concatenate([jnp.dot(a[g], b[g], preferred_element_type=F32) for g in groups], axis=0)

    def grp_dot_nt(a, b):
        return jnp.concatenate([lax.dot_general(a[g], b[g], NT_DIMS, preferred_element_type=F32)
                                for g in groups], axis=0)

    bias = bias_ref[...]
    offdiag = offd_ref[...]
    eye_b = eye_ref[...]
    qs, ks, gammas, pbs, rhss, qgs, khs = [], [], [], [], [], [], []
    for h in heads:
        q = c_scr[:, h * GDN_DK:(h + 1) * GDN_DK]
        k = c_scr[:, hk + h * GDN_DK:hk + (h + 1) * GDN_DK]
        v = c_scr[:, 2 * hk + h * GDN_DV:2 * hk + (h + 1) * GDN_DV]
        q = q * lax.rsqrt(jnp.sum(q * q, axis=-1, keepdims=True) + EPS) * (GDN_DK ** -0.5)
        k = k * lax.rsqrt(jnp.sum(k * k, axis=-1, keepdims=True) + EPS)
        beta = beta_c[:, h:h + 1]
        gcc = gc_c[:, 8 + h:9 + h]
        gtc = gt_c[:, 8 + h:9 + h]
        gcr = gc_r[8 + h:9 + h, :]
        dg = jnp.concatenate([gcc[g] - gcr[:, g] for g in groups], axis=0)
        gamma = jnp.exp(dg + bias)
        kbeta = k * beta
        kb = k.astype(BF16)
        pbs.append((grp_dot_nt((-kbeta).astype(BF16), kb) * (gamma * offdiag)).astype(BF16))
        eg = jnp.exp(gcc)
        rhss.append(jnp.concatenate([v * beta, kbeta * eg], axis=1))
        qs.append(q.astype(BF16))
        ks.append(kb)
        gammas.append(gamma)
        qgs.append(q * eg)
        khs.append(k * jnp.exp(gtc - gcc))

    tbs = [pbs[h] + eye_b for h in heads]
    for lvl in range(5):
        pbs = [grp_dot(pbs[h], pbs[h]).astype(BF16) for h in heads]
        tnew = [grp_dot(tbs[h], pbs[h] + eye_b) for h in heads]
        tbs = [x.astype(BF16) for x in tnew]

    us, ws, qks = [], [], []
    for h in heads:
        uw = rhss[h] + grp_dot(tbs[h] - eye_b, rhss[h].astype(BF16))
        us.append(uw[:, :GDN_DV])
        ws.append(uw[:, GDN_DV:])
        qks.append((grp_dot_nt(qs[h], ks[h]) * gammas[h]).astype(BF16))

    s = [s_scr[h] for h in heads]
    vn_parts = [[] for _ in heads]
    qs_parts = [[] for _ in heads]
    for c in range(nchunk):
        rows = slice(c * GDN_CHUNK, (c + 1) * GDN_CHUNK)
        for h in heads:
            wq = _mm(jnp.concatenate([ws[h][rows], qgs[h][rows]], axis=0), s[h])
            vn = us[h][rows] - wq[:GDN_CHUNK]
            qs_parts[h].append(wq[GDN_CHUNK:])
            vn_parts[h].append(vn)
            decay = jnp.exp(gt_l[8 + h:9 + h, c * 128:(c + 1) * 128])
            s[h] = decay * s[h] + _mm_tn(khs[h][rows], vn)
    for h in heads:
        hs = slice(h * GDN_DV, (h + 1) * GDN_DV)
        s_scr[h] = s[h]
        vn = jnp.concatenate(vn_parts[h], axis=0).astype(BF16)
        o = jnp.concatenate(qs_parts[h], axis=0) + grp_dot(qks[h], vn)
        o_ref[0, :, hs] = (_rms(o, ng_ref[...]) * _silu(z_ref[0, :, hs])).astype(BF16)

    @pl.when(t == pl.num_programs(1) - 1)
    def _():
        s_out_ref[0] = s_scr[...]


def _gdn_prompt(proj3, ba3, bat, conv_w, alog_r, dt_r, alog_c, dt_c, norm_g, tb=256):
    b, t, _ = proj3.shape
    nt = t // tb
    lbd, ubd, obd, csel, bias, offdiag, eye = _gdn_tables(tb)
    vw = GDN_HEADS * GDN_DV
    c2 = lambda i, j: (0, 0)
    return pl.pallas_call(
        functools.partial(_gdn_body, tb),
        grid=(b, nt),
        in_specs=[
            pl.BlockSpec((1, tb, CONV_CH), lambda i, j: (i, j, COL_GDN // CONV_CH)),
            pl.BlockSpec((1, tb, vw), lambda i, j: (i, j, (COL_GDN + CONV_CH) // vw)),
            pl.BlockSpec((1, tb, BA_PAD), lambda i, j: (i, j, 0)),
            pl.BlockSpec((BA_PAD, tb), lambda i, j: (0, i * nt + j)),
            pl.BlockSpec((CONV_W, CONV_CH), c2),
            pl.BlockSpec((1, BA_PAD), c2),
            pl.BlockSpec((1, BA_PAD), c2),
            pl.BlockSpec((BA_PAD, 1), c2),
            pl.BlockSpec((BA_PAD, 1), c2),
            pl.BlockSpec((1, GDN_DV), c2),
            pl.BlockSpec((tb, tb), c2),
            pl.BlockSpec((tb, tb), c2),
            pl.BlockSpec((tb, tb), c2),
            pl.BlockSpec((tb, (tb // GDN_CHUNK) * 128), c2),
            pl.BlockSpec((tb, 2 * GDN_CHUNK), c2),
            pl.BlockSpec((tb, 2 * GDN_CHUNK), c2),
            pl.BlockSpec((tb, 2 * GDN_CHUNK), c2),
        ],
        out_specs=[
            pl.BlockSpec((1, tb, vw), lambda i, j: (i, j, 0)),
            pl.BlockSpec((1, GDN_HEADS, GDN_DK, GDN_DV), lambda i, j: (i, 0, 0, 0)),
            pl.BlockSpec((1, CONV_W - 1, CONV_CH), lambda i, j: (i, 0, 0)),
        ],
        out_shape=[
            jax.ShapeDtypeStruct((b, t, vw), BF16),
            jax.ShapeDtypeStruct((b, GDN_HEADS, GDN_DK, GDN_DV), F32),
            jax.ShapeDtypeStruct((b, CONV_W - 1, CONV_CH), F32),
        ],
        scratch_shapes=[pltpu.VMEM((GDN_HEADS, GDN_DK, GDN_DV), F32), pltpu.VMEM((tb + 8, CONV_CH), F32),
                        pltpu.VMEM((tb, CONV_CH), F32)],
        compiler_params=_params("parallel", "arbitrary"),
        name="gdn_prompt",
    )(proj3, proj3, ba3, bat, conv_w, alog_r, dt_r, alog_c, dt_c, norm_g, lbd, ubd, obd, csel, bias, offdiag, eye)


def _sample_mix_body(cdec, ret_ref, gdn_ref, z_ref, ba_ref, sr_ref, sg_ref, sc_ref, cos_ref, sin_ref, cw_ref,
                     alog_ref, dt_ref, gn_ref, ng_ref,
                     oa_ref, ob_ref, sr_out_ref, sg_out_ref, sc_out_ref):
    hk = GDN_HEADS * GDN_DK
    samples = range(ret_ref.shape[0])
    cos, sin = cos_ref[...], sin_ref[...]
    rets, convs, zs, betas, egs = [], [], [], [], []
    for g in samples:
        ret = ret_ref[g]
        x_new = gdn_ref[g]
        buf = sc_ref[g]
        conv = x_new * cw_ref[CONV_W - 1:CONV_W, :]
        for i in range(CONV_W - 1):
            conv = conv + buf[i:i + 1, :] * cw_ref[i:i + 1, :]
        conv = _silu(conv)
        sc_out_ref[g, 0:2, :] = buf[1:3, :]
        sc_out_ref[g, 2:3, :] = x_new
        ba = ba_ref[g]
        betas.append(jax.nn.sigmoid(ba))
        egs.append(jnp.exp(-jnp.exp(alog_ref[...]) * _softplus(ba + dt_ref[...])))
        rets.append(ret)
        convs.append(conv)
        zs.append(z_ref[g])

    row8 = lax.broadcasted_iota(jnp.int32, (8, RET_DK), 0)

    def pad8(k, q):
        return jnp.where(row8 == 0, k, jnp.where(row8 == 1, q, 0.0)).astype(BF16)

    for h in range(RET_HEADS):
        vv = slice(h * RET_DV, (h + 1) * RET_DV)
        for g in samples:
            ret = rets[g]
            q = _rot(ret[:, h * RET_DK:(h + 1) * RET_DK], cos, sin)
            k = _rot(ret[:, 512 + h * RET_DK:512 + (h + 1) * RET_DK], cos, sin) * (RET_DK ** -0.5)
            v = ret[:, 1024 + h * RET_DV:1024 + (h + 1) * RET_DV]
            s = sr_ref[g, h]
            qs = jnp.dot(pad8(k, q), s.astype(BF16), preferred_element_type=F32)[1:2]
            v8 = jnp.broadcast_to(v, (8, RET_DV)).astype(BF16)
            sr_out_ref[g, h] = cdec[h] * s + lax.dot_general(pad8(k, 0.0), v8, TN_DIMS, preferred_element_type=F32)
            o = cdec[h] * qs + jnp.sum(q * k, axis=-1, keepdims=True) * v
            oa_ref[g, :, vv] = _group_norm_gate(o, ret[:, 2048 + h * RET_DV:2048 + (h + 1) * RET_DV],
                                                gn_ref[:, vv]).astype(BF16)

    for h in range(GDN_HEADS):
        hs = slice(h * GDN_DV, (h + 1) * GDN_DV)
        for g in samples:
            conv = convs[g]
            q = conv[:, h * GDN_DK:(h + 1) * GDN_DK]
            k = conv[:, hk + h * GDN_DK:hk + (h + 1) * GDN_DK]
            q = q * lax.rsqrt(jnp.sum(q * q, axis=-1, keepdims=True) + EPS) * (GDN_DK ** -0.5)
            k = k * lax.rsqrt(jnp.sum(k * k, axis=-1, keepdims=True) + EPS)
            v = conv[:, 2 * hk + h * GDN_DV:2 * hk + (h + 1) * GDN_DV]
            egh = egs[g][:, 8 + h:9 + h]
            s = sg_ref[g, h]
            kq_s = jnp.dot(pad8(k, q), s.astype(BF16), preferred_element_type=F32)
            vn = betas[g][:, h:h + 1] * (v - egh * kq_s[0:1])
            vn8 = jnp.broadcast_to(vn, (8, GDN_DV)).astype(BF16)
            sg_out_ref[g, h] = egh * s + lax.dot_general(pad8(k, 0.0), vn8, TN_DIMS, preferred_element_type=F32)
            o = egh * kq_s[1:2] + jnp.sum(q * k, axis=-1, keepdims=True) * vn
            ob_ref[g, :, hs] = (_rms(o, ng_ref[...]) * _silu(zs[g][:, hs])).astype(BF16)


def _sample_mix(proj, ba, state_ret, state_gdn, state_conv, cos, sin, conv_w, alog_r, dt_r, gn, norm_g, gs=4):
    ns = proj.shape[0]
    proj3 = proj.reshape(ns, 1, N_MAIN)
    ba3 = ba.reshape(ns, 1, BA_PAD)
    cdec = [float(v) for v in np.exp(_RET_LOG_G)]
    vw = RET_HEADS * RET_DV
    gw = GDN_HEADS * GDN_DV
    c2 = lambda i: (0, 0)
    return pl.pallas_call(
        functools.partial(_sample_mix_body, cdec),
        grid=(ns // gs,),
        in_specs=[
            pl.BlockSpec((gs, 1, 3072), lambda i: (i, 0, 0)),
            pl.BlockSpec((gs, 1, CONV_CH), lambda i: (i, 0, COL_GDN // CONV_CH)),
            pl.BlockSpec((gs, 1, gw), lambda i: (i, 0, (COL_GDN + CONV_CH) // gw)),
            pl.BlockSpec((gs, 1, BA_PAD), lambda i: (i, 0, 0)),
            pl.BlockSpec((gs, RET_HEADS, RET_DK, RET_DV), lambda i: (i, 0, 0, 0)),
            pl.BlockSpec((gs, GDN_HEADS, GDN_DK, GDN_DV), lambda i: (i, 0, 0, 0)),
            pl.BlockSpec((gs, CONV_W - 1, CONV_CH), lambda i: (i, 0, 0)),
            pl.BlockSpec((1, RET_DK), c2),
            pl.BlockSpec((1, RET_DK), c2),
            pl.BlockSpec((CONV_W, CONV_CH), c2),
            pl.BlockSpec((1, BA_PAD), c2),
            pl.BlockSpec((1, BA_PAD), c2),
            pl.BlockSpec((1, vw), c2),
            pl.BlockSpec((1, GDN_DV), c2),
        ],
        out_specs=[
            pl.BlockSpec((gs, 1, vw), lambda i: (i, 0, 0)),
            pl.BlockSpec((gs, 1, gw), lambda i: (i, 0, 0)),
            pl.BlockSpec((gs, RET_HEADS, RET_DK, RET_DV), lambda i: (i, 0, 0, 0)),
            pl.BlockSpec((gs, GDN_HEADS, GDN_DK, GDN_DV), lambda i: (i, 0, 0, 0)),
            pl.BlockSpec((gs, CONV_W - 1, CONV_CH), lambda i: (i, 0, 0)),
        ],
        out_shape=[
            jax.ShapeDtypeStruct((ns, 1, vw), BF16),
            jax.ShapeDtypeStruct((ns, 1, gw), BF16),
            jax.ShapeDtypeStruct(state_ret.shape, F32),
            jax.ShapeDtypeStruct(state_gdn.shape, F32),
            jax.ShapeDtypeStruct(state_conv.shape, F32),
        ],
        compiler_params=_params("parallel"),
        name="sample_mix",
    )(proj3, proj3, proj3, ba3, state_ret, state_gdn, state_conv, cos, sin, conv_w, alog_r, dt_r, gn, norm_g)


def _merge_body(x_ref, ga_ref, gb_ref, oa_ref, ob_ref, wa_ref, wb_ref, wo_ref, ng_ref, wq_ref, x1_ref, q_ref):
    ya = jnp.dot(oa_ref[...], wa_ref[...], preferred_element_type=F32)
    yb = jnp.dot(ob_ref[...], wb_ref[...], preferred_element_type=F32)
    merged = jax.nn.sigmoid(ga_ref[...]) * ya + jax.nn.sigmoid(gb_ref[...]) * yb
    x1 = x_ref[...] + _mm(merged, wo_ref[...])
    x1_ref[...] = x1
    q_ref[...] = _mm(_rms(x1, ng_ref[...]), wq_ref[...]).astype(BF16)


def _merge(x, proj, oa, ob, wa, wb, wo, ng, wq, tm):
    m = x.shape[0]
    row = lambda i: (i, 0)
    c2 = lambda i: (0, 0)
    wspec = pl.BlockSpec((D_MODEL, D_MODEL), c2)
    return pl.pallas_call(
        _merge_body,
        grid=(m // tm,),
        in_specs=[
            pl.BlockSpec((tm, D_MODEL), row),
            pl.BlockSpec((tm, D_MODEL), lambda i: (i, COL_GATE // D_MODEL)),
            pl.BlockSpec((tm, D_MODEL), lambda i: (i, COL_GATE // D_MODEL + 1)),
            pl.BlockSpec((tm, D_MODEL), row),
            pl.BlockSpec((tm, D_MODEL), row),
            wspec, wspec, wspec,
            pl.BlockSpec((1, D_MODEL), c2),
            wspec,
        ],
        out_specs=[pl.BlockSpec((tm, D_MODEL), row), pl.BlockSpec((tm, D_MODEL), row)],
        out_shape=[jax.ShapeDtypeStruct((m, D_MODEL), F32), jax.ShapeDtypeStruct((m, D_MODEL), BF16)],
        compiler_params=_params("parallel"),
        name="merge",
    )(x, proj, proj, oa, ob, wa, wb, wo, ng, wq)


def _memkv_body(m_ref, g_ref, wk_ref, wv_ref, k_ref, v_ref):
    mn = _rms(m_ref[...], g_ref[...]).astype(BF16)
    k_ref[...] = jnp.dot(mn, wk_ref[...], preferred_element_type=F32)
    v_ref[...] = jnp.dot(mn, wv_ref[...], preferred_element_type=F32)


def _memkv(mem, g, wk, wv, tm=512):
    m = mem.shape[0]
    row = lambda i: (i, 0)
    c2 = lambda i: (0, 0)
    return pl.pallas_call(
        _memkv_body,
        grid=(m // tm,),
        in_specs=[pl.BlockSpec((tm, D_MODEL), row), pl.BlockSpec((1, D_MODEL), c2),
                  pl.BlockSpec((D_MODEL, D_MODEL), c2), pl.BlockSpec((D_MODEL, D_MODEL), c2)],
        out_specs=[pl.BlockSpec((tm, D_MODEL), row), pl.BlockSpec((tm, D_MODEL), row)],
        out_shape=[jax.ShapeDtypeStruct((m, D_MODEL), F32)] * 2,
        compiler_params=_params("parallel"),
        name="memkv",
    )(mem, g, wk, wv)


def _xattn_body(q_ref, mk_ref, mv_ref, x1_ref, wo_ref, x2_ref):
    parts = []
    for h in range(X_HEADS):
        hs = slice(h * X_HD, (h + 1) * X_HD)
        s = _mm_nt(q_ref[0, :, hs], mk_ref[0, :, hs]) * (X_HD ** -0.5)
        p = jnp.exp(s - jnp.max(s, axis=-1, keepdims=True))
        o = _mm(p, mv_ref[0, :, hs]) / jnp.sum(p, axis=-1, keepdims=True)
        parts.append(o.astype(BF16))
    x2_ref[0] = x1_ref[0] + jnp.dot(jnp.concatenate(parts, axis=1), wo_ref[...], preferred_element_type=F32)


def _xattn_prompt(q3, mk3, mv3, x13, wo, tq=512):
    b, t, _ = q3.shape
    tok = lambda i, j: (i, j, 0)
    mem = lambda i, j: (i, 0, 0)
    return pl.pallas_call(
        _xattn_body,
        grid=(b, t // tq),
        in_specs=[pl.BlockSpec((1, tq, D_MODEL), tok), pl.BlockSpec((1, N_MEM, D_MODEL), mem),
                  pl.BlockSpec((1, N_MEM, D_MODEL), mem), pl.BlockSpec((1, tq, D_MODEL), tok),
                  pl.BlockSpec((D_MODEL, D_MODEL), lambda i, j: (0, 0))],
        out_specs=pl.BlockSpec((1, tq, D_MODEL), tok),
        out_shape=jax.ShapeDtypeStruct((b, t, D_MODEL), F32),
        compiler_params=_params("parallel", "parallel"),
        name="xattn_prompt",
    )(q3, mk3, mv3, x13, wo)


def _xattn_sample_body(q_ref, mk_ref, mv_ref, o_ref):
    for g in range(q_ref.shape[0]):
        q = q_ref[g]
        s = jnp.sum(mk_ref[g] * q[None], axis=-1, keepdims=True) * (X_HD ** -0.5)
        p = jnp.exp(s - jnp.max(s, axis=0, keepdims=True))
        o_ref[g] = jnp.sum(p * mv_ref[g], axis=0) / jnp.sum(p, axis=0)


def _xattn_sample(q, mk4, mv4, gs=4):
    ns = q.shape[0]
    q3 = q.astype(F32).reshape(ns, X_HEADS, X_HD)
    row = lambda i: (i, 0, 0)
    mem = lambda i: (i, 0, 0, 0)
    return pl.pallas_call(
        _xattn_sample_body,
        grid=(ns // gs,),
        in_specs=[pl.BlockSpec((gs, X_HEADS, X_HD), row), pl.BlockSpec((gs, N_MEM, X_HEADS, X_HD), mem),
                  pl.BlockSpec((gs, N_MEM, X_HEADS, X_HD), mem)],
        out_specs=pl.BlockSpec((gs, X_HEADS, X_HD), row),
        out_shape=jax.ShapeDtypeStruct((ns, X_HEADS, X_HD), F32),
        compiler_params=_params("parallel"),
        name="xattn_sample",
    )(q3, mk4, mv4)


def _resid_mm_body(x_ref, a_ref, w_ref, o_ref):
    o_ref[...] = x_ref[...] + jnp.dot(a_ref[...], w_ref[...], preferred_element_type=F32)


def _resid_mm(x, a, w):
    m = x.shape[0]
    return pl.pallas_call(
        _resid_mm_body,
        out_shape=jax.ShapeDtypeStruct((m, D_MODEL), F32),
        compiler_params=pltpu.CompilerParams(vmem_limit_bytes=VMEM_LIMIT),
        name="resid_mm",
    )(x, a, w)


def _ffn_body(x_ref, ng_ref, wg_ref, wu_ref, wd_ref, nf_ref, y_ref):
    x = x_ref[...]
    h = _rms(x, ng_ref[...]).astype(BF16)
    gate = jnp.dot(h, wg_ref[...], preferred_element_type=F32)
    up = jnp.dot(h, wu_ref[...], preferred_element_type=F32)
    x3 = x + _mm(_silu(gate) * up, wd_ref[...])
    y_ref[...] = _rms(x3, nf_ref[...])


def _ffn(x, ng, wg, wu, wd, nf, tm):
    m = x.shape[0]
    dff = wg.shape[1]
    row = lambda i: (i, 0)
    c2 = lambda i: (0, 0)
    single = pl.Buffered(1)
    return pl.pallas_call(
        _ffn_body,
        grid=(m // tm,),
        in_specs=[pl.BlockSpec((tm, D_MODEL), row), pl.BlockSpec((1, D_MODEL), c2),
                  pl.BlockSpec((D_MODEL, dff), c2, pipeline_mode=single),
                  pl.BlockSpec((D_MODEL, dff), c2, pipeline_mode=single),
                  pl.BlockSpec((dff, D_MODEL), c2, pipeline_mode=single),
                  pl.BlockSpec((1, D_MODEL), c2)],
        out_specs=pl.BlockSpec((tm, D_MODEL), row),
        out_shape=jax.ShapeDtypeStruct((m, D_MODEL), F32),
        compiler_params=_params("parallel"),
        name="ffn",
    )(x, ng, wg, wu, wd, nf)


def _rope_tables(pos):
    half = RET_DK // 2
    inv = ROPE_BASE ** (-jnp.arange(half, dtype=F32) / half)
    ang = pos.astype(F32)[:, None] * inv[None, :]
    cos, sin = jnp.cos(ang), jnp.sin(ang)
    return jnp.concatenate([cos, cos], axis=-1), jnp.concatenate([-sin, sin], axis=-1)


def _pad_lanes(v, offset):
    return jnp.zeros((BA_PAD,), F32).at[offset:offset + v.shape[0]].set(v)


def kernel(x_prompt, x_sample, state_ret, state_gdn, state_conv, cache_mem_k, cache_mem_v, mem_prompt,
           norm_mix_g, w_in, ret_gn_g, w_branch_a, gdn_conv_w, gdn_a_log, gdn_dt_bias, gdn_norm_g,
           w_branch_b, w_out, norm_x_g, mem_norm_g, w_xq, w_xk, w_xv, w_xo, norm_ffn_g, w_gate, w_up,
           w_down, norm_final_g):
    depth = w_in.shape[0]
    assert depth == 1, "single-layer kernel"
    b, t, _ = x_prompt.shape
    ns = x_sample.shape[0]
    l = 0

    w = w_in[l]
    ba0 = 7168
    w_main = jnp.concatenate([w[:, :ba0], w[:, ba0 + 2 * GDN_HEADS:]], axis=1).astype(BF16)
    w_ba = jnp.pad(w[:, ba0:ba0 + 2 * GDN_HEADS], ((0, 0), (0, BA_PAD - 2 * GDN_HEADS))).astype(BF16)
    w_bat = w_ba.T
    row = lambda v: v.reshape(1, -1)
    wa, wb, wo = w_branch_a[l].astype(BF16), w_branch_b[l].astype(BF16), w_out[l].astype(BF16)
    wq, wk, wv, wxo = w_xq[l].astype(BF16), w_xk[l].astype(BF16), w_xv[l].astype(BF16), w_xo[l].astype(BF16)
    wg, wu, wd = w_gate[l].astype(BF16), w_up[l].astype(BF16), w_down[l].astype(BF16)
    alog_r = _pad_lanes(gdn_a_log[l], GDN_HEADS).reshape(1, BA_PAD)
    dt_r = _pad_lanes(gdn_dt_bias[l], GDN_HEADS).reshape(1, BA_PAD)
    alog_c, dt_c = alog_r.reshape(BA_PAD, 1), dt_r.reshape(BA_PAD, 1)
    cos_p, sin_p = _rope_tables(jnp.arange(t))
    cos_s, sin_s = _rope_tables(PAST_LEN + jnp.arange(1))

    xp = x_prompt.reshape(b * t, D_MODEL)
    proj_p, ba_p, bat_p = _inproj(xp, row(norm_mix_g[l]), w_main, w_ba, w_bat, tm=1024)
    proj_p3 = proj_p.reshape(b, t, N_MAIN)
    oa_p, sr_p = _retention_prompt(proj_p3, cos_p, sin_p, row(ret_gn_g[l]))
    ob_p, sg_p, sc_p = _gdn_prompt(proj_p3, ba_p.reshape(b, t, BA_PAD), bat_p, gdn_conv_w[l], alog_r, dt_r,
                                   alog_c, dt_c, row(gdn_norm_g[l]))
    x1_p, q_p = _merge(xp, proj_p, oa_p.reshape(b * t, -1), ob_p.reshape(b * t, -1), wa, wb, wo,
                       row(norm_x_g[l]), wq, tm=512)
    mk_p, mv_p = _memkv(mem_prompt.reshape(b * N_MEM, D_MODEL), row(mem_norm_g[l]), wk, wv)
    x2_p = _xattn_prompt(q_p.reshape(b, t, D_MODEL), mk_p.reshape(b, N_MEM, D_MODEL),
                         mv_p.reshape(b, N_MEM, D_MODEL), x1_p.reshape(b, t, D_MODEL), wxo)
    y_p = _ffn(x2_p.reshape(b * t, D_MODEL), row(norm_ffn_g[l]), wg, wu, wd, row(norm_final_g), tm=256)

    xs = x_sample.reshape(ns, D_MODEL)
    proj_s, ba_s, _ = _inproj(xs, row(norm_mix_g[l]), w_main, w_ba, w_bat, tm=ns)
    oa_s, ob_s, sr_s, sg_s, sc_s = _sample_mix(proj_s, ba_s, state_ret[l], state_gdn[l], state_conv[l],
                                               cos_s, sin_s, gdn_conv_w[l], alog_r, dt_r, row(ret_gn_g[l]),
                                               row(gdn_norm_g[l]))
    x1_s, q_s = _merge(xs, proj_s, oa_s.reshape(ns, -1), ob_s.reshape(ns, -1), wa, wb, wo,
                       row(norm_x_g[l]), wq, tm=ns)
    o_s = _xattn_sample(q_s, cache_mem_k[l], cache_mem_v[l])
    x2_s = _resid_mm(x1_s, o_s.reshape(ns, D_MODEL).astype(BF16), wxo)
    y_s = _ffn(x2_s, row(norm_ffn_g[l]), wg, wu, wd, row(norm_final_g), tm=ns)

    return (y_p.reshape(b, t, D_MODEL), y_s.reshape(ns, 1, D_MODEL),
            sr_p[None], sg_p[None], sc_p[None],
            mk_p.reshape(1, b, N_MEM, X_HEADS, X_HD), mv_p.reshape(1, b, N_MEM, X_HEADS, X_HD),
            sr_s[None], sg_s[None], sc_s[None])
```

```python
import functools

import numpy as np
import jax
import jax.numpy as jnp
from jax import lax
from jax.experimental import pallas as pl
from jax.experimental.pallas import tpu as pltpu

F32 = jnp.float32
BF16 = jnp.bfloat16

D_MODEL = 1024
RET_HEADS, RET_DK, RET_DV = 4, 128, 256
GDN_HEADS, GDN_DK, GDN_DV = 8, 128, 128
CONV_W = 4
CONV_CH = 3 * GDN_HEADS * GDN_DK
N_MEM, X_HEADS, X_HD = 256, 4, 256
PAST_LEN = 16384
ROPE_BASE = 10000.0
EPS = 1e-6
GDN_CHUNK = 64

COL_RET = 0
COL_GDN = 3072
COL_GATE = 7168
N_MAIN = 9216
BA_PAD = 128

VMEM_LIMIT = 56 * 1024 * 1024

NT_DIMS = (((1,), (1,)), ((), ()))
TN_DIMS = (((0,), (0,)), ((), ()))


def _mm(a, b):
    return jnp.dot(a.astype(BF16), b.astype(BF16), preferred_element_type=F32)


def _mm_nt(a, b):
    return lax.dot_general(a.astype(BF16), b.astype(BF16), NT_DIMS, preferred_element_type=F32)


def _mm_tn(a, b):
    return lax.dot_general(a.astype(BF16), b.astype(BF16), TN_DIMS, preferred_element_type=F32)


def _mm_f32(a, b):
    return jnp.dot(a, b, preferred_element_type=F32, precision=lax.Precision.HIGHEST)


def _rms(x, g):
    return x * lax.rsqrt(jnp.mean(x * x, axis=-1, keepdims=True) + EPS) * g


def _silu(x):
    return x * jax.nn.sigmoid(x)


def _softplus(x):
    return jnp.maximum(x, 0.0) + jnp.log1p(jnp.exp(-jnp.abs(x)))


def _params(*sem):
    return pltpu.CompilerParams(dimension_semantics=sem, vmem_limit_bytes=VMEM_LIMIT)


def _inproj_body(with_ba, x_ref, g_ref, w_ref, *rest):
    if with_ba:
        wba_ref, o_ref, oba_ref, h_scr = rest
    else:
        o_ref, h_scr = rest

    @pl.when(pl.program_id(1) == 0)
    def _():
        hb = _rms(x_ref[...], g_ref[...]).astype(BF16)
        h_scr[...] = hb
        if with_ba:
            oba_ref[...] = jnp.dot(hb, wba_ref[...], preferred_element_type=F32)

    o_ref[...] = jnp.dot(h_scr[...], w_ref[...], preferred_element_type=F32)


def _inproj(x, g, w_main, tm, w_ba=None, tn=1024):
    m = x.shape[0]
    n = w_main.shape[1]
    in_specs = [
        pl.BlockSpec((tm, D_MODEL), lambda i, j: (i, 0)),
        pl.BlockSpec((1, D_MODEL), lambda i, j: (0, 0)),
        pl.BlockSpec((D_MODEL, tn), lambda i, j: (0, j)),
    ]
    out_specs = [pl.BlockSpec((tm, tn), lambda i, j: (i, j))]
    out_shape = [jax.ShapeDtypeStruct((m, n), F32)]
    args = [x, g, w_main]
    if w_ba is not None:
        in_specs.append(pl.BlockSpec((D_MODEL, BA_PAD), lambda i, j: (0, 0)))
        out_specs.append(pl.BlockSpec((tm, BA_PAD), lambda i, j: (i, 0)))
        out_shape.append(jax.ShapeDtypeStruct((m, BA_PAD), F32))
        args.append(w_ba)
    return pl.pallas_call(
        functools.partial(_inproj_body, w_ba is not None),
        grid=(m // tm, n // tn),
        in_specs=in_specs,
        out_specs=out_specs,
        out_shape=out_shape,
        scratch_shapes=[pltpu.VMEM((tm, D_MODEL), BF16)],
        compiler_params=_params("parallel", "arbitrary"),
        name="inproj",
    )(*args)


_RET_LOG_G = np.log1p(-np.exp2(-5.0 - np.arange(RET_HEADS, dtype=np.float64)))


def _ret_tables(c):
    idx = np.arange(c, dtype=np.float64)
    diff = idx[:, None] - idx[None, :]
    dmat = np.where(diff >= 0, np.exp(np.maximum(diff, 0.0)[None] * _RET_LOG_G[:, None, None]), 0.0)
    qdec = np.exp((idx + 1.0)[None, :] * _RET_LOG_G[:, None])
    kdec = np.exp((c - 1.0 - idx)[None, :] * _RET_LOG_G[:, None])
    lane = np.ones((1, 1, RET_DK))
    return (jnp.asarray(dmat, F32), jnp.asarray(qdec[:, :, None] * lane, F32),
            jnp.asarray(kdec[:, :, None] * lane, F32), [float(v) for v in np.exp(c * _RET_LOG_G)])


def _rot(x, cos, sin):
    return x * cos + pltpu.roll(x, RET_DK // 2, 1) * sin


def _group_norm_gate(o, gate, gn):
    mu = jnp.mean(o, axis=-1, keepdims=True)
    d = o - mu
    var = jnp.mean(d * d, axis=-1, keepdims=True)
    return _silu(gate) * (d * lax.rsqrt(var + EPS) * gn)


def _ret_body(cdec, q_ref, k_ref, v_ref, g_ref, cos_ref, sin_ref, dmat_ref, qdec_ref, kdec_ref, gn_ref,
              o_ref, s_out_ref, s_scr):
    t = pl.program_id(1)

    @pl.when(t == 0)
    def _():
        s_scr[...] = jnp.zeros_like(s_scr)

    cos, sin = cos_ref[...], sin_ref[...]
    for h in range(RET_HEADS):
        qk = slice(h * RET_DK, (h + 1) * RET_DK)
        vv = slice(h * RET_DV, (h + 1) * RET_DV)
        q = _rot(q_ref[0, :, qk], cos, sin)
        k = _rot(k_ref[0, :, qk], cos, sin) * (RET_DK ** -0.5)
        v = v_ref[0, :, vv]
        s = s_scr[h]
        scores = _mm_nt(q, k) * dmat_ref[h]
        o = _mm(scores, v) + _mm(q * qdec_ref[h], s)
        s_scr[h] = cdec[h] * s + _mm_tn(k * kdec_ref[h], v)
        o_ref[0, :, vv] = _group_norm_gate(o, g_ref[0, :, vv], gn_ref[:, vv]).astype(BF16)

    @pl.when(t == pl.num_programs(1) - 1)
    def _():
        s_out_ref[0] = s_scr[...]


def _retention_prompt(proj3, cos, sin, gn, tb=256):
    b, t, _ = proj3.shape
    dmat, qdec, kdec, cdec = _ret_tables(tb)
    qw = RET_HEADS * RET_DK
    vw = RET_HEADS * RET_DV
    const3 = lambda i, j: (0, 0, 0)
    return pl.pallas_call(
        functools.partial(_ret_body, cdec),
        grid=(b, t // tb),
        in_specs=[
            pl.BlockSpec((1, tb, qw), lambda i, j: (i, j, 0)),
            pl.BlockSpec((1, tb, qw), lambda i, j: (i, j, 1)),
            pl.BlockSpec((1, tb, vw), lambda i, j: (i, j, 1)),
            pl.BlockSpec((1, tb, vw), lambda i, j: (i, j, 2)),
            pl.BlockSpec((tb, RET_DK), lambda i, j: (j, 0)),
            pl.BlockSpec((tb, RET_DK), lambda i, j: (j, 0)),
            pl.BlockSpec((RET_HEADS, tb, tb), const3),
            pl.BlockSpec((RET_HEADS, tb, RET_DK), const3),
            pl.BlockSpec((RET_HEADS, tb, RET_DK), const3),
            pl.BlockSpec((1, vw), lambda i, j: (0, 0)),
        ],
        out_specs=[
            pl.BlockSpec((1, tb, vw), lambda i, j: (i, j, 0)),
            pl.BlockSpec((1, RET_HEADS, RET_DK, RET_DV), lambda i, j: (i, 0, 0, 0)),
        ],
        out_shape=[
            jax.ShapeDtypeStruct((b, t, vw), BF16),
            jax.ShapeDtypeStruct((b, RET_HEADS, RET_DK, RET_DV), F32),
        ],
        scratch_shapes=[pltpu.VMEM((RET_HEADS, RET_DK, RET_DV), F32)],
        compiler_params=_params("parallel", "arbitrary"),
        name="retention_prompt",
    )(proj3, proj3, proj3, proj3, cos, sin, dmat, qdec, kdec, gn)


def _gdn_tables(tb):
    idx = np.arange(tb)
    same = (idx[:, None] // GDN_CHUNK) == (idx[None, :] // GDN_CHUNK)
    lower = same & (idx[:, None] >= idx[None, :])
    nchunk = tb // GDN_CHUNK
    chunk_sel = np.repeat((idx[:, None] // GDN_CHUNK) == np.arange(nchunk)[None, :], 128, axis=1)
    grp = 2 * GDN_CHUNK
    il = (idx % grp)[:, None]
    jl = np.arange(grp)[None, :]
    bias = np.where((il // GDN_CHUNK == jl // GDN_CHUNK) & (il >= jl), 0.0, -1e30)
    eye = (il == jl).astype(np.float64)
    return (jnp.asarray(lower, F32), jnp.asarray(lower.T, F32), jnp.asarray(same, F32),
            jnp.asarray(chunk_sel, F32), jnp.asarray(bias, F32), jnp.asarray(1.0 - eye, F32),
            jnp.asarray(eye, BF16))


def _gdn_body(tb, x_ref, nmg_ref, wg_ref, wba_ref, wbat_ref, cw_ref, alog_r_ref, dt_r_ref, alog_c_ref, dt_c_ref,
              ng_ref, lbd_ref, ubd_ref, obd_ref, csel_ref, bias_ref, offd_ref, eye_ref,
              o_ref, s_out_ref, conv_out_ref, s_scr, x_scr, c_scr, z_scr):
    t = pl.program_id(1)
    nchunk = tb // GDN_CHUNK
    hk = GDN_HEADS * GDN_DK

    @pl.when(t == 0)
    def _():
        s_scr[...] = jnp.zeros_like(s_scr)
        x_scr[0:8, :] = jnp.zeros((8, CONV_CH), F32)

    hb = _rms(x_ref[0], nmg_ref[...]).astype(BF16)
    pw = 512
    for c0 in range(0, CONV_CH, pw):
        x_scr[8:8 + tb, c0:c0 + pw] = jnp.dot(hb, wg_ref[:, c0:c0 + pw], preferred_element_type=F32)
    z_scr[...] = jnp.dot(hb, wg_ref[:, CONV_CH:], preferred_element_type=F32)
    ba = jnp.dot(hb, wba_ref[...], preferred_element_type=F32)
    bat = lax.dot_general(wbat_ref[...], hb, NT_DIMS, preferred_element_type=F32)

    for c0 in range(0, CONV_CH, GDN_DK):
        cs = slice(c0, c0 + GDN_DK)
        acc = x_scr[5:5 + tb, cs] * cw_ref[0:1, cs]
        for i in range(1, CONV_W):
            acc = acc + x_scr[5 + i:5 + i + tb, cs] * cw_ref[i:i + 1, cs]
        c_scr[:, cs] = _silu(acc)

    @pl.when(t == pl.num_programs(1) - 1)
    def _():
        conv_out_ref[0] = x_scr[tb + 5:tb + 8, :]

    x_scr[0:8, :] = x_scr[tb:tb + 8, :]

    beta_c = jax.nn.sigmoid(ba)
    g_c = -jnp.exp(alog_r_ref[...]) * _softplus(ba + dt_r_ref[...])
    g_r = -jnp.exp(alog_c_ref[...]) * _softplus(bat + dt_c_ref[...])
    gc_c = _mm_f32(lbd_ref[...], g_c)
    gt_c = _mm_f32(obd_ref[...], g_c)
    gc_r = _mm_f32(g_r, ubd_ref[...])
    gt_l = _mm_f32(g_r, csel_ref[...])

    heads = range(GDN_HEADS)
    grp = 2 * GDN_CHUNK
    groups = [slice(p * grp, (p + 1) * grp) for p in range(tb // grp)]

    def grp_dot(a, b):
        return jnp.concatenate([jnp.dot(a[g], b[g], preferred_element_type=F32) for g in groups], axis=0)

    def grp_dot_nt(a, b):
        return jnp.concatenate([lax.dot_general(a[g], b[g], NT_DIMS, preferred_element_type=F32)
                                for g in groups], axis=0)

    bias = bias_ref[...]
    offdiag = offd_ref[...]
    eye_b = eye_ref[...]
    qs, ks, gammas, pbs, rhss, qgs, khs = [], [], [], [], [], [], []
    for h in heads:
        q = c_scr[:, h * GDN_DK:(h + 1) * GDN_DK]
        k = c_scr[:, hk + h * GDN_DK:hk + (h + 1) * GDN_DK]
        v = c_scr[:, 2 * hk + h * GDN_DV:2 * hk + (h + 1) * GDN_DV]
        q = q * lax.rsqrt(jnp.sum(q * q, axis=-1, keepdims=True) + EPS) * (GDN_DK ** -0.5)
        k = k * lax.rsqrt(jnp.sum(k * k, axis=-1, keepdims=True) + EPS)
        beta = beta_c[:, h:h + 1]
        gcc = gc_c[:, 8 + h:9 + h]
        gtc = gt_c[:, 8 + h:9 + h]
        gcr = gc_r[8 + h:9 + h, :]
        dg = jnp.concatenate([gcc[g] - gcr[:, g] for g in groups], axis=0)
        gamma = jnp.exp(dg + bias)
        kbeta = k * beta
        kb = k.astype(BF16)
        pbs.append((grp_dot_nt((-kbeta).astype(BF16), kb) * (gamma * offdiag)).astype(BF16))
        eg = jnp.exp(gcc)
        rhss.append(jnp.concatenate([v * beta, kbeta * eg], axis=1))
        qs.append(q.astype(BF16))
        ks.append(kb)
        gammas.append(gamma)
        qgs.append(q * eg)
        khs.append(k * jnp.exp(gtc - gcc))

    tbs = [pbs[h] + eye_b for h in heads]
    for lvl in range(5):
        pbs = [grp_dot(pbs[h], pbs[h]).astype(BF16) for h in heads]
        tnew = [grp_dot(tbs[h], pbs[h] + eye_b) for h in heads]
        tbs = [x.astype(BF16) for x in tnew]

    us, ws, qks = [], [], []
    for h in heads:
        uw = rhss[h] + grp_dot(tbs[h] - eye_b, rhss[h].astype(BF16))
        us.append(uw[:, :GDN_DV])
        ws.append(uw[:, GDN_DV:])
        qks.append((grp_dot_nt(qs[h], ks[h]) * gammas[h]).astype(BF16))

    s = [s_scr[h] for h in heads]
    vn_parts = [[] for _ in heads]
    qs_parts = [[] for _ in heads]
    for c in range(nchunk):
        rows = slice(c * GDN_CHUNK, (c + 1) * GDN_CHUNK)
        for h in heads:
            wq = _mm(jnp.concatenate([ws[h][rows], qgs[h][rows]], axis=0), s[h])
            vn = us[h][rows] - wq[:GDN_CHUNK]
            qs_parts[h].append(wq[GDN_CHUNK:])
            vn_parts[h].append(vn)
            decay = jnp.exp(gt_l[8 + h:9 + h, c * 128:(c + 1) * 128])
            s[h] = decay * s[h] + _mm_tn(khs[h][rows], vn)
    for h in heads:
        hs = slice(h * GDN_DV, (h + 1) * GDN_DV)
        s_scr[h] = s[h]
        vn = jnp.concatenate(vn_parts[h], axis=0).astype(BF16)
        o = jnp.concatenate(qs_parts[h], axis=0) + grp_dot(qks[h], vn)
        o_ref[0, :, hs] = (_rms(o, ng_ref[...]) * _silu(z_scr[:, hs])).astype(BF16)

    @pl.when(t == pl.num_programs(1) - 1)
    def _():
        s_out_ref[0] = s_scr[...]


def _gdn_prompt(x3, mix_g, w_gdn, w_ba, w_bat, conv_w, alog_r, dt_r, alog_c, dt_c, norm_g, tb=256):
    b, t, _ = x3.shape
    nt = t // tb
    lbd, ubd, obd, csel, bias, offdiag, eye = _gdn_tables(tb)
    vw = GDN_HEADS * GDN_DV
    c2 = lambda i, j: (0, 0)
    single = pl.Buffered(1)
    return pl.pallas_call(
        functools.partial(_gdn_body, tb),
        grid=(b, nt),
        in_specs=[
            pl.BlockSpec((1, tb, D_MODEL), lambda i, j: (i, j, 0)),
            pl.BlockSpec((1, D_MODEL), c2),
            pl.BlockSpec((D_MODEL, CONV_CH + vw), c2, pipeline_mode=single),
            pl.BlockSpec((D_MODEL, BA_PAD), c2),
            pl.BlockSpec((BA_PAD, D_MODEL), c2),
            pl.BlockSpec((CONV_W, CONV_CH), c2),
            pl.BlockSpec((1, BA_PAD), c2),
            pl.BlockSpec((1, BA_PAD), c2),
            pl.BlockSpec((BA_PAD, 1), c2),
            pl.BlockSpec((BA_PAD, 1), c2),
            pl.BlockSpec((1, GDN_DV), c2),
            pl.BlockSpec((tb, tb), c2),
            pl.BlockSpec((tb, tb), c2),
            pl.BlockSpec((tb, tb), c2),
            pl.BlockSpec((tb, (tb // GDN_CHUNK) * 128), c2),
            pl.BlockSpec((tb, 2 * GDN_CHUNK), c2),
            pl.BlockSpec((tb, 2 * GDN_CHUNK), c2),
            pl.BlockSpec((tb, 2 * GDN_CHUNK), c2),
        ],
        out_specs=[
            pl.BlockSpec((1, tb, vw), lambda i, j: (i, j, 0)),
            pl.BlockSpec((1, GDN_HEADS, GDN_DK, GDN_DV), lambda i, j: (i, 0, 0, 0)),
            pl.BlockSpec((1, CONV_W - 1, CONV_CH), lambda i, j: (i, 0, 0)),
        ],
        out_shape=[
            jax.ShapeDtypeStruct((b, t, vw), BF16),
            jax.ShapeDtypeStruct((b, GDN_HEADS, GDN_DK, GDN_DV), F32),
            jax.ShapeDtypeStruct((b, CONV_W - 1, CONV_CH), F32),
        ],
        scratch_shapes=[pltpu.VMEM((GDN_HEADS, GDN_DK, GDN_DV), F32), pltpu.VMEM((tb + 8, CONV_CH), F32),
                        pltpu.VMEM((tb, CONV_CH), F32), pltpu.VMEM((tb, vw), F32)],
        compiler_params=_params("parallel", "arbitrary"),
        name="gdn_prompt",
    )(x3, mix_g, w_gdn, w_ba, w_bat, conv_w, alog_r, dt_r, alog_c, dt_c, norm_g, lbd, ubd, obd, csel, bias,
      offdiag, eye)


def _sample_mix_body(cdec, ret_ref, gdn_ref, z_ref, ba_ref, sr_ref, sg_ref, sc_ref, cos_ref, sin_ref, cw_ref,
                     alog_ref, dt_ref, gn_ref, ng_ref,
                     oa_ref, ob_ref, sr_out_ref, sg_out_ref, sc_out_ref):
    hk = GDN_HEADS * GDN_DK
    samples = range(ret_ref.shape[0])
    cos, sin = cos_ref[...], sin_ref[...]
    rets, convs, zs, betas, egs = [], [], [], [], []
    for g in samples:
        ret = ret_ref[g]
        x_new = gdn_ref[g]
        buf = sc_ref[g]
        conv = x_new * cw_ref[CONV_W - 1:CONV_W, :]
        for i in range(CONV_W - 1):
            conv = conv + buf[i:i + 1, :] * cw_ref[i:i + 1, :]
        conv = _silu(conv)
        sc_out_ref[g, 0:2, :] = buf[1:3, :]
        sc_out_ref[g, 2:3, :] = x_new
        ba = ba_ref[g]
        betas.append(jax.nn.sigmoid(ba))
        egs.append(jnp.exp(-jnp.exp(alog_ref[...]) * _softplus(ba + dt_ref[...])))
        rets.append(ret)
        convs.append(conv)
        zs.append(z_ref[g])

    row8 = lax.broadcasted_iota(jnp.int32, (8, RET_DK), 0)

    def pad8(k, q):
        return jnp.where(row8 == 0, k, jnp.where(row8 == 1, q, 0.0)).astype(BF16)

    for h in range(RET_HEADS):
        vv = slice(h * RET_DV, (h + 1) * RET_DV)
        for g in samples:
            ret = rets[g]
            q = _rot(ret[:, h * RET_DK:(h + 1) * RET_DK], cos, sin)
            k = _rot(ret[:, 512 + h * RET_DK:512 + (h + 1) * RET_DK], cos, sin) * (RET_DK ** -0.5)
            v = ret[:, 1024 + h * RET_DV:1024 + (h + 1) * RET_DV]
            s = sr_ref[g, h]
            qs = jnp.dot(pad8(k, q), s.astype(BF16), preferred_element_type=F32)[1:2]
            v8 = jnp.broadcast_to(v, (8, RET_DV)).astype(BF16)
            sr_out_ref[g, h] = cdec[h] * s + lax.dot_general(pad8(k, 0.0), v8, TN_DIMS, preferred_element_type=F32)
            o = cdec[h] * qs + jnp.sum(q * k, axis=-1, keepdims=True) * v
            oa_ref[g, :, vv] = _group_norm_gate(o, ret[:, 2048 + h * RET_DV:2048 + (h + 1) * RET_DV],
                                                gn_ref[:, vv]).astype(BF16)

    for h in range(GDN_HEADS):
        hs = slice(h * GDN_DV, (h + 1) * GDN_DV)
        for g in samples:
            conv = convs[g]
            q = conv[:, h * GDN_DK:(h + 1) * GDN_DK]
            k = conv[:, hk + h * GDN_DK:hk + (h + 1) * GDN_DK]
            q = q * lax.rsqrt(jnp.sum(q * q, axis=-1, keepdims=True) + EPS) * (GDN_DK ** -0.5)
            k = k * lax.rsqrt(jnp.sum(k * k, axis=-1, keepdims=True) + EPS)
            v = conv[:, 2 * hk + h * GDN_DV:2 * hk + (h + 1) * GDN_DV]
            egh = egs[g][:, 8 + h:9 + h]
            s = sg_ref[g, h]
            kq_s = jnp.dot(pad8(k, q), s.astype(BF16), preferred_element_type=F32)
            vn = betas[g][:, h:h + 1] * (v - egh * kq_s[0:1])
            vn8 = jnp.broadcast_to(vn, (8, GDN_DV)).astype(BF16)
            sg_out_ref[g, h] = egh * s + lax.dot_general(pad8(k, 0.0), vn8, TN_DIMS, preferred_element_type=F32)
            o = egh * kq_s[1:2] + jnp.sum(q * k, axis=-1, keepdims=True) * vn
            ob_ref[g, :, hs] = (_rms(o, ng_ref[...]) * _silu(zs[g][:, hs])).astype(BF16)


def _sample_mix(proj, ba, state_ret, state_gdn, state_conv, cos, sin, conv_w, alog_r, dt_r, gn, norm_g, gs=4):
    ns = proj.shape[0]
    proj3 = proj.reshape(ns, 1, N_MAIN)
    ba3 = ba.reshape(ns, 1, BA_PAD)
    cdec = [float(v) for v in np.exp(_RET_LOG_G)]
    vw = RET_HEADS * RET_DV
    gw = GDN_HEADS * GDN_DV
    c2 = lambda i: (0, 0)
    return pl.pallas_call(
        functools.partial(_sample_mix_body, cdec),
        grid=(ns // gs,),
        in_specs=[
            pl.BlockSpec((gs, 1, 3072), lambda i: (i, 0, 0)),
            pl.BlockSpec((gs, 1, CONV_CH), lambda i: (i, 0, COL_GDN // CONV_CH)),
            pl.BlockSpec((gs, 1, gw), lambda i: (i, 0, (COL_GDN + CONV_CH) // gw)),
            pl.BlockSpec((gs, 1, BA_PAD), lambda i: (i, 0, 0)),
            pl.BlockSpec((gs, RET_HEADS, RET_DK, RET_DV), lambda i: (i, 0, 0, 0)),
            pl.BlockSpec((gs, GDN_HEADS, GDN_DK, GDN_DV), lambda i: (i, 0, 0, 0)),
            pl.BlockSpec((gs, CONV_W - 1, CONV_CH), lambda i: (i, 0, 0)),
            pl.BlockSpec((1, RET_DK), c2),
            pl.BlockSpec((1, RET_DK), c2),
            pl.BlockSpec((CONV_W, CONV_CH), c2),
            pl.BlockSpec((1, BA_PAD), c2),
            pl.BlockSpec((1, BA_PAD), c2),
            pl.BlockSpec((1, vw), c2),
            pl.BlockSpec((1, GDN_DV), c2),
        ],
        out_specs=[
            pl.BlockSpec((gs, 1, vw), lambda i: (i, 0, 0)),
            pl.BlockSpec((gs, 1, gw), lambda i: (i, 0, 0)),
            pl.BlockSpec((gs, RET_HEADS, RET_DK, RET_DV), lambda i: (i, 0, 0, 0)),
            pl.BlockSpec((gs, GDN_HEADS, GDN_DK, GDN_DV), lambda i: (i, 0, 0, 0)),
            pl.BlockSpec((gs, CONV_W - 1, CONV_CH), lambda i: (i, 0, 0)),
        ],
        out_shape=[
            jax.ShapeDtypeStruct((ns, 1, vw), BF16),
            jax.ShapeDtypeStruct((ns, 1, gw), BF16),
            jax.ShapeDtypeStruct(state_ret.shape, F32),
            jax.ShapeDtypeStruct(state_gdn.shape, F32),
            jax.ShapeDtypeStruct(state_conv.shape, F32),
        ],
        compiler_params=_params("parallel"),
        name="sample_mix",
    )(proj3, proj3, proj3, ba3, state_ret, state_gdn, state_conv, cos, sin, conv_w, alog_r, dt_r, gn, norm_g)


def _merge_body(x_ref, ga_ref, gb_ref, oa_ref, ob_ref, wa_ref, wb_ref, wo_ref, ng_ref, wq_ref, x1_ref, q_ref):
    ya = jnp.dot(oa_ref[...], wa_ref[...], preferred_element_type=F32)
    yb = jnp.dot(ob_ref[...], wb_ref[...], preferred_element_type=F32)
    merged = jax.nn.sigmoid(ga_ref[...]) * ya + jax.nn.sigmoid(gb_ref[...]) * yb
    x1 = x_ref[...] + _mm(merged, wo_ref[...])
    x1_ref[...] = x1
    q_ref[...] = _mm(_rms(x1, ng_ref[...]), wq_ref[...]).astype(BF16)


def _merge(x, proj, gate_col, oa, ob, wa, wb, wo, ng, wq, tm):
    m = x.shape[0]
    row = lambda i: (i, 0)
    c2 = lambda i: (0, 0)
    wspec = pl.BlockSpec((D_MODEL, D_MODEL), c2)
    gblk = gate_col // D_MODEL
    return pl.pallas_call(
        _merge_body,
        grid=(m // tm,),
        in_specs=[
            pl.BlockSpec((tm, D_MODEL), row),
            pl.BlockSpec((tm, D_MODEL), lambda i: (i, gblk)),
            pl.BlockSpec((tm, D_MODEL), lambda i: (i, gblk + 1)),
            pl.BlockSpec((tm, D_MODEL), row),
            pl.BlockSpec((tm, D_MODEL), row),
            wspec, wspec, wspec,
            pl.BlockSpec((1, D_MODEL), c2),
            wspec,
        ],
        out_specs=[pl.BlockSpec((tm, D_MODEL), row), pl.BlockSpec((tm, D_MODEL), row)],
        out_shape=[jax.ShapeDtypeStruct((m, D_MODEL), F32), jax.ShapeDtypeStruct((m, D_MODEL), BF16)],
        compiler_params=_params("parallel"),
        name="merge",
    )(x, proj, proj, oa, ob, wa, wb, wo, ng, wq)


def _memkv_body(m_ref, g_ref, wk_ref, wv_ref, k_ref, v_ref):
    mn = _rms(m_ref[...], g_ref[...]).astype(BF16)
    k_ref[...] = jnp.dot(mn, wk_ref[...], preferred_element_type=F32)
    v_ref[...] = jnp.dot(mn, wv_ref[...], preferred_element_type=F32)


def _memkv(mem, g, wk, wv, tm=512):
    m = mem.shape[0]
    row = lambda i: (i, 0)
    c2 = lambda i: (0, 0)
    return pl.pallas_call(
        _memkv_body,
        grid=(m // tm,),
        in_specs=[pl.BlockSpec((tm, D_MODEL), row), pl.BlockSpec((1, D_MODEL), c2),
                  pl.BlockSpec((D_MODEL, D_MODEL), c2), pl.BlockSpec((D_MODEL, D_MODEL), c2)],
        out_specs=[pl.BlockSpec((tm, D_MODEL), row), pl.BlockSpec((tm, D_MODEL), row)],
        out_shape=[jax.ShapeDtypeStruct((m, D_MODEL), F32)] * 2,
        compiler_params=_params("parallel"),
        name="memkv",
    )(mem, g, wk, wv)


def _xattn_body(q_ref, mk_ref, mv_ref, x1_ref, wo_ref, x2_ref):
    parts = []
    for h in range(X_HEADS):
        hs = slice(h * X_HD, (h + 1) * X_HD)
        s = _mm_nt(q_ref[0, :, hs], mk_ref[0, :, hs]) * (X_HD ** -0.5)
        p = jnp.exp(s - jnp.max(s, axis=-1, keepdims=True))
        o = _mm(p, mv_ref[0, :, hs]) / jnp.sum(p, axis=-1, keepdims=True)
        parts.append(o.astype(BF16))
    x2_ref[0] = x1_ref[0] + jnp.dot(jnp.concatenate(parts, axis=1), wo_ref[...], preferred_element_type=F32)


def _xattn_prompt(q3, mk3, mv3, x13, wo, tq=512):
    b, t, _ = q3.shape
    tok = lambda i, j: (i, j, 0)
    mem = lambda i, j: (i, 0, 0)
    return pl.pallas_call(
        _xattn_body,
        grid=(b, t // tq),
        in_specs=[pl.BlockSpec((1, tq, D_MODEL), tok), pl.BlockSpec((1, N_MEM, D_MODEL), mem),
                  pl.BlockSpec((1, N_MEM, D_MODEL), mem), pl.BlockSpec((1, tq, D_MODEL), tok),
                  pl.BlockSpec((D_MODEL, D_MODEL), lambda i, j: (0, 0))],
        out_specs=pl.BlockSpec((1, tq, D_MODEL), tok),
        out_shape=jax.ShapeDtypeStruct((b, t, D_MODEL), F32),
        compiler_params=_params("parallel", "parallel"),
        name="xattn_prompt",
    )(q3, mk3, mv3, x13, wo)


def _xattn_sample_body(q_ref, mk_ref, mv_ref, o_ref):
    for g in range(q_ref.shape[0]):
        q = q_ref[g]
        s = jnp.sum(mk_ref[g] * q[None], axis=-1, keepdims=True) * (X_HD ** -0.5)
        p = jnp.exp(s - jnp.max(s, axis=0, keepdims=True))
        o_ref[g] = jnp.sum(p * mv_ref[g], axis=0) / jnp.sum(p, axis=0)


def _xattn_sample(q, mk4, mv4, gs=4):
    ns = q.shape[0]
    q3 = q.astype(F32).reshape(ns, X_HEADS, X_HD)
    row = lambda i: (i, 0, 0)
    mem = lambda i: (i, 0, 0, 0)
    return pl.pallas_call(
        _xattn_sample_body,
        grid=(ns // gs,),
        in_specs=[pl.BlockSpec((gs, X_HEADS, X_HD), row), pl.BlockSpec((gs, N_MEM, X_HEADS, X_HD), mem),
                  pl.BlockSpec((gs, N_MEM, X_HEADS, X_HD), mem)],
        out_specs=pl.BlockSpec((gs, X_HEADS, X_HD), row),
        out_shape=jax.ShapeDtypeStruct((ns, X_HEADS, X_HD), F32),
        compiler_params=_params("parallel"),
        name="xattn_sample",
    )(q3, mk4, mv4)


def _resid_mm_body(x_ref, a_ref, w_ref, o_ref):
    o_ref[...] = x_ref[...] + jnp.dot(a_ref[...], w_ref[...], preferred_element_type=F32)


def _resid_mm(x, a, w):
    m = x.shape[0]
    return pl.pallas_call(
        _resid_mm_body,
        out_shape=jax.ShapeDtypeStruct((m, D_MODEL), F32),
        compiler_params=pltpu.CompilerParams(vmem_limit_bytes=VMEM_LIMIT),
        name="resid_mm",
    )(x, a, w)


def _ffn_body(x_ref, ng_ref, wg_ref, wu_ref, wd_ref, nf_ref, y_ref):
    x = x_ref[...]
    h = _rms(x, ng_ref[...]).astype(BF16)
    gate = jnp.dot(h, wg_ref[...], preferred_element_type=F32)
    up = jnp.dot(h, wu_ref[...], preferred_element_type=F32)
    x3 = x + _mm(_silu(gate) * up, wd_ref[...])
    y_ref[...] = _rms(x3, nf_ref[...])


def _ffn(x, ng, wg, wu, wd, nf, tm):
    m = x.shape[0]
    dff = wg.shape[1]
    row = lambda i: (i, 0)
    c2 = lambda i: (0, 0)
    single = pl.Buffered(1)
    return pl.pallas_call(
        _ffn_body,
        grid=(m // tm,),
        in_specs=[pl.BlockSpec((tm, D_MODEL), row), pl.BlockSpec((1, D_MODEL), c2),
                  pl.BlockSpec((D_MODEL, dff), c2, pipeline_mode=single),
                  pl.BlockSpec((D_MODEL, dff), c2, pipeline_mode=single),
                  pl.BlockSpec((dff, D_MODEL), c2, pipeline_mode=single),
                  pl.BlockSpec((1, D_MODEL), c2)],
        out_specs=pl.BlockSpec((tm, D_MODEL), row),
        out_shape=jax.ShapeDtypeStruct((m, D_MODEL), F32),
        compiler_params=_params("parallel"),
        name="ffn",
    )(x, ng, wg, wu, wd, nf)


def _rope_tables(pos):
    half = RET_DK // 2
    inv = ROPE_BASE ** (-jnp.arange(half, dtype=F32) / half)
    ang = pos.astype(F32)[:, None] * inv[None, :]
    cos, sin = jnp.cos(ang), jnp.sin(ang)
    return jnp.concatenate([cos, cos], axis=-1), jnp.concatenate([-sin, sin], axis=-1)


def _pad_lanes(v, offset):
    return jnp.zeros((BA_PAD,), F32).at[offset:offset + v.shape[0]].set(v)


def kernel(x_prompt, x_sample, state_ret, state_gdn, state_conv, cache_mem_k, cache_mem_v, mem_prompt,
           norm_mix_g, w_in, ret_gn_g, w_branch_a, gdn_conv_w, gdn_a_log, gdn_dt_bias, gdn_norm_g,
           w_branch_b, w_out, norm_x_g, mem_norm_g, w_xq, w_xk, w_xv, w_xo, norm_ffn_g, w_gate, w_up,
           w_down, norm_final_g):
    depth = w_in.shape[0]
    assert depth == 1, "single-layer kernel"
    b, t, _ = x_prompt.shape
    ns = x_sample.shape[0]
    l = 0

    w = w_in[l]
    ba0, g0 = COL_GATE, COL_GATE + 2 * GDN_HEADS
    w_ret, w_gdn, w_gates = w[:, :COL_GDN].astype(BF16), w[:, COL_GDN:ba0].astype(BF16), w[:, g0:].astype(BF16)
    w_main_s = jnp.concatenate([w_ret, w_gdn, w_gates], axis=1)
    w_main_p = jnp.concatenate([w_ret, w_gates], axis=1)
    w_ba = jnp.pad(w[:, ba0:g0], ((0, 0), (0, BA_PAD - 2 * GDN_HEADS))).astype(BF16)
    w_bat = w_ba.T
    row = lambda v: v.reshape(1, -1)
    wa, wb, wo = w_branch_a[l].astype(BF16), w_branch_b[l].astype(BF16), w_out[l].astype(BF16)
    wq, wk, wv, wxo = w_xq[l].astype(BF16), w_xk[l].astype(BF16), w_xv[l].astype(BF16), w_xo[l].astype(BF16)
    wg, wu, wd = w_gate[l].astype(BF16), w_up[l].astype(BF16), w_down[l].astype(BF16)
    alog_r = _pad_lanes(gdn_a_log[l], GDN_HEADS).reshape(1, BA_PAD)
    dt_r = _pad_lanes(gdn_dt_bias[l], GDN_HEADS).reshape(1, BA_PAD)
    alog_c, dt_c = alog_r.reshape(BA_PAD, 1), dt_r.reshape(BA_PAD, 1)
    cos_p, sin_p = _rope_tables(jnp.arange(t))
    cos_s, sin_s = _rope_tables(PAST_LEN + jnp.arange(1))

    xp = x_prompt.reshape(b * t, D_MODEL)
    (proj_p,) = _inproj(xp, row(norm_mix_g[l]), w_main_p, tm=1024)
    oa_p, sr_p = _retention_prompt(proj_p.reshape(b, t, -1), cos_p, sin_p, row(ret_gn_g[l]))
    ob_p, sg_p, sc_p = _gdn_prompt(x_prompt, row(norm_mix_g[l]), w_gdn, w_ba, w_bat, gdn_conv_w[l], alog_r, dt_r,
                                   alog_c, dt_c, row(gdn_norm_g[l]))
    x1_p, q_p = _merge(xp, proj_p, COL_GDN, oa_p.reshape(b * t, -1), ob_p.reshape(b * t, -1), wa, wb, wo,
                       row(norm_x_g[l]), wq, tm=512)
    mk_p, mv_p = _memkv(mem_prompt.reshape(b * N_MEM, D_MODEL), row(mem_norm_g[l]), wk, wv)
    x2_p = _xattn_prompt(q_p.reshape(b, t, D_MODEL), mk_p.reshape(b, N_MEM, D_MODEL),
                         mv_p.reshape(b, N_MEM, D_MODEL), x1_p.reshape(b, t, D_MODEL), wxo)
    y_p = _ffn(x2_p.reshape(b * t, D_MODEL), row(norm_ffn_g[l]), wg, wu, wd, row(norm_final_g), tm=256)

    xs = x_sample.reshape(ns, D_MODEL)
    proj_s, ba_s = _inproj(xs, row(norm_mix_g[l]), w_main_s, tm=ns, w_ba=w_ba)
    oa_s, ob_s, sr_s, sg_s, sc_s = _sample_mix(proj_s, ba_s, state_ret[l], state_gdn[l], state_conv[l],
                                               cos_s, sin_s, gdn_conv_w[l], alog_r, dt_r, row(ret_gn_g[l]),
                                               row(gdn_norm_g[l]))
    x1_s, q_s = _merge(xs, proj_s, COL_GATE, oa_s.reshape(ns, -1), ob_s.reshape(ns, -1), wa, wb, wo,
                       row(norm_x_g[l]), wq, tm=ns)
    o_s = _xattn_sample(q_s, cache_mem_k[l], cache_mem_v[l])
    x2_s = _resid_mm(x1_s, o_s.reshape(ns, D_MODEL).astype(BF16), wxo)
    y_s = _ffn(x2_s, row(norm_ffn_g[l]), wg, wu, wd, row(norm_final_g), tm=ns)

    return (y_p.reshape(b, t, D_MODEL), y_s.reshape(ns, 1, D_MODEL),
            sr_p[None], sg_p[None], sc_p[None],
            mk_p.reshape(1, b, N_MEM, X_HEADS, X_HD), mv_p.reshape(1, b, N_MEM, X_HEADS, X_HD),
            sr_s[None], sg_s[None], sc_s[None])
```

```python
import functools

import numpy as np
import jax
import jax.numpy as jnp
from jax import lax
from jax.experimental import pallas as pl
from jax.experimental.pallas import tpu as pltpu

F32 = jnp.float32
BF16 = jnp.bfloat16

D_MODEL = 1024
RET_HEADS, RET_DK, RET_DV = 4, 128, 256
GDN_HEADS, GDN_DK, GDN_DV = 8, 128, 128
CONV_W = 4
CONV_CH = 3 * GDN_HEADS * GDN_DK
N_MEM, X_HEADS, X_HD = 256, 4, 256
PAST_LEN = 16384
ROPE_BASE = 10000.0
EPS = 1e-6
GDN_CHUNK = 64

COL_RET = 0
COL_GDN = 3072
COL_GATE = 7168
BA_PAD = 128

VMEM_LIMIT = 56 * 1024 * 1024

NT_DIMS = (((1,), (1,)), ((), ()))
TN_DIMS = (((0,), (0,)), ((), ()))


def _mm(a, b):
    return jnp.dot(a.astype(BF16), b.astype(BF16), preferred_element_type=F32)


def _mm_nt(a, b):
    return lax.dot_general(a.astype(BF16), b.astype(BF16), NT_DIMS, preferred_element_type=F32)


def _mm_tn(a, b):
    return lax.dot_general(a.astype(BF16), b.astype(BF16), TN_DIMS, preferred_element_type=F32)


def _mm_f32(a, b):
    return jnp.dot(a, b, preferred_element_type=F32, precision=lax.Precision.HIGHEST)


def _rms(x, g):
    return x * lax.rsqrt(jnp.mean(x * x, axis=-1, keepdims=True) + EPS) * g


def _silu(x):
    return x * jax.nn.sigmoid(x)


def _softplus(x):
    return jnp.maximum(x, 0.0) + jnp.log1p(jnp.exp(-jnp.abs(x)))


def _params(*sem):
    return pltpu.CompilerParams(dimension_semantics=sem, vmem_limit_bytes=VMEM_LIMIT)


def _inproj_body(with_ba, x_ref, g_ref, w_ref, *rest):
    if with_ba:
        wba_ref, o_ref, oba_ref, h_scr = rest
    else:
        o_ref, h_scr = rest

    @pl.when(pl.program_id(1) == 0)
    def _():
        hb = _rms(x_ref[...], g_ref[...]).astype(BF16)
        h_scr[...] = hb
        if with_ba:
            oba_ref[...] = jnp.dot(hb, wba_ref[...], preferred_element_type=F32)

    o_ref[...] = jnp.dot(h_scr[...], w_ref[...], preferred_element_type=F32)


def _inproj(x, g, w_main, tm, w_ba=None, tn=1024):
    m = x.shape[0]
    n = w_main.shape[1]
    in_specs = [
        pl.BlockSpec((tm, D_MODEL), lambda i, j: (i, 0)),
        pl.BlockSpec((1, D_MODEL), lambda i, j: (0, 0)),
        pl.BlockSpec((D_MODEL, tn), lambda i, j: (0, j)),
    ]
    out_specs = [pl.BlockSpec((tm, tn), lambda i, j: (i, j))]
    out_shape = [jax.ShapeDtypeStruct((m, n), F32)]
    args = [x, g, w_main]
    if w_ba is not None:
        in_specs.append(pl.BlockSpec((D_MODEL, BA_PAD), lambda i, j: (0, 0)))
        out_specs.append(pl.BlockSpec((tm, BA_PAD), lambda i, j: (i, 0)))
        out_shape.append(jax.ShapeDtypeStruct((m, BA_PAD), F32))
        args.append(w_ba)
    return pl.pallas_call(
        functools.partial(_inproj_body, w_ba is not None),
        grid=(m // tm, n // tn),
        in_specs=in_specs,
        out_specs=out_specs,
        out_shape=out_shape,
        scratch_shapes=[pltpu.VMEM((tm, D_MODEL), BF16)],
        compiler_params=_params("parallel", "arbitrary"),
        name="inproj",
    )(*args)


_RET_LOG_G = np.log1p(-np.exp2(-5.0 - np.arange(RET_HEADS, dtype=np.float64)))


def _ret_tables(c):
    idx = np.arange(c, dtype=np.float64)
    diff = idx[:, None] - idx[None, :]
    dmat = np.where(diff >= 0, np.exp(np.maximum(diff, 0.0)[None] * _RET_LOG_G[:, None, None]), 0.0)
    qdec = np.exp((idx + 1.0)[None, :] * _RET_LOG_G[:, None])
    kdec = np.exp((c - 1.0 - idx)[None, :] * _RET_LOG_G[:, None])
    lane = np.ones((1, 1, RET_DK))
    return (jnp.asarray(dmat, F32), jnp.asarray(qdec[:, :, None] * lane, F32),
            jnp.asarray(kdec[:, :, None] * lane, F32), [float(v) for v in np.exp(c * _RET_LOG_G)])


def _rot(x, cos, sin):
    return x * cos + pltpu.roll(x, RET_DK // 2, 1) * sin


def _group_norm_gate(o, gate, gn):
    mu = jnp.mean(o, axis=-1, keepdims=True)
    d = o - mu
    var = jnp.mean(d * d, axis=-1, keepdims=True)
    return _silu(gate) * (d * lax.rsqrt(var + EPS) * gn)


def _proj_tiles(hb, w_ref, dst, width=512):
    for c0 in range(0, w_ref.shape[1], width):
        dst[:, c0:c0 + width] = jnp.dot(hb, w_ref[:, c0:c0 + width], preferred_element_type=F32)


def _next_block(nt, nblocks):
    def index_map(i, j):
        n1 = jnp.minimum(i * nt + j + 1, nblocks - 1)
        return (n1 // nt, n1 % nt, 0)
    return index_map


def _ret_body(cdec, x0_ref, xn_ref, nmg_ref, w_ref, *rest):
    *consts, o_ref, s_out_ref, s_scr, pa, pb = rest
    t = pl.program_id(1)
    n = pl.program_id(0) * pl.num_programs(1) + t
    bufs = (pa, pb)

    @pl.when(t == 0)
    def _():
        s_scr[...] = jnp.zeros_like(s_scr)

    @pl.when(n == 0)
    def _():
        _proj_tiles(_rms(x0_ref[0], nmg_ref[...]).astype(BF16), w_ref, pa)

    for slot in range(2):
        @pl.when(n % 2 == slot)
        def _(slot=slot):
            _ret_step(cdec, t, xn_ref, nmg_ref, w_ref, bufs[slot], bufs[1 - slot], *consts, o_ref, s_out_ref, s_scr)


def _ret_step(cdec, t, xn_ref, nmg_ref, w_ref, proj, proj_next, cos_ref, sin_ref, dmat_ref, qdec_ref, kdec_ref,
              gn_ref, o_ref, s_out_ref, s_scr):
    qw = RET_HEADS * RET_DK
    _proj_tiles(_rms(xn_ref[0], nmg_ref[...]).astype(BF16), w_ref, proj_next)

    cos, sin = cos_ref[...], sin_ref[...]
    for h in range(RET_HEADS):
        qk = slice(h * RET_DK, (h + 1) * RET_DK)
        kk = slice(qw + h * RET_DK, qw + (h + 1) * RET_DK)
        vv = slice(h * RET_DV, (h + 1) * RET_DV)
        q = _rot(proj[:, qk], cos, sin)
        k = _rot(proj[:, kk], cos, sin) * (RET_DK ** -0.5)
        v = proj[:, 2 * qw + h * RET_DV:2 * qw + (h + 1) * RET_DV]
        gate = proj[:, 2 * qw + (RET_HEADS + h) * RET_DV:2 * qw + (RET_HEADS + h + 1) * RET_DV]
        s = s_scr[h]
        scores = _mm_nt(q, k) * dmat_ref[h]
        o = _mm(scores, v) + _mm(q * qdec_ref[h], s)
        s_scr[h] = cdec[h] * s + _mm_tn(k * kdec_ref[h], v)
        o_ref[0, :, vv] = _group_norm_gate(o, gate, gn_ref[:, vv]).astype(BF16)

    @pl.when(t == pl.num_programs(1) - 1)
    def _():
        s_out_ref[0] = s_scr[...]


def _retention_prompt(x3, mix_g, w_ret, cos, sin, gn, tb=256):
    b, t, _ = x3.shape
    nt = t // tb
    dmat, qdec, kdec, cdec = _ret_tables(tb)
    vw = RET_HEADS * RET_DV
    const3 = lambda i, j: (0, 0, 0)
    return pl.pallas_call(
        functools.partial(_ret_body, cdec),
        grid=(b, nt),
        in_specs=[
            pl.BlockSpec((1, tb, D_MODEL), const3),
            pl.BlockSpec((1, tb, D_MODEL), _next_block(nt, b * nt)),
            pl.BlockSpec((1, D_MODEL), lambda i, j: (0, 0)),
            pl.BlockSpec(w_ret.shape, lambda i, j: (0, 0), pipeline_mode=pl.Buffered(1)),
            pl.BlockSpec((tb, RET_DK), lambda i, j: (j, 0)),
            pl.BlockSpec((tb, RET_DK), lambda i, j: (j, 0)),
            pl.BlockSpec((RET_HEADS, tb, tb), const3),
            pl.BlockSpec((RET_HEADS, tb, RET_DK), const3),
            pl.BlockSpec((RET_HEADS, tb, RET_DK), const3),
            pl.BlockSpec((1, vw), lambda i, j: (0, 0)),
        ],
        out_specs=[
            pl.BlockSpec((1, tb, vw), lambda i, j: (i, j, 0)),
            pl.BlockSpec((1, RET_HEADS, RET_DK, RET_DV), lambda i, j: (i, 0, 0, 0)),
        ],
        out_shape=[
            jax.ShapeDtypeStruct((b, t, vw), BF16),
            jax.ShapeDtypeStruct((b, RET_HEADS, RET_DK, RET_DV), F32),
        ],
        scratch_shapes=[pltpu.VMEM((RET_HEADS, RET_DK, RET_DV), F32)]
        + 2 * [pltpu.VMEM((tb, w_ret.shape[1]), F32)],
        compiler_params=_params("arbitrary", "arbitrary"),
        name="retention_prompt",
    )(x3, x3, mix_g, w_ret, cos, sin, dmat, qdec, kdec, gn)


def _gdn_tables(tb):
    idx = np.arange(tb)
    same = (idx[:, None] // GDN_CHUNK) == (idx[None, :] // GDN_CHUNK)
    lower = same & (idx[:, None] >= idx[None, :])
    nchunk = tb // GDN_CHUNK
    chunk_sel = np.repeat((idx[:, None] // GDN_CHUNK) == np.arange(nchunk)[None, :], 128, axis=1)
    grp = 2 * GDN_CHUNK
    il = (idx % grp)[:, None]
    jl = np.arange(grp)[None, :]
    bias = np.where((il // GDN_CHUNK == jl // GDN_CHUNK) & (il >= jl), 0.0, -1e30)
    eye = (il == jl).astype(np.float64)
    return (jnp.asarray(lower, F32), jnp.asarray(lower.T, F32), jnp.asarray(same, F32),
            jnp.asarray(chunk_sel, F32), jnp.asarray(bias, F32), jnp.asarray(1.0 - eye, F32),
            jnp.asarray(eye, BF16))


def _gdn_project(x_ref, nmg_ref, wqkv_ref, wz_ref, wba_ref, wbat_ref, xs, zs, bas, bats, tb):
    hb = _rms(x_ref[0], nmg_ref[...]).astype(BF16)
    _proj_tiles(hb, wqkv_ref, xs.at[8:8 + tb])
    _proj_tiles(hb, wz_ref, zs)
    bas[...] = jnp.dot(hb, wba_ref[...], preferred_element_type=F32)
    bats[...] = lax.dot_general(wbat_ref[...], hb, NT_DIMS, preferred_element_type=F32)


def _gdn_body(tb, x0_ref, xn_ref, nmg_ref, wqkv_ref, wz_ref, wba_ref, wbat_ref, *rest):
    *consts, o_ref, s_out_ref, conv_out_ref, s_scr, c_scr, xa, za, baa, bata, xb, zb, bab, batb = rest
    t = pl.program_id(1)
    n = pl.program_id(0) * pl.num_programs(1) + t
    weights = (nmg_ref, wqkv_ref, wz_ref, wba_ref, wbat_ref)
    bufs = ((xa, za, baa, bata), (xb, zb, bab, batb))

    @pl.when(t == 0)
    def _():
        s_scr[...] = jnp.zeros_like(s_scr)

    @pl.when(n == 0)
    def _():
        _gdn_project(x0_ref, *weights, *bufs[0], tb)

    for slot in range(2):
        @pl.when(n % 2 == slot)
        def _(slot=slot):
            _gdn_step(tb, t, xn_ref, weights, bufs[slot], bufs[1 - slot], *consts,
                      o_ref, s_out_ref, conv_out_ref, s_scr, c_scr)


def _gdn_step(tb, t, xn_ref, weights, cur, nxt, cw_ref, alog_r_ref, dt_r_ref, alog_c_ref, dt_c_ref, ng_ref,
              lbd_ref, ubd_ref, obd_ref, csel_ref, bias_ref, offd_ref, eye_ref,
              o_ref, s_out_ref, conv_out_ref, s_scr, c_scr):
    nchunk = tb // GDN_CHUNK
    hk = GDN_HEADS * GDN_DK
    xs, zz, ba_ref, bat_ref = cur

    xs[0:8, :] = jnp.where(t == 0, 0.0, nxt[0][tb:tb + 8, :])

    _gdn_project(xn_ref, *weights, *nxt, tb)
    ba, bat = ba_ref[...], bat_ref[...]

    for c0 in range(0, CONV_CH, GDN_DK):
        cs = slice(c0, c0 + GDN_DK)
        acc = xs[5:5 + tb, cs] * cw_ref[0:1, cs]
        for i in range(1, CONV_W):
            acc = acc + xs[5 + i:5 + i + tb, cs] * cw_ref[i:i + 1, cs]
        c_scr[:, cs] = _silu(acc)

    @pl.when(t == pl.num_programs(1) - 1)
    def _():
        conv_out_ref[0] = xs[tb + 5:tb + 8, :]

    beta_c = jax.nn.sigmoid(ba)
    g_c = -jnp.exp(alog_r_ref[...]) * _softplus(ba + dt_r_ref[...])
    g_r = -jnp.exp(alog_c_ref[...]) * _softplus(bat + dt_c_ref[...])
    gc_c = _mm_f32(lbd_ref[...], g_c)
    gt_c = _mm_f32(obd_ref[...], g_c)
    gc_r = _mm_f32(g_r, ubd_ref[...])
    gt_l = _mm_f32(g_r, csel_ref[...])

    heads = range(GDN_HEADS)
    grp = 2 * GDN_CHUNK
    groups = [slice(p * grp, (p + 1) * grp) for p in range(tb // grp)]

    def grp_dot(a, b):
        return jnp.concatenate([jnp.dot(a[g], b[g], preferred_element_type=F32) for g in groups], axis=0)

    def grp_dot_nt(a, b):
        return jnp.concatenate([lax.dot_general(a[g], b[g], NT_DIMS, preferred_element_type=F32)
                                for g in groups], axis=0)

    bias = bias_ref[...]
    offdiag = offd_ref[...]
    eye_b = eye_ref[...]
    qs, ks, gammas, pbs, rhss, qgs, khs = [], [], [], [], [], [], []
    for h in heads:
        q = c_scr[:, h * GDN_DK:(h + 1) * GDN_DK]
        k = c_scr[:, hk + h * GDN_DK:hk + (h + 1) * GDN_DK]
        v = c_scr[:, 2 * hk + h * GDN_DV:2 * hk + (h + 1) * GDN_DV]
        q = q * lax.rsqrt(jnp.sum(q * q, axis=-1, keepdims=True) + EPS) * (GDN_DK ** -0.5)
        k = k * lax.rsqrt(jnp.sum(k * k, axis=-1, keepdims=True) + EPS)
        beta = beta_c[:, h:h + 1]
        gcc = gc_c[:, 8 + h:9 + h]
        gtc = gt_c[:, 8 + h:9 + h]
        gcr = gc_r[8 + h:9 + h, :]
        dg = jnp.concatenate([gcc[g] - gcr[:, g] for g in groups], axis=0)
        gamma = jnp.exp(dg + bias)
        kbeta = k * beta
        kb = k.astype(BF16)
        pbs.append((grp_dot_nt((-kbeta).astype(BF16), kb) * (gamma * offdiag)).astype(BF16))
        eg = jnp.exp(gcc)
        rhss.append(jnp.concatenate([v * beta, kbeta * eg], axis=1))
        qs.append(q.astype(BF16))
        ks.append(kb)
        gammas.append(gamma)
        qgs.append(q * eg)
        khs.append(k * jnp.exp(gtc - gcc))

    tbs = [pbs[h] + eye_b for h in heads]
    for lvl in range(5):
        pbs = [grp_dot(pbs[h], pbs[h]).astype(BF16) for h in heads]
        tnew = [grp_dot(tbs[h], pbs[h] + eye_b) for h in heads]
        tbs = [x.astype(BF16) for x in tnew]

    us, ws, qks = [], [], []
    for h in heads:
        uw = rhss[h] + grp_dot(tbs[h] - eye_b, rhss[h].astype(BF16))
        us.append(uw[:, :GDN_DV])
        ws.append(uw[:, GDN_DV:])
        qks.append((grp_dot_nt(qs[h], ks[h]) * gammas[h]).astype(BF16))

    s = [s_scr[h] for h in heads]
    vn_parts = [[] for _ in heads]
    qs_parts = [[] for _ in heads]
    for c in range(nchunk):
        rows = slice(c * GDN_CHUNK, (c + 1) * GDN_CHUNK)
        for h in heads:
            wq = _mm(jnp.concatenate([ws[h][rows], qgs[h][rows]], axis=0), s[h])
            vn = us[h][rows] - wq[:GDN_CHUNK]
            qs_parts[h].append(wq[GDN_CHUNK:])
            vn_parts[h].append(vn)
            decay = jnp.exp(gt_l[8 + h:9 + h, c * 128:(c + 1) * 128])
            s[h] = decay * s[h] + _mm_tn(khs[h][rows], vn)
    for h in heads:
        hs = slice(h * GDN_DV, (h + 1) * GDN_DV)
        s_scr[h] = s[h]
        vn = jnp.concatenate(vn_parts[h], axis=0).astype(BF16)
        o = jnp.concatenate(qs_parts[h], axis=0) + grp_dot(qks[h], vn)
        o_ref[0, :, hs] = (_rms(o, ng_ref[...]) * _silu(zz[:, hs])).astype(BF16)

    @pl.when(t == pl.num_programs(1) - 1)
    def _():
        s_out_ref[0] = s_scr[...]


def _gdn_prompt(x3, mix_g, w_gdn, w_ba, w_bat, conv_w, alog_r, dt_r, alog_c, dt_c, norm_g, tb=256):
    b, t, _ = x3.shape
    nt = t // tb
    lbd, ubd, obd, csel, bias, offdiag, eye = _gdn_tables(tb)
    vw = GDN_HEADS * GDN_DV
    c2 = lambda i, j: (0, 0)
    single = pl.Buffered(1)
    return pl.pallas_call(
        functools.partial(_gdn_body, tb),
        grid=(b, nt),
        in_specs=[
            pl.BlockSpec((1, tb, D_MODEL), lambda i, j: (0, 0, 0)),
            pl.BlockSpec((1, tb, D_MODEL), _next_block(nt, b * nt)),
            pl.BlockSpec((1, D_MODEL), c2),
            pl.BlockSpec((D_MODEL, CONV_CH), c2, pipeline_mode=single),
            pl.BlockSpec((D_MODEL, vw), lambda i, j: (0, CONV_CH // vw), pipeline_mode=single),
            pl.BlockSpec((D_MODEL, BA_PAD), c2),
            pl.BlockSpec((BA_PAD, D_MODEL), c2),
            pl.BlockSpec((CONV_W, CONV_CH), c2),
            pl.BlockSpec((1, BA_PAD), c2),
            pl.BlockSpec((1, BA_PAD), c2),
            pl.BlockSpec((BA_PAD, 1), c2),
            pl.BlockSpec((BA_PAD, 1), c2),
            pl.BlockSpec((1, GDN_DV), c2),
            pl.BlockSpec((tb, tb), c2),
            pl.BlockSpec((tb, tb), c2),
            pl.BlockSpec((tb, tb), c2),
            pl.BlockSpec((tb, (tb // GDN_CHUNK) * 128), c2),
            pl.BlockSpec((tb, 2 * GDN_CHUNK), c2),
            pl.BlockSpec((tb, 2 * GDN_CHUNK), c2),
            pl.BlockSpec((tb, 2 * GDN_CHUNK), c2),
        ],
        out_specs=[
            pl.BlockSpec((1, tb, vw), lambda i, j: (i, j, 0)),
            pl.BlockSpec((1, GDN_HEADS, GDN_DK, GDN_DV), lambda i, j: (i, 0, 0, 0)),
            pl.BlockSpec((1, CONV_W - 1, CONV_CH), lambda i, j: (i, 0, 0)),
        ],
        out_shape=[
            jax.ShapeDtypeStruct((b, t, vw), BF16),
            jax.ShapeDtypeStruct((b, GDN_HEADS, GDN_DK, GDN_DV), F32),
            jax.ShapeDtypeStruct((b, CONV_W - 1, CONV_CH), F32),
        ],
        scratch_shapes=[pltpu.VMEM((GDN_HEADS, GDN_DK, GDN_DV), F32), pltpu.VMEM((tb, CONV_CH), F32)]
        + 2 * [pltpu.VMEM((tb + 8, CONV_CH), F32), pltpu.VMEM((tb, vw), F32),
               pltpu.VMEM((tb, BA_PAD), F32), pltpu.VMEM((BA_PAD, tb), F32)],
        compiler_params=_params("arbitrary", "arbitrary"),
        name="gdn_prompt",
    )(x3, x3, mix_g, w_gdn, w_gdn, w_ba, w_bat, conv_w, alog_r, dt_r, alog_c, dt_c, norm_g, lbd, ubd, obd, csel,
      bias, offdiag, eye)


def _sample_mix_body(cdec, ret_ref, gdn_ref, z_ref, ba_ref, sr_ref, sg_ref, sc_ref, cos_ref, sin_ref, cw_ref,
                     alog_ref, dt_ref, gn_ref, ng_ref,
                     oa_ref, ob_ref, sr_out_ref, sg_out_ref, sc_out_ref):
    hk = GDN_HEADS * GDN_DK
    samples = range(ret_ref.shape[0])
    cos, sin = cos_ref[...], sin_ref[...]
    rets, convs, zs, betas, egs = [], [], [], [], []
    for g in samples:
        ret = ret_ref[g]
        x_new = gdn_ref[g]
        buf = sc_ref[g]
        conv = x_new * cw_ref[CONV_W - 1:CONV_W, :]
        for i in range(CONV_W - 1):
            conv = conv + buf[i:i + 1, :] * cw_ref[i:i + 1, :]
        conv = _silu(conv)
        sc_out_ref[g, 0:2, :] = buf[1:3, :]
        sc_out_ref[g, 2:3, :] = x_new
        ba = ba_ref[g]
        betas.append(jax.nn.sigmoid(ba))
        egs.append(jnp.exp(-jnp.exp(alog_ref[...]) * _softplus(ba + dt_ref[...])))
        rets.append(ret)
        convs.append(conv)
        zs.append(z_ref[g])

    row8 = lax.broadcasted_iota(jnp.int32, (8, RET_DK), 0)

    def pad8(k, q):
        return jnp.where(row8 == 0, k, jnp.where(row8 == 1, q, 0.0)).astype(BF16)

    for h in range(RET_HEADS):
        vv = slice(h * RET_DV, (h + 1) * RET_DV)
        for g in samples:
            ret = rets[g]
            q = _rot(ret[:, h * RET_DK:(h + 1) * RET_DK], cos, sin)
            k = _rot(ret[:, 512 + h * RET_DK:512 + (h + 1) * RET_DK], cos, sin) * (RET_DK ** -0.5)
            v = ret[:, 1024 + h * RET_DV:1024 + (h + 1) * RET_DV]
            s = sr_ref[g, h]
            qs = jnp.dot(pad8(k, q), s.astype(BF16), preferred_element_type=F32)[1:2]
            v8 = jnp.broadcast_to(v, (8, RET_DV)).astype(BF16)
            sr_out_ref[g, h] = cdec[h] * s + lax.dot_general(pad8(k, 0.0), v8, TN_DIMS, preferred_element_type=F32)
            o = cdec[h] * qs + jnp.sum(q * k, axis=-1, keepdims=True) * v
            oa_ref[g, :, vv] = _group_norm_gate(o, ret[:, 2048 + h * RET_DV:2048 + (h + 1) * RET_DV],
                                                gn_ref[:, vv]).astype(BF16)

    for h in range(GDN_HEADS):
        hs = slice(h * GDN_DV, (h + 1) * GDN_DV)
        for g in samples:
            conv = convs[g]
            q = conv[:, h * GDN_DK:(h + 1) * GDN_DK]
            k = conv[:, hk + h * GDN_DK:hk + (h + 1) * GDN_DK]
            q = q * lax.rsqrt(jnp.sum(q * q, axis=-1, keepdims=True) + EPS) * (GDN_DK ** -0.5)
            k = k * lax.rsqrt(jnp.sum(k * k, axis=-1, keepdims=True) + EPS)
            v = conv[:, 2 * hk + h * GDN_DV:2 * hk + (h + 1) * GDN_DV]
            egh = egs[g][:, 8 + h:9 + h]
            s = sg_ref[g, h]
            kq_s = jnp.dot(pad8(k, q), s.astype(BF16), preferred_element_type=F32)
            vn = betas[g][:, h:h + 1] * (v - egh * kq_s[0:1])
            vn8 = jnp.broadcast_to(vn, (8, GDN_DV)).astype(BF16)
            sg_out_ref[g, h] = egh * s + lax.dot_general(pad8(k, 0.0), vn8, TN_DIMS, preferred_element_type=F32)
            o = egh * kq_s[1:2] + jnp.sum(q * k, axis=-1, keepdims=True) * vn
            ob_ref[g, :, hs] = (_rms(o, ng_ref[...]) * _silu(zs[g][:, hs])).astype(BF16)


def _sample_mix(proj, ba, state_ret, state_gdn, state_conv, cos, sin, conv_w, alog_r, dt_r, gn, norm_g, gs=4):
    ns = proj.shape[0]
    proj3 = proj.reshape(ns, 1, COL_GATE)
    ba3 = ba.reshape(ns, 1, BA_PAD)
    cdec = [float(v) for v in np.exp(_RET_LOG_G)]
    vw = RET_HEADS * RET_DV
    gw = GDN_HEADS * GDN_DV
    c2 = lambda i: (0, 0)
    return pl.pallas_call(
        functools.partial(_sample_mix_body, cdec),
        grid=(ns // gs,),
        in_specs=[
            pl.BlockSpec((gs, 1, 3072), lambda i: (i, 0, 0)),
            pl.BlockSpec((gs, 1, CONV_CH), lambda i: (i, 0, COL_GDN // CONV_CH)),
            pl.BlockSpec((gs, 1, gw), lambda i: (i, 0, (COL_GDN + CONV_CH) // gw)),
            pl.BlockSpec((gs, 1, BA_PAD), lambda i: (i, 0, 0)),
            pl.BlockSpec((gs, RET_HEADS, RET_DK, RET_DV), lambda i: (i, 0, 0, 0)),
            pl.BlockSpec((gs, GDN_HEADS, GDN_DK, GDN_DV), lambda i: (i, 0, 0, 0)),
            pl.BlockSpec((gs, CONV_W - 1, CONV_CH), lambda i: (i, 0, 0)),
            pl.BlockSpec((1, RET_DK), c2),
            pl.BlockSpec((1, RET_DK), c2),
            pl.BlockSpec((CONV_W, CONV_CH), c2),
            pl.BlockSpec((1, BA_PAD), c2),
            pl.BlockSpec((1, BA_PAD), c2),
            pl.BlockSpec((1, vw), c2),
            pl.BlockSpec((1, GDN_DV), c2),
        ],
        out_specs=[
            pl.BlockSpec((gs, 1, vw), lambda i: (i, 0, 0)),
            pl.BlockSpec((gs, 1, gw), lambda i: (i, 0, 0)),
            pl.BlockSpec((gs, RET_HEADS, RET_DK, RET_DV), lambda i: (i, 0, 0, 0)),
            pl.BlockSpec((gs, GDN_HEADS, GDN_DK, GDN_DV), lambda i: (i, 0, 0, 0)),
            pl.BlockSpec((gs, CONV_W - 1, CONV_CH), lambda i: (i, 0, 0)),
        ],
        out_shape=[
            jax.ShapeDtypeStruct((ns, 1, vw), BF16),
            jax.ShapeDtypeStruct((ns, 1, gw), BF16),
            jax.ShapeDtypeStruct(state_ret.shape, F32),
            jax.ShapeDtypeStruct(state_gdn.shape, F32),
            jax.ShapeDtypeStruct(state_conv.shape, F32),
        ],
        compiler_params=_params("parallel"),
        name="sample_mix",
    )(proj3, proj3, proj3, ba3, state_ret, state_gdn, state_conv, cos, sin, conv_w, alog_r, dt_r, gn, norm_g)


def _merge_body(x_ref, nmg_ref, wgt_ref, oa_ref, ob_ref, wa_ref, wb_ref, wo_ref, ng_ref, wq_ref, x1_ref, q_ref):
    x = x_ref[...]
    gates = jnp.dot(_rms(x, nmg_ref[...]).astype(BF16), wgt_ref[...], preferred_element_type=F32)
    ya = jnp.dot(oa_ref[...], wa_ref[...], preferred_element_type=F32)
    yb = jnp.dot(ob_ref[...], wb_ref[...], preferred_element_type=F32)
    merged = jax.nn.sigmoid(gates[:, :D_MODEL]) * ya + jax.nn.sigmoid(gates[:, D_MODEL:]) * yb
    x1 = x + _mm(merged, wo_ref[...])
    x1_ref[...] = x1
    q_ref[...] = _mm(_rms(x1, ng_ref[...]), wq_ref[...]).astype(BF16)


def _merge(x, mix_g, w_gates, oa, ob, wa, wb, wo, ng, wq, tm):
    m = x.shape[0]
    row = lambda i: (i, 0)
    c2 = lambda i: (0, 0)
    wspec = pl.BlockSpec((D_MODEL, D_MODEL), c2)
    return pl.pallas_call(
        _merge_body,
        grid=(m // tm,),
        in_specs=[
            pl.BlockSpec((tm, D_MODEL), row),
            pl.BlockSpec((1, D_MODEL), c2),
            pl.BlockSpec((D_MODEL, 2 * D_MODEL), c2),
            pl.BlockSpec((tm, D_MODEL), row),
            pl.BlockSpec((tm, D_MODEL), row),
            wspec, wspec, wspec,
            pl.BlockSpec((1, D_MODEL), c2),
            wspec,
        ],
        out_specs=[pl.BlockSpec((tm, D_MODEL), row), pl.BlockSpec((tm, D_MODEL), row)],
        out_shape=[jax.ShapeDtypeStruct((m, D_MODEL), F32), jax.ShapeDtypeStruct((m, D_MODEL), BF16)],
        compiler_params=_params("parallel"),
        name="merge",
    )(x, mix_g, w_gates, oa, ob, wa, wb, wo, ng, wq)


def _memkv_body(m_ref, g_ref, wk_ref, wv_ref, k_ref, v_ref):
    mn = _rms(m_ref[...], g_ref[...]).astype(BF16)
    k_ref[...] = jnp.dot(mn, wk_ref[...], preferred_element_type=F32)
    v_ref[...] = jnp.dot(mn, wv_ref[...], preferred_element_type=F32)


def _memkv(mem, g, wk, wv, tm=512):
    m = mem.shape[0]
    row = lambda i: (i, 0)
    c2 = lambda i: (0, 0)
    return pl.pallas_call(
        _memkv_body,
        grid=(m // tm,),
        in_specs=[pl.BlockSpec((tm, D_MODEL), row), pl.BlockSpec((1, D_MODEL), c2),
                  pl.BlockSpec((D_MODEL, D_MODEL), c2), pl.BlockSpec((D_MODEL, D_MODEL), c2)],
        out_specs=[pl.BlockSpec((tm, D_MODEL), row), pl.BlockSpec((tm, D_MODEL), row)],
        out_shape=[jax.ShapeDtypeStruct((m, D_MODEL), F32)] * 2,
        compiler_params=_params("parallel"),
        name="memkv",
    )(mem, g, wk, wv)


def _xattn_body(q_ref, mk_ref, mv_ref, x1_ref, wo_ref, x2_ref):
    parts = []
    for h in range(X_HEADS):
        hs = slice(h * X_HD, (h + 1) * X_HD)
        s = _mm_nt(q_ref[0, :, hs], mk_ref[0, :, hs]) * (X_HD ** -0.5)
        p = jnp.exp(s - jnp.max(s, axis=-1, keepdims=True))
        o = _mm(p, mv_ref[0, :, hs]) / jnp.sum(p, axis=-1, keepdims=True)
        parts.append(o.astype(BF16))
    x2_ref[0] = x1_ref[0] + jnp.dot(jnp.concatenate(parts, axis=1), wo_ref[...], preferred_element_type=F32)


def _xattn_prompt(q3, mk3, mv3, x13, wo, tq=512):
    b, t, _ = q3.shape
    tok = lambda i, j: (i, j, 0)
    mem = lambda i, j: (i, 0, 0)
    return pl.pallas_call(
        _xattn_body,
        grid=(b, t // tq),
        in_specs=[pl.BlockSpec((1, tq, D_MODEL), tok), pl.BlockSpec((1, N_MEM, D_MODEL), mem),
                  pl.BlockSpec((1, N_MEM, D_MODEL), mem), pl.BlockSpec((1, tq, D_MODEL), tok),
                  pl.BlockSpec((D_MODEL, D_MODEL), lambda i, j: (0, 0))],
        out_specs=pl.BlockSpec((1, tq, D_MODEL), tok),
        out_shape=jax.ShapeDtypeStruct((b, t, D_MODEL), F32),
        compiler_params=_params("parallel", "parallel"),
        name="xattn_prompt",
    )(q3, mk3, mv3, x13, wo)


def _xattn_sample_body(q_ref, mk_ref, mv_ref, o_ref):
    for g in range(q_ref.shape[0]):
        q = q_ref[g]
        s = jnp.sum(mk_ref[g] * q[None], axis=-1, keepdims=True) * (X_HD ** -0.5)
        p = jnp.exp(s - jnp.max(s, axis=0, keepdims=True))
        o_ref[g] = jnp.sum(p * mv_ref[g], axis=0) / jnp.sum(p, axis=0)


def _xattn_sample(q, mk4, mv4, gs=4):
    ns = q.shape[0]
    q3 = q.astype(F32).reshape(ns, X_HEADS, X_HD)
    row = lambda i: (i, 0, 0)
    mem = lambda i: (i, 0, 0, 0)
    return pl.pallas_call(
        _xattn_sample_body,
        grid=(ns // gs,),
        in_specs=[pl.BlockSpec((gs, X_HEADS, X_HD), row), pl.BlockSpec((gs, N_MEM, X_HEADS, X_HD), mem),
                  pl.BlockSpec((gs, N_MEM, X_HEADS, X_HD), mem)],
        out_specs=pl.BlockSpec((gs, X_HEADS, X_HD), row),
        out_shape=jax.ShapeDtypeStruct((ns, X_HEADS, X_HD), F32),
        compiler_params=_params("parallel"),
        name="xattn_sample",
    )(q3, mk4, mv4)


def _resid_mm_body(x_ref, a_ref, w_ref, o_ref):
    o_ref[...] = x_ref[...] + jnp.dot(a_ref[...], w_ref[...], preferred_element_type=F32)


def _resid_mm(x, a, w):
    m = x.shape[0]
    return pl.pallas_call(
        _resid_mm_body,
        out_shape=jax.ShapeDtypeStruct((m, D_MODEL), F32),
        compiler_params=pltpu.CompilerParams(vmem_limit_bytes=VMEM_LIMIT),
        name="resid_mm",
    )(x, a, w)


def _ffn_body(x_ref, ng_ref, wg_ref, wu_ref, wd_ref, nf_ref, y_ref):
    x = x_ref[...]
    h = _rms(x, ng_ref[...]).astype(BF16)
    gate = jnp.dot(h, wg_ref[...], preferred_element_type=F32)
    up = jnp.dot(h, wu_ref[...], preferred_element_type=F32)
    x3 = x + _mm(_silu(gate) * up, wd_ref[...])
    y_ref[...] = _rms(x3, nf_ref[...])


def _ffn(x, ng, wg, wu, wd, nf, tm):
    m = x.shape[0]
    dff = wg.shape[1]
    row = lambda i: (i, 0)
    c2 = lambda i: (0, 0)
    single = pl.Buffered(1)
    return pl.pallas_call(
        _ffn_body,
        grid=(m // tm,),
        in_specs=[pl.BlockSpec((tm, D_MODEL), row), pl.BlockSpec((1, D_MODEL), c2),
                  pl.BlockSpec((D_MODEL, dff), c2, pipeline_mode=single),
                  pl.BlockSpec((D_MODEL, dff), c2, pipeline_mode=single),
                  pl.BlockSpec((dff, D_MODEL), c2, pipeline_mode=single),
                  pl.BlockSpec((1, D_MODEL), c2)],
        out_specs=pl.BlockSpec((tm, D_MODEL), row),
        out_shape=jax.ShapeDtypeStruct((m, D_MODEL), F32),
        compiler_params=_params("parallel"),
        name="ffn",
    )(x, ng, wg, wu, wd, nf)


def _rope_tables(pos):
    half = RET_DK // 2
    inv = ROPE_BASE ** (-jnp.arange(half, dtype=F32) / half)
    ang = pos.astype(F32)[:, None] * inv[None, :]
    cos, sin = jnp.cos(ang), jnp.sin(ang)
    return jnp.concatenate([cos, cos], axis=-1), jnp.concatenate([-sin, sin], axis=-1)


def _pad_lanes(v, offset):
    return jnp.zeros((BA_PAD,), F32).at[offset:offset + v.shape[0]].set(v)


def kernel(x_prompt, x_sample, state_ret, state_gdn, state_conv, cache_mem_k, cache_mem_v, mem_prompt,
           norm_mix_g, w_in, ret_gn_g, w_branch_a, gdn_conv_w, gdn_a_log, gdn_dt_bias, gdn_norm_g,
           w_branch_b, w_out, norm_x_g, mem_norm_g, w_xq, w_xk, w_xv, w_xo, norm_ffn_g, w_gate, w_up,
           w_down, norm_final_g):
    depth = w_in.shape[0]
    assert depth == 1, "single-layer kernel"
    b, t, _ = x_prompt.shape
    ns = x_sample.shape[0]
    l = 0

    w = w_in[l]
    ba0, g0 = COL_GATE, COL_GATE + 2 * GDN_HEADS
    w_rg = w[:, :ba0].astype(BF16)
    w_ret, w_gdn, w_gates = w_rg[:, :COL_GDN], w_rg[:, COL_GDN:], w[:, g0:].astype(BF16)
    w_ba =jnp.pad(w[:, ba0:g0], ((0, 0), (0, BA_PAD - 2 * GDN_HEADS))).astype(BF16)
    w_bat = w_ba.T
    row = lambda v: v.reshape(1, -1)
    wa, wb, wo = w_branch_a[l].astype(BF16), w_branch_b[l].astype(BF16), w_out[l].astype(BF16)
    wq, wk, wv, wxo = w_xq[l].astype(BF16), w_xk[l].astype(BF16), w_xv[l].astype(BF16), w_xo[l].astype(BF16)
    wg, wu, wd = w_gate[l].astype(BF16), w_up[l].astype(BF16), w_down[l].astype(BF16)
    alog_r = _pad_lanes(gdn_a_log[l], GDN_HEADS).reshape(1, BA_PAD)
    dt_r = _pad_lanes(gdn_dt_bias[l], GDN_HEADS).reshape(1, BA_PAD)
    alog_c, dt_c = alog_r.reshape(BA_PAD, 1), dt_r.reshape(BA_PAD, 1)
    cos_p, sin_p = _rope_tables(jnp.arange(t))
    cos_s, sin_s = _rope_tables(PAST_LEN + jnp.arange(1))

    xp = x_prompt.reshape(b * t, D_MODEL)
    mix_g = row(norm_mix_g[l])
    oa_p, sr_p = _retention_prompt(x_prompt, mix_g, w_ret, cos_p, sin_p, row(ret_gn_g[l]))
    ob_p, sg_p, sc_p = _gdn_prompt(x_prompt, mix_g, w_gdn, w_ba, w_bat, gdn_conv_w[l], alog_r, dt_r,
                                   alog_c, dt_c, row(gdn_norm_g[l]))
    x1_p, q_p = _merge(xp, mix_g, w_gates, oa_p.reshape(b * t, -1), ob_p.reshape(b * t, -1), wa, wb, wo,
                       row(norm_x_g[l]), wq, tm=512)
    mk_p, mv_p = _memkv(mem_prompt.reshape(b * N_MEM, D_MODEL), row(mem_norm_g[l]), wk, wv)
    x2_p = _xattn_prompt(q_p.reshape(b, t, D_MODEL), mk_p.reshape(b, N_MEM, D_MODEL),
                         mv_p.reshape(b, N_MEM, D_MODEL), x1_p.reshape(b, t, D_MODEL), wxo)
    y_p = _ffn(x2_p.reshape(b * t, D_MODEL), row(norm_ffn_g[l]), wg, wu, wd, row(norm_final_g), tm=256)

    xs = x_sample.reshape(ns, D_MODEL)
    proj_s, ba_s = _inproj(xs, mix_g, w_rg, tm=ns, w_ba=w_ba)
    oa_s, ob_s, sr_s, sg_s, sc_s = _sample_mix(proj_s, ba_s, state_ret[l], state_gdn[l], state_conv[l],
                                               cos_s, sin_s, gdn_conv_w[l], alog_r, dt_r, row(ret_gn_g[l]),
                                               row(gdn_norm_g[l]))
    x1_s, q_s = _merge(xs, mix_g, w_gates, oa_s.reshape(ns, -1), ob_s.reshape(ns, -1), wa, wb, wo,
                       row(norm_x_g[l]), wq, tm=ns)
    o_s = _xattn_sample(q_s, cache_mem_k[l], cache_mem_v[l])
    x2_s = _resid_mm(x1_s, o_s.reshape(ns, D_MODEL).astype(BF16), wxo)
    y_s = _ffn(x2_s, row(norm_ffn_g[l]), wg, wu, wd, row(norm_final_g), tm=ns)

    return (y_p.reshape(b, t, D_MODEL), y_s.reshape(ns, 1, D_MODEL),
            sr_p[None], sg_p[None], sc_p[None],
            mk_p.reshape(1, b, N_MEM, X_HEADS, X_HD), mv_p.reshape(1, b, N_MEM, X_HEADS, X_HD),
            sr_s[None], sg_s[None], sc_s[None])
```

```python
import functools

import numpy as np
import jax
import jax.numpy as jnp
from jax import lax
from jax.experimental import pallas as pl
from jax.experimental.pallas import tpu as pltpu

F32 = jnp.float32
BF16 = jnp.bfloat16

D_MODEL = 1024
RET_HEADS, RET_DK, RET_DV = 4, 128, 256
GDN_HEADS, GDN_DK, GDN_DV = 8, 128, 128
CONV_W = 4
CONV_CH = 3 * GDN_HEADS * GDN_DK
N_MEM, X_HEADS, X_HD = 256, 4, 256
PAST_LEN = 16384
ROPE_BASE = 10000.0
EPS = 1e-6
GDN_CHUNK = 64

COL_RET = 0
COL_GDN = 3072
COL_GATE = 7168
BA_PAD = 128

VMEM_LIMIT = 56 * 1024 * 1024

NT_DIMS = (((1,), (1,)), ((), ()))
TN_DIMS = (((0,), (0,)), ((), ()))


def _mm(a, b):
    return jnp.dot(a.astype(BF16), b.astype(BF16), preferred_element_type=F32)


def _mm_nt(a, b):
    return lax.dot_general(a.astype(BF16), b.astype(BF16), NT_DIMS, preferred_element_type=F32)


def _mm_tn(a, b):
    return lax.dot_general(a.astype(BF16), b.astype(BF16), TN_DIMS, preferred_element_type=F32)


def _mm_f32(a, b):
    return jnp.dot(a, b, preferred_element_type=F32, precision=lax.Precision.HIGHEST)


def _rms(x, g):
    return x * lax.rsqrt(jnp.mean(x * x, axis=-1, keepdims=True) + EPS) * g


def _silu(x):
    return x * jax.nn.sigmoid(x)


def _softplus(x):
    return jnp.maximum(x, 0.0) + jnp.log1p(jnp.exp(-jnp.abs(x)))


def _params(*sem):
    return pltpu.CompilerParams(dimension_semantics=sem, vmem_limit_bytes=VMEM_LIMIT)


def _inproj_body(with_ba, x_ref, g_ref, w_ref, *rest):
    if with_ba:
        wba_ref, o_ref, oba_ref, h_scr = rest
    else:
        o_ref, h_scr = rest

    @pl.when(pl.program_id(1) == 0)
    def _():
        hb = _rms(x_ref[...], g_ref[...]).astype(BF16)
        h_scr[...] = hb
        if with_ba:
            oba_ref[...] = jnp.dot(hb, wba_ref[...], preferred_element_type=F32)

    o_ref[...] = jnp.dot(h_scr[...], w_ref[...], preferred_element_type=F32)


def _inproj(x, g, w_main, tm, w_ba=None, tn=1024):
    m = x.shape[0]
    n = w_main.shape[1]
    in_specs = [
        pl.BlockSpec((tm, D_MODEL), lambda i, j: (i, 0)),
        pl.BlockSpec((1, D_MODEL), lambda i, j: (0, 0)),
        pl.BlockSpec((D_MODEL, tn), lambda i, j: (0, j)),
    ]
    out_specs = [pl.BlockSpec((tm, tn), lambda i, j: (i, j))]
    out_shape = [jax.ShapeDtypeStruct((m, n), F32)]
    args = [x, g, w_main]
    if w_ba is not None:
        in_specs.append(pl.BlockSpec((D_MODEL, BA_PAD), lambda i, j: (0, 0)))
        out_specs.append(pl.BlockSpec((tm, BA_PAD), lambda i, j: (i, 0)))
        out_shape.append(jax.ShapeDtypeStruct((m, BA_PAD), F32))
        args.append(w_ba)
    return pl.pallas_call(
        functools.partial(_inproj_body, w_ba is not None),
        grid=(m // tm, n // tn),
        in_specs=in_specs,
        out_specs=out_specs,
        out_shape=out_shape,
        scratch_shapes=[pltpu.VMEM((tm, D_MODEL), BF16)],
        compiler_params=_params("parallel", "arbitrary"),
        name="inproj",
    )(*args)


_RET_LOG_G = np.log1p(-np.exp2(-5.0 - np.arange(RET_HEADS, dtype=np.float64)))


def _ret_tables(c):
    idx = np.arange(c, dtype=np.float64)
    diff = idx[:, None] - idx[None, :]
    dmat = np.where(diff >= 0, np.exp(np.maximum(diff, 0.0)[None] * _RET_LOG_G[:, None, None]), 0.0)
    qdec = np.exp((idx + 1.0)[None, :] * _RET_LOG_G[:, None])
    kdec = np.exp((c - 1.0 - idx)[None, :] * _RET_LOG_G[:, None])
    lane = np.ones((1, 1, RET_DK))
    return (jnp.asarray(dmat, F32), jnp.asarray(qdec[:, :, None] * lane, F32),
            jnp.asarray(kdec[:, :, None] * lane, F32), [float(v) for v in np.exp(c * _RET_LOG_G)])


def _rot(x, cos, sin):
    return x * cos + pltpu.roll(x, RET_DK // 2, 1) * sin


def _group_norm_gate(o, gate, gn):
    mu = jnp.mean(o, axis=-1, keepdims=True)
    d = o - mu
    var = jnp.mean(d * d, axis=-1, keepdims=True)
    return _silu(gate) * (d * lax.rsqrt(var + EPS) * gn)


def _proj_tiles(hb, w_ref, dst, width=512):
    for c0 in range(0, w_ref.shape[1], width):
        dst[:, c0:c0 + width] = jnp.dot(hb, w_ref[:, c0:c0 + width], preferred_element_type=F32)


def _next_block(nt, nblocks):
    def index_map(i, j):
        n1 = jnp.minimum(i * nt + j + 1, nblocks - 1)
        return (n1 // nt, n1 % nt, 0)
    return index_map


def _ret_body(cdec, x0_ref, xn_ref, nmg_ref, w_ref, *rest):
    *consts, o_ref, s_out_ref, s_scr, pa, pb = rest
    t = pl.program_id(1)
    n = pl.program_id(0) * pl.num_programs(1) + t
    bufs = (pa, pb)

    @pl.when(t == 0)
    def _():
        s_scr[...] = jnp.zeros_like(s_scr)

    @pl.when(n == 0)
    def _():
        _proj_tiles(_rms(x0_ref[0], nmg_ref[...]).astype(BF16), w_ref, pa)

    for slot in range(2):
        @pl.when(n % 2 == slot)
        def _(slot=slot):
            _ret_step(cdec, t, xn_ref, nmg_ref, w_ref, bufs[slot], bufs[1 - slot], *consts, o_ref, s_out_ref, s_scr)


def _ret_step(cdec, t, xn_ref, nmg_ref, w_ref, proj, proj_next, cos_ref, sin_ref, dmat_ref, qdec_ref, kdec_ref,
              gn_ref, o_ref, s_out_ref, s_scr):
    qw = RET_HEADS * RET_DK
    _proj_tiles(_rms(xn_ref[0], nmg_ref[...]).astype(BF16), w_ref, proj_next)

    cos, sin = cos_ref[...], sin_ref[...]
    for h in range(RET_HEADS):
        qk = slice(h * RET_DK, (h + 1) * RET_DK)
        kk = slice(qw + h * RET_DK, qw + (h + 1) * RET_DK)
        vv = slice(h * RET_DV, (h + 1) * RET_DV)
        q = _rot(proj[:, qk], cos, sin)
        k = _rot(proj[:, kk], cos, sin) * (RET_DK ** -0.5)
        v = proj[:, 2 * qw + h * RET_DV:2 * qw + (h + 1) * RET_DV]
        gate = proj[:, 2 * qw + (RET_HEADS + h) * RET_DV:2 * qw + (RET_HEADS + h + 1) * RET_DV]
        s = s_scr[h]
        scores = _mm_nt(q, k) * dmat_ref[h]
        o = _mm(scores, v) + _mm(q * qdec_ref[h], s)
        s_scr[h] = cdec[h] * s + _mm_tn(k * kdec_ref[h], v)
        o_ref[0, :, vv] = _group_norm_gate(o, gate, gn_ref[:, vv]).astype(BF16)

    @pl.when(t == pl.num_programs(1) - 1)
    def _():
        s_out_ref[0] = s_scr[...]


def _retention_prompt(x3, mix_g, w_ret, cos, sin, gn, tb=256):
    b, t, _ = x3.shape
    nt = t // tb
    dmat, qdec, kdec, cdec = _ret_tables(tb)
    vw = RET_HEADS * RET_DV
    const3 = lambda i, j: (0, 0, 0)
    return pl.pallas_call(
        functools.partial(_ret_body, cdec),
        grid=(b, nt),
        in_specs=[
            pl.BlockSpec((1, tb, D_MODEL), const3),
            pl.BlockSpec((1, tb, D_MODEL), _next_block(nt, b * nt)),
            pl.BlockSpec((1, D_MODEL), lambda i, j: (0, 0)),
            pl.BlockSpec(w_ret.shape, lambda i, j: (0, 0), pipeline_mode=pl.Buffered(1)),
            pl.BlockSpec((tb, RET_DK), lambda i, j: (j, 0)),
            pl.BlockSpec((tb, RET_DK), lambda i, j: (j, 0)),
            pl.BlockSpec((RET_HEADS, tb, tb), const3),
            pl.BlockSpec((RET_HEADS, tb, RET_DK), const3),
            pl.BlockSpec((RET_HEADS, tb, RET_DK), const3),
            pl.BlockSpec((1, vw), lambda i, j: (0, 0)),
        ],
        out_specs=[
            pl.BlockSpec((1, tb, vw), lambda i, j: (i, j, 0)),
            pl.BlockSpec((1, RET_HEADS, RET_DK, RET_DV), lambda i, j: (i, 0, 0, 0)),
        ],
        out_shape=[
            jax.ShapeDtypeStruct((b, t, vw), BF16),
            jax.ShapeDtypeStruct((b, RET_HEADS, RET_DK, RET_DV), F32),
        ],
        scratch_shapes=[pltpu.VMEM((RET_HEADS, RET_DK, RET_DV), F32)]
        + 2 * [pltpu.VMEM((tb, w_ret.shape[1]), F32)],
        compiler_params=_params("arbitrary", "arbitrary"),
        name="retention_prompt",
    )(x3, x3, mix_g, w_ret, cos, sin, dmat, qdec, kdec, gn)


def _gdn_tables(tb):
    idx = np.arange(tb)
    same = (idx[:, None] // GDN_CHUNK) == (idx[None, :] // GDN_CHUNK)
    lower = same & (idx[:, None] >= idx[None, :])
    nchunk = tb // GDN_CHUNK
    chunk_sel = np.repeat((idx[:, None] // GDN_CHUNK) == np.arange(nchunk)[None, :], 128, axis=1)
    grp = 2 * GDN_CHUNK
    il = (idx % grp)[:, None]
    jl = np.arange(grp)[None, :]
    bias = np.where((il // GDN_CHUNK == jl // GDN_CHUNK) & (il >= jl), 0.0, -1e30)
    eye = (il == jl).astype(np.float64)
    return (jnp.asarray(lower, F32), jnp.asarray(lower.T, F32), jnp.asarray(same, F32),
            jnp.asarray(chunk_sel, F32), jnp.asarray(bias, F32), jnp.asarray(1.0 - eye, F32),
            jnp.asarray(eye, BF16))


def _gdn_prepare(x_ref, first_of_seq, nmg_ref, wqkv_ref, wz_ref, wba_ref, wbat_ref, cw_ref, xr, cs_buf, zs, bas,
                 bats, tb):
    st = {}

    def start():
        xr[0:8, :] = jnp.where(first_of_seq, 0.0, xr[tb:tb + 8, :])
        st["hb"] = _rms(x_ref[0], nmg_ref[...]).astype(BF16)

    def tile(w_ref, dst, c0, width=512):
        def run():
            dst[:, c0:c0 + width] = jnp.dot(st["hb"], w_ref[:, c0:c0 + width], preferred_element_type=F32)
        return run

    def logits():
        bas[...] = jnp.dot(st["hb"], wba_ref[...], preferred_element_type=F32)
        bats[...] = lax.dot_general(wbat_ref[...], st["hb"], NT_DIMS, preferred_element_type=F32)

    def conv(c0):
        def run():
            cs = slice(c0, c0 + GDN_DK)
            acc = xr[5:5 + tb, cs] * cw_ref[0:1, cs]
            for i in range(1, CONV_W):
                acc = acc + xr[5 + i:5 + i + tb, cs] * cw_ref[i:i + 1, cs]
            cs_buf[:, cs] = _silu(acc)
        return run

    mxu_items = ([start] + [tile(wqkv_ref, xr.at[8:8 + tb], c0) for c0 in range(0, CONV_CH, 512)]
                 + [tile(wz_ref, zs, c0) for c0 in range(0, wz_ref.shape[1], 512)] + [logits])
    vpu_items = [conv(c0) for c0 in range(0, CONV_CH, GDN_DK)]
    return mxu_items, vpu_items


class _Background:
    def __init__(self, items):
        self._items = list(items)

    def __call__(self, count):
        for item in self._items[:count]:
            item()
        del self._items[:count]

    def drain(self):
        self(len(self._items))


def _gdn_body(tb, x0_ref, xn_ref, nmg_ref, wqkv_ref, wz_ref, wba_ref, wbat_ref, cw_ref, *rest):
    *consts, o_ref, s_out_ref, conv_out_ref, s_scr, xr, ca, za, baa, bata, cb, zb, bab, batb = rest
    t = pl.program_id(1)
    nt = pl.num_programs(1)
    n = pl.program_id(0) * nt + t
    weights = (nmg_ref, wqkv_ref, wz_ref, wba_ref, wbat_ref, cw_ref)
    bufs = ((ca, za, baa, bata), (cb, zb, bab, batb))

    @pl.when(t == 0)
    def _():
        s_scr[...] = jnp.zeros_like(s_scr)

    @pl.when(n == 0)
    def _():
        mxu_items, vpu_items = _gdn_prepare(x0_ref, True, *weights, xr, *bufs[0], tb)
        _Background(mxu_items + vpu_items).drain()

    for slot in range(2):
        @pl.when(n % 2 == slot)
        def _(slot=slot):
            mxu_items, vpu_items = _gdn_prepare(xn_ref, t == nt - 1, *weights, xr, *bufs[1 - slot], tb)
            _gdn_step(tb, t, bufs[slot], *consts, o_ref, s_out_ref, s_scr,
                      _Background(mxu_items), _Background(vpu_items))

    @pl.when(t == nt - 2)
    def _():
        conv_out_ref[0] = xr[tb + 5:tb + 8, :]


def _gdn_step(tb, t, cur, alog_r_ref, dt_r_ref, alog_c_ref, dt_c_ref, ng_ref,
              lbd_ref, ubd_ref, obd_ref, csel_ref, bias_ref, offd_ref, eye_ref, o_ref, s_out_ref, s_scr,
              bg_mxu, bg_vpu):
    nchunk = tb // GDN_CHUNK
    hk = GDN_HEADS * GDN_DK
    c_scr, zz, ba_ref, bat_ref = cur
    ba, bat = ba_ref[...], bat_ref[...]

    bg_mxu(2)
    beta_c = jax.nn.sigmoid(ba)
    g_c = -jnp.exp(alog_r_ref[...]) * _softplus(ba + dt_r_ref[...])
    g_r = -jnp.exp(alog_c_ref[...]) * _softplus(bat + dt_c_ref[...])
    gc_c = _mm_f32(lbd_ref[...], g_c)
    gt_c = _mm_f32(obd_ref[...], g_c)
    gc_r = _mm_f32(g_r, ubd_ref[...])
    gt_l = _mm_f32(g_r, csel_ref[...])

    heads = range(GDN_HEADS)
    grp = 2 * GDN_CHUNK
    groups = [slice(p * grp, (p + 1) * grp) for p in range(tb // grp)]

    def grp_dot(a, b):
        return jnp.concatenate([jnp.dot(a[g], b[g], preferred_element_type=F32) for g in groups], axis=0)

    def grp_dot_nt(a, b):
        return jnp.concatenate([lax.dot_general(a[g], b[g], NT_DIMS, preferred_element_type=F32)
                                for g in groups], axis=0)

    bias = bias_ref[...]
    offdiag = offd_ref[...]
    eye_b = eye_ref[...]
    qs, ks, gammas, pbs, rhss, qgs, khs = [], [], [], [], [], [], []
    for h in heads:
        q = c_scr[:, h * GDN_DK:(h + 1) * GDN_DK]
        k = c_scr[:, hk + h * GDN_DK:hk + (h + 1) * GDN_DK]
        v = c_scr[:, 2 * hk + h * GDN_DV:2 * hk + (h + 1) * GDN_DV]
        q = q * lax.rsqrt(jnp.sum(q * q, axis=-1, keepdims=True) + EPS) * (GDN_DK ** -0.5)
        k = k * lax.rsqrt(jnp.sum(k * k, axis=-1, keepdims=True) + EPS)
        beta = beta_c[:, h:h + 1]
        gcc = gc_c[:, 8 + h:9 + h]
        gtc = gt_c[:, 8 + h:9 + h]
        gcr = gc_r[8 + h:9 + h, :]
        dg = jnp.concatenate([gcc[g] - gcr[:, g] for g in groups], axis=0)
        gamma = jnp.exp(dg + bias)
        kbeta = k * beta
        kb = k.astype(BF16)
        pbs.append((grp_dot_nt((-kbeta).astype(BF16), kb) * (gamma * offdiag)).astype(BF16))
        eg = jnp.exp(gcc)
        rhss.append(jnp.concatenate([v * beta, kbeta * eg], axis=1))
        qs.append(q.astype(BF16))
        ks.append(kb)
        gammas.append(gamma)
        qgs.append(q * eg)
        khs.append(k * jnp.exp(gtc - gcc))
        bg_mxu(1)
    bg_mxu.drain()

    def solve(_, carry):
        pb = pbs
        tbs = [pb[h] + eye_b for h in heads]
        for lvl in range(5):
            pb = [grp_dot(pb[h], pb[h]).astype(BF16) for h in heads]
            bg_vpu(1)
            tnew = [grp_dot(tbs[h], pb[h] + eye_b) for h in heads]
            bg_vpu(1)
            tbs = [x.astype(BF16) for x in tnew]
            bg_vpu(1)

        us, ws, qks = [], [], []
        for h in heads:
            uw = rhss[h] + grp_dot(tbs[h] - eye_b, rhss[h].astype(BF16))
            us.append(uw[:, :GDN_DV])
            ws.append(uw[:, GDN_DV:])
            qks.append((grp_dot_nt(qs[h], ks[h]) * gammas[h]).astype(BF16))
        bg_vpu(2)

        s = [s_scr[h] for h in heads]
        vn_parts = [[] for _ in heads]
        qs_parts = [[] for _ in heads]
        for c in range(nchunk):
            rows = slice(c * GDN_CHUNK, (c + 1) * GDN_CHUNK)
            for h in heads:
                wq = _mm(jnp.concatenate([ws[h][rows], qgs[h][rows]], axis=0), s[h])
                vn = us[h][rows] - wq[:GDN_CHUNK]
                qs_parts[h].append(wq[GDN_CHUNK:])
                vn_parts[h].append(vn)
                decay = jnp.exp(gt_l[8 + h:9 + h, c * 128:(c + 1) * 128])
                s[h] = decay * s[h] + _mm_tn(khs[h][rows], vn)
            bg_vpu(1)
        bg_vpu.drain()
        for h in heads:
            hs = slice(h * GDN_DV, (h + 1) * GDN_DV)
            s_scr[h] = s[h]
            vn = jnp.concatenate(vn_parts[h], axis=0).astype(BF16)
            o = jnp.concatenate(qs_parts[h], axis=0) + grp_dot(qks[h], vn)
            o_ref[0, :, hs] = (_rms(o, ng_ref[...]) * _silu(zz[:, hs])).astype(BF16)
        return carry

    lax.fori_loop(0, jnp.minimum(t + 1, 1), solve, 0)

    @pl.when(t == pl.num_programs(1) - 1)
    def _():
        s_out_ref[0] = s_scr[...]


def _gdn_prompt(x3, mix_g, w_gdn, w_ba, w_bat, conv_w, alog_r, dt_r, alog_c, dt_c, norm_g, tb=256):
    b, t, _ = x3.shape
    nt = t // tb
    assert t % tb == 0 and nt >= 2, "a sequence's last block must be prepared during one of its own steps"
    lbd, ubd, obd, csel, bias, offdiag, eye = _gdn_tables(tb)
    vw = GDN_HEADS * GDN_DV
    c2 = lambda i, j: (0, 0)
    single = pl.Buffered(1)
    return pl.pallas_call(
        functools.partial(_gdn_body, tb),
        grid=(b, nt),
        in_specs=[
            pl.BlockSpec((1, tb, D_MODEL), lambda i, j: (0, 0, 0)),
            pl.BlockSpec((1, tb, D_MODEL), _next_block(nt, b * nt)),
            pl.BlockSpec((1, D_MODEL), c2),
            pl.BlockSpec((D_MODEL, CONV_CH), c2, pipeline_mode=single),
            pl.BlockSpec((D_MODEL, vw), lambda i, j: (0, CONV_CH // vw), pipeline_mode=single),
            pl.BlockSpec((D_MODEL, BA_PAD), c2),
            pl.BlockSpec((BA_PAD, D_MODEL), c2),
            pl.BlockSpec((CONV_W, CONV_CH), c2),
            pl.BlockSpec((1, BA_PAD), c2),
            pl.BlockSpec((1, BA_PAD), c2),
            pl.BlockSpec((BA_PAD, 1), c2),
            pl.BlockSpec((BA_PAD, 1), c2),
            pl.BlockSpec((1, GDN_DV), c2),
            pl.BlockSpec((tb, tb), c2),
            pl.BlockSpec((tb, tb), c2),
            pl.BlockSpec((tb, tb), c2),
            pl.BlockSpec((tb, (tb // GDN_CHUNK) * 128), c2),
            pl.BlockSpec((tb, 2 * GDN_CHUNK), c2),
            pl.BlockSpec((tb, 2 * GDN_CHUNK), c2),
            pl.BlockSpec((tb, 2 * GDN_CHUNK), c2),
        ],
        out_specs=[
            pl.BlockSpec((1, tb, vw), lambda i, j: (i, j, 0)),
            pl.BlockSpec((1, GDN_HEADS, GDN_DK, GDN_DV), lambda i, j: (i, 0, 0, 0)),
            pl.BlockSpec((1, CONV_W - 1, CONV_CH), lambda i, j: (i, 0, 0)),
        ],
        out_shape=[
            jax.ShapeDtypeStruct((b, t, vw), BF16),
            jax.ShapeDtypeStruct((b, GDN_HEADS, GDN_DK, GDN_DV), F32),
            jax.ShapeDtypeStruct((b, CONV_W - 1, CONV_CH), F32),
        ],
        scratch_shapes=[pltpu.VMEM((GDN_HEADS, GDN_DK, GDN_DV), F32), pltpu.VMEM((tb + 8, CONV_CH), F32)]
        + 2 * [pltpu.VMEM((tb, CONV_CH), F32), pltpu.VMEM((tb, vw), F32),
               pltpu.VMEM((tb, BA_PAD), F32), pltpu.VMEM((BA_PAD, tb), F32)],
        compiler_params=_params("arbitrary", "arbitrary"),
        name="gdn_prompt",
    )(x3, x3, mix_g, w_gdn, w_gdn, w_ba, w_bat, conv_w, alog_r, dt_r, alog_c, dt_c, norm_g, lbd, ubd, obd, csel,
      bias, offdiag, eye)


def _sample_mix_body(cdec, ret_ref, gdn_ref, z_ref, ba_ref, sr_ref, sg_ref, sc_ref, cos_ref, sin_ref, cw_ref,
                     alog_ref, dt_ref, gn_ref, ng_ref,
                     oa_ref, ob_ref, sr_out_ref, sg_out_ref, sc_out_ref):
    hk = GDN_HEADS * GDN_DK
    samples = range(ret_ref.shape[0])
    cos, sin = cos_ref[...], sin_ref[...]
    rets, convs, zs, betas, egs = [], [], [], [], []
    for g in samples:
        ret = ret_ref[g]
        x_new = gdn_ref[g]
        buf = sc_ref[g]
        conv = x_new * cw_ref[CONV_W - 1:CONV_W, :]
        for i in range(CONV_W - 1):
            conv = conv + buf[i:i + 1, :] * cw_ref[i:i + 1, :]
        conv = _silu(conv)
        sc_out_ref[g, 0:2, :] = buf[1:3, :]
        sc_out_ref[g, 2:3, :] = x_new
        ba = ba_ref[g]
        betas.append(jax.nn.sigmoid(ba))
        egs.append(jnp.exp(-jnp.exp(alog_ref[...]) * _softplus(ba + dt_ref[...])))
        rets.append(ret)
        convs.append(conv)
        zs.append(z_ref[g])

    row8 = lax.broadcasted_iota(jnp.int32, (8, RET_DK), 0)

    def pad8(k, q):
        return jnp.where(row8 == 0, k, jnp.where(row8 == 1, q, 0.0)).astype(BF16)

    for h in range(RET_HEADS):
        vv = slice(h * RET_DV, (h + 1) * RET_DV)
        for g in samples:
            ret = rets[g]
            q = _rot(ret[:, h * RET_DK:(h + 1) * RET_DK], cos, sin)
            k = _rot(ret[:, 512 + h * RET_DK:512 + (h + 1) * RET_DK], cos, sin) * (RET_DK ** -0.5)
            v = ret[:, 1024 + h * RET_DV:1024 + (h + 1) * RET_DV]
            s = sr_ref[g, h]
            qs = jnp.dot(pad8(k, q), s.astype(BF16), preferred_element_type=F32)[1:2]
            v8 = jnp.broadcast_to(v, (8, RET_DV)).astype(BF16)
            sr_out_ref[g, h] = cdec[h] * s + lax.dot_general(pad8(k, 0.0), v8, TN_DIMS, preferred_element_type=F32)
            o = cdec[h] * qs + jnp.sum(q * k, axis=-1, keepdims=True) * v
            oa_ref[g, :, vv] = _group_norm_gate(o, ret[:, 2048 + h * RET_DV:2048 + (h + 1) * RET_DV],
                                                gn_ref[:, vv]).astype(BF16)

    for h in range(GDN_HEADS):
        hs = slice(h * GDN_DV, (h + 1) * GDN_DV)
        for g in samples:
            conv = convs[g]
            q = conv[:, h * GDN_DK:(h + 1) * GDN_DK]
            k = conv[:, hk + h * GDN_DK:hk + (h + 1) * GDN_DK]
            q = q * lax.rsqrt(jnp.sum(q * q, axis=-1, keepdims=True) + EPS) * (GDN_DK ** -0.5)
            k = k * lax.rsqrt(jnp.sum(k * k, axis=-1, keepdims=True) + EPS)
            v = conv[:, 2 * hk + h * GDN_DV:2 * hk + (h + 1) * GDN_DV]
            egh = egs[g][:, 8 + h:9 + h]
            s = sg_ref[g, h]
            kq_s = jnp.dot(pad8(k, q), s.astype(BF16), preferred_element_type=F32)
            vn = betas[g][:, h:h + 1] * (v - egh * kq_s[0:1])
            vn8 = jnp.broadcast_to(vn, (8, GDN_DV)).astype(BF16)
            sg_out_ref[g, h] = egh * s + lax.dot_general(pad8(k, 0.0), vn8, TN_DIMS, preferred_element_type=F32)
            o = egh * kq_s[1:2] + jnp.sum(q * k, axis=-1, keepdims=True) * vn
            ob_ref[g, :, hs] = (_rms(o, ng_ref[...]) * _silu(zs[g][:, hs])).astype(BF16)


def _sample_mix(proj, ba, state_ret, state_gdn, state_conv, cos, sin, conv_w, alog_r, dt_r, gn, norm_g, gs=4):
    ns = proj.shape[0]
    proj3 = proj.reshape(ns, 1, COL_GATE)
    ba3 = ba.reshape(ns, 1, BA_PAD)
    cdec = [float(v) for v in np.exp(_RET_LOG_G)]
    vw = RET_HEADS * RET_DV
    gw = GDN_HEADS * GDN_DV
    c2 = lambda i: (0, 0)
    return pl.pallas_call(
        functools.partial(_sample_mix_body, cdec),
        grid=(ns // gs,),
        in_specs=[
            pl.BlockSpec((gs, 1, 3072), lambda i: (i, 0, 0)),
            pl.BlockSpec((gs, 1, CONV_CH), lambda i: (i, 0, COL_GDN // CONV_CH)),
            pl.BlockSpec((gs, 1, gw), lambda i: (i, 0, (COL_GDN + CONV_CH) // gw)),
            pl.BlockSpec((gs, 1, BA_PAD), lambda i: (i, 0, 0)),
            pl.BlockSpec((gs, RET_HEADS, RET_DK, RET_DV), lambda i: (i, 0, 0, 0)),
            pl.BlockSpec((gs, GDN_HEADS, GDN_DK, GDN_DV), lambda i: (i, 0, 0, 0)),
            pl.BlockSpec((gs, CONV_W - 1, CONV_CH), lambda i: (i, 0, 0)),
            pl.BlockSpec((1, RET_DK), c2),
            pl.BlockSpec((1, RET_DK), c2),
            pl.BlockSpec((CONV_W, CONV_CH), c2),
            pl.BlockSpec((1, BA_PAD), c2),
            pl.BlockSpec((1, BA_PAD), c2),
            pl.BlockSpec((1, vw), c2),
            pl.BlockSpec((1, GDN_DV), c2),
        ],
        out_specs=[
            pl.BlockSpec((gs, 1, vw), lambda i: (i, 0, 0)),
            pl.BlockSpec((gs, 1, gw), lambda i: (i, 0, 0)),
            pl.BlockSpec((gs, RET_HEADS, RET_DK, RET_DV), lambda i: (i, 0, 0, 0)),
            pl.BlockSpec((gs, GDN_HEADS, GDN_DK, GDN_DV), lambda i: (i, 0, 0, 0)),
            pl.BlockSpec((gs, CONV_W - 1, CONV_CH), lambda i: (i, 0, 0)),
        ],
        out_shape=[
            jax.ShapeDtypeStruct((ns, 1, vw), BF16),
            jax.ShapeDtypeStruct((ns, 1, gw), BF16),
            jax.ShapeDtypeStruct(state_ret.shape, F32),
            jax.ShapeDtypeStruct(state_gdn.shape, F32),
            jax.ShapeDtypeStruct(state_conv.shape, F32),
        ],
        compiler_params=_params("parallel"),
        name="sample_mix",
    )(proj3, proj3, proj3, ba3, state_ret, state_gdn, state_conv, cos, sin, conv_w, alog_r, dt_r, gn, norm_g)


def _merge_body(x_ref, nmg_ref, wgt_ref, oa_ref, ob_ref, wa_ref, wb_ref, wo_ref, ng_ref, wq_ref, x1_ref, q_ref):
    x = x_ref[...]
    gates = jnp.dot(_rms(x, nmg_ref[...]).astype(BF16), wgt_ref[...], preferred_element_type=F32)
    ya = jnp.dot(oa_ref[...], wa_ref[...], preferred_element_type=F32)
    yb = jnp.dot(ob_ref[...], wb_ref[...], preferred_element_type=F32)
    merged = jax.nn.sigmoid(gates[:, :D_MODEL]) * ya + jax.nn.sigmoid(gates[:, D_MODEL:]) * yb
    x1 = x + _mm(merged, wo_ref[...])
    x1_ref[...] = x1
    q_ref[...] = _mm(_rms(x1, ng_ref[...]), wq_ref[...]).astype(BF16)


def _merge(x, mix_g, w_gates, oa, ob, wa, wb, wo, ng, wq, tm):
    m = x.shape[0]
    row = lambda i: (i, 0)
    c2 = lambda i: (0, 0)
    wspec = pl.BlockSpec((D_MODEL, D_MODEL), c2)
    return pl.pallas_call(
        _merge_body,
        grid=(m // tm,),
        in_specs=[
            pl.BlockSpec((tm, D_MODEL), row),
            pl.BlockSpec((1, D_MODEL), c2),
            pl.BlockSpec((D_MODEL, 2 * D_MODEL), c2),
            pl.BlockSpec((tm, D_MODEL), row),
            pl.BlockSpec((tm, D_MODEL), row),
            wspec, wspec, wspec,
            pl.BlockSpec((1, D_MODEL), c2),
            wspec,
        ],
        out_specs=[pl.BlockSpec((tm, D_MODEL), row), pl.BlockSpec((tm, D_MODEL), row)],
        out_shape=[jax.ShapeDtypeStruct((m, D_MODEL), F32), jax.ShapeDtypeStruct((m, D_MODEL), BF16)],
        compiler_params=_params("parallel"),
        name="merge",
    )(x, mix_g, w_gates, oa, ob, wa, wb, wo, ng, wq)


def _memkv_body(m_ref, g_ref, wk_ref, wv_ref, k_ref, v_ref):
    mn = _rms(m_ref[...], g_ref[...]).astype(BF16)
    k_ref[...] = jnp.dot(mn, wk_ref[...], preferred_element_type=F32)
    v_ref[...] = jnp.dot(mn, wv_ref[...], preferred_element_type=F32)


def _memkv(mem, g, wk, wv, tm=512):
    m = mem.shape[0]
    row = lambda i: (i, 0)
    c2 = lambda i: (0, 0)
    return pl.pallas_call(
        _memkv_body,
        grid=(m // tm,),
        in_specs=[pl.BlockSpec((tm, D_MODEL), row), pl.BlockSpec((1, D_MODEL), c2),
                  pl.BlockSpec((D_MODEL, D_MODEL), c2), pl.BlockSpec((D_MODEL, D_MODEL), c2)],
        out_specs=[pl.BlockSpec((tm, D_MODEL), row), pl.BlockSpec((tm, D_MODEL), row)],
        out_shape=[jax.ShapeDtypeStruct((m, D_MODEL), F32)] * 2,
        compiler_params=_params("parallel"),
        name="memkv",
    )(mem, g, wk, wv)


def _xattn_body(q_ref, mk_ref, mv_ref, x1_ref, wo_ref, x2_ref):
    parts = []
    for h in range(X_HEADS):
        hs = slice(h * X_HD, (h + 1) * X_HD)
        s = _mm_nt(q_ref[0, :, hs], mk_ref[0, :, hs]) * (X_HD ** -0.5)
        p = jnp.exp(s - jnp.max(s, axis=-1, keepdims=True))
        o = _mm(p, mv_ref[0, :, hs]) / jnp.sum(p, axis=-1, keepdims=True)
        parts.append(o.astype(BF16))
    x2_ref[0] = x1_ref[0] + jnp.dot(jnp.concatenate(parts, axis=1), wo_ref[...], preferred_element_type=F32)


def _xattn_prompt(q3, mk3, mv3, x13, wo, tq=512):
    b, t, _ = q3.shape
    tok = lambda i, j: (i, j, 0)
    mem = lambda i, j: (i, 0, 0)
    return pl.pallas_call(
        _xattn_body,
        grid=(b, t // tq),
        in_specs=[pl.BlockSpec((1, tq, D_MODEL), tok), pl.BlockSpec((1, N_MEM, D_MODEL), mem),
                  pl.BlockSpec((1, N_MEM, D_MODEL), mem), pl.BlockSpec((1, tq, D_MODEL), tok),
                  pl.BlockSpec((D_MODEL, D_MODEL), lambda i, j: (0, 0))],
        out_specs=pl.BlockSpec((1, tq, D_MODEL), tok),
        out_shape=jax.ShapeDtypeStruct((b, t, D_MODEL), F32),
        compiler_params=_params("parallel", "parallel"),
        name="xattn_prompt",
    )(q3, mk3, mv3, x13, wo)


def _xattn_sample_body(q_ref, mk_ref, mv_ref, o_ref):
    for g in range(q_ref.shape[0]):
        q = q_ref[g]
        s = jnp.sum(mk_ref[g] * q[None], axis=-1, keepdims=True) * (X_HD ** -0.5)
        p = jnp.exp(s - jnp.max(s, axis=0, keepdims=True))
        o_ref[g] = jnp.sum(p * mv_ref[g], axis=0) / jnp.sum(p, axis=0)


def _xattn_sample(q, mk4, mv4, gs=4):
    ns = q.shape[0]
    q3 = q.astype(F32).reshape(ns, X_HEADS, X_HD)
    row = lambda i: (i, 0, 0)
    mem = lambda i: (i, 0, 0, 0)
    return pl.pallas_call(
        _xattn_sample_body,
        grid=(ns // gs,),
        in_specs=[pl.BlockSpec((gs, X_HEADS, X_HD), row), pl.BlockSpec((gs, N_MEM, X_HEADS, X_HD), mem),
                  pl.BlockSpec((gs, N_MEM, X_HEADS, X_HD), mem)],
        out_specs=pl.BlockSpec((gs, X_HEADS, X_HD), row),
        out_shape=jax.ShapeDtypeStruct((ns, X_HEADS, X_HD), F32),
        compiler_params=_params("parallel"),
        name="xattn_sample",
    )(q3, mk4, mv4)


def _resid_mm_body(x_ref, a_ref, w_ref, o_ref):
    o_ref[...] = x_ref[...] + jnp.dot(a_ref[...], w_ref[...], preferred_element_type=F32)


def _resid_mm(x, a, w):
    m = x.shape[0]
    return pl.pallas_call(
        _resid_mm_body,
        out_shape=jax.ShapeDtypeStruct((m, D_MODEL), F32),
        compiler_params=pltpu.CompilerParams(vmem_limit_bytes=VMEM_LIMIT),
        name="resid_mm",
    )(x, a, w)


def _ffn_body(x_ref, ng_ref, wg_ref, wu_ref, wd_ref, nf_ref, y_ref):
    x = x_ref[...]
    h = _rms(x, ng_ref[...]).astype(BF16)
    gate = jnp.dot(h, wg_ref[...], preferred_element_type=F32)
    up = jnp.dot(h, wu_ref[...], preferred_element_type=F32)
    x3 = x + _mm(_silu(gate) * up, wd_ref[...])
    y_ref[...] = _rms(x3, nf_ref[...])


def _ffn(x, ng, wg, wu, wd, nf, tm):
    m = x.shape[0]
    dff = wg.shape[1]
    row = lambda i: (i, 0)
    c2 = lambda i: (0, 0)
    single = pl.Buffered(1)
    return pl.pallas_call(
        _ffn_body,
        grid=(m // tm,),
        in_specs=[pl.BlockSpec((tm, D_MODEL), row), pl.BlockSpec((1, D_MODEL), c2),
                  pl.BlockSpec((D_MODEL, dff), c2, pipeline_mode=single),
                  pl.BlockSpec((D_MODEL, dff), c2, pipeline_mode=single),
                  pl.BlockSpec((dff, D_MODEL), c2, pipeline_mode=single),
                  pl.BlockSpec((1, D_MODEL), c2)],
        out_specs=pl.BlockSpec((tm, D_MODEL), row),
        out_shape=jax.ShapeDtypeStruct((m, D_MODEL), F32),
        compiler_params=_params("parallel"),
        name="ffn",
    )(x, ng, wg, wu, wd, nf)


def _rope_tables(pos):
    half = RET_DK // 2
    inv = ROPE_BASE ** (-jnp.arange(half, dtype=F32) / half)
    ang = pos.astype(F32)[:, None] * inv[None, :]
    cos, sin = jnp.cos(ang), jnp.sin(ang)
    return jnp.concatenate([cos, cos], axis=-1), jnp.concatenate([-sin, sin], axis=-1)


def _pad_lanes(v, offset):
    return jnp.zeros((BA_PAD,), F32).at[offset:offset + v.shape[0]].set(v)


def kernel(x_prompt, x_sample, state_ret, state_gdn, state_conv, cache_mem_k, cache_mem_v, mem_prompt,
           norm_mix_g, w_in, ret_gn_g, w_branch_a, gdn_conv_w, gdn_a_log, gdn_dt_bias, gdn_norm_g,
           w_branch_b, w_out, norm_x_g, mem_norm_g, w_xq, w_xk, w_xv, w_xo, norm_ffn_g, w_gate, w_up,
           w_down, norm_final_g):
    depth = w_in.shape[0]
    assert depth == 1, "single-layer kernel"
    b, t, _ = x_prompt.shape
    ns = x_sample.shape[0]
    l = 0

    w = w_in[l]
    ba0, g0 = COL_GATE, COL_GATE + 2 * GDN_HEADS
    w_rg = w[:, :ba0].astype(BF16)
    w_ret, w_gdn, w_gates = w_rg[:, :COL_GDN], w_rg[:, COL_GDN:], w[:, g0:].astype(BF16)
    w_ba =jnp.pad(w[:, ba0:g0], ((0, 0), (0, BA_PAD - 2 * GDN_HEADS))).astype(BF16)
    w_bat = w_ba.T
    row = lambda v: v.reshape(1, -1)
    wa, wb, wo = w_branch_a[l].astype(BF16), w_branch_b[l].astype(BF16), w_out[l].astype(BF16)
    wq, wk, wv, wxo = w_xq[l].astype(BF16), w_xk[l].astype(BF16), w_xv[l].astype(BF16), w_xo[l].astype(BF16)
    wg, wu, wd = w_gate[l].astype(BF16), w_up[l].astype(BF16), w_down[l].astype(BF16)
    alog_r = _pad_lanes(gdn_a_log[l], GDN_HEADS).reshape(1, BA_PAD)
    dt_r = _pad_lanes(gdn_dt_bias[l], GDN_HEADS).reshape(1, BA_PAD)
    alog_c, dt_c = alog_r.reshape(BA_PAD, 1), dt_r.reshape(BA_PAD, 1)
    cos_p, sin_p = _rope_tables(jnp.arange(t))
    cos_s, sin_s = _rope_tables(PAST_LEN + jnp.arange(1))

    xp = x_prompt.reshape(b * t, D_MODEL)
    mix_g = row(norm_mix_g[l])
    oa_p, sr_p = _retention_prompt(x_prompt, mix_g, w_ret, cos_p, sin_p, row(ret_gn_g[l]))
    ob_p, sg_p, sc_p = _gdn_prompt(x_prompt, mix_g, w_gdn, w_ba, w_bat, gdn_conv_w[l], alog_r, dt_r,
                                   alog_c, dt_c, row(gdn_norm_g[l]))
    x1_p, q_p = _merge(xp, mix_g, w_gates, oa_p.reshape(b * t, -1), ob_p.reshape(b * t, -1), wa, wb, wo,
                       row(norm_x_g[l]), wq, tm=512)
    mk_p, mv_p = _memkv(mem_prompt.reshape(b * N_MEM, D_MODEL), row(mem_norm_g[l]), wk, wv)
    x2_p = _xattn_prompt(q_p.reshape(b, t, D_MODEL), mk_p.reshape(b, N_MEM, D_MODEL),
                         mv_p.reshape(b, N_MEM, D_MODEL), x1_p.reshape(b, t, D_MODEL), wxo)
    y_p = _ffn(x2_p.reshape(b * t, D_MODEL), row(norm_ffn_g[l]), wg, wu, wd, row(norm_final_g), tm=256)

    xs = x_sample.reshape(ns, D_MODEL)
    proj_s, ba_s = _inproj(xs, mix_g, w_rg, tm=ns, w_ba=w_ba)
    oa_s, ob_s, sr_s, sg_s, sc_s = _sample_mix(proj_s, ba_s, state_ret[l], state_gdn[l], state_conv[l],
                                               cos_s, sin_s, gdn_conv_w[l], alog_r, dt_r, row(ret_gn_g[l]),
                                               row(gdn_norm_g[l]))
    x1_s, q_s = _merge(xs, mix_g, w_gates, oa_s.reshape(ns, -1), ob_s.reshape(ns, -1), wa, wb, wo,
                       row(norm_x_g[l]), wq, tm=ns)
    o_s = _xattn_sample(q_s, cache_mem_k[l], cache_mem_v[l])
    x2_s = _resid_mm(x1_s, o_s.reshape(ns, D_MODEL).astype(BF16), wxo)
    y_s = _ffn(x2_s, row(norm_ffn_g[l]), wg, wu, wd, row(norm_final_g), tm=ns)

    return (y_p.reshape(b, t, D_MODEL), y_s.reshape(ns, 1, D_MODEL),
            sr_p[None], sg_p[None], sc_p[None],
            mk_p.reshape(1, b, N_MEM, X_HEADS, X_HD), mv_p.reshape(1, b, N_MEM, X_HEADS, X_HD),
            sr_s[None], sg_s[None], sc_s[None])
```

```python
import functools

import numpy as np
import jax
import jax.numpy as jnp
from jax import lax
from jax.experimental import pallas as pl
from jax.experimental.pallas import tpu as pltpu

F32 = jnp.float32
BF16 = jnp.bfloat16

D_MODEL = 1024
RET_HEADS, RET_DK, RET_DV = 4, 128, 256
GDN_HEADS, GDN_DK, GDN_DV = 8, 128, 128
CONV_W = 4
CONV_CH = 3 * GDN_HEADS * GDN_DK
N_MEM, X_HEADS, X_HD = 256, 4, 256
PAST_LEN = 16384
ROPE_BASE = 10000.0
EPS = 1e-6
GDN_CHUNK = 64

COL_RET = 0
COL_GDN = 3072
COL_GATE = 7168
BA_PAD = 128

VMEM_LIMIT = 56 * 1024 * 1024

NT_DIMS = (((1,), (1,)), ((), ()))
TN_DIMS = (((0,), (0,)), ((), ()))


def _mm(a, b):
    return jnp.dot(a.astype(BF16), b.astype(BF16), preferred_element_type=F32)


def _mm_nt(a, b):
    return lax.dot_general(a.astype(BF16), b.astype(BF16), NT_DIMS, preferred_element_type=F32)


def _mm_tn(a, b):
    return lax.dot_general(a.astype(BF16), b.astype(BF16), TN_DIMS, preferred_element_type=F32)


def _mm_f32(a, b):
    return jnp.dot(a, b, preferred_element_type=F32, precision=lax.Precision.HIGHEST)


def _rms(x, g):
    return x * lax.rsqrt(jnp.mean(x * x, axis=-1, keepdims=True) + EPS) * g


def _silu(x):
    return x * jax.nn.sigmoid(x)


def _softplus(x):
    return jnp.maximum(x, 0.0) + jnp.log1p(jnp.exp(-jnp.abs(x)))


def _params(*sem):
    return pltpu.CompilerParams(dimension_semantics=sem, vmem_limit_bytes=VMEM_LIMIT)


def _inproj_body(with_ba, x_ref, g_ref, w_ref, *rest):
    if with_ba:
        wba_ref, o_ref, oba_ref, h_scr = rest
    else:
        o_ref, h_scr = rest

    @pl.when(pl.program_id(1) == 0)
    def _():
        hb = _rms(x_ref[...], g_ref[...]).astype(BF16)
        h_scr[...] = hb
        if with_ba:
            oba_ref[...] = jnp.dot(hb, wba_ref[...], preferred_element_type=F32)

    o_ref[...] = jnp.dot(h_scr[...], w_ref[...], preferred_element_type=F32)


def _inproj(x, g, w_main, tm, w_ba=None, tn=1024):
    m = x.shape[0]
    n = w_main.shape[1]
    in_specs = [
        pl.BlockSpec((tm, D_MODEL), lambda i, j: (i, 0)),
        pl.BlockSpec((1, D_MODEL), lambda i, j: (0, 0)),
        pl.BlockSpec((D_MODEL, tn), lambda i, j: (0, j)),
    ]
    out_specs = [pl.BlockSpec((tm, tn), lambda i, j: (i, j))]
    out_shape = [jax.ShapeDtypeStruct((m, n), F32)]
    args = [x, g, w_main]
    if w_ba is not None:
        in_specs.append(pl.BlockSpec((D_MODEL, BA_PAD), lambda i, j: (0, 0)))
        out_specs.append(pl.BlockSpec((tm, BA_PAD), lambda i, j: (i, 0)))
        out_shape.append(jax.ShapeDtypeStruct((m, BA_PAD), F32))
        args.append(w_ba)
    return pl.pallas_call(
        functools.partial(_inproj_body, w_ba is not None),
        grid=(m // tm, n // tn),
        in_specs=in_specs,
        out_specs=out_specs,
        out_shape=out_shape,
        scratch_shapes=[pltpu.VMEM((tm, D_MODEL), BF16)],
        compiler_params=_params("parallel", "arbitrary"),
        name="inproj",
    )(*args)


_RET_LOG_G = np.log1p(-np.exp2(-5.0 - np.arange(RET_HEADS, dtype=np.float64)))


def _ret_tables(c):
    idx = np.arange(c, dtype=np.float64)
    diff = idx[:, None] - idx[None, :]
    dmat = np.where(diff >= 0, np.exp(np.maximum(diff, 0.0)[None] * _RET_LOG_G[:, None, None]), 0.0)
    qdec = np.exp((idx + 1.0)[None, :] * _RET_LOG_G[:, None])
    kdec = np.exp((c - 1.0 - idx)[None, :] * _RET_LOG_G[:, None])
    lane = np.ones((1, 1, RET_DK))
    return (jnp.asarray(dmat, F32), jnp.asarray(qdec[:, :, None] * lane, F32),
            jnp.asarray(kdec[:, :, None] * lane, F32), [float(v) for v in np.exp(c * _RET_LOG_G)])


def _rot(x, cos, sin):
    return x * cos + pltpu.roll(x, RET_DK // 2, 1) * sin


def _group_norm_gate(o, gate, gn):
    mu = jnp.mean(o, axis=-1, keepdims=True)
    d = o - mu
    var = jnp.mean(d * d, axis=-1, keepdims=True)
    return _silu(gate) * (d * lax.rsqrt(var + EPS) * gn)


def _proj_tiles(hb, w_ref, dst, width=512):
    for c0 in range(0, w_ref.shape[1], width):
        dst[:, c0:c0 + width] = jnp.dot(hb, w_ref[:, c0:c0 + width], preferred_element_type=F32)


def _next_block(nt, nblocks):
    def index_map(i, j):
        n1 = jnp.minimum(i * nt + j + 1, nblocks - 1)
        return (n1 // nt, n1 % nt, 0)
    return index_map


def _ret_body(cdec, x0_ref, xn_ref, nmg_ref, w_ref, *rest):
    *consts, o_ref, s_out_ref, s_scr, pa, pb = rest
    t = pl.program_id(1)
    n = pl.program_id(0) * pl.num_programs(1) + t
    bufs = (pa, pb)

    @pl.when(t == 0)
    def _():
        s_scr[...] = jnp.zeros_like(s_scr)

    @pl.when(n == 0)
    def _():
        _proj_tiles(_rms(x0_ref[0], nmg_ref[...]).astype(BF16), w_ref, pa)

    for slot in range(2):
        @pl.when(n % 2 == slot)
        def _(slot=slot):
            _ret_step(cdec, t, xn_ref, nmg_ref, w_ref, bufs[slot], bufs[1 - slot], *consts, o_ref, s_out_ref, s_scr)


def _ret_step(cdec, t, xn_ref, nmg_ref, w_ref, proj, proj_next, cos_ref, sin_ref, dmat_ref, qdec_ref, kdec_ref,
              gn_ref, o_ref, s_out_ref, s_scr):
    qw = RET_HEADS * RET_DK
    _proj_tiles(_rms(xn_ref[0], nmg_ref[...]).astype(BF16), w_ref, proj_next)

    chunk = dmat_ref.shape[1]
    for c0 in range(0, proj.shape[0], chunk):
        rows = slice(c0, c0 + chunk)
        cos, sin = cos_ref[rows, :], sin_ref[rows, :]
        for h in range(RET_HEADS):
            qk = slice(h * RET_DK, (h + 1) * RET_DK)
            kk = slice(qw + h * RET_DK, qw + (h + 1) * RET_DK)
            vv = slice(h * RET_DV, (h + 1) * RET_DV)
            q = _rot(proj[rows, qk], cos, sin)
            k = _rot(proj[rows, kk], cos, sin) * (RET_DK ** -0.5)
            v = proj[rows, 2 * qw + h * RET_DV:2 * qw + (h + 1) * RET_DV]
            gate = proj[rows, 2 * qw + (RET_HEADS + h) * RET_DV:2 * qw + (RET_HEADS + h + 1) * RET_DV]
            s = s_scr[h]
            scores = _mm_nt(q, k) * dmat_ref[h]
            o = _mm(scores, v) + _mm(q * qdec_ref[h], s)
            s_scr[h] = cdec[h] * s + _mm_tn(k * kdec_ref[h], v)
            o_ref[0, rows, vv] = _group_norm_gate(o, gate, gn_ref[:, vv]).astype(BF16)

    @pl.when(t == pl.num_programs(1) - 1)
    def _():
        s_out_ref[0] = s_scr[...]


def _retention_prompt(x3, mix_g, w_ret, cos, sin, gn, tb=512, chunk=256):
    b, t, _ = x3.shape
    nt = t // tb
    dmat, qdec, kdec, cdec = _ret_tables(chunk)
    vw = RET_HEADS * RET_DV
    const3 = lambda i, j: (0, 0, 0)
    return pl.pallas_call(
        functools.partial(_ret_body, cdec),
        grid=(b, nt),
        in_specs=[
            pl.BlockSpec((1, tb, D_MODEL), const3),
            pl.BlockSpec((1, tb, D_MODEL), _next_block(nt, b * nt)),
            pl.BlockSpec((1, D_MODEL), lambda i, j: (0, 0)),
            pl.BlockSpec(w_ret.shape, lambda i, j: (0, 0), pipeline_mode=pl.Buffered(1)),
            pl.BlockSpec((tb, RET_DK), lambda i, j: (j, 0)),
            pl.BlockSpec((tb, RET_DK), lambda i, j: (j, 0)),
            pl.BlockSpec((RET_HEADS, chunk, chunk), const3),
            pl.BlockSpec((RET_HEADS, chunk, RET_DK), const3),
            pl.BlockSpec((RET_HEADS, chunk, RET_DK), const3),
            pl.BlockSpec((1, vw), lambda i, j: (0, 0)),
        ],
        out_specs=[
            pl.BlockSpec((1, tb, vw), lambda i, j: (i, j, 0)),
            pl.BlockSpec((1, RET_HEADS, RET_DK, RET_DV), lambda i, j: (i, 0, 0, 0)),
        ],
        out_shape=[
            jax.ShapeDtypeStruct((b, t, vw), BF16),
            jax.ShapeDtypeStruct((b, RET_HEADS, RET_DK, RET_DV), F32),
        ],
        scratch_shapes=[pltpu.VMEM((RET_HEADS, RET_DK, RET_DV), F32)]
        + 2 * [pltpu.VMEM((tb, w_ret.shape[1]), F32)],
        compiler_params=_params("arbitrary", "arbitrary"),
        name="retention_prompt",
    )(x3, x3, mix_g, w_ret, cos, sin, dmat, qdec, kdec, gn)


def _gdn_tables(tb):
    idx = np.arange(tb)
    same = (idx[:, None] // GDN_CHUNK) == (idx[None, :] // GDN_CHUNK)
    lower = same & (idx[:, None] >= idx[None, :])
    nchunk = tb // GDN_CHUNK
    chunk_sel = np.repeat((idx[:, None] // GDN_CHUNK) == np.arange(nchunk)[None, :], 128, axis=1)
    grp = 2 * GDN_CHUNK
    il = (idx % grp)[:, None]
    jl = np.arange(grp)[None, :]
    bias = np.where((il // GDN_CHUNK == jl // GDN_CHUNK) & (il >= jl), 0.0, -1e30)
    eye = (il == jl).astype(np.float64)
    return (jnp.asarray(lower, F32), jnp.asarray(lower.T, F32), jnp.asarray(same, F32),
            jnp.asarray(chunk_sel, F32), jnp.asarray(bias, F32), jnp.asarray(1.0 - eye, F32),
            jnp.asarray(eye, BF16))


def _gdn_prepare(x_ref, first_of_seq, nmg_ref, wqkv_ref, wz_ref, wba_ref, wbat_ref, cw_ref, xr, cs_buf, zs, bas,
                 bats, tb):
    st = {}

    def start():
        xr[0:8, :] = jnp.where(first_of_seq, 0.0, xr[tb:tb + 8, :])
        st["hb"] = _rms(x_ref[0], nmg_ref[...]).astype(BF16)

    def tile(w_ref, dst, c0, width=512):
        def run():
            dst[:, c0:c0 + width] = jnp.dot(st["hb"], w_ref[:, c0:c0 + width], preferred_element_type=F32)
        return run

    def logits():
        bas[...] = jnp.dot(st["hb"], wba_ref[...], preferred_element_type=F32)
        bats[...] = lax.dot_general(wbat_ref[...], st["hb"], NT_DIMS, preferred_element_type=F32)

    def conv(c0):
        def run():
            cs = slice(c0, c0 + GDN_DK)
            acc = xr[5:5 + tb, cs] * cw_ref[0:1, cs]
            for i in range(1, CONV_W):
                acc = acc + xr[5 + i:5 + i + tb, cs] * cw_ref[i:i + 1, cs]
            cs_buf[:, cs] = _silu(acc)
        return run

    mxu_items = ([start] + [tile(wqkv_ref, xr.at[8:8 + tb], c0) for c0 in range(0, CONV_CH, 512)]
                 + [tile(wz_ref, zs, c0) for c0 in range(0, wz_ref.shape[1], 512)] + [logits])
    vpu_items = [conv(c0) for c0 in range(0, CONV_CH, GDN_DK)]
    return mxu_items, vpu_items


class _Background:
    def __init__(self, items):
        self._items = list(items)

    def __call__(self, count):
        for item in self._items[:count]:
            item()
        del self._items[:count]

    def drain(self):
        self(len(self._items))


def _gdn_body(tb, x0_ref, xn_ref, nmg_ref, wqkv_ref, wz_ref, wba_ref, wbat_ref, cw_ref, *rest):
    *consts, o_ref, s_out_ref, conv_out_ref, s_scr, xr, ca, za, baa, bata, cb, zb, bab, batb = rest
    t = pl.program_id(1)
    nt = pl.num_programs(1)
    n = pl.program_id(0) * nt + t
    weights = (nmg_ref, wqkv_ref, wz_ref, wba_ref, wbat_ref, cw_ref)
    bufs = ((ca, za, baa, bata), (cb, zb, bab, batb))

    @pl.when(t == 0)
    def _():
        s_scr[...] = jnp.zeros_like(s_scr)

    @pl.when(n == 0)
    def _():
        mxu_items, vpu_items = _gdn_prepare(x0_ref, True, *weights, xr, *bufs[0], tb)
        _Background(mxu_items + vpu_items).drain()

    for slot in range(2):
        @pl.when(n % 2 == slot)
        def _(slot=slot):
            mxu_items, vpu_items = _gdn_prepare(xn_ref, t == nt - 1, *weights, xr, *bufs[1 - slot], tb)
            _gdn_step(tb, t, bufs[slot], *consts, o_ref, s_out_ref, s_scr,
                      _Background(mxu_items), _Background(vpu_items))

    @pl.when(t == nt - 2)
    def _():
        conv_out_ref[0] = xr[tb + 5:tb + 8, :]


def _gdn_step(tb, t, cur, alog_r_ref, dt_r_ref, alog_c_ref, dt_c_ref, ng_ref,
              lbd_ref, ubd_ref, obd_ref, csel_ref, bias_ref, offd_ref, eye_ref, o_ref, s_out_ref, s_scr,
              bg_mxu, bg_vpu):
    nchunk = tb // GDN_CHUNK
    hk = GDN_HEADS * GDN_DK
    c_scr, zz, ba_ref, bat_ref = cur
    ba, bat = ba_ref[...], bat_ref[...]

    bg_mxu(2)
    beta_c = jax.nn.sigmoid(ba)
    g_c = -jnp.exp(alog_r_ref[...]) * _softplus(ba + dt_r_ref[...])
    g_r = -jnp.exp(alog_c_ref[...]) * _softplus(bat + dt_c_ref[...])
    gc_c = _mm_f32(lbd_ref[...], g_c)
    gt_c = _mm_f32(obd_ref[...], g_c)
    gc_r = _mm_f32(g_r, ubd_ref[...])
    gt_l = _mm_f32(g_r, csel_ref[...])

    heads = range(GDN_HEADS)
    grp = 2 * GDN_CHUNK
    groups = [slice(p * grp, (p + 1) * grp) for p in range(tb // grp)]

    def grp_dot(a, b):
        return jnp.concatenate([jnp.dot(a[g], b[g], preferred_element_type=F32) for g in groups], axis=0)

    def grp_dot_nt(a, b):
        return jnp.concatenate([lax.dot_general(a[g], b[g], NT_DIMS, preferred_element_type=F32)
                                for g in groups], axis=0)

    bias = bias_ref[...]
    offdiag = offd_ref[...]
    eye_b = eye_ref[...]
    qs, ks, gammas, pbs, rhss, qgs, khs = [], [], [], [], [], [], []
    for h in heads:
        q = c_scr[:, h * GDN_DK:(h + 1) * GDN_DK]
        k = c_scr[:, hk + h * GDN_DK:hk + (h + 1) * GDN_DK]
        v = c_scr[:, 2 * hk + h * GDN_DV:2 * hk + (h + 1) * GDN_DV]
        q = q * lax.rsqrt(jnp.sum(q * q, axis=-1, keepdims=True) + EPS) * (GDN_DK ** -0.5)
        k = k * lax.rsqrt(jnp.sum(k * k, axis=-1, keepdims=True) + EPS)
        beta = beta_c[:, h:h + 1]
        gcc = gc_c[:, 8 + h:9 + h]
        gtc = gt_c[:, 8 + h:9 + h]
        gcr = gc_r[8 + h:9 + h, :]
        dg = jnp.concatenate([gcc[g] - gcr[:, g] for g in groups], axis=0)
        gamma = jnp.exp(dg + bias)
        kbeta = k * beta
        kb = k.astype(BF16)
        pbs.append((grp_dot_nt((-kbeta).astype(BF16), kb) * (gamma * offdiag)).astype(BF16))
        eg = jnp.exp(gcc)
        rhss.append(jnp.concatenate([v * beta, kbeta * eg], axis=1))
        qs.append(q.astype(BF16))
        ks.append(kb)
        gammas.append(gamma)
        qgs.append(q * eg)
        khs.append(k * jnp.exp(gtc - gcc))
        bg_mxu(1)
    bg_mxu.drain()

    def solve(_, carry):
        pb = pbs
        tbs = [pb[h] + eye_b for h in heads]
        for lvl in range(5):
            pb = [grp_dot(pb[h], pb[h]).astype(BF16) for h in heads]
            bg_vpu(1)
            tnew = [grp_dot(tbs[h], pb[h] + eye_b) for h in heads]
            bg_vpu(1)
            tbs = [x.astype(BF16) for x in tnew]
            bg_vpu(1)

        us, ws, qks = [], [], []
        for h in heads:
            uw = rhss[h] + grp_dot(tbs[h] - eye_b, rhss[h].astype(BF16))
            us.append(uw[:, :GDN_DV])
            ws.append(uw[:, GDN_DV:])
            qks.append((grp_dot_nt(qs[h], ks[h]) * gammas[h]).astype(BF16))
        bg_vpu(2)

        s = [s_scr[h] for h in heads]
        vn_parts = [[] for _ in heads]
        qs_parts = [[] for _ in heads]
        for c in range(nchunk):
            rows = slice(c * GDN_CHUNK, (c + 1) * GDN_CHUNK)
            for h in heads:
                wq = _mm(jnp.concatenate([ws[h][rows], qgs[h][rows]], axis=0), s[h])
                vn = us[h][rows] - wq[:GDN_CHUNK]
                qs_parts[h].append(wq[GDN_CHUNK:])
                vn_parts[h].append(vn)
                decay = jnp.exp(gt_l[8 + h:9 + h, c * 128:(c + 1) * 128])
                s[h] = decay * s[h] + _mm_tn(khs[h][rows], vn)
            bg_vpu(1)
        bg_vpu.drain()
        for h in heads:
            hs = slice(h * GDN_DV, (h + 1) * GDN_DV)
            s_scr[h] = s[h]
            vn = jnp.concatenate(vn_parts[h], axis=0).astype(BF16)
            o = jnp.concatenate(qs_parts[h], axis=0) + grp_dot(qks[h], vn)
            o_ref[0, :, hs] = (_rms(o, ng_ref[...]) * _silu(zz[:, hs])).astype(BF16)
        return carry

    lax.fori_loop(0, jnp.minimum(t + 1, 1), solve, 0)

    @pl.when(t == pl.num_programs(1) - 1)
    def _():
        s_out_ref[0] = s_scr[...]


def _gdn_prompt(x3, mix_g, w_gdn, w_ba, w_bat, conv_w, alog_r, dt_r, alog_c, dt_c, norm_g, tb=256):
    b, t, _ = x3.shape
    nt = t // tb
    assert t % tb == 0 and nt >= 2, "a sequence's last block must be prepared during one of its own steps"
    lbd, ubd, obd, csel, bias, offdiag, eye = _gdn_tables(tb)
    vw = GDN_HEADS * GDN_DV
    c2 = lambda i, j: (0, 0)
    single = pl.Buffered(1)
    return pl.pallas_call(
        functools.partial(_gdn_body, tb),
        grid=(b, nt),
        in_specs=[
            pl.BlockSpec((1, tb, D_MODEL), lambda i, j: (0, 0, 0)),
            pl.BlockSpec((1, tb, D_MODEL), _next_block(nt, b * nt)),
            pl.BlockSpec((1, D_MODEL), c2),
            pl.BlockSpec((D_MODEL, CONV_CH), c2, pipeline_mode=single),
            pl.BlockSpec((D_MODEL, vw), lambda i, j: (0, CONV_CH // vw), pipeline_mode=single),
            pl.BlockSpec((D_MODEL, BA_PAD), c2),
            pl.BlockSpec((BA_PAD, D_MODEL), c2),
            pl.BlockSpec((CONV_W, CONV_CH), c2),
            pl.BlockSpec((1, BA_PAD), c2),
            pl.BlockSpec((1, BA_PAD), c2),
            pl.BlockSpec((BA_PAD, 1), c2),
            pl.BlockSpec((BA_PAD, 1), c2),
            pl.BlockSpec((1, GDN_DV), c2),
            pl.BlockSpec((tb, tb), c2),
            pl.BlockSpec((tb, tb), c2),
            pl.BlockSpec((tb, tb), c2),
            pl.BlockSpec((tb, (tb // GDN_CHUNK) * 128), c2),
            pl.BlockSpec((tb, 2 * GDN_CHUNK), c2),
            pl.BlockSpec((tb, 2 * GDN_CHUNK), c2),
            pl.BlockSpec((tb, 2 * GDN_CHUNK), c2),
        ],
        out_specs=[
            pl.BlockSpec((1, tb, vw), lambda i, j: (i, j, 0)),
            pl.BlockSpec((1, GDN_HEADS, GDN_DK, GDN_DV), lambda i, j: (i, 0, 0, 0)),
            pl.BlockSpec((1, CONV_W - 1, CONV_CH), lambda i, j: (i, 0, 0)),
        ],
        out_shape=[
            jax.ShapeDtypeStruct((b, t, vw), BF16),
            jax.ShapeDtypeStruct((b, GDN_HEADS, GDN_DK, GDN_DV), F32),
            jax.ShapeDtypeStruct((b, CONV_W - 1, CONV_CH), F32),
        ],
        scratch_shapes=[pltpu.VMEM((GDN_HEADS, GDN_DK, GDN_DV), F32), pltpu.VMEM((tb + 8, CONV_CH), F32)]
        + 2 * [pltpu.VMEM((tb, CONV_CH), F32), pltpu.VMEM((tb, vw), F32),
               pltpu.VMEM((tb, BA_PAD), F32), pltpu.VMEM((BA_PAD, tb), F32)],
        compiler_params=_params("arbitrary", "arbitrary"),
        name="gdn_prompt",
    )(x3, x3, mix_g, w_gdn, w_gdn, w_ba, w_bat, conv_w, alog_r, dt_r, alog_c, dt_c, norm_g, lbd, ubd, obd, csel,
      bias, offdiag, eye)


def _sample_mix_body(cdec, ret_ref, gdn_ref, z_ref, ba_ref, sr_ref, sg_ref, sc_ref, cos_ref, sin_ref, cw_ref,
                     alog_ref, dt_ref, gn_ref, ng_ref,
                     oa_ref, ob_ref, sr_out_ref, sg_out_ref, sc_out_ref):
    hk = GDN_HEADS * GDN_DK
    samples = range(ret_ref.shape[0])
    cos, sin = cos_ref[...], sin_ref[...]
    rets, convs, zs, betas, egs = [], [], [], [], []
    for g in samples:
        ret = ret_ref[g]
        x_new = gdn_ref[g]
        buf = sc_ref[g]
        conv = x_new * cw_ref[CONV_W - 1:CONV_W, :]
        for i in range(CONV_W - 1):
            conv = conv + buf[i:i + 1, :] * cw_ref[i:i + 1, :]
        conv = _silu(conv)
        sc_out_ref[g, 0:2, :] = buf[1:3, :]
        sc_out_ref[g, 2:3, :] = x_new
        ba = ba_ref[g]
        betas.append(jax.nn.sigmoid(ba))
        egs.append(jnp.exp(-jnp.exp(alog_ref[...]) * _softplus(ba + dt_ref[...])))
        rets.append(ret)
        convs.append(conv)
        zs.append(z_ref[g])

    row8 = lax.broadcasted_iota(jnp.int32, (8, RET_DK), 0)

    def pad8(k, q):
        return jnp.where(row8 == 0, k, jnp.where(row8 == 1, q, 0.0)).astype(BF16)

    for h in range(RET_HEADS):
        vv = slice(h * RET_DV, (h + 1) * RET_DV)
        for g in samples:
            ret = rets[g]
            q = _rot(ret[:, h * RET_DK:(h + 1) * RET_DK], cos, sin)
            k = _rot(ret[:, 512 + h * RET_DK:512 + (h + 1) * RET_DK], cos, sin) * (RET_DK ** -0.5)
            v = ret[:, 1024 + h * RET_DV:1024 + (h + 1) * RET_DV]
            s = sr_ref[g, h]
            qs = jnp.dot(pad8(k, q), s.astype(BF16), preferred_element_type=F32)[1:2]
            v8 = jnp.broadcast_to(v, (8, RET_DV)).astype(BF16)
            sr_out_ref[g, h] = cdec[h] * s + lax.dot_general(pad8(k, 0.0), v8, TN_DIMS, preferred_element_type=F32)
            o = cdec[h] * qs + jnp.sum(q * k, axis=-1, keepdims=True) * v
            oa_ref[g, :, vv] = _group_norm_gate(o, ret[:, 2048 + h * RET_DV:2048 + (h + 1) * RET_DV],
                                                gn_ref[:, vv]).astype(BF16)

    for h in range(GDN_HEADS):
        hs = slice(h * GDN_DV, (h + 1) * GDN_DV)
        for g in samples:
            conv = convs[g]
            q = conv[:, h * GDN_DK:(h + 1) * GDN_DK]
            k = conv[:, hk + h * GDN_DK:hk + (h + 1) * GDN_DK]
            q = q * lax.rsqrt(jnp.sum(q * q, axis=-1, keepdims=True) + EPS) * (GDN_DK ** -0.5)
            k = k * lax.rsqrt(jnp.sum(k * k, axis=-1, keepdims=True) + EPS)
            v = conv[:, 2 * hk + h * GDN_DV:2 * hk + (h + 1) * GDN_DV]
            egh = egs[g][:, 8 + h:9 + h]
            s = sg_ref[g, h]
            kq_s = jnp.dot(pad8(k, q), s.astype(BF16), preferred_element_type=F32)
            vn = betas[g][:, h:h + 1] * (v - egh * kq_s[0:1])
            vn8 = jnp.broadcast_to(vn, (8, GDN_DV)).astype(BF16)
            sg_out_ref[g, h] = egh * s + lax.dot_general(pad8(k, 0.0), vn8, TN_DIMS, preferred_element_type=F32)
            o = egh * kq_s[1:2] + jnp.sum(q * k, axis=-1, keepdims=True) * vn
            ob_ref[g, :, hs] = (_rms(o, ng_ref[...]) * _silu(zs[g][:, hs])).astype(BF16)


def _sample_mix(proj_ret, proj_gdn, ba, state_ret, state_gdn, state_conv, cos, sin, conv_w, alog_r, dt_r, gn,
                norm_g, gs=4):
    ns = proj_ret.shape[0]
    pret3 = proj_ret.reshape(ns, 1, -1)
    pgdn3 = proj_gdn.reshape(ns, 1, -1)
    ba3 = ba.reshape(ns, 1, BA_PAD)
    cdec = [float(v) for v in np.exp(_RET_LOG_G)]
    vw = RET_HEADS * RET_DV
    gw = GDN_HEADS * GDN_DV
    c2 = lambda i: (0, 0)
    return pl.pallas_call(
        functools.partial(_sample_mix_body, cdec),
        grid=(ns // gs,),
        in_specs=[
            pl.BlockSpec((gs, 1, COL_GDN), lambda i: (i, 0, 0)),
            pl.BlockSpec((gs, 1, CONV_CH), lambda i: (i, 0, 0)),
            pl.BlockSpec((gs, 1, gw), lambda i: (i, 0, CONV_CH // gw)),
            pl.BlockSpec((gs, 1, BA_PAD), lambda i: (i, 0, 0)),
            pl.BlockSpec((gs, RET_HEADS, RET_DK, RET_DV), lambda i: (i, 0, 0, 0)),
            pl.BlockSpec((gs, GDN_HEADS, GDN_DK, GDN_DV), lambda i: (i, 0, 0, 0)),
            pl.BlockSpec((gs, CONV_W - 1, CONV_CH), lambda i: (i, 0, 0)),
            pl.BlockSpec((1, RET_DK), c2),
            pl.BlockSpec((1, RET_DK), c2),
            pl.BlockSpec((CONV_W, CONV_CH), c2),
            pl.BlockSpec((1, BA_PAD), c2),
            pl.BlockSpec((1, BA_PAD), c2),
            pl.BlockSpec((1, vw), c2),
            pl.BlockSpec((1, GDN_DV), c2),
        ],
        out_specs=[
            pl.BlockSpec((gs, 1, vw), lambda i: (i, 0, 0)),
            pl.BlockSpec((gs, 1, gw), lambda i: (i, 0, 0)),
            pl.BlockSpec((gs, RET_HEADS, RET_DK, RET_DV), lambda i: (i, 0, 0, 0)),
            pl.BlockSpec((gs, GDN_HEADS, GDN_DK, GDN_DV), lambda i: (i, 0, 0, 0)),
            pl.BlockSpec((gs, CONV_W - 1, CONV_CH), lambda i: (i, 0, 0)),
        ],
        out_shape=[
            jax.ShapeDtypeStruct((ns, 1, vw), BF16),
            jax.ShapeDtypeStruct((ns, 1, gw), BF16),
            jax.ShapeDtypeStruct(state_ret.shape, F32),
            jax.ShapeDtypeStruct(state_gdn.shape, F32),
            jax.ShapeDtypeStruct(state_conv.shape, F32),
        ],
        compiler_params=_params("parallel"),
        name="sample_mix",
    )(pret3, pgdn3, pgdn3, ba3, state_ret, state_gdn, state_conv, cos, sin, conv_w, alog_r, dt_r, gn, norm_g)


def _merge_body(x_ref, nmg_ref, wgt_ref, oa_ref, ob_ref, wa_ref, wb_ref, wo_ref, ng_ref, wq_ref, x1_ref, q_ref):
    x = x_ref[...]
    gates = jnp.dot(_rms(x, nmg_ref[...]).astype(BF16), wgt_ref[...], preferred_element_type=F32)
    ya = jnp.dot(oa_ref[...], wa_ref[...], preferred_element_type=F32)
    yb = jnp.dot(ob_ref[...], wb_ref[...], preferred_element_type=F32)
    merged = jax.nn.sigmoid(gates[:, :D_MODEL]) * ya + jax.nn.sigmoid(gates[:, D_MODEL:]) * yb
    x1 = x + _mm(merged, wo_ref[...])
    x1_ref[...] = x1
    q_ref[...] = _mm(_rms(x1, ng_ref[...]), wq_ref[...]).astype(BF16)


def _merge(x, mix_g, w_gates, oa, ob, wa, wb, wo, ng, wq, tm):
    m = x.shape[0]
    row = lambda i: (i, 0)
    c2 = lambda i: (0, 0)
    wspec = pl.BlockSpec((D_MODEL, D_MODEL), c2)
    return pl.pallas_call(
        _merge_body,
        grid=(m // tm,),
        in_specs=[
            pl.BlockSpec((tm, D_MODEL), row),
            pl.BlockSpec((1, D_MODEL), c2),
            pl.BlockSpec((D_MODEL, 2 * D_MODEL), c2),
            pl.BlockSpec((tm, D_MODEL), row),
            pl.BlockSpec((tm, D_MODEL), row),
            wspec, wspec, wspec,
            pl.BlockSpec((1, D_MODEL), c2),
            wspec,
        ],
        out_specs=[pl.BlockSpec((tm, D_MODEL), row), pl.BlockSpec((tm, D_MODEL), row)],
        out_shape=[jax.ShapeDtypeStruct((m, D_MODEL), F32), jax.ShapeDtypeStruct((m, D_MODEL), BF16)],
        compiler_params=_params("parallel"),
        name="merge",
    )(x, mix_g, w_gates, oa, ob, wa, wb, wo, ng, wq)


def _memkv_body(m_ref, g_ref, wk_ref, wv_ref, k_ref, v_ref):
    mn = _rms(m_ref[...], g_ref[...]).astype(BF16)
    k_ref[...] = jnp.dot(mn, wk_ref[...], preferred_element_type=F32)
    v_ref[...] = jnp.dot(mn, wv_ref[...], preferred_element_type=F32)


def _memkv(mem, g, wk, wv, tm=512):
    m = mem.shape[0]
    row = lambda i: (i, 0)
    c2 = lambda i: (0, 0)
    return pl.pallas_call(
        _memkv_body,
        grid=(m // tm,),
        in_specs=[pl.BlockSpec((tm, D_MODEL), row), pl.BlockSpec((1, D_MODEL), c2),
                  pl.BlockSpec((D_MODEL, D_MODEL), c2), pl.BlockSpec((D_MODEL, D_MODEL), c2)],
        out_specs=[pl.BlockSpec((tm, D_MODEL), row), pl.BlockSpec((tm, D_MODEL), row)],
        out_shape=[jax.ShapeDtypeStruct((m, D_MODEL), F32)] * 2,
        compiler_params=_params("parallel"),
        name="memkv",
    )(mem, g, wk, wv)


def _xattn_body(q_ref, mk_ref, mv_ref, x1_ref, wo_ref, x2_ref):
    parts = []
    for h in range(X_HEADS):
        hs = slice(h * X_HD, (h + 1) * X_HD)
        s = _mm_nt(q_ref[0, :, hs], mk_ref[0, :, hs]) * (X_HD ** -0.5)
        p = jnp.exp(s - jnp.max(s, axis=-1, keepdims=True))
        o = _mm(p, mv_ref[0, :, hs]) / jnp.sum(p, axis=-1, keepdims=True)
        parts.append(o.astype(BF16))
    x2_ref[0] = x1_ref[0] + jnp.dot(jnp.concatenate(parts, axis=1), wo_ref[...], preferred_element_type=F32)


def _xattn_prompt(q3, mk3, mv3, x13, wo, tq=512):
    b, t, _ = q3.shape
    tok = lambda i, j: (i, j, 0)
    mem = lambda i, j: (i, 0, 0)
    return pl.pallas_call(
        _xattn_body,
        grid=(b, t // tq),
        in_specs=[pl.BlockSpec((1, tq, D_MODEL), tok), pl.BlockSpec((1, N_MEM, D_MODEL), mem),
                  pl.BlockSpec((1, N_MEM, D_MODEL), mem), pl.BlockSpec((1, tq, D_MODEL), tok),
                  pl.BlockSpec((D_MODEL, D_MODEL), lambda i, j: (0, 0))],
        out_specs=pl.BlockSpec((1, tq, D_MODEL), tok),
        out_shape=jax.ShapeDtypeStruct((b, t, D_MODEL), F32),
        compiler_params=_params("parallel", "parallel"),
        name="xattn_prompt",
    )(q3, mk3, mv3, x13, wo)


def _xattn_sample_body(q_ref, mk_ref, mv_ref, o_ref):
    for g in range(q_ref.shape[0]):
        q = q_ref[g]
        s = jnp.sum(mk_ref[g] * q[None], axis=-1, keepdims=True) * (X_HD ** -0.5)
        p = jnp.exp(s - jnp.max(s, axis=0, keepdims=True))
        o_ref[g] = jnp.sum(p * mv_ref[g], axis=0) / jnp.sum(p, axis=0)


def _xattn_sample(q, mk4, mv4, gs=4):
    ns = q.shape[0]
    q3 = q.astype(F32).reshape(ns, X_HEADS, X_HD)
    row = lambda i: (i, 0, 0)
    mem = lambda i: (i, 0, 0, 0)
    return pl.pallas_call(
        _xattn_sample_body,
        grid=(ns // gs,),
        in_specs=[pl.BlockSpec((gs, X_HEADS, X_HD), row), pl.BlockSpec((gs, N_MEM, X_HEADS, X_HD), mem),
                  pl.BlockSpec((gs, N_MEM, X_HEADS, X_HD), mem)],
        out_specs=pl.BlockSpec((gs, X_HEADS, X_HD), row),
        out_shape=jax.ShapeDtypeStruct((ns, X_HEADS, X_HD), F32),
        compiler_params=_params("parallel"),
        name="xattn_sample",
    )(q3, mk4, mv4)


def _resid_mm_body(x_ref, a_ref, w_ref, o_ref):
    o_ref[...] = x_ref[...] + jnp.dot(a_ref[...], w_ref[...], preferred_element_type=F32)


def _resid_mm(x, a, w):
    m = x.shape[0]
    return pl.pallas_call(
        _resid_mm_body,
        out_shape=jax.ShapeDtypeStruct((m, D_MODEL), F32),
        compiler_params=pltpu.CompilerParams(vmem_limit_bytes=VMEM_LIMIT),
        name="resid_mm",
    )(x, a, w)


def _ffn_body(x_ref, ng_ref, wg_ref, wu_ref, wd_ref, nf_ref, y_ref):
    x = x_ref[...]
    h = _rms(x, ng_ref[...]).astype(BF16)
    gate = jnp.dot(h, wg_ref[...], preferred_element_type=F32)
    up = jnp.dot(h, wu_ref[...], preferred_element_type=F32)
    x3 = x + _mm(_silu(gate) * up, wd_ref[...])
    y_ref[...] = _rms(x3, nf_ref[...])


def _ffn(x, ng, wg, wu, wd, nf, tm):
    m = x.shape[0]
    dff = wg.shape[1]
    row = lambda i: (i, 0)
    c2 = lambda i: (0, 0)
    single = pl.Buffered(1)
    return pl.pallas_call(
        _ffn_body,
        grid=(m // tm,),
        in_specs=[pl.BlockSpec((tm, D_MODEL), row), pl.BlockSpec((1, D_MODEL), c2),
                  pl.BlockSpec((D_MODEL, dff), c2, pipeline_mode=single),
                  pl.BlockSpec((D_MODEL, dff), c2, pipeline_mode=single),
                  pl.BlockSpec((dff, D_MODEL), c2, pipeline_mode=single),
                  pl.BlockSpec((1, D_MODEL), c2)],
        out_specs=pl.BlockSpec((tm, D_MODEL), row),
        out_shape=jax.ShapeDtypeStruct((m, D_MODEL), F32),
        compiler_params=_params("parallel"),
        name="ffn",
    )(x, ng, wg, wu, wd, nf)


def _rope_tables(pos):
    half = RET_DK // 2
    inv = ROPE_BASE ** (-jnp.arange(half, dtype=F32) / half)
    ang = pos.astype(F32)[:, None] * inv[None, :]
    cos, sin = jnp.cos(ang), jnp.sin(ang)
    return jnp.concatenate([cos, cos], axis=-1), jnp.concatenate([-sin, sin], axis=-1)


def _pad_lanes(v, offset):
    return jnp.zeros((BA_PAD,), F32).at[offset:offset + v.shape[0]].set(v)


def kernel(x_prompt, x_sample, state_ret, state_gdn, state_conv, cache_mem_k, cache_mem_v, mem_prompt,
           norm_mix_g, w_in, ret_gn_g, w_branch_a, gdn_conv_w, gdn_a_log, gdn_dt_bias, gdn_norm_g,
           w_branch_b, w_out, norm_x_g, mem_norm_g, w_xq, w_xk, w_xv, w_xo, norm_ffn_g, w_gate, w_up,
           w_down, norm_final_g):
    depth = w_in.shape[0]
    assert depth == 1, "single-layer kernel"
    b, t, _ = x_prompt.shape
    ns = x_sample.shape[0]
    l = 0

    w = w_in[l]
    ba0, g0 = COL_GATE, COL_GATE + 2 * GDN_HEADS
    w_ret, w_gdn, w_gates = w[:, :COL_GDN].astype(BF16), w[:, COL_GDN:ba0].astype(BF16), w[:, g0:].astype(BF16)
    w_ba = jnp.pad(w[:, ba0:g0], ((0, 0), (0, BA_PAD - 2 * GDN_HEADS))).astype(BF16)
    w_bat = w_ba.T
    row = lambda v: v.reshape(1, -1)
    wa, wb, wo = w_branch_a[l].astype(BF16), w_branch_b[l].astype(BF16), w_out[l].astype(BF16)
    wq, wk, wv, wxo = w_xq[l].astype(BF16), w_xk[l].astype(BF16), w_xv[l].astype(BF16), w_xo[l].astype(BF16)
    wg, wu, wd = w_gate[l].astype(BF16), w_up[l].astype(BF16), w_down[l].astype(BF16)
    alog_r = _pad_lanes(gdn_a_log[l], GDN_HEADS).reshape(1, BA_PAD)
    dt_r = _pad_lanes(gdn_dt_bias[l], GDN_HEADS).reshape(1, BA_PAD)
    alog_c, dt_c = alog_r.reshape(BA_PAD, 1), dt_r.reshape(BA_PAD, 1)
    cos_p, sin_p = _rope_tables(jnp.arange(t))
    cos_s, sin_s = _rope_tables(PAST_LEN + jnp.arange(1))

    xp = x_prompt.reshape(b * t, D_MODEL)
    mix_g = row(norm_mix_g[l])
    oa_p, sr_p = _retention_prompt(x_prompt, mix_g, w_ret, cos_p, sin_p, row(ret_gn_g[l]))
    ob_p, sg_p, sc_p = _gdn_prompt(x_prompt, mix_g, w_gdn, w_ba, w_bat, gdn_conv_w[l], alog_r, dt_r,
                                   alog_c, dt_c, row(gdn_norm_g[l]))
    x1_p, q_p = _merge(xp, mix_g, w_gates, oa_p.reshape(b * t, -1), ob_p.reshape(b * t, -1), wa, wb, wo,
                       row(norm_x_g[l]), wq, tm=512)
    mk_p, mv_p = _memkv(mem_prompt.reshape(b * N_MEM, D_MODEL), row(mem_norm_g[l]), wk, wv)
    x2_p = _xattn_prompt(q_p.reshape(b, t, D_MODEL), mk_p.reshape(b, N_MEM, D_MODEL),
                         mv_p.reshape(b, N_MEM, D_MODEL), x1_p.reshape(b, t, D_MODEL), wxo)
    y_p = _ffn(x2_p.reshape(b * t, D_MODEL), row(norm_ffn_g[l]), wg, wu, wd, row(norm_final_g), tm=256)

    xs = x_sample.reshape(ns, D_MODEL)
    (pret_s,) = _inproj(xs, mix_g, w_ret, tm=ns)
    pgdn_s, ba_s = _inproj(xs, mix_g, w_gdn, tm=ns, w_ba=w_ba)
    oa_s, ob_s, sr_s, sg_s, sc_s = _sample_mix(pret_s, pgdn_s, ba_s, state_ret[l], state_gdn[l], state_conv[l],
                                               cos_s, sin_s, gdn_conv_w[l], alog_r, dt_r, row(ret_gn_g[l]),
                                               row(gdn_norm_g[l]))
    x1_s, q_s = _merge(xs, mix_g, w_gates, oa_s.reshape(ns, -1), ob_s.reshape(ns, -1), wa, wb, wo,
                       row(norm_x_g[l]), wq, tm=ns)
    o_s = _xattn_sample(q_s, cache_mem_k[l], cache_mem_v[l])
    x2_s = _resid_mm(x1_s, o_s.reshape(ns, D_MODEL).astype(BF16), wxo)
    y_s = _ffn(x2_s, row(norm_ffn_g[l]), wg, wu, wd, row(norm_final_g), tm=ns)

    return (y_p.reshape(b, t, D_MODEL), y_s.reshape(ns, 1, D_MODEL),
            sr_p[None], sg_p[None], sc_p[None],
            mk_p.reshape(1, b, N_MEM, X_HEADS, X_HD), mv_p.reshape(1, b, N_MEM, X_HEADS, X_HD),
            sr_s[None], sg_s[None], sc_s[None])
```

```python
import functools

import numpy as np
import jax
import jax.numpy as jnp
from jax import lax
from jax.experimental import pallas as pl
from jax.experimental.pallas import tpu as pltpu

F32 = jnp.float32
BF16 = jnp.bfloat16

D_MODEL = 1024
RET_HEADS, RET_DK, RET_DV = 4, 128, 256
GDN_HEADS, GDN_DK, GDN_DV = 8, 128, 128
CONV_W = 4
CONV_CH = 3 * GDN_HEADS * GDN_DK
N_MEM, X_HEADS, X_HD = 256, 4, 256
PAST_LEN = 16384
ROPE_BASE = 10000.0
EPS = 1e-6
GDN_CHUNK = 64

COL_RET = 0
COL_GDN = 3072
COL_GATE = 7168
BA_PAD = 128

VMEM_LIMIT = 56 * 1024 * 1024

NT_DIMS = (((1,), (1,)), ((), ()))
TN_DIMS = (((0,), (0,)), ((), ()))


def _mm(a, b):
    return jnp.dot(a.astype(BF16), b.astype(BF16), preferred_element_type=F32)


def _mm_nt(a, b):
    return lax.dot_general(a.astype(BF16), b.astype(BF16), NT_DIMS, preferred_element_type=F32)


def _mm_tn(a, b):
    return lax.dot_general(a.astype(BF16), b.astype(BF16), TN_DIMS, preferred_element_type=F32)


def _mm_f32(a, b):
    return jnp.dot(a, b, preferred_element_type=F32, precision=lax.Precision.HIGHEST)


def _rms(x, g):
    return x * lax.rsqrt(jnp.mean(x * x, axis=-1, keepdims=True) + EPS) * g


def _silu(x):
    return x * jax.nn.sigmoid(x)


def _softplus(x):
    return jnp.maximum(x, 0.0) + jnp.log1p(jnp.exp(-jnp.abs(x)))


def _params(*sem):
    return pltpu.CompilerParams(dimension_semantics=sem, vmem_limit_bytes=VMEM_LIMIT)


def _inproj_body(with_ba, x_ref, g_ref, w_ref, *rest):
    if with_ba:
        wba_ref, o_ref, oba_ref, h_scr = rest
    else:
        o_ref, h_scr = rest

    @pl.when(pl.program_id(1) == 0)
    def _():
        hb = _rms(x_ref[...], g_ref[...]).astype(BF16)
        h_scr[...] = hb
        if with_ba:
            oba_ref[...] = jnp.dot(hb, wba_ref[...], preferred_element_type=F32)

    o_ref[...] = jnp.dot(h_scr[...], w_ref[...], preferred_element_type=F32)


def _inproj(x, g, w_main, n, tm, w_ba=None, tn=1024):
    m = x.shape[0]
    assert n % tn == 0 and n <= w_main.shape[1]
    in_specs = [
        pl.BlockSpec((tm, D_MODEL), lambda i, j: (i, 0)),
        pl.BlockSpec((1, D_MODEL), lambda i, j: (0, 0)),
        pl.BlockSpec((D_MODEL, tn), lambda i, j: (0, j)),
    ]
    out_specs = [pl.BlockSpec((tm, tn), lambda i, j: (i, j))]
    out_shape = [jax.ShapeDtypeStruct((m, n), F32)]
    args = [x, g, w_main]
    if w_ba is not None:
        in_specs.append(pl.BlockSpec((D_MODEL, BA_PAD), lambda i, j: (0, 0)))
        out_specs.append(pl.BlockSpec((tm, BA_PAD), lambda i, j: (i, 0)))
        out_shape.append(jax.ShapeDtypeStruct((m, BA_PAD), F32))
        args.append(w_ba)
    return pl.pallas_call(
        functools.partial(_inproj_body, w_ba is not None),
        grid=(m // tm, n // tn),
        in_specs=in_specs,
        out_specs=out_specs,
        out_shape=out_shape,
        scratch_shapes=[pltpu.VMEM((tm, D_MODEL), BF16)],
        compiler_params=_params("parallel", "arbitrary"),
        name="inproj",
    )(*args)


_RET_LOG_G = np.log1p(-np.exp2(-5.0 - np.arange(RET_HEADS, dtype=np.float64)))


def _ret_tables(c):
    idx = np.arange(c, dtype=np.float64)
    diff = idx[:, None] - idx[None, :]
    dmat = np.where(diff >= 0, np.exp(np.maximum(diff, 0.0)[None] * _RET_LOG_G[:, None, None]), 0.0)
    qdec = np.exp((idx + 1.0)[None, :] * _RET_LOG_G[:, None])
    kdec = np.exp((c - 1.0 - idx)[None, :] * _RET_LOG_G[:, None])
    lane = np.ones((1, 1, RET_DK))
    return (jnp.asarray(dmat, F32), jnp.asarray(qdec[:, :, None] * lane, F32),
            jnp.asarray(kdec[:, :, None] * lane, F32), [float(v) for v in np.exp(c * _RET_LOG_G)])


def _rot(x, cos, sin):
    return x * cos + pltpu.roll(x, RET_DK // 2, 1) * sin


def _group_norm_gate(o, gate, gn):
    mu = jnp.mean(o, axis=-1, keepdims=True)
    d = o - mu
    var = jnp.mean(d * d, axis=-1, keepdims=True)
    return _silu(gate) * (d * lax.rsqrt(var + EPS) * gn)


def _proj_tiles(hb, w_ref, dst, width=512):
    for c0 in range(0, w_ref.shape[1], width):
        dst[:, c0:c0 + width] = jnp.dot(hb, w_ref[:, c0:c0 + width], preferred_element_type=F32)


def _next_block(nt, nblocks):
    def index_map(i, j):
        n1 = jnp.minimum(i * nt + j + 1, nblocks - 1)
        return (n1 // nt, n1 % nt, 0)
    return index_map


def _ret_body(cdec, x0_ref, xn_ref, nmg_ref, w_ref, *rest):
    *consts, o_ref, s_out_ref, s_scr, pa, pb = rest
    t = pl.program_id(1)
    n = pl.program_id(0) * pl.num_programs(1) + t
    bufs = (pa, pb)

    @pl.when(t == 0)
    def _():
        s_scr[...] = jnp.zeros_like(s_scr)

    @pl.when(n == 0)
    def _():
        _proj_tiles(_rms(x0_ref[0], nmg_ref[...]).astype(BF16), w_ref, pa)

    for slot in range(2):
        @pl.when(n % 2 == slot)
        def _(slot=slot):
            _ret_step(cdec, t, xn_ref, nmg_ref, w_ref, bufs[slot], bufs[1 - slot], *consts, o_ref, s_out_ref, s_scr)


def _ret_step(cdec, t, xn_ref, nmg_ref, w_ref, proj, proj_next, cos_ref, sin_ref, dmat_ref, qdec_ref, kdec_ref,
              gn_ref, o_ref, s_out_ref, s_scr):
    qw = RET_HEADS * RET_DK
    _proj_tiles(_rms(xn_ref[0], nmg_ref[...]).astype(BF16), w_ref, proj_next)

    chunk = dmat_ref.shape[1]
    for c0 in range(0, proj.shape[0], chunk):
        rows = slice(c0, c0 + chunk)
        cos, sin = cos_ref[rows, :], sin_ref[rows, :]
        for h in range(RET_HEADS):
            qk = slice(h * RET_DK, (h + 1) * RET_DK)
            kk = slice(qw + h * RET_DK, qw + (h + 1) * RET_DK)
            vv = slice(h * RET_DV, (h + 1) * RET_DV)
            q = _rot(proj[rows, qk], cos, sin)
            k = _rot(proj[rows, kk], cos, sin) * (RET_DK ** -0.5)
            v = proj[rows, 2 * qw + h * RET_DV:2 * qw + (h + 1) * RET_DV]
            gate = proj[rows, 2 * qw + (RET_HEADS + h) * RET_DV:2 * qw + (RET_HEADS + h + 1) * RET_DV]
            s = s_scr[h]
            scores = _mm_nt(q, k) * dmat_ref[h]
            o = _mm(scores, v) + _mm(q * qdec_ref[h], s)
            s_scr[h] = cdec[h] * s + _mm_tn(k * kdec_ref[h], v)
            o_ref[0, rows, vv] = _group_norm_gate(o, gate, gn_ref[:, vv]).astype(BF16)

    @pl.when(t == pl.num_programs(1) - 1)
    def _():
        s_out_ref[0] = s_scr[...]


def _retention_prompt(x3, mix_g, w_ret, cos, sin, gn, tb=512, chunk=256):
    b, t, _ = x3.shape
    nt = t // tb
    dmat, qdec, kdec, cdec = _ret_tables(chunk)
    vw = RET_HEADS * RET_DV
    const3 = lambda i, j: (0, 0, 0)
    return pl.pallas_call(
        functools.partial(_ret_body, cdec),
        grid=(b, nt),
        in_specs=[
            pl.BlockSpec((1, tb, D_MODEL), const3),
            pl.BlockSpec((1, tb, D_MODEL), _next_block(nt, b * nt)),
            pl.BlockSpec((1, D_MODEL), lambda i, j: (0, 0)),
            pl.BlockSpec((D_MODEL, COL_GDN), lambda i, j: (0, COL_RET // COL_GDN), pipeline_mode=pl.Buffered(1)),
            pl.BlockSpec((tb, RET_DK), lambda i, j: (j, 0)),
            pl.BlockSpec((tb, RET_DK), lambda i, j: (j, 0)),
            pl.BlockSpec((RET_HEADS, chunk, chunk), const3),
            pl.BlockSpec((RET_HEADS, chunk, RET_DK), const3),
            pl.BlockSpec((RET_HEADS, chunk, RET_DK), const3),
            pl.BlockSpec((1, vw), lambda i, j: (0, 0)),
        ],
        out_specs=[
            pl.BlockSpec((1, tb, vw), lambda i, j: (i, j, 0)),
            pl.BlockSpec((1, RET_HEADS, RET_DK, RET_DV), lambda i, j: (i, 0, 0, 0)),
        ],
        out_shape=[
            jax.ShapeDtypeStruct((b, t, vw), BF16),
            jax.ShapeDtypeStruct((b, RET_HEADS, RET_DK, RET_DV), F32),
        ],
        scratch_shapes=[pltpu.VMEM((RET_HEADS, RET_DK, RET_DV), F32)]
        + 2 * [pltpu.VMEM((tb, COL_GDN), F32)],
        compiler_params=_params("arbitrary", "arbitrary"),
        name="retention_prompt",
    )(x3, x3, mix_g, w_ret, cos, sin, dmat, qdec, kdec, gn)


def _gdn_tables(tb):
    idx = np.arange(tb)
    same = (idx[:, None] // GDN_CHUNK) == (idx[None, :] // GDN_CHUNK)
    lower = same & (idx[:, None] >= idx[None, :])
    nchunk = tb // GDN_CHUNK
    chunk_sel = np.repeat((idx[:, None] // GDN_CHUNK) == np.arange(nchunk)[None, :], 128, axis=1)
    grp = 2 * GDN_CHUNK
    il = (idx % grp)[:, None]
    jl = np.arange(grp)[None, :]
    bias = np.where((il // GDN_CHUNK == jl // GDN_CHUNK) & (il >= jl), 0.0, -1e30)
    eye = (il == jl).astype(np.float64)
    return (jnp.asarray(lower, F32), jnp.asarray(lower.T, F32), jnp.asarray(same, F32),
            jnp.asarray(chunk_sel, F32), jnp.asarray(bias, F32), jnp.asarray(1.0 - eye, F32),
            jnp.asarray(eye, BF16))


def _gdn_prepare(x_ref, first_of_seq, nmg_ref, wqkv_ref, wz_ref, wba_ref, wbat_ref, cw_ref, xr, cs_buf, zs, bas,
                 bats, tb):
    st = {}

    def start():
        xr[0:8, :] = jnp.where(first_of_seq, 0.0, xr[tb:tb + 8, :])
        st["hb"] = _rms(x_ref[0], nmg_ref[...]).astype(BF16)

    def tile(w_ref, dst, c0, width=512):
        def run():
            dst[:, c0:c0 + width] = jnp.dot(st["hb"], w_ref[:, c0:c0 + width], preferred_element_type=F32)
        return run

    def logits():
        bas[...] = jnp.dot(st["hb"], wba_ref[...], preferred_element_type=F32)
        bats[...] = lax.dot_general(wbat_ref[...], st["hb"], NT_DIMS, preferred_element_type=F32)

    def conv(c0):
        def run():
            cs = slice(c0, c0 + GDN_DK)
            acc = xr[5:5 + tb, cs] * cw_ref[0:1, cs]
            for i in range(1, CONV_W):
                acc = acc + xr[5 + i:5 + i + tb, cs] * cw_ref[i:i + 1, cs]
            cs_buf[:, cs] = _silu(acc)
        return run

    mxu_items = ([start] + [tile(wqkv_ref, xr.at[8:8 + tb], c0) for c0 in range(0, CONV_CH, 512)]
                 + [tile(wz_ref, zs, c0) for c0 in range(0, wz_ref.shape[1], 512)] + [logits])
    vpu_items = [conv(c0) for c0 in range(0, CONV_CH, GDN_DK)]
    return mxu_items, vpu_items


class _Background:
    def __init__(self, items):
        self._items = list(items)

    def __call__(self, count):
        for item in self._items[:count]:
            item()
        del self._items[:count]

    def drain(self):
        self(len(self._items))


def _gdn_body(tb, x0_ref, xn_ref, nmg_ref, wqkv_ref, wz_ref, wba_ref, wbat_ref, cw_ref, *rest):
    *consts, o_ref, s_out_ref, conv_out_ref, s_scr, xr, ca, za, baa, bata, cb, zb, bab, batb = rest
    t = pl.program_id(1)
    nt = pl.num_programs(1)
    n = pl.program_id(0) * nt + t
    weights = (nmg_ref, wqkv_ref, wz_ref, wba_ref, wbat_ref, cw_ref)
    bufs = ((ca, za, baa, bata), (cb, zb, bab, batb))

    @pl.when(t == 0)
    def _():
        s_scr[...] = jnp.zeros_like(s_scr)

    @pl.when(n == 0)
    def _():
        mxu_items, vpu_items = _gdn_prepare(x0_ref, True, *weights, xr, *bufs[0], tb)
        _Background(mxu_items + vpu_items).drain()

    for slot in range(2):
        @pl.when(n % 2 == slot)
        def _(slot=slot):
            mxu_items, vpu_items = _gdn_prepare(xn_ref, t == nt - 1, *weights, xr, *bufs[1 - slot], tb)
            _gdn_step(tb, t, bufs[slot], *consts, o_ref, s_out_ref, s_scr,
                      _Background(mxu_items), _Background(vpu_items))

    @pl.when(t == nt - 2)
    def _():
        conv_out_ref[0] = xr[tb + 5:tb + 8, :]


def _gdn_step(tb, t, cur, alog_r_ref, dt_r_ref, alog_c_ref, dt_c_ref, ng_ref,
              lbd_ref, ubd_ref, obd_ref, csel_ref, bias_ref, offd_ref, eye_ref, o_ref, s_out_ref, s_scr,
              bg_mxu, bg_vpu):
    nchunk = tb // GDN_CHUNK
    hk = GDN_HEADS * GDN_DK
    c_scr, zz, ba_ref, bat_ref = cur
    ba, bat = ba_ref[...], bat_ref[...]

    bg_mxu(2)
    beta_c = jax.nn.sigmoid(ba)
    g_c = -jnp.exp(alog_r_ref[...]) * _softplus(ba + dt_r_ref[...])
    g_r = -jnp.exp(alog_c_ref[...]) * _softplus(bat + dt_c_ref[...])
    gc_c = _mm_f32(lbd_ref[...], g_c)
    gt_c = _mm_f32(obd_ref[...], g_c)
    gc_r = _mm_f32(g_r, ubd_ref[...])
    gt_l = _mm_f32(g_r, csel_ref[...])

    heads = range(GDN_HEADS)
    grp = 2 * GDN_CHUNK
    groups = [slice(p * grp, (p + 1) * grp) for p in range(tb // grp)]

    def grp_dot(a, b):
        return jnp.concatenate([jnp.dot(a[g], b[g], preferred_element_type=F32) for g in groups], axis=0)

    def grp_dot_nt(a, b):
        return jnp.concatenate([lax.dot_general(a[g], b[g], NT_DIMS, preferred_element_type=F32)
                                for g in groups], axis=0)

    bias = bias_ref[...]
    offdiag = offd_ref[...]
    eye_b = eye_ref[...]
    qs, ks, gammas, pbs, rhss, qgs, khs = [], [], [], [], [], [], []
    for h in heads:
        q = c_scr[:, h * GDN_DK:(h + 1) * GDN_DK]
        k = c_scr[:, hk + h * GDN_DK:hk + (h + 1) * GDN_DK]
        v = c_scr[:, 2 * hk + h * GDN_DV:2 * hk + (h + 1) * GDN_DV]
        q = q * lax.rsqrt(jnp.sum(q * q, axis=-1, keepdims=True) + EPS) * (GDN_DK ** -0.5)
        k = k * lax.rsqrt(jnp.sum(k * k, axis=-1, keepdims=True) + EPS)
        beta = beta_c[:, h:h + 1]
        gcc = gc_c[:, 8 + h:9 + h]
        gtc = gt_c[:, 8 + h:9 + h]
        gcr = gc_r[8 + h:9 + h, :]
        dg = jnp.concatenate([gcc[g] - gcr[:, g] for g in groups], axis=0)
        gamma = jnp.exp(dg + bias)
        kbeta = k * beta
        kb = k.astype(BF16)
        pbs.append((grp_dot_nt((-kbeta).astype(BF16), kb) * (gamma * offdiag)).astype(BF16))
        eg = jnp.exp(gcc)
        rhss.append(jnp.concatenate([v * beta, kbeta * eg], axis=1))
        qs.append(q.astype(BF16))
        ks.append(kb)
        gammas.append(gamma)
        qgs.append(q * eg)
        khs.append(k * jnp.exp(gtc - gcc))
        bg_mxu(1)
    bg_mxu.drain()

    def solve(_, carry):
        pb = pbs
        tbs = [pb[h] + eye_b for h in heads]
        for lvl in range(5):
            pb = [grp_dot(pb[h], pb[h]).astype(BF16) for h in heads]
            bg_vpu(1)
            tnew = [grp_dot(tbs[h], pb[h] + eye_b) for h in heads]
            bg_vpu(1)
            tbs = [x.astype(BF16) for x in tnew]
            bg_vpu(1)

        us, ws, qks = [], [], []
        for h in heads:
            uw = rhss[h] + grp_dot(tbs[h] - eye_b, rhss[h].astype(BF16))
            us.append(uw[:, :GDN_DV])
            ws.append(uw[:, GDN_DV:])
            qks.append((grp_dot_nt(qs[h], ks[h]) * gammas[h]).astype(BF16))
        bg_vpu(2)

        s = [s_scr[h] for h in heads]
        vn_parts = [[] for _ in heads]
        qs_parts = [[] for _ in heads]
        for c in range(nchunk):
            rows = slice(c * GDN_CHUNK, (c + 1) * GDN_CHUNK)
            for h in heads:
                wq = _mm(jnp.concatenate([ws[h][rows], qgs[h][rows]], axis=0), s[h])
                vn = us[h][rows] - wq[:GDN_CHUNK]
                qs_parts[h].append(wq[GDN_CHUNK:])
                vn_parts[h].append(vn)
                decay = jnp.exp(gt_l[8 + h:9 + h, c * 128:(c + 1) * 128])
                s[h] = decay * s[h] + _mm_tn(khs[h][rows], vn)
            bg_vpu(1)
        bg_vpu.drain()
        for h in heads:
            hs = slice(h * GDN_DV, (h + 1) * GDN_DV)
            s_scr[h] = s[h]
            vn = jnp.concatenate(vn_parts[h], axis=0).astype(BF16)
            o = jnp.concatenate(qs_parts[h], axis=0) + grp_dot(qks[h], vn)
            o_ref[0, :, hs] = (_rms(o, ng_ref[...]) * _silu(zz[:, hs])).astype(BF16)
        return carry

    lax.fori_loop(0, jnp.minimum(t + 1, 1), solve, 0)

    @pl.when(t == pl.num_programs(1) - 1)
    def _():
        s_out_ref[0] = s_scr[...]


def _gdn_prompt(x3, mix_g, w_gdn, w_ba, w_bat, conv_w, alog_r, dt_r, alog_c, dt_c, norm_g, tb=256):
    b, t, _ = x3.shape
    nt = t // tb
    assert t % tb == 0 and nt >= 2, "a sequence's last block must be prepared during one of its own steps"
    lbd, ubd, obd, csel, bias, offdiag, eye = _gdn_tables(tb)
    vw = GDN_HEADS * GDN_DV
    c2 = lambda i, j: (0, 0)
    single = pl.Buffered(1)
    return pl.pallas_call(
        functools.partial(_gdn_body, tb),
        grid=(b, nt),
        in_specs=[
            pl.BlockSpec((1, tb, D_MODEL), lambda i, j: (0, 0, 0)),
            pl.BlockSpec((1, tb, D_MODEL), _next_block(nt, b * nt)),
            pl.BlockSpec((1, D_MODEL), c2),
            pl.BlockSpec((D_MODEL, CONV_CH), lambda i, j: (0, COL_GDN // CONV_CH), pipeline_mode=single),
            pl.BlockSpec((D_MODEL, vw), lambda i, j: (0, (COL_GDN + CONV_CH) // vw), pipeline_mode=single),
            pl.BlockSpec((D_MODEL, BA_PAD), c2),
            pl.BlockSpec((BA_PAD, D_MODEL), c2),
            pl.BlockSpec((CONV_W, CONV_CH), c2),
            pl.BlockSpec((1, BA_PAD), c2),
            pl.BlockSpec((1, BA_PAD), c2),
            pl.BlockSpec((BA_PAD, 1), c2),
            pl.BlockSpec((BA_PAD, 1), c2),
            pl.BlockSpec((1, GDN_DV), c2),
            pl.BlockSpec((tb, tb), c2),
            pl.BlockSpec((tb, tb), c2),
            pl.BlockSpec((tb, tb), c2),
            pl.BlockSpec((tb, (tb // GDN_CHUNK) * 128), c2),
            pl.BlockSpec((tb, 2 * GDN_CHUNK), c2),
            pl.BlockSpec((tb, 2 * GDN_CHUNK), c2),
            pl.BlockSpec((tb, 2 * GDN_CHUNK), c2),
        ],
        out_specs=[
            pl.BlockSpec((1, tb, vw), lambda i, j: (i, j, 0)),
            pl.BlockSpec((1, GDN_HEADS, GDN_DK, GDN_DV), lambda i, j: (i, 0, 0, 0)),
            pl.BlockSpec((1, CONV_W - 1, CONV_CH), lambda i, j: (i, 0, 0)),
        ],
        out_shape=[
            jax.ShapeDtypeStruct((b, t, vw), BF16),
            jax.ShapeDtypeStruct((b, GDN_HEADS, GDN_DK, GDN_DV), F32),
            jax.ShapeDtypeStruct((b, CONV_W - 1, CONV_CH), F32),
        ],
        scratch_shapes=[pltpu.VMEM((GDN_HEADS, GDN_DK, GDN_DV), F32), pltpu.VMEM((tb + 8, CONV_CH), F32)]
        + 2 * [pltpu.VMEM((tb, CONV_CH), F32), pltpu.VMEM((tb, vw), F32),
               pltpu.VMEM((tb, BA_PAD), F32), pltpu.VMEM((BA_PAD, tb), F32)],
        compiler_params=_params("arbitrary", "arbitrary"),
        name="gdn_prompt",
    )(x3, x3, mix_g, w_gdn, w_gdn, w_ba, w_bat, conv_w, alog_r, dt_r, alog_c, dt_c, norm_g, lbd, ubd, obd, csel,
      bias, offdiag, eye)


def _block_diag_rows(x, nblk):
    row = lax.broadcasted_iota(jnp.int32, (8, nblk * 128), 0)
    blk = lax.broadcasted_iota(jnp.int32, (8, nblk * 128), 1) // 128
    return jnp.where(row == blk, jnp.concatenate([x] * nblk, axis=1), 0.0)


def _sample_step_body(cdec, qk_ref, rv_ref, rg_ref, x_ref, z_ref, bcol_ref, sr_ref, sg_ref, sc_ref, cos_ref, sin_ref,
                      cw_ref, alog_ref, dt_ref, gn_ref, ng_ref,
                      oa_ref, ob_ref, sr_out_ref, sg_out_ref, sc_out_ref):
    nh = GDN_HEADS
    cos, sin = cos_ref[...], sin_ref[...]
    gam = jnp.where(lax.broadcasted_iota(jnp.int32, (8, 1), 0) == 0, cdec[0], 0.0)
    for h in range(1, RET_HEADS):
        gam = jnp.where(lax.broadcasted_iota(jnp.int32, (8, 1), 0) == h, cdec[h], gam)
    zero4 = jnp.zeros((RET_HEADS, RET_DV), F32)
    for g in range(x_ref.shape[0]):
        x_new = x_ref[g]
        buf = sc_ref[g]
        conv = x_new * cw_ref[CONV_W - 1]
        for i in range(CONV_W - 1):
            conv = conv + buf[i] * cw_ref[i]
        conv = _silu(conv)
        sc_out_ref[g, 0] = buf[1]
        sc_out_ref[g, 1] = buf[2]
        sc_out_ref[g, 2] = x_new
        q, k, v = conv[0:nh], conv[nh:2 * nh], conv[2 * nh:3 * nh]
        q = q * lax.rsqrt(jnp.sum(q * q, axis=-1, keepdims=True) + EPS) * (GDN_DK ** -0.5)
        k = k * lax.rsqrt(jnp.sum(k * k, axis=-1, keepdims=True) + EPS)
        col = bcol_ref[g]
        beta = jax.nn.sigmoid(col[0:nh])
        eg = jnp.exp(-jnp.exp(alog_ref[...]) * _softplus(col[nh:2 * nh] + dt_ref[...]))
        kbd, qbd = _block_diag_rows(k, nh), _block_diag_rows(q, nh)
        s_flat = sg_ref[g].reshape(nh * GDN_DK, GDN_DV)
        kq_s = _mm(jnp.concatenate([kbd, qbd], axis=0), s_flat)
        vn = beta * (v - eg * kq_s[0:nh])
        o = eg * kq_s[nh:2 * nh] + jnp.sum(q * k, axis=-1, keepdims=True) * vn
        ob_ref[g] = _rms(o, ng_ref[...]) * _silu(z_ref[g])
        upd = _mm_tn(kbd, vn)
        eg_l = jnp.broadcast_to(eg, (nh, GDN_DV))
        for h in range(nh):
            sg_out_ref[g, h] = eg_l[h:h + 1] * sg_ref[g, h] + upd[h * GDN_DK:(h + 1) * GDN_DK]

        qk = _rot(qk_ref[g], cos, sin)
        rq = jnp.concatenate([qk[0:RET_HEADS], qk[0:RET_HEADS]], axis=0)
        rk = jnp.concatenate([qk[RET_HEADS:], qk[RET_HEADS:]], axis=0) * (RET_DK ** -0.5)
        rv = jnp.concatenate([rv_ref[g], zero4], axis=0)
        qbd_r, kbd_r = _block_diag_rows(rq, RET_HEADS), _block_diag_rows(rk, RET_HEADS)
        sr_flat = sr_ref[g].reshape(RET_HEADS * RET_DK, RET_DV)
        q_s = _mm(qbd_r, sr_flat)
        o_r = gam * q_s + jnp.sum(rq * rk, axis=-1, keepdims=True) * rv
        oa_ref[g] = _group_norm_gate(o_r[0:RET_HEADS], rg_ref[g], gn_ref[...])
        upd_r = _mm_tn(kbd_r, rv)
        for h in range(RET_HEADS):
            sr_out_ref[g, h] = cdec[h] * sr_ref[g, h] + upd_r[h * RET_DK:(h + 1) * RET_DK]


def _sample_step(proj_ret, proj_gdn, ba, state_ret, state_gdn, state_conv, cos, sin, conv_w, a_log, dt_bias, gn,
                 norm_g, gs=8):
    ns = proj_ret.shape[0]
    nh, rh = GDN_HEADS, RET_HEADS
    nq = rh * RET_DK
    qk3 = proj_ret[:, :2 * nq].reshape(ns, 2 * rh, RET_DK)
    rv3 = proj_ret[:, 2 * nq:2 * nq + rh * RET_DV].reshape(ns, rh, RET_DV)
    rg3 = proj_ret[:, 2 * nq + rh * RET_DV:].reshape(ns, rh, RET_DV)
    x3 = proj_gdn[:, :CONV_CH].reshape(ns, 3 * nh, GDN_DK)
    z3 = proj_gdn[:, CONV_CH:].reshape(ns, nh, GDN_DV)
    bcol = ba[:, :2 * nh].reshape(ns, 2 * nh, 1)
    sc4 = state_conv.reshape(ns, CONV_W - 1, 3 * nh, GDN_DK)
    cw3 = conv_w.reshape(CONV_W, 3 * nh, GDN_DK)
    cdec = [float(v) for v in np.exp(_RET_LOG_G)]
    c2 = lambda i: (0, 0)
    c3 = lambda i: (0, 0, 0)
    b3 = lambda i: (i, 0, 0)
    b4 = lambda i: (i, 0, 0, 0)
    oa, ob, sr, sg, sc = pl.pallas_call(
        functools.partial(_sample_step_body, cdec),
        grid=(ns // gs,),
        in_specs=[
            pl.BlockSpec((gs, 2 * rh, RET_DK), b3),
            pl.BlockSpec((gs, rh, RET_DV), b3),
            pl.BlockSpec((gs, rh, RET_DV), b3),
            pl.BlockSpec((gs, 3 * nh, GDN_DK), b3),
            pl.BlockSpec((gs, nh, GDN_DV), b3),
            pl.BlockSpec((gs, 2 * nh, 1), b3),
            pl.BlockSpec((gs, rh, RET_DK, RET_DV), b4),
            pl.BlockSpec((gs, nh, GDN_DK, GDN_DV), b4),
            pl.BlockSpec((gs, CONV_W - 1, 3 * nh, GDN_DK), b4),
            pl.BlockSpec((1, RET_DK), c2),
            pl.BlockSpec((1, RET_DK), c2),
            pl.BlockSpec((CONV_W, 3 * nh, GDN_DK), c3),
            pl.BlockSpec((nh, 1), c2),
            pl.BlockSpec((nh, 1), c2),
            pl.BlockSpec((rh, RET_DV), c2),
            pl.BlockSpec((1, GDN_DV), c2),
        ],
        out_specs=[
            pl.BlockSpec((gs, rh, RET_DV), b3),
            pl.BlockSpec((gs, nh, GDN_DV), b3),
            pl.BlockSpec((gs, rh, RET_DK, RET_DV), b4),
            pl.BlockSpec((gs, nh, GDN_DK, GDN_DV), b4),
            pl.BlockSpec((gs, CONV_W - 1, 3 * nh, GDN_DK), b4),
        ],
        out_shape=[
            jax.ShapeDtypeStruct((ns, rh, RET_DV), F32),
            jax.ShapeDtypeStruct((ns, nh, GDN_DV), F32),
            jax.ShapeDtypeStruct(state_ret.shape, F32),
            jax.ShapeDtypeStruct(state_gdn.shape, F32),
            jax.ShapeDtypeStruct(sc4.shape, F32),
        ],
        compiler_params=_params("parallel"),
        name="sample_step",
    )(qk3, rv3, rg3, x3, z3, bcol, state_ret, state_gdn, sc4, cos, sin, cw3, a_log.reshape(nh, 1),
      dt_bias.reshape(nh, 1), gn.reshape(rh, RET_DV), norm_g)
    return (oa.reshape(ns, rh * RET_DV).astype(BF16), ob.reshape(ns, nh * GDN_DV).astype(BF16), sr, sg,
            sc.reshape(state_conv.shape))


def _merge_body(x_ref, nmg_ref, wgt_ref, oa_ref, ob_ref, wa_ref, wb_ref, wo_ref, ng_ref, wq_ref, x1_ref, q_ref):
    x = x_ref[...]
    gates = jnp.dot(_rms(x, nmg_ref[...]).astype(BF16), wgt_ref[...], preferred_element_type=F32)
    ya = jnp.dot(oa_ref[...], wa_ref[...], preferred_element_type=F32)
    yb = jnp.dot(ob_ref[...], wb_ref[...], preferred_element_type=F32)
    merged = jax.nn.sigmoid(gates[:, :D_MODEL]) * ya + jax.nn.sigmoid(gates[:, D_MODEL:]) * yb
    x1 = x + _mm(merged, wo_ref[...])
    x1_ref[...] = x1
    q_ref[...] = _mm(_rms(x1, ng_ref[...]), wq_ref[...]).astype(BF16)


def _merge(x, mix_g, w_gates, oa, ob, wa, wb, wo, ng, wq, tm):
    m = x.shape[0]
    row = lambda i: (i, 0)
    c2 = lambda i: (0, 0)
    wspec = pl.BlockSpec((D_MODEL, D_MODEL), c2)
    return pl.pallas_call(
        _merge_body,
        grid=(m // tm,),
        in_specs=[
            pl.BlockSpec((tm, D_MODEL), row),
            pl.BlockSpec((1, D_MODEL), c2),
            pl.BlockSpec((D_MODEL, 2 * D_MODEL), c2),
            pl.BlockSpec((tm, D_MODEL), row),
            pl.BlockSpec((tm, D_MODEL), row),
            wspec, wspec, wspec,
            pl.BlockSpec((1, D_MODEL), c2),
            wspec,
        ],
        out_specs=[pl.BlockSpec((tm, D_MODEL), row), pl.BlockSpec((tm, D_MODEL), row)],
        out_shape=[jax.ShapeDtypeStruct((m, D_MODEL), F32), jax.ShapeDtypeStruct((m, D_MODEL), BF16)],
        compiler_params=_params("parallel"),
        name="merge",
    )(x, mix_g, w_gates, oa, ob, wa, wb, wo, ng, wq)


def _memkv_body(m_ref, g_ref, wk_ref, wv_ref, k_ref, v_ref):
    mn = _rms(m_ref[...], g_ref[...]).astype(BF16)
    k_ref[...] = jnp.dot(mn, wk_ref[...], preferred_element_type=F32)
    v_ref[...] = jnp.dot(mn, wv_ref[...], preferred_element_type=F32)


def _memkv(mem, g, wk, wv, tm=512):
    m = mem.shape[0]
    row = lambda i: (i, 0)
    c2 = lambda i: (0, 0)
    return pl.pallas_call(
        _memkv_body,
        grid=(m // tm,),
        in_specs=[pl.BlockSpec((tm, D_MODEL), row), pl.BlockSpec((1, D_MODEL), c2),
                  pl.BlockSpec((D_MODEL, D_MODEL), c2), pl.BlockSpec((D_MODEL, D_MODEL), c2)],
        out_specs=[pl.BlockSpec((tm, D_MODEL), row), pl.BlockSpec((tm, D_MODEL), row)],
        out_shape=[jax.ShapeDtypeStruct((m, D_MODEL), F32)] * 2,
        compiler_params=_params("parallel"),
        name="memkv",
    )(mem, g, wk, wv)


def _xattn_body(q_ref, mk_ref, mv_ref, x1_ref, wo_ref, x2_ref):
    parts = []
    for h in range(X_HEADS):
        hs = slice(h * X_HD, (h + 1) * X_HD)
        s = _mm_nt(q_ref[0, :, hs], mk_ref[0, :, hs]) * (X_HD ** -0.5)
        p = jnp.exp(s - jnp.max(s, axis=-1, keepdims=True))
        o = _mm(p, mv_ref[0, :, hs]) / jnp.sum(p, axis=-1, keepdims=True)
        parts.append(o.astype(BF16))
    x2_ref[0] = x1_ref[0] + jnp.dot(jnp.concatenate(parts, axis=1), wo_ref[...], preferred_element_type=F32)


def _xattn_prompt(q3, mk3, mv3, x13, wo, tq=512):
    b, t, _ = q3.shape
    tok = lambda i, j: (i, j, 0)
    mem = lambda i, j: (i, 0, 0)
    return pl.pallas_call(
        _xattn_body,
        grid=(b, t // tq),
        in_specs=[pl.BlockSpec((1, tq, D_MODEL), tok), pl.BlockSpec((1, N_MEM, D_MODEL), mem),
                  pl.BlockSpec((1, N_MEM, D_MODEL), mem), pl.BlockSpec((1, tq, D_MODEL), tok),
                  pl.BlockSpec((D_MODEL, D_MODEL), lambda i, j: (0, 0))],
        out_specs=pl.BlockSpec((1, tq, D_MODEL), tok),
        out_shape=jax.ShapeDtypeStruct((b, t, D_MODEL), F32),
        compiler_params=_params("parallel", "parallel"),
        name="xattn_prompt",
    )(q3, mk3, mv3, x13, wo)


def _xattn_sample_body(q_ref, mk_ref, mv_ref, o_ref):
    for g in range(q_ref.shape[0]):
        q = q_ref[g]
        s = jnp.sum(mk_ref[g] * q[None], axis=-1, keepdims=True) * (X_HD ** -0.5)
        p = jnp.exp(s - jnp.max(s, axis=0, keepdims=True))
        o_ref[g] = jnp.sum(p * mv_ref[g], axis=0) / jnp.sum(p, axis=0)


def _xattn_sample(q, mk4, mv4, gs=4):
    ns = q.shape[0]
    q3 = q.astype(F32).reshape(ns, X_HEADS, X_HD)
    row = lambda i: (i, 0, 0)
    mem = lambda i: (i, 0, 0, 0)
    return pl.pallas_call(
        _xattn_sample_body,
        grid=(ns // gs,),
        in_specs=[pl.BlockSpec((gs, X_HEADS, X_HD), row), pl.BlockSpec((gs, N_MEM, X_HEADS, X_HD), mem),
                  pl.BlockSpec((gs, N_MEM, X_HEADS, X_HD), mem)],
        out_specs=pl.BlockSpec((gs, X_HEADS, X_HD), row),
        out_shape=jax.ShapeDtypeStruct((ns, X_HEADS, X_HD), F32),
        compiler_params=_params("parallel"),
        name="xattn_sample",
    )(q3, mk4, mv4)


def _resid_mm_body(x_ref, a_ref, w_ref, o_ref):
    o_ref[...] = x_ref[...] + jnp.dot(a_ref[...], w_ref[...], preferred_element_type=F32)


def _resid_mm(x, a, w):
    m = x.shape[0]
    return pl.pallas_call(
        _resid_mm_body,
        out_shape=jax.ShapeDtypeStruct((m, D_MODEL), F32),
        compiler_params=pltpu.CompilerParams(vmem_limit_bytes=VMEM_LIMIT),
        name="resid_mm",
    )(x, a, w)


def _ffn_body(x_ref, ng_ref, wg_ref, wu_ref, wd_ref, nf_ref, y_ref):
    x = x_ref[...]
    h = _rms(x, ng_ref[...]).astype(BF16)
    gate = jnp.dot(h, wg_ref[...], preferred_element_type=F32)
    up = jnp.dot(h, wu_ref[...], preferred_element_type=F32)
    x3 = x + _mm(_silu(gate) * up, wd_ref[...])
    y_ref[...] = _rms(x3, nf_ref[...])


def _ffn(x, ng, wg, wu, wd, nf, tm):
    m = x.shape[0]
    dff = wg.shape[1]
    row = lambda i: (i, 0)
    c2 = lambda i: (0, 0)
    single = pl.Buffered(1)
    return pl.pallas_call(
        _ffn_body,
        grid=(m // tm,),
        in_specs=[pl.BlockSpec((tm, D_MODEL), row), pl.BlockSpec((1, D_MODEL), c2),
                  pl.BlockSpec((D_MODEL, dff), c2, pipeline_mode=single),
                  pl.BlockSpec((D_MODEL, dff), c2, pipeline_mode=single),
                  pl.BlockSpec((dff, D_MODEL), c2, pipeline_mode=single),
                  pl.BlockSpec((1, D_MODEL), c2)],
        out_specs=pl.BlockSpec((tm, D_MODEL), row),
        out_shape=jax.ShapeDtypeStruct((m, D_MODEL), F32),
        compiler_params=_params("parallel"),
        name="ffn",
    )(x, ng, wg, wu, wd, nf)


def _rope_tables(pos):
    half = RET_DK // 2
    inv = ROPE_BASE ** (-jnp.arange(half, dtype=F32) / half)
    ang = pos.astype(F32)[:, None] * inv[None, :]
    cos, sin = jnp.cos(ang), jnp.sin(ang)
    return jnp.concatenate([cos, cos], axis=-1), jnp.concatenate([-sin, sin], axis=-1)


def _pad_lanes(v, offset):
    return jnp.zeros((BA_PAD,), F32).at[offset:offset + v.shape[0]].set(v)


def kernel(x_prompt, x_sample, state_ret, state_gdn, state_conv, cache_mem_k, cache_mem_v, mem_prompt,
           norm_mix_g, w_in, ret_gn_g, w_branch_a, gdn_conv_w, gdn_a_log, gdn_dt_bias, gdn_norm_g,
           w_branch_b, w_out, norm_x_g, mem_norm_g, w_xq, w_xk, w_xv, w_xo, norm_ffn_g, w_gate, w_up,
           w_down, norm_final_g):
    depth = w_in.shape[0]
    assert depth == 1, "single-layer kernel"
    b, t, _ = x_prompt.shape
    ns = x_sample.shape[0]
    l = 0

    w = w_in[l]
    ba0, g0 = COL_GATE, COL_GATE + 2 * GDN_HEADS
    w_bf = w.astype(BF16)
    w_gates = w_bf[:, g0:]
    w_ba = jnp.pad(w[:, ba0:g0], ((0, 0), (0, BA_PAD - 2 * GDN_HEADS))).astype(BF16)
    w_bat = w_ba.T
    row = lambda v: v.reshape(1, -1)
    wa, wb, wo = w_branch_a[l].astype(BF16), w_branch_b[l].astype(BF16), w_out[l].astype(BF16)
    wq, wk, wv, wxo = w_xq[l].astype(BF16), w_xk[l].astype(BF16), w_xv[l].astype(BF16), w_xo[l].astype(BF16)
    wg, wu, wd = w_gate[l].astype(BF16), w_up[l].astype(BF16), w_down[l].astype(BF16)
    alog_r = _pad_lanes(gdn_a_log[l], GDN_HEADS).reshape(1, BA_PAD)
    dt_r = _pad_lanes(gdn_dt_bias[l], GDN_HEADS).reshape(1, BA_PAD)
    alog_c, dt_c = alog_r.reshape(BA_PAD, 1), dt_r.reshape(BA_PAD, 1)
    cos_p, sin_p = _rope_tables(jnp.arange(t))
    cos_s, sin_s = _rope_tables(PAST_LEN + jnp.arange(1))

    xp = x_prompt.reshape(b * t, D_MODEL)
    mix_g = row(norm_mix_g[l])
    oa_p, sr_p = _retention_prompt(x_prompt, mix_g, w_bf, cos_p, sin_p, row(ret_gn_g[l]))
    ob_p, sg_p, sc_p = _gdn_prompt(x_prompt, mix_g, w_bf, w_ba, w_bat, gdn_conv_w[l], alog_r, dt_r,
                                   alog_c, dt_c, row(gdn_norm_g[l]))
    x1_p, q_p = _merge(xp, mix_g, w_gates, oa_p.reshape(b * t, -1), ob_p.reshape(b * t, -1), wa, wb, wo,
                       row(norm_x_g[l]), wq, tm=512)
    mk_p, mv_p = _memkv(mem_prompt.reshape(b * N_MEM, D_MODEL), row(mem_norm_g[l]), wk, wv)
    x2_p = _xattn_prompt(q_p.reshape(b, t, D_MODEL), mk_p.reshape(b, N_MEM, D_MODEL),
                         mv_p.reshape(b, N_MEM, D_MODEL), x1_p.reshape(b, t, D_MODEL), wxo)
    y_p = _ffn(x2_p.reshape(b * t, D_MODEL), row(norm_ffn_g[l]), wg, wu, wd, row(norm_final_g), tm=512)

    xs = x_sample.reshape(ns, D_MODEL)
    proj_s, ba_s = _inproj(xs, mix_g, w_bf, COL_GATE, tm=ns, w_ba=w_ba)
    oa_s, ob_s, sr_s, sg_s, sc_s = _sample_step(proj_s[:, :COL_GDN], proj_s[:, COL_GDN:], ba_s, state_ret[l], state_gdn[l], state_conv[l],
                                                cos_s, sin_s, gdn_conv_w[l], gdn_a_log[l], gdn_dt_bias[l],
                                                ret_gn_g[l], row(gdn_norm_g[l]))
    x1_s, q_s = _merge(xs, mix_g, w_gates, oa_s.reshape(ns, -1), ob_s.reshape(ns, -1), wa, wb, wo,
                       row(norm_x_g[l]), wq, tm=ns)
    o_s = _xattn_sample(q_s, cache_mem_k[l], cache_mem_v[l])
    x2_s = _resid_mm(x1_s, o_s.reshape(ns, D_MODEL).astype(BF16), wxo)
    y_s = _ffn(x2_s, row(norm_ffn_g[l]), wg, wu, wd, row(norm_final_g), tm=ns)

    return (y_p.reshape(b, t, D_MODEL), y_s.reshape(ns, 1, D_MODEL),
            sr_p[None], sg_p[None], sc_p[None],
            mk_p.reshape(1, b, N_MEM, X_HEADS, X_HD), mv_p.reshape(1, b, N_MEM, X_HEADS, X_HD),
            sr_s[None], sg_s[None], sc_s[None])
```

```python
import functools

import numpy as np
import jax
import jax.numpy as jnp
from jax import lax
from jax.experimental import pallas as pl
from jax.experimental.pallas import tpu as pltpu

F32 = jnp.float32
BF16 = jnp.bfloat16

D_MODEL = 1024
RET_HEADS, RET_DK, RET_DV = 4, 128, 256
GDN_HEADS, GDN_DK, GDN_DV = 8, 128, 128
CONV_W = 4
CONV_CH = 3 * GDN_HEADS * GDN_DK
N_MEM, X_HEADS, X_HD = 256, 4, 256
PAST_LEN = 16384
ROPE_BASE = 10000.0
EPS = 1e-6
GDN_CHUNK = 64

COL_RET = 0
COL_GDN = 3072
COL_GATE = 7168
BA_PAD = 128

VMEM_LIMIT = 56 * 1024 * 1024

NT_DIMS = (((1,), (1,)), ((), ()))
TN_DIMS = (((0,), (0,)), ((), ()))


def _mm(a, b):
    return jnp.dot(a.astype(BF16), b.astype(BF16), preferred_element_type=F32)


def _mm_nt(a, b):
    return lax.dot_general(a.astype(BF16), b.astype(BF16), NT_DIMS, preferred_element_type=F32)


def _mm_tn(a, b):
    return lax.dot_general(a.astype(BF16), b.astype(BF16), TN_DIMS, preferred_element_type=F32)


def _split3(x):
    hi = x.astype(BF16)
    r = x - hi.astype(F32)
    mid = r.astype(BF16)
    return hi, mid, (r - mid.astype(F32)).astype(BF16)


def _mm_sel(sel, x):
    return sum(jnp.dot(sel, p, preferred_element_type=F32) for p in _split3(x))


def _mm_sel_r(x, sel):
    return sum(jnp.dot(p, sel, preferred_element_type=F32) for p in _split3(x))


def _rms(x, g):
    return x * lax.rsqrt(jnp.mean(x * x, axis=-1, keepdims=True) + EPS) * g


def _silu(x):
    return x * jax.nn.sigmoid(x)


def _softplus(x):
    return jnp.maximum(x, 0.0) + jnp.log1p(jnp.exp(-jnp.abs(x)))


def _params(*sem):
    return pltpu.CompilerParams(dimension_semantics=sem, vmem_limit_bytes=VMEM_LIMIT)


def _inproj_body(with_ba, x_ref, g_ref, w_ref, *rest):
    if with_ba:
        wba_ref, o_ref, oba_ref, h_scr = rest
    else:
        o_ref, h_scr = rest

    @pl.when(pl.program_id(1) == 0)
    def _():
        hb = _rms(x_ref[...], g_ref[...]).astype(BF16)
        h_scr[...] = hb
        if with_ba:
            oba_ref[...] = jnp.dot(hb, wba_ref[...], preferred_element_type=F32)

    o_ref[...] = jnp.dot(h_scr[...], w_ref[...], preferred_element_type=F32)


def _inproj(x, g, w_main, n, tm, w_ba=None, tn=1024):
    m = x.shape[0]
    assert n % tn == 0 and n <= w_main.shape[1]
    in_specs = [
        pl.BlockSpec((tm, D_MODEL), lambda i, j: (i, 0)),
        pl.BlockSpec((1, D_MODEL), lambda i, j: (0, 0)),
        pl.BlockSpec((D_MODEL, tn), lambda i, j: (0, j)),
    ]
    out_specs = [pl.BlockSpec((tm, tn), lambda i, j: (i, j))]
    out_shape = [jax.ShapeDtypeStruct((m, n), F32)]
    args = [x, g, w_main]
    if w_ba is not None:
        in_specs.append(pl.BlockSpec((D_MODEL, BA_PAD), lambda i, j: (0, 0)))
        out_specs.append(pl.BlockSpec((tm, BA_PAD), lambda i, j: (i, 0)))
        out_shape.append(jax.ShapeDtypeStruct((m, BA_PAD), F32))
        args.append(w_ba)
    return pl.pallas_call(
        functools.partial(_inproj_body, w_ba is not None),
        grid=(m // tm, n // tn),
        in_specs=in_specs,
        out_specs=out_specs,
        out_shape=out_shape,
        scratch_shapes=[pltpu.VMEM((tm, D_MODEL), BF16)],
        compiler_params=_params("parallel", "arbitrary"),
        name="inproj",
    )(*args)


_RET_LOG_G = np.log1p(-np.exp2(-5.0 - np.arange(RET_HEADS, dtype=np.float64)))


def _ret_tables(c):
    idx = np.arange(c, dtype=np.float64)
    diff = idx[:, None] - idx[None, :]
    dmat = np.where(diff >= 0, np.exp(np.maximum(diff, 0.0)[None] * _RET_LOG_G[:, None, None]), 0.0)
    qdec = np.exp((idx + 1.0)[None, :] * _RET_LOG_G[:, None])
    kdec = np.exp((c - 1.0 - idx)[None, :] * _RET_LOG_G[:, None])
    lane = np.ones((1, 1, RET_DK))
    return (jnp.asarray(dmat, F32), jnp.asarray(qdec[:, :, None] * lane, F32),
            jnp.asarray(kdec[:, :, None] * lane, F32), [float(v) for v in np.exp(c * _RET_LOG_G)])


def _rot(x, cos, sin):
    return x * cos + pltpu.roll(x, RET_DK // 2, 1) * sin


def _group_norm_gate(o, gate, gn):
    mu = jnp.mean(o, axis=-1, keepdims=True)
    d = o - mu
    var = jnp.mean(d * d, axis=-1, keepdims=True)
    return _silu(gate) * (d * lax.rsqrt(var + EPS) * gn)


def _proj_tiles(hb, w_ref, dst, width=512):
    for c0 in range(0, w_ref.shape[1], width):
        dst[:, c0:c0 + width] = jnp.dot(hb, w_ref[:, c0:c0 + width], preferred_element_type=F32)


def _next_block(nt, nblocks):
    def index_map(i, j):
        n1 = jnp.minimum(i * nt + j + 1, nblocks - 1)
        return (n1 // nt, n1 % nt, 0)
    return index_map


def _ret_body(cdec, x0_ref, xn_ref, nmg_ref, w_ref, *rest):
    *consts, o_ref, s_out_ref, s_scr, pa, pb = rest
    t = pl.program_id(1)
    n = pl.program_id(0) * pl.num_programs(1) + t
    bufs = (pa, pb)

    @pl.when(t == 0)
    def _():
        s_scr[...] = jnp.zeros_like(s_scr)

    @pl.when(n == 0)
    def _():
        _proj_tiles(_rms(x0_ref[0], nmg_ref[...]).astype(BF16), w_ref, pa)

    for slot in range(2):
        @pl.when(n % 2 == slot)
        def _(slot=slot):
            _ret_step(cdec, t, xn_ref, nmg_ref, w_ref, bufs[slot], bufs[1 - slot], *consts, o_ref, s_out_ref, s_scr)


def _ret_step(cdec, t, xn_ref, nmg_ref, w_ref, proj, proj_next, cos_ref, sin_ref, dmat_ref, qdec_ref, kdec_ref,
              gn_ref, o_ref, s_out_ref, s_scr):
    qw = RET_HEADS * RET_DK
    _proj_tiles(_rms(xn_ref[0], nmg_ref[...]).astype(BF16), w_ref, proj_next)

    chunk = dmat_ref.shape[1]
    for c0 in range(0, proj.shape[0], chunk):
        rows = slice(c0, c0 + chunk)
        cos, sin = cos_ref[rows, :], sin_ref[rows, :]
        for h in range(RET_HEADS):
            qk = slice(h * RET_DK, (h + 1) * RET_DK)
            kk = slice(qw + h * RET_DK, qw + (h + 1) * RET_DK)
            vv = slice(h * RET_DV, (h + 1) * RET_DV)
            q = _rot(proj[rows, qk], cos, sin)
            k = _rot(proj[rows, kk], cos, sin) * (RET_DK ** -0.5)
            v = proj[rows, 2 * qw + h * RET_DV:2 * qw + (h + 1) * RET_DV]
            gate = proj[rows, 2 * qw + (RET_HEADS + h) * RET_DV:2 * qw + (RET_HEADS + h + 1) * RET_DV]
            s = s_scr[h]
            scores = _mm_nt(q, k) * dmat_ref[h]
            o = _mm(scores, v) + _mm(q * qdec_ref[h], s)
            s_scr[h] = cdec[h] * s + _mm_tn(k * kdec_ref[h], v)
            o_ref[0, rows, vv] = _group_norm_gate(o, gate, gn_ref[:, vv]).astype(BF16)

    @pl.when(t == pl.num_programs(1) - 1)
    def _():
        s_out_ref[0] = s_scr[...]


def _retention_prompt(x3, mix_g, w_ret, cos, sin, gn, tb=512, chunk=256):
    b, t, _ = x3.shape
    nt = t // tb
    dmat, qdec, kdec, cdec = _ret_tables(chunk)
    vw = RET_HEADS * RET_DV
    const3 = lambda i, j: (0, 0, 0)
    return pl.pallas_call(
        functools.partial(_ret_body, cdec),
        grid=(b, nt),
        in_specs=[
            pl.BlockSpec((1, tb, D_MODEL), const3),
            pl.BlockSpec((1, tb, D_MODEL), _next_block(nt, b * nt)),
            pl.BlockSpec((1, D_MODEL), lambda i, j: (0, 0)),
            pl.BlockSpec((D_MODEL, COL_GDN), lambda i, j: (0, COL_RET // COL_GDN), pipeline_mode=pl.Buffered(1)),
            pl.BlockSpec((tb, RET_DK), lambda i, j: (j, 0)),
            pl.BlockSpec((tb, RET_DK), lambda i, j: (j, 0)),
            pl.BlockSpec((RET_HEADS, chunk, chunk), const3),
            pl.BlockSpec((RET_HEADS, chunk, RET_DK), const3),
            pl.BlockSpec((RET_HEADS, chunk, RET_DK), const3),
            pl.BlockSpec((1, vw), lambda i, j: (0, 0)),
        ],
        out_specs=[
            pl.BlockSpec((1, tb, vw), lambda i, j: (i, j, 0)),
            pl.BlockSpec((1, RET_HEADS, RET_DK, RET_DV), lambda i, j: (i, 0, 0, 0)),
        ],
        out_shape=[
            jax.ShapeDtypeStruct((b, t, vw), BF16),
            jax.ShapeDtypeStruct((b, RET_HEADS, RET_DK, RET_DV), F32),
        ],
        scratch_shapes=[pltpu.VMEM((RET_HEADS, RET_DK, RET_DV), F32)]
        + 2 * [pltpu.VMEM((tb, COL_GDN), F32)],
        compiler_params=_params("arbitrary", "arbitrary"),
        name="retention_prompt",
    )(x3, x3, mix_g, w_ret, cos, sin, dmat, qdec, kdec, gn)


def _gdn_tables(tb):
    idx = np.arange(tb)
    same = (idx[:, None] // GDN_CHUNK) == (idx[None, :] // GDN_CHUNK)
    lower = same & (idx[:, None] >= idx[None, :])
    nchunk = tb // GDN_CHUNK
    chunk_sel = np.repeat((idx[:, None] // GDN_CHUNK) == np.arange(nchunk)[None, :], 128, axis=1)
    grp = 2 * GDN_CHUNK
    il = (idx % grp)[:, None]
    jl = np.arange(grp)[None, :]
    bias = np.where((il // GDN_CHUNK == jl // GDN_CHUNK) & (il >= jl), 0.0, -1e30)
    eye = (il == jl).astype(np.float64)
    return (jnp.asarray(lower, BF16), jnp.asarray(lower.T, BF16), jnp.asarray(same, BF16),
            jnp.asarray(chunk_sel, BF16), jnp.asarray(bias, F32), jnp.asarray(eye - 1.0, F32),
            jnp.asarray(eye, BF16))


def _gdn_prepare(x_ref, first_of_seq, nmg_ref, wqkv_ref, wz_ref, wba_ref, wbat_ref, cw_ref, xr, cs_buf, zs, bas,
                 bats, tb):
    st = {}

    def start():
        xr[0:8, :] = jnp.where(first_of_seq, 0.0, xr[tb:tb + 8, :])
        st["hb"] = _rms(x_ref[0], nmg_ref[...]).astype(BF16)

    def tile(w_ref, dst, c0, width=512):
        def run():
            dst[:, c0:c0 + width] = jnp.dot(st["hb"], w_ref[:, c0:c0 + width], preferred_element_type=F32)
        return run

    def logits():
        bas[...] = jnp.dot(st["hb"], wba_ref[...], preferred_element_type=F32)
        bats[...] = lax.dot_general(wbat_ref[...], st["hb"], NT_DIMS, preferred_element_type=F32)

    def conv(c0):
        def run():
            cs = slice(c0, c0 + GDN_DK)
            acc = xr[5:5 + tb, cs] * cw_ref[0:1, cs]
            for i in range(1, CONV_W):
                acc = acc + xr[5 + i:5 + i + tb, cs] * cw_ref[i:i + 1, cs]
            cs_buf[:, cs] = _silu(acc)
        return run

    mxu_items = ([start] + [tile(wqkv_ref, xr.at[8:8 + tb], c0) for c0 in range(0, CONV_CH, 512)]
                 + [tile(wz_ref, zs, c0) for c0 in range(0, wz_ref.shape[1], 512)] + [logits])
    vpu_items = [conv(c0) for c0 in range(0, CONV_CH, GDN_DK)]
    return mxu_items, vpu_items


class _Background:
    def __init__(self, items):
        self._items = list(items)

    def __call__(self, count):
        for item in self._items[:count]:
            item()
        del self._items[:count]

    def drain(self):
        self(len(self._items))


def _gdn_body(tb, x0_ref, xn_ref, nmg_ref, wqkv_ref, wz_ref, wba_ref, wbat_ref, cw_ref, *rest):
    *consts, o_ref, s_out_ref, conv_out_ref, s_scr, xr, ca, za, baa, bata, cb, zb, bab, batb = rest
    t = pl.program_id(1)
    nt = pl.num_programs(1)
    n = pl.program_id(0) * nt + t
    weights = (nmg_ref, wqkv_ref, wz_ref, wba_ref, wbat_ref, cw_ref)
    bufs = ((ca, za, baa, bata), (cb, zb, bab, batb))

    @pl.when(t == 0)
    def _():
        s_scr[...] = jnp.zeros_like(s_scr)

    @pl.when(n == 0)
    def _():
        mxu_items, vpu_items = _gdn_prepare(x0_ref, True, *weights, xr, *bufs[0], tb)
        _Background(mxu_items + vpu_items).drain()

    for slot in range(2):
        @pl.when(n % 2 == slot)
        def _(slot=slot):
            mxu_items, vpu_items = _gdn_prepare(xn_ref, t == nt - 1, *weights, xr, *bufs[1 - slot], tb)
            _gdn_step(tb, t, bufs[slot], *consts, o_ref, s_out_ref, s_scr,
                      _Background(mxu_items), _Background(vpu_items))

    @pl.when(t == nt - 2)
    def _():
        conv_out_ref[0] = xr[tb + 5:tb + 8, :]


def _gdn_step(tb, t, cur, alog_r_ref, dt_r_ref, alog_c_ref, dt_c_ref, ng_ref,
              lbd_ref, ubd_ref, obd_ref, csel_ref, bias_ref, offd_ref, eye_ref, o_ref, s_out_ref, s_scr,
              bg_mxu, bg_vpu):
    nchunk = tb // GDN_CHUNK
    hk = GDN_HEADS * GDN_DK
    c_scr, zz, ba_ref, bat_ref = cur
    ba, bat = ba_ref[...], bat_ref[...]

    bg_mxu(2)
    beta_c = jax.nn.sigmoid(ba)
    g_c = -jnp.exp(alog_r_ref[...]) * _softplus(ba + dt_r_ref[...])
    g_r = -jnp.exp(alog_c_ref[...]) * _softplus(bat + dt_c_ref[...])
    gc_c = _mm_sel(lbd_ref[...], g_c)
    gt_c = _mm_sel(obd_ref[...], g_c)
    gc_r = _mm_sel_r(g_r, ubd_ref[...])
    gt_l = _mm_sel_r(g_r, csel_ref[...])

    heads = range(GDN_HEADS)
    grp = 2 * GDN_CHUNK
    groups = [slice(p * grp, (p + 1) * grp) for p in range(tb // grp)]

    def grp_dot(a, b):
        return jnp.concatenate([jnp.dot(a[g], b[g], preferred_element_type=F32) for g in groups], axis=0)

    def grp_dot_nt(a, b):
        return jnp.concatenate([lax.dot_general(a[g], b[g], NT_DIMS, preferred_element_type=F32)
                                for g in groups], axis=0)

    bias = bias_ref[...]
    offdiag = offd_ref[...]
    eye_b = eye_ref[...]
    qs, ks, gammas, pbs, rhss, qgs, khs = [], [], [], [], [], [], []
    for h in heads:
        q = c_scr[:, h * GDN_DK:(h + 1) * GDN_DK]
        k = c_scr[:, hk + h * GDN_DK:hk + (h + 1) * GDN_DK]
        v = c_scr[:, 2 * hk + h * GDN_DV:2 * hk + (h + 1) * GDN_DV]
        q = q * lax.rsqrt(jnp.sum(q * q, axis=-1, keepdims=True) + EPS) * (GDN_DK ** -0.5)
        k = k * lax.rsqrt(jnp.sum(k * k, axis=-1, keepdims=True) + EPS)
        beta = beta_c[:, h:h + 1]
        gcc = gc_c[:, 8 + h:9 + h]
        gtc = gt_c[:, 8 + h:9 + h]
        gcr = gc_r[8 + h:9 + h, :]
        dg = jnp.concatenate([gcc[g] - gcr[:, g] for g in groups], axis=0)
        gamma = jnp.exp(dg + bias)
        kbeta = k * beta
        kb = k.astype(BF16)
        pbs.append((grp_dot_nt(kbeta.astype(BF16), kb) * (gamma * offdiag)).astype(BF16))
        eg = jnp.exp(gcc)
        rhss.append(jnp.concatenate([v * beta, kbeta * eg], axis=1))
        qs.append(q.astype(BF16))
        ks.append(kb)
        gammas.append(gamma)
        qgs.append(q * eg)
        khs.append(k * jnp.exp(gtc - gcc))
        bg_mxu(1)
        bg_vpu(1)
    bg_mxu.drain()
    bg_vpu(4)

    def solve(_, carry):
        pb = pbs
        tbs = [pb[h] + eye_b for h in heads]
        for lvl in range(5):
            pb = [grp_dot(pb[h], pb[h]).astype(BF16) for h in heads]
            bg_vpu(1)
            tnew = [grp_dot(tbs[h], pb[h] + eye_b) for h in heads]
            bg_vpu(1)
            tbs = [x.astype(BF16) for x in tnew]

        us, ws, qks = [], [], []
        for h in heads:
            uw = rhss[h] + grp_dot(tbs[h] - eye_b, rhss[h].astype(BF16))
            us.append(uw[:, :GDN_DV])
            ws.append(uw[:, GDN_DV:])
            qks.append((grp_dot_nt(qs[h], ks[h]) * gammas[h]).astype(BF16))
        bg_vpu(2)

        s = [s_scr[h] for h in heads]
        vn_parts = [[] for _ in heads]
        qs_parts = [[] for _ in heads]
        for c in range(nchunk):
            rows = slice(c * GDN_CHUNK, (c + 1) * GDN_CHUNK)
            for h in heads:
                wq = _mm(jnp.concatenate([ws[h][rows], qgs[h][rows]], axis=0), s[h])
                vn = us[h][rows] - wq[:GDN_CHUNK]
                qs_parts[h].append(wq[GDN_CHUNK:])
                vn_parts[h].append(vn)
                decay = jnp.exp(gt_l[8 + h:9 + h, c * 128:(c + 1) * 128])
                s[h] = decay * s[h] + _mm_tn(khs[h][rows], vn)
            bg_vpu(1)
        bg_vpu.drain()
        for h in heads:
            hs = slice(h * GDN_DV, (h + 1) * GDN_DV)
            s_scr[h] = s[h]
            vn = jnp.concatenate(vn_parts[h], axis=0).astype(BF16)
            o = jnp.concatenate(qs_parts[h], axis=0) + grp_dot(qks[h], vn)
            o_ref[0, :, hs] = (_rms(o, ng_ref[...]) * _silu(zz[:, hs])).astype(BF16)
        return carry

    lax.fori_loop(0, jnp.minimum(t + 1, 1), solve, 0)

    @pl.when(t == pl.num_programs(1) - 1)
    def _():
        s_out_ref[0] = s_scr[...]


def _gdn_prompt(x3, mix_g, w_gdn, w_ba, w_bat, conv_w, alog_r, dt_r, alog_c, dt_c, norm_g, tb=256):
    b, t, _ = x3.shape
    nt = t // tb
    assert t % tb == 0 and nt >= 2, "a sequence's last block must be prepared during one of its own steps"
    lbd, ubd, obd, csel, bias, offdiag, eye = _gdn_tables(tb)
    vw = GDN_HEADS * GDN_DV
    c2 = lambda i, j: (0, 0)
    single = pl.Buffered(1)
    return pl.pallas_call(
        functools.partial(_gdn_body, tb),
        grid=(b, nt),
        in_specs=[
            pl.BlockSpec((1, tb, D_MODEL), lambda i, j: (0, 0, 0)),
            pl.BlockSpec((1, tb, D_MODEL), _next_block(nt, b * nt)),
            pl.BlockSpec((1, D_MODEL), c2),
            pl.BlockSpec((D_MODEL, CONV_CH), lambda i, j: (0, COL_GDN // CONV_CH), pipeline_mode=single),
            pl.BlockSpec((D_MODEL, vw), lambda i, j: (0, (COL_GDN + CONV_CH) // vw), pipeline_mode=single),
            pl.BlockSpec((D_MODEL, BA_PAD), c2),
            pl.BlockSpec((BA_PAD, D_MODEL), c2),
            pl.BlockSpec((CONV_W, CONV_CH), c2),
            pl.BlockSpec((1, BA_PAD), c2),
            pl.BlockSpec((1, BA_PAD), c2),
            pl.BlockSpec((BA_PAD, 1), c2),
            pl.BlockSpec((BA_PAD, 1), c2),
            pl.BlockSpec((1, GDN_DV), c2),
            pl.BlockSpec((tb, tb), c2),
            pl.BlockSpec((tb, tb), c2),
            pl.BlockSpec((tb, tb), c2),
            pl.BlockSpec((tb, (tb // GDN_CHUNK) * 128), c2),
            pl.BlockSpec((tb, 2 * GDN_CHUNK), c2),
            pl.BlockSpec((tb, 2 * GDN_CHUNK), c2),
            pl.BlockSpec((tb, 2 * GDN_CHUNK), c2),
        ],
        out_specs=[
            pl.BlockSpec((1, tb, vw), lambda i, j: (i, j, 0)),
            pl.BlockSpec((1, GDN_HEADS, GDN_DK, GDN_DV), lambda i, j: (i, 0, 0, 0)),
            pl.BlockSpec((1, CONV_W - 1, CONV_CH), lambda i, j: (i, 0, 0)),
        ],
        out_shape=[
            jax.ShapeDtypeStruct((b, t, vw), BF16),
            jax.ShapeDtypeStruct((b, GDN_HEADS, GDN_DK, GDN_DV), F32),
            jax.ShapeDtypeStruct((b, CONV_W - 1, CONV_CH), F32),
        ],
        scratch_shapes=[pltpu.VMEM((GDN_HEADS, GDN_DK, GDN_DV), F32), pltpu.VMEM((tb + 8, CONV_CH), F32)]
        + 2 * [pltpu.VMEM((tb, CONV_CH), F32), pltpu.VMEM((tb, vw), F32),
               pltpu.VMEM((tb, BA_PAD), F32), pltpu.VMEM((BA_PAD, tb), F32)],
        compiler_params=_params("arbitrary", "arbitrary"),
        name="gdn_prompt",
    )(x3, x3, mix_g, w_gdn, w_gdn, w_ba, w_bat, conv_w, alog_r, dt_r, alog_c, dt_c, norm_g, lbd, ubd, obd, csel,
      bias, offdiag, eye)


def _block_diag_rows(x, nblk):
    row = lax.broadcasted_iota(jnp.int32, (8, nblk * 128), 0)
    blk = lax.broadcasted_iota(jnp.int32, (8, nblk * 128), 1) // 128
    return jnp.where(row == blk, jnp.concatenate([x] * nblk, axis=1), 0.0)


def _sample_step_body(cdec, qk_ref, rv_ref, rg_ref, x_ref, z_ref, bcol_ref, sr_ref, sg_ref, sc_ref, cos_ref, sin_ref,
                      cw_ref, alog_ref, dt_ref, gn_ref, ng_ref,
                      oa_ref, ob_ref, sr_out_ref, sg_out_ref, sc_out_ref):
    nh = GDN_HEADS
    cos, sin = cos_ref[...], sin_ref[...]
    gam = jnp.where(lax.broadcasted_iota(jnp.int32, (8, 1), 0) == 0, cdec[0], 0.0)
    for h in range(1, RET_HEADS):
        gam = jnp.where(lax.broadcasted_iota(jnp.int32, (8, 1), 0) == h, cdec[h], gam)
    zero4 = jnp.zeros((RET_HEADS, RET_DV), F32)
    for g in range(x_ref.shape[0]):
        x_new = x_ref[g]
        buf = sc_ref[g]
        conv = x_new * cw_ref[CONV_W - 1]
        for i in range(CONV_W - 1):
            conv = conv + buf[i] * cw_ref[i]
        conv = _silu(conv)
        sc_out_ref[g, 0] = buf[1]
        sc_out_ref[g, 1] = buf[2]
        sc_out_ref[g, 2] = x_new
        q, k, v = conv[0:nh], conv[nh:2 * nh], conv[2 * nh:3 * nh]
        q = q * lax.rsqrt(jnp.sum(q * q, axis=-1, keepdims=True) + EPS) * (GDN_DK ** -0.5)
        k = k * lax.rsqrt(jnp.sum(k * k, axis=-1, keepdims=True) + EPS)
        col = bcol_ref[g]
        beta = jax.nn.sigmoid(col[0:nh])
        eg = jnp.exp(-jnp.exp(alog_ref[...]) * _softplus(col[nh:2 * nh] + dt_ref[...]))
        kbd, qbd = _block_diag_rows(k, nh), _block_diag_rows(q, nh)
        s_flat = sg_ref[g].reshape(nh * GDN_DK, GDN_DV)
        kq_s = _mm(jnp.concatenate([kbd, qbd], axis=0), s_flat)
        vn = beta * (v - eg * kq_s[0:nh])
        o = eg * kq_s[nh:2 * nh] + jnp.sum(q * k, axis=-1, keepdims=True) * vn
        ob_ref[g] = _rms(o, ng_ref[...]) * _silu(z_ref[g])
        upd = _mm_tn(kbd, vn)
        eg_l = jnp.broadcast_to(eg, (nh, GDN_DV))
        for h in range(nh):
            sg_out_ref[g, h] = eg_l[h:h + 1] * sg_ref[g, h] + upd[h * GDN_DK:(h + 1) * GDN_DK]

        qk = _rot(qk_ref[g], cos, sin)
        rq = jnp.concatenate([qk[0:RET_HEADS], qk[0:RET_HEADS]], axis=0)
        rk = jnp.concatenate([qk[RET_HEADS:], qk[RET_HEADS:]], axis=0) * (RET_DK ** -0.5)
        rv = jnp.concatenate([rv_ref[g], zero4], axis=0)
        qbd_r, kbd_r = _block_diag_rows(rq, RET_HEADS), _block_diag_rows(rk, RET_HEADS)
        sr_flat = sr_ref[g].reshape(RET_HEADS * RET_DK, RET_DV)
        q_s = _mm(qbd_r, sr_flat)
        o_r = gam * q_s + jnp.sum(rq * rk, axis=-1, keepdims=True) * rv
        oa_ref[g] = _group_norm_gate(o_r[0:RET_HEADS], rg_ref[g], gn_ref[...])
        upd_r = _mm_tn(kbd_r, rv)
        for h in range(RET_HEADS):
            sr_out_ref[g, h] = cdec[h] * sr_ref[g, h] + upd_r[h * RET_DK:(h + 1) * RET_DK]


def _sample_step(proj_ret, proj_gdn, ba, state_ret, state_gdn, state_conv, cos, sin, conv_w, a_log, dt_bias, gn,
                 norm_g, gs=8):
    ns = proj_ret.shape[0]
    nh, rh = GDN_HEADS, RET_HEADS
    nq = rh * RET_DK
    qk3 = proj_ret[:, :2 * nq].reshape(ns, 2 * rh, RET_DK)
    rv3 = proj_ret[:, 2 * nq:2 * nq + rh * RET_DV].reshape(ns, rh, RET_DV)
    rg3 = proj_ret[:, 2 * nq + rh * RET_DV:].reshape(ns, rh, RET_DV)
    x3 = proj_gdn[:, :CONV_CH].reshape(ns, 3 * nh, GDN_DK)
    z3 = proj_gdn[:, CONV_CH:].reshape(ns, nh, GDN_DV)
    bcol = ba[:, :2 * nh].reshape(ns, 2 * nh, 1)
    sc4 = state_conv.reshape(ns, CONV_W - 1, 3 * nh, GDN_DK)
    cw3 = conv_w.reshape(CONV_W, 3 * nh, GDN_DK)
    cdec = [float(v) for v in np.exp(_RET_LOG_G)]
    c2 = lambda i: (0, 0)
    c3 = lambda i: (0, 0, 0)
    b3 = lambda i: (i, 0, 0)
    b4 = lambda i: (i, 0, 0, 0)
    oa, ob, sr, sg, sc = pl.pallas_call(
        functools.partial(_sample_step_body, cdec),
        grid=(ns // gs,),
        in_specs=[
            pl.BlockSpec((gs, 2 * rh, RET_DK), b3),
            pl.BlockSpec((gs, rh, RET_DV), b3),
            pl.BlockSpec((gs, rh, RET_DV), b3),
            pl.BlockSpec((gs, 3 * nh, GDN_DK), b3),
            pl.BlockSpec((gs, nh, GDN_DV), b3),
            pl.BlockSpec((gs, 2 * nh, 1), b3),
            pl.BlockSpec((gs, rh, RET_DK, RET_DV), b4),
            pl.BlockSpec((gs, nh, GDN_DK, GDN_DV), b4),
            pl.BlockSpec((gs, CONV_W - 1, 3 * nh, GDN_DK), b4),
            pl.BlockSpec((1, RET_DK), c2),
            pl.BlockSpec((1, RET_DK), c2),
            pl.BlockSpec((CONV_W, 3 * nh, GDN_DK), c3),
            pl.BlockSpec((nh, 1), c2),
            pl.BlockSpec((nh, 1), c2),
            pl.BlockSpec((rh, RET_DV), c2),
            pl.BlockSpec((1, GDN_DV), c2),
        ],
        out_specs=[
            pl.BlockSpec((gs, rh, RET_DV), b3),
            pl.BlockSpec((gs, nh, GDN_DV), b3),
            pl.BlockSpec((gs, rh, RET_DK, RET_DV), b4),
            pl.BlockSpec((gs, nh, GDN_DK, GDN_DV), b4),
            pl.BlockSpec((gs, CONV_W - 1, 3 * nh, GDN_DK), b4),
        ],
        out_shape=[
            jax.ShapeDtypeStruct((ns, rh, RET_DV), F32),
            jax.ShapeDtypeStruct((ns, nh, GDN_DV), F32),
            jax.ShapeDtypeStruct(state_ret.shape, F32),
            jax.ShapeDtypeStruct(state_gdn.shape, F32),
            jax.ShapeDtypeStruct(sc4.shape, F32),
        ],
        compiler_params=_params("parallel"),
        name="sample_step",
    )(qk3, rv3, rg3, x3, z3, bcol, state_ret, state_gdn, sc4, cos, sin, cw3, a_log.reshape(nh, 1),
      dt_bias.reshape(nh, 1), gn.reshape(rh, RET_DV), norm_g)
    return (oa.reshape(ns, rh * RET_DV).astype(BF16), ob.reshape(ns, nh * GDN_DV).astype(BF16), sr, sg,
            sc.reshape(state_conv.shape))


def _merge_body(x_ref, nmg_ref, wgt_ref, oa_ref, ob_ref, wa_ref, wb_ref, wo_ref, ng_ref, wq_ref, x1_ref, q_ref):
    x = x_ref[...]
    gates = jnp.dot(_rms(x, nmg_ref[...]).astype(BF16), wgt_ref[...], preferred_element_type=F32)
    ya = jnp.dot(oa_ref[...], wa_ref[...], preferred_element_type=F32)
    yb = jnp.dot(ob_ref[...], wb_ref[...], preferred_element_type=F32)
    merged = jax.nn.sigmoid(gates[:, :D_MODEL]) * ya + jax.nn.sigmoid(gates[:, D_MODEL:]) * yb
    x1 = x + _mm(merged, wo_ref[...])
    x1_ref[...] = x1
    q_ref[...] = _mm(_rms(x1, ng_ref[...]), wq_ref[...]).astype(BF16)


def _merge(x, mix_g, w_gates, oa, ob, wa, wb, wo, ng, wq, tm):
    m = x.shape[0]
    row = lambda i: (i, 0)
    c2 = lambda i: (0, 0)
    wspec = pl.BlockSpec((D_MODEL, D_MODEL), c2)
    return pl.pallas_call(
        _merge_body,
        grid=(m // tm,),
        in_specs=[
            pl.BlockSpec((tm, D_MODEL), row),
            pl.BlockSpec((1, D_MODEL), c2),
            pl.BlockSpec((D_MODEL, 2 * D_MODEL), c2),
            pl.BlockSpec((tm, D_MODEL), row),
            pl.BlockSpec((tm, D_MODEL), row),
            wspec, wspec, wspec,
            pl.BlockSpec((1, D_MODEL), c2),
            wspec,
        ],
        out_specs=[pl.BlockSpec((tm, D_MODEL), row), pl.BlockSpec((tm, D_MODEL), row)],
        out_shape=[jax.ShapeDtypeStruct((m, D_MODEL), F32), jax.ShapeDtypeStruct((m, D_MODEL), BF16)],
        compiler_params=_params("parallel"),
        name="merge",
    )(x, mix_g, w_gates, oa, ob, wa, wb, wo, ng, wq)


def _memkv_body(m_ref, g_ref, wk_ref, wv_ref, k_ref, v_ref):
    mn = _rms(m_ref[...], g_ref[...]).astype(BF16)
    k_ref[...] = jnp.dot(mn, wk_ref[...], preferred_element_type=F32)
    v_ref[...] = jnp.dot(mn, wv_ref[...], preferred_element_type=F32)


def _memkv(mem, g, wk, wv, tm=512):
    m = mem.shape[0]
    row = lambda i: (i, 0)
    c2 = lambda i: (0, 0)
    return pl.pallas_call(
        _memkv_body,
        grid=(m // tm,),
        in_specs=[pl.BlockSpec((tm, D_MODEL), row), pl.BlockSpec((1, D_MODEL), c2),
                  pl.BlockSpec((D_MODEL, D_MODEL), c2), pl.BlockSpec((D_MODEL, D_MODEL), c2)],
        out_specs=[pl.BlockSpec((tm, D_MODEL), row), pl.BlockSpec((tm, D_MODEL), row)],
        out_shape=[jax.ShapeDtypeStruct((m, D_MODEL), F32)] * 2,
        compiler_params=_params("parallel"),
        name="memkv",
    )(mem, g, wk, wv)


def _xattn_body(q_ref, mk_ref, mv_ref, x1_ref, wo_ref, x2_ref):
    parts = []
    for h in range(X_HEADS):
        hs = slice(h * X_HD, (h + 1) * X_HD)
        s = _mm_nt(q_ref[0, :, hs], mk_ref[0, :, hs]) * (X_HD ** -0.5)
        p = jnp.exp(s - jnp.max(s, axis=-1, keepdims=True))
        o = _mm(p, mv_ref[0, :, hs]) / jnp.sum(p, axis=-1, keepdims=True)
        parts.append(o.astype(BF16))
    x2_ref[0] = x1_ref[0] + jnp.dot(jnp.concatenate(parts, axis=1), wo_ref[...], preferred_element_type=F32)


def _xattn_prompt(q3, mk3, mv3, x13, wo, tq=512):
    b, t, _ = q3.shape
    tok = lambda i, j: (i, j, 0)
    mem = lambda i, j: (i, 0, 0)
    return pl.pallas_call(
        _xattn_body,
        grid=(b, t // tq),
        in_specs=[pl.BlockSpec((1, tq, D_MODEL), tok), pl.BlockSpec((1, N_MEM, D_MODEL), mem),
                  pl.BlockSpec((1, N_MEM, D_MODEL), mem), pl.BlockSpec((1, tq, D_MODEL), tok),
                  pl.BlockSpec((D_MODEL, D_MODEL), lambda i, j: (0, 0))],
        out_specs=pl.BlockSpec((1, tq, D_MODEL), tok),
        out_shape=jax.ShapeDtypeStruct((b, t, D_MODEL), F32),
        compiler_params=_params("parallel", "parallel"),
        name="xattn_prompt",
    )(q3, mk3, mv3, x13, wo)


def _xattn_sample_body(q_ref, mk_ref, mv_ref, o_ref):
    for g in range(q_ref.shape[0]):
        q = q_ref[g]
        s = jnp.sum(mk_ref[g] * q[None], axis=-1, keepdims=True) * (X_HD ** -0.5)
        p = jnp.exp(s - jnp.max(s, axis=0, keepdims=True))
        o_ref[g] = jnp.sum(p * mv_ref[g], axis=0) / jnp.sum(p, axis=0)


def _xattn_sample(q, mk4, mv4, gs=4):
    ns = q.shape[0]
    q3 = q.astype(F32).reshape(ns, X_HEADS, X_HD)
    row = lambda i: (i, 0, 0)
    mem = lambda i: (i, 0, 0, 0)
    return pl.pallas_call(
        _xattn_sample_body,
        grid=(ns // gs,),
        in_specs=[pl.BlockSpec((gs, X_HEADS, X_HD), row), pl.BlockSpec((gs, N_MEM, X_HEADS, X_HD), mem),
                  pl.BlockSpec((gs, N_MEM, X_HEADS, X_HD), mem)],
        out_specs=pl.BlockSpec((gs, X_HEADS, X_HD), row),
        out_shape=jax.ShapeDtypeStruct((ns, X_HEADS, X_HD), F32),
        compiler_params=_params("parallel"),
        name="xattn_sample",
    )(q3, mk4, mv4)


def _resid_mm_body(x_ref, a_ref, w_ref, o_ref):
    o_ref[...] = x_ref[...] + jnp.dot(a_ref[...], w_ref[...], preferred_element_type=F32)


def _resid_mm(x, a, w):
    m = x.shape[0]
    return pl.pallas_call(
        _resid_mm_body,
        out_shape=jax.ShapeDtypeStruct((m, D_MODEL), F32),
        compiler_params=pltpu.CompilerParams(vmem_limit_bytes=VMEM_LIMIT),
        name="resid_mm",
    )(x, a, w)


def _ffn_body(x_ref, ng_ref, wg_ref, wu_ref, wd_ref, nf_ref, y_ref):
    x = x_ref[...]
    h = _rms(x, ng_ref[...]).astype(BF16)
    gate = jnp.dot(h, wg_ref[...], preferred_element_type=F32)
    up = jnp.dot(h, wu_ref[...], preferred_element_type=F32)
    x3 = x + _mm(_silu(gate) * up, wd_ref[...])
    y_ref[...] = _rms(x3, nf_ref[...])


def _ffn(x, ng, wg, wu, wd, nf, tm):
    m = x.shape[0]
    dff = wg.shape[1]
    row = lambda i: (i, 0)
    c2 = lambda i: (0, 0)
    single = pl.Buffered(1)
    return pl.pallas_call(
        _ffn_body,
        grid=(m // tm,),
        in_specs=[pl.BlockSpec((tm, D_MODEL), row), pl.BlockSpec((1, D_MODEL), c2),
                  pl.BlockSpec((D_MODEL, dff), c2, pipeline_mode=single),
                  pl.BlockSpec((D_MODEL, dff), c2, pipeline_mode=single),
                  pl.BlockSpec((dff, D_MODEL), c2, pipeline_mode=single),
                  pl.BlockSpec((1, D_MODEL), c2)],
        out_specs=pl.BlockSpec((tm, D_MODEL), row),
        out_shape=jax.ShapeDtypeStruct((m, D_MODEL), F32),
        compiler_params=_params("parallel"),
        name="ffn",
    )(x, ng, wg, wu, wd, nf)


def _rope_tables(pos):
    half = RET_DK // 2
    inv = ROPE_BASE ** (-jnp.arange(half, dtype=F32) / half)
    ang = pos.astype(F32)[:, None] * inv[None, :]
    cos, sin = jnp.cos(ang), jnp.sin(ang)
    return jnp.concatenate([cos, cos], axis=-1), jnp.concatenate([-sin, sin], axis=-1)


def _pad_lanes(v, offset):
    return jnp.zeros((BA_PAD,), F32).at[offset:offset + v.shape[0]].set(v)


def kernel(x_prompt, x_sample, state_ret, state_gdn, state_conv, cache_mem_k, cache_mem_v, mem_prompt,
           norm_mix_g, w_in, ret_gn_g, w_branch_a, gdn_conv_w, gdn_a_log, gdn_dt_bias, gdn_norm_g,
           w_branch_b, w_out, norm_x_g, mem_norm_g, w_xq, w_xk, w_xv, w_xo, norm_ffn_g, w_gate, w_up,
           w_down, norm_final_g):
    depth = w_in.shape[0]
    assert depth == 1, "single-layer kernel"
    b, t, _ = x_prompt.shape
    ns = x_sample.shape[0]
    l = 0

    w = w_in[l]
    ba0, g0 = COL_GATE, COL_GATE + 2 * GDN_HEADS
    w_bf = w.astype(BF16)
    w_gates = w_bf[:, g0:]
    w_ba = jnp.pad(w[:, ba0:g0], ((0, 0), (0, BA_PAD - 2 * GDN_HEADS))).astype(BF16)
    w_bat = w_ba.T
    row = lambda v: v.reshape(1, -1)
    wa, wb, wo = w_branch_a[l].astype(BF16), w_branch_b[l].astype(BF16), w_out[l].astype(BF16)
    wq, wk, wv, wxo = w_xq[l].astype(BF16), w_xk[l].astype(BF16), w_xv[l].astype(BF16), w_xo[l].astype(BF16)
    wg, wu, wd = w_gate[l].astype(BF16), w_up[l].astype(BF16), w_down[l].astype(BF16)
    alog_r = _pad_lanes(gdn_a_log[l], GDN_HEADS).reshape(1, BA_PAD)
    dt_r = _pad_lanes(gdn_dt_bias[l], GDN_HEADS).reshape(1, BA_PAD)
    alog_c, dt_c = alog_r.reshape(BA_PAD, 1), dt_r.reshape(BA_PAD, 1)
    cos_p, sin_p = _rope_tables(jnp.arange(t))
    cos_s, sin_s = _rope_tables(PAST_LEN + jnp.arange(1))

    xp = x_prompt.reshape(b * t, D_MODEL)
    mix_g = row(norm_mix_g[l])
    oa_p, sr_p = _retention_prompt(x_prompt, mix_g, w_bf, cos_p, sin_p, row(ret_gn_g[l]))
    ob_p, sg_p, sc_p = _gdn_prompt(x_prompt, mix_g, w_bf, w_ba, w_bat, gdn_conv_w[l], alog_r, dt_r,
                                   alog_c, dt_c, row(gdn_norm_g[l]))
    x1_p, q_p = _merge(xp, mix_g, w_gates, oa_p.reshape(b * t, -1), ob_p.reshape(b * t, -1), wa, wb, wo,
                       row(norm_x_g[l]), wq, tm=512)
    mk_p, mv_p = _memkv(mem_prompt.reshape(b * N_MEM, D_MODEL), row(mem_norm_g[l]), wk, wv)
    x2_p = _xattn_prompt(q_p.reshape(b, t, D_MODEL), mk_p.reshape(b, N_MEM, D_MODEL),
                         mv_p.reshape(b, N_MEM, D_MODEL), x1_p.reshape(b, t, D_MODEL), wxo)
    y_p = _ffn(x2_p.reshape(b * t, D_MODEL), row(norm_ffn_g[l]), wg, wu, wd, row(norm_final_g), tm=512)

    xs = x_sample.reshape(ns, D_MODEL)
    proj_s, ba_s = _inproj(xs, mix_g, w_bf, COL_GATE, tm=ns, w_ba=w_ba)
    oa_s, ob_s, sr_s, sg_s, sc_s = _sample_step(proj_s[:, :COL_GDN], proj_s[:, COL_GDN:], ba_s, state_ret[l], state_gdn[l], state_conv[l],
                                                cos_s, sin_s, gdn_conv_w[l], gdn_a_log[l], gdn_dt_bias[l],
                                                ret_gn_g[l], row(gdn_norm_g[l]))
    x1_s, q_s = _merge(xs, mix_g, w_gates, oa_s.reshape(ns, -1), ob_s.reshape(ns, -1), wa, wb, wo,
                       row(norm_x_g[l]), wq, tm=ns)
    o_s = _xattn_sample(q_s, cache_mem_k[l], cache_mem_v[l])
    x2_s = _resid_mm(x1_s, o_s.reshape(ns, D_MODEL).astype(BF16), wxo)
    y_s = _ffn(x2_s, row(norm_ffn_g[l]), wg, wu, wd, row(norm_final_g), tm=ns)

    return (y_p.reshape(b, t, D_MODEL), y_s.reshape(ns, 1, D_MODEL),
            sr_p[None], sg_p[None], sc_p[None],
            mk_p.reshape(1, b, N_MEM, X_HEADS, X_HD), mv_p.reshape(1, b, N_MEM, X_HEADS, X_HD),
            sr_s[None], sg_s[None], sc_s[None])
```

```python
import functools

import numpy as np
import jax
import jax.numpy as jnp
from jax import lax
from jax.experimental import pallas as pl
from jax.experimental.pallas import tpu as pltpu

F32 = jnp.float32
BF16 = jnp.bfloat16

D_MODEL = 1024
RET_HEADS, RET_DK, RET_DV = 4, 128, 256
GDN_HEADS, GDN_DK, GDN_DV = 8, 128, 128
CONV_W = 4
CONV_CH = 3 * GDN_HEADS * GDN_DK
N_MEM, X_HEADS, X_HD = 256, 4, 256
PAST_LEN = 16384
ROPE_BASE = 10000.0
EPS = 1e-6
GDN_CHUNK = 64

COL_RET = 0
COL_GDN = 3072
COL_GATE = 7168
BA_PAD = 128

VMEM_LIMIT = 56 * 1024 * 1024

NT_DIMS = (((1,), (1,)), ((), ()))
TN_DIMS = (((0,), (0,)), ((), ()))


def _mm(a, b):
    return jnp.dot(a.astype(BF16), b.astype(BF16), preferred_element_type=F32)


def _mm_nt(a, b):
    return lax.dot_general(a.astype(BF16), b.astype(BF16), NT_DIMS, preferred_element_type=F32)


def _mm_tn(a, b):
    return lax.dot_general(a.astype(BF16), b.astype(BF16), TN_DIMS, preferred_element_type=F32)


def _split3(x):
    hi = x.astype(BF16)
    r = x - hi.astype(F32)
    mid = r.astype(BF16)
    return hi, mid, (r - mid.astype(F32)).astype(BF16)


def _mm_sel(sel, x):
    return sum(jnp.dot(sel, p, preferred_element_type=F32) for p in _split3(x))


def _mm_sel_r(x, sel):
    return sum(jnp.dot(p, sel, preferred_element_type=F32) for p in _split3(x))


def _rms(x, g):
    return x * lax.rsqrt(jnp.mean(x * x, axis=-1, keepdims=True) + EPS) * g


def _silu(x):
    return x * jax.nn.sigmoid(x)


def _softplus(x):
    return jnp.maximum(x, 0.0) + jnp.log1p(jnp.exp(-jnp.abs(x)))


def _params(*sem):
    return pltpu.CompilerParams(dimension_semantics=sem, vmem_limit_bytes=VMEM_LIMIT)


def _inproj_body(x_ref, g_ref, w_ref, wba_ref, o_ref, oba_ref, wout_ref, h_scr):
    @pl.when(pl.program_id(0) == 0)
    def _():
        hb = _rms(x_ref[...], g_ref[...]).astype(BF16)
        h_scr[...] = hb
        oba_ref[...] = jnp.dot(hb, wba_ref[...], preferred_element_type=F32)

    wb = w_ref[...].astype(BF16)
    wout_ref[...] = wb
    o_ref[...] = jnp.dot(h_scr[...], wb, preferred_element_type=F32)


def _inproj(x, g, w_f32, n, w_ba, tn=1024):
    m = x.shape[0]
    assert n % tn == 0 and n <= w_f32.shape[1]
    return pl.pallas_call(
        _inproj_body,
        grid=(n // tn,),
        in_specs=[
            pl.BlockSpec((m, D_MODEL), lambda j: (0, 0)),
            pl.BlockSpec((1, D_MODEL), lambda j: (0, 0)),
            pl.BlockSpec((D_MODEL, tn), lambda j: (0, j)),
            pl.BlockSpec((D_MODEL, BA_PAD), lambda j: (0, 0)),
        ],
        out_specs=[
            pl.BlockSpec((m, tn), lambda j: (0, j)),
            pl.BlockSpec((m, BA_PAD), lambda j: (0, 0)),
            pl.BlockSpec((D_MODEL, tn), lambda j: (0, j)),
        ],
        out_shape=[
            jax.ShapeDtypeStruct((m, n), F32),
            jax.ShapeDtypeStruct((m, BA_PAD), F32),
            jax.ShapeDtypeStruct((D_MODEL, n), BF16),
        ],
        scratch_shapes=[pltpu.VMEM((m, D_MODEL), BF16)],
        compiler_params=_params("arbitrary"),
        name="inproj",
    )(x, g, w_f32, w_ba)


_RET_LOG_G = np.log1p(-np.exp2(-5.0 - np.arange(RET_HEADS, dtype=np.float64)))


def _ret_tables(c):
    idx = np.arange(c, dtype=np.float64)
    diff = idx[:, None] - idx[None, :]
    dmat = np.where(diff >= 0, np.exp(np.maximum(diff, 0.0)[None] * _RET_LOG_G[:, None, None]), 0.0)
    qdec = np.exp((idx + 1.0)[None, :] * _RET_LOG_G[:, None])
    kdec = np.exp((c - 1.0 - idx)[None, :] * _RET_LOG_G[:, None])
    lane = np.ones((1, 1, RET_DK))
    return (jnp.asarray(dmat, F32), jnp.asarray(qdec[:, :, None] * lane, F32),
            jnp.asarray(kdec[:, :, None] * lane, F32), [float(v) for v in np.exp(c * _RET_LOG_G)])


def _rot(x, cos, sin):
    return x * cos + pltpu.roll(x, RET_DK // 2, 1) * sin


def _group_norm_gate(o, gate, gn):
    mu = jnp.mean(o, axis=-1, keepdims=True)
    d = o - mu
    var = jnp.mean(d * d, axis=-1, keepdims=True)
    return _silu(gate) * (d * lax.rsqrt(var + EPS) * gn)


def _proj_tiles(hb, w_ref, dst, width=512):
    for c0 in range(0, w_ref.shape[1], width):
        dst[:, c0:c0 + width] = jnp.dot(hb, w_ref[:, c0:c0 + width], preferred_element_type=F32)


def _next_block(nt, nblocks):
    def index_map(i, j):
        n1 = jnp.minimum(i * nt + j + 1, nblocks - 1)
        return (n1 // nt, n1 % nt, 0)
    return index_map


def _ret_body(cdec, x0_ref, xn_ref, nmg_ref, w_ref, *rest):
    *consts, o_ref, s_out_ref, s_scr, pa, pb = rest
    t = pl.program_id(1)
    n = pl.program_id(0) * pl.num_programs(1) + t
    bufs = (pa, pb)

    @pl.when(t == 0)
    def _():
        s_scr[...] = jnp.zeros_like(s_scr)

    @pl.when(n == 0)
    def _():
        _proj_tiles(_rms(x0_ref[0], nmg_ref[...]).astype(BF16), w_ref, pa)

    for slot in range(2):
        @pl.when(n % 2 == slot)
        def _(slot=slot):
            _ret_step(cdec, t, xn_ref, nmg_ref, w_ref, bufs[slot], bufs[1 - slot], *consts, o_ref, s_out_ref, s_scr)


def _ret_step(cdec, t, xn_ref, nmg_ref, w_ref, proj, proj_next, cos_ref, sin_ref, dmat_ref, qdec_ref, kdec_ref,
              gn_ref, o_ref, s_out_ref, s_scr):
    qw = RET_HEADS * RET_DK
    _proj_tiles(_rms(xn_ref[0], nmg_ref[...]).astype(BF16), w_ref, proj_next)

    chunk = dmat_ref.shape[1]
    for c0 in range(0, proj.shape[0], chunk):
        rows = slice(c0, c0 + chunk)
        cos, sin = cos_ref[rows, :], sin_ref[rows, :]
        for h in range(RET_HEADS):
            qk = slice(h * RET_DK, (h + 1) * RET_DK)
            kk = slice(qw + h * RET_DK, qw + (h + 1) * RET_DK)
            vv = slice(h * RET_DV, (h + 1) * RET_DV)
            q = _rot(proj[rows, qk], cos, sin)
            k = _rot(proj[rows, kk], cos, sin) * (RET_DK ** -0.5)
            v = proj[rows, 2 * qw + h * RET_DV:2 * qw + (h + 1) * RET_DV]
            gate = proj[rows, 2 * qw + (RET_HEADS + h) * RET_DV:2 * qw + (RET_HEADS + h + 1) * RET_DV]
            s = s_scr[h]
            scores = _mm_nt(q, k) * dmat_ref[h]
            o = _mm(scores, v) + _mm(q * qdec_ref[h], s)
            s_scr[h] = cdec[h] * s + _mm_tn(k * kdec_ref[h], v)
            o_ref[0, rows, vv] = _group_norm_gate(o, gate, gn_ref[:, vv]).astype(BF16)

    @pl.when(t == pl.num_programs(1) - 1)
    def _():
        s_out_ref[0] = s_scr[...]


def _retention_prompt(x3, mix_g, w_rg, cos, sin, gn, tb=512, chunk=256):
    b, t, _ = x3.shape
    nt = t // tb
    dmat, qdec, kdec, cdec = _ret_tables(chunk)
    vw = RET_HEADS * RET_DV
    const3 = lambda i, j: (0, 0, 0)
    return pl.pallas_call(
        functools.partial(_ret_body, cdec),
        grid=(b, nt),
        in_specs=[
            pl.BlockSpec((1, tb, D_MODEL), const3),
            pl.BlockSpec((1, tb, D_MODEL), _next_block(nt, b * nt)),
            pl.BlockSpec((1, D_MODEL), lambda i, j: (0, 0)),
            pl.BlockSpec((D_MODEL, COL_GDN), lambda i, j: (0, COL_RET // COL_GDN), pipeline_mode=pl.Buffered(1)),
            pl.BlockSpec((tb, RET_DK), lambda i, j: (j, 0)),
            pl.BlockSpec((tb, RET_DK), lambda i, j: (j, 0)),
            pl.BlockSpec((RET_HEADS, chunk, chunk), const3),
            pl.BlockSpec((RET_HEADS, chunk, RET_DK), const3),
            pl.BlockSpec((RET_HEADS, chunk, RET_DK), const3),
            pl.BlockSpec((1, vw), lambda i, j: (0, 0)),
        ],
        out_specs=[
            pl.BlockSpec((1, tb, vw), lambda i, j: (i, j, 0)),
            pl.BlockSpec((1, RET_HEADS, RET_DK, RET_DV), lambda i, j: (i, 0, 0, 0)),
        ],
        out_shape=[
            jax.ShapeDtypeStruct((b, t, vw), BF16),
            jax.ShapeDtypeStruct((b, RET_HEADS, RET_DK, RET_DV), F32),
        ],
        scratch_shapes=[pltpu.VMEM((RET_HEADS, RET_DK, RET_DV), F32)]
        + 2 * [pltpu.VMEM((tb, COL_GDN), F32)],
        compiler_params=_params("arbitrary", "arbitrary"),
        name="retention_prompt",
    )(x3, x3, mix_g, w_rg, cos, sin, dmat, qdec, kdec, gn)


def _gdn_tables(tb):
    idx = np.arange(tb)
    same = (idx[:, None] // GDN_CHUNK) == (idx[None, :] // GDN_CHUNK)
    lower = same & (idx[:, None] >= idx[None, :])
    nchunk = tb // GDN_CHUNK
    chunk_sel = np.repeat((idx[:, None] // GDN_CHUNK) == np.arange(nchunk)[None, :], 128, axis=1)
    grp = 2 * GDN_CHUNK
    il = (idx % grp)[:, None]
    jl = np.arange(grp)[None, :]
    bias = np.where((il // GDN_CHUNK == jl // GDN_CHUNK) & (il >= jl), 0.0, -1e30)
    eye = (il == jl).astype(np.float64)
    return (jnp.asarray(lower, BF16), jnp.asarray(lower.T, BF16), jnp.asarray(same, BF16),
            jnp.asarray(chunk_sel, BF16), jnp.asarray(bias, F32), jnp.asarray(eye - 1.0, F32),
            jnp.asarray(eye, BF16))


def _gdn_prepare(x_ref, first_of_seq, nmg_ref, wqkv_ref, wz_ref, wba_ref, wbat_ref, cw_ref, xr, cs_buf, zs, bas,
                 bats, tb):
    st = {}

    def start():
        xr[0:8, :] = jnp.where(first_of_seq, 0.0, xr[tb:tb + 8, :])
        st["hb"] = _rms(x_ref[0], nmg_ref[...]).astype(BF16)

    def tile(w_ref, dst, c0, width=512):
        def run():
            dst[:, c0:c0 + width] = jnp.dot(st["hb"], w_ref[:, c0:c0 + width], preferred_element_type=F32)
        return run

    def logits():
        bas[...] = jnp.dot(st["hb"], wba_ref[...], preferred_element_type=F32)
        bats[...] = lax.dot_general(wbat_ref[...], st["hb"], NT_DIMS, preferred_element_type=F32)

    def conv(c0):
        def run():
            cs = slice(c0, c0 + GDN_DK)
            acc = xr[5:5 + tb, cs] * cw_ref[0:1, cs]
            for i in range(1, CONV_W):
                acc = acc + xr[5 + i:5 + i + tb, cs] * cw_ref[i:i + 1, cs]
            cs_buf[:, cs] = _silu(acc)
        return run

    mxu_items = ([start] + [tile(wqkv_ref, xr.at[8:8 + tb], c0) for c0 in range(0, CONV_CH, 512)]
                 + [tile(wz_ref, zs, c0) for c0 in range(0, wz_ref.shape[1], 512)] + [logits])
    vpu_items = [conv(c0) for c0 in range(0, CONV_CH, GDN_DK)]
    return mxu_items, vpu_items


class _Background:
    def __init__(self, items):
        self._items = list(items)

    def __call__(self, count):
        for item in self._items[:count]:
            item()
        del self._items[:count]

    def drain(self):
        self(len(self._items))


def _gdn_body(tb, x0_ref, xn_ref, nmg_ref, wqkv_ref, wz_ref, wba_ref, wbat_ref, cw_ref, *rest):
    *consts, o_ref, s_out_ref, conv_out_ref, s_scr, xr, ca, za, baa, bata, cb, zb, bab, batb = rest
    t = pl.program_id(1)
    nt = pl.num_programs(1)
    n = pl.program_id(0) * nt + t
    weights = (nmg_ref, wqkv_ref, wz_ref, wba_ref, wbat_ref, cw_ref)
    bufs = ((ca, za, baa, bata), (cb, zb, bab, batb))

    @pl.when(t == 0)
    def _():
        s_scr[...] = jnp.zeros_like(s_scr)

    @pl.when(n == 0)
    def _():
        mxu_items, vpu_items = _gdn_prepare(x0_ref, True, *weights, xr, *bufs[0], tb)
        _Background(mxu_items + vpu_items).drain()

    for slot in range(2):
        @pl.when(n % 2 == slot)
        def _(slot=slot):
            mxu_items, vpu_items = _gdn_prepare(xn_ref, t == nt - 1, *weights, xr, *bufs[1 - slot], tb)
            _gdn_step(tb, t, bufs[slot], *consts, o_ref, s_out_ref, s_scr,
                      _Background(mxu_items), _Background(vpu_items))

    @pl.when(t == nt - 2)
    def _():
        conv_out_ref[0] = xr[tb + 5:tb + 8, :]


def _gdn_step(tb, t, cur, alog_r_ref, dt_r_ref, alog_c_ref, dt_c_ref, ng_ref,
              lbd_ref, ubd_ref, obd_ref, csel_ref, bias_ref, offd_ref, eye_ref, o_ref, s_out_ref, s_scr,
              bg_mxu, bg_vpu):
    nchunk = tb // GDN_CHUNK
    hk = GDN_HEADS * GDN_DK
    c_scr, zz, ba_ref, bat_ref = cur
    ba, bat = ba_ref[...], bat_ref[...]

    bg_mxu(2)
    beta_c = jax.nn.sigmoid(ba)
    g_c = -jnp.exp(alog_r_ref[...]) * _softplus(ba + dt_r_ref[...])
    g_r = -jnp.exp(alog_c_ref[...]) * _softplus(bat + dt_c_ref[...])
    gc_c = _mm_sel(lbd_ref[...], g_c)
    gt_c = _mm_sel(obd_ref[...], g_c)
    gc_r = _mm_sel_r(g_r, ubd_ref[...])
    gt_l = _mm_sel_r(g_r, csel_ref[...])

    heads = range(GDN_HEADS)
    grp = 2 * GDN_CHUNK
    groups = [slice(p * grp, (p + 1) * grp) for p in range(tb // grp)]

    def grp_dot(a, b):
        return jnp.concatenate([jnp.dot(a[g], b[g], preferred_element_type=F32) for g in groups], axis=0)

    def grp_dot_nt(a, b):
        return jnp.concatenate([lax.dot_general(a[g], b[g], NT_DIMS, preferred_element_type=F32)
                                for g in groups], axis=0)

    bias = bias_ref[...]
    offdiag = offd_ref[...]
    eye_b = eye_ref[...]
    qs, ks, gammas, pbs, rhss, qgs, khs = [], [], [], [], [], [], []
    for h in heads:
        q = c_scr[:, h * GDN_DK:(h + 1) * GDN_DK]
        k = c_scr[:, hk + h * GDN_DK:hk + (h + 1) * GDN_DK]
        v = c_scr[:, 2 * hk + h * GDN_DV:2 * hk + (h + 1) * GDN_DV]
        q = q * lax.rsqrt(jnp.sum(q * q, axis=-1, keepdims=True) + EPS) * (GDN_DK ** -0.5)
        k = k * lax.rsqrt(jnp.sum(k * k, axis=-1, keepdims=True) + EPS)
        beta = beta_c[:, h:h + 1]
        gcc = gc_c[:, 8 + h:9 + h]
        gtc = gt_c[:, 8 + h:9 + h]
        gcr = gc_r[8 + h:9 + h, :]
        dg = jnp.concatenate([gcc[g] - gcr[:, g] for g in groups], axis=0)
        gamma = jnp.exp(dg + bias)
        kbeta = k * beta
        kb = k.astype(BF16)
        pbs.append((grp_dot_nt(kbeta.astype(BF16), kb) * (gamma * offdiag)).astype(BF16))
        eg = jnp.exp(gcc)
        rhss.append(jnp.concatenate([v * beta, kbeta * eg], axis=1))
        qs.append(q.astype(BF16))
        ks.append(kb)
        gammas.append(gamma)
        qgs.append(q * eg)
        khs.append(k * jnp.exp(gtc - gcc))
        bg_mxu(1)
        bg_vpu(1)
    bg_mxu.drain()
    bg_vpu(4)

    def solve(_, carry):
        pb = pbs
        tbs = [pb[h] + eye_b for h in heads]
        for lvl in range(5):
            pb = [grp_dot(pb[h], pb[h]).astype(BF16) for h in heads]
            bg_vpu(1)
            tnew = [grp_dot(tbs[h], pb[h] + eye_b) for h in heads]
            bg_vpu(1)
            tbs = [x.astype(BF16) for x in tnew]

        us, ws, qks = [], [], []
        for h in heads:
            uw = rhss[h] + grp_dot(tbs[h] - eye_b, rhss[h].astype(BF16))
            us.append(uw[:, :GDN_DV])
            ws.append(uw[:, GDN_DV:])
            qks.append((grp_dot_nt(qs[h], ks[h]) * gammas[h]).astype(BF16))
        bg_vpu(2)

        s = [s_scr[h] for h in heads]
        vn_parts = [[] for _ in heads]
        qs_parts = [[] for _ in heads]
        for c in range(nchunk):
            rows = slice(c * GDN_CHUNK, (c + 1) * GDN_CHUNK)
            for h in heads:
                wq = _mm(jnp.concatenate([ws[h][rows], qgs[h][rows]], axis=0), s[h])
                vn = us[h][rows] - wq[:GDN_CHUNK]
                qs_parts[h].append(wq[GDN_CHUNK:])
                vn_parts[h].append(vn)
                decay = jnp.exp(gt_l[8 + h:9 + h, c * 128:(c + 1) * 128])
                s[h] = decay * s[h] + _mm_tn(khs[h][rows], vn)
            bg_vpu(1)
        bg_vpu.drain()
        for h in heads:
            hs = slice(h * GDN_DV, (h + 1) * GDN_DV)
            s_scr[h] = s[h]
            vn = jnp.concatenate(vn_parts[h], axis=0).astype(BF16)
            o = jnp.concatenate(qs_parts[h], axis=0) + grp_dot(qks[h], vn)
            o_ref[0, :, hs] = (_rms(o, ng_ref[...]) * _silu(zz[:, hs])).astype(BF16)
        return carry

    lax.fori_loop(0, jnp.minimum(t + 1, 1), solve, 0)

    @pl.when(t == pl.num_programs(1) - 1)
    def _():
        s_out_ref[0] = s_scr[...]


def _gdn_prompt(x3, mix_g, w_rg, w_ba, w_bat, conv_w, alog_r, dt_r, alog_c, dt_c, norm_g, tb=256):
    b, t, _ = x3.shape
    nt = t // tb
    assert t % tb == 0 and nt >= 2, "a sequence's last block must be prepared during one of its own steps"
    lbd, ubd, obd, csel, bias, offdiag, eye = _gdn_tables(tb)
    vw = GDN_HEADS * GDN_DV
    c2 = lambda i, j: (0, 0)
    single = pl.Buffered(1)
    return pl.pallas_call(
        functools.partial(_gdn_body, tb),
        grid=(b, nt),
        in_specs=[
            pl.BlockSpec((1, tb, D_MODEL), lambda i, j: (0, 0, 0)),
            pl.BlockSpec((1, tb, D_MODEL), _next_block(nt, b * nt)),
            pl.BlockSpec((1, D_MODEL), c2),
            pl.BlockSpec((D_MODEL, CONV_CH), lambda i, j: (0, COL_GDN // CONV_CH), pipeline_mode=single),
            pl.BlockSpec((D_MODEL, vw), lambda i, j: (0, (COL_GDN + CONV_CH) // vw), pipeline_mode=single),
            pl.BlockSpec((D_MODEL, BA_PAD), c2),
            pl.BlockSpec((BA_PAD, D_MODEL), c2),
            pl.BlockSpec((CONV_W, CONV_CH), c2),
            pl.BlockSpec((1, BA_PAD), c2),
            pl.BlockSpec((1, BA_PAD), c2),
            pl.BlockSpec((BA_PAD, 1), c2),
            pl.BlockSpec((BA_PAD, 1), c2),
            pl.BlockSpec((1, GDN_DV), c2),
            pl.BlockSpec((tb, tb), c2),
            pl.BlockSpec((tb, tb), c2),
            pl.BlockSpec((tb, tb), c2),
            pl.BlockSpec((tb, (tb // GDN_CHUNK) * 128), c2),
            pl.BlockSpec((tb, 2 * GDN_CHUNK), c2),
            pl.BlockSpec((tb, 2 * GDN_CHUNK), c2),
            pl.BlockSpec((tb, 2 * GDN_CHUNK), c2),
        ],
        out_specs=[
            pl.BlockSpec((1, tb, vw), lambda i, j: (i, j, 0)),
            pl.BlockSpec((1, GDN_HEADS, GDN_DK, GDN_DV), lambda i, j: (i, 0, 0, 0)),
            pl.BlockSpec((1, CONV_W - 1, CONV_CH), lambda i, j: (i, 0, 0)),
        ],
        out_shape=[
            jax.ShapeDtypeStruct((b, t, vw), BF16),
            jax.ShapeDtypeStruct((b, GDN_HEADS, GDN_DK, GDN_DV), F32),
            jax.ShapeDtypeStruct((b, CONV_W - 1, CONV_CH), F32),
        ],
        scratch_shapes=[pltpu.VMEM((GDN_HEADS, GDN_DK, GDN_DV), F32), pltpu.VMEM((tb + 8, CONV_CH), F32)]
        + 2 * [pltpu.VMEM((tb, CONV_CH), F32), pltpu.VMEM((tb, vw), F32),
               pltpu.VMEM((tb, BA_PAD), F32), pltpu.VMEM((BA_PAD, tb), F32)],
        compiler_params=_params("arbitrary", "arbitrary"),
        name="gdn_prompt",
    )(x3, x3, mix_g, w_rg, w_rg, w_ba, w_bat, conv_w, alog_r, dt_r, alog_c, dt_c, norm_g, lbd, ubd, obd, csel,
      bias, offdiag, eye)


def _block_diag_rows(x, nblk):
    row = lax.broadcasted_iota(jnp.int32, (8, nblk * 128), 0)
    blk = lax.broadcasted_iota(jnp.int32, (8, nblk * 128), 1) // 128
    return jnp.where(row == blk, jnp.concatenate([x] * nblk, axis=1), 0.0)


def _sample_step_body(cdec, qk_ref, rv_ref, rg_ref, x_ref, z_ref, bcol_ref, sr_ref, sg_ref, sc_ref, cos_ref, sin_ref,
                      cw_ref, alog_ref, dt_ref, gn_ref, ng_ref,
                      oa_ref, ob_ref, sr_out_ref, sg_out_ref, sc_out_ref):
    nh = GDN_HEADS
    cos, sin = cos_ref[...], sin_ref[...]
    gam = jnp.where(lax.broadcasted_iota(jnp.int32, (8, 1), 0) == 0, cdec[0], 0.0)
    for h in range(1, RET_HEADS):
        gam = jnp.where(lax.broadcasted_iota(jnp.int32, (8, 1), 0) == h, cdec[h], gam)
    zero4 = jnp.zeros((RET_HEADS, RET_DV), F32)
    for g in range(x_ref.shape[0]):
        x_new = x_ref[g]
        buf = sc_ref[g]
        conv = x_new * cw_ref[CONV_W - 1]
        for i in range(CONV_W - 1):
            conv = conv + buf[i] * cw_ref[i]
        conv = _silu(conv)
        sc_out_ref[g, 0] = buf[1]
        sc_out_ref[g, 1] = buf[2]
        sc_out_ref[g, 2] = x_new
        q, k, v = conv[0:nh], conv[nh:2 * nh], conv[2 * nh:3 * nh]
        q = q * lax.rsqrt(jnp.sum(q * q, axis=-1, keepdims=True) + EPS) * (GDN_DK ** -0.5)
        k = k * lax.rsqrt(jnp.sum(k * k, axis=-1, keepdims=True) + EPS)
        col = bcol_ref[g]
        beta = jax.nn.sigmoid(col[0:nh])
        eg = jnp.exp(-jnp.exp(alog_ref[...]) * _softplus(col[nh:2 * nh] + dt_ref[...]))
        kbd, qbd = _block_diag_rows(k, nh), _block_diag_rows(q, nh)
        s_flat = sg_ref[g].reshape(nh * GDN_DK, GDN_DV)
        kq_s = _mm(jnp.concatenate([kbd, qbd], axis=0), s_flat)
        vn = beta * (v - eg * kq_s[0:nh])
        o = eg * kq_s[nh:2 * nh] + jnp.sum(q * k, axis=-1, keepdims=True) * vn
        ob_ref[g] = _rms(o, ng_ref[...]) * _silu(z_ref[g])
        upd = _mm_tn(kbd, vn)
        eg_l = jnp.broadcast_to(eg, (nh, GDN_DV))
        for h in range(nh):
            sg_out_ref[g, h] = eg_l[h:h + 1] * sg_ref[g, h] + upd[h * GDN_DK:(h + 1) * GDN_DK]

        qk = _rot(qk_ref[g], cos, sin)
        rq = jnp.concatenate([qk[0:RET_HEADS], qk[0:RET_HEADS]], axis=0)
        rk = jnp.concatenate([qk[RET_HEADS:], qk[RET_HEADS:]], axis=0) * (RET_DK ** -0.5)
        rv = jnp.concatenate([rv_ref[g], zero4], axis=0)
        qbd_r, kbd_r = _block_diag_rows(rq, RET_HEADS), _block_diag_rows(rk, RET_HEADS)
        sr_flat = sr_ref[g].reshape(RET_HEADS * RET_DK, RET_DV)
        q_s = _mm(qbd_r, sr_flat)
        o_r = gam * q_s + jnp.sum(rq * rk, axis=-1, keepdims=True) * rv
        oa_ref[g] = _group_norm_gate(o_r[0:RET_HEADS], rg_ref[g], gn_ref[...])
        upd_r = _mm_tn(kbd_r, rv)
        for h in range(RET_HEADS):
            sr_out_ref[g, h] = cdec[h] * sr_ref[g, h] + upd_r[h * RET_DK:(h + 1) * RET_DK]


def _sample_step(proj_ret, proj_gdn, ba, state_ret, state_gdn, state_conv, cos, sin, conv_w, a_log, dt_bias, gn,
                 norm_g, gs=8):
    ns = proj_ret.shape[0]
    nh, rh = GDN_HEADS, RET_HEADS
    nq = rh * RET_DK
    qk3 = proj_ret[:, :2 * nq].reshape(ns, 2 * rh, RET_DK)
    rv3 = proj_ret[:, 2 * nq:2 * nq + rh * RET_DV].reshape(ns, rh, RET_DV)
    rg3 = proj_ret[:, 2 * nq + rh * RET_DV:].reshape(ns, rh, RET_DV)
    x3 = proj_gdn[:, :CONV_CH].reshape(ns, 3 * nh, GDN_DK)
    z3 = proj_gdn[:, CONV_CH:].reshape(ns, nh, GDN_DV)
    bcol = ba[:, :2 * nh].reshape(ns, 2 * nh, 1)
    sc4 = state_conv.reshape(ns, CONV_W - 1, 3 * nh, GDN_DK)
    cw3 = conv_w.reshape(CONV_W, 3 * nh, GDN_DK)
    cdec = [float(v) for v in np.exp(_RET_LOG_G)]
    c2 = lambda i: (0, 0)
    c3 = lambda i: (0, 0, 0)
    b3 = lambda i: (i, 0, 0)
    b4 = lambda i: (i, 0, 0, 0)
    oa, ob, sr, sg, sc = pl.pallas_call(
        functools.partial(_sample_step_body, cdec),
        grid=(ns // gs,),
        in_specs=[
            pl.BlockSpec((gs, 2 * rh, RET_DK), b3),
            pl.BlockSpec((gs, rh, RET_DV), b3),
            pl.BlockSpec((gs, rh, RET_DV), b3),
            pl.BlockSpec((gs, 3 * nh, GDN_DK), b3),
            pl.BlockSpec((gs, nh, GDN_DV), b3),
            pl.BlockSpec((gs, 2 * nh, 1), b3),
            pl.BlockSpec((gs, rh, RET_DK, RET_DV), b4),
            pl.BlockSpec((gs, nh, GDN_DK, GDN_DV), b4),
            pl.BlockSpec((gs, CONV_W - 1, 3 * nh, GDN_DK), b4),
            pl.BlockSpec((1, RET_DK), c2),
            pl.BlockSpec((1, RET_DK), c2),
            pl.BlockSpec((CONV_W, 3 * nh, GDN_DK), c3),
            pl.BlockSpec((nh, 1), c2),
            pl.BlockSpec((nh, 1), c2),
            pl.BlockSpec((rh, RET_DV), c2),
            pl.BlockSpec((1, GDN_DV), c2),
        ],
        out_specs=[
            pl.BlockSpec((gs, rh, RET_DV), b3),
            pl.BlockSpec((gs, nh, GDN_DV), b3),
            pl.BlockSpec((gs, rh, RET_DK, RET_DV), b4),
            pl.BlockSpec((gs, nh, GDN_DK, GDN_DV), b4),
            pl.BlockSpec((gs, CONV_W - 1, 3 * nh, GDN_DK), b4),
        ],
        out_shape=[
            jax.ShapeDtypeStruct((ns, rh, RET_DV), F32),
            jax.ShapeDtypeStruct((ns, nh, GDN_DV), F32),
            jax.ShapeDtypeStruct(state_ret.shape, F32),
            jax.ShapeDtypeStruct(state_gdn.shape, F32),
            jax.ShapeDtypeStruct(sc4.shape, F32),
        ],
        compiler_params=_params("parallel"),
        name="sample_step",
    )(qk3, rv3, rg3, x3, z3, bcol, state_ret, state_gdn, sc4, cos, sin, cw3, a_log.reshape(nh, 1),
      dt_bias.reshape(nh, 1), gn.reshape(rh, RET_DV), norm_g)
    return (oa.reshape(ns, rh * RET_DV).astype(BF16), ob.reshape(ns, nh * GDN_DV).astype(BF16), sr, sg,
            sc.reshape(state_conv.shape))


def _merge_body(x_ref, nmg_ref, wgt_ref, oa_ref, ob_ref, wa_ref, wb_ref, wo_ref, ng_ref, wq_ref, x1_ref, q_ref):
    x = x_ref[...]
    gates = jnp.dot(_rms(x, nmg_ref[...]).astype(BF16), wgt_ref[...], preferred_element_type=F32)
    ya = jnp.dot(oa_ref[...], wa_ref[...], preferred_element_type=F32)
    yb = jnp.dot(ob_ref[...], wb_ref[...], preferred_element_type=F32)
    merged = jax.nn.sigmoid(gates[:, :D_MODEL]) * ya + jax.nn.sigmoid(gates[:, D_MODEL:]) * yb
    x1 = x + _mm(merged, wo_ref[...])
    x1_ref[...] = x1
    q_ref[...] = _mm(_rms(x1, ng_ref[...]), wq_ref[...]).astype(BF16)


def _merge(x, mix_g, w_gates, oa, ob, wa, wb, wo, ng, wq, tm):
    m = x.shape[0]
    row = lambda i: (i, 0)
    c2 = lambda i: (0, 0)
    wspec = pl.BlockSpec((D_MODEL, D_MODEL), c2)
    return pl.pallas_call(
        _merge_body,
        grid=(m // tm,),
        in_specs=[
            pl.BlockSpec((tm, D_MODEL), row),
            pl.BlockSpec((1, D_MODEL), c2),
            pl.BlockSpec((D_MODEL, 2 * D_MODEL), c2),
            pl.BlockSpec((tm, D_MODEL), row),
            pl.BlockSpec((tm, D_MODEL), row),
            wspec, wspec, wspec,
            pl.BlockSpec((1, D_MODEL), c2),
            wspec,
        ],
        out_specs=[pl.BlockSpec((tm, D_MODEL), row), pl.BlockSpec((tm, D_MODEL), row)],
        out_shape=[jax.ShapeDtypeStruct((m, D_MODEL), F32), jax.ShapeDtypeStruct((m, D_MODEL), BF16)],
        compiler_params=_params("parallel"),
        name="merge",
    )(x, mix_g, w_gates, oa, ob, wa, wb, wo, ng, wq)


def _memkv_body(m_ref, g_ref, wk_ref, wv_ref, k_ref, v_ref):
    mn = _rms(m_ref[...], g_ref[...]).astype(BF16)
    k_ref[...] = jnp.dot(mn, wk_ref[...], preferred_element_type=F32)
    v_ref[...] = jnp.dot(mn, wv_ref[...], preferred_element_type=F32)


def _memkv(mem, g, wk, wv, tm=512):
    m = mem.shape[0]
    row = lambda i: (i, 0)
    c2 = lambda i: (0, 0)
    return pl.pallas_call(
        _memkv_body,
        grid=(m // tm,),
        in_specs=[pl.BlockSpec((tm, D_MODEL), row), pl.BlockSpec((1, D_MODEL), c2),
                  pl.BlockSpec((D_MODEL, D_MODEL), c2), pl.BlockSpec((D_MODEL, D_MODEL), c2)],
        out_specs=[pl.BlockSpec((tm, D_MODEL), row), pl.BlockSpec((tm, D_MODEL), row)],
        out_shape=[jax.ShapeDtypeStruct((m, D_MODEL), F32)] * 2,
        compiler_params=_params("parallel"),
        name="memkv",
    )(mem, g, wk, wv)


def _xattn_body(q_ref, mk_ref, mv_ref, x1_ref, wo_ref, x2_ref):
    parts = []
    for h in range(X_HEADS):
        hs = slice(h * X_HD, (h + 1) * X_HD)
        s = _mm_nt(q_ref[0, :, hs], mk_ref[0, :, hs]) * (X_HD ** -0.5)
        p = jnp.exp(s - jnp.max(s, axis=-1, keepdims=True))
        o = _mm(p, mv_ref[0, :, hs]) / jnp.sum(p, axis=-1, keepdims=True)
        parts.append(o.astype(BF16))
    x2_ref[0] = x1_ref[0] + jnp.dot(jnp.concatenate(parts, axis=1), wo_ref[...], preferred_element_type=F32)


def _xattn_prompt(q3, mk3, mv3, x13, wo, tq=512):
    b, t, _ = q3.shape
    tok = lambda i, j: (i, j, 0)
    mem = lambda i, j: (i, 0, 0)
    return pl.pallas_call(
        _xattn_body,
        grid=(b, t // tq),
        in_specs=[pl.BlockSpec((1, tq, D_MODEL), tok), pl.BlockSpec((1, N_MEM, D_MODEL), mem),
                  pl.BlockSpec((1, N_MEM, D_MODEL), mem), pl.BlockSpec((1, tq, D_MODEL), tok),
                  pl.BlockSpec((D_MODEL, D_MODEL), lambda i, j: (0, 0))],
        out_specs=pl.BlockSpec((1, tq, D_MODEL), tok),
        out_shape=jax.ShapeDtypeStruct((b, t, D_MODEL), F32),
        compiler_params=_params("parallel", "parallel"),
        name="xattn_prompt",
    )(q3, mk3, mv3, x13, wo)


def _xattn_sample_body(q_ref, mk_ref, mv_ref, o_ref):
    for g in range(q_ref.shape[0]):
        q = q_ref[g]
        s = jnp.sum(mk_ref[g] * q[None], axis=-1, keepdims=True) * (X_HD ** -0.5)
        p = jnp.exp(s - jnp.max(s, axis=0, keepdims=True))
        o_ref[g] = jnp.sum(p * mv_ref[g], axis=0) / jnp.sum(p, axis=0)


def _xattn_sample(q, mk4, mv4, gs=4):
    ns = q.shape[0]
    q3 = q.astype(F32).reshape(ns, X_HEADS, X_HD)
    row = lambda i: (i, 0, 0)
    mem = lambda i: (i, 0, 0, 0)
    return pl.pallas_call(
        _xattn_sample_body,
        grid=(ns // gs,),
        in_specs=[pl.BlockSpec((gs, X_HEADS, X_HD), row), pl.BlockSpec((gs, N_MEM, X_HEADS, X_HD), mem),
                  pl.BlockSpec((gs, N_MEM, X_HEADS, X_HD), mem)],
        out_specs=pl.BlockSpec((gs, X_HEADS, X_HD), row),
        out_shape=jax.ShapeDtypeStruct((ns, X_HEADS, X_HD), F32),
        compiler_params=_params("parallel"),
        name="xattn_sample",
    )(q3, mk4, mv4)


def _resid_mm_body(x_ref, a_ref, w_ref, o_ref):
    o_ref[...] = x_ref[...] + jnp.dot(a_ref[...], w_ref[...], preferred_element_type=F32)


def _resid_mm(x, a, w):
    m = x.shape[0]
    return pl.pallas_call(
        _resid_mm_body,
        out_shape=jax.ShapeDtypeStruct((m, D_MODEL), F32),
        compiler_params=pltpu.CompilerParams(vmem_limit_bytes=VMEM_LIMIT),
        name="resid_mm",
    )(x, a, w)


def _ffn_body(x_ref, ng_ref, wg_ref, wu_ref, wd_ref, nf_ref, y_ref):
    x = x_ref[...]
    h = _rms(x, ng_ref[...]).astype(BF16)
    gate = jnp.dot(h, wg_ref[...], preferred_element_type=F32)
    up = jnp.dot(h, wu_ref[...], preferred_element_type=F32)
    x3 = x + _mm(_silu(gate) * up, wd_ref[...])
    y_ref[...] = _rms(x3, nf_ref[...])


def _ffn(x, ng, wg, wu, wd, nf, tm):
    m = x.shape[0]
    dff = wg.shape[1]
    row = lambda i: (i, 0)
    c2 = lambda i: (0, 0)
    single = pl.Buffered(1)
    return pl.pallas_call(
        _ffn_body,
        grid=(m // tm,),
        in_specs=[pl.BlockSpec((tm, D_MODEL), row), pl.BlockSpec((1, D_MODEL), c2),
                  pl.BlockSpec((D_MODEL, dff), c2, pipeline_mode=single),
                  pl.BlockSpec((D_MODEL, dff), c2, pipeline_mode=single),
                  pl.BlockSpec((dff, D_MODEL), c2, pipeline_mode=single),
                  pl.BlockSpec((1, D_MODEL), c2)],
        out_specs=pl.BlockSpec((tm, D_MODEL), row),
        out_shape=jax.ShapeDtypeStruct((m, D_MODEL), F32),
        compiler_params=_params("parallel"),
        name="ffn",
    )(x, ng, wg, wu, wd, nf)


def _rope_tables(pos):
    half = RET_DK // 2
    inv = ROPE_BASE ** (-jnp.arange(half, dtype=F32) / half)
    ang = pos.astype(F32)[:, None] * inv[None, :]
    cos, sin = jnp.cos(ang), jnp.sin(ang)
    return jnp.concatenate([cos, cos], axis=-1), jnp.concatenate([-sin, sin], axis=-1)


def _pad_lanes(v, offset):
    return jnp.zeros((BA_PAD,), F32).at[offset:offset + v.shape[0]].set(v)


def kernel(x_prompt, x_sample, state_ret, state_gdn, state_conv, cache_mem_k, cache_mem_v, mem_prompt,
           norm_mix_g, w_in, ret_gn_g, w_branch_a, gdn_conv_w, gdn_a_log, gdn_dt_bias, gdn_norm_g,
           w_branch_b, w_out, norm_x_g, mem_norm_g, w_xq, w_xk, w_xv, w_xo, norm_ffn_g, w_gate, w_up,
           w_down, norm_final_g):
    depth = w_in.shape[0]
    assert depth == 1, "single-layer kernel"
    b, t, _ = x_prompt.shape
    ns = x_sample.shape[0]
    l = 0

    w = w_in[l]
    ba0, g0 = COL_GATE, COL_GATE + 2 * GDN_HEADS
    w_gates = w[:, g0:].astype(BF16)
    w_ba = jnp.pad(w[:, ba0:g0], ((0, 0), (0, BA_PAD - 2 * GDN_HEADS))).astype(BF16)
    w_bat = w_ba.T
    row = lambda v: v.reshape(1, -1)
    wa, wb, wo = w_branch_a[l].astype(BF16), w_branch_b[l].astype(BF16), w_out[l].astype(BF16)
    wq, wk, wv, wxo = w_xq[l].astype(BF16), w_xk[l].astype(BF16), w_xv[l].astype(BF16), w_xo[l].astype(BF16)
    wg, wu, wd = w_gate[l].astype(BF16), w_up[l].astype(BF16), w_down[l].astype(BF16)
    alog_r = _pad_lanes(gdn_a_log[l], GDN_HEADS).reshape(1, BA_PAD)
    dt_r = _pad_lanes(gdn_dt_bias[l], GDN_HEADS).reshape(1, BA_PAD)
    alog_c, dt_c = alog_r.reshape(BA_PAD, 1), dt_r.reshape(BA_PAD, 1)
    cos_p, sin_p = _rope_tables(jnp.arange(t))
    cos_s, sin_s = _rope_tables(PAST_LEN + jnp.arange(1))

    mix_g = row(norm_mix_g[l])
    xs = x_sample.reshape(ns, D_MODEL)
    proj_s, ba_s, w_bf = _inproj(xs, mix_g, w, COL_GATE, w_ba)

    xp = x_prompt.reshape(b * t, D_MODEL)
    oa_p, sr_p = _retention_prompt(x_prompt, mix_g, w_bf, cos_p, sin_p, row(ret_gn_g[l]))
    ob_p, sg_p, sc_p = _gdn_prompt(x_prompt, mix_g, w_bf, w_ba, w_bat, gdn_conv_w[l], alog_r, dt_r,
                                   alog_c, dt_c, row(gdn_norm_g[l]))
    x1_p, q_p = _merge(xp, mix_g, w_gates, oa_p.reshape(b * t, -1), ob_p.reshape(b * t, -1), wa, wb, wo,
                       row(norm_x_g[l]), wq, tm=512)
    mk_p, mv_p = _memkv(mem_prompt.reshape(b * N_MEM, D_MODEL), row(mem_norm_g[l]), wk, wv)
    x2_p = _xattn_prompt(q_p.reshape(b, t, D_MODEL), mk_p.reshape(b, N_MEM, D_MODEL),
                         mv_p.reshape(b, N_MEM, D_MODEL), x1_p.reshape(b, t, D_MODEL), wxo)
    y_p = _ffn(x2_p.reshape(b * t, D_MODEL), row(norm_ffn_g[l]), wg, wu, wd, row(norm_final_g), tm=512)

    oa_s, ob_s, sr_s, sg_s, sc_s = _sample_step(proj_s[:, :COL_GDN], proj_s[:, COL_GDN:], ba_s, state_ret[l],
                                                state_gdn[l], state_conv[l],
                                                cos_s, sin_s, gdn_conv_w[l], gdn_a_log[l], gdn_dt_bias[l],
                                                ret_gn_g[l], row(gdn_norm_g[l]))
    x1_s, q_s = _merge(xs, mix_g, w_gates, oa_s.reshape(ns, -1), ob_s.reshape(ns, -1), wa, wb, wo,
                       row(norm_x_g[l]), wq, tm=ns)
    o_s = _xattn_sample(q_s, cache_mem_k[l], cache_mem_v[l])
    x2_s = _resid_mm(x1_s, o_s.reshape(ns, D_MODEL).astype(BF16), wxo)
    y_s = _ffn(x2_s, row(norm_ffn_g[l]), wg, wu, wd, row(norm_final_g), tm=ns)

    return (y_p.reshape(b, t, D_MODEL), y_s.reshape(ns, 1, D_MODEL),
            sr_p[None], sg_p[None], sc_p[None],
            mk_p.reshape(1, b, N_MEM, X_HEADS, X_HD), mv_p.reshape(1, b, N_MEM, X_HEADS, X_HD),
            sr_s[None], sg_s[None], sc_s[None])
```

```python
import functools

import numpy as np
import jax
import jax.numpy as jnp
from jax import lax
from jax.experimental import pallas as pl
from jax.experimental.pallas import tpu as pltpu

F32 = jnp.float32
BF16 = jnp.bfloat16

D_MODEL = 1024
RET_HEADS, RET_DK, RET_DV = 4, 128, 256
GDN_HEADS, GDN_DK, GDN_DV = 8, 128, 128
CONV_W = 4
CONV_CH = 3 * GDN_HEADS * GDN_DK
N_MEM, X_HEADS, X_HD = 256, 4, 256
PAST_LEN = 16384
ROPE_BASE = 10000.0
EPS = 1e-6
GDN_CHUNK = 64

COL_RET = 0
COL_GDN = 3072
COL_GATE = 7168
BA_PAD = 128

VMEM_LIMIT = 56 * 1024 * 1024

NT_DIMS = (((1,), (1,)), ((), ()))
TN_DIMS = (((0,), (0,)), ((), ()))


def _mm(a, b):
    return jnp.dot(a.astype(BF16), b.astype(BF16), preferred_element_type=F32)


def _mm_nt(a, b):
    return lax.dot_general(a.astype(BF16), b.astype(BF16), NT_DIMS, preferred_element_type=F32)


def _mm_tn(a, b):
    return lax.dot_general(a.astype(BF16), b.astype(BF16), TN_DIMS, preferred_element_type=F32)


def _split3(x):
    hi = x.astype(BF16)
    r = x - hi.astype(F32)
    mid = r.astype(BF16)
    return hi, mid, (r - mid.astype(F32)).astype(BF16)


def _mm_sel(sel, x):
    return sum(jnp.dot(sel, p, preferred_element_type=F32) for p in _split3(x))


def _mm_sel_r(x, sel):
    return sum(jnp.dot(p, sel, preferred_element_type=F32) for p in _split3(x))


def _rms(x, g):
    return x * lax.rsqrt(jnp.mean(x * x, axis=-1, keepdims=True) + EPS) * g


def _silu(x):
    return x * jax.nn.sigmoid(x)


def _softplus(x):
    return jnp.maximum(x, 0.0) + jnp.log1p(jnp.exp(-jnp.abs(x)))


def _params(*sem):
    return pltpu.CompilerParams(dimension_semantics=sem, vmem_limit_bytes=VMEM_LIMIT)


def _inproj_body(x_ref, g_ref, w_ref, wba_ref, o_ref, oba_ref, h_scr):
    @pl.when(pl.program_id(0) == 0)
    def _():
        hb = _rms(x_ref[...], g_ref[...]).astype(BF16)
        h_scr[...] = hb
        oba_ref[...] = jnp.dot(hb, wba_ref[...], preferred_element_type=F32)

    o_ref[...] = jnp.dot(h_scr[...], w_ref[...], preferred_element_type=F32)


def _inproj(x, g, w_bf, n, w_ba, tn=1024):
    m = x.shape[0]
    assert n % tn == 0 and n <= w_bf.shape[1]
    return pl.pallas_call(
        _inproj_body,
        grid=(n // tn,),
        in_specs=[
            pl.BlockSpec((m, D_MODEL), lambda j: (0, 0)),
            pl.BlockSpec((1, D_MODEL), lambda j: (0, 0)),
            pl.BlockSpec((D_MODEL, tn), lambda j: (0, j)),
            pl.BlockSpec((D_MODEL, BA_PAD), lambda j: (0, 0)),
        ],
        out_specs=[
            pl.BlockSpec((m, tn), lambda j: (0, j)),
            pl.BlockSpec((m, BA_PAD), lambda j: (0, 0)),
        ],
        out_shape=[
            jax.ShapeDtypeStruct((m, n), F32),
            jax.ShapeDtypeStruct((m, BA_PAD), F32),
        ],
        scratch_shapes=[pltpu.VMEM((m, D_MODEL), BF16)],
        compiler_params=_params("arbitrary"),
        name="inproj",
    )(x, g, w_bf, w_ba)


_RET_LOG_G = np.log1p(-np.exp2(-5.0 - np.arange(RET_HEADS, dtype=np.float64)))


def _ret_tables(c):
    idx = np.arange(c, dtype=np.float64)
    diff = idx[:, None] - idx[None, :]
    dmat = np.where(diff >= 0, np.exp(np.maximum(diff, 0.0)[None] * _RET_LOG_G[:, None, None]), 0.0)
    qdec = np.exp((idx + 1.0)[None, :] * _RET_LOG_G[:, None])
    kdec = np.exp((c - 1.0 - idx)[None, :] * _RET_LOG_G[:, None])
    lane = np.ones((1, 1, RET_DK))
    return (jnp.asarray(dmat, F32), jnp.asarray(qdec[:, :, None] * lane, F32),
            jnp.asarray(kdec[:, :, None] * lane, F32), [float(v) for v in np.exp(c * _RET_LOG_G)])


def _rot(x, cos, sin):
    return x * cos + pltpu.roll(x, RET_DK // 2, 1) * sin


def _group_norm_gate(o, gate, gn):
    mu = jnp.mean(o, axis=-1, keepdims=True)
    d = o - mu
    var = jnp.mean(d * d, axis=-1, keepdims=True)
    return _silu(gate) * (d * lax.rsqrt(var + EPS) * gn)


def _proj_tiles(hb, w_ref, dst, width=512):
    for c0 in range(0, w_ref.shape[1], width):
        dst[:, c0:c0 + width] = jnp.dot(hb, w_ref[:, c0:c0 + width], preferred_element_type=F32)


def _next_block(nt, nblocks):
    def index_map(i, j):
        n1 = jnp.minimum(i * nt + j + 1, nblocks - 1)
        return (n1 // nt, n1 % nt, 0)
    return index_map


def _ret_body(cdec, x0_ref, xn_ref, nmg_ref, w_ref, *rest):
    *consts, o_ref, s_out_ref, s_scr, pa, pb = rest
    t = pl.program_id(1)
    n = pl.program_id(0) * pl.num_programs(1) + t
    bufs = (pa, pb)

    @pl.when(t == 0)
    def _():
        s_scr[...] = jnp.zeros_like(s_scr)

    @pl.when(n == 0)
    def _():
        _proj_tiles(_rms(x0_ref[0], nmg_ref[...]).astype(BF16), w_ref, pa)

    for slot in range(2):
        @pl.when(n % 2 == slot)
        def _(slot=slot):
            _ret_step(cdec, t, xn_ref, nmg_ref, w_ref, bufs[slot], bufs[1 - slot], *consts, o_ref, s_out_ref, s_scr)


def _ret_step(cdec, t, xn_ref, nmg_ref, w_ref, proj, proj_next, cos_ref, sin_ref, dmat_ref, qdec_ref, kdec_ref,
              gn_ref, o_ref, s_out_ref, s_scr):
    qw = RET_HEADS * RET_DK
    _proj_tiles(_rms(xn_ref[0], nmg_ref[...]).astype(BF16), w_ref, proj_next)

    chunk = dmat_ref.shape[1]
    for c0 in range(0, proj.shape[0], chunk):
        rows = slice(c0, c0 + chunk)
        cos, sin = cos_ref[rows, :], sin_ref[rows, :]
        for h in range(RET_HEADS):
            qk = slice(h * RET_DK, (h + 1) * RET_DK)
            kk = slice(qw + h * RET_DK, qw + (h + 1) * RET_DK)
            vv = slice(h * RET_DV, (h + 1) * RET_DV)
            q = _rot(proj[rows, qk], cos, sin)
            k = _rot(proj[rows, kk], cos, sin) * (RET_DK ** -0.5)
            v = proj[rows, 2 * qw + h * RET_DV:2 * qw + (h + 1) * RET_DV]
            gate = proj[rows, 2 * qw + (RET_HEADS + h) * RET_DV:2 * qw + (RET_HEADS + h + 1) * RET_DV]
            s = s_scr[h]
            scores = _mm_nt(q, k) * dmat_ref[h]
            o = _mm(scores, v) + _mm(q * qdec_ref[h], s)
            s_scr[h] = cdec[h] * s + _mm_tn(k * kdec_ref[h], v)
            o_ref[0, rows, vv] = _group_norm_gate(o, gate, gn_ref[:, vv]).astype(BF16)

    @pl.when(t == pl.num_programs(1) - 1)
    def _():
        s_out_ref[0] = s_scr[...]


def _retention_prompt(x3, mix_g, w_rg, cos, sin, gn, tb=512, chunk=256):
    b, t, _ = x3.shape
    nt = t // tb
    dmat, qdec, kdec, cdec = _ret_tables(chunk)
    vw = RET_HEADS * RET_DV
    const3 = lambda i, j: (0, 0, 0)
    return pl.pallas_call(
        functools.partial(_ret_body, cdec),
        grid=(b, nt),
        in_specs=[
            pl.BlockSpec((1, tb, D_MODEL), const3),
            pl.BlockSpec((1, tb, D_MODEL), _next_block(nt, b * nt)),
            pl.BlockSpec((1, D_MODEL), lambda i, j: (0, 0)),
            pl.BlockSpec((D_MODEL, COL_GDN), lambda i, j: (0, COL_RET // COL_GDN), pipeline_mode=pl.Buffered(1)),
            pl.BlockSpec((tb, RET_DK), lambda i, j: (j, 0)),
            pl.BlockSpec((tb, RET_DK), lambda i, j: (j, 0)),
            pl.BlockSpec((RET_HEADS, chunk, chunk), const3),
            pl.BlockSpec((RET_HEADS, chunk, RET_DK), const3),
            pl.BlockSpec((RET_HEADS, chunk, RET_DK), const3),
            pl.BlockSpec((1, vw), lambda i, j: (0, 0)),
        ],
        out_specs=[
            pl.BlockSpec((1, tb, vw), lambda i, j: (i, j, 0)),
            pl.BlockSpec((1, RET_HEADS, RET_DK, RET_DV), lambda i, j: (i, 0, 0, 0)),
        ],
        out_shape=[
            jax.ShapeDtypeStruct((b, t, vw), BF16),
            jax.ShapeDtypeStruct((b, RET_HEADS, RET_DK, RET_DV), F32),
        ],
        scratch_shapes=[pltpu.VMEM((RET_HEADS, RET_DK, RET_DV), F32)]
        + 2 * [pltpu.VMEM((tb, COL_GDN), F32)],
        compiler_params=_params("arbitrary", "arbitrary"),
        name="retention_prompt",
    )(x3, x3, mix_g, w_rg, cos, sin, dmat, qdec, kdec, gn)


def _gdn_tables(tb):
    idx = np.arange(tb)
    same = (idx[:, None] // GDN_CHUNK) == (idx[None, :] // GDN_CHUNK)
    lower = same & (idx[:, None] >= idx[None, :])
    nchunk = tb // GDN_CHUNK
    chunk_sel = np.repeat((idx[:, None] // GDN_CHUNK) == np.arange(nchunk)[None, :], 128, axis=1)
    grp = 2 * GDN_CHUNK
    il = (idx % grp)[:, None]
    jl = np.arange(grp)[None, :]
    bias = np.where((il // GDN_CHUNK == jl // GDN_CHUNK) & (il >= jl), 0.0, -1e30)
    eye = (il == jl).astype(np.float64)
    return (jnp.asarray(lower, BF16), jnp.asarray(lower.T, BF16), jnp.asarray(same, BF16),
            jnp.asarray(chunk_sel, BF16), jnp.asarray(bias, F32), jnp.asarray(eye - 1.0, F32),
            jnp.asarray(eye, BF16))


def _gdn_prepare(x_ref, first_of_seq, nmg_ref, wqkv_ref, wz_ref, wba_ref, wbat_ref, cw_ref, xr, cs_buf, zs, bas,
                 bats, tb):
    st = {}

    def start():
        xr[0:8, :] = jnp.where(first_of_seq, 0.0, xr[tb:tb + 8, :])
        st["hb"] = _rms(x_ref[0], nmg_ref[...]).astype(BF16)

    def tile(w_ref, dst, c0, width=512):
        def run():
            dst[:, c0:c0 + width] = jnp.dot(st["hb"], w_ref[:, c0:c0 + width], preferred_element_type=F32)
        return run

    def logits():
        bas[...] = jnp.dot(st["hb"], wba_ref[...], preferred_element_type=F32)
        bats[...] = lax.dot_general(wbat_ref[...], st["hb"], NT_DIMS, preferred_element_type=F32)

    def conv(c0):
        def run():
            cs = slice(c0, c0 + GDN_DK)
            acc = xr[5:5 + tb, cs] * cw_ref[0:1, cs]
            for i in range(1, CONV_W):
                acc = acc + xr[5 + i:5 + i + tb, cs] * cw_ref[i:i + 1, cs]
            cs_buf[:, cs] = _silu(acc)
        return run

    mxu_items = ([start] + [tile(wqkv_ref, xr.at[8:8 + tb], c0) for c0 in range(0, CONV_CH, 512)]
                 + [tile(wz_ref, zs, c0) for c0 in range(0, wz_ref.shape[1], 512)] + [logits])
    vpu_items = [conv(c0) for c0 in range(0, CONV_CH, GDN_DK)]
    return mxu_items, vpu_items


class _Background:
    def __init__(self, items):
        self._items = list(items)

    def __call__(self, count):
        for item in self._items[:count]:
            item()
        del self._items[:count]

    def drain(self):
        self(len(self._items))


def _gdn_body(tb, x0_ref, xn_ref, nmg_ref, wqkv_ref, wz_ref, wba_ref, wbat_ref, cw_ref, *rest):
    *consts, o_ref, s_out_ref, conv_out_ref, s_scr, xr, ca, za, baa, bata, cb, zb, bab, batb = rest
    t = pl.program_id(1)
    nt = pl.num_programs(1)
    n = pl.program_id(0) * nt + t
    weights = (nmg_ref, wqkv_ref, wz_ref, wba_ref, wbat_ref, cw_ref)
    bufs = ((ca, za, baa, bata), (cb, zb, bab, batb))

    @pl.when(t == 0)
    def _():
        s_scr[...] = jnp.zeros_like(s_scr)

    @pl.when(n == 0)
    def _():
        mxu_items, vpu_items = _gdn_prepare(x0_ref, True, *weights, xr, *bufs[0], tb)
        _Background(mxu_items + vpu_items).drain()

    for slot in range(2):
        @pl.when(n % 2 == slot)
        def _(slot=slot):
            mxu_items, vpu_items = _gdn_prepare(xn_ref, t == nt - 1, *weights, xr, *bufs[1 - slot], tb)
            _gdn_step(tb, t, bufs[slot], *consts, o_ref, s_out_ref, s_scr,
                      _Background(mxu_items), _Background(vpu_items))

    @pl.when(t == nt - 2)
    def _():
        conv_out_ref[0] = xr[tb + 5:tb + 8, :]


def _gdn_step(tb, t, cur, alog_r_ref, dt_r_ref, alog_c_ref, dt_c_ref, ng_ref,
              lbd_ref, ubd_ref, obd_ref, csel_ref, bias_ref, offd_ref, eye_ref, o_ref, s_out_ref, s_scr,
              bg_mxu, bg_vpu):
    nchunk = tb // GDN_CHUNK
    hk = GDN_HEADS * GDN_DK
    c_scr, zz, ba_ref, bat_ref = cur
    ba, bat = ba_ref[...], bat_ref[...]

    bg_mxu(2)
    beta_c = jax.nn.sigmoid(ba)
    g_c = -jnp.exp(alog_r_ref[...]) * _softplus(ba + dt_r_ref[...])
    g_r = -jnp.exp(alog_c_ref[...]) * _softplus(bat + dt_c_ref[...])
    gc_c = _mm_sel(lbd_ref[...], g_c)
    gt_c = _mm_sel(obd_ref[...], g_c)
    gc_r = _mm_sel_r(g_r, ubd_ref[...])
    gt_l = _mm_sel_r(g_r, csel_ref[...])

    heads = range(GDN_HEADS)
    grp = 2 * GDN_CHUNK
    groups = [slice(p * grp, (p + 1) * grp) for p in range(tb // grp)]

    def grp_dot(a, b):
        return jnp.concatenate([jnp.dot(a[g], b[g], preferred_element_type=F32) for g in groups], axis=0)

    def grp_dot_nt(a, b):
        return jnp.concatenate([lax.dot_general(a[g], b[g], NT_DIMS, preferred_element_type=F32)
                                for g in groups], axis=0)

    bias = bias_ref[...]
    offdiag = offd_ref[...]
    eye_b = eye_ref[...]
    qs, ks, gammas, pbs, rhss, qgs, khs = [], [], [], [], [], [], []
    for h in heads:
        q = c_scr[:, h * GDN_DK:(h + 1) * GDN_DK]
        k = c_scr[:, hk + h * GDN_DK:hk + (h + 1) * GDN_DK]
        v = c_scr[:, 2 * hk + h * GDN_DV:2 * hk + (h + 1) * GDN_DV]
        q = q * lax.rsqrt(jnp.sum(q * q, axis=-1, keepdims=True) + EPS) * (GDN_DK ** -0.5)
        k = k * lax.rsqrt(jnp.sum(k * k, axis=-1, keepdims=True) + EPS)
        beta = beta_c[:, h:h + 1]
        gcc = gc_c[:, 8 + h:9 + h]
        gtc = gt_c[:, 8 + h:9 + h]
        gcr = gc_r[8 + h:9 + h, :]
        dg = jnp.concatenate([gcc[g] - gcr[:, g] for g in groups], axis=0)
        gamma = jnp.exp(dg + bias)
        kbeta = k * beta
        kb = k.astype(BF16)
        pbs.append((grp_dot_nt(kbeta.astype(BF16), kb) * (gamma * offdiag)).astype(BF16))
        eg = jnp.exp(gcc)
        rhss.append(jnp.concatenate([v * beta, kbeta * eg], axis=1))
        qs.append(q.astype(BF16))
        ks.append(kb)
        gammas.append(gamma)
        qgs.append(q * eg)
        khs.append(k * jnp.exp(gtc - gcc))
        bg_mxu(1)
        bg_vpu(1)
    bg_mxu.drain()
    bg_vpu(4)

    def solve(_, carry):
        pb = pbs
        tbs = [pb[h] + eye_b for h in heads]
        for lvl in range(5):
            pb = [grp_dot(pb[h], pb[h]).astype(BF16) for h in heads]
            bg_vpu(1)
            tnew = [grp_dot(tbs[h], pb[h] + eye_b) for h in heads]
            bg_vpu(1)
            tbs = [x.astype(BF16) for x in tnew]

        us, ws, qks = [], [], []
        for h in heads:
            uw = rhss[h] + grp_dot(tbs[h] - eye_b, rhss[h].astype(BF16))
            us.append(uw[:, :GDN_DV])
            ws.append(uw[:, GDN_DV:])
            qks.append((grp_dot_nt(qs[h], ks[h]) * gammas[h]).astype(BF16))
        bg_vpu(2)

        s = [s_scr[h] for h in heads]
        vn_parts = [[] for _ in heads]
        qs_parts = [[] for _ in heads]
        for c in range(nchunk):
            rows = slice(c * GDN_CHUNK, (c + 1) * GDN_CHUNK)
            for h in heads:
                wq = _mm(jnp.concatenate([ws[h][rows], qgs[h][rows]], axis=0), s[h])
                vn = us[h][rows] - wq[:GDN_CHUNK]
                qs_parts[h].append(wq[GDN_CHUNK:])
                vn_parts[h].append(vn)
                decay = jnp.exp(gt_l[8 + h:9 + h, c * 128:(c + 1) * 128])
                s[h] = decay * s[h] + _mm_tn(khs[h][rows], vn)
            bg_vpu(1)
        bg_vpu.drain()
        for h in heads:
            hs = slice(h * GDN_DV, (h + 1) * GDN_DV)
            s_scr[h] = s[h]
            vn = jnp.concatenate(vn_parts[h], axis=0).astype(BF16)
            o = jnp.concatenate(qs_parts[h], axis=0) + grp_dot(qks[h], vn)
            o_ref[0, :, hs] = (_rms(o, ng_ref[...]) * _silu(zz[:, hs])).astype(BF16)
        return carry

    lax.fori_loop(0, jnp.minimum(t + 1, 1), solve, 0)

    @pl.when(t == pl.num_programs(1) - 1)
    def _():
        s_out_ref[0] = s_scr[...]


def _gdn_prompt(x3, mix_g, w_rg, w_ba, w_bat, conv_w, alog_r, dt_r, alog_c, dt_c, norm_g, tb=256):
    b, t, _ = x3.shape
    nt = t // tb
    assert t % tb == 0 and nt >= 2, "a sequence's last block must be prepared during one of its own steps"
    lbd, ubd, obd, csel, bias, offdiag, eye = _gdn_tables(tb)
    vw = GDN_HEADS * GDN_DV
    c2 = lambda i, j: (0, 0)
    single = pl.Buffered(1)
    return pl.pallas_call(
        functools.partial(_gdn_body, tb),
        grid=(b, nt),
        in_specs=[
            pl.BlockSpec((1, tb, D_MODEL), lambda i, j: (0, 0, 0)),
            pl.BlockSpec((1, tb, D_MODEL), _next_block(nt, b * nt)),
            pl.BlockSpec((1, D_MODEL), c2),
            pl.BlockSpec((D_MODEL, CONV_CH), lambda i, j: (0, COL_GDN // CONV_CH), pipeline_mode=single),
            pl.BlockSpec((D_MODEL, vw), lambda i, j: (0, (COL_GDN + CONV_CH) // vw), pipeline_mode=single),
            pl.BlockSpec((D_MODEL, BA_PAD), c2),
            pl.BlockSpec((BA_PAD, D_MODEL), c2),
            pl.BlockSpec((CONV_W, CONV_CH), c2),
            pl.BlockSpec((1, BA_PAD), c2),
            pl.BlockSpec((1, BA_PAD), c2),
            pl.BlockSpec((BA_PAD, 1), c2),
            pl.BlockSpec((BA_PAD, 1), c2),
            pl.BlockSpec((1, GDN_DV), c2),
            pl.BlockSpec((tb, tb), c2),
            pl.BlockSpec((tb, tb), c2),
            pl.BlockSpec((tb, tb), c2),
            pl.BlockSpec((tb, (tb // GDN_CHUNK) * 128), c2),
            pl.BlockSpec((tb, 2 * GDN_CHUNK), c2),
            pl.BlockSpec((tb, 2 * GDN_CHUNK), c2),
            pl.BlockSpec((tb, 2 * GDN_CHUNK), c2),
        ],
        out_specs=[
            pl.BlockSpec((1, tb, vw), lambda i, j: (i, j, 0)),
            pl.BlockSpec((1, GDN_HEADS, GDN_DK, GDN_DV), lambda i, j: (i, 0, 0, 0)),
            pl.BlockSpec((1, CONV_W - 1, CONV_CH), lambda i, j: (i, 0, 0)),
        ],
        out_shape=[
            jax.ShapeDtypeStruct((b, t, vw), BF16),
            jax.ShapeDtypeStruct((b, GDN_HEADS, GDN_DK, GDN_DV), F32),
            jax.ShapeDtypeStruct((b, CONV_W - 1, CONV_CH), F32),
        ],
        scratch_shapes=[pltpu.VMEM((GDN_HEADS, GDN_DK, GDN_DV), F32), pltpu.VMEM((tb + 8, CONV_CH), F32)]
        + 2 * [pltpu.VMEM((tb, CONV_CH), F32), pltpu.VMEM((tb, vw), F32),
               pltpu.VMEM((tb, BA_PAD), F32), pltpu.VMEM((BA_PAD, tb), F32)],
        compiler_params=_params("arbitrary", "arbitrary"),
        name="gdn_prompt",
    )(x3, x3, mix_g, w_rg, w_rg, w_ba, w_bat, conv_w, alog_r, dt_r, alog_c, dt_c, norm_g, lbd, ubd, obd, csel,
      bias, offdiag, eye)


def _block_diag_rows(x, nblk):
    row = lax.broadcasted_iota(jnp.int32, (8, nblk * 128), 0)
    blk = lax.broadcasted_iota(jnp.int32, (8, nblk * 128), 1) // 128
    return jnp.where(row == blk, jnp.concatenate([x] * nblk, axis=1), 0.0)


def _sample_step_body(cdec, qk_ref, rv_ref, rg_ref, x_ref, z_ref, bcol_ref, sr_ref, sg_ref, sc_ref, cos_ref, sin_ref,
                      cw_ref, alog_ref, dt_ref, gn_ref, ng_ref,
                      oa_ref, ob_ref, sr_out_ref, sg_out_ref, sc_out_ref):
    nh = GDN_HEADS
    cos, sin = cos_ref[...], sin_ref[...]
    gam = jnp.where(lax.broadcasted_iota(jnp.int32, (8, 1), 0) == 0, cdec[0], 0.0)
    for h in range(1, RET_HEADS):
        gam = jnp.where(lax.broadcasted_iota(jnp.int32, (8, 1), 0) == h, cdec[h], gam)
    zero4 = jnp.zeros((RET_HEADS, RET_DV), F32)
    for g in range(x_ref.shape[0]):
        x_new = x_ref[g]
        buf = sc_ref[g]
        conv = x_new * cw_ref[CONV_W - 1]
        for i in range(CONV_W - 1):
            conv = conv + buf[i] * cw_ref[i]
        conv = _silu(conv)
        sc_out_ref[g, 0] = buf[1]
        sc_out_ref[g, 1] = buf[2]
        sc_out_ref[g, 2] = x_new
        q, k, v = conv[0:nh], conv[nh:2 * nh], conv[2 * nh:3 * nh]
        q = q * lax.rsqrt(jnp.sum(q * q, axis=-1, keepdims=True) + EPS) * (GDN_DK ** -0.5)
        k = k * lax.rsqrt(jnp.sum(k * k, axis=-1, keepdims=True) + EPS)
        col = bcol_ref[g]
        beta = jax.nn.sigmoid(col[0:nh])
        eg = jnp.exp(-jnp.exp(alog_ref[...]) * _softplus(col[nh:2 * nh] + dt_ref[...]))
        kbd, qbd = _block_diag_rows(k, nh), _block_diag_rows(q, nh)
        s_flat = sg_ref[g].reshape(nh * GDN_DK, GDN_DV)
        kq_s = _mm(jnp.concatenate([kbd, qbd], axis=0), s_flat)
        vn = beta * (v - eg * kq_s[0:nh])
        o = eg * kq_s[nh:2 * nh] + jnp.sum(q * k, axis=-1, keepdims=True) * vn
        ob_ref[g] = _rms(o, ng_ref[...]) * _silu(z_ref[g])
        upd = _mm_tn(kbd, vn)
        eg_l = jnp.broadcast_to(eg, (nh, GDN_DV))
        for h in range(nh):
            sg_out_ref[g, h] = eg_l[h:h + 1] * sg_ref[g, h] + upd[h * GDN_DK:(h + 1) * GDN_DK]

        qk = _rot(qk_ref[g], cos, sin)
        rq = jnp.concatenate([qk[0:RET_HEADS], qk[0:RET_HEADS]], axis=0)
        rk = jnp.concatenate([qk[RET_HEADS:], qk[RET_HEADS:]], axis=0) * (RET_DK ** -0.5)
        rv = jnp.concatenate([rv_ref[g], zero4], axis=0)
        qbd_r, kbd_r = _block_diag_rows(rq, RET_HEADS), _block_diag_rows(rk, RET_HEADS)
        sr_flat = sr_ref[g].reshape(RET_HEADS * RET_DK, RET_DV)
        q_s = _mm(qbd_r, sr_flat)
        o_r = gam * q_s + jnp.sum(rq * rk, axis=-1, keepdims=True) * rv
        oa_ref[g] = _group_norm_gate(o_r[0:RET_HEADS], rg_ref[g], gn_ref[...])
        upd_r = _mm_tn(kbd_r, rv)
        for h in range(RET_HEADS):
            sr_out_ref[g, h] = cdec[h] * sr_ref[g, h] + upd_r[h * RET_DK:(h + 1) * RET_DK]


def _sample_step(proj_ret, proj_gdn, ba, state_ret, state_gdn, state_conv, cos, sin, conv_w, a_log, dt_bias, gn,
                 norm_g, gs=8):
    ns = proj_ret.shape[0]
    nh, rh = GDN_HEADS, RET_HEADS
    nq = rh * RET_DK
    qk3 = proj_ret[:, :2 * nq].reshape(ns, 2 * rh, RET_DK)
    rv3 = proj_ret[:, 2 * nq:2 * nq + rh * RET_DV].reshape(ns, rh, RET_DV)
    rg3 = proj_ret[:, 2 * nq + rh * RET_DV:].reshape(ns, rh, RET_DV)
    x3 = proj_gdn[:, :CONV_CH].reshape(ns, 3 * nh, GDN_DK)
    z3 = proj_gdn[:, CONV_CH:].reshape(ns, nh, GDN_DV)
    bcol = ba[:, :2 * nh].reshape(ns, 2 * nh, 1)
    sc4 = state_conv.reshape(ns, CONV_W - 1, 3 * nh, GDN_DK)
    cw3 = conv_w.reshape(CONV_W, 3 * nh, GDN_DK)
    cdec = [float(v) for v in np.exp(_RET_LOG_G)]
    c2 = lambda i: (0, 0)
    c3 = lambda i: (0, 0, 0)
    b3 = lambda i: (i, 0, 0)
    b4 = lambda i: (i, 0, 0, 0)
    oa, ob, sr, sg, sc = pl.pallas_call(
        functools.partial(_sample_step_body, cdec),
        grid=(ns // gs,),
        in_specs=[
            pl.BlockSpec((gs, 2 * rh, RET_DK), b3),
            pl.BlockSpec((gs, rh, RET_DV), b3),
            pl.BlockSpec((gs, rh, RET_DV), b3),
            pl.BlockSpec((gs, 3 * nh, GDN_DK), b3),
            pl.BlockSpec((gs, nh, GDN_DV), b3),
            pl.BlockSpec((gs, 2 * nh, 1), b3),
            pl.BlockSpec((gs, rh, RET_DK, RET_DV), b4),
            pl.BlockSpec((gs, nh, GDN_DK, GDN_DV), b4),
            pl.BlockSpec((gs, CONV_W - 1, 3 * nh, GDN_DK), b4),
            pl.BlockSpec((1, RET_DK), c2),
            pl.BlockSpec((1, RET_DK), c2),
            pl.BlockSpec((CONV_W, 3 * nh, GDN_DK), c3),
            pl.BlockSpec((nh, 1), c2),
            pl.BlockSpec((nh, 1), c2),
            pl.BlockSpec((rh, RET_DV), c2),
            pl.BlockSpec((1, GDN_DV), c2),
        ],
        out_specs=[
            pl.BlockSpec((gs, rh, RET_DV), b3),
            pl.BlockSpec((gs, nh, GDN_DV), b3),
            pl.BlockSpec((gs, rh, RET_DK, RET_DV), b4),
            pl.BlockSpec((gs, nh, GDN_DK, GDN_DV), b4),
            pl.BlockSpec((gs, CONV_W - 1, 3 * nh, GDN_DK), b4),
        ],
        out_shape=[
            jax.ShapeDtypeStruct((ns, rh, RET_DV), F32),
            jax.ShapeDtypeStruct((ns, nh, GDN_DV), F32),
            jax.ShapeDtypeStruct(state_ret.shape, F32),
            jax.ShapeDtypeStruct(state_gdn.shape, F32),
            jax.ShapeDtypeStruct(sc4.shape, F32),
        ],
        compiler_params=_params("parallel"),
        name="sample_step",
    )(qk3, rv3, rg3, x3, z3, bcol, state_ret, state_gdn, sc4, cos, sin, cw3, a_log.reshape(nh, 1),
      dt_bias.reshape(nh, 1), gn.reshape(rh, RET_DV), norm_g)
    return (oa.reshape(ns, rh * RET_DV).astype(BF16), ob.reshape(ns, nh * GDN_DV).astype(BF16), sr, sg,
            sc.reshape(state_conv.shape))


def _merge_body(x_ref, nmg_ref, wgt_ref, oa_ref, ob_ref, wa_ref, wb_ref, wo_ref, ng_ref, wq_ref, x1_ref, q_ref):
    x = x_ref[...]
    gates = jnp.dot(_rms(x, nmg_ref[...]).astype(BF16), wgt_ref[...], preferred_element_type=F32)
    ya = jnp.dot(oa_ref[...], wa_ref[...], preferred_element_type=F32)
    yb = jnp.dot(ob_ref[...], wb_ref[...], preferred_element_type=F32)
    merged = jax.nn.sigmoid(gates[:, :D_MODEL]) * ya + jax.nn.sigmoid(gates[:, D_MODEL:]) * yb
    x1 = x + _mm(merged, wo_ref[...])
    x1_ref[...] = x1
    q_ref[...] = _mm(_rms(x1, ng_ref[...]), wq_ref[...]).astype(BF16)


def _merge(x, mix_g, w_gates, oa, ob, wa, wb, wo, ng, wq, tm):
    m = x.shape[0]
    row = lambda i: (i, 0)
    c2 = lambda i: (0, 0)
    wspec = pl.BlockSpec((D_MODEL, D_MODEL), c2)
    return pl.pallas_call(
        _merge_body,
        grid=(m // tm,),
        in_specs=[
            pl.BlockSpec((tm, D_MODEL), row),
            pl.BlockSpec((1, D_MODEL), c2),
            pl.BlockSpec((D_MODEL, 2 * D_MODEL), c2),
            pl.BlockSpec((tm, D_MODEL), row),
            pl.BlockSpec((tm, D_MODEL), row),
            wspec, wspec, wspec,
            pl.BlockSpec((1, D_MODEL), c2),
            wspec,
        ],
        out_specs=[pl.BlockSpec((tm, D_MODEL), row), pl.BlockSpec((tm, D_MODEL), row)],
        out_shape=[jax.ShapeDtypeStruct((m, D_MODEL), F32), jax.ShapeDtypeStruct((m, D_MODEL), BF16)],
        compiler_params=_params("parallel"),
        name="merge",
    )(x, mix_g, w_gates, oa, ob, wa, wb, wo, ng, wq)


def _memkv_body(m_ref, g_ref, wk_ref, wv_ref, k_ref, v_ref, k4_ref, v4_ref):
    mn = _rms(m_ref[...], g_ref[...]).astype(BF16)
    k = jnp.dot(mn, wk_ref[...], preferred_element_type=F32)
    v = jnp.dot(mn, wv_ref[...], preferred_element_type=F32)
    k_ref[...] = k
    v_ref[...] = v
    for h in range(X_HEADS):
        k4_ref[:, h, :] = k[:, h * X_HD:(h + 1) * X_HD]
        v4_ref[:, h, :] = v[:, h * X_HD:(h + 1) * X_HD]


def _memkv(mem, g, wk, wv, tm=512):
    m = mem.shape[0]
    row = lambda i: (i, 0)
    row3 = lambda i: (i, 0, 0)
    c2 = lambda i: (0, 0)
    return pl.pallas_call(
        _memkv_body,
        grid=(m // tm,),
        in_specs=[pl.BlockSpec((tm, D_MODEL), row), pl.BlockSpec((1, D_MODEL), c2),
                  pl.BlockSpec((D_MODEL, D_MODEL), c2), pl.BlockSpec((D_MODEL, D_MODEL), c2)],
        out_specs=[pl.BlockSpec((tm, D_MODEL), row), pl.BlockSpec((tm, D_MODEL), row),
                   pl.BlockSpec((tm, X_HEADS, X_HD), row3), pl.BlockSpec((tm, X_HEADS, X_HD), row3)],
        out_shape=[jax.ShapeDtypeStruct((m, D_MODEL), F32)] * 2
        + [jax.ShapeDtypeStruct((m, X_HEADS, X_HD), F32)] * 2,
        compiler_params=_params("parallel"),
        name="memkv",
    )(mem, g, wk, wv)


def _xattn_body(q_ref, mk_ref, mv_ref, x1_ref, wo_ref, x2_ref):
    parts = []
    for h in range(X_HEADS):
        hs = slice(h * X_HD, (h + 1) * X_HD)
        s = _mm_nt(q_ref[0, :, hs], mk_ref[0, :, hs]) * (X_HD ** -0.5)
        p = jnp.exp(s - jnp.max(s, axis=-1, keepdims=True))
        o = _mm(p, mv_ref[0, :, hs]) / jnp.sum(p, axis=-1, keepdims=True)
        parts.append(o.astype(BF16))
    x2_ref[0] = x1_ref[0] + jnp.dot(jnp.concatenate(parts, axis=1), wo_ref[...], preferred_element_type=F32)


def _xattn_prompt(q3, mk3, mv3, x13, wo, tq=512):
    b, t, _ = q3.shape
    tok = lambda i, j: (i, j, 0)
    mem = lambda i, j: (i, 0, 0)
    return pl.pallas_call(
        _xattn_body,
        grid=(b, t // tq),
        in_specs=[pl.BlockSpec((1, tq, D_MODEL), tok), pl.BlockSpec((1, N_MEM, D_MODEL), mem),
                  pl.BlockSpec((1, N_MEM, D_MODEL), mem), pl.BlockSpec((1, tq, D_MODEL), tok),
                  pl.BlockSpec((D_MODEL, D_MODEL), lambda i, j: (0, 0))],
        out_specs=pl.BlockSpec((1, tq, D_MODEL), tok),
        out_shape=jax.ShapeDtypeStruct((b, t, D_MODEL), F32),
        compiler_params=_params("parallel", "parallel"),
        name="xattn_prompt",
    )(q3, mk3, mv3, x13, wo)


def _xattn_sample_body(q_ref, mk_ref, mv_ref, o_ref):
    for g in range(q_ref.shape[0]):
        q = q_ref[g]
        s = jnp.sum(mk_ref[g] * q[None], axis=-1, keepdims=True) * (X_HD ** -0.5)
        p = jnp.exp(s - jnp.max(s, axis=0, keepdims=True))
        o_ref[g] = jnp.sum(p * mv_ref[g], axis=0) / jnp.sum(p, axis=0)


def _xattn_sample(q, mk4, mv4, gs=4):
    ns = q.shape[0]
    q3 = q.astype(F32).reshape(ns, X_HEADS, X_HD)
    row = lambda i: (i, 0, 0)
    mem = lambda i: (i, 0, 0, 0)
    return pl.pallas_call(
        _xattn_sample_body,
        grid=(ns // gs,),
        in_specs=[pl.BlockSpec((gs, X_HEADS, X_HD), row), pl.BlockSpec((gs, N_MEM, X_HEADS, X_HD), mem),
                  pl.BlockSpec((gs, N_MEM, X_HEADS, X_HD), mem)],
        out_specs=pl.BlockSpec((gs, X_HEADS, X_HD), row),
        out_shape=jax.ShapeDtypeStruct((ns, X_HEADS, X_HD), F32),
        compiler_params=_params("parallel"),
        name="xattn_sample",
    )(q3, mk4, mv4)


def _resid_mm_body(x_ref, a_ref, w_ref, o_ref):
    o_ref[...] = x_ref[...] + jnp.dot(a_ref[...], w_ref[...], preferred_element_type=F32)


def _resid_mm(x, a, w):
    m = x.shape[0]
    return pl.pallas_call(
        _resid_mm_body,
        out_shape=jax.ShapeDtypeStruct((m, D_MODEL), F32),
        compiler_params=pltpu.CompilerParams(vmem_limit_bytes=VMEM_LIMIT),
        name="resid_mm",
    )(x, a, w)


def _ffn_body(x_ref, ng_ref, wg_ref, wu_ref, wd_ref, nf_ref, y_ref):
    x = x_ref[...]
    h = _rms(x, ng_ref[...]).astype(BF16)
    gate = jnp.dot(h, wg_ref[...], preferred_element_type=F32)
    up = jnp.dot(h, wu_ref[...], preferred_element_type=F32)
    x3 = x + _mm(_silu(gate) * up, wd_ref[...])
    y_ref[...] = _rms(x3, nf_ref[...])


def _ffn(x, ng, wg, wu, wd, nf, tm):
    m = x.shape[0]
    dff = wg.shape[1]
    row = lambda i: (i, 0)
    c2 = lambda i: (0, 0)
    single = pl.Buffered(1)
    return pl.pallas_call(
        _ffn_body,
        grid=(m // tm,),
        in_specs=[pl.BlockSpec((tm, D_MODEL), row), pl.BlockSpec((1, D_MODEL), c2),
                  pl.BlockSpec((D_MODEL, dff), c2, pipeline_mode=single),
                  pl.BlockSpec((D_MODEL, dff), c2, pipeline_mode=single),
                  pl.BlockSpec((dff, D_MODEL), c2, pipeline_mode=single),
                  pl.BlockSpec((1, D_MODEL), c2)],
        out_specs=pl.BlockSpec((tm, D_MODEL), row),
        out_shape=jax.ShapeDtypeStruct((m, D_MODEL), F32),
        compiler_params=_params("parallel"),
        name="ffn",
    )(x, ng, wg, wu, wd, nf)


def _rope_tables(pos):
    half = RET_DK // 2
    inv = ROPE_BASE ** (-jnp.arange(half, dtype=F32) / half)
    ang = pos.astype(F32)[:, None] * inv[None, :]
    cos, sin = jnp.cos(ang), jnp.sin(ang)
    return jnp.concatenate([cos, cos], axis=-1), jnp.concatenate([-sin, sin], axis=-1)


def _pad_lanes(v, offset):
    return jnp.zeros((BA_PAD,), F32).at[offset:offset + v.shape[0]].set(v)


def kernel(x_prompt, x_sample, state_ret, state_gdn, state_conv, cache_mem_k, cache_mem_v, mem_prompt,
           norm_mix_g, w_in, ret_gn_g, w_branch_a, gdn_conv_w, gdn_a_log, gdn_dt_bias, gdn_norm_g,
           w_branch_b, w_out, norm_x_g, mem_norm_g, w_xq, w_xk, w_xv, w_xo, norm_ffn_g, w_gate, w_up,
           w_down, norm_final_g):
    depth = w_in.shape[0]
    assert depth == 1, "single-layer kernel"
    b, t, _ = x_prompt.shape
    ns = x_sample.shape[0]
    l = 0

    w = w_in[l]
    ba0, g0 = COL_GATE, COL_GATE + 2 * GDN_HEADS
    w_bf = w.astype(BF16)
    w_gates = w_bf[:, g0:]
    w_ba = jnp.pad(w[:, ba0:g0], ((0, 0), (0, BA_PAD - 2 * GDN_HEADS))).astype(BF16)
    w_bat = w_ba.T
    row = lambda v: v.reshape(1, -1)
    wa, wb, wo = w_branch_a[l].astype(BF16), w_branch_b[l].astype(BF16), w_out[l].astype(BF16)
    wq, wk, wv, wxo = w_xq[l].astype(BF16), w_xk[l].astype(BF16), w_xv[l].astype(BF16), w_xo[l].astype(BF16)
    wg, wu, wd = w_gate[l].astype(BF16), w_up[l].astype(BF16), w_down[l].astype(BF16)
    alog_r = _pad_lanes(gdn_a_log[l], GDN_HEADS).reshape(1, BA_PAD)
    dt_r = _pad_lanes(gdn_dt_bias[l], GDN_HEADS).reshape(1, BA_PAD)
    alog_c, dt_c = alog_r.reshape(BA_PAD, 1), dt_r.reshape(BA_PAD, 1)
    cos_p, sin_p = _rope_tables(jnp.arange(t))
    cos_s, sin_s = _rope_tables(PAST_LEN + jnp.arange(1))

    mix_g = row(norm_mix_g[l])
    xp = x_prompt.reshape(b * t, D_MODEL)
    oa_p, sr_p = _retention_prompt(x_prompt, mix_g, w_bf, cos_p, sin_p, row(ret_gn_g[l]))
    ob_p, sg_p, sc_p = _gdn_prompt(x_prompt, mix_g, w_bf, w_ba, w_bat, gdn_conv_w[l], alog_r, dt_r,
                                   alog_c, dt_c, row(gdn_norm_g[l]))
    x1_p, q_p = _merge(xp, mix_g, w_gates, oa_p.reshape(b * t, -1), ob_p.reshape(b * t, -1), wa, wb, wo,
                       row(norm_x_g[l]), wq, tm=512)
    mk_p, mv_p, mk4_p, mv4_p = _memkv(mem_prompt.reshape(b * N_MEM, D_MODEL), row(mem_norm_g[l]), wk, wv)
    x2_p = _xattn_prompt(q_p.reshape(b, t, D_MODEL), mk_p.reshape(b, N_MEM, D_MODEL),
                         mv_p.reshape(b, N_MEM, D_MODEL), x1_p.reshape(b, t, D_MODEL), wxo)
    y_p = _ffn(x2_p.reshape(b * t, D_MODEL), row(norm_ffn_g[l]), wg, wu, wd, row(norm_final_g), tm=512)

    xs = x_sample.reshape(ns, D_MODEL)
    proj_s, ba_s = _inproj(xs, mix_g, w_bf, COL_GATE, w_ba)
    oa_s, ob_s, sr_s, sg_s, sc_s = _sample_step(proj_s[:, :COL_GDN], proj_s[:, COL_GDN:], ba_s, state_ret[l],
                                                state_gdn[l], state_conv[l],
                                                cos_s, sin_s, gdn_conv_w[l], gdn_a_log[l], gdn_dt_bias[l],
                                                ret_gn_g[l], row(gdn_norm_g[l]))
    x1_s, q_s = _merge(xs, mix_g, w_gates, oa_s.reshape(ns, -1), ob_s.reshape(ns, -1), wa, wb, wo,
                       row(norm_x_g[l]), wq, tm=ns)
    o_s = _xattn_sample(q_s, cache_mem_k[l], cache_mem_v[l])
    x2_s = _resid_mm(x1_s, o_s.reshape(ns, D_MODEL).astype(BF16), wxo)
    y_s = _ffn(x2_s, row(norm_ffn_g[l]), wg, wu, wd, row(norm_final_g), tm=ns)

    return (y_p.reshape(b, t, D_MODEL), y_s.reshape(ns, 1, D_MODEL),
            sr_p[None], sg_p[None], sc_p[None],
            mk4_p.reshape(1, b, N_MEM, X_HEADS, X_HD), mv4_p.reshape(1, b, N_MEM, X_HEADS, X_HD),
            sr_s[None], sg_s[None], sc_s[None])
```

```python
import functools

import numpy as np
import jax
import jax.numpy as jnp
from jax import lax
from jax.experimental import pallas as pl
from jax.experimental.pallas import tpu as pltpu

F32 = jnp.float32
BF16 = jnp.bfloat16

D_MODEL = 1024
RET_HEADS, RET_DK, RET_DV = 4, 128, 256
GDN_HEADS, GDN_DK, GDN_DV = 8, 128, 128
CONV_W = 4
CONV_CH = 3 * GDN_HEADS * GDN_DK
N_MEM, X_HEADS, X_HD = 256, 4, 256
PAST_LEN = 16384
ROPE_BASE = 10000.0
EPS = 1e-6
GDN_CHUNK = 64

COL_RET = 0
COL_GDN = 3072
COL_GATE = 7168
BA_PAD = 128

VMEM_LIMIT = 56 * 1024 * 1024

NT_DIMS = (((1,), (1,)), ((), ()))
TN_DIMS = (((0,), (0,)), ((), ()))


def _mm(a, b):
    return jnp.dot(a.astype(BF16), b.astype(BF16), preferred_element_type=F32)


def _mm_nt(a, b):
    return lax.dot_general(a.astype(BF16), b.astype(BF16), NT_DIMS, preferred_element_type=F32)


def _mm_tn(a, b):
    return lax.dot_general(a.astype(BF16), b.astype(BF16), TN_DIMS, preferred_element_type=F32)


def _split3(x):
    hi = x.astype(BF16)
    r = x - hi.astype(F32)
    mid = r.astype(BF16)
    return hi, mid, (r - mid.astype(F32)).astype(BF16)


def _mm_sel(sel, x):
    return sum(jnp.dot(sel, p, preferred_element_type=F32) for p in _split3(x))


def _mm_sel_r(x, sel):
    return sum(jnp.dot(p, sel, preferred_element_type=F32) for p in _split3(x))


def _rms(x, g):
    return x * lax.rsqrt(jnp.mean(x * x, axis=-1, keepdims=True) + EPS) * g


def _silu(x):
    h = 0.5 * x
    return h + h * jnp.tanh(h)


def _softplus(x):
    return jnp.maximum(x, 0.0) + jnp.log1p(jnp.exp(-jnp.abs(x)))


def _params(*sem):
    return pltpu.CompilerParams(dimension_semantics=sem, vmem_limit_bytes=VMEM_LIMIT)


def _inproj_body(x_ref, g_ref, w_ref, wba_ref, o_ref, oba_ref, h_scr):
    @pl.when(pl.program_id(0) == 0)
    def _():
        hb = _rms(x_ref[...], g_ref[...]).astype(BF16)
        h_scr[...] = hb
        oba_ref[...] = jnp.dot(hb, wba_ref[...], preferred_element_type=F32)

    o_ref[...] = jnp.dot(h_scr[...], w_ref[...], preferred_element_type=F32)


def _inproj(x, g, w_bf, n, w_ba, tn=1024):
    m = x.shape[0]
    assert n % tn == 0 and n <= w_bf.shape[1]
    return pl.pallas_call(
        _inproj_body,
        grid=(n // tn,),
        in_specs=[
            pl.BlockSpec((m, D_MODEL), lambda j: (0, 0)),
            pl.BlockSpec((1, D_MODEL), lambda j: (0, 0)),
            pl.BlockSpec((D_MODEL, tn), lambda j: (0, j)),
            pl.BlockSpec((D_MODEL, BA_PAD), lambda j: (0, 0)),
        ],
        out_specs=[
            pl.BlockSpec((m, tn), lambda j: (0, j)),
            pl.BlockSpec((m, BA_PAD), lambda j: (0, 0)),
        ],
        out_shape=[
            jax.ShapeDtypeStruct((m, n), F32),
            jax.ShapeDtypeStruct((m, BA_PAD), F32),
        ],
        scratch_shapes=[pltpu.VMEM((m, D_MODEL), BF16)],
        compiler_params=_params("arbitrary"),
        name="inproj",
    )(x, g, w_bf, w_ba)


_RET_LOG_G = np.log1p(-np.exp2(-5.0 - np.arange(RET_HEADS, dtype=np.float64)))


def _ret_tables(c):
    idx = np.arange(c, dtype=np.float64)
    diff = idx[:, None] - idx[None, :]
    dmat = np.where(diff >= 0, np.exp(np.maximum(diff, 0.0)[None] * _RET_LOG_G[:, None, None]), 0.0)
    qdec = np.exp((idx + 1.0)[None, :] * _RET_LOG_G[:, None])
    kdec = np.exp((c - 1.0 - idx)[None, :] * _RET_LOG_G[:, None])
    lane = np.ones((1, 1, RET_DK))
    return (jnp.asarray(dmat, F32), jnp.asarray(qdec[:, :, None] * lane, F32),
            jnp.asarray(kdec[:, :, None] * lane, F32), [float(v) for v in np.exp(c * _RET_LOG_G)])


def _rot(x, cos, sin):
    return x * cos + pltpu.roll(x, RET_DK // 2, 1) * sin


def _group_norm_gate(o, gate, gn):
    mu = jnp.mean(o, axis=-1, keepdims=True)
    d = o - mu
    var = jnp.mean(d * d, axis=-1, keepdims=True)
    return _silu(gate) * (d * lax.rsqrt(var + EPS) * gn)


def _proj_tiles(hb, w_ref, dst, width=512):
    for c0 in range(0, w_ref.shape[1], width):
        dst[:, c0:c0 + width] = jnp.dot(hb, w_ref[:, c0:c0 + width], preferred_element_type=F32)


def _next_block(nt, nblocks):
    def index_map(i, j):
        n1 = jnp.minimum(i * nt + j + 1, nblocks - 1)
        return (n1 // nt, n1 % nt, 0)
    return index_map


def _ret_body(cdec, x0_ref, xn_ref, nmg_ref, w_ref, *rest):
    *consts, o_ref, s_out_ref, s_scr, pa, pb = rest
    t = pl.program_id(1)
    n = pl.program_id(0) * pl.num_programs(1) + t
    bufs = (pa, pb)

    @pl.when(t == 0)
    def _():
        s_scr[...] = jnp.zeros_like(s_scr)

    @pl.when(n == 0)
    def _():
        _proj_tiles(_rms(x0_ref[0], nmg_ref[...]).astype(BF16), w_ref, pa)

    for slot in range(2):
        @pl.when(n % 2 == slot)
        def _(slot=slot):
            _ret_step(cdec, t, xn_ref, nmg_ref, w_ref, bufs[slot], bufs[1 - slot], *consts, o_ref, s_out_ref, s_scr)


def _ret_step(cdec, t, xn_ref, nmg_ref, w_ref, proj, proj_next, cos_ref, sin_ref, dmat_ref, qdec_ref, kdec_ref,
              gn_ref, o_ref, s_out_ref, s_scr):
    qw = RET_HEADS * RET_DK
    _proj_tiles(_rms(xn_ref[0], nmg_ref[...]).astype(BF16), w_ref, proj_next)

    chunk = dmat_ref.shape[1]
    for c0 in range(0, proj.shape[0], chunk):
        rows = slice(c0, c0 + chunk)
        cos, sin = cos_ref[rows, :], sin_ref[rows, :]
        for h in range(RET_HEADS):
            qk = slice(h * RET_DK, (h + 1) * RET_DK)
            kk = slice(qw + h * RET_DK, qw + (h + 1) * RET_DK)
            vv = slice(h * RET_DV, (h + 1) * RET_DV)
            q = _rot(proj[rows, qk], cos, sin)
            k = _rot(proj[rows, kk], cos, sin) * (RET_DK ** -0.5)
            v = proj[rows, 2 * qw + h * RET_DV:2 * qw + (h + 1) * RET_DV]
            gate = proj[rows, 2 * qw + (RET_HEADS + h) * RET_DV:2 * qw + (RET_HEADS + h + 1) * RET_DV]
            s = s_scr[h]
            scores = _mm_nt(q, k) * dmat_ref[h]
            o = _mm(scores, v) + _mm(q * qdec_ref[h], s)
            s_scr[h] = cdec[h] * s + _mm_tn(k * kdec_ref[h], v)
            o_ref[0, rows, vv] = _group_norm_gate(o, gate, gn_ref[:, vv]).astype(BF16)

    @pl.when(t == pl.num_programs(1) - 1)
    def _():
        s_out_ref[0] = s_scr[...]


def _retention_prompt(x3, mix_g, w_rg, cos, sin, gn, tb=512, chunk=256):
    b, t, _ = x3.shape
    nt = t // tb
    dmat, qdec, kdec, cdec = _ret_tables(chunk)
    vw = RET_HEADS * RET_DV
    const3 = lambda i, j: (0, 0, 0)
    return pl.pallas_call(
        functools.partial(_ret_body, cdec),
        grid=(b, nt),
        in_specs=[
            pl.BlockSpec((1, tb, D_MODEL), const3),
            pl.BlockSpec((1, tb, D_MODEL), _next_block(nt, b * nt)),
            pl.BlockSpec((1, D_MODEL), lambda i, j: (0, 0)),
            pl.BlockSpec((D_MODEL, COL_GDN), lambda i, j: (0, COL_RET // COL_GDN), pipeline_mode=pl.Buffered(1)),
            pl.BlockSpec((tb, RET_DK), lambda i, j: (j, 0)),
            pl.BlockSpec((tb, RET_DK), lambda i, j: (j, 0)),
            pl.BlockSpec((RET_HEADS, chunk, chunk), const3),
            pl.BlockSpec((RET_HEADS, chunk, RET_DK), const3),
            pl.BlockSpec((RET_HEADS, chunk, RET_DK), const3),
            pl.BlockSpec((1, vw), lambda i, j: (0, 0)),
        ],
        out_specs=[
            pl.BlockSpec((1, tb, vw), lambda i, j: (i, j, 0)),
            pl.BlockSpec((1, RET_HEADS, RET_DK, RET_DV), lambda i, j: (i, 0, 0, 0)),
        ],
        out_shape=[
            jax.ShapeDtypeStruct((b, t, vw), BF16),
            jax.ShapeDtypeStruct((b, RET_HEADS, RET_DK, RET_DV), F32),
        ],
        scratch_shapes=[pltpu.VMEM((RET_HEADS, RET_DK, RET_DV), F32)]
        + 2 * [pltpu.VMEM((tb, COL_GDN), F32)],
        compiler_params=_params("arbitrary", "arbitrary"),
        name="retention_prompt",
    )(x3, x3, mix_g, w_rg, cos, sin, dmat, qdec, kdec, gn)


def _gdn_tables(tb):
    idx = np.arange(tb)
    same = (idx[:, None] // GDN_CHUNK) == (idx[None, :] // GDN_CHUNK)
    lower = same & (idx[:, None] >= idx[None, :])
    nchunk = tb // GDN_CHUNK
    chunk_sel = np.repeat((idx[:, None] // GDN_CHUNK) == np.arange(nchunk)[None, :], 128, axis=1)
    grp = 2 * GDN_CHUNK
    il = (idx % grp)[:, None]
    jl = np.arange(grp)[None, :]
    bias = np.where((il // GDN_CHUNK == jl // GDN_CHUNK) & (il >= jl), 0.0, -1e30)
    eye = (il == jl).astype(np.float64)
    return (jnp.asarray(lower, BF16), jnp.asarray(lower.T, BF16), jnp.asarray(same, BF16),
            jnp.asarray(chunk_sel, BF16), jnp.asarray(bias, F32), jnp.asarray(eye - 1.0, F32),
            jnp.asarray(eye, BF16))


def _gdn_prepare(x_ref, first_of_seq, nmg_ref, wqkv_ref, wz_ref, wba_ref, wbat_ref, cw_ref, xr, cs_buf, zs, bas,
                 bats, tb):
    st = {}

    def start():
        xr[0:8, :] = jnp.where(first_of_seq, 0.0, xr[tb:tb + 8, :])
        st["hb"] = _rms(x_ref[0], nmg_ref[...]).astype(BF16)

    def tile(w_ref, dst, c0, width=512):
        def run():
            dst[:, c0:c0 + width] = jnp.dot(st["hb"], w_ref[:, c0:c0 + width], preferred_element_type=F32)
        return run

    def logits():
        bas[...] = jnp.dot(st["hb"], wba_ref[...], preferred_element_type=F32)
        bats[...] = lax.dot_general(wbat_ref[...], st["hb"], NT_DIMS, preferred_element_type=F32)

    def conv(c0):
        def run():
            cs = slice(c0, c0 + GDN_DK)
            acc = xr[5:5 + tb, cs] * cw_ref[0:1, cs]
            for i in range(1, CONV_W):
                acc = acc + xr[5 + i:5 + i + tb, cs] * cw_ref[i:i + 1, cs]
            cs_buf[:, cs] = _silu(acc)
        return run

    mxu_items = ([start] + [tile(wqkv_ref, xr.at[8:8 + tb], c0) for c0 in range(0, CONV_CH, 512)]
                 + [tile(wz_ref, zs, c0) for c0 in range(0, wz_ref.shape[1], 512)] + [logits])
    vpu_items = [conv(c0) for c0 in range(0, CONV_CH, GDN_DK)]
    return mxu_items, vpu_items


class _Background:
    def __init__(self, items):
        self._items = list(items)

    def __call__(self, count):
        for item in self._items[:count]:
            item()
        del self._items[:count]

    def drain(self):
        self(len(self._items))


def _gdn_body(tb, x0_ref, xn_ref, nmg_ref, wqkv_ref, wz_ref, wba_ref, wbat_ref, cw_ref, *rest):
    *consts, o_ref, s_out_ref, conv_out_ref, s_scr, xr, ca, za, baa, bata, cb, zb, bab, batb = rest
    t = pl.program_id(1)
    nt = pl.num_programs(1)
    n = pl.program_id(0) * nt + t
    weights = (nmg_ref, wqkv_ref, wz_ref, wba_ref, wbat_ref, cw_ref)
    bufs = ((ca, za, baa, bata), (cb, zb, bab, batb))

    @pl.when(t == 0)
    def _():
        s_scr[...] = jnp.zeros_like(s_scr)

    @pl.when(n == 0)
    def _():
        mxu_items, vpu_items = _gdn_prepare(x0_ref, True, *weights, xr, *bufs[0], tb)
        _Background(mxu_items + vpu_items).drain()

    for slot in range(2):
        @pl.when(n % 2 == slot)
        def _(slot=slot):
            mxu_items, vpu_items = _gdn_prepare(xn_ref, t == nt - 1, *weights, xr, *bufs[1 - slot], tb)
            _gdn_step(tb, t, bufs[slot], *consts, o_ref, s_out_ref, s_scr,
                      _Background(mxu_items), _Background(vpu_items))

    @pl.when(t == nt - 2)
    def _():
        conv_out_ref[0] = xr[tb + 5:tb + 8, :]


def _gdn_step(tb, t, cur, alog_r_ref, dt_r_ref, alog_c_ref, dt_c_ref, ng_ref,
              lbd_ref, ubd_ref, obd_ref, csel_ref, bias_ref, offd_ref, eye_ref, o_ref, s_out_ref, s_scr,
              bg_mxu, bg_vpu):
    nchunk = tb // GDN_CHUNK
    hk = GDN_HEADS * GDN_DK
    c_scr, zz, ba_ref, bat_ref = cur
    ba, bat = ba_ref[...], bat_ref[...]

    bg_mxu(2)
    beta_c = jax.nn.sigmoid(ba)
    g_c = -jnp.exp(alog_r_ref[...]) * _softplus(ba + dt_r_ref[...])
    g_r = -jnp.exp(alog_c_ref[...]) * _softplus(bat + dt_c_ref[...])
    gc_c = _mm_sel(lbd_ref[...], g_c)
    gt_c = _mm_sel(obd_ref[...], g_c)
    gc_r = _mm_sel_r(g_r, ubd_ref[...])
    gt_l = _mm_sel_r(g_r, csel_ref[...])

    heads = range(GDN_HEADS)
    grp = 2 * GDN_CHUNK
    groups = [slice(p * grp, (p + 1) * grp) for p in range(tb // grp)]

    def grp_dot(a, b):
        return jnp.concatenate([jnp.dot(a[g], b[g], preferred_element_type=F32) for g in groups], axis=0)

    def grp_dot_nt(a, b):
        return jnp.concatenate([lax.dot_general(a[g], b[g], NT_DIMS, preferred_element_type=F32)
                                for g in groups], axis=0)

    bias = bias_ref[...]
    offdiag = offd_ref[...]
    eye_b = eye_ref[...]
    qs, ks, gammas, pbs, rhss, qgs, khs = [], [], [], [], [], [], []
    for h in heads:
        q = c_scr[:, h * GDN_DK:(h + 1) * GDN_DK]
        k = c_scr[:, hk + h * GDN_DK:hk + (h + 1) * GDN_DK]
        v = c_scr[:, 2 * hk + h * GDN_DV:2 * hk + (h + 1) * GDN_DV]
        q = q * lax.rsqrt(jnp.sum(q * q, axis=-1, keepdims=True) + EPS) * (GDN_DK ** -0.5)
        k = k * lax.rsqrt(jnp.sum(k * k, axis=-1, keepdims=True) + EPS)
        beta = beta_c[:, h:h + 1]
        gcc = gc_c[:, 8 + h:9 + h]
        gtc = gt_c[:, 8 + h:9 + h]
        gcr = gc_r[8 + h:9 + h, :]
        dg = jnp.concatenate([gcc[g] - gcr[:, g] for g in groups], axis=0)
        gamma = jnp.exp(dg + bias)
        kbeta = k * beta
        kb = k.astype(BF16)
        pbs.append((grp_dot_nt(kbeta.astype(BF16), kb) * (gamma * offdiag)).astype(BF16))
        eg = jnp.exp(gcc)
        rhss.append(jnp.concatenate([v * beta, kbeta * eg], axis=1))
        qs.append(q.astype(BF16))
        ks.append(kb)
        gammas.append(gamma)
        qgs.append(q * eg)
        khs.append(k * jnp.exp(gtc - gcc))
        bg_mxu(1)
        bg_vpu(1)
    bg_mxu.drain()
    bg_vpu(4)

    def solve(_, carry):
        pb = pbs
        tbs = [pb[h] + eye_b for h in heads]
        for lvl in range(5):
            pb = [grp_dot(pb[h], pb[h]).astype(BF16) for h in heads]
            bg_vpu(1)
            tnew = [grp_dot(tbs[h], pb[h] + eye_b) for h in heads]
            bg_vpu(1)
            tbs = [x.astype(BF16) for x in tnew]

        us, ws, qks = [], [], []
        for h in heads:
            uw = rhss[h] + grp_dot(tbs[h] - eye_b, rhss[h].astype(BF16))
            us.append(uw[:, :GDN_DV])
            ws.append(uw[:, GDN_DV:])
            qks.append((grp_dot_nt(qs[h], ks[h]) * gammas[h]).astype(BF16))
        bg_vpu(2)

        s = [s_scr[h] for h in heads]
        vn_parts = [[] for _ in heads]
        qs_parts = [[] for _ in heads]
        for c in range(nchunk):
            rows = slice(c * GDN_CHUNK, (c + 1) * GDN_CHUNK)
            for h in heads:
                wq = _mm(jnp.concatenate([ws[h][rows], qgs[h][rows]], axis=0), s[h])
                vn = us[h][rows] - wq[:GDN_CHUNK]
                qs_parts[h].append(wq[GDN_CHUNK:])
                vn_parts[h].append(vn)
                decay = jnp.exp(gt_l[8 + h:9 + h, c * 128:(c + 1) * 128])
                s[h] = decay * s[h] + _mm_tn(khs[h][rows], vn)
            bg_vpu(1)
        bg_vpu.drain()
        for h in heads:
            hs = slice(h * GDN_DV, (h + 1) * GDN_DV)
            s_scr[h] = s[h]
            vn = jnp.concatenate(vn_parts[h], axis=0).astype(BF16)
            o = jnp.concatenate(qs_parts[h], axis=0) + grp_dot(qks[h], vn)
            o_ref[0, :, hs] = (_rms(o, ng_ref[...]) * _silu(zz[:, hs])).astype(BF16)
        return carry

    lax.fori_loop(0, jnp.minimum(t + 1, 1), solve, 0)

    @pl.when(t == pl.num_programs(1) - 1)
    def _():
        s_out_ref[0] = s_scr[...]


def _gdn_prompt(x3, mix_g, w_rg, w_ba, w_bat, conv_w, alog_r, dt_r, alog_c, dt_c, norm_g, tb=256):
    b, t, _ = x3.shape
    nt = t // tb
    assert t % tb == 0 and nt >= 2, "a sequence's last block must be prepared during one of its own steps"
    lbd, ubd, obd, csel, bias, offdiag, eye = _gdn_tables(tb)
    vw = GDN_HEADS * GDN_DV
    c2 = lambda i, j: (0, 0)
    single = pl.Buffered(1)
    return pl.pallas_call(
        functools.partial(_gdn_body, tb),
        grid=(b, nt),
        in_specs=[
            pl.BlockSpec((1, tb, D_MODEL), lambda i, j: (0, 0, 0)),
            pl.BlockSpec((1, tb, D_MODEL), _next_block(nt, b * nt)),
            pl.BlockSpec((1, D_MODEL), c2),
            pl.BlockSpec((D_MODEL, CONV_CH), lambda i, j: (0, COL_GDN // CONV_CH), pipeline_mode=single),
            pl.BlockSpec((D_MODEL, vw), lambda i, j: (0, (COL_GDN + CONV_CH) // vw), pipeline_mode=single),
            pl.BlockSpec((D_MODEL, BA_PAD), c2),
            pl.BlockSpec((BA_PAD, D_MODEL), c2),
            pl.BlockSpec((CONV_W, CONV_CH), c2),
            pl.BlockSpec((1, BA_PAD), c2),
            pl.BlockSpec((1, BA_PAD), c2),
            pl.BlockSpec((BA_PAD, 1), c2),
            pl.BlockSpec((BA_PAD, 1), c2),
            pl.BlockSpec((1, GDN_DV), c2),
            pl.BlockSpec((tb, tb), c2),
            pl.BlockSpec((tb, tb), c2),
            pl.BlockSpec((tb, tb), c2),
            pl.BlockSpec((tb, (tb // GDN_CHUNK) * 128), c2),
            pl.BlockSpec((tb, 2 * GDN_CHUNK), c2),
            pl.BlockSpec((tb, 2 * GDN_CHUNK), c2),
            pl.BlockSpec((tb, 2 * GDN_CHUNK), c2),
        ],
        out_specs=[
            pl.BlockSpec((1, tb, vw), lambda i, j: (i, j, 0)),
            pl.BlockSpec((1, GDN_HEADS, GDN_DK, GDN_DV), lambda i, j: (i, 0, 0, 0)),
            pl.BlockSpec((1, CONV_W - 1, CONV_CH), lambda i, j: (i, 0, 0)),
        ],
        out_shape=[
            jax.ShapeDtypeStruct((b, t, vw), BF16),
            jax.ShapeDtypeStruct((b, GDN_HEADS, GDN_DK, GDN_DV), F32),
            jax.ShapeDtypeStruct((b, CONV_W - 1, CONV_CH), F32),
        ],
        scratch_shapes=[pltpu.VMEM((GDN_HEADS, GDN_DK, GDN_DV), F32), pltpu.VMEM((tb + 8, CONV_CH), F32)]
        + 2 * [pltpu.VMEM((tb, CONV_CH), F32), pltpu.VMEM((tb, vw), F32),
               pltpu.VMEM((tb, BA_PAD), F32), pltpu.VMEM((BA_PAD, tb), F32)],
        compiler_params=_params("arbitrary", "arbitrary"),
        name="gdn_prompt",
    )(x3, x3, mix_g, w_rg, w_rg, w_ba, w_bat, conv_w, alog_r, dt_r, alog_c, dt_c, norm_g, lbd, ubd, obd, csel,
      bias, offdiag, eye)


def _block_diag_rows(x, nblk):
    row = lax.broadcasted_iota(jnp.int32, (8, nblk * 128), 0)
    blk = lax.broadcasted_iota(jnp.int32, (8, nblk * 128), 1) // 128
    return jnp.where(row == blk, jnp.concatenate([x] * nblk, axis=1), 0.0)


def _sample_step_body(cdec, qk_ref, rv_ref, rg_ref, x_ref, z_ref, bcol_ref, sr_ref, sg_ref, sc_ref, cos_ref, sin_ref,
                      cw_ref, alog_ref, dt_ref, gn_ref, ng_ref,
                      oa_ref, ob_ref, sr_out_ref, sg_out_ref, sc_out_ref):
    nh = GDN_HEADS
    cos, sin = cos_ref[...], sin_ref[...]
    gam = jnp.where(lax.broadcasted_iota(jnp.int32, (8, 1), 0) == 0, cdec[0], 0.0)
    for h in range(1, RET_HEADS):
        gam = jnp.where(lax.broadcasted_iota(jnp.int32, (8, 1), 0) == h, cdec[h], gam)
    zero4 = jnp.zeros((RET_HEADS, RET_DV), F32)
    for g in range(x_ref.shape[0]):
        x_new = x_ref[g]
        buf = sc_ref[g]
        conv = x_new * cw_ref[CONV_W - 1]
        for i in range(CONV_W - 1):
            conv = conv + buf[i] * cw_ref[i]
        conv = _silu(conv)
        sc_out_ref[g, 0] = buf[1]
        sc_out_ref[g, 1] = buf[2]
        sc_out_ref[g, 2] = x_new
        q, k, v = conv[0:nh], conv[nh:2 * nh], conv[2 * nh:3 * nh]
        q = q * lax.rsqrt(jnp.sum(q * q, axis=-1, keepdims=True) + EPS) * (GDN_DK ** -0.5)
        k = k * lax.rsqrt(jnp.sum(k * k, axis=-1, keepdims=True) + EPS)
        col = bcol_ref[g]
        beta = jax.nn.sigmoid(col[0:nh])
        eg = jnp.exp(-jnp.exp(alog_ref[...]) * _softplus(col[nh:2 * nh] + dt_ref[...]))
        kbd, qbd = _block_diag_rows(k, nh), _block_diag_rows(q, nh)
        s_flat = sg_ref[g].reshape(nh * GDN_DK, GDN_DV)
        kq_s = _mm(jnp.concatenate([kbd, qbd], axis=0), s_flat)
        vn = beta * (v - eg * kq_s[0:nh])
        o = eg * kq_s[nh:2 * nh] + jnp.sum(q * k, axis=-1, keepdims=True) * vn
        ob_ref[g] = _rms(o, ng_ref[...]) * _silu(z_ref[g])
        upd = _mm_tn(kbd, vn)
        eg_l = jnp.broadcast_to(eg, (nh, GDN_DV))
        for h in range(nh):
            sg_out_ref[g, h] = eg_l[h:h + 1] * sg_ref[g, h] + upd[h * GDN_DK:(h + 1) * GDN_DK]

        qk = _rot(qk_ref[g], cos, sin)
        rq = jnp.concatenate([qk[0:RET_HEADS], qk[0:RET_HEADS]], axis=0)
        rk = jnp.concatenate([qk[RET_HEADS:], qk[RET_HEADS:]], axis=0) * (RET_DK ** -0.5)
        rv = jnp.concatenate([rv_ref[g], zero4], axis=0)
        qbd_r, kbd_r = _block_diag_rows(rq, RET_HEADS), _block_diag_rows(rk, RET_HEADS)
        sr_flat = sr_ref[g].reshape(RET_HEADS * RET_DK, RET_DV)
        q_s = _mm(qbd_r, sr_flat)
        o_r = gam * q_s + jnp.sum(rq * rk, axis=-1, keepdims=True) * rv
        oa_ref[g] = _group_norm_gate(o_r[0:RET_HEADS], rg_ref[g], gn_ref[...])
        upd_r = _mm_tn(kbd_r, rv)
        for h in range(RET_HEADS):
            sr_out_ref[g, h] = cdec[h] * sr_ref[g, h] + upd_r[h * RET_DK:(h + 1) * RET_DK]


def _sample_step(proj_ret, proj_gdn, ba, state_ret, state_gdn, state_conv, cos, sin, conv_w, a_log, dt_bias, gn,
                 norm_g, gs=8):
    ns = proj_ret.shape[0]
    nh, rh = GDN_HEADS, RET_HEADS
    nq = rh * RET_DK
    qk3 = proj_ret[:, :2 * nq].reshape(ns, 2 * rh, RET_DK)
    rv3 = proj_ret[:, 2 * nq:2 * nq + rh * RET_DV].reshape(ns, rh, RET_DV)
    rg3 = proj_ret[:, 2 * nq + rh * RET_DV:].reshape(ns, rh, RET_DV)
    x3 = proj_gdn[:, :CONV_CH].reshape(ns, 3 * nh, GDN_DK)
    z3 = proj_gdn[:, CONV_CH:].reshape(ns, nh, GDN_DV)
    bcol = ba[:, :2 * nh].reshape(ns, 2 * nh, 1)
    sc4 = state_conv.reshape(ns, CONV_W - 1, 3 * nh, GDN_DK)
    cw3 = conv_w.reshape(CONV_W, 3 * nh, GDN_DK)
    cdec = [float(v) for v in np.exp(_RET_LOG_G)]
    c2 = lambda i: (0, 0)
    c3 = lambda i: (0, 0, 0)
    b3 = lambda i: (i, 0, 0)
    b4 = lambda i: (i, 0, 0, 0)
    oa, ob, sr, sg, sc = pl.pallas_call(
        functools.partial(_sample_step_body, cdec),
        grid=(ns // gs,),
        in_specs=[
            pl.BlockSpec((gs, 2 * rh, RET_DK), b3),
            pl.BlockSpec((gs, rh, RET_DV), b3),
            pl.BlockSpec((gs, rh, RET_DV), b3),
            pl.BlockSpec((gs, 3 * nh, GDN_DK), b3),
            pl.BlockSpec((gs, nh, GDN_DV), b3),
            pl.BlockSpec((gs, 2 * nh, 1), b3),
            pl.BlockSpec((gs, rh, RET_DK, RET_DV), b4),
            pl.BlockSpec((gs, nh, GDN_DK, GDN_DV), b4),
            pl.BlockSpec((gs, CONV_W - 1, 3 * nh, GDN_DK), b4),
            pl.BlockSpec((1, RET_DK), c2),
            pl.BlockSpec((1, RET_DK), c2),
            pl.BlockSpec((CONV_W, 3 * nh, GDN_DK), c3),
            pl.BlockSpec((nh, 1), c2),
            pl.BlockSpec((nh, 1), c2),
            pl.BlockSpec((rh, RET_DV), c2),
            pl.BlockSpec((1, GDN_DV), c2),
        ],
        out_specs=[
            pl.BlockSpec((gs, rh, RET_DV), b3),
            pl.BlockSpec((gs, nh, GDN_DV), b3),
            pl.BlockSpec((gs, rh, RET_DK, RET_DV), b4),
            pl.BlockSpec((gs, nh, GDN_DK, GDN_DV), b4),
            pl.BlockSpec((gs, CONV_W - 1, 3 * nh, GDN_DK), b4),
        ],
        out_shape=[
            jax.ShapeDtypeStruct((ns, rh, RET_DV), F32),
            jax.ShapeDtypeStruct((ns, nh, GDN_DV), F32),
            jax.ShapeDtypeStruct(state_ret.shape, F32),
            jax.ShapeDtypeStruct(state_gdn.shape, F32),
            jax.ShapeDtypeStruct(sc4.shape, F32),
        ],
        compiler_params=_params("parallel"),
        name="sample_step",
    )(qk3, rv3, rg3, x3, z3, bcol, state_ret, state_gdn, sc4, cos, sin, cw3, a_log.reshape(nh, 1),
      dt_bias.reshape(nh, 1), gn.reshape(rh, RET_DV), norm_g)
    return (oa.reshape(ns, rh * RET_DV).astype(BF16), ob.reshape(ns, nh * GDN_DV).astype(BF16), sr, sg,
            sc.reshape(state_conv.shape))


def _merge_body(x_ref, nmg_ref, wgt_ref, oa_ref, ob_ref, wa_ref, wb_ref, wo_ref, ng_ref, wq_ref, x1_ref, q_ref):
    x = x_ref[...]
    gates = jnp.dot(_rms(x, nmg_ref[...]).astype(BF16), wgt_ref[...], preferred_element_type=F32)
    ya = jnp.dot(oa_ref[...], wa_ref[...], preferred_element_type=F32)
    yb = jnp.dot(ob_ref[...], wb_ref[...], preferred_element_type=F32)
    merged = jax.nn.sigmoid(gates[:, :D_MODEL]) * ya + jax.nn.sigmoid(gates[:, D_MODEL:]) * yb
    x1 = x + _mm(merged, wo_ref[...])
    x1_ref[...] = x1
    q_ref[...] = _mm(_rms(x1, ng_ref[...]), wq_ref[...]).astype(BF16)


def _merge(x, mix_g, w_gates, oa, ob, wa, wb, wo, ng, wq, tm):
    m = x.shape[0]
    row = lambda i: (i, 0)
    c2 = lambda i: (0, 0)
    wspec = pl.BlockSpec((D_MODEL, D_MODEL), c2)
    return pl.pallas_call(
        _merge_body,
        grid=(m // tm,),
        in_specs=[
            pl.BlockSpec((tm, D_MODEL), row),
            pl.BlockSpec((1, D_MODEL), c2),
            pl.BlockSpec((D_MODEL, 2 * D_MODEL), c2),
            pl.BlockSpec((tm, D_MODEL), row),
            pl.BlockSpec((tm, D_MODEL), row),
            wspec, wspec, wspec,
            pl.BlockSpec((1, D_MODEL), c2),
            wspec,
        ],
        out_specs=[pl.BlockSpec((tm, D_MODEL), row), pl.BlockSpec((tm, D_MODEL), row)],
        out_shape=[jax.ShapeDtypeStruct((m, D_MODEL), F32), jax.ShapeDtypeStruct((m, D_MODEL), BF16)],
        compiler_params=_params("parallel"),
        name="merge",
    )(x, mix_g, w_gates, oa, ob, wa, wb, wo, ng, wq)


def _memkv_body(m_ref, g_ref, wk_ref, wv_ref, k_ref, v_ref, k4_ref, v4_ref):
    mn = _rms(m_ref[...], g_ref[...]).astype(BF16)
    k = jnp.dot(mn, wk_ref[...], preferred_element_type=F32)
    v = jnp.dot(mn, wv_ref[...], preferred_element_type=F32)
    k_ref[...] = k
    v_ref[...] = v
    for h in range(X_HEADS):
        k4_ref[:, h, :] = k[:, h * X_HD:(h + 1) * X_HD]
        v4_ref[:, h, :] = v[:, h * X_HD:(h + 1) * X_HD]


def _memkv(mem, g, wk, wv, tm=512):
    m = mem.shape[0]
    row = lambda i: (i, 0)
    row3 = lambda i: (i, 0, 0)
    c2 = lambda i: (0, 0)
    return pl.pallas_call(
        _memkv_body,
        grid=(m // tm,),
        in_specs=[pl.BlockSpec((tm, D_MODEL), row), pl.BlockSpec((1, D_MODEL), c2),
                  pl.BlockSpec((D_MODEL, D_MODEL), c2), pl.BlockSpec((D_MODEL, D_MODEL), c2)],
        out_specs=[pl.BlockSpec((tm, D_MODEL), row), pl.BlockSpec((tm, D_MODEL), row),
                   pl.BlockSpec((tm, X_HEADS, X_HD), row3), pl.BlockSpec((tm, X_HEADS, X_HD), row3)],
        out_shape=[jax.ShapeDtypeStruct((m, D_MODEL), F32)] * 2
        + [jax.ShapeDtypeStruct((m, X_HEADS, X_HD), F32)] * 2,
        compiler_params=_params("parallel"),
        name="memkv",
    )(mem, g, wk, wv)


def _xattn_body(q_ref, mk_ref, mv_ref, x1_ref, wo_ref, x2_ref):
    parts = []
    for h in range(X_HEADS):
        hs = slice(h * X_HD, (h + 1) * X_HD)
        s = _mm_nt(q_ref[0, :, hs], mk_ref[0, :, hs]) * (X_HD ** -0.5)
        p = jnp.exp(s - jnp.max(s, axis=-1, keepdims=True))
        o = _mm(p, mv_ref[0, :, hs]) / jnp.sum(p, axis=-1, keepdims=True)
        parts.append(o.astype(BF16))
    x2_ref[0] = x1_ref[0] + jnp.dot(jnp.concatenate(parts, axis=1), wo_ref[...], preferred_element_type=F32)


def _xattn_prompt(q3, mk3, mv3, x13, wo, tq=512):
    b, t, _ = q3.shape
    tok = lambda i, j: (i, j, 0)
    mem = lambda i, j: (i, 0, 0)
    return pl.pallas_call(
        _xattn_body,
        grid=(b, t // tq),
        in_specs=[pl.BlockSpec((1, tq, D_MODEL), tok), pl.BlockSpec((1, N_MEM, D_MODEL), mem),
                  pl.BlockSpec((1, N_MEM, D_MODEL), mem), pl.BlockSpec((1, tq, D_MODEL), tok),
                  pl.BlockSpec((D_MODEL, D_MODEL), lambda i, j: (0, 0))],
        out_specs=pl.BlockSpec((1, tq, D_MODEL), tok),
        out_shape=jax.ShapeDtypeStruct((b, t, D_MODEL), F32),
        compiler_params=_params("parallel", "parallel"),
        name="xattn_prompt",
    )(q3, mk3, mv3, x13, wo)


def _xattn_sample_body(q_ref, mk_ref, mv_ref, o_ref):
    for g in range(q_ref.shape[0]):
        q = q_ref[g]
        s = jnp.sum(mk_ref[g] * q[None], axis=-1, keepdims=True) * (X_HD ** -0.5)
        p = jnp.exp(s - jnp.max(s, axis=0, keepdims=True))
        o_ref[g] = jnp.sum(p * mv_ref[g], axis=0) / jnp.sum(p, axis=0)


def _xattn_sample(q, mk4, mv4, gs=4):
    ns = q.shape[0]
    q3 = q.astype(F32).reshape(ns, X_HEADS, X_HD)
    row = lambda i: (i, 0, 0)
    mem = lambda i: (i, 0, 0, 0)
    return pl.pallas_call(
        _xattn_sample_body,
        grid=(ns // gs,),
        in_specs=[pl.BlockSpec((gs, X_HEADS, X_HD), row), pl.BlockSpec((gs, N_MEM, X_HEADS, X_HD), mem),
                  pl.BlockSpec((gs, N_MEM, X_HEADS, X_HD), mem)],
        out_specs=pl.BlockSpec((gs, X_HEADS, X_HD), row),
        out_shape=jax.ShapeDtypeStruct((ns, X_HEADS, X_HD), F32),
        compiler_params=_params("parallel"),
        name="xattn_sample",
    )(q3, mk4, mv4)


def _resid_mm_body(x_ref, a_ref, w_ref, o_ref):
    o_ref[...] = x_ref[...] + jnp.dot(a_ref[...], w_ref[...], preferred_element_type=F32)


def _resid_mm(x, a, w):
    m = x.shape[0]
    return pl.pallas_call(
        _resid_mm_body,
        out_shape=jax.ShapeDtypeStruct((m, D_MODEL), F32),
        compiler_params=pltpu.CompilerParams(vmem_limit_bytes=VMEM_LIMIT),
        name="resid_mm",
    )(x, a, w)


def _ffn_body(x_ref, ng_ref, wg_ref, wu_ref, wd_ref, nf_ref, y_ref):
    x = x_ref[...]
    h = _rms(x, ng_ref[...]).astype(BF16)
    gate = jnp.dot(h, wg_ref[...], preferred_element_type=F32)
    up = jnp.dot(h, wu_ref[...], preferred_element_type=F32)
    x3 = x + _mm(_silu(gate) * up, wd_ref[...])
    y_ref[...] = _rms(x3, nf_ref[...])


def _ffn(x, ng, wg, wu, wd, nf, tm):
    m = x.shape[0]
    dff = wg.shape[1]
    row = lambda i: (i, 0)
    c2 = lambda i: (0, 0)
    single = pl.Buffered(1)
    return pl.pallas_call(
        _ffn_body,
        grid=(m // tm,),
        in_specs=[pl.BlockSpec((tm, D_MODEL), row), pl.BlockSpec((1, D_MODEL), c2),
                  pl.BlockSpec((D_MODEL, dff), c2, pipeline_mode=single),
                  pl.BlockSpec((D_MODEL, dff), c2, pipeline_mode=single),
                  pl.BlockSpec((dff, D_MODEL), c2, pipeline_mode=single),
                  pl.BlockSpec((1, D_MODEL), c2)],
        out_specs=pl.BlockSpec((tm, D_MODEL), row),
        out_shape=jax.ShapeDtypeStruct((m, D_MODEL), F32),
        compiler_params=_params("parallel"),
        name="ffn",
    )(x, ng, wg, wu, wd, nf)


def _rope_tables(pos):
    half = RET_DK // 2
    inv = ROPE_BASE ** (-jnp.arange(half, dtype=F32) / half)
    ang = pos.astype(F32)[:, None] * inv[None, :]
    cos, sin = jnp.cos(ang), jnp.sin(ang)
    return jnp.concatenate([cos, cos], axis=-1), jnp.concatenate([-sin, sin], axis=-1)


def _pad_lanes(v, offset):
    return jnp.zeros((BA_PAD,), F32).at[offset:offset + v.shape[0]].set(v)


def kernel(x_prompt, x_sample, state_ret, state_gdn, state_conv, cache_mem_k, cache_mem_v, mem_prompt,
           norm_mix_g, w_in, ret_gn_g, w_branch_a, gdn_conv_w, gdn_a_log, gdn_dt_bias, gdn_norm_g,
           w_branch_b, w_out, norm_x_g, mem_norm_g, w_xq, w_xk, w_xv, w_xo, norm_ffn_g, w_gate, w_up,
           w_down, norm_final_g):
    depth = w_in.shape[0]
    assert depth == 1, "single-layer kernel"
    b, t, _ = x_prompt.shape
    ns = x_sample.shape[0]
    l = 0

    w = w_in[l]
    ba0, g0 = COL_GATE, COL_GATE + 2 * GDN_HEADS
    w_bf = w.astype(BF16)
    w_gates = w_bf[:, g0:]
    w_ba = jnp.pad(w[:, ba0:g0], ((0, 0), (0, BA_PAD - 2 * GDN_HEADS))).astype(BF16)
    w_bat = w_ba.T
    row = lambda v: v.reshape(1, -1)
    wa, wb, wo = w_branch_a[l].astype(BF16), w_branch_b[l].astype(BF16), w_out[l].astype(BF16)
    wq, wk, wv, wxo = w_xq[l].astype(BF16), w_xk[l].astype(BF16), w_xv[l].astype(BF16), w_xo[l].astype(BF16)
    wg, wu, wd = w_gate[l].astype(BF16), w_up[l].astype(BF16), w_down[l].astype(BF16)
    alog_r = _pad_lanes(gdn_a_log[l], GDN_HEADS).reshape(1, BA_PAD)
    dt_r = _pad_lanes(gdn_dt_bias[l], GDN_HEADS).reshape(1, BA_PAD)
    alog_c, dt_c = alog_r.reshape(BA_PAD, 1), dt_r.reshape(BA_PAD, 1)
    cos_p, sin_p = _rope_tables(jnp.arange(t))
    cos_s, sin_s = _rope_tables(PAST_LEN + jnp.arange(1))

    mix_g = row(norm_mix_g[l])
    xp = x_prompt.reshape(b * t, D_MODEL)
    oa_p, sr_p = _retention_prompt(x_prompt, mix_g, w_bf, cos_p, sin_p, row(ret_gn_g[l]))
    ob_p, sg_p, sc_p = _gdn_prompt(x_prompt, mix_g, w_bf, w_ba, w_bat, gdn_conv_w[l], alog_r, dt_r,
                                   alog_c, dt_c, row(gdn_norm_g[l]))
    x1_p, q_p = _merge(xp, mix_g, w_gates, oa_p.reshape(b * t, -1), ob_p.reshape(b * t, -1), wa, wb, wo,
                       row(norm_x_g[l]), wq, tm=512)
    mk_p, mv_p, mk4_p, mv4_p = _memkv(mem_prompt.reshape(b * N_MEM, D_MODEL), row(mem_norm_g[l]), wk, wv)
    x2_p = _xattn_prompt(q_p.reshape(b, t, D_MODEL), mk_p.reshape(b, N_MEM, D_MODEL),
                         mv_p.reshape(b, N_MEM, D_MODEL), x1_p.reshape(b, t, D_MODEL), wxo)
    y_p = _ffn(x2_p.reshape(b * t, D_MODEL), row(norm_ffn_g[l]), wg, wu, wd, row(norm_final_g), tm=512)

    xs = x_sample.reshape(ns, D_MODEL)
    proj_s, ba_s = _inproj(xs, mix_g, w_bf, COL_GATE, w_ba)
    oa_s, ob_s, sr_s, sg_s, sc_s = _sample_step(proj_s[:, :COL_GDN], proj_s[:, COL_GDN:], ba_s, state_ret[l],
                                                state_gdn[l], state_conv[l],
                                                cos_s, sin_s, gdn_conv_w[l], gdn_a_log[l], gdn_dt_bias[l],
                                                ret_gn_g[l], row(gdn_norm_g[l]))
    x1_s, q_s = _merge(xs, mix_g, w_gates, oa_s.reshape(ns, -1), ob_s.reshape(ns, -1), wa, wb, wo,
                       row(norm_x_g[l]), wq, tm=ns)
    o_s = _xattn_sample(q_s, cache_mem_k[l], cache_mem_v[l])
    x2_s = _resid_mm(x1_s, o_s.reshape(ns, D_MODEL).astype(BF16), wxo)
    y_s = _ffn(x2_s, row(norm_ffn_g[l]), wg, wu, wd, row(norm_final_g), tm=ns)

    return (y_p.reshape(b, t, D_MODEL), y_s.reshape(ns, 1, D_MODEL),
            sr_p[None], sg_p[None], sc_p[None],
            mk4_p.reshape(1, b, N_MEM, X_HEADS, X_HD), mv4_p.reshape(1, b, N_MEM, X_HEADS, X_HD),
            sr_s[None], sg_s[None], sc_s[None])
```

```python
import functools

import numpy as np
import jax
import jax.numpy as jnp
from jax import lax
from jax.experimental import pallas as pl
from jax.experimental.pallas import tpu as pltpu

F32 = jnp.float32
BF16 = jnp.bfloat16

D_MODEL = 1024
RET_HEADS, RET_DK, RET_DV = 4, 128, 256
GDN_HEADS, GDN_DK, GDN_DV = 8, 128, 128
CONV_W = 4
CONV_CH = 3 * GDN_HEADS * GDN_DK
N_MEM, X_HEADS, X_HD = 256, 4, 256
PAST_LEN = 16384
ROPE_BASE = 10000.0
EPS = 1e-6
GDN_CHUNK = 64

COL_RET = 0
COL_GDN = 3072
COL_GATE = 7168
BA_PAD = 128

VMEM_LIMIT = 56 * 1024 * 1024

NT_DIMS = (((1,), (1,)), ((), ()))
TN_DIMS = (((0,), (0,)), ((), ()))


def _mm(a, b):
    return jnp.dot(a.astype(BF16), b.astype(BF16), preferred_element_type=F32)


def _mm_nt(a, b):
    return lax.dot_general(a.astype(BF16), b.astype(BF16), NT_DIMS, preferred_element_type=F32)


def _mm_tn(a, b):
    return lax.dot_general(a.astype(BF16), b.astype(BF16), TN_DIMS, preferred_element_type=F32)


def _split3(x):
    hi = x.astype(BF16)
    r = x - hi.astype(F32)
    mid = r.astype(BF16)
    return hi, mid, (r - mid.astype(F32)).astype(BF16)


def _mm_sel(sel, x):
    return sum(jnp.dot(sel, p, preferred_element_type=F32) for p in _split3(x))


def _mm_sel_r(x, sel):
    return sum(jnp.dot(p, sel, preferred_element_type=F32) for p in _split3(x))


def _rms(x, g):
    return x * lax.rsqrt(jnp.mean(x * x, axis=-1, keepdims=True) + EPS) * g


def _silu(x):
    h = 0.5 * x
    return h + h * jnp.tanh(h)


def _softplus(x):
    return jnp.maximum(x, 0.0) + jnp.log1p(jnp.exp(-jnp.abs(x)))


def _params(*sem):
    return pltpu.CompilerParams(dimension_semantics=sem, vmem_limit_bytes=VMEM_LIMIT)


def _inproj_body(x_ref, g_ref, w_ref, wba_ref, o_ref, oba_ref, h_scr):
    @pl.when(pl.program_id(0) == 0)
    def _():
        hb = _rms(x_ref[...], g_ref[...]).astype(BF16)
        h_scr[...] = hb
        oba_ref[...] = jnp.dot(hb, wba_ref[...], preferred_element_type=F32)

    o_ref[...] = jnp.dot(h_scr[...], w_ref[...], preferred_element_type=F32)


def _inproj(x, g, w_bf, n, w_ba, tn=1024):
    m = x.shape[0]
    assert n % tn == 0 and n <= w_bf.shape[1]
    return pl.pallas_call(
        _inproj_body,
        grid=(n // tn,),
        in_specs=[
            pl.BlockSpec((m, D_MODEL), lambda j: (0, 0)),
            pl.BlockSpec((1, D_MODEL), lambda j: (0, 0)),
            pl.BlockSpec((D_MODEL, tn), lambda j: (0, j)),
            pl.BlockSpec((D_MODEL, BA_PAD), lambda j: (0, 0)),
        ],
        out_specs=[
            pl.BlockSpec((m, tn), lambda j: (0, j)),
            pl.BlockSpec((m, BA_PAD), lambda j: (0, 0)),
        ],
        out_shape=[
            jax.ShapeDtypeStruct((m, n), F32),
            jax.ShapeDtypeStruct((m, BA_PAD), F32),
        ],
        scratch_shapes=[pltpu.VMEM((m, D_MODEL), BF16)],
        compiler_params=_params("arbitrary"),
        name="inproj",
    )(x, g, w_bf, w_ba)


_RET_LOG_G = np.log1p(-np.exp2(-5.0 - np.arange(RET_HEADS, dtype=np.float64)))


def _ret_tables(c):
    idx = np.arange(c, dtype=np.float64)
    diff = idx[:, None] - idx[None, :]
    dmat = np.where(diff >= 0, np.exp(np.maximum(diff, 0.0)[None] * _RET_LOG_G[:, None, None]), 0.0)
    qdec = np.exp((idx + 1.0)[None, :] * _RET_LOG_G[:, None])
    kdec = np.exp((c - 1.0 - idx)[None, :] * _RET_LOG_G[:, None])
    lane = np.ones((1, 1, RET_DK))
    return (jnp.asarray(dmat, F32), jnp.asarray(qdec[:, :, None] * lane, F32),
            jnp.asarray(kdec[:, :, None] * lane, F32), [float(v) for v in np.exp(c * _RET_LOG_G)])


def _rot(x, cos, sin):
    return x * cos + pltpu.roll(x, RET_DK // 2, 1) * sin


def _group_norm_gate(o, gate, gn):
    mu = jnp.mean(o, axis=-1, keepdims=True)
    d = o - mu
    var = jnp.mean(d * d, axis=-1, keepdims=True)
    return _silu(gate) * (d * lax.rsqrt(var + EPS) * gn)


def _proj_tiles(hb, w_ref, dst, width=512):
    for c0 in range(0, w_ref.shape[1], width):
        dst[:, c0:c0 + width] = jnp.dot(hb, w_ref[:, c0:c0 + width], preferred_element_type=F32)


def _next_block(nt, nblocks):
    def index_map(i, j):
        n1 = jnp.minimum(i * nt + j + 1, nblocks - 1)
        return (n1 // nt, n1 % nt, 0)
    return index_map


def _ret_body(cdec, x0_ref, xn_ref, nmg_ref, w_ref, *rest):
    *consts, o_ref, s_out_ref, s_scr, pa, pb = rest
    t = pl.program_id(1)
    n = pl.program_id(0) * pl.num_programs(1) + t
    bufs = (pa, pb)

    @pl.when(t == 0)
    def _():
        s_scr[...] = jnp.zeros_like(s_scr)

    @pl.when(n == 0)
    def _():
        _proj_tiles(_rms(x0_ref[0], nmg_ref[...]).astype(BF16), w_ref, pa)

    for slot in range(2):
        @pl.when(n % 2 == slot)
        def _(slot=slot):
            _ret_step(cdec, t, xn_ref, nmg_ref, w_ref, bufs[slot], bufs[1 - slot], *consts, o_ref, s_out_ref, s_scr)


def _ret_step(cdec, t, xn_ref, nmg_ref, w_ref, proj, proj_next, cos_ref, sin_ref, dmat_ref, qdec_ref, kdec_ref,
              gn_ref, o_ref, s_out_ref, s_scr):
    qw = RET_HEADS * RET_DK
    _proj_tiles(_rms(xn_ref[0], nmg_ref[...]).astype(BF16), w_ref, proj_next)

    chunk = dmat_ref.shape[1]
    for c0 in range(0, proj.shape[0], chunk):
        rows = slice(c0, c0 + chunk)
        cos, sin = cos_ref[rows, :], sin_ref[rows, :]
        for h in range(RET_HEADS):
            qk = slice(h * RET_DK, (h + 1) * RET_DK)
            kk = slice(qw + h * RET_DK, qw + (h + 1) * RET_DK)
            vv = slice(h * RET_DV, (h + 1) * RET_DV)
            q = _rot(proj[rows, qk], cos, sin)
            k = _rot(proj[rows, kk], cos, sin) * (RET_DK ** -0.5)
            v = proj[rows, 2 * qw + h * RET_DV:2 * qw + (h + 1) * RET_DV]
            gate = proj[rows, 2 * qw + (RET_HEADS + h) * RET_DV:2 * qw + (RET_HEADS + h + 1) * RET_DV]
            s = s_scr[h]
            scores = _mm_nt(q, k) * dmat_ref[h]
            o = _mm(scores, v) + _mm(q * qdec_ref[h], s)
            s_scr[h] = cdec[h] * s + _mm_tn(k * kdec_ref[h], v)
            o_ref[0, rows, vv] = _group_norm_gate(o, gate, gn_ref[:, vv]).astype(BF16)

    @pl.when(t == pl.num_programs(1) - 1)
    def _():
        s_out_ref[0] = s_scr[...]


def _retention_prompt(x3, mix_g, w_rg, cos, sin, gn, tb=512, chunk=256):
    b, t, _ = x3.shape
    nt = t // tb
    dmat, qdec, kdec, cdec = _ret_tables(chunk)
    vw = RET_HEADS * RET_DV
    const3 = lambda i, j: (0, 0, 0)
    return pl.pallas_call(
        functools.partial(_ret_body, cdec),
        grid=(b, nt),
        in_specs=[
            pl.BlockSpec((1, tb, D_MODEL), const3),
            pl.BlockSpec((1, tb, D_MODEL), _next_block(nt, b * nt)),
            pl.BlockSpec((1, D_MODEL), lambda i, j: (0, 0)),
            pl.BlockSpec((D_MODEL, COL_GDN), lambda i, j: (0, COL_RET // COL_GDN), pipeline_mode=pl.Buffered(1)),
            pl.BlockSpec((tb, RET_DK), lambda i, j: (j, 0)),
            pl.BlockSpec((tb, RET_DK), lambda i, j: (j, 0)),
            pl.BlockSpec((RET_HEADS, chunk, chunk), const3),
            pl.BlockSpec((RET_HEADS, chunk, RET_DK), const3),
            pl.BlockSpec((RET_HEADS, chunk, RET_DK), const3),
            pl.BlockSpec((1, vw), lambda i, j: (0, 0)),
        ],
        out_specs=[
            pl.BlockSpec((1, tb, vw), lambda i, j: (i, j, 0)),
            pl.BlockSpec((1, RET_HEADS, RET_DK, RET_DV), lambda i, j: (i, 0, 0, 0)),
        ],
        out_shape=[
            jax.ShapeDtypeStruct((b, t, vw), BF16),
            jax.ShapeDtypeStruct((b, RET_HEADS, RET_DK, RET_DV), F32),
        ],
        scratch_shapes=[pltpu.VMEM((RET_HEADS, RET_DK, RET_DV), F32)]
        + 2 * [pltpu.VMEM((tb, COL_GDN), F32)],
        compiler_params=_params("arbitrary", "arbitrary"),
        name="retention_prompt",
    )(x3, x3, mix_g, w_rg, cos, sin, dmat, qdec, kdec, gn)


def _gdn_tables(tb):
    idx = np.arange(tb)
    same = (idx[:, None] // GDN_CHUNK) == (idx[None, :] // GDN_CHUNK)
    lower = same & (idx[:, None] >= idx[None, :])
    nchunk = tb // GDN_CHUNK
    chunk_sel = np.repeat((idx[:, None] // GDN_CHUNK) == np.arange(nchunk)[None, :], 128, axis=1)
    grp = 2 * GDN_CHUNK
    il = (idx % grp)[:, None]
    jl = np.arange(grp)[None, :]
    bias = np.where((il // GDN_CHUNK == jl // GDN_CHUNK) & (il >= jl), 0.0, -1e30)
    eye = (il == jl).astype(np.float64)
    return (jnp.asarray(lower, BF16), jnp.asarray(lower.T, BF16), jnp.asarray(same, BF16),
            jnp.asarray(chunk_sel, BF16), jnp.asarray(bias, F32), jnp.asarray(eye - 1.0, F32),
            jnp.asarray(eye, BF16))


def _gdn_prepare(x_ref, first_of_seq, nmg_ref, wqkv_ref, wz_ref, wba_ref, wbat_ref, cw_ref, xr, cs_buf, zs, bas,
                 bats, tb):
    st = {}

    def start():
        xr[0:8, :] = jnp.where(first_of_seq, 0.0, xr[tb:tb + 8, :])
        st["hb"] = _rms(x_ref[0], nmg_ref[...]).astype(BF16)

    def tile(w_ref, dst, c0, width=512):
        def run():
            dst[:, c0:c0 + width] = jnp.dot(st["hb"], w_ref[:, c0:c0 + width], preferred_element_type=F32)
        return run

    def logits():
        bas[...] = jnp.dot(st["hb"], wba_ref[...], preferred_element_type=F32)
        bats[...] = lax.dot_general(wbat_ref[...], st["hb"], NT_DIMS, preferred_element_type=F32)

    def conv(c0):
        def run():
            cs = slice(c0, c0 + GDN_DK)
            acc = xr[5:5 + tb, cs] * cw_ref[0:1, cs]
            for i in range(1, CONV_W):
                acc = acc + xr[5 + i:5 + i + tb, cs] * cw_ref[i:i + 1, cs]
            cs_buf[:, cs] = _silu(acc)
        return run

    mxu_items = ([start] + [tile(wqkv_ref, xr.at[8:8 + tb], c0) for c0 in range(0, CONV_CH, 512)]
                 + [tile(wz_ref, zs, c0) for c0 in range(0, wz_ref.shape[1], 512)] + [logits])
    vpu_items = [conv(c0) for c0 in range(0, CONV_CH, GDN_DK)]
    return mxu_items, vpu_items


class _Background:
    def __init__(self, items):
        self._items = list(items)

    def __call__(self, count):
        for item in self._items[:count]:
            item()
        del self._items[:count]

    def drain(self):
        self(len(self._items))


def _gdn_body(tb, x0_ref, xn_ref, nmg_ref, wqkv_ref, wz_ref, wba_ref, wbat_ref, cw_ref, *rest):
    *consts, o_ref, s_out_ref, conv_out_ref, s_scr, xr, ca, za, baa, bata, cb, zb, bab, batb = rest
    t = pl.program_id(1)
    nt = pl.num_programs(1)
    n = pl.program_id(0) * nt + t
    weights = (nmg_ref, wqkv_ref, wz_ref, wba_ref, wbat_ref, cw_ref)
    bufs = ((ca, za, baa, bata), (cb, zb, bab, batb))

    @pl.when(t == 0)
    def _():
        s_scr[...] = jnp.zeros_like(s_scr)

    @pl.when(n == 0)
    def _():
        mxu_items, vpu_items = _gdn_prepare(x0_ref, True, *weights, xr, *bufs[0], tb)
        _Background(mxu_items + vpu_items).drain()

    for slot in range(2):
        @pl.when(n % 2 == slot)
        def _(slot=slot):
            mxu_items, vpu_items = _gdn_prepare(xn_ref, t == nt - 1, *weights, xr, *bufs[1 - slot], tb)
            _gdn_step(tb, t, bufs[slot], *consts, o_ref, s_out_ref, s_scr,
                      _Background(mxu_items), _Background(vpu_items))

    @pl.when(t == nt - 2)
    def _():
        conv_out_ref[0] = xr[tb + 5:tb + 8, :]


def _gdn_step(tb, t, cur, alog_r_ref, dt_r_ref, alog_c_ref, dt_c_ref, ng_ref,
              lbd_ref, ubd_ref, obd_ref, csel_ref, bias_ref, offd_ref, eye_ref, o_ref, s_out_ref, s_scr,
              bg_mxu, bg_vpu):
    nchunk = tb // GDN_CHUNK
    hk = GDN_HEADS * GDN_DK
    c_scr, zz, ba_ref, bat_ref = cur
    ba, bat = ba_ref[...], bat_ref[...]

    bg_mxu(2)
    beta_c = jax.nn.sigmoid(ba)
    g_c = -jnp.exp(alog_r_ref[...]) * _softplus(ba + dt_r_ref[...])
    g_r = -jnp.exp(alog_c_ref[...]) * _softplus(bat + dt_c_ref[...])
    gc_c = _mm_sel(lbd_ref[...], g_c)
    gt_c = _mm_sel(obd_ref[...], g_c)
    gc_r = _mm_sel_r(g_r, ubd_ref[...])
    gt_l = _mm_sel_r(g_r, csel_ref[...])

    heads = range(GDN_HEADS)
    grp = 2 * GDN_CHUNK
    groups = [slice(p * grp, (p + 1) * grp) for p in range(tb // grp)]

    def grp_dot(a, b):
        return jnp.concatenate([jnp.dot(a[g], b[g], preferred_element_type=F32) for g in groups], axis=0)

    def grp_dot_nt(a, b):
        return jnp.concatenate([lax.dot_general(a[g], b[g], NT_DIMS, preferred_element_type=F32)
                                for g in groups], axis=0)

    bias = bias_ref[...]
    offdiag = offd_ref[...]
    eye_b = eye_ref[...]
    qs, ks, gammas, pbs, rhss, qgs, khs = [], [], [], [], [], [], []
    for h in heads:
        q = c_scr[:, h * GDN_DK:(h + 1) * GDN_DK]
        k = c_scr[:, hk + h * GDN_DK:hk + (h + 1) * GDN_DK]
        v = c_scr[:, 2 * hk + h * GDN_DV:2 * hk + (h + 1) * GDN_DV]
        q = q * lax.rsqrt(jnp.sum(q * q, axis=-1, keepdims=True) + EPS) * (GDN_DK ** -0.5)
        k = k * lax.rsqrt(jnp.sum(k * k, axis=-1, keepdims=True) + EPS)
        beta = beta_c[:, h:h + 1]
        gcc = gc_c[:, 8 + h:9 + h]
        gtc = gt_c[:, 8 + h:9 + h]
        gcr = gc_r[8 + h:9 + h, :]
        dg = jnp.concatenate([gcc[g] - gcr[:, g] for g in groups], axis=0)
        gamma = jnp.exp(dg + bias)
        kbeta = k * beta
        kb = k.astype(BF16)
        pbs.append((grp_dot_nt(kbeta.astype(BF16), kb) * (gamma * offdiag)).astype(BF16))
        eg = jnp.exp(gcc)
        rhss.append(jnp.concatenate([v * beta, kbeta * eg], axis=1))
        qs.append(q.astype(BF16))
        ks.append(kb)
        gammas.append(gamma)
        qgs.append(q * eg)
        khs.append(k * jnp.exp(gtc - gcc))
        bg_mxu(1)
        bg_vpu(1)
    bg_mxu.drain()
    bg_vpu(4)

    def solve(_, carry):
        pb = pbs
        tbs = [pb[h] + eye_b for h in heads]
        for lvl in range(5):
            pb = [grp_dot(pb[h], pb[h]).astype(BF16) for h in heads]
            bg_vpu(1)
            tnew = [grp_dot(tbs[h], pb[h] + eye_b) for h in heads]
            bg_vpu(1)
            tbs = [x.astype(BF16) for x in tnew]

        us, ws, qks = [], [], []
        for h in heads:
            uw = rhss[h] + grp_dot(tbs[h] - eye_b, rhss[h].astype(BF16))
            us.append(uw[:, :GDN_DV])
            ws.append(uw[:, GDN_DV:])
            qks.append((grp_dot_nt(qs[h], ks[h]) * gammas[h]).astype(BF16))
        bg_vpu(2)

        s = [s_scr[h] for h in heads]
        vn_parts = [[] for _ in heads]
        qs_parts = [[] for _ in heads]
        for c in range(nchunk):
            rows = slice(c * GDN_CHUNK, (c + 1) * GDN_CHUNK)
            for h in heads:
                wq = _mm(jnp.concatenate([ws[h][rows], qgs[h][rows]], axis=0), s[h])
                vn = us[h][rows] - wq[:GDN_CHUNK]
                qs_parts[h].append(wq[GDN_CHUNK:])
                vn_parts[h].append(vn)
                decay = jnp.exp(gt_l[8 + h:9 + h, c * 128:(c + 1) * 128])
                s[h] = decay * s[h] + _mm_tn(khs[h][rows], vn)
            bg_vpu(1)
        bg_vpu.drain()
        for h in heads:
            hs = slice(h * GDN_DV, (h + 1) * GDN_DV)
            s_scr[h] = s[h]
            vn = jnp.concatenate(vn_parts[h], axis=0).astype(BF16)
            o = jnp.concatenate(qs_parts[h], axis=0) + grp_dot(qks[h], vn)
            o_ref[0, :, hs] = (_rms(o, ng_ref[...]) * _silu(zz[:, hs])).astype(BF16)
        return carry

    lax.fori_loop(0, jnp.minimum(t + 1, 1), solve, 0)

    @pl.when(t == pl.num_programs(1) - 1)
    def _():
        s_out_ref[0] = s_scr[...]


def _gdn_prompt(x3, mix_g, w_rg, w_ba, w_bat, conv_w, alog_r, dt_r, alog_c, dt_c, norm_g, tb=256):
    b, t, _ = x3.shape
    nt = t // tb
    assert t % tb == 0 and nt >= 2, "a sequence's last block must be prepared during one of its own steps"
    lbd, ubd, obd, csel, bias, offdiag, eye = _gdn_tables(tb)
    vw = GDN_HEADS * GDN_DV
    c2 = lambda i, j: (0, 0)
    single = pl.Buffered(1)
    return pl.pallas_call(
        functools.partial(_gdn_body, tb),
        grid=(b, nt),
        in_specs=[
            pl.BlockSpec((1, tb, D_MODEL), lambda i, j: (0, 0, 0)),
            pl.BlockSpec((1, tb, D_MODEL), _next_block(nt, b * nt)),
            pl.BlockSpec((1, D_MODEL), c2),
            pl.BlockSpec((D_MODEL, CONV_CH), lambda i, j: (0, COL_GDN // CONV_CH), pipeline_mode=single),
            pl.BlockSpec((D_MODEL, vw), lambda i, j: (0, (COL_GDN + CONV_CH) // vw), pipeline_mode=single),
            pl.BlockSpec((D_MODEL, BA_PAD), c2),
            pl.BlockSpec((BA_PAD, D_MODEL), c2),
            pl.BlockSpec((CONV_W, CONV_CH), c2),
            pl.BlockSpec((1, BA_PAD), c2),
            pl.BlockSpec((1, BA_PAD), c2),
            pl.BlockSpec((BA_PAD, 1), c2),
            pl.BlockSpec((BA_PAD, 1), c2),
            pl.BlockSpec((1, GDN_DV), c2),
            pl.BlockSpec((tb, tb), c2),
            pl.BlockSpec((tb, tb), c2),
            pl.BlockSpec((tb, tb), c2),
            pl.BlockSpec((tb, (tb // GDN_CHUNK) * 128), c2),
            pl.BlockSpec((tb, 2 * GDN_CHUNK), c2),
            pl.BlockSpec((tb, 2 * GDN_CHUNK), c2),
            pl.BlockSpec((tb, 2 * GDN_CHUNK), c2),
        ],
        out_specs=[
            pl.BlockSpec((1, tb, vw), lambda i, j: (i, j, 0)),
            pl.BlockSpec((1, GDN_HEADS, GDN_DK, GDN_DV), lambda i, j: (i, 0, 0, 0)),
            pl.BlockSpec((1, CONV_W - 1, CONV_CH), lambda i, j: (i, 0, 0)),
        ],
        out_shape=[
            jax.ShapeDtypeStruct((b, t, vw), BF16),
            jax.ShapeDtypeStruct((b, GDN_HEADS, GDN_DK, GDN_DV), F32),
            jax.ShapeDtypeStruct((b, CONV_W - 1, CONV_CH), F32),
        ],
        scratch_shapes=[pltpu.VMEM((GDN_HEADS, GDN_DK, GDN_DV), F32), pltpu.VMEM((tb + 8, CONV_CH), F32)]
        + 2 * [pltpu.VMEM((tb, CONV_CH), F32), pltpu.VMEM((tb, vw), F32),
               pltpu.VMEM((tb, BA_PAD), F32), pltpu.VMEM((BA_PAD, tb), F32)],
        compiler_params=_params("arbitrary", "arbitrary"),
        name="gdn_prompt",
    )(x3, x3, mix_g, w_rg, w_rg, w_ba, w_bat, conv_w, alog_r, dt_r, alog_c, dt_c, norm_g, lbd, ubd, obd, csel,
      bias, offdiag, eye)


def _block_diag_rows(x, nblk):
    row = lax.broadcasted_iota(jnp.int32, (8, nblk * 128), 0)
    blk = lax.broadcasted_iota(jnp.int32, (8, nblk * 128), 1) // 128
    return jnp.where(row == blk, jnp.concatenate([x] * nblk, axis=1), 0.0)


def _sample_step_body(cdec, qk_ref, rv_ref, rg_ref, x_ref, z_ref, bcol_ref, sr_ref, sg_ref, sc_ref, cos_ref, sin_ref,
                      cw_ref, alog_ref, dt_ref, gn_ref, ng_ref,
                      oa_ref, ob_ref, sr_out_ref, sg_out_ref, sc_out_ref):
    nh = GDN_HEADS
    cos, sin = cos_ref[...], sin_ref[...]
    gam = jnp.where(lax.broadcasted_iota(jnp.int32, (8, 1), 0) == 0, cdec[0], 0.0)
    for h in range(1, RET_HEADS):
        gam = jnp.where(lax.broadcasted_iota(jnp.int32, (8, 1), 0) == h, cdec[h], gam)
    zero4 = jnp.zeros((RET_HEADS, RET_DV), F32)
    for g in range(x_ref.shape[0]):
        x_new = x_ref[g]
        buf = sc_ref[g]
        conv = x_new * cw_ref[CONV_W - 1]
        for i in range(CONV_W - 1):
            conv = conv + buf[i] * cw_ref[i]
        conv = _silu(conv)
        sc_out_ref[g, 0] = buf[1]
        sc_out_ref[g, 1] = buf[2]
        sc_out_ref[g, 2] = x_new
        q, k, v = conv[0:nh], conv[nh:2 * nh], conv[2 * nh:3 * nh]
        q = q * lax.rsqrt(jnp.sum(q * q, axis=-1, keepdims=True) + EPS) * (GDN_DK ** -0.5)
        k = k * lax.rsqrt(jnp.sum(k * k, axis=-1, keepdims=True) + EPS)
        col = bcol_ref[g]
        beta = jax.nn.sigmoid(col[0:nh])
        eg = jnp.exp(-jnp.exp(alog_ref[...]) * _softplus(col[nh:2 * nh] + dt_ref[...]))
        kbd, qbd = _block_diag_rows(k, nh), _block_diag_rows(q, nh)
        s_flat = sg_ref[g].reshape(nh * GDN_DK, GDN_DV)
        kq_s = _mm(jnp.concatenate([kbd, qbd], axis=0), s_flat)
        vn = beta * (v - eg * kq_s[0:nh])
        o = eg * kq_s[nh:2 * nh] + jnp.sum(q * k, axis=-1, keepdims=True) * vn
        ob_ref[g] = _rms(o, ng_ref[...]) * _silu(z_ref[g])
        upd = _mm_tn(kbd, vn)
        eg_l = jnp.broadcast_to(eg, (nh, GDN_DV))
        for h in range(nh):
            sg_out_ref[g, h] = eg_l[h:h + 1] * sg_ref[g, h] + upd[h * GDN_DK:(h + 1) * GDN_DK]

        qk = _rot(qk_ref[g], cos, sin)
        rq = jnp.concatenate([qk[0:RET_HEADS], qk[0:RET_HEADS]], axis=0)
        rk = jnp.concatenate([qk[RET_HEADS:], qk[RET_HEADS:]], axis=0) * (RET_DK ** -0.5)
        rv = jnp.concatenate([rv_ref[g], zero4], axis=0)
        qbd_r, kbd_r = _block_diag_rows(rq, RET_HEADS), _block_diag_rows(rk, RET_HEADS)
        sr_flat = sr_ref[g].reshape(RET_HEADS * RET_DK, RET_DV)
        q_s = _mm(qbd_r, sr_flat)
        o_r = gam * q_s + jnp.sum(rq * rk, axis=-1, keepdims=True) * rv
        oa_ref[g] = _group_norm_gate(o_r[0:RET_HEADS], rg_ref[g], gn_ref[...])
        upd_r = _mm_tn(kbd_r, rv)
        for h in range(RET_HEADS):
            sr_out_ref[g, h] = cdec[h] * sr_ref[g, h] + upd_r[h * RET_DK:(h + 1) * RET_DK]


def _sample_step(proj_ret, proj_gdn, ba, state_ret, state_gdn, state_conv, cos, sin, conv_w, a_log, dt_bias, gn,
                 norm_g, gs=8):
    ns = proj_ret.shape[0]
    nh, rh = GDN_HEADS, RET_HEADS
    nq = rh * RET_DK
    qk3 = proj_ret[:, :2 * nq].reshape(ns, 2 * rh, RET_DK)
    rv3 = proj_ret[:, 2 * nq:2 * nq + rh * RET_DV].reshape(ns, rh, RET_DV)
    rg3 = proj_ret[:, 2 * nq + rh * RET_DV:].reshape(ns, rh, RET_DV)
    x3 = proj_gdn[:, :CONV_CH].reshape(ns, 3 * nh, GDN_DK)
    z3 = proj_gdn[:, CONV_CH:].reshape(ns, nh, GDN_DV)
    bcol = ba[:, :2 * nh].reshape(ns, 2 * nh, 1)
    sc4 = state_conv.reshape(ns, CONV_W - 1, 3 * nh, GDN_DK)
    cw3 = conv_w.reshape(CONV_W, 3 * nh, GDN_DK)
    cdec = [float(v) for v in np.exp(_RET_LOG_G)]
    c2 = lambda i: (0, 0)
    c3 = lambda i: (0, 0, 0)
    b3 = lambda i: (i, 0, 0)
    b4 = lambda i: (i, 0, 0, 0)
    oa, ob, sr, sg, sc = pl.pallas_call(
        functools.partial(_sample_step_body, cdec),
        grid=(ns // gs,),
        in_specs=[
            pl.BlockSpec((gs, 2 * rh, RET_DK), b3),
            pl.BlockSpec((gs, rh, RET_DV), b3),
            pl.BlockSpec((gs, rh, RET_DV), b3),
            pl.BlockSpec((gs, 3 * nh, GDN_DK), b3),
            pl.BlockSpec((gs, nh, GDN_DV), b3),
            pl.BlockSpec((gs, 2 * nh, 1), b3),
            pl.BlockSpec((gs, rh, RET_DK, RET_DV), b4),
            pl.BlockSpec((gs, nh, GDN_DK, GDN_DV), b4),
            pl.BlockSpec((gs, CONV_W - 1, 3 * nh, GDN_DK), b4),
            pl.BlockSpec((1, RET_DK), c2),
            pl.BlockSpec((1, RET_DK), c2),
            pl.BlockSpec((CONV_W, 3 * nh, GDN_DK), c3),
            pl.BlockSpec((nh, 1), c2),
            pl.BlockSpec((nh, 1), c2),
            pl.BlockSpec((rh, RET_DV), c2),
            pl.BlockSpec((1, GDN_DV), c2),
        ],
        out_specs=[
            pl.BlockSpec((gs, rh, RET_DV), b3),
            pl.BlockSpec((gs, nh, GDN_DV), b3),
            pl.BlockSpec((gs, rh, RET_DK, RET_DV), b4),
            pl.BlockSpec((gs, nh, GDN_DK, GDN_DV), b4),
            pl.BlockSpec((gs, CONV_W - 1, 3 * nh, GDN_DK), b4),
        ],
        out_shape=[
            jax.ShapeDtypeStruct((ns, rh, RET_DV), F32),
            jax.ShapeDtypeStruct((ns, nh, GDN_DV), F32),
            jax.ShapeDtypeStruct(state_ret.shape, F32),
            jax.ShapeDtypeStruct(state_gdn.shape, F32),
            jax.ShapeDtypeStruct(sc4.shape, F32),
        ],
        compiler_params=_params("parallel"),
        name="sample_step",
    )(qk3, rv3, rg3, x3, z3, bcol, state_ret, state_gdn, sc4, cos, sin, cw3, a_log.reshape(nh, 1),
      dt_bias.reshape(nh, 1), gn.reshape(rh, RET_DV), norm_g)
    return (oa.reshape(ns, rh * RET_DV).astype(BF16), ob.reshape(ns, nh * GDN_DV).astype(BF16), sr, sg,
            sc.reshape(state_conv.shape))


def _merge_body(x_ref, nmg_ref, wgt_ref, oa_ref, ob_ref, wa_ref, wb_ref, wo_ref, ng_ref, wq_ref, x1_ref, q_ref):
    x = x_ref[...]
    gates = jnp.dot(_rms(x, nmg_ref[...]).astype(BF16), wgt_ref[...], preferred_element_type=F32)
    ya = jnp.dot(oa_ref[...], wa_ref[...], preferred_element_type=F32)
    yb = jnp.dot(ob_ref[...], wb_ref[...], preferred_element_type=F32)
    merged = jax.nn.sigmoid(gates[:, :D_MODEL]) * ya + jax.nn.sigmoid(gates[:, D_MODEL:]) * yb
    x1 = x + _mm(merged, wo_ref[...])
    x1_ref[...] = x1
    q_ref[...] = _mm(_rms(x1, ng_ref[...]), wq_ref[...]).astype(BF16)


def _merge(x, mix_g, w_gates, oa, ob, wa, wb, wo, ng, wq, tm):
    m = x.shape[0]
    row = lambda i: (i, 0)
    c2 = lambda i: (0, 0)
    wspec = pl.BlockSpec((D_MODEL, D_MODEL), c2)
    return pl.pallas_call(
        _merge_body,
        grid=(m // tm,),
        in_specs=[
            pl.BlockSpec((tm, D_MODEL), row),
            pl.BlockSpec((1, D_MODEL), c2),
            pl.BlockSpec((D_MODEL, 2 * D_MODEL), c2),
            pl.BlockSpec((tm, D_MODEL), row),
            pl.BlockSpec((tm, D_MODEL), row),
            wspec, wspec, wspec,
            pl.BlockSpec((1, D_MODEL), c2),
            wspec,
        ],
        out_specs=[pl.BlockSpec((tm, D_MODEL), row), pl.BlockSpec((tm, D_MODEL), row)],
        out_shape=[jax.ShapeDtypeStruct((m, D_MODEL), F32), jax.ShapeDtypeStruct((m, D_MODEL), BF16)],
        compiler_params=_params("parallel"),
        name="merge",
    )(x, mix_g, w_gates, oa, ob, wa, wb, wo, ng, wq)


def _memkv_body(m_ref, g_ref, wk_ref, wv_ref, k_ref, v_ref, k4_ref, v4_ref):
    mn = _rms(m_ref[...], g_ref[...]).astype(BF16)
    k = jnp.dot(mn, wk_ref[...], preferred_element_type=F32)
    v = jnp.dot(mn, wv_ref[...], preferred_element_type=F32)
    k_ref[...] = k
    v_ref[...] = v
    for h in range(X_HEADS):
        k4_ref[:, h, :] = k[:, h * X_HD:(h + 1) * X_HD]
        v4_ref[:, h, :] = v[:, h * X_HD:(h + 1) * X_HD]


def _memkv(mem, g, wk, wv, tm=512):
    m = mem.shape[0]
    row = lambda i: (i, 0)
    row3 = lambda i: (i, 0, 0)
    c2 = lambda i: (0, 0)
    return pl.pallas_call(
        _memkv_body,
        grid=(m // tm,),
        in_specs=[pl.BlockSpec((tm, D_MODEL), row), pl.BlockSpec((1, D_MODEL), c2),
                  pl.BlockSpec((D_MODEL, D_MODEL), c2), pl.BlockSpec((D_MODEL, D_MODEL), c2)],
        out_specs=[pl.BlockSpec((tm, D_MODEL), row), pl.BlockSpec((tm, D_MODEL), row),
                   pl.BlockSpec((tm, X_HEADS, X_HD), row3), pl.BlockSpec((tm, X_HEADS, X_HD), row3)],
        out_shape=[jax.ShapeDtypeStruct((m, D_MODEL), F32)] * 2
        + [jax.ShapeDtypeStruct((m, X_HEADS, X_HD), F32)] * 2,
        compiler_params=_params("parallel"),
        name="memkv",
    )(mem, g, wk, wv)


def _xattn_body(q_ref, mk_ref, mv_ref, x1_ref, wo_ref, x2_ref):
    parts = []
    for h in range(X_HEADS):
        hs = slice(h * X_HD, (h + 1) * X_HD)
        s = _mm_nt(q_ref[0, :, hs], mk_ref[0, :, hs]) * (X_HD ** -0.5)
        p = jnp.exp(s - jnp.max(s, axis=-1, keepdims=True))
        o = _mm(p, mv_ref[0, :, hs]) / jnp.sum(p, axis=-1, keepdims=True)
        parts.append(o.astype(BF16))
    x2_ref[0] = x1_ref[0] + jnp.dot(jnp.concatenate(parts, axis=1), wo_ref[...], preferred_element_type=F32)


def _xattn_prompt(q3, mk3, mv3, x13, wo, tq=1024):
    b, t, _ = q3.shape
    tok = lambda i, j: (i, j, 0)
    mem = lambda i, j: (i, 0, 0)
    return pl.pallas_call(
        _xattn_body,
        grid=(b, t // tq),
        in_specs=[pl.BlockSpec((1, tq, D_MODEL), tok), pl.BlockSpec((1, N_MEM, D_MODEL), mem),
                  pl.BlockSpec((1, N_MEM, D_MODEL), mem), pl.BlockSpec((1, tq, D_MODEL), tok),
                  pl.BlockSpec((D_MODEL, D_MODEL), lambda i, j: (0, 0))],
        out_specs=pl.BlockSpec((1, tq, D_MODEL), tok),
        out_shape=jax.ShapeDtypeStruct((b, t, D_MODEL), F32),
        compiler_params=_params("parallel", "parallel"),
        name="xattn_prompt",
    )(q3, mk3, mv3, x13, wo)


def _xattn_sample_body(q_ref, mk_ref, mv_ref, o_ref):
    for g in range(q_ref.shape[0]):
        q = q_ref[g]
        s = jnp.sum(mk_ref[g] * q[None], axis=-1, keepdims=True) * (X_HD ** -0.5)
        p = jnp.exp(s - jnp.max(s, axis=0, keepdims=True))
        o_ref[g] = jnp.sum(p * mv_ref[g], axis=0) / jnp.sum(p, axis=0)


def _xattn_sample(q, mk4, mv4, gs=8):
    ns = q.shape[0]
    q3 = q.astype(F32).reshape(ns, X_HEADS, X_HD)
    row = lambda i: (i, 0, 0)
    mem = lambda i: (i, 0, 0, 0)
    return pl.pallas_call(
        _xattn_sample_body,
        grid=(ns // gs,),
        in_specs=[pl.BlockSpec((gs, X_HEADS, X_HD), row), pl.BlockSpec((gs, N_MEM, X_HEADS, X_HD), mem),
                  pl.BlockSpec((gs, N_MEM, X_HEADS, X_HD), mem)],
        out_specs=pl.BlockSpec((gs, X_HEADS, X_HD), row),
        out_shape=jax.ShapeDtypeStruct((ns, X_HEADS, X_HD), F32),
        compiler_params=_params("parallel"),
        name="xattn_sample",
    )(q3, mk4, mv4)


def _resid_mm_body(x_ref, a_ref, w_ref, o_ref):
    o_ref[...] = x_ref[...] + jnp.dot(a_ref[...], w_ref[...], preferred_element_type=F32)


def _resid_mm(x, a, w):
    m = x.shape[0]
    return pl.pallas_call(
        _resid_mm_body,
        out_shape=jax.ShapeDtypeStruct((m, D_MODEL), F32),
        compiler_params=pltpu.CompilerParams(vmem_limit_bytes=VMEM_LIMIT),
        name="resid_mm",
    )(x, a, w)


def _ffn_body(x_ref, ng_ref, wg_ref, wu_ref, wd_ref, nf_ref, y_ref):
    x = x_ref[...]
    h = _rms(x, ng_ref[...]).astype(BF16)
    gate = jnp.dot(h, wg_ref[...], preferred_element_type=F32)
    up = jnp.dot(h, wu_ref[...], preferred_element_type=F32)
    x3 = x + _mm(_silu(gate) * up, wd_ref[...])
    y_ref[...] = _rms(x3, nf_ref[...])


def _ffn(x, ng, wg, wu, wd, nf, tm):
    m = x.shape[0]
    dff = wg.shape[1]
    row = lambda i: (i, 0)
    c2 = lambda i: (0, 0)
    single = pl.Buffered(1)
    return pl.pallas_call(
        _ffn_body,
        grid=(m // tm,),
        in_specs=[pl.BlockSpec((tm, D_MODEL), row), pl.BlockSpec((1, D_MODEL), c2),
                  pl.BlockSpec((D_MODEL, dff), c2, pipeline_mode=single),
                  pl.BlockSpec((D_MODEL, dff), c2, pipeline_mode=single),
                  pl.BlockSpec((dff, D_MODEL), c2, pipeline_mode=single),
                  pl.BlockSpec((1, D_MODEL), c2)],
        out_specs=pl.BlockSpec((tm, D_MODEL), row),
        out_shape=jax.ShapeDtypeStruct((m, D_MODEL), F32),
        compiler_params=_params("parallel"),
        name="ffn",
    )(x, ng, wg, wu, wd, nf)


def _rope_tables(pos):
    half = RET_DK // 2
    inv = ROPE_BASE ** (-jnp.arange(half, dtype=F32) / half)
    ang = pos.astype(F32)[:, None] * inv[None, :]
    cos, sin = jnp.cos(ang), jnp.sin(ang)
    return jnp.concatenate([cos, cos], axis=-1), jnp.concatenate([-sin, sin], axis=-1)


def _pad_lanes(v, offset):
    return jnp.zeros((BA_PAD,), F32).at[offset:offset + v.shape[0]].set(v)


def kernel(x_prompt, x_sample, state_ret, state_gdn, state_conv, cache_mem_k, cache_mem_v, mem_prompt,
           norm_mix_g, w_in, ret_gn_g, w_branch_a, gdn_conv_w, gdn_a_log, gdn_dt_bias, gdn_norm_g,
           w_branch_b, w_out, norm_x_g, mem_norm_g, w_xq, w_xk, w_xv, w_xo, norm_ffn_g, w_gate, w_up,
           w_down, norm_final_g):
    depth = w_in.shape[0]
    assert depth == 1, "single-layer kernel"
    b, t, _ = x_prompt.shape
    ns = x_sample.shape[0]
    l = 0

    w = w_in[l]
    ba0, g0 = COL_GATE, COL_GATE + 2 * GDN_HEADS
    w_bf = w.astype(BF16)
    w_gates = w_bf[:, g0:]
    w_ba = jnp.pad(w[:, ba0:g0], ((0, 0), (0, BA_PAD - 2 * GDN_HEADS))).astype(BF16)
    w_bat = w_ba.T
    row = lambda v: v.reshape(1, -1)
    wa, wb, wo = w_branch_a[l].astype(BF16), w_branch_b[l].astype(BF16), w_out[l].astype(BF16)
    wq, wk, wv, wxo = w_xq[l].astype(BF16), w_xk[l].astype(BF16), w_xv[l].astype(BF16), w_xo[l].astype(BF16)
    wg, wu, wd = w_gate[l].astype(BF16), w_up[l].astype(BF16), w_down[l].astype(BF16)
    alog_r = _pad_lanes(gdn_a_log[l], GDN_HEADS).reshape(1, BA_PAD)
    dt_r = _pad_lanes(gdn_dt_bias[l], GDN_HEADS).reshape(1, BA_PAD)
    alog_c, dt_c = alog_r.reshape(BA_PAD, 1), dt_r.reshape(BA_PAD, 1)
    cos_p, sin_p = _rope_tables(jnp.arange(t))
    cos_s, sin_s = _rope_tables(PAST_LEN + jnp.arange(1))

    mix_g = row(norm_mix_g[l])
    xp = x_prompt.reshape(b * t, D_MODEL)
    oa_p, sr_p = _retention_prompt(x_prompt, mix_g, w_bf, cos_p, sin_p, row(ret_gn_g[l]))
    ob_p, sg_p, sc_p = _gdn_prompt(x_prompt, mix_g, w_bf, w_ba, w_bat, gdn_conv_w[l], alog_r, dt_r,
                                   alog_c, dt_c, row(gdn_norm_g[l]))
    x1_p, q_p = _merge(xp, mix_g, w_gates, oa_p.reshape(b * t, -1), ob_p.reshape(b * t, -1), wa, wb, wo,
                       row(norm_x_g[l]), wq, tm=512)
    mk_p, mv_p, mk4_p, mv4_p = _memkv(mem_prompt.reshape(b * N_MEM, D_MODEL), row(mem_norm_g[l]), wk, wv)
    x2_p = _xattn_prompt(q_p.reshape(b, t, D_MODEL), mk_p.reshape(b, N_MEM, D_MODEL),
                         mv_p.reshape(b, N_MEM, D_MODEL), x1_p.reshape(b, t, D_MODEL), wxo)
    y_p = _ffn(x2_p.reshape(b * t, D_MODEL), row(norm_ffn_g[l]), wg, wu, wd, row(norm_final_g), tm=512)

    xs = x_sample.reshape(ns, D_MODEL)
    proj_s, ba_s = _inproj(xs, mix_g, w_bf, COL_GATE, w_ba)
    oa_s, ob_s, sr_s, sg_s, sc_s = _sample_step(proj_s[:, :COL_GDN], proj_s[:, COL_GDN:], ba_s, state_ret[l],
                                                state_gdn[l], state_conv[l],
                                                cos_s, sin_s, gdn_conv_w[l], gdn_a_log[l], gdn_dt_bias[l],
                                                ret_gn_g[l], row(gdn_norm_g[l]))
    x1_s, q_s = _merge(xs, mix_g, w_gates, oa_s.reshape(ns, -1), ob_s.reshape(ns, -1), wa, wb, wo,
                       row(norm_x_g[l]), wq, tm=ns)
    o_s = _xattn_sample(q_s, cache_mem_k[l], cache_mem_v[l])
    x2_s = _resid_mm(x1_s, o_s.reshape(ns, D_MODEL).astype(BF16), wxo)
    y_s = _ffn(x2_s, row(norm_ffn_g[l]), wg, wu, wd, row(norm_final_g), tm=ns)

    return (y_p.reshape(b, t, D_MODEL), y_s.reshape(ns, 1, D_MODEL),
            sr_p[None], sg_p[None], sc_p[None],
            mk4_p.reshape(1, b, N_MEM, X_HEADS, X_HD), mv4_p.reshape(1, b, N_MEM, X_HEADS, X_HD),
            sr_s[None], sg_s[None], sc_s[None])
```

```python
import functools

import numpy as np
import jax
import jax.numpy as jnp
from jax import lax
from jax.experimental import pallas as pl
from jax.experimental.pallas import tpu as pltpu

F32 = jnp.float32
BF16 = jnp.bfloat16

D_MODEL = 1024
RET_HEADS, RET_DK, RET_DV = 4, 128, 256
GDN_HEADS, GDN_DK, GDN_DV = 8, 128, 128
CONV_W = 4
CONV_CH = 3 * GDN_HEADS * GDN_DK
N_MEM, X_HEADS, X_HD = 256, 4, 256
PAST_LEN = 16384
ROPE_BASE = 10000.0
EPS = 1e-6
GDN_CHUNK = 64

COL_RET = 0
COL_GDN = 3072
COL_GATE = 7168
BA_PAD = 128

VMEM_LIMIT = 56 * 1024 * 1024

NT_DIMS = (((1,), (1,)), ((), ()))
TN_DIMS = (((0,), (0,)), ((), ()))


def _mm(a, b):
    return jnp.dot(a.astype(BF16), b.astype(BF16), preferred_element_type=F32)


def _mm_nt(a, b):
    return lax.dot_general(a.astype(BF16), b.astype(BF16), NT_DIMS, preferred_element_type=F32)


def _mm_tn(a, b):
    return lax.dot_general(a.astype(BF16), b.astype(BF16), TN_DIMS, preferred_element_type=F32)


def _split3(x):
    hi = x.astype(BF16)
    r = x - hi.astype(F32)
    mid = r.astype(BF16)
    return hi, mid, (r - mid.astype(F32)).astype(BF16)


def _mm_sel(sel, x):
    return sum(jnp.dot(sel, p, preferred_element_type=F32) for p in _split3(x))


def _mm_sel_r(x, sel):
    return sum(jnp.dot(p, sel, preferred_element_type=F32) for p in _split3(x))


def _rms(x, g):
    return x * lax.rsqrt(jnp.mean(x * x, axis=-1, keepdims=True) + EPS) * g


def _silu(x):
    h = 0.5 * x
    return h + h * jnp.tanh(h)


def _softplus(x):
    return jnp.maximum(x, 0.0) + jnp.log1p(jnp.exp(-jnp.abs(x)))


def _params(*sem):
    return pltpu.CompilerParams(dimension_semantics=sem, vmem_limit_bytes=VMEM_LIMIT)


def _inproj_body(x_ref, g_ref, w_ref, wba_ref, o_ref, oba_ref, h_scr):
    @pl.when(pl.program_id(0) == 0)
    def _():
        hb = _rms(x_ref[...], g_ref[...]).astype(BF16)
        h_scr[...] = hb
        oba_ref[...] = jnp.dot(hb, wba_ref[...], preferred_element_type=F32)

    o_ref[...] = jnp.dot(h_scr[...], w_ref[...], preferred_element_type=F32)


def _inproj(x, g, w_bf, n, w_ba, tn=1024):
    m = x.shape[0]
    assert n % tn == 0 and n <= w_bf.shape[1]
    return pl.pallas_call(
        _inproj_body,
        grid=(n // tn,),
        in_specs=[
            pl.BlockSpec((m, D_MODEL), lambda j: (0, 0)),
            pl.BlockSpec((1, D_MODEL), lambda j: (0, 0)),
            pl.BlockSpec((D_MODEL, tn), lambda j: (0, j)),
            pl.BlockSpec((D_MODEL, BA_PAD), lambda j: (0, 0)),
        ],
        out_specs=[
            pl.BlockSpec((m, tn), lambda j: (0, j)),
            pl.BlockSpec((m, BA_PAD), lambda j: (0, 0)),
        ],
        out_shape=[
            jax.ShapeDtypeStruct((m, n), F32),
            jax.ShapeDtypeStruct((m, BA_PAD), F32),
        ],
        scratch_shapes=[pltpu.VMEM((m, D_MODEL), BF16)],
        compiler_params=_params("arbitrary"),
        name="inproj",
    )(x, g, w_bf, w_ba)


_RET_LOG_G = np.log1p(-np.exp2(-5.0 - np.arange(RET_HEADS, dtype=np.float64)))


def _ret_tables(c):
    idx = np.arange(c, dtype=np.float64)
    diff = idx[:, None] - idx[None, :]
    dmat = np.where(diff >= 0, np.exp(np.maximum(diff, 0.0)[None] * _RET_LOG_G[:, None, None]), 0.0)
    qdec = np.exp((idx + 1.0)[None, :] * _RET_LOG_G[:, None])
    kdec = np.exp((c - 1.0 - idx)[None, :] * _RET_LOG_G[:, None])
    lane = np.ones((1, 1, RET_DK))
    return (jnp.asarray(dmat, F32), jnp.asarray(qdec[:, :, None] * lane, F32),
            jnp.asarray(kdec[:, :, None] * lane, F32), [float(v) for v in np.exp(c * _RET_LOG_G)])


def _rot(x, cos, sin):
    return x * cos + pltpu.roll(x, RET_DK // 2, 1) * sin


def _group_norm_gate(o, gate, gn):
    mu = jnp.mean(o, axis=-1, keepdims=True)
    d = o - mu
    var = jnp.mean(d * d, axis=-1, keepdims=True)
    return _silu(gate) * (d * lax.rsqrt(var + EPS) * gn)


def _proj_tiles(hb, w_ref, dst, width=512):
    for c0 in range(0, w_ref.shape[1], width):
        dst[:, c0:c0 + width] = jnp.dot(hb, w_ref[:, c0:c0 + width], preferred_element_type=F32)


def _next_block(nt, nblocks):
    def index_map(i, j):
        n1 = jnp.minimum(i * nt + j + 1, nblocks - 1)
        return (n1 // nt, n1 % nt, 0)
    return index_map


def _ret_body(cdec, x0_ref, xn_ref, nmg_ref, w_ref, *rest):
    *consts, o_ref, s_out_ref, s_scr, pa, pb = rest
    t = pl.program_id(1)
    n = pl.program_id(0) * pl.num_programs(1) + t
    bufs = (pa, pb)

    @pl.when(t == 0)
    def _():
        s_scr[...] = jnp.zeros_like(s_scr)

    @pl.when(n == 0)
    def _():
        _proj_tiles(_rms(x0_ref[0], nmg_ref[...]).astype(BF16), w_ref, pa)

    for slot in range(2):
        @pl.when(n % 2 == slot)
        def _(slot=slot):
            _ret_step(cdec, t, xn_ref, nmg_ref, w_ref, bufs[slot], bufs[1 - slot], *consts, o_ref, s_out_ref, s_scr)


def _ret_step(cdec, t, xn_ref, nmg_ref, w_ref, proj, proj_next, cos_ref, sin_ref, dmat_ref, qdec_ref, kdec_ref,
              gn_ref, o_ref, s_out_ref, s_scr):
    qw = RET_HEADS * RET_DK
    _proj_tiles(_rms(xn_ref[0], nmg_ref[...]).astype(BF16), w_ref, proj_next)

    chunk = dmat_ref.shape[1]
    for c0 in range(0, proj.shape[0], chunk):
        rows = slice(c0, c0 + chunk)
        cos, sin = cos_ref[rows, :], sin_ref[rows, :]
        for h in range(RET_HEADS):
            qk = slice(h * RET_DK, (h + 1) * RET_DK)
            kk = slice(qw + h * RET_DK, qw + (h + 1) * RET_DK)
            vv = slice(h * RET_DV, (h + 1) * RET_DV)
            q = _rot(proj[rows, qk], cos, sin)
            k = _rot(proj[rows, kk], cos, sin) * (RET_DK ** -0.5)
            v = proj[rows, 2 * qw + h * RET_DV:2 * qw + (h + 1) * RET_DV]
            gate = proj[rows, 2 * qw + (RET_HEADS + h) * RET_DV:2 * qw + (RET_HEADS + h + 1) * RET_DV]
            s = s_scr[h]
            scores = _mm_nt(q, k) * dmat_ref[h]
            o = _mm(scores, v) + _mm(q * qdec_ref[h], s)
            s_scr[h] = cdec[h] * s + _mm_tn(k * kdec_ref[h], v)
            o_ref[0, rows, vv] = _group_norm_gate(o, gate, gn_ref[:, vv]).astype(BF16)

    @pl.when(t == pl.num_programs(1) - 1)
    def _():
        s_out_ref[0] = s_scr[...]


def _retention_prompt(x3, mix_g, w_rg, cos, sin, gn, tb=512, chunk=256):
    b, t, _ = x3.shape
    nt = t // tb
    dmat, qdec, kdec, cdec = _ret_tables(chunk)
    vw = RET_HEADS * RET_DV
    const3 = lambda i, j: (0, 0, 0)
    return pl.pallas_call(
        functools.partial(_ret_body, cdec),
        grid=(b, nt),
        in_specs=[
            pl.BlockSpec((1, tb, D_MODEL), const3),
            pl.BlockSpec((1, tb, D_MODEL), _next_block(nt, b * nt)),
            pl.BlockSpec((1, D_MODEL), lambda i, j: (0, 0)),
            pl.BlockSpec((D_MODEL, COL_GDN), lambda i, j: (0, COL_RET // COL_GDN), pipeline_mode=pl.Buffered(1)),
            pl.BlockSpec((tb, RET_DK), lambda i, j: (j, 0)),
            pl.BlockSpec((tb, RET_DK), lambda i, j: (j, 0)),
            pl.BlockSpec((RET_HEADS, chunk, chunk), const3),
            pl.BlockSpec((RET_HEADS, chunk, RET_DK), const3),
            pl.BlockSpec((RET_HEADS, chunk, RET_DK), const3),
            pl.BlockSpec((1, vw), lambda i, j: (0, 0)),
        ],
        out_specs=[
            pl.BlockSpec((1, tb, vw), lambda i, j: (i, j, 0)),
            pl.BlockSpec((1, RET_HEADS, RET_DK, RET_DV), lambda i, j: (i, 0, 0, 0)),
        ],
        out_shape=[
            jax.ShapeDtypeStruct((b, t, vw), BF16),
            jax.ShapeDtypeStruct((b, RET_HEADS, RET_DK, RET_DV), F32),
        ],
        scratch_shapes=[pltpu.VMEM((RET_HEADS, RET_DK, RET_DV), F32)]
        + 2 * [pltpu.VMEM((tb, COL_GDN), F32)],
        compiler_params=_params("arbitrary", "arbitrary"),
        name="retention_prompt",
    )(x3, x3, mix_g, w_rg, cos, sin, dmat, qdec, kdec, gn)


def _gdn_tables(tb):
    idx = np.arange(tb)
    same = (idx[:, None] // GDN_CHUNK) == (idx[None, :] // GDN_CHUNK)
    lower = same & (idx[:, None] >= idx[None, :])
    nchunk = tb // GDN_CHUNK
    chunk_sel = np.repeat((idx[:, None] // GDN_CHUNK) == np.arange(nchunk)[None, :], 128, axis=1)
    grp = 2 * GDN_CHUNK
    il = (idx % grp)[:, None]
    jl = np.arange(grp)[None, :]
    bias = np.where((il // GDN_CHUNK == jl // GDN_CHUNK) & (il >= jl), 0.0, -1e30)
    eye = (il == jl).astype(np.float64)
    return (jnp.asarray(lower, BF16), jnp.asarray(lower.T, BF16), jnp.asarray(same, BF16),
            jnp.asarray(chunk_sel, BF16), jnp.asarray(bias, F32), jnp.asarray(eye - 1.0, F32),
            jnp.asarray(eye, BF16))


def _gdn_prepare(x_ref, first_of_seq, nmg_ref, wqkv_ref, wz_ref, wba_ref, wbat_ref, cw_ref, xr, cs_buf, zs, bas,
                 bats, tb):
    st = {}

    def start():
        xr[0:8, :] = jnp.where(first_of_seq, 0.0, xr[tb:tb + 8, :])
        st["hb"] = _rms(x_ref[0], nmg_ref[...]).astype(BF16)

    def tile(w_ref, dst, c0, width=512):
        def run():
            dst[:, c0:c0 + width] = jnp.dot(st["hb"], w_ref[:, c0:c0 + width], preferred_element_type=F32)
        return run

    def logits():
        bas[...] = jnp.dot(st["hb"], wba_ref[...], preferred_element_type=F32)
        bats[...] = lax.dot_general(wbat_ref[...], st["hb"], NT_DIMS, preferred_element_type=F32)

    def conv(c0):
        def run():
            cs = slice(c0, c0 + GDN_DK)
            acc = xr[5:5 + tb, cs] * cw_ref[0:1, cs]
            for i in range(1, CONV_W):
                acc = acc + xr[5 + i:5 + i + tb, cs] * cw_ref[i:i + 1, cs]
            cs_buf[:, cs] = _silu(acc)
        return run

    mxu_items = ([start] + [tile(wqkv_ref, xr.at[8:8 + tb], c0) for c0 in range(0, CONV_CH, 512)]
                 + [tile(wz_ref, zs, c0) for c0 in range(0, wz_ref.shape[1], 512)] + [logits])
    vpu_items = [conv(c0) for c0 in range(0, CONV_CH, GDN_DK)]
    return mxu_items, vpu_items


class _Background:
    def __init__(self, items):
        self._items = list(items)

    def __call__(self, count):
        for item in self._items[:count]:
            item()
        del self._items[:count]

    def drain(self):
        self(len(self._items))


def _gdn_body(tb, x0_ref, xn_ref, nmg_ref, wqkv_ref, wz_ref, wba_ref, wbat_ref, cw_ref, *rest):
    *consts, o_ref, s_out_ref, conv_out_ref, s_scr, xr, ca, za, baa, bata, cb, zb, bab, batb = rest
    t = pl.program_id(1)
    nt = pl.num_programs(1)
    n = pl.program_id(0) * nt + t
    weights = (nmg_ref, wqkv_ref, wz_ref, wba_ref, wbat_ref, cw_ref)
    bufs = ((ca, za, baa, bata), (cb, zb, bab, batb))

    @pl.when(t == 0)
    def _():
        s_scr[...] = jnp.zeros_like(s_scr)

    @pl.when(n == 0)
    def _():
        mxu_items, vpu_items = _gdn_prepare(x0_ref, True, *weights, xr, *bufs[0], tb)
        _Background(mxu_items + vpu_items).drain()

    for slot in range(2):
        @pl.when(n % 2 == slot)
        def _(slot=slot):
            mxu_items, vpu_items = _gdn_prepare(xn_ref, t == nt - 1, *weights, xr, *bufs[1 - slot], tb)
            _gdn_step(tb, t, bufs[slot], *consts, o_ref, s_out_ref, s_scr,
                      _Background(mxu_items), _Background(vpu_items))

    @pl.when(t == nt - 2)
    def _():
        conv_out_ref[0] = xr[tb + 5:tb + 8, :]


def _gdn_step(tb, t, cur, alog_r_ref, dt_r_ref, alog_c_ref, dt_c_ref, ng_ref,
              lbd_ref, ubd_ref, obd_ref, csel_ref, bias_ref, offd_ref, eye_ref, o_ref, s_out_ref, s_scr,
              bg_mxu, bg_vpu):
    nchunk = tb // GDN_CHUNK
    hk = GDN_HEADS * GDN_DK
    c_scr, zz, ba_ref, bat_ref = cur
    ba, bat = ba_ref[...], bat_ref[...]

    bg_mxu(2)
    beta_c = jax.nn.sigmoid(ba)
    g_c = -jnp.exp(alog_r_ref[...]) * _softplus(ba + dt_r_ref[...])
    g_r = -jnp.exp(alog_c_ref[...]) * _softplus(bat + dt_c_ref[...])
    gc_c = _mm_sel(lbd_ref[...], g_c)
    gt_c = _mm_sel(obd_ref[...], g_c)
    gc_r = _mm_sel_r(g_r, ubd_ref[...])
    gt_l = _mm_sel_r(g_r, csel_ref[...])

    heads = range(GDN_HEADS)
    grp = 2 * GDN_CHUNK
    groups = [slice(p * grp, (p + 1) * grp) for p in range(tb // grp)]

    def grp_dot(a, b):
        return jnp.concatenate([jnp.dot(a[g], b[g], preferred_element_type=F32) for g in groups], axis=0)

    def grp_dot_nt(a, b):
        return jnp.concatenate([lax.dot_general(a[g], b[g], NT_DIMS, preferred_element_type=F32)
                                for g in groups], axis=0)

    bias = bias_ref[...]
    offdiag = offd_ref[...]
    eye_b = eye_ref[...]
    qs, ks, gammas, pbs, rhss, qgs, khs = [], [], [], [], [], [], []
    for h in heads:
        q = c_scr[:, h * GDN_DK:(h + 1) * GDN_DK]
        k = c_scr[:, hk + h * GDN_DK:hk + (h + 1) * GDN_DK]
        v = c_scr[:, 2 * hk + h * GDN_DV:2 * hk + (h + 1) * GDN_DV]
        q = q * lax.rsqrt(jnp.sum(q * q, axis=-1, keepdims=True) + EPS) * (GDN_DK ** -0.5)
        k = k * lax.rsqrt(jnp.sum(k * k, axis=-1, keepdims=True) + EPS)
        beta = beta_c[:, h:h + 1]
        gcc = gc_c[:, 8 + h:9 + h]
        gtc = gt_c[:, 8 + h:9 + h]
        gcr = gc_r[8 + h:9 + h, :]
        dg = jnp.concatenate([gcc[g] - gcr[:, g] for g in groups], axis=0)
        gamma = jnp.exp(dg + bias)
        kbeta = k * beta
        kb = k.astype(BF16)
        pbs.append((grp_dot_nt(kbeta.astype(BF16), kb) * (gamma * offdiag)).astype(BF16))
        eg = jnp.exp(gcc)
        rhss.append(jnp.concatenate([v * beta, kbeta * eg], axis=1))
        qs.append(q.astype(BF16))
        ks.append(kb)
        gammas.append(gamma)
        qgs.append(q * eg)
        khs.append(k * jnp.exp(gtc - gcc))
        bg_mxu(1)
        bg_vpu(1)
    bg_mxu.drain()
    bg_vpu(4)

    def solve(_, carry):
        pb = pbs
        tbs = [pb[h] + eye_b for h in heads]
        for lvl in range(5):
            pb = [grp_dot(pb[h], pb[h]).astype(BF16) for h in heads]
            bg_vpu(1)
            tnew = [grp_dot(tbs[h], pb[h] + eye_b) for h in heads]
            bg_vpu(1)
            tbs = [x.astype(BF16) for x in tnew]

        us, ws, qks = [], [], []
        for h in heads:
            uw = rhss[h] + grp_dot(tbs[h] - eye_b, rhss[h].astype(BF16))
            us.append(uw[:, :GDN_DV])
            ws.append(uw[:, GDN_DV:])
            qks.append((grp_dot_nt(qs[h], ks[h]) * gammas[h]).astype(BF16))
        bg_vpu(2)

        s = [s_scr[h] for h in heads]
        vn_parts = [[] for _ in heads]
        qs_parts = [[] for _ in heads]
        for c in range(nchunk):
            rows = slice(c * GDN_CHUNK, (c + 1) * GDN_CHUNK)
            for h in heads:
                wq = _mm(jnp.concatenate([ws[h][rows], qgs[h][rows]], axis=0), s[h])
                vn = us[h][rows] - wq[:GDN_CHUNK]
                qs_parts[h].append(wq[GDN_CHUNK:])
                vn_parts[h].append(vn)
                decay = jnp.exp(gt_l[8 + h:9 + h, c * 128:(c + 1) * 128])
                s[h] = decay * s[h] + _mm_tn(khs[h][rows], vn)
            bg_vpu(1)
        bg_vpu.drain()
        for h in heads:
            hs = slice(h * GDN_DV, (h + 1) * GDN_DV)
            s_scr[h] = s[h]
            vn = jnp.concatenate(vn_parts[h], axis=0).astype(BF16)
            o = jnp.concatenate(qs_parts[h], axis=0) + grp_dot(qks[h], vn)
            o_ref[0, :, hs] = (_rms(o, ng_ref[...]) * _silu(zz[:, hs])).astype(BF16)
        return carry

    lax.fori_loop(0, jnp.minimum(t + 1, 1), solve, 0)

    @pl.when(t == pl.num_programs(1) - 1)
    def _():
        s_out_ref[0] = s_scr[...]


def _gdn_prompt(x3, mix_g, w_rg, w_ba, w_bat, conv_w, alog_r, dt_r, alog_c, dt_c, norm_g, tb=128):
    b, t, _ = x3.shape
    nt = t // tb
    assert t % tb == 0 and nt >= 2, "a sequence's last block must be prepared during one of its own steps"
    lbd, ubd, obd, csel, bias, offdiag, eye = _gdn_tables(tb)
    vw = GDN_HEADS * GDN_DV
    c2 = lambda i, j: (0, 0)
    single = pl.Buffered(1)
    return pl.pallas_call(
        functools.partial(_gdn_body, tb),
        grid=(b, nt),
        in_specs=[
            pl.BlockSpec((1, tb, D_MODEL), lambda i, j: (0, 0, 0)),
            pl.BlockSpec((1, tb, D_MODEL), _next_block(nt, b * nt)),
            pl.BlockSpec((1, D_MODEL), c2),
            pl.BlockSpec((D_MODEL, CONV_CH), lambda i, j: (0, COL_GDN // CONV_CH), pipeline_mode=single),
            pl.BlockSpec((D_MODEL, vw), lambda i, j: (0, (COL_GDN + CONV_CH) // vw), pipeline_mode=single),
            pl.BlockSpec((D_MODEL, BA_PAD), c2),
            pl.BlockSpec((BA_PAD, D_MODEL), c2),
            pl.BlockSpec((CONV_W, CONV_CH), c2),
            pl.BlockSpec((1, BA_PAD), c2),
            pl.BlockSpec((1, BA_PAD), c2),
            pl.BlockSpec((BA_PAD, 1), c2),
            pl.BlockSpec((BA_PAD, 1), c2),
            pl.BlockSpec((1, GDN_DV), c2),
            pl.BlockSpec((tb, tb), c2),
            pl.BlockSpec((tb, tb), c2),
            pl.BlockSpec((tb, tb), c2),
            pl.BlockSpec((tb, (tb // GDN_CHUNK) * 128), c2),
            pl.BlockSpec((tb, 2 * GDN_CHUNK), c2),
            pl.BlockSpec((tb, 2 * GDN_CHUNK), c2),
            pl.BlockSpec((tb, 2 * GDN_CHUNK), c2),
        ],
        out_specs=[
            pl.BlockSpec((1, tb, vw), lambda i, j: (i, j, 0)),
            pl.BlockSpec((1, GDN_HEADS, GDN_DK, GDN_DV), lambda i, j: (i, 0, 0, 0)),
            pl.BlockSpec((1, CONV_W - 1, CONV_CH), lambda i, j: (i, 0, 0)),
        ],
        out_shape=[
            jax.ShapeDtypeStruct((b, t, vw), BF16),
            jax.ShapeDtypeStruct((b, GDN_HEADS, GDN_DK, GDN_DV), F32),
            jax.ShapeDtypeStruct((b, CONV_W - 1, CONV_CH), F32),
        ],
        scratch_shapes=[pltpu.VMEM((GDN_HEADS, GDN_DK, GDN_DV), F32), pltpu.VMEM((tb + 8, CONV_CH), F32)]
        + 2 * [pltpu.VMEM((tb, CONV_CH), F32), pltpu.VMEM((tb, vw), F32),
               pltpu.VMEM((tb, BA_PAD), F32), pltpu.VMEM((BA_PAD, tb), F32)],
        compiler_params=_params("arbitrary", "arbitrary"),
        name="gdn_prompt",
    )(x3, x3, mix_g, w_rg, w_rg, w_ba, w_bat, conv_w, alog_r, dt_r, alog_c, dt_c, norm_g, lbd, ubd, obd, csel,
      bias, offdiag, eye)


def _block_diag_rows(x, nblk):
    row = lax.broadcasted_iota(jnp.int32, (8, nblk * 128), 0)
    blk = lax.broadcasted_iota(jnp.int32, (8, nblk * 128), 1) // 128
    return jnp.where(row == blk, jnp.concatenate([x] * nblk, axis=1), 0.0)


def _sample_step_body(cdec, qk_ref, rv_ref, rg_ref, x_ref, z_ref, bcol_ref, sr_ref, sg_ref, sc_ref, cos_ref, sin_ref,
                      cw_ref, alog_ref, dt_ref, gn_ref, ng_ref,
                      oa_ref, ob_ref, sr_out_ref, sg_out_ref, sc_out_ref):
    nh = GDN_HEADS
    cos, sin = cos_ref[...], sin_ref[...]
    gam = jnp.where(lax.broadcasted_iota(jnp.int32, (8, 1), 0) == 0, cdec[0], 0.0)
    for h in range(1, RET_HEADS):
        gam = jnp.where(lax.broadcasted_iota(jnp.int32, (8, 1), 0) == h, cdec[h], gam)
    zero4 = jnp.zeros((RET_HEADS, RET_DV), F32)
    for g in range(x_ref.shape[0]):
        x_new = x_ref[g]
        buf = sc_ref[g]
        conv = x_new * cw_ref[CONV_W - 1]
        for i in range(CONV_W - 1):
            conv = conv + buf[i] * cw_ref[i]
        conv = _silu(conv)
        sc_out_ref[g, 0] = buf[1]
        sc_out_ref[g, 1] = buf[2]
        sc_out_ref[g, 2] = x_new
        q, k, v = conv[0:nh], conv[nh:2 * nh], conv[2 * nh:3 * nh]
        q = q * lax.rsqrt(jnp.sum(q * q, axis=-1, keepdims=True) + EPS) * (GDN_DK ** -0.5)
        k = k * lax.rsqrt(jnp.sum(k * k, axis=-1, keepdims=True) + EPS)
        col = bcol_ref[g]
        beta = jax.nn.sigmoid(col[0:nh])
        eg = jnp.exp(-jnp.exp(alog_ref[...]) * _softplus(col[nh:2 * nh] + dt_ref[...]))
        kbd, qbd = _block_diag_rows(k, nh), _block_diag_rows(q, nh)
        s_flat = sg_ref[g].reshape(nh * GDN_DK, GDN_DV)
        kq_s = _mm(jnp.concatenate([kbd, qbd], axis=0), s_flat)
        vn = beta * (v - eg * kq_s[0:nh])
        o = eg * kq_s[nh:2 * nh] + jnp.sum(q * k, axis=-1, keepdims=True) * vn
        ob_ref[g] = _rms(o, ng_ref[...]) * _silu(z_ref[g])
        upd = _mm_tn(kbd, vn)
        eg_l = jnp.broadcast_to(eg, (nh, GDN_DV))
        for h in range(nh):
            sg_out_ref[g, h] = eg_l[h:h + 1] * sg_ref[g, h] + upd[h * GDN_DK:(h + 1) * GDN_DK]

        qk = _rot(qk_ref[g], cos, sin)
        rq = jnp.concatenate([qk[0:RET_HEADS], qk[0:RET_HEADS]], axis=0)
        rk = jnp.concatenate([qk[RET_HEADS:], qk[RET_HEADS:]], axis=0) * (RET_DK ** -0.5)
        rv = jnp.concatenate([rv_ref[g], zero4], axis=0)
        qbd_r, kbd_r = _block_diag_rows(rq, RET_HEADS), _block_diag_rows(rk, RET_HEADS)
        sr_flat = sr_ref[g].reshape(RET_HEADS * RET_DK, RET_DV)
        q_s = _mm(qbd_r, sr_flat)
        o_r = gam * q_s + jnp.sum(rq * rk, axis=-1, keepdims=True) * rv
        oa_ref[g] = _group_norm_gate(o_r[0:RET_HEADS], rg_ref[g], gn_ref[...])
        upd_r = _mm_tn(kbd_r, rv)
        for h in range(RET_HEADS):
            sr_out_ref[g, h] = cdec[h] * sr_ref[g, h] + upd_r[h * RET_DK:(h + 1) * RET_DK]


def _sample_step(proj_ret, proj_gdn, ba, state_ret, state_gdn, state_conv, cos, sin, conv_w, a_log, dt_bias, gn,
                 norm_g, gs=8):
    ns = proj_ret.shape[0]
    nh, rh = GDN_HEADS, RET_HEADS
    nq = rh * RET_DK
    qk3 = proj_ret[:, :2 * nq].reshape(ns, 2 * rh, RET_DK)
    rv3 = proj_ret[:, 2 * nq:2 * nq + rh * RET_DV].reshape(ns, rh, RET_DV)
    rg3 = proj_ret[:, 2 * nq + rh * RET_DV:].reshape(ns, rh, RET_DV)
    x3 = proj_gdn[:, :CONV_CH].reshape(ns, 3 * nh, GDN_DK)
    z3 = proj_gdn[:, CONV_CH:].reshape(ns, nh, GDN_DV)
    bcol = ba[:, :2 * nh].reshape(ns, 2 * nh, 1)
    sc4 = state_conv.reshape(ns, CONV_W - 1, 3 * nh, GDN_DK)
    cw3 = conv_w.reshape(CONV_W, 3 * nh, GDN_DK)
    cdec = [float(v) for v in np.exp(_RET_LOG_G)]
    c2 = lambda i: (0, 0)
    c3 = lambda i: (0, 0, 0)
    b3 = lambda i: (i, 0, 0)
    b4 = lambda i: (i, 0, 0, 0)
    oa, ob, sr, sg, sc = pl.pallas_call(
        functools.partial(_sample_step_body, cdec),
        grid=(ns // gs,),
        in_specs=[
            pl.BlockSpec((gs, 2 * rh, RET_DK), b3),
            pl.BlockSpec((gs, rh, RET_DV), b3),
            pl.BlockSpec((gs, rh, RET_DV), b3),
            pl.BlockSpec((gs, 3 * nh, GDN_DK), b3),
            pl.BlockSpec((gs, nh, GDN_DV), b3),
            pl.BlockSpec((gs, 2 * nh, 1), b3),
            pl.BlockSpec((gs, rh, RET_DK, RET_DV), b4),
            pl.BlockSpec((gs, nh, GDN_DK, GDN_DV), b4),
            pl.BlockSpec((gs, CONV_W - 1, 3 * nh, GDN_DK), b4),
            pl.BlockSpec((1, RET_DK), c2),
            pl.BlockSpec((1, RET_DK), c2),
            pl.BlockSpec((CONV_W, 3 * nh, GDN_DK), c3),
            pl.BlockSpec((nh, 1), c2),
            pl.BlockSpec((nh, 1), c2),
            pl.BlockSpec((rh, RET_DV), c2),
            pl.BlockSpec((1, GDN_DV), c2),
        ],
        out_specs=[
            pl.BlockSpec((gs, rh, RET_DV), b3),
            pl.BlockSpec((gs, nh, GDN_DV), b3),
            pl.BlockSpec((gs, rh, RET_DK, RET_DV), b4),
            pl.BlockSpec((gs, nh, GDN_DK, GDN_DV), b4),
            pl.BlockSpec((gs, CONV_W - 1, 3 * nh, GDN_DK), b4),
        ],
        out_shape=[
            jax.ShapeDtypeStruct((ns, rh, RET_DV), F32),
            jax.ShapeDtypeStruct((ns, nh, GDN_DV), F32),
            jax.ShapeDtypeStruct(state_ret.shape, F32),
            jax.ShapeDtypeStruct(state_gdn.shape, F32),
            jax.ShapeDtypeStruct(sc4.shape, F32),
        ],
        compiler_params=_params("parallel"),
        name="sample_step",
    )(qk3, rv3, rg3, x3, z3, bcol, state_ret, state_gdn, sc4, cos, sin, cw3, a_log.reshape(nh, 1),
      dt_bias.reshape(nh, 1), gn.reshape(rh, RET_DV), norm_g)
    return (oa.reshape(ns, rh * RET_DV).astype(BF16), ob.reshape(ns, nh * GDN_DV).astype(BF16), sr, sg,
            sc.reshape(state_conv.shape))


def _merge_body(x_ref, nmg_ref, wgt_ref, oa_ref, ob_ref, wa_ref, wb_ref, wo_ref, ng_ref, wq_ref, x1_ref, q_ref):
    x = x_ref[...]
    gates = jnp.dot(_rms(x, nmg_ref[...]).astype(BF16), wgt_ref[...], preferred_element_type=F32)
    ya = jnp.dot(oa_ref[...], wa_ref[...], preferred_element_type=F32)
    yb = jnp.dot(ob_ref[...], wb_ref[...], preferred_element_type=F32)
    merged = jax.nn.sigmoid(gates[:, :D_MODEL]) * ya + jax.nn.sigmoid(gates[:, D_MODEL:]) * yb
    x1 = x + _mm(merged, wo_ref[...])
    x1_ref[...] = x1
    q_ref[...] = _mm(_rms(x1, ng_ref[...]), wq_ref[...]).astype(BF16)


def _merge(x, mix_g, w_gates, oa, ob, wa, wb, wo, ng, wq, tm):
    m = x.shape[0]
    row = lambda i: (i, 0)
    c2 = lambda i: (0, 0)
    wspec = pl.BlockSpec((D_MODEL, D_MODEL), c2)
    return pl.pallas_call(
        _merge_body,
        grid=(m // tm,),
        in_specs=[
            pl.BlockSpec((tm, D_MODEL), row),
            pl.BlockSpec((1, D_MODEL), c2),
            pl.BlockSpec((D_MODEL, 2 * D_MODEL), c2),
            pl.BlockSpec((tm, D_MODEL), row),
            pl.BlockSpec((tm, D_MODEL), row),
            wspec, wspec, wspec,
            pl.BlockSpec((1, D_MODEL), c2),
            wspec,
        ],
        out_specs=[pl.BlockSpec((tm, D_MODEL), row), pl.BlockSpec((tm, D_MODEL), row)],
        out_shape=[jax.ShapeDtypeStruct((m, D_MODEL), F32), jax.ShapeDtypeStruct((m, D_MODEL), BF16)],
        compiler_params=_params("parallel"),
        name="merge",
    )(x, mix_g, w_gates, oa, ob, wa, wb, wo, ng, wq)


def _memkv_body(m_ref, g_ref, wk_ref, wv_ref, k_ref, v_ref, k4_ref, v4_ref):
    mn = _rms(m_ref[...], g_ref[...]).astype(BF16)
    k = jnp.dot(mn, wk_ref[...], preferred_element_type=F32)
    v = jnp.dot(mn, wv_ref[...], preferred_element_type=F32)
    k_ref[...] = k
    v_ref[...] = v
    for h in range(X_HEADS):
        k4_ref[:, h, :] = k[:, h * X_HD:(h + 1) * X_HD]
        v4_ref[:, h, :] = v[:, h * X_HD:(h + 1) * X_HD]


def _memkv(mem, g, wk, wv, tm=512):
    m = mem.shape[0]
    row = lambda i: (i, 0)
    row3 = lambda i: (i, 0, 0)
    c2 = lambda i: (0, 0)
    return pl.pallas_call(
        _memkv_body,
        grid=(m // tm,),
        in_specs=[pl.BlockSpec((tm, D_MODEL), row), pl.BlockSpec((1, D_MODEL), c2),
                  pl.BlockSpec((D_MODEL, D_MODEL), c2), pl.BlockSpec((D_MODEL, D_MODEL), c2)],
        out_specs=[pl.BlockSpec((tm, D_MODEL), row), pl.BlockSpec((tm, D_MODEL), row),
                   pl.BlockSpec((tm, X_HEADS, X_HD), row3), pl.BlockSpec((tm, X_HEADS, X_HD), row3)],
        out_shape=[jax.ShapeDtypeStruct((m, D_MODEL), F32)] * 2
        + [jax.ShapeDtypeStruct((m, X_HEADS, X_HD), F32)] * 2,
        compiler_params=_params("parallel"),
        name="memkv",
    )(mem, g, wk, wv)


def _xattn_body(q_ref, mk_ref, mv_ref, x1_ref, wo_ref, x2_ref):
    parts = []
    for h in range(X_HEADS):
        hs = slice(h * X_HD, (h + 1) * X_HD)
        s = _mm_nt(q_ref[0, :, hs], mk_ref[0, :, hs]) * (X_HD ** -0.5)
        p = jnp.exp(s - jnp.max(s, axis=-1, keepdims=True))
        o = _mm(p, mv_ref[0, :, hs]) / jnp.sum(p, axis=-1, keepdims=True)
        parts.append(o.astype(BF16))
    x2_ref[0] = x1_ref[0] + jnp.dot(jnp.concatenate(parts, axis=1), wo_ref[...], preferred_element_type=F32)


def _xattn_prompt(q3, mk3, mv3, x13, wo, tq=1024):
    b, t, _ = q3.shape
    tok = lambda i, j: (i, j, 0)
    mem = lambda i, j: (i, 0, 0)
    return pl.pallas_call(
        _xattn_body,
        grid=(b, t // tq),
        in_specs=[pl.BlockSpec((1, tq, D_MODEL), tok), pl.BlockSpec((1, N_MEM, D_MODEL), mem),
                  pl.BlockSpec((1, N_MEM, D_MODEL), mem), pl.BlockSpec((1, tq, D_MODEL), tok),
                  pl.BlockSpec((D_MODEL, D_MODEL), lambda i, j: (0, 0))],
        out_specs=pl.BlockSpec((1, tq, D_MODEL), tok),
        out_shape=jax.ShapeDtypeStruct((b, t, D_MODEL), F32),
        compiler_params=_params("parallel", "parallel"),
        name="xattn_prompt",
    )(q3, mk3, mv3, x13, wo)


def _xattn_sample_body(q_ref, mk_ref, mv_ref, o_ref):
    for g in range(q_ref.shape[0]):
        q = q_ref[g]
        s = jnp.sum(mk_ref[g] * q[None], axis=-1, keepdims=True) * (X_HD ** -0.5)
        p = jnp.exp(s - jnp.max(s, axis=0, keepdims=True))
        o_ref[g] = jnp.sum(p * mv_ref[g], axis=0) / jnp.sum(p, axis=0)


def _xattn_sample(q, mk4, mv4, gs=8):
    ns = q.shape[0]
    q3 = q.astype(F32).reshape(ns, X_HEADS, X_HD)
    row = lambda i: (i, 0, 0)
    mem = lambda i: (i, 0, 0, 0)
    return pl.pallas_call(
        _xattn_sample_body,
        grid=(ns // gs,),
        in_specs=[pl.BlockSpec((gs, X_HEADS, X_HD), row), pl.BlockSpec((gs, N_MEM, X_HEADS, X_HD), mem),
                  pl.BlockSpec((gs, N_MEM, X_HEADS, X_HD), mem)],
        out_specs=pl.BlockSpec((gs, X_HEADS, X_HD), row),
        out_shape=jax.ShapeDtypeStruct((ns, X_HEADS, X_HD), F32),
        compiler_params=_params("parallel"),
        name="xattn_sample",
    )(q3, mk4, mv4)


def _resid_mm_body(x_ref, a_ref, w_ref, o_ref):
    o_ref[...] = x_ref[...] + jnp.dot(a_ref[...], w_ref[...], preferred_element_type=F32)


def _resid_mm(x, a, w):
    m = x.shape[0]
    return pl.pallas_call(
        _resid_mm_body,
        out_shape=jax.ShapeDtypeStruct((m, D_MODEL), F32),
        compiler_params=pltpu.CompilerParams(vmem_limit_bytes=VMEM_LIMIT),
        name="resid_mm",
    )(x, a, w)


def _ffn_body(x_ref, ng_ref, wg_ref, wu_ref, wd_ref, nf_ref, y_ref):
    x = x_ref[...]
    h = _rms(x, ng_ref[...]).astype(BF16)
    gate = jnp.dot(h, wg_ref[...], preferred_element_type=F32)
    up = jnp.dot(h, wu_ref[...], preferred_element_type=F32)
    x3 = x + _mm(_silu(gate) * up, wd_ref[...])
    y_ref[...] = _rms(x3, nf_ref[...])


def _ffn(x, ng, wg, wu, wd, nf, tm):
    m = x.shape[0]
    dff = wg.shape[1]
    row = lambda i: (i, 0)
    c2 = lambda i: (0, 0)
    single = pl.Buffered(1)
    return pl.pallas_call(
        _ffn_body,
        grid=(m // tm,),
        in_specs=[pl.BlockSpec((tm, D_MODEL), row), pl.BlockSpec((1, D_MODEL), c2),
                  pl.BlockSpec((D_MODEL, dff), c2, pipeline_mode=single),
                  pl.BlockSpec((D_MODEL, dff), c2, pipeline_mode=single),
                  pl.BlockSpec((dff, D_MODEL), c2, pipeline_mode=single),
                  pl.BlockSpec((1, D_MODEL), c2)],
        out_specs=pl.BlockSpec((tm, D_MODEL), row),
        out_shape=jax.ShapeDtypeStruct((m, D_MODEL), F32),
        compiler_params=_params("parallel"),
        name="ffn",
    )(x, ng, wg, wu, wd, nf)


def _rope_tables(pos):
    half = RET_DK // 2
    inv = ROPE_BASE ** (-jnp.arange(half, dtype=F32) / half)
    ang = pos.astype(F32)[:, None] * inv[None, :]
    cos, sin = jnp.cos(ang), jnp.sin(ang)
    return jnp.concatenate([cos, cos], axis=-1), jnp.concatenate([-sin, sin], axis=-1)


def _pad_lanes(v, offset):
    return jnp.zeros((BA_PAD,), F32).at[offset:offset + v.shape[0]].set(v)


def kernel(x_prompt, x_sample, state_ret, state_gdn, state_conv, cache_mem_k, cache_mem_v, mem_prompt,
           norm_mix_g, w_in, ret_gn_g, w_branch_a, gdn_conv_w, gdn_a_log, gdn_dt_bias, gdn_norm_g,
           w_branch_b, w_out, norm_x_g, mem_norm_g, w_xq, w_xk, w_xv, w_xo, norm_ffn_g, w_gate, w_up,
           w_down, norm_final_g):
    depth = w_in.shape[0]
    assert depth == 1, "single-layer kernel"
    b, t, _ = x_prompt.shape
    ns = x_sample.shape[0]
    l = 0

    w = w_in[l]
    ba0, g0 = COL_GATE, COL_GATE + 2 * GDN_HEADS
    w_bf = w.astype(BF16)
    w_gates = w_bf[:, g0:]
    w_ba = jnp.pad(w[:, ba0:g0], ((0, 0), (0, BA_PAD - 2 * GDN_HEADS))).astype(BF16)
    w_bat = w_ba.T
    row = lambda v: v.reshape(1, -1)
    wa, wb, wo = w_branch_a[l].astype(BF16), w_branch_b[l].astype(BF16), w_out[l].astype(BF16)
    wq, wk, wv, wxo = w_xq[l].astype(BF16), w_xk[l].astype(BF16), w_xv[l].astype(BF16), w_xo[l].astype(BF16)
    wg, wu, wd = w_gate[l].astype(BF16), w_up[l].astype(BF16), w_down[l].astype(BF16)
    alog_r = _pad_lanes(gdn_a_log[l], GDN_HEADS).reshape(1, BA_PAD)
    dt_r = _pad_lanes(gdn_dt_bias[l], GDN_HEADS).reshape(1, BA_PAD)
    alog_c, dt_c = alog_r.reshape(BA_PAD, 1), dt_r.reshape(BA_PAD, 1)
    cos_p, sin_p = _rope_tables(jnp.arange(t))
    cos_s, sin_s = _rope_tables(PAST_LEN + jnp.arange(1))

    mix_g = row(norm_mix_g[l])
    xp = x_prompt.reshape(b * t, D_MODEL)
    oa_p, sr_p = _retention_prompt(x_prompt, mix_g, w_bf, cos_p, sin_p, row(ret_gn_g[l]))
    ob_p, sg_p, sc_p = _gdn_prompt(x_prompt, mix_g, w_bf, w_ba, w_bat, gdn_conv_w[l], alog_r, dt_r,
                                   alog_c, dt_c, row(gdn_norm_g[l]))
    x1_p, q_p = _merge(xp, mix_g, w_gates, oa_p.reshape(b * t, -1), ob_p.reshape(b * t, -1), wa, wb, wo,
                       row(norm_x_g[l]), wq, tm=512)
    mk_p, mv_p, mk4_p, mv4_p = _memkv(mem_prompt.reshape(b * N_MEM, D_MODEL), row(mem_norm_g[l]), wk, wv)
    x2_p = _xattn_prompt(q_p.reshape(b, t, D_MODEL), mk_p.reshape(b, N_MEM, D_MODEL),
                         mv_p.reshape(b, N_MEM, D_MODEL), x1_p.reshape(b, t, D_MODEL), wxo)
    y_p = _ffn(x2_p.reshape(b * t, D_MODEL), row(norm_ffn_g[l]), wg, wu, wd, row(norm_final_g), tm=512)

    xs = x_sample.reshape(ns, D_MODEL)
    proj_s, ba_s = _inproj(xs, mix_g, w_bf, COL_GATE, w_ba)
    oa_s, ob_s, sr_s, sg_s, sc_s = _sample_step(proj_s[:, :COL_GDN], proj_s[:, COL_GDN:], ba_s, state_ret[l],
                                                state_gdn[l], state_conv[l],
                                                cos_s, sin_s, gdn_conv_w[l], gdn_a_log[l], gdn_dt_bias[l],
                                                ret_gn_g[l], row(gdn_norm_g[l]))
    x1_s, q_s = _merge(xs, mix_g, w_gates, oa_s.reshape(ns, -1), ob_s.reshape(ns, -1), wa, wb, wo,
                       row(norm_x_g[l]), wq, tm=ns)
    o_s = _xattn_sample(q_s, cache_mem_k[l], cache_mem_v[l])
    x2_s = _resid_mm(x1_s, o_s.reshape(ns, D_MODEL).astype(BF16), wxo)
    y_s = _ffn(x2_s, row(norm_ffn_g[l]), wg, wu, wd, row(norm_final_g), tm=ns)

    return (y_p.reshape(b, t, D_MODEL), y_s.reshape(ns, 1, D_MODEL),
            sr_p[None], sg_p[None], sc_p[None],
            mk4_p.reshape(1, b, N_MEM, X_HEADS, X_HD), mv4_p.reshape(1, b, N_MEM, X_HEADS, X_HD),
            sr_s[None], sg_s[None], sc_s[None])
```

```python
import functools

import numpy as np
import jax
import jax.numpy as jnp
from jax import lax
from jax.experimental import pallas as pl
from jax.experimental.pallas import tpu as pltpu

F32 = jnp.float32
BF16 = jnp.bfloat16

D_MODEL = 1024
RET_HEADS, RET_DK, RET_DV = 4, 128, 256
GDN_HEADS, GDN_DK, GDN_DV = 8, 128, 128
CONV_W = 4
CONV_CH = 3 * GDN_HEADS * GDN_DK
N_MEM, X_HEADS, X_HD = 256, 4, 256
PAST_LEN = 16384
ROPE_BASE = 10000.0
EPS = 1e-6
GDN_CHUNK = 64

COL_RET = 0
COL_GDN = 3072
COL_GATE = 7168
BA_PAD = 128

VMEM_LIMIT = 56 * 1024 * 1024

NT_DIMS = (((1,), (1,)), ((), ()))
TN_DIMS = (((0,), (0,)), ((), ()))


def _mm(a, b):
    return jnp.dot(a.astype(BF16), b.astype(BF16), preferred_element_type=F32)


def _mm_nt(a, b):
    return lax.dot_general(a.astype(BF16), b.astype(BF16), NT_DIMS, preferred_element_type=F32)


def _mm_tn(a, b):
    return lax.dot_general(a.astype(BF16), b.astype(BF16), TN_DIMS, preferred_element_type=F32)


def _split3(x):
    hi = x.astype(BF16)
    r = x - hi.astype(F32)
    mid = r.astype(BF16)
    return hi, mid, (r - mid.astype(F32)).astype(BF16)


def _mm_sel(sel, x):
    return sum(jnp.dot(sel, p, preferred_element_type=F32) for p in _split3(x))


def _mm_sel_r(x, sel):
    return sum(jnp.dot(p, sel, preferred_element_type=F32) for p in _split3(x))


def _rms(x, g):
    return x * lax.rsqrt(jnp.mean(x * x, axis=-1, keepdims=True) + EPS) * g


def _silu(x):
    h = 0.5 * x
    return h + h * jnp.tanh(h)


def _softplus(x):
    return jnp.maximum(x, 0.0) + jnp.log1p(jnp.exp(-jnp.abs(x)))


def _params(*sem):
    return pltpu.CompilerParams(dimension_semantics=sem, vmem_limit_bytes=VMEM_LIMIT)


def _inproj_body(x_ref, g_ref, w_ref, wba_ref, o_ref, oba_ref, h_scr):
    @pl.when(pl.program_id(0) == 0)
    def _():
        hb = _rms(x_ref[...], g_ref[...]).astype(BF16)
        h_scr[...] = hb
        oba_ref[...] = jnp.dot(hb, wba_ref[...], preferred_element_type=F32)

    o_ref[...] = jnp.dot(h_scr[...], w_ref[...], preferred_element_type=F32)


def _inproj(x, g, w_bf, n, w_ba, tn=1024):
    m = x.shape[0]
    assert n % tn == 0 and n <= w_bf.shape[1]
    return pl.pallas_call(
        _inproj_body,
        grid=(n // tn,),
        in_specs=[
            pl.BlockSpec((m, D_MODEL), lambda j: (0, 0)),
            pl.BlockSpec((1, D_MODEL), lambda j: (0, 0)),
            pl.BlockSpec((D_MODEL, tn), lambda j: (0, j)),
            pl.BlockSpec((D_MODEL, BA_PAD), lambda j: (0, 0)),
        ],
        out_specs=[
            pl.BlockSpec((m, tn), lambda j: (0, j)),
            pl.BlockSpec((m, BA_PAD), lambda j: (0, 0)),
        ],
        out_shape=[
            jax.ShapeDtypeStruct((m, n), F32),
            jax.ShapeDtypeStruct((m, BA_PAD), F32),
        ],
        scratch_shapes=[pltpu.VMEM((m, D_MODEL), BF16)],
        compiler_params=_params("arbitrary"),
        name="inproj",
    )(x, g, w_bf, w_ba)


_RET_LOG_G = np.log1p(-np.exp2(-5.0 - np.arange(RET_HEADS, dtype=np.float64)))


def _ret_tables(c):
    idx = np.arange(c, dtype=np.float64)
    diff = idx[:, None] - idx[None, :]
    dmat = np.where(diff >= 0, np.exp(np.maximum(diff, 0.0)[None] * _RET_LOG_G[:, None, None]), 0.0)
    qdec = np.exp((idx + 1.0)[None, :] * _RET_LOG_G[:, None])
    kdec = np.exp((c - 1.0 - idx)[None, :] * _RET_LOG_G[:, None])
    lane = np.ones((1, 1, RET_DK))
    return (jnp.asarray(dmat, F32), jnp.asarray(qdec[:, :, None] * lane, F32),
            jnp.asarray(kdec[:, :, None] * lane, F32), [float(v) for v in np.exp(c * _RET_LOG_G)])


def _rot(x, cos, sin):
    return x * cos + pltpu.roll(x, RET_DK // 2, 1) * sin


def _group_norm_gate(o, gate, gn):
    mu = jnp.mean(o, axis=-1, keepdims=True)
    d = o - mu
    var = jnp.mean(d * d, axis=-1, keepdims=True)
    return _silu(gate) * (d * lax.rsqrt(var + EPS) * gn)


def _proj_tiles(hb, w_ref, dst, width=512):
    for c0 in range(0, w_ref.shape[1], width):
        dst[:, c0:c0 + width] = jnp.dot(hb, w_ref[:, c0:c0 + width], preferred_element_type=F32)


def _next_block(nt, nblocks):
    def index_map(i, j):
        n1 = jnp.minimum(i * nt + j + 1, nblocks - 1)
        return (n1 // nt, n1 % nt, 0)
    return index_map


def _ret_body(cdec, x0_ref, xn_ref, nmg_ref, w_ref, *rest):
    *consts, o_ref, s_out_ref, s_scr, pa, pb = rest
    t = pl.program_id(1)
    n = pl.program_id(0) * pl.num_programs(1) + t
    bufs = (pa, pb)

    @pl.when(t == 0)
    def _():
        s_scr[...] = jnp.zeros_like(s_scr)

    @pl.when(n == 0)
    def _():
        _proj_tiles(_rms(x0_ref[0], nmg_ref[...]).astype(BF16), w_ref, pa)

    for slot in range(2):
        @pl.when(n % 2 == slot)
        def _(slot=slot):
            _ret_step(cdec, t, xn_ref, nmg_ref, w_ref, bufs[slot], bufs[1 - slot], *consts, o_ref, s_out_ref, s_scr)


def _ret_step(cdec, t, xn_ref, nmg_ref, w_ref, proj, proj_next, cos_ref, sin_ref, dmat_ref, qdec_ref, kdec_ref,
              gn_ref, o_ref, s_out_ref, s_scr):
    qw = RET_HEADS * RET_DK
    _proj_tiles(_rms(xn_ref[0], nmg_ref[...]).astype(BF16), w_ref, proj_next)

    chunk = dmat_ref.shape[1]
    for c0 in range(0, proj.shape[0], chunk):
        rows = slice(c0, c0 + chunk)
        cos, sin = cos_ref[rows, :], sin_ref[rows, :]
        for h in range(RET_HEADS):
            qk = slice(h * RET_DK, (h + 1) * RET_DK)
            kk = slice(qw + h * RET_DK, qw + (h + 1) * RET_DK)
            vv = slice(h * RET_DV, (h + 1) * RET_DV)
            q = _rot(proj[rows, qk], cos, sin)
            k = _rot(proj[rows, kk], cos, sin) * (RET_DK ** -0.5)
            v = proj[rows, 2 * qw + h * RET_DV:2 * qw + (h + 1) * RET_DV]
            gate = proj[rows, 2 * qw + (RET_HEADS + h) * RET_DV:2 * qw + (RET_HEADS + h + 1) * RET_DV]
            s = s_scr[h]
            scores = _mm_nt(q, k) * dmat_ref[h]
            o = _mm(scores, v) + _mm(q * qdec_ref[h], s)
            s_scr[h] = cdec[h] * s + _mm_tn(k * kdec_ref[h], v)
            o_ref[0, rows, vv] = _group_norm_gate(o, gate, gn_ref[:, vv]).astype(BF16)

    @pl.when(t == pl.num_programs(1) - 1)
    def _():
        s_out_ref[0] = s_scr[...]


def _retention_prompt(x3, mix_g, w_rg, cos, sin, gn, tb=512, chunk=256):
    b, t, _ = x3.shape
    nt = t // tb
    dmat, qdec, kdec, cdec = _ret_tables(chunk)
    vw = RET_HEADS * RET_DV
    const3 = lambda i, j: (0, 0, 0)
    return pl.pallas_call(
        functools.partial(_ret_body, cdec),
        grid=(b, nt),
        in_specs=[
            pl.BlockSpec((1, tb, D_MODEL), const3),
            pl.BlockSpec((1, tb, D_MODEL), _next_block(nt, b * nt)),
            pl.BlockSpec((1, D_MODEL), lambda i, j: (0, 0)),
            pl.BlockSpec((D_MODEL, COL_GDN), lambda i, j: (0, COL_RET // COL_GDN), pipeline_mode=pl.Buffered(1)),
            pl.BlockSpec((tb, RET_DK), lambda i, j: (j, 0)),
            pl.BlockSpec((tb, RET_DK), lambda i, j: (j, 0)),
            pl.BlockSpec((RET_HEADS, chunk, chunk), const3),
            pl.BlockSpec((RET_HEADS, chunk, RET_DK), const3),
            pl.BlockSpec((RET_HEADS, chunk, RET_DK), const3),
            pl.BlockSpec((1, vw), lambda i, j: (0, 0)),
        ],
        out_specs=[
            pl.BlockSpec((1, tb, vw), lambda i, j: (i, j, 0)),
            pl.BlockSpec((1, RET_HEADS, RET_DK, RET_DV), lambda i, j: (i, 0, 0, 0)),
        ],
        out_shape=[
            jax.ShapeDtypeStruct((b, t, vw), BF16),
            jax.ShapeDtypeStruct((b, RET_HEADS, RET_DK, RET_DV), F32),
        ],
        scratch_shapes=[pltpu.VMEM((RET_HEADS, RET_DK, RET_DV), F32)]
        + 2 * [pltpu.VMEM((tb, COL_GDN), F32)],
        compiler_params=_params("arbitrary", "arbitrary"),
        name="retention_prompt",
    )(x3, x3, mix_g, w_rg, cos, sin, dmat, qdec, kdec, gn)


def _gdn_tables(tb):
    idx = np.arange(tb)
    same = (idx[:, None] // GDN_CHUNK) == (idx[None, :] // GDN_CHUNK)
    lower = same & (idx[:, None] >= idx[None, :])
    nchunk = tb // GDN_CHUNK
    chunk_sel = np.repeat((idx[:, None] // GDN_CHUNK) == np.arange(nchunk)[None, :], 128, axis=1)
    grp = 2 * GDN_CHUNK
    il = (idx % grp)[:, None]
    jl = np.arange(grp)[None, :]
    bias = np.where((il // GDN_CHUNK == jl // GDN_CHUNK) & (il >= jl), 0.0, -1e30)
    eye = (il == jl).astype(np.float64)
    return (jnp.asarray(lower, BF16), jnp.asarray(lower.T, BF16), jnp.asarray(same, BF16),
            jnp.asarray(chunk_sel, BF16), jnp.asarray(bias, F32), jnp.asarray(eye - 1.0, F32),
            jnp.asarray(eye, BF16))


def _gdn_prepare(x_ref, first_of_seq, nmg_ref, wqkv_ref, wz_ref, wba_ref, wbat_ref, cw_ref, xr, cs_buf, zs, bas,
                 bats, tb):
    st = {}

    def start():
        xr[0:8, :] = jnp.where(first_of_seq, 0.0, xr[tb:tb + 8, :])
        st["hb"] = _rms(x_ref[0], nmg_ref[...]).astype(BF16)

    def tile(w_ref, dst, c0, width=512):
        def run():
            dst[:, c0:c0 + width] = jnp.dot(st["hb"], w_ref[:, c0:c0 + width], preferred_element_type=F32)
        return run

    def logits():
        bas[...] = jnp.dot(st["hb"], wba_ref[...], preferred_element_type=F32)
        bats[...] = lax.dot_general(wbat_ref[...], st["hb"], NT_DIMS, preferred_element_type=F32)

    def conv(c0):
        def run():
            cs = slice(c0, c0 + GDN_DK)
            acc = xr[5:5 + tb, cs] * cw_ref[0:1, cs]
            for i in range(1, CONV_W):
                acc = acc + xr[5 + i:5 + i + tb, cs] * cw_ref[i:i + 1, cs]
            cs_buf[:, cs] = _silu(acc)
        return run

    mxu_items = ([start] + [tile(wqkv_ref, xr.at[8:8 + tb], c0) for c0 in range(0, CONV_CH, 512)]
                 + [tile(wz_ref, zs, c0) for c0 in range(0, wz_ref.shape[1], 512)] + [logits])
    vpu_items = [conv(c0) for c0 in range(0, CONV_CH, GDN_DK)]
    return mxu_items, vpu_items


class _Background:
    def __init__(self, items):
        self._items = list(items)

    def __call__(self, count):
        for item in self._items[:count]:
            item()
        del self._items[:count]

    def drain(self):
        self(len(self._items))


def _gdn_body(tb, x0_ref, xn_ref, nmg_ref, wqkv_ref, wz_ref, wba_ref, wbat_ref, cw_ref, *rest):
    *consts, o_ref, s_out_ref, conv_out_ref, s_scr, xr, ca, za, baa, bata, cb, zb, bab, batb = rest
    t = pl.program_id(1)
    nt = pl.num_programs(1)
    n = pl.program_id(0) * nt + t
    weights = (nmg_ref, wqkv_ref, wz_ref, wba_ref, wbat_ref, cw_ref)
    bufs = ((ca, za, baa, bata), (cb, zb, bab, batb))

    @pl.when(t == 0)
    def _():
        s_scr[...] = jnp.zeros_like(s_scr)

    @pl.when(n == 0)
    def _():
        mxu_items, vpu_items = _gdn_prepare(x0_ref, True, *weights, xr, *bufs[0], tb)
        _Background(mxu_items + vpu_items).drain()

    for slot in range(2):
        @pl.when(n % 2 == slot)
        def _(slot=slot):
            mxu_items, vpu_items = _gdn_prepare(xn_ref, t == nt - 1, *weights, xr, *bufs[1 - slot], tb)
            _gdn_step(tb, t, bufs[slot], *consts, o_ref, s_out_ref, s_scr,
                      _Background(mxu_items), _Background(vpu_items))

    @pl.when(t == nt - 2)
    def _():
        conv_out_ref[0] = xr[tb + 5:tb + 8, :]


def _gdn_step(tb, t, cur, alog_r_ref, dt_r_ref, alog_c_ref, dt_c_ref, ng_ref,
              lbd_ref, ubd_ref, obd_ref, csel_ref, bias_ref, offd_ref, eye_ref, o_ref, s_out_ref, s_scr,
              bg_mxu, bg_vpu):
    nchunk = tb // GDN_CHUNK
    hk = GDN_HEADS * GDN_DK
    c_scr, zz, ba_ref, bat_ref = cur
    nh = GDN_HEADS
    ba, bat = ba_ref[...], bat_ref[nh:2 * nh, :]

    bg_mxu(2)
    log2e = 1.4426950408889634
    beta_c = jax.nn.sigmoid(ba)
    g_c = (-log2e * jnp.exp(alog_r_ref[...])) * _softplus(ba + dt_r_ref[...])
    g_r = (-log2e * jnp.exp(alog_c_ref[nh:2 * nh, :])) * _softplus(bat + dt_c_ref[nh:2 * nh, :])
    gc_c = _mm_sel(lbd_ref[...], g_c)
    gt_c = _mm_sel(obd_ref[...], g_c)
    gc_r = _mm_sel_r(g_r, ubd_ref[...])
    gt_l = _mm_sel_r(g_r, csel_ref[...])

    heads = range(GDN_HEADS)
    grp = 2 * GDN_CHUNK
    groups = [slice(p * grp, (p + 1) * grp) for p in range(tb // grp)]

    def grp_dot(a, b):
        return jnp.concatenate([jnp.dot(a[g], b[g], preferred_element_type=F32) for g in groups], axis=0)

    def grp_dot_nt(a, b):
        return jnp.concatenate([lax.dot_general(a[g], b[g], NT_DIMS, preferred_element_type=F32)
                                for g in groups], axis=0)

    bias = bias_ref[...]
    offdiag = offd_ref[...]
    eye_b = eye_ref[...]
    qs, ks, gammas, pbs, rhss, qgs, khs = [], [], [], [], [], [], []
    for h in heads:
        q = c_scr[:, h * GDN_DK:(h + 1) * GDN_DK]
        k = c_scr[:, hk + h * GDN_DK:hk + (h + 1) * GDN_DK]
        v = c_scr[:, 2 * hk + h * GDN_DV:2 * hk + (h + 1) * GDN_DV]
        q = q * lax.rsqrt(jnp.sum(q * q, axis=-1, keepdims=True) + EPS) * (GDN_DK ** -0.5)
        k = k * lax.rsqrt(jnp.sum(k * k, axis=-1, keepdims=True) + EPS)
        beta = beta_c[:, h:h + 1]
        gcc = gc_c[:, 8 + h:9 + h]
        gtc = gt_c[:, 8 + h:9 + h]
        gcr = gc_r[h:h + 1, :]
        dg = jnp.concatenate([gcc[g] - gcr[:, g] for g in groups], axis=0)
        gamma = jnp.exp2(dg + bias)
        kbeta = k * beta
        kb = k.astype(BF16)
        pbs.append((grp_dot_nt(kbeta.astype(BF16), kb) * (gamma * offdiag)).astype(BF16))
        eg = jnp.exp2(gcc)
        rhss.append(jnp.concatenate([v * beta, kbeta * eg], axis=1))
        qs.append(q.astype(BF16))
        ks.append(kb)
        gammas.append(gamma)
        qgs.append(q * eg)
        khs.append(k * jnp.exp2(gtc - gcc))
        bg_mxu(1)
        bg_vpu(1)
    bg_mxu.drain()
    bg_vpu(4)

    def solve(_, carry):
        pb = pbs
        tbs = [pb[h] + eye_b for h in heads]
        for lvl in range(5):
            pb = [grp_dot(pb[h], pb[h]).astype(BF16) for h in heads]
            bg_vpu(1)
            tnew = [grp_dot(tbs[h], pb[h] + eye_b) for h in heads]
            bg_vpu(1)
            tbs = [x.astype(BF16) for x in tnew]

        us, ws, qks = [], [], []
        for h in heads:
            uw = rhss[h] + grp_dot(tbs[h] - eye_b, rhss[h].astype(BF16))
            us.append(uw[:, :GDN_DV])
            ws.append(uw[:, GDN_DV:])
            qks.append((grp_dot_nt(qs[h], ks[h]) * gammas[h]).astype(BF16))
        bg_vpu(2)

        s = [s_scr[h] for h in heads]
        vn_parts = [[] for _ in heads]
        qs_parts = [[] for _ in heads]
        for c in range(nchunk):
            rows = slice(c * GDN_CHUNK, (c + 1) * GDN_CHUNK)
            for h in heads:
                wq = _mm(jnp.concatenate([ws[h][rows], qgs[h][rows]], axis=0), s[h])
                vn = us[h][rows] - wq[:GDN_CHUNK]
                qs_parts[h].append(wq[GDN_CHUNK:])
                vn_parts[h].append(vn)
                decay = jnp.exp2(gt_l[h:h + 1, c * 128:(c + 1) * 128])
                s[h] = decay * s[h] + _mm_tn(khs[h][rows], vn)
            bg_vpu(1)
        bg_vpu.drain()
        for h in heads:
            hs = slice(h * GDN_DV, (h + 1) * GDN_DV)
            s_scr[h] = s[h]
            vn = jnp.concatenate(vn_parts[h], axis=0).astype(BF16)
            o = jnp.concatenate(qs_parts[h], axis=0) + grp_dot(qks[h], vn)
            o_ref[0, :, hs] = (_rms(o, ng_ref[...]) * _silu(zz[:, hs])).astype(BF16)
        return carry

    lax.fori_loop(0, jnp.minimum(t + 1, 1), solve, 0)

    @pl.when(t == pl.num_programs(1) - 1)
    def _():
        s_out_ref[0] = s_scr[...]


def _gdn_prompt(x3, mix_g, w_rg, w_ba, w_bat, conv_w, alog_r, dt_r, alog_c, dt_c, norm_g, tb=256):
    b, t, _ = x3.shape
    nt = t // tb
    assert t % tb == 0 and nt >= 2, "a sequence's last block must be prepared during one of its own steps"
    lbd, ubd, obd, csel, bias, offdiag, eye = _gdn_tables(tb)
    vw = GDN_HEADS * GDN_DV
    c2 = lambda i, j: (0, 0)
    single = pl.Buffered(1)
    return pl.pallas_call(
        functools.partial(_gdn_body, tb),
        grid=(b, nt),
        in_specs=[
            pl.BlockSpec((1, tb, D_MODEL), lambda i, j: (0, 0, 0)),
            pl.BlockSpec((1, tb, D_MODEL), _next_block(nt, b * nt)),
            pl.BlockSpec((1, D_MODEL), c2),
            pl.BlockSpec((D_MODEL, CONV_CH), lambda i, j: (0, COL_GDN // CONV_CH), pipeline_mode=single),
            pl.BlockSpec((D_MODEL, vw), lambda i, j: (0, (COL_GDN + CONV_CH) // vw), pipeline_mode=single),
            pl.BlockSpec((D_MODEL, BA_PAD), c2),
            pl.BlockSpec((BA_PAD, D_MODEL), c2),
            pl.BlockSpec((CONV_W, CONV_CH), c2),
            pl.BlockSpec((1, BA_PAD), c2),
            pl.BlockSpec((1, BA_PAD), c2),
            pl.BlockSpec((BA_PAD, 1), c2),
            pl.BlockSpec((BA_PAD, 1), c2),
            pl.BlockSpec((1, GDN_DV), c2),
            pl.BlockSpec((tb, tb), c2),
            pl.BlockSpec((tb, tb), c2),
            pl.BlockSpec((tb, tb), c2),
            pl.BlockSpec((tb, (tb // GDN_CHUNK) * 128), c2),
            pl.BlockSpec((tb, 2 * GDN_CHUNK), c2),
            pl.BlockSpec((tb, 2 * GDN_CHUNK), c2),
            pl.BlockSpec((tb, 2 * GDN_CHUNK), c2),
        ],
        out_specs=[
            pl.BlockSpec((1, tb, vw), lambda i, j: (i, j, 0)),
            pl.BlockSpec((1, GDN_HEADS, GDN_DK, GDN_DV), lambda i, j: (i, 0, 0, 0)),
            pl.BlockSpec((1, CONV_W - 1, CONV_CH), lambda i, j: (i, 0, 0)),
        ],
        out_shape=[
            jax.ShapeDtypeStruct((b, t, vw), BF16),
            jax.ShapeDtypeStruct((b, GDN_HEADS, GDN_DK, GDN_DV), F32),
            jax.ShapeDtypeStruct((b, CONV_W - 1, CONV_CH), F32),
        ],
        scratch_shapes=[pltpu.VMEM((GDN_HEADS, GDN_DK, GDN_DV), F32), pltpu.VMEM((tb + 8, CONV_CH), F32)]
        + 2 * [pltpu.VMEM((tb, CONV_CH), F32), pltpu.VMEM((tb, vw), F32),
               pltpu.VMEM((tb, BA_PAD), F32), pltpu.VMEM((BA_PAD, tb), F32)],
        compiler_params=_params("arbitrary", "arbitrary"),
        name="gdn_prompt",
    )(x3, x3, mix_g, w_rg, w_rg, w_ba, w_bat, conv_w, alog_r, dt_r, alog_c, dt_c, norm_g, lbd, ubd, obd, csel,
      bias, offdiag, eye)


def _block_diag_rows(x, nblk):
    row = lax.broadcasted_iota(jnp.int32, (8, nblk * 128), 0)
    blk = lax.broadcasted_iota(jnp.int32, (8, nblk * 128), 1) // 128
    return jnp.where(row == blk, jnp.concatenate([x] * nblk, axis=1), 0.0)


def _sample_step_body(cdec, qk_ref, rv_ref, rg_ref, x_ref, z_ref, bcol_ref, sr_ref, sg_ref, sc_ref, cos_ref, sin_ref,
                      cw_ref, alog_ref, dt_ref, gn_ref, ng_ref,
                      oa_ref, ob_ref, sr_out_ref, sg_out_ref, sc_out_ref):
    nh = GDN_HEADS
    cos, sin = cos_ref[...], sin_ref[...]
    gam = jnp.where(lax.broadcasted_iota(jnp.int32, (8, 1), 0) == 0, cdec[0], 0.0)
    for h in range(1, RET_HEADS):
        gam = jnp.where(lax.broadcasted_iota(jnp.int32, (8, 1), 0) == h, cdec[h], gam)
    zero4 = jnp.zeros((RET_HEADS, RET_DV), F32)
    for g in range(x_ref.shape[0]):
        x_new = x_ref[g]
        buf = sc_ref[g]
        conv = x_new * cw_ref[CONV_W - 1]
        for i in range(CONV_W - 1):
            conv = conv + buf[i] * cw_ref[i]
        conv = _silu(conv)
        sc_out_ref[g, 0] = buf[1]
        sc_out_ref[g, 1] = buf[2]
        sc_out_ref[g, 2] = x_new
        q, k, v = conv[0:nh], conv[nh:2 * nh], conv[2 * nh:3 * nh]
        q = q * lax.rsqrt(jnp.sum(q * q, axis=-1, keepdims=True) + EPS) * (GDN_DK ** -0.5)
        k = k * lax.rsqrt(jnp.sum(k * k, axis=-1, keepdims=True) + EPS)
        col = bcol_ref[g]
        beta = jax.nn.sigmoid(col[0:nh])
        eg = jnp.exp(-jnp.exp(alog_ref[...]) * _softplus(col[nh:2 * nh] + dt_ref[...]))
        kbd, qbd = _block_diag_rows(k, nh), _block_diag_rows(q, nh)
        s_flat = sg_ref[g].reshape(nh * GDN_DK, GDN_DV)
        kq_s = _mm(jnp.concatenate([kbd, qbd], axis=0), s_flat)
        vn = beta * (v - eg * kq_s[0:nh])
        o = eg * kq_s[nh:2 * nh] + jnp.sum(q * k, axis=-1, keepdims=True) * vn
        ob_ref[g] = _rms(o, ng_ref[...]) * _silu(z_ref[g])
        upd = _mm_tn(kbd, vn)
        eg_l = jnp.broadcast_to(eg, (nh, GDN_DV))
        for h in range(nh):
            sg_out_ref[g, h] = eg_l[h:h + 1] * sg_ref[g, h] + upd[h * GDN_DK:(h + 1) * GDN_DK]

        qk = _rot(qk_ref[g], cos, sin)
        rq = jnp.concatenate([qk[0:RET_HEADS], qk[0:RET_HEADS]], axis=0)
        rk = jnp.concatenate([qk[RET_HEADS:], qk[RET_HEADS:]], axis=0) * (RET_DK ** -0.5)
        rv = jnp.concatenate([rv_ref[g], zero4], axis=0)
        qbd_r, kbd_r = _block_diag_rows(rq, RET_HEADS), _block_diag_rows(rk, RET_HEADS)
        sr_flat = sr_ref[g].reshape(RET_HEADS * RET_DK, RET_DV)
        q_s = _mm(qbd_r, sr_flat)
        o_r = gam * q_s + jnp.sum(rq * rk, axis=-1, keepdims=True) * rv
        oa_ref[g] = _group_norm_gate(o_r[0:RET_HEADS], rg_ref[g], gn_ref[...])
        upd_r = _mm_tn(kbd_r, rv)
        for h in range(RET_HEADS):
            sr_out_ref[g, h] = cdec[h] * sr_ref[g, h] + upd_r[h * RET_DK:(h + 1) * RET_DK]


def _sample_step(proj_ret, proj_gdn, ba, state_ret, state_gdn, state_conv, cos, sin, conv_w, a_log, dt_bias, gn,
                 norm_g, gs=8):
    ns = proj_ret.shape[0]
    nh, rh = GDN_HEADS, RET_HEADS
    nq = rh * RET_DK
    qk3 = proj_ret[:, :2 * nq].reshape(ns, 2 * rh, RET_DK)
    rv3 = proj_ret[:, 2 * nq:2 * nq + rh * RET_DV].reshape(ns, rh, RET_DV)
    rg3 = proj_ret[:, 2 * nq + rh * RET_DV:].reshape(ns, rh, RET_DV)
    x3 = proj_gdn[:, :CONV_CH].reshape(ns, 3 * nh, GDN_DK)
    z3 = proj_gdn[:, CONV_CH:].reshape(ns, nh, GDN_DV)
    bcol = ba[:, :2 * nh].reshape(ns, 2 * nh, 1)
    sc4 = state_conv.reshape(ns, CONV_W - 1, 3 * nh, GDN_DK)
    cw3 = conv_w.reshape(CONV_W, 3 * nh, GDN_DK)
    cdec = [float(v) for v in np.exp(_RET_LOG_G)]
    c2 = lambda i: (0, 0)
    c3 = lambda i: (0, 0, 0)
    b3 = lambda i: (i, 0, 0)
    b4 = lambda i: (i, 0, 0, 0)
    oa, ob, sr, sg, sc = pl.pallas_call(
        functools.partial(_sample_step_body, cdec),
        grid=(ns // gs,),
        in_specs=[
            pl.BlockSpec((gs, 2 * rh, RET_DK), b3),
            pl.BlockSpec((gs, rh, RET_DV), b3),
            pl.BlockSpec((gs, rh, RET_DV), b3),
            pl.BlockSpec((gs, 3 * nh, GDN_DK), b3),
            pl.BlockSpec((gs, nh, GDN_DV), b3),
            pl.BlockSpec((gs, 2 * nh, 1), b3),
            pl.BlockSpec((gs, rh, RET_DK, RET_DV), b4),
            pl.BlockSpec((gs, nh, GDN_DK, GDN_DV), b4),
            pl.BlockSpec((gs, CONV_W - 1, 3 * nh, GDN_DK), b4),
            pl.BlockSpec((1, RET_DK), c2),
            pl.BlockSpec((1, RET_DK), c2),
            pl.BlockSpec((CONV_W, 3 * nh, GDN_DK), c3),
            pl.BlockSpec((nh, 1), c2),
            pl.BlockSpec((nh, 1), c2),
            pl.BlockSpec((rh, RET_DV), c2),
            pl.BlockSpec((1, GDN_DV), c2),
        ],
        out_specs=[
            pl.BlockSpec((gs, rh, RET_DV), b3),
            pl.BlockSpec((gs, nh, GDN_DV), b3),
            pl.BlockSpec((gs, rh, RET_DK, RET_DV), b4),
            pl.BlockSpec((gs, nh, GDN_DK, GDN_DV), b4),
            pl.BlockSpec((gs, CONV_W - 1, 3 * nh, GDN_DK), b4),
        ],
        out_shape=[
            jax.ShapeDtypeStruct((ns, rh, RET_DV), F32),
            jax.ShapeDtypeStruct((ns, nh, GDN_DV), F32),
            jax.ShapeDtypeStruct(state_ret.shape, F32),
            jax.ShapeDtypeStruct(state_gdn.shape, F32),
            jax.ShapeDtypeStruct(sc4.shape, F32),
        ],
        compiler_params=_params("parallel"),
        name="sample_step",
    )(qk3, rv3, rg3, x3, z3, bcol, state_ret, state_gdn, sc4, cos, sin, cw3, a_log.reshape(nh, 1),
      dt_bias.reshape(nh, 1), gn.reshape(rh, RET_DV), norm_g)
    return (oa.reshape(ns, rh * RET_DV).astype(BF16), ob.reshape(ns, nh * GDN_DV).astype(BF16), sr, sg,
            sc.reshape(state_conv.shape))


def _merge_body(x_ref, nmg_ref, wgt_ref, oa_ref, ob_ref, wa_ref, wb_ref, wo_ref, ng_ref, wq_ref, x1_ref, q_ref):
    x = x_ref[...]
    gates = jnp.dot(_rms(x, nmg_ref[...]).astype(BF16), wgt_ref[...], preferred_element_type=F32)
    ya = jnp.dot(oa_ref[...], wa_ref[...], preferred_element_type=F32)
    yb = jnp.dot(ob_ref[...], wb_ref[...], preferred_element_type=F32)
    merged = jax.nn.sigmoid(gates[:, :D_MODEL]) * ya + jax.nn.sigmoid(gates[:, D_MODEL:]) * yb
    x1 = x + _mm(merged, wo_ref[...])
    x1_ref[...] = x1
    q_ref[...] = _mm(_rms(x1, ng_ref[...]), wq_ref[...]).astype(BF16)


def _merge(x, mix_g, w_gates, oa, ob, wa, wb, wo, ng, wq, tm):
    m = x.shape[0]
    row = lambda i: (i, 0)
    c2 = lambda i: (0, 0)
    wspec = pl.BlockSpec((D_MODEL, D_MODEL), c2)
    return pl.pallas_call(
        _merge_body,
        grid=(m // tm,),
        in_specs=[
            pl.BlockSpec((tm, D_MODEL), row),
            pl.BlockSpec((1, D_MODEL), c2),
            pl.BlockSpec((D_MODEL, 2 * D_MODEL), c2),
            pl.BlockSpec((tm, D_MODEL), row),
            pl.BlockSpec((tm, D_MODEL), row),
            wspec, wspec, wspec,
            pl.BlockSpec((1, D_MODEL), c2),
            wspec,
        ],
        out_specs=[pl.BlockSpec((tm, D_MODEL), row), pl.BlockSpec((tm, D_MODEL), row)],
        out_shape=[jax.ShapeDtypeStruct((m, D_MODEL), F32), jax.ShapeDtypeStruct((m, D_MODEL), BF16)],
        compiler_params=_params("parallel"),
        name="merge",
    )(x, mix_g, w_gates, oa, ob, wa, wb, wo, ng, wq)


def _memkv_body(m_ref, g_ref, wk_ref, wv_ref, k_ref, v_ref, k4_ref, v4_ref):
    mn = _rms(m_ref[...], g_ref[...]).astype(BF16)
    k = jnp.dot(mn, wk_ref[...], preferred_element_type=F32)
    v = jnp.dot(mn, wv_ref[...], preferred_element_type=F32)
    k_ref[...] = k
    v_ref[...] = v
    for h in range(X_HEADS):
        k4_ref[:, h, :] = k[:, h * X_HD:(h + 1) * X_HD]
        v4_ref[:, h, :] = v[:, h * X_HD:(h + 1) * X_HD]


def _memkv(mem, g, wk, wv, tm=512):
    m = mem.shape[0]
    row = lambda i: (i, 0)
    row3 = lambda i: (i, 0, 0)
    c2 = lambda i: (0, 0)
    return pl.pallas_call(
        _memkv_body,
        grid=(m // tm,),
        in_specs=[pl.BlockSpec((tm, D_MODEL), row), pl.BlockSpec((1, D_MODEL), c2),
                  pl.BlockSpec((D_MODEL, D_MODEL), c2), pl.BlockSpec((D_MODEL, D_MODEL), c2)],
        out_specs=[pl.BlockSpec((tm, D_MODEL), row), pl.BlockSpec((tm, D_MODEL), row),
                   pl.BlockSpec((tm, X_HEADS, X_HD), row3), pl.BlockSpec((tm, X_HEADS, X_HD), row3)],
        out_shape=[jax.ShapeDtypeStruct((m, D_MODEL), F32)] * 2
        + [jax.ShapeDtypeStruct((m, X_HEADS, X_HD), F32)] * 2,
        compiler_params=_params("parallel"),
        name="memkv",
    )(mem, g, wk, wv)


def _xattn_body(q_ref, mk_ref, mv_ref, x1_ref, wo_ref, x2_ref):
    parts = []
    for h in range(X_HEADS):
        hs = slice(h * X_HD, (h + 1) * X_HD)
        s = _mm_nt(q_ref[0, :, hs], mk_ref[0, :, hs]) * (X_HD ** -0.5)
        p = jnp.exp(s - jnp.max(s, axis=-1, keepdims=True))
        o = _mm(p, mv_ref[0, :, hs]) / jnp.sum(p, axis=-1, keepdims=True)
        parts.append(o.astype(BF16))
    x2_ref[0] = x1_ref[0] + jnp.dot(jnp.concatenate(parts, axis=1), wo_ref[...], preferred_element_type=F32)


def _xattn_prompt(q3, mk3, mv3, x13, wo, tq=1024):
    b, t, _ = q3.shape
    tok = lambda i, j: (i, j, 0)
    mem = lambda i, j: (i, 0, 0)
    return pl.pallas_call(
        _xattn_body,
        grid=(b, t // tq),
        in_specs=[pl.BlockSpec((1, tq, D_MODEL), tok), pl.BlockSpec((1, N_MEM, D_MODEL), mem),
                  pl.BlockSpec((1, N_MEM, D_MODEL), mem), pl.BlockSpec((1, tq, D_MODEL), tok),
                  pl.BlockSpec((D_MODEL, D_MODEL), lambda i, j: (0, 0))],
        out_specs=pl.BlockSpec((1, tq, D_MODEL), tok),
        out_shape=jax.ShapeDtypeStruct((b, t, D_MODEL), F32),
        compiler_params=_params("parallel", "parallel"),
        name="xattn_prompt",
    )(q3, mk3, mv3, x13, wo)


def _xattn_sample_body(q_ref, mk_ref, mv_ref, o_ref):
    for g in range(q_ref.shape[0]):
        q = q_ref[g]
        s = jnp.sum(mk_ref[g] * q[None], axis=-1, keepdims=True) * (X_HD ** -0.5)
        p = jnp.exp(s - jnp.max(s, axis=0, keepdims=True))
        o_ref[g] = jnp.sum(p * mv_ref[g], axis=0) / jnp.sum(p, axis=0)


def _xattn_sample(q, mk4, mv4, gs=8):
    ns = q.shape[0]
    q3 = q.astype(F32).reshape(ns, X_HEADS, X_HD)
    row = lambda i: (i, 0, 0)
    mem = lambda i: (i, 0, 0, 0)
    return pl.pallas_call(
        _xattn_sample_body,
        grid=(ns // gs,),
        in_specs=[pl.BlockSpec((gs, X_HEADS, X_HD), row), pl.BlockSpec((gs, N_MEM, X_HEADS, X_HD), mem),
                  pl.BlockSpec((gs, N_MEM, X_HEADS, X_HD), mem)],
        out_specs=pl.BlockSpec((gs, X_HEADS, X_HD), row),
        out_shape=jax.ShapeDtypeStruct((ns, X_HEADS, X_HD), F32),
        compiler_params=_params("parallel"),
        name="xattn_sample",
    )(q3, mk4, mv4)


def _resid_mm_body(x_ref, a_ref, w_ref, o_ref):
    o_ref[...] = x_ref[...] + jnp.dot(a_ref[...], w_ref[...], preferred_element_type=F32)


def _resid_mm(x, a, w):
    m = x.shape[0]
    return pl.pallas_call(
        _resid_mm_body,
        out_shape=jax.ShapeDtypeStruct((m, D_MODEL), F32),
        compiler_params=pltpu.CompilerParams(vmem_limit_bytes=VMEM_LIMIT),
        name="resid_mm",
    )(x, a, w)


def _ffn_body(x_ref, ng_ref, wg_ref, wu_ref, wd_ref, nf_ref, y_ref):
    x = x_ref[...]
    h = _rms(x, ng_ref[...]).astype(BF16)
    gate = jnp.dot(h, wg_ref[...], preferred_element_type=F32)
    up = jnp.dot(h, wu_ref[...], preferred_element_type=F32)
    x3 = x + _mm(_silu(gate) * up, wd_ref[...])
    y_ref[...] = _rms(x3, nf_ref[...])


def _ffn(x, ng, wg, wu, wd, nf, tm):
    m = x.shape[0]
    dff = wg.shape[1]
    row = lambda i: (i, 0)
    c2 = lambda i: (0, 0)
    single = pl.Buffered(1)
    return pl.pallas_call(
        _ffn_body,
        grid=(m // tm,),
        in_specs=[pl.BlockSpec((tm, D_MODEL), row), pl.BlockSpec((1, D_MODEL), c2),
                  pl.BlockSpec((D_MODEL, dff), c2, pipeline_mode=single),
                  pl.BlockSpec((D_MODEL, dff), c2, pipeline_mode=single),
                  pl.BlockSpec((dff, D_MODEL), c2, pipeline_mode=single),
                  pl.BlockSpec((1, D_MODEL), c2)],
        out_specs=pl.BlockSpec((tm, D_MODEL), row),
        out_shape=jax.ShapeDtypeStruct((m, D_MODEL), F32),
        compiler_params=_params("parallel"),
        name="ffn",
    )(x, ng, wg, wu, wd, nf)


def _rope_tables(pos):
    half = RET_DK // 2
    inv = ROPE_BASE ** (-jnp.arange(half, dtype=F32) / half)
    ang = pos.astype(F32)[:, None] * inv[None, :]
    cos, sin = jnp.cos(ang), jnp.sin(ang)
    return jnp.concatenate([cos, cos], axis=-1), jnp.concatenate([-sin, sin], axis=-1)


def _pad_lanes(v, offset):
    return jnp.zeros((BA_PAD,), F32).at[offset:offset + v.shape[0]].set(v)


def kernel(x_prompt, x_sample, state_ret, state_gdn, state_conv, cache_mem_k, cache_mem_v, mem_prompt,
           norm_mix_g, w_in, ret_gn_g, w_branch_a, gdn_conv_w, gdn_a_log, gdn_dt_bias, gdn_norm_g,
           w_branch_b, w_out, norm_x_g, mem_norm_g, w_xq, w_xk, w_xv, w_xo, norm_ffn_g, w_gate, w_up,
           w_down, norm_final_g):
    depth = w_in.shape[0]
    assert depth == 1, "single-layer kernel"
    b, t, _ = x_prompt.shape
    ns = x_sample.shape[0]
    l = 0

    w = w_in[l]
    ba0, g0 = COL_GATE, COL_GATE + 2 * GDN_HEADS
    w_bf = w.astype(BF16)
    w_gates = w_bf[:, g0:]
    w_ba = jnp.pad(w[:, ba0:g0], ((0, 0), (0, BA_PAD - 2 * GDN_HEADS))).astype(BF16)
    w_bat = w_ba.T
    row = lambda v: v.reshape(1, -1)
    wa, wb, wo = w_branch_a[l].astype(BF16), w_branch_b[l].astype(BF16), w_out[l].astype(BF16)
    wq, wk, wv, wxo = w_xq[l].astype(BF16), w_xk[l].astype(BF16), w_xv[l].astype(BF16), w_xo[l].astype(BF16)
    wg, wu, wd = w_gate[l].astype(BF16), w_up[l].astype(BF16), w_down[l].astype(BF16)
    alog_r = _pad_lanes(gdn_a_log[l], GDN_HEADS).reshape(1, BA_PAD)
    dt_r = _pad_lanes(gdn_dt_bias[l], GDN_HEADS).reshape(1, BA_PAD)
    alog_c, dt_c = alog_r.reshape(BA_PAD, 1), dt_r.reshape(BA_PAD, 1)
    cos_p, sin_p = _rope_tables(jnp.arange(t))
    cos_s, sin_s = _rope_tables(PAST_LEN + jnp.arange(1))

    mix_g = row(norm_mix_g[l])
    xp = x_prompt.reshape(b * t, D_MODEL)
    oa_p, sr_p = _retention_prompt(x_prompt, mix_g, w_bf, cos_p, sin_p, row(ret_gn_g[l]))
    ob_p, sg_p, sc_p = _gdn_prompt(x_prompt, mix_g, w_bf, w_ba, w_bat, gdn_conv_w[l], alog_r, dt_r,
                                   alog_c, dt_c, row(gdn_norm_g[l]))
    x1_p, q_p = _merge(xp, mix_g, w_gates, oa_p.reshape(b * t, -1), ob_p.reshape(b * t, -1), wa, wb, wo,
                       row(norm_x_g[l]), wq, tm=512)
    mk_p, mv_p, mk4_p, mv4_p = _memkv(mem_prompt.reshape(b * N_MEM, D_MODEL), row(mem_norm_g[l]), wk, wv)
    x2_p = _xattn_prompt(q_p.reshape(b, t, D_MODEL), mk_p.reshape(b, N_MEM, D_MODEL),
                         mv_p.reshape(b, N_MEM, D_MODEL), x1_p.reshape(b, t, D_MODEL), wxo)
    y_p = _ffn(x2_p.reshape(b * t, D_MODEL), row(norm_ffn_g[l]), wg, wu, wd, row(norm_final_g), tm=512)

    xs = x_sample.reshape(ns, D_MODEL)
    proj_s, ba_s = _inproj(xs, mix_g, w_bf, COL_GATE, w_ba)
    oa_s, ob_s, sr_s, sg_s, sc_s = _sample_step(proj_s[:, :COL_GDN], proj_s[:, COL_GDN:], ba_s, state_ret[l],
                                                state_gdn[l], state_conv[l],
                                                cos_s, sin_s, gdn_conv_w[l], gdn_a_log[l], gdn_dt_bias[l],
                                                ret_gn_g[l], row(gdn_norm_g[l]))
    x1_s, q_s = _merge(xs, mix_g, w_gates, oa_s.reshape(ns, -1), ob_s.reshape(ns, -1), wa, wb, wo,
                       row(norm_x_g[l]), wq, tm=ns)
    o_s = _xattn_sample(q_s, cache_mem_k[l], cache_mem_v[l])
    x2_s = _resid_mm(x1_s, o_s.reshape(ns, D_MODEL).astype(BF16), wxo)
    y_s = _ffn(x2_s, row(norm_ffn_g[l]), wg, wu, wd, row(norm_final_g), tm=ns)

    return (y_p.reshape(b, t, D_MODEL), y_s.reshape(ns, 1, D_MODEL),
            sr_p[None], sg_p[None], sc_p[None],
            mk4_p.reshape(1, b, N_MEM, X_HEADS, X_HD), mv4_p.reshape(1, b, N_MEM, X_HEADS, X_HD),
            sr_s[None], sg_s[None], sc_s[None])
```

```python
import functools

import numpy as np
import jax
import jax.numpy as jnp
from jax import lax
from jax.experimental import pallas as pl
from jax.experimental.pallas import tpu as pltpu

F32 = jnp.float32
BF16 = jnp.bfloat16

D_MODEL = 1024
RET_HEADS, RET_DK, RET_DV = 4, 128, 256
GDN_HEADS, GDN_DK, GDN_DV = 8, 128, 128
CONV_W = 4
CONV_CH = 3 * GDN_HEADS * GDN_DK
N_MEM, X_HEADS, X_HD = 256, 4, 256
PAST_LEN = 16384
ROPE_BASE = 10000.0
EPS = 1e-6
GDN_CHUNK = 64

COL_RET = 0
COL_GDN = 3072
COL_GATE = 7168
BA_PAD = 128

VMEM_LIMIT = 56 * 1024 * 1024

NT_DIMS = (((1,), (1,)), ((), ()))
TN_DIMS = (((0,), (0,)), ((), ()))


def _mm(a, b):
    return jnp.dot(a.astype(BF16), b.astype(BF16), preferred_element_type=F32)


def _mm_nt(a, b):
    return lax.dot_general(a.astype(BF16), b.astype(BF16), NT_DIMS, preferred_element_type=F32)


def _mm_tn(a, b):
    return lax.dot_general(a.astype(BF16), b.astype(BF16), TN_DIMS, preferred_element_type=F32)


def _split3(x):
    hi = x.astype(BF16)
    r = x - hi.astype(F32)
    mid = r.astype(BF16)
    return hi, mid, (r - mid.astype(F32)).astype(BF16)


def _mm_sel(sel, x):
    return sum(jnp.dot(sel, p, preferred_element_type=F32) for p in _split3(x))


def _mm_sel_r(x, sel):
    return sum(jnp.dot(p, sel, preferred_element_type=F32) for p in _split3(x))


def _rms(x, g):
    return x * lax.rsqrt(jnp.mean(x * x, axis=-1, keepdims=True) + EPS) * g


def _silu(x):
    h = 0.5 * x
    return h + h * jnp.tanh(h)


def _softplus(x):
    return jnp.maximum(x, 0.0) + jnp.log1p(jnp.exp(-jnp.abs(x)))


def _params(*sem):
    return pltpu.CompilerParams(dimension_semantics=sem, vmem_limit_bytes=VMEM_LIMIT)


def _inproj_body(x_ref, g_ref, w_ref, wba_ref, o_ref, oba_ref, h_scr):
    @pl.when(pl.program_id(0) == 0)
    def _():
        hb = _rms(x_ref[...], g_ref[...]).astype(BF16)
        h_scr[...] = hb
        oba_ref[...] = jnp.dot(hb, wba_ref[...], preferred_element_type=F32)

    o_ref[...] = jnp.dot(h_scr[...], w_ref[...], preferred_element_type=F32)


def _inproj(x, g, w_bf, n, w_ba, tn=1024):
    m = x.shape[0]
    assert n % tn == 0 and n <= w_bf.shape[1]
    return pl.pallas_call(
        _inproj_body,
        grid=(n // tn,),
        in_specs=[
            pl.BlockSpec((m, D_MODEL), lambda j: (0, 0)),
            pl.BlockSpec((1, D_MODEL), lambda j: (0, 0)),
            pl.BlockSpec((D_MODEL, tn), lambda j: (0, j)),
            pl.BlockSpec((D_MODEL, BA_PAD), lambda j: (0, 0)),
        ],
        out_specs=[
            pl.BlockSpec((m, tn), lambda j: (0, j)),
            pl.BlockSpec((m, BA_PAD), lambda j: (0, 0)),
        ],
        out_shape=[
            jax.ShapeDtypeStruct((m, n), F32),
            jax.ShapeDtypeStruct((m, BA_PAD), F32),
        ],
        scratch_shapes=[pltpu.VMEM((m, D_MODEL), BF16)],
        compiler_params=_params("arbitrary"),
        name="inproj",
    )(x, g, w_bf, w_ba)


_RET_LOG_G = np.log1p(-np.exp2(-5.0 - np.arange(RET_HEADS, dtype=np.float64)))


def _ret_tables(c):
    idx = np.arange(c, dtype=np.float64)
    diff = idx[:, None] - idx[None, :]
    dmat = np.where(diff >= 0, np.exp(np.maximum(diff, 0.0)[None] * _RET_LOG_G[:, None, None]), 0.0)
    qdec = np.exp((idx + 1.0)[None, :] * _RET_LOG_G[:, None])
    kdec = np.exp((c - 1.0 - idx)[None, :] * _RET_LOG_G[:, None])
    lane = np.ones((1, 1, RET_DK))
    return (jnp.asarray(dmat, F32), jnp.asarray(qdec[:, :, None] * lane, F32),
            jnp.asarray(kdec[:, :, None] * lane, F32), [float(v) for v in np.exp(c * _RET_LOG_G)])


def _rot(x, cos, sin):
    return x * cos + pltpu.roll(x, RET_DK // 2, 1) * sin


def _group_norm_gate(o, gate, gn):
    mu = jnp.mean(o, axis=-1, keepdims=True)
    d = o - mu
    var = jnp.mean(d * d, axis=-1, keepdims=True)
    return _silu(gate) * (d * lax.rsqrt(var + EPS) * gn)


def _proj_tiles(hb, w_ref, dst, width=512):
    for c0 in range(0, w_ref.shape[1], width):
        dst[:, c0:c0 + width] = jnp.dot(hb, w_ref[:, c0:c0 + width], preferred_element_type=F32)


def _next_block(nt, nblocks):
    def index_map(i, j):
        n1 = jnp.minimum(i * nt + j + 1, nblocks - 1)
        return (n1 // nt, n1 % nt, 0)
    return index_map


def _ret_body(cdec, x0_ref, xn_ref, nmg_ref, w_ref, *rest):
    *consts, o_ref, s_out_ref, s_scr, pa, pb = rest
    t = pl.program_id(1)
    n = pl.program_id(0) * pl.num_programs(1) + t
    bufs = (pa, pb)

    @pl.when(t == 0)
    def _():
        s_scr[...] = jnp.zeros_like(s_scr)

    @pl.when(n == 0)
    def _():
        _proj_tiles(_rms(x0_ref[0], nmg_ref[...]).astype(BF16), w_ref, pa)

    for slot in range(2):
        @pl.when(n % 2 == slot)
        def _(slot=slot):
            _ret_step(cdec, t, xn_ref, nmg_ref, w_ref, bufs[slot], bufs[1 - slot], *consts, o_ref, s_out_ref, s_scr)


def _ret_step(cdec, t, xn_ref, nmg_ref, w_ref, proj, proj_next, cos_ref, sin_ref, dmat_ref, qdec_ref, kdec_ref,
              gn_ref, o_ref, s_out_ref, s_scr):
    qw = RET_HEADS * RET_DK
    _proj_tiles(_rms(xn_ref[0], nmg_ref[...]).astype(BF16), w_ref, proj_next)

    chunk = dmat_ref.shape[1]
    for c0 in range(0, proj.shape[0], chunk):
        rows = slice(c0, c0 + chunk)
        cos, sin = cos_ref[rows, :], sin_ref[rows, :]
        for h in range(RET_HEADS):
            qk = slice(h * RET_DK, (h + 1) * RET_DK)
            kk = slice(qw + h * RET_DK, qw + (h + 1) * RET_DK)
            vv = slice(h * RET_DV, (h + 1) * RET_DV)
            q = _rot(proj[rows, qk], cos, sin)
            k = _rot(proj[rows, kk], cos, sin) * (RET_DK ** -0.5)
            v = proj[rows, 2 * qw + h * RET_DV:2 * qw + (h + 1) * RET_DV]
            gate = proj[rows, 2 * qw + (RET_HEADS + h) * RET_DV:2 * qw + (RET_HEADS + h + 1) * RET_DV]
            s = s_scr[h]
            scores = _mm_nt(q, k) * dmat_ref[h]
            o = _mm(scores, v) + _mm(q * qdec_ref[h], s)
            s_scr[h] = cdec[h] * s + _mm_tn(k * kdec_ref[h], v)
            o_ref[0, rows, vv] = _group_norm_gate(o, gate, gn_ref[:, vv]).astype(BF16)

    @pl.when(t == pl.num_programs(1) - 1)
    def _():
        s_out_ref[0] = s_scr[...]


def _retention_prompt(x3, mix_g, w_rg, cos, sin, gn, tb=512, chunk=256):
    b, t, _ = x3.shape
    nt = t // tb
    dmat, qdec, kdec, cdec = _ret_tables(chunk)
    vw = RET_HEADS * RET_DV
    const3 = lambda i, j: (0, 0, 0)
    return pl.pallas_call(
        functools.partial(_ret_body, cdec),
        grid=(b, nt),
        in_specs=[
            pl.BlockSpec((1, tb, D_MODEL), const3),
            pl.BlockSpec((1, tb, D_MODEL), _next_block(nt, b * nt)),
            pl.BlockSpec((1, D_MODEL), lambda i, j: (0, 0)),
            pl.BlockSpec((D_MODEL, COL_GDN), lambda i, j: (0, COL_RET // COL_GDN), pipeline_mode=pl.Buffered(1)),
            pl.BlockSpec((tb, RET_DK), lambda i, j: (j, 0)),
            pl.BlockSpec((tb, RET_DK), lambda i, j: (j, 0)),
            pl.BlockSpec((RET_HEADS, chunk, chunk), const3),
            pl.BlockSpec((RET_HEADS, chunk, RET_DK), const3),
            pl.BlockSpec((RET_HEADS, chunk, RET_DK), const3),
            pl.BlockSpec((1, vw), lambda i, j: (0, 0)),
        ],
        out_specs=[
            pl.BlockSpec((1, tb, vw), lambda i, j: (i, j, 0)),
            pl.BlockSpec((1, RET_HEADS, RET_DK, RET_DV), lambda i, j: (i, 0, 0, 0)),
        ],
        out_shape=[
            jax.ShapeDtypeStruct((b, t, vw), BF16),
            jax.ShapeDtypeStruct((b, RET_HEADS, RET_DK, RET_DV), F32),
        ],
        scratch_shapes=[pltpu.VMEM((RET_HEADS, RET_DK, RET_DV), F32)]
        + 2 * [pltpu.VMEM((tb, COL_GDN), F32)],
        compiler_params=_params("arbitrary", "arbitrary"),
        name="retention_prompt",
    )(x3, x3, mix_g, w_rg, cos, sin, dmat, qdec, kdec, gn)


def _gdn_tables(tb):
    idx = np.arange(tb)
    same = (idx[:, None] // GDN_CHUNK) == (idx[None, :] // GDN_CHUNK)
    lower = same & (idx[:, None] >= idx[None, :])
    nchunk = tb // GDN_CHUNK
    chunk_sel = np.repeat((idx[:, None] // GDN_CHUNK) == np.arange(nchunk)[None, :], 128, axis=1)
    grp = 2 * GDN_CHUNK
    il = (idx % grp)[:, None]
    jl = np.arange(grp)[None, :]
    bias = np.where((il // GDN_CHUNK == jl // GDN_CHUNK) & (il >= jl), 0.0, -1e30)
    eye = (il == jl).astype(np.float64)
    return (jnp.asarray(lower, BF16), jnp.asarray(lower.T, BF16), jnp.asarray(same, BF16),
            jnp.asarray(chunk_sel, BF16), jnp.asarray(bias, F32), jnp.asarray(eye - 1.0, F32),
            jnp.asarray(eye, BF16))


def _gdn_prepare(x_ref, first_of_seq, nmg_ref, wqkv_ref, wz_ref, wba_ref, wbat_ref, cw_ref, xr, cs_buf, zs, bas,
                 bats, tb):
    st = {}

    def start():
        xr[0:8, :] = jnp.where(first_of_seq, 0.0, xr[tb:tb + 8, :])
        st["hb"] = _rms(x_ref[0], nmg_ref[...]).astype(BF16)

    def tile(w_ref, dst, c0, width=512):
        def run():
            dst[:, c0:c0 + width] = jnp.dot(st["hb"], w_ref[:, c0:c0 + width], preferred_element_type=F32)
        return run

    def logits():
        bas[...] = jnp.dot(st["hb"], wba_ref[...], preferred_element_type=F32)
        bats[...] = lax.dot_general(wbat_ref[...], st["hb"], NT_DIMS, preferred_element_type=F32)

    def conv(c0):
        def run():
            cs = slice(c0, c0 + GDN_DK)
            acc = xr[5:5 + tb, cs] * cw_ref[0:1, cs]
            for i in range(1, CONV_W):
                acc = acc + xr[5 + i:5 + i + tb, cs] * cw_ref[i:i + 1, cs]
            cs_buf[:, cs] = _silu(acc)
        return run

    mxu_items = ([start] + [tile(wqkv_ref, xr.at[8:8 + tb], c0) for c0 in range(0, CONV_CH, 512)]
                 + [tile(wz_ref, zs, c0) for c0 in range(0, wz_ref.shape[1], 512)] + [logits])
    vpu_items = [conv(c0) for c0 in range(0, CONV_CH, GDN_DK)]
    return mxu_items, vpu_items


class _Background:
    def __init__(self, items):
        self._items = list(items)

    def __call__(self, count):
        for item in self._items[:count]:
            item()
        del self._items[:count]

    def drain(self):
        self(len(self._items))


def _gdn_body(tb, x0_ref, xn_ref, nmg_ref, wqkv_ref, wz_ref, wba_ref, wbat_ref, cw_ref, *rest):
    *consts, o_ref, s_out_ref, conv_out_ref, s_scr, xr, ca, za, baa, bata, cb, zb, bab, batb = rest
    t = pl.program_id(1)
    nt = pl.num_programs(1)
    n = pl.program_id(0) * nt + t
    weights = (nmg_ref, wqkv_ref, wz_ref, wba_ref, wbat_ref, cw_ref)
    bufs = ((ca, za, baa, bata), (cb, zb, bab, batb))

    @pl.when(t == 0)
    def _():
        s_scr[...] = jnp.zeros_like(s_scr)

    @pl.when(n == 0)
    def _():
        mxu_items, vpu_items = _gdn_prepare(x0_ref, True, *weights, xr, *bufs[0], tb)
        _Background(mxu_items + vpu_items).drain()

    for slot in range(2):
        @pl.when(n % 2 == slot)
        def _(slot=slot):
            mxu_items, vpu_items = _gdn_prepare(xn_ref, t == nt - 1, *weights, xr, *bufs[1 - slot], tb)
            _gdn_step(tb, t, bufs[slot], *consts, o_ref, s_out_ref, s_scr,
                      _Background(mxu_items), _Background(vpu_items))

    @pl.when(t == nt - 2)
    def _():
        conv_out_ref[0] = xr[tb + 5:tb + 8, :]


def _gdn_step(tb, t, cur, alog_r_ref, dt_r_ref, alog_c_ref, dt_c_ref, ng_ref,
              lbd_ref, ubd_ref, obd_ref, csel_ref, bias_ref, offd_ref, eye_ref, o_ref, s_out_ref, s_scr,
              bg_mxu, bg_vpu):
    nchunk = tb // GDN_CHUNK
    hk = GDN_HEADS * GDN_DK
    c_scr, zz, ba_ref, bat_ref = cur
    nh = GDN_HEADS
    ba, bat = ba_ref[...], bat_ref[nh:2 * nh, :]

    bg_mxu(2)
    log2e = 1.4426950408889634
    beta_c = jax.nn.sigmoid(ba)
    g_c = (-log2e * jnp.exp(alog_r_ref[...])) * _softplus(ba + dt_r_ref[...])
    g_r = (-log2e * jnp.exp(alog_c_ref[nh:2 * nh, :])) * _softplus(bat + dt_c_ref[nh:2 * nh, :])
    gc_c = _mm_sel(lbd_ref[...], g_c)
    gt_c = _mm_sel(obd_ref[...], g_c)
    gc_r = _mm_sel_r(g_r, ubd_ref[...])
    gt_l = _mm_sel_r(g_r, csel_ref[...])

    heads = range(GDN_HEADS)
    grp = 2 * GDN_CHUNK
    groups = [slice(p * grp, (p + 1) * grp) for p in range(tb // grp)]

    def grp_dot(a, b):
        return jnp.concatenate([jnp.dot(a[g], b[g], preferred_element_type=F32) for g in groups], axis=0)

    def grp_dot_nt(a, b):
        return jnp.concatenate([lax.dot_general(a[g], b[g], NT_DIMS, preferred_element_type=F32)
                                for g in groups], axis=0)

    bias = bias_ref[...]
    offdiag = offd_ref[...]
    eye_b = eye_ref[...]
    qs, ks, gammas, pbs, rhss, qgs, khs = [], [], [], [], [], [], []
    for h in heads:
        q = c_scr[:, h * GDN_DK:(h + 1) * GDN_DK]
        k = c_scr[:, hk + h * GDN_DK:hk + (h + 1) * GDN_DK]
        v = c_scr[:, 2 * hk + h * GDN_DV:2 * hk + (h + 1) * GDN_DV]
        q = q * lax.rsqrt(jnp.sum(q * q, axis=-1, keepdims=True) + EPS) * (GDN_DK ** -0.5)
        k = k * lax.rsqrt(jnp.sum(k * k, axis=-1, keepdims=True) + EPS)
        beta = beta_c[:, h:h + 1]
        gcc = gc_c[:, 8 + h:9 + h]
        gtc = gt_c[:, 8 + h:9 + h]
        gcr = gc_r[h:h + 1, :]
        dg = jnp.concatenate([gcc[g] - gcr[:, g] for g in groups], axis=0)
        gamma = jnp.exp2(dg + bias)
        kbeta = k * beta
        kb = k.astype(BF16)
        pbs.append((grp_dot_nt(kbeta.astype(BF16), kb) * (gamma * offdiag)).astype(BF16))
        eg = jnp.exp2(gcc)
        rhss.append(jnp.concatenate([v * beta, kbeta * eg], axis=1))
        qs.append(q.astype(BF16))
        ks.append(kb)
        gammas.append(gamma)
        qgs.append(q * eg)
        khs.append(k * jnp.exp2(gtc - gcc))
        bg_mxu(1)
        bg_vpu(1)
    bg_mxu.drain()

    def solve(_, carry):
        pb = pbs
        tbs = [pb[h] + eye_b for h in heads]
        for lvl in range(5):
            pb = [grp_dot(pb[h], pb[h]).astype(BF16) for h in heads]
            bg_vpu(1)
            tnew = [grp_dot(tbs[h], pb[h] + eye_b) for h in heads]
            bg_vpu(1)
            tbs = [x.astype(BF16) for x in tnew]

        us, ws, qks = [], [], []
        for h in heads:
            uw = rhss[h] + grp_dot(tbs[h] - eye_b, rhss[h].astype(BF16))
            us.append(uw[:, :GDN_DV])
            ws.append(uw[:, GDN_DV:])
            qks.append((grp_dot_nt(qs[h], ks[h]) * gammas[h]).astype(BF16))
        bg_vpu(2)

        s = [s_scr[h] for h in heads]
        vn_parts = [[] for _ in heads]
        qs_parts = [[] for _ in heads]
        for c in range(nchunk):
            rows = slice(c * GDN_CHUNK, (c + 1) * GDN_CHUNK)
            for h in heads:
                wq = _mm(jnp.concatenate([ws[h][rows], qgs[h][rows]], axis=0), s[h])
                vn = us[h][rows] - wq[:GDN_CHUNK]
                qs_parts[h].append(wq[GDN_CHUNK:])
                vn_parts[h].append(vn)
                decay = jnp.exp2(gt_l[h:h + 1, c * 128:(c + 1) * 128])
                s[h] = decay * s[h] + _mm_tn(khs[h][rows], vn)
            bg_vpu(1)
        bg_vpu.drain()
        for h in heads:
            hs = slice(h * GDN_DV, (h + 1) * GDN_DV)
            s_scr[h] = s[h]
            vn = jnp.concatenate(vn_parts[h], axis=0).astype(BF16)
            o = jnp.concatenate(qs_parts[h], axis=0) + grp_dot(qks[h], vn)
            o_ref[0, :, hs] = (_rms(o, ng_ref[...]) * _silu(zz[:, hs])).astype(BF16)
        return carry

    lax.fori_loop(0, jnp.minimum(t + 1, 1), solve, 0)

    @pl.when(t == pl.num_programs(1) - 1)
    def _():
        s_out_ref[0] = s_scr[...]


def _gdn_prompt(x3, mix_g, w_rg, w_ba, w_bat, conv_w, alog_r, dt_r, alog_c, dt_c, norm_g, tb=256):
    b, t, _ = x3.shape
    nt = t // tb
    assert t % tb == 0 and nt >= 2, "a sequence's last block must be prepared during one of its own steps"
    lbd, ubd, obd, csel, bias, offdiag, eye = _gdn_tables(tb)
    vw = GDN_HEADS * GDN_DV
    c2 = lambda i, j: (0, 0)
    single = pl.Buffered(1)
    return pl.pallas_call(
        functools.partial(_gdn_body, tb),
        grid=(b, nt),
        in_specs=[
            pl.BlockSpec((1, tb, D_MODEL), lambda i, j: (0, 0, 0)),
            pl.BlockSpec((1, tb, D_MODEL), _next_block(nt, b * nt)),
            pl.BlockSpec((1, D_MODEL), c2),
            pl.BlockSpec((D_MODEL, CONV_CH), lambda i, j: (0, COL_GDN // CONV_CH), pipeline_mode=single),
            pl.BlockSpec((D_MODEL, vw), lambda i, j: (0, (COL_GDN + CONV_CH) // vw), pipeline_mode=single),
            pl.BlockSpec((D_MODEL, BA_PAD), c2),
            pl.BlockSpec((BA_PAD, D_MODEL), c2),
            pl.BlockSpec((CONV_W, CONV_CH), c2),
            pl.BlockSpec((1, BA_PAD), c2),
            pl.BlockSpec((1, BA_PAD), c2),
            pl.BlockSpec((BA_PAD, 1), c2),
            pl.BlockSpec((BA_PAD, 1), c2),
            pl.BlockSpec((1, GDN_DV), c2),
            pl.BlockSpec((tb, tb), c2),
            pl.BlockSpec((tb, tb), c2),
            pl.BlockSpec((tb, tb), c2),
            pl.BlockSpec((tb, (tb // GDN_CHUNK) * 128), c2),
            pl.BlockSpec((tb, 2 * GDN_CHUNK), c2),
            pl.BlockSpec((tb, 2 * GDN_CHUNK), c2),
            pl.BlockSpec((tb, 2 * GDN_CHUNK), c2),
        ],
        out_specs=[
            pl.BlockSpec((1, tb, vw), lambda i, j: (i, j, 0)),
            pl.BlockSpec((1, GDN_HEADS, GDN_DK, GDN_DV), lambda i, j: (i, 0, 0, 0)),
            pl.BlockSpec((1, CONV_W - 1, CONV_CH), lambda i, j: (i, 0, 0)),
        ],
        out_shape=[
            jax.ShapeDtypeStruct((b, t, vw), BF16),
            jax.ShapeDtypeStruct((b, GDN_HEADS, GDN_DK, GDN_DV), F32),
            jax.ShapeDtypeStruct((b, CONV_W - 1, CONV_CH), F32),
        ],
        scratch_shapes=[pltpu.VMEM((GDN_HEADS, GDN_DK, GDN_DV), F32), pltpu.VMEM((tb + 8, CONV_CH), F32)]
        + 2 * [pltpu.VMEM((tb, CONV_CH), F32), pltpu.VMEM((tb, vw), F32),
               pltpu.VMEM((tb, BA_PAD), F32), pltpu.VMEM((BA_PAD, tb), F32)],
        compiler_params=_params("arbitrary", "arbitrary"),
        name="gdn_prompt",
    )(x3, x3, mix_g, w_rg, w_rg, w_ba, w_bat, conv_w, alog_r, dt_r, alog_c, dt_c, norm_g, lbd, ubd, obd, csel,
      bias, offdiag, eye)


def _block_diag_rows(x, nblk):
    row = lax.broadcasted_iota(jnp.int32, (8, nblk * 128), 0)
    blk = lax.broadcasted_iota(jnp.int32, (8, nblk * 128), 1) // 128
    return jnp.where(row == blk, jnp.concatenate([x] * nblk, axis=1), 0.0)


def _sample_step_body(cdec, qk_ref, rv_ref, rg_ref, x_ref, z_ref, bcol_ref, sr_ref, sg_ref, sc_ref, cos_ref, sin_ref,
                      cw_ref, alog_ref, dt_ref, gn_ref, ng_ref,
                      oa_ref, ob_ref, sr_out_ref, sg_out_ref, sc_out_ref):
    nh = GDN_HEADS
    cos, sin = cos_ref[...], sin_ref[...]
    gam = jnp.where(lax.broadcasted_iota(jnp.int32, (8, 1), 0) == 0, cdec[0], 0.0)
    for h in range(1, RET_HEADS):
        gam = jnp.where(lax.broadcasted_iota(jnp.int32, (8, 1), 0) == h, cdec[h], gam)
    zero4 = jnp.zeros((RET_HEADS, RET_DV), F32)
    for g in range(x_ref.shape[0]):
        x_new = x_ref[g]
        buf = sc_ref[g]
        conv = x_new * cw_ref[CONV_W - 1]
        for i in range(CONV_W - 1):
            conv = conv + buf[i] * cw_ref[i]
        conv = _silu(conv)
        sc_out_ref[g, 0] = buf[1]
        sc_out_ref[g, 1] = buf[2]
        sc_out_ref[g, 2] = x_new
        q, k, v = conv[0:nh], conv[nh:2 * nh], conv[2 * nh:3 * nh]
        q = q * lax.rsqrt(jnp.sum(q * q, axis=-1, keepdims=True) + EPS) * (GDN_DK ** -0.5)
        k = k * lax.rsqrt(jnp.sum(k * k, axis=-1, keepdims=True) + EPS)
        col = bcol_ref[g]
        beta = jax.nn.sigmoid(col[0:nh])
        eg = jnp.exp(-jnp.exp(alog_ref[...]) * _softplus(col[nh:2 * nh] + dt_ref[...]))
        kbd, qbd = _block_diag_rows(k, nh), _block_diag_rows(q, nh)
        s_flat = sg_ref[g].reshape(nh * GDN_DK, GDN_DV)
        kq_s = _mm(jnp.concatenate([kbd, qbd], axis=0), s_flat)
        vn = beta * (v - eg * kq_s[0:nh])
        o = eg * kq_s[nh:2 * nh] + jnp.sum(q * k, axis=-1, keepdims=True) * vn
        ob_ref[g] = _rms(o, ng_ref[...]) * _silu(z_ref[g])
        upd = _mm_tn(kbd, vn)
        eg_l = jnp.broadcast_to(eg, (nh, GDN_DV))
        for h in range(nh):
            sg_out_ref[g, h] = eg_l[h:h + 1] * sg_ref[g, h] + upd[h * GDN_DK:(h + 1) * GDN_DK]

        qk = _rot(qk_ref[g], cos, sin)
        rq = jnp.concatenate([qk[0:RET_HEADS], qk[0:RET_HEADS]], axis=0)
        rk = jnp.concatenate([qk[RET_HEADS:], qk[RET_HEADS:]], axis=0) * (RET_DK ** -0.5)
        rv = jnp.concatenate([rv_ref[g], zero4], axis=0)
        qbd_r, kbd_r = _block_diag_rows(rq, RET_HEADS), _block_diag_rows(rk, RET_HEADS)
        sr_flat = sr_ref[g].reshape(RET_HEADS * RET_DK, RET_DV)
        q_s = _mm(qbd_r, sr_flat)
        o_r = gam * q_s + jnp.sum(rq * rk, axis=-1, keepdims=True) * rv
        oa_ref[g] = _group_norm_gate(o_r[0:RET_HEADS], rg_ref[g], gn_ref[...])
        upd_r = _mm_tn(kbd_r, rv)
        for h in range(RET_HEADS):
            sr_out_ref[g, h] = cdec[h] * sr_ref[g, h] + upd_r[h * RET_DK:(h + 1) * RET_DK]


def _sample_step(proj_ret, proj_gdn, ba, state_ret, state_gdn, state_conv, cos, sin, conv_w, a_log, dt_bias, gn,
                 norm_g, gs=8):
    ns = proj_ret.shape[0]
    nh, rh = GDN_HEADS, RET_HEADS
    nq = rh * RET_DK
    qk3 = proj_ret[:, :2 * nq].reshape(ns, 2 * rh, RET_DK)
    rv3 = proj_ret[:, 2 * nq:2 * nq + rh * RET_DV].reshape(ns, rh, RET_DV)
    rg3 = proj_ret[:, 2 * nq + rh * RET_DV:].reshape(ns, rh, RET_DV)
    x3 = proj_gdn[:, :CONV_CH].reshape(ns, 3 * nh, GDN_DK)
    z3 = proj_gdn[:, CONV_CH:].reshape(ns, nh, GDN_DV)
    bcol = ba[:, :2 * nh].reshape(ns, 2 * nh, 1)
    sc4 = state_conv.reshape(ns, CONV_W - 1, 3 * nh, GDN_DK)
    cw3 = conv_w.reshape(CONV_W, 3 * nh, GDN_DK)
    cdec = [float(v) for v in np.exp(_RET_LOG_G)]
    c2 = lambda i: (0, 0)
    c3 = lambda i: (0, 0, 0)
    b3 = lambda i: (i, 0, 0)
    b4 = lambda i: (i, 0, 0, 0)
    oa, ob, sr, sg, sc = pl.pallas_call(
        functools.partial(_sample_step_body, cdec),
        grid=(ns // gs,),
        in_specs=[
            pl.BlockSpec((gs, 2 * rh, RET_DK), b3),
            pl.BlockSpec((gs, rh, RET_DV), b3),
            pl.BlockSpec((gs, rh, RET_DV), b3),
            pl.BlockSpec((gs, 3 * nh, GDN_DK), b3),
            pl.BlockSpec((gs, nh, GDN_DV), b3),
            pl.BlockSpec((gs, 2 * nh, 1), b3),
            pl.BlockSpec((gs, rh, RET_DK, RET_DV), b4),
            pl.BlockSpec((gs, nh, GDN_DK, GDN_DV), b4),
            pl.BlockSpec((gs, CONV_W - 1, 3 * nh, GDN_DK), b4),
            pl.BlockSpec((1, RET_DK), c2),
            pl.BlockSpec((1, RET_DK), c2),
            pl.BlockSpec((CONV_W, 3 * nh, GDN_DK), c3),
            pl.BlockSpec((nh, 1), c2),
            pl.BlockSpec((nh, 1), c2),
            pl.BlockSpec((rh, RET_DV), c2),
            pl.BlockSpec((1, GDN_DV), c2),
        ],
        out_specs=[
            pl.BlockSpec((gs, rh, RET_DV), b3),
            pl.BlockSpec((gs, nh, GDN_DV), b3),
            pl.BlockSpec((gs, rh, RET_DK, RET_DV), b4),
            pl.BlockSpec((gs, nh, GDN_DK, GDN_DV), b4),
            pl.BlockSpec((gs, CONV_W - 1, 3 * nh, GDN_DK), b4),
        ],
        out_shape=[
            jax.ShapeDtypeStruct((ns, rh, RET_DV), F32),
            jax.ShapeDtypeStruct((ns, nh, GDN_DV), F32),
            jax.ShapeDtypeStruct(state_ret.shape, F32),
            jax.ShapeDtypeStruct(state_gdn.shape, F32),
            jax.ShapeDtypeStruct(sc4.shape, F32),
        ],
        compiler_params=_params("parallel"),
        name="sample_step",
    )(qk3, rv3, rg3, x3, z3, bcol, state_ret, state_gdn, sc4, cos, sin, cw3, a_log.reshape(nh, 1),
      dt_bias.reshape(nh, 1), gn.reshape(rh, RET_DV), norm_g)
    return (oa.reshape(ns, rh * RET_DV).astype(BF16), ob.reshape(ns, nh * GDN_DV).astype(BF16), sr, sg,
            sc.reshape(state_conv.shape))


def _merge_body(x_ref, nmg_ref, wgt_ref, oa_ref, ob_ref, wa_ref, wb_ref, wo_ref, ng_ref, wq_ref, x1_ref, q_ref):
    x = x_ref[...]
    gates = jnp.dot(_rms(x, nmg_ref[...]).astype(BF16), wgt_ref[...], preferred_element_type=F32)
    ya = jnp.dot(oa_ref[...], wa_ref[...], preferred_element_type=F32)
    yb = jnp.dot(ob_ref[...], wb_ref[...], preferred_element_type=F32)
    merged = jax.nn.sigmoid(gates[:, :D_MODEL]) * ya + jax.nn.sigmoid(gates[:, D_MODEL:]) * yb
    x1 = x + _mm(merged, wo_ref[...])
    x1_ref[...] = x1
    q_ref[...] = _mm(_rms(x1, ng_ref[...]), wq_ref[...]).astype(BF16)


def _merge(x, mix_g, w_gates, oa, ob, wa, wb, wo, ng, wq, tm):
    m = x.shape[0]
    row = lambda i: (i, 0)
    c2 = lambda i: (0, 0)
    wspec = pl.BlockSpec((D_MODEL, D_MODEL), c2)
    return pl.pallas_call(
        _merge_body,
        grid=(m // tm,),
        in_specs=[
            pl.BlockSpec((tm, D_MODEL), row),
            pl.BlockSpec((1, D_MODEL), c2),
            pl.BlockSpec((D_MODEL, 2 * D_MODEL), c2),
            pl.BlockSpec((tm, D_MODEL), row),
            pl.BlockSpec((tm, D_MODEL), row),
            wspec, wspec, wspec,
            pl.BlockSpec((1, D_MODEL), c2),
            wspec,
        ],
        out_specs=[pl.BlockSpec((tm, D_MODEL), row), pl.BlockSpec((tm, D_MODEL), row)],
        out_shape=[jax.ShapeDtypeStruct((m, D_MODEL), F32), jax.ShapeDtypeStruct((m, D_MODEL), BF16)],
        compiler_params=_params("parallel"),
        name="merge",
    )(x, mix_g, w_gates, oa, ob, wa, wb, wo, ng, wq)


def _memkv_body(m_ref, g_ref, wk_ref, wv_ref, k_ref, v_ref, k4_ref, v4_ref):
    mn = _rms(m_ref[...], g_ref[...]).astype(BF16)
    k = jnp.dot(mn, wk_ref[...], preferred_element_type=F32)
    v = jnp.dot(mn, wv_ref[...], preferred_element_type=F32)
    k_ref[...] = k
    v_ref[...] = v
    for h in range(X_HEADS):
        k4_ref[:, h, :] = k[:, h * X_HD:(h + 1) * X_HD]
        v4_ref[:, h, :] = v[:, h * X_HD:(h + 1) * X_HD]


def _memkv(mem, g, wk, wv, tm=512):
    m = mem.shape[0]
    row = lambda i: (i, 0)
    row3 = lambda i: (i, 0, 0)
    c2 = lambda i: (0, 0)
    return pl.pallas_call(
        _memkv_body,
        grid=(m // tm,),
        in_specs=[pl.BlockSpec((tm, D_MODEL), row), pl.BlockSpec((1, D_MODEL), c2),
                  pl.BlockSpec((D_MODEL, D_MODEL), c2), pl.BlockSpec((D_MODEL, D_MODEL), c2)],
        out_specs=[pl.BlockSpec((tm, D_MODEL), row), pl.BlockSpec((tm, D_MODEL), row),
                   pl.BlockSpec((tm, X_HEADS, X_HD), row3), pl.BlockSpec((tm, X_HEADS, X_HD), row3)],
        out_shape=[jax.ShapeDtypeStruct((m, D_MODEL), F32)] * 2
        + [jax.ShapeDtypeStruct((m, X_HEADS, X_HD), F32)] * 2,
        compiler_params=_params("parallel"),
        name="memkv",
    )(mem, g, wk, wv)


def _xattn_body(q_ref, mk_ref, mv_ref, x1_ref, wo_ref, x2_ref):
    parts = []
    for h in range(X_HEADS):
        hs = slice(h * X_HD, (h + 1) * X_HD)
        s = _mm_nt(q_ref[0, :, hs], mk_ref[0, :, hs]) * (X_HD ** -0.5)
        p = jnp.exp(s - jnp.max(s, axis=-1, keepdims=True))
        o = _mm(p, mv_ref[0, :, hs]) / jnp.sum(p, axis=-1, keepdims=True)
        parts.append(o.astype(BF16))
    x2_ref[0] = x1_ref[0] + jnp.dot(jnp.concatenate(parts, axis=1), wo_ref[...], preferred_element_type=F32)


def _xattn_prompt(q3, mk3, mv3, x13, wo, tq=1024):
    b, t, _ = q3.shape
    tok = lambda i, j: (i, j, 0)
    mem = lambda i, j: (i, 0, 0)
    return pl.pallas_call(
        _xattn_body,
        grid=(b, t // tq),
        in_specs=[pl.BlockSpec((1, tq, D_MODEL), tok), pl.BlockSpec((1, N_MEM, D_MODEL), mem),
                  pl.BlockSpec((1, N_MEM, D_MODEL), mem), pl.BlockSpec((1, tq, D_MODEL), tok),
                  pl.BlockSpec((D_MODEL, D_MODEL), lambda i, j: (0, 0))],
        out_specs=pl.BlockSpec((1, tq, D_MODEL), tok),
        out_shape=jax.ShapeDtypeStruct((b, t, D_MODEL), F32),
        compiler_params=_params("parallel", "parallel"),
        name="xattn_prompt",
    )(q3, mk3, mv3, x13, wo)


def _xattn_sample_body(q_ref, mk_ref, mv_ref, o_ref):
    for g in range(q_ref.shape[0]):
        q = q_ref[g]
        s = jnp.sum(mk_ref[g] * q[None], axis=-1, keepdims=True) * (X_HD ** -0.5)
        p = jnp.exp(s - jnp.max(s, axis=0, keepdims=True))
        o_ref[g] = jnp.sum(p * mv_ref[g], axis=0) / jnp.sum(p, axis=0)


def _xattn_sample(q, mk4, mv4, gs=8):
    ns = q.shape[0]
    q3 = q.astype(F32).reshape(ns, X_HEADS, X_HD)
    row = lambda i: (i, 0, 0)
    mem = lambda i: (i, 0, 0, 0)
    return pl.pallas_call(
        _xattn_sample_body,
        grid=(ns // gs,),
        in_specs=[pl.BlockSpec((gs, X_HEADS, X_HD), row), pl.BlockSpec((gs, N_MEM, X_HEADS, X_HD), mem),
                  pl.BlockSpec((gs, N_MEM, X_HEADS, X_HD), mem)],
        out_specs=pl.BlockSpec((gs, X_HEADS, X_HD), row),
        out_shape=jax.ShapeDtypeStruct((ns, X_HEADS, X_HD), F32),
        compiler_params=_params("parallel"),
        name="xattn_sample",
    )(q3, mk4, mv4)


def _resid_mm_body(x_ref, a_ref, w_ref, o_ref):
    o_ref[...] = x_ref[...] + jnp.dot(a_ref[...], w_ref[...], preferred_element_type=F32)


def _resid_mm(x, a, w):
    m = x.shape[0]
    return pl.pallas_call(
        _resid_mm_body,
        out_shape=jax.ShapeDtypeStruct((m, D_MODEL), F32),
        compiler_params=pltpu.CompilerParams(vmem_limit_bytes=VMEM_LIMIT),
        name="resid_mm",
    )(x, a, w)


def _ffn_body(x_ref, ng_ref, wg_ref, wu_ref, wd_ref, nf_ref, y_ref):
    x = x_ref[...]
    h = _rms(x, ng_ref[...]).astype(BF16)
    gate = jnp.dot(h, wg_ref[...], preferred_element_type=F32)
    up = jnp.dot(h, wu_ref[...], preferred_element_type=F32)
    x3 = x + _mm(_silu(gate) * up, wd_ref[...])
    y_ref[...] = _rms(x3, nf_ref[...])


def _ffn(x, ng, wg, wu, wd, nf, tm):
    m = x.shape[0]
    dff = wg.shape[1]
    row = lambda i: (i, 0)
    c2 = lambda i: (0, 0)
    single = pl.Buffered(1)
    return pl.pallas_call(
        _ffn_body,
        grid=(m // tm,),
        in_specs=[pl.BlockSpec((tm, D_MODEL), row), pl.BlockSpec((1, D_MODEL), c2),
                  pl.BlockSpec((D_MODEL, dff), c2, pipeline_mode=single),
                  pl.BlockSpec((D_MODEL, dff), c2, pipeline_mode=single),
                  pl.BlockSpec((dff, D_MODEL), c2, pipeline_mode=single),
                  pl.BlockSpec((1, D_MODEL), c2)],
        out_specs=pl.BlockSpec((tm, D_MODEL), row),
        out_shape=jax.ShapeDtypeStruct((m, D_MODEL), F32),
        compiler_params=_params("parallel"),
        name="ffn",
    )(x, ng, wg, wu, wd, nf)


def _rope_tables(pos):
    half = RET_DK // 2
    inv = ROPE_BASE ** (-jnp.arange(half, dtype=F32) / half)
    ang = pos.astype(F32)[:, None] * inv[None, :]
    cos, sin = jnp.cos(ang), jnp.sin(ang)
    return jnp.concatenate([cos, cos], axis=-1), jnp.concatenate([-sin, sin], axis=-1)


def _pad_lanes(v, offset):
    return jnp.zeros((BA_PAD,), F32).at[offset:offset + v.shape[0]].set(v)


def kernel(x_prompt, x_sample, state_ret, state_gdn, state_conv, cache_mem_k, cache_mem_v, mem_prompt,
           norm_mix_g, w_in, ret_gn_g, w_branch_a, gdn_conv_w, gdn_a_log, gdn_dt_bias, gdn_norm_g,
           w_branch_b, w_out, norm_x_g, mem_norm_g, w_xq, w_xk, w_xv, w_xo, norm_ffn_g, w_gate, w_up,
           w_down, norm_final_g):
    depth = w_in.shape[0]
    assert depth == 1, "single-layer kernel"
    b, t, _ = x_prompt.shape
    ns = x_sample.shape[0]
    l = 0

    w = w_in[l]
    ba0, g0 = COL_GATE, COL_GATE + 2 * GDN_HEADS
    w_bf = w.astype(BF16)
    w_gates = w_bf[:, g0:]
    w_ba = jnp.pad(w[:, ba0:g0], ((0, 0), (0, BA_PAD - 2 * GDN_HEADS))).astype(BF16)
    w_bat = w_ba.T
    row = lambda v: v.reshape(1, -1)
    wa, wb, wo = w_branch_a[l].astype(BF16), w_branch_b[l].astype(BF16), w_out[l].astype(BF16)
    wq, wk, wv, wxo = w_xq[l].astype(BF16), w_xk[l].astype(BF16), w_xv[l].astype(BF16), w_xo[l].astype(BF16)
    wg, wu, wd = w_gate[l].astype(BF16), w_up[l].astype(BF16), w_down[l].astype(BF16)
    alog_r = _pad_lanes(gdn_a_log[l], GDN_HEADS).reshape(1, BA_PAD)
    dt_r = _pad_lanes(gdn_dt_bias[l], GDN_HEADS).reshape(1, BA_PAD)
    alog_c, dt_c = alog_r.reshape(BA_PAD, 1), dt_r.reshape(BA_PAD, 1)
    cos_p, sin_p = _rope_tables(jnp.arange(t))
    cos_s, sin_s = _rope_tables(PAST_LEN + jnp.arange(1))

    mix_g = row(norm_mix_g[l])
    xp = x_prompt.reshape(b * t, D_MODEL)
    oa_p, sr_p = _retention_prompt(x_prompt, mix_g, w_bf, cos_p, sin_p, row(ret_gn_g[l]))
    ob_p, sg_p, sc_p = _gdn_prompt(x_prompt, mix_g, w_bf, w_ba, w_bat, gdn_conv_w[l], alog_r, dt_r,
                                   alog_c, dt_c, row(gdn_norm_g[l]))
    x1_p, q_p = _merge(xp, mix_g, w_gates, oa_p.reshape(b * t, -1), ob_p.reshape(b * t, -1), wa, wb, wo,
                       row(norm_x_g[l]), wq, tm=512)
    mk_p, mv_p, mk4_p, mv4_p = _memkv(mem_prompt.reshape(b * N_MEM, D_MODEL), row(mem_norm_g[l]), wk, wv)
    x2_p = _xattn_prompt(q_p.reshape(b, t, D_MODEL), mk_p.reshape(b, N_MEM, D_MODEL),
                         mv_p.reshape(b, N_MEM, D_MODEL), x1_p.reshape(b, t, D_MODEL), wxo)
    y_p = _ffn(x2_p.reshape(b * t, D_MODEL), row(norm_ffn_g[l]), wg, wu, wd, row(norm_final_g), tm=512)

    xs = x_sample.reshape(ns, D_MODEL)
    proj_s, ba_s = _inproj(xs, mix_g, w_bf, COL_GATE, w_ba)
    oa_s, ob_s, sr_s, sg_s, sc_s = _sample_step(proj_s[:, :COL_GDN], proj_s[:, COL_GDN:], ba_s, state_ret[l],
                                                state_gdn[l], state_conv[l],
                                                cos_s, sin_s, gdn_conv_w[l], gdn_a_log[l], gdn_dt_bias[l],
                                                ret_gn_g[l], row(gdn_norm_g[l]))
    x1_s, q_s = _merge(xs, mix_g, w_gates, oa_s.reshape(ns, -1), ob_s.reshape(ns, -1), wa, wb, wo,
                       row(norm_x_g[l]), wq, tm=ns)
    o_s = _xattn_sample(q_s, cache_mem_k[l], cache_mem_v[l])
    x2_s = _resid_mm(x1_s, o_s.reshape(ns, D_MODEL).astype(BF16), wxo)
    y_s = _ffn(x2_s, row(norm_ffn_g[l]), wg, wu, wd, row(norm_final_g), tm=ns)

    return (y_p.reshape(b, t, D_MODEL), y_s.reshape(ns, 1, D_MODEL),
            sr_p[None], sg_p[None], sc_p[None],
            mk4_p.reshape(1, b, N_MEM, X_HEADS, X_HD), mv4_p.reshape(1, b, N_MEM, X_HEADS, X_HD),
            sr_s[None], sg_s[None], sc_s[None])
```

```python
import functools

import numpy as np
import jax
import jax.numpy as jnp
from jax import lax
from jax.experimental import pallas as pl
from jax.experimental.pallas import tpu as pltpu

F32 = jnp.float32
BF16 = jnp.bfloat16

D_MODEL = 1024
RET_HEADS, RET_DK, RET_DV = 4, 128, 256
GDN_HEADS, GDN_DK, GDN_DV = 8, 128, 128
CONV_W = 4
CONV_CH = 3 * GDN_HEADS * GDN_DK
N_MEM, X_HEADS, X_HD = 256, 4, 256
PAST_LEN = 16384
ROPE_BASE = 10000.0
EPS = 1e-6
GDN_CHUNK = 64

COL_RET = 0
COL_GDN = 3072
COL_GATE = 7168
BA_PAD = 128

VMEM_LIMIT = 56 * 1024 * 1024

NT_DIMS = (((1,), (1,)), ((), ()))
TN_DIMS = (((0,), (0,)), ((), ()))


def _mm(a, b):
    return jnp.dot(a.astype(BF16), b.astype(BF16), preferred_element_type=F32)


def _mm_nt(a, b):
    return lax.dot_general(a.astype(BF16), b.astype(BF16), NT_DIMS, preferred_element_type=F32)


def _mm_tn(a, b):
    return lax.dot_general(a.astype(BF16), b.astype(BF16), TN_DIMS, preferred_element_type=F32)


def _split3(x):
    hi = x.astype(BF16)
    r = x - hi.astype(F32)
    mid = r.astype(BF16)
    return hi, mid, (r - mid.astype(F32)).astype(BF16)


def _mm_sel(sel, x):
    return sum(jnp.dot(sel, p, preferred_element_type=F32) for p in _split3(x))


def _mm_sel_r(x, sel):
    return sum(jnp.dot(p, sel, preferred_element_type=F32) for p in _split3(x))


def _rms(x, g):
    return x * lax.rsqrt(jnp.mean(x * x, axis=-1, keepdims=True) + EPS) * g


def _silu(x):
    h = 0.5 * x
    return h + h * jnp.tanh(h)


def _softplus(x):
    return jnp.maximum(x, 0.0) + jnp.log1p(jnp.exp(-jnp.abs(x)))


def _params(*sem):
    return pltpu.CompilerParams(dimension_semantics=sem, vmem_limit_bytes=VMEM_LIMIT)


def _inproj_body(x_ref, g_ref, w_ref, wba_ref, o_ref, oba_ref, h_scr):
    @pl.when(pl.program_id(0) == 0)
    def _():
        hb = _rms(x_ref[...], g_ref[...]).astype(BF16)
        h_scr[...] = hb
        oba_ref[...] = jnp.dot(hb, wba_ref[...], preferred_element_type=F32)

    o_ref[...] = jnp.dot(h_scr[...], w_ref[...], preferred_element_type=F32)


def _inproj(x, g, w_bf, n, w_ba, tn=1024):
    m = x.shape[0]
    assert n % tn == 0 and n <= w_bf.shape[1]
    return pl.pallas_call(
        _inproj_body,
        grid=(n // tn,),
        in_specs=[
            pl.BlockSpec((m, D_MODEL), lambda j: (0, 0)),
            pl.BlockSpec((1, D_MODEL), lambda j: (0, 0)),
            pl.BlockSpec((D_MODEL, tn), lambda j: (0, j)),
            pl.BlockSpec((D_MODEL, BA_PAD), lambda j: (0, 0)),
        ],
        out_specs=[
            pl.BlockSpec((m, tn), lambda j: (0, j)),
            pl.BlockSpec((m, BA_PAD), lambda j: (0, 0)),
        ],
        out_shape=[
            jax.ShapeDtypeStruct((m, n), F32),
            jax.ShapeDtypeStruct((m, BA_PAD), F32),
        ],
        scratch_shapes=[pltpu.VMEM((m, D_MODEL), BF16)],
        compiler_params=_params("arbitrary"),
        name="inproj",
    )(x, g, w_bf, w_ba)


_RET_LOG_G = np.log1p(-np.exp2(-5.0 - np.arange(RET_HEADS, dtype=np.float64)))


def _ret_tables(c):
    idx = np.arange(c, dtype=np.float64)
    diff = idx[:, None] - idx[None, :]
    dmat = np.where(diff >= 0, np.exp(np.maximum(diff, 0.0)[None] * _RET_LOG_G[:, None, None]), 0.0)
    qdec = np.exp((idx + 1.0)[None, :] * _RET_LOG_G[:, None])
    kdec = np.exp((c - 1.0 - idx)[None, :] * _RET_LOG_G[:, None])
    lane = np.ones((1, 1, RET_DK))
    return (jnp.asarray(dmat, F32), jnp.asarray(qdec[:, :, None] * lane, F32),
            jnp.asarray(kdec[:, :, None] * lane, F32), [float(v) for v in np.exp(c * _RET_LOG_G)])


def _rot(x, cos, sin):
    return x * cos + pltpu.roll(x, RET_DK // 2, 1) * sin


def _group_norm_gate(o, gate, gn):
    mu = jnp.mean(o, axis=-1, keepdims=True)
    d = o - mu
    var = jnp.mean(d * d, axis=-1, keepdims=True)
    return _silu(gate) * (d * lax.rsqrt(var + EPS) * gn)


def _proj_tiles(hb, w_ref, dst, width=512):
    for c0 in range(0, w_ref.shape[1], width):
        dst[:, c0:c0 + width] = jnp.dot(hb, w_ref[:, c0:c0 + width], preferred_element_type=F32)


def _next_block(nt, nblocks):
    def index_map(i, j):
        n1 = jnp.minimum(i * nt + j + 1, nblocks - 1)
        return (n1 // nt, n1 % nt, 0)
    return index_map


def _ret_body(cdec, x0_ref, xn_ref, nmg_ref, w_ref, *rest):
    *consts, o_ref, s_out_ref, s_scr, pa, pb = rest
    t = pl.program_id(1)
    n = pl.program_id(0) * pl.num_programs(1) + t
    bufs = (pa, pb)

    @pl.when(t == 0)
    def _():
        s_scr[...] = jnp.zeros_like(s_scr)

    @pl.when(n == 0)
    def _():
        _proj_tiles(_rms(x0_ref[0], nmg_ref[...]).astype(BF16), w_ref, pa)

    for slot in range(2):
        @pl.when(n % 2 == slot)
        def _(slot=slot):
            _ret_step(cdec, t, xn_ref, nmg_ref, w_ref, bufs[slot], bufs[1 - slot], *consts, o_ref, s_out_ref, s_scr)


def _ret_step(cdec, t, xn_ref, nmg_ref, w_ref, proj, proj_next, cos_ref, sin_ref, dmat_ref, qdec_ref, kdec_ref,
              gn_ref, o_ref, s_out_ref, s_scr):
    qw = RET_HEADS * RET_DK
    _proj_tiles(_rms(xn_ref[0], nmg_ref[...]).astype(BF16), w_ref, proj_next)

    chunk = dmat_ref.shape[1]
    for c0 in range(0, proj.shape[0], chunk):
        rows = slice(c0, c0 + chunk)
        cos, sin = cos_ref[rows, :], sin_ref[rows, :]
        for h in range(RET_HEADS):
            qk = slice(h * RET_DK, (h + 1) * RET_DK)
            kk = slice(qw + h * RET_DK, qw + (h + 1) * RET_DK)
            vv = slice(h * RET_DV, (h + 1) * RET_DV)
            q = _rot(proj[rows, qk], cos, sin)
            k = _rot(proj[rows, kk], cos, sin) * (RET_DK ** -0.5)
            v = proj[rows, 2 * qw + h * RET_DV:2 * qw + (h + 1) * RET_DV]
            gate = proj[rows, 2 * qw + (RET_HEADS + h) * RET_DV:2 * qw + (RET_HEADS + h + 1) * RET_DV]
            s = s_scr[h]
            scores = _mm_nt(q, k) * dmat_ref[h]
            o = _mm(scores, v) + _mm(q * qdec_ref[h], s)
            s_scr[h] = cdec[h] * s + _mm_tn(k * kdec_ref[h], v)
            o_ref[0, rows, vv] = _group_norm_gate(o, gate, gn_ref[:, vv]).astype(BF16)

    @pl.when(t == pl.num_programs(1) - 1)
    def _():
        s_out_ref[0] = s_scr[...]


def _retention_prompt(x3, mix_g, w_rg, cos, sin, gn, tb=512, chunk=256):
    b, t, _ = x3.shape
    nt = t // tb
    dmat, qdec, kdec, cdec = _ret_tables(chunk)
    vw = RET_HEADS * RET_DV
    const3 = lambda i, j: (0, 0, 0)
    return pl.pallas_call(
        functools.partial(_ret_body, cdec),
        grid=(b, nt),
        in_specs=[
            pl.BlockSpec((1, tb, D_MODEL), const3),
            pl.BlockSpec((1, tb, D_MODEL), _next_block(nt, b * nt)),
            pl.BlockSpec((1, D_MODEL), lambda i, j: (0, 0)),
            pl.BlockSpec((D_MODEL, COL_GDN), lambda i, j: (0, COL_RET // COL_GDN), pipeline_mode=pl.Buffered(1)),
            pl.BlockSpec((tb, RET_DK), lambda i, j: (j, 0)),
            pl.BlockSpec((tb, RET_DK), lambda i, j: (j, 0)),
            pl.BlockSpec((RET_HEADS, chunk, chunk), const3),
            pl.BlockSpec((RET_HEADS, chunk, RET_DK), const3),
            pl.BlockSpec((RET_HEADS, chunk, RET_DK), const3),
            pl.BlockSpec((1, vw), lambda i, j: (0, 0)),
        ],
        out_specs=[
            pl.BlockSpec((1, tb, vw), lambda i, j: (i, j, 0)),
            pl.BlockSpec((1, RET_HEADS, RET_DK, RET_DV), lambda i, j: (i, 0, 0, 0)),
        ],
        out_shape=[
            jax.ShapeDtypeStruct((b, t, vw), BF16),
            jax.ShapeDtypeStruct((b, RET_HEADS, RET_DK, RET_DV), F32),
        ],
        scratch_shapes=[pltpu.VMEM((RET_HEADS, RET_DK, RET_DV), F32)]
        + 2 * [pltpu.VMEM((tb, COL_GDN), F32)],
        compiler_params=_params("arbitrary", "arbitrary"),
        name="retention_prompt",
    )(x3, x3, mix_g, w_rg, cos, sin, dmat, qdec, kdec, gn)


def _gdn_tables(tb):
    idx = np.arange(tb)
    same = (idx[:, None] // GDN_CHUNK) == (idx[None, :] // GDN_CHUNK)
    lower = same & (idx[:, None] >= idx[None, :])
    nchunk = tb // GDN_CHUNK
    chunk_sel = np.repeat((idx[:, None] // GDN_CHUNK) == np.arange(nchunk)[None, :], 128, axis=1)
    grp = 2 * GDN_CHUNK
    il = (idx % grp)[:, None]
    jl = np.arange(grp)[None, :]
    bias = np.where((il // GDN_CHUNK == jl // GDN_CHUNK) & (il >= jl), 0.0, -1e30)
    eye = (il == jl).astype(np.float64)
    return (jnp.asarray(lower, BF16), jnp.asarray(lower.T, BF16), jnp.asarray(same, BF16),
            jnp.asarray(chunk_sel, BF16), jnp.asarray(bias, F32), jnp.asarray(eye - 1.0, F32),
            jnp.asarray(eye, BF16))


def _gdn_prepare(x_ref, first_of_seq, nmg_ref, wqkv_ref, wz_ref, wba_ref, wbat_ref, cw_ref, xr, cs_buf, zs, bas,
                 bats, tb):
    st = {}

    def start():
        xr[0:8, :] = jnp.where(first_of_seq, 0.0, xr[tb:tb + 8, :])
        st["hb"] = _rms(x_ref[0], nmg_ref[...]).astype(BF16)

    def tile(w_ref, dst, c0, width=512):
        def run():
            dst[:, c0:c0 + width] = jnp.dot(st["hb"], w_ref[:, c0:c0 + width], preferred_element_type=F32)
        return run

    def logits():
        bas[...] = jnp.dot(st["hb"], wba_ref[...], preferred_element_type=F32)
        bats[...] = lax.dot_general(wbat_ref[...], st["hb"], NT_DIMS, preferred_element_type=F32)

    def conv(c0):
        def run():
            cs = slice(c0, c0 + GDN_DK)
            acc = xr[5:5 + tb, cs] * cw_ref[0:1, cs]
            for i in range(1, CONV_W):
                acc = acc + xr[5 + i:5 + i + tb, cs] * cw_ref[i:i + 1, cs]
            cs_buf[:, cs] = _silu(acc)
        return run

    mxu_items = ([start] + [tile(wqkv_ref, xr.at[8:8 + tb], c0) for c0 in range(0, CONV_CH, 512)]
                 + [tile(wz_ref, zs, c0) for c0 in range(0, wz_ref.shape[1], 512)] + [logits])
    vpu_items = [conv(c0) for c0 in range(0, CONV_CH, GDN_DK)]
    return mxu_items, vpu_items


class _Background:
    def __init__(self, items):
        self._items = list(items)

    def __call__(self, count):
        for item in self._items[:count]:
            item()
        del self._items[:count]

    def drain(self):
        self(len(self._items))


def _gdn_body(tb, x0_ref, xn_ref, nmg_ref, wqkv_ref, wz_ref, wba_ref, wbat_ref, cw_ref, *rest):
    *consts, o_ref, s_out_ref, conv_out_ref, s_scr, xr, ca, za, baa, bata, cb, zb, bab, batb = rest
    t = pl.program_id(1)
    nt = pl.num_programs(1)
    n = pl.program_id(0) * nt + t
    weights = (nmg_ref, wqkv_ref, wz_ref, wba_ref, wbat_ref, cw_ref)
    bufs = ((ca, za, baa, bata), (cb, zb, bab, batb))

    @pl.when(t == 0)
    def _():
        s_scr[...] = jnp.zeros_like(s_scr)

    @pl.when(n == 0)
    def _():
        mxu_items, vpu_items = _gdn_prepare(x0_ref, True, *weights, xr, *bufs[0], tb)
        _Background(mxu_items + vpu_items).drain()

    for slot in range(2):
        @pl.when(n % 2 == slot)
        def _(slot=slot):
            mxu_items, vpu_items = _gdn_prepare(xn_ref, t == nt - 1, *weights, xr, *bufs[1 - slot], tb)
            _gdn_step(tb, t, bufs[slot], *consts, o_ref, s_out_ref, s_scr,
                      _Background(mxu_items), _Background(vpu_items))

    @pl.when(t == nt - 2)
    def _():
        conv_out_ref[0] = xr[tb + 5:tb + 8, :]


def _gdn_step(tb, t, cur, alog_r_ref, dt_r_ref, alog_c_ref, dt_c_ref, ng_ref,
              lbd_ref, ubd_ref, obd_ref, csel_ref, bias_ref, offd_ref, eye_ref, o_ref, s_out_ref, s_scr,
              bg_mxu, bg_vpu):
    nchunk = tb // GDN_CHUNK
    hk = GDN_HEADS * GDN_DK
    c_scr, zz, ba_ref, bat_ref = cur
    nh = GDN_HEADS
    ba, bat = ba_ref[...], bat_ref[nh:2 * nh, :]

    bg_mxu(2)
    log2e = 1.4426950408889634
    beta_c = jax.nn.sigmoid(ba)
    g_c = (-log2e * jnp.exp(alog_r_ref[...])) * _softplus(ba + dt_r_ref[...])
    g_r = (-log2e * jnp.exp(alog_c_ref[nh:2 * nh, :])) * _softplus(bat + dt_c_ref[nh:2 * nh, :])
    gc_c = _mm_sel(lbd_ref[...], g_c)
    gt_c = _mm_sel(obd_ref[...], g_c)
    gc_r = _mm_sel_r(g_r, ubd_ref[...])
    gt_l = _mm_sel_r(g_r, csel_ref[...])

    heads = range(GDN_HEADS)
    grp = 2 * GDN_CHUNK
    groups = [slice(p * grp, (p + 1) * grp) for p in range(tb // grp)]

    def grp_dot(a, b):
        return jnp.concatenate([jnp.dot(a[g], b[g], preferred_element_type=F32) for g in groups], axis=0)

    def grp_dot_nt(a, b):
        return jnp.concatenate([lax.dot_general(a[g], b[g], NT_DIMS, preferred_element_type=F32)
                                for g in groups], axis=0)

    bias = bias_ref[...]
    offdiag = offd_ref[...]
    eye_b = eye_ref[...]
    qs, ks, gammas, pbs, rhss, qgs, khs = [], [], [], [], [], [], []
    for h in heads:
        q = c_scr[:, h * GDN_DK:(h + 1) * GDN_DK]
        k = c_scr[:, hk + h * GDN_DK:hk + (h + 1) * GDN_DK]
        v = c_scr[:, 2 * hk + h * GDN_DV:2 * hk + (h + 1) * GDN_DV]
        q = q * lax.rsqrt(jnp.sum(q * q, axis=-1, keepdims=True) + EPS) * (GDN_DK ** -0.5)
        k = k * lax.rsqrt(jnp.sum(k * k, axis=-1, keepdims=True) + EPS)
        beta = beta_c[:, h:h + 1]
        gcc = gc_c[:, 8 + h:9 + h]
        gtc = gt_c[:, 8 + h:9 + h]
        gcr = gc_r[h:h + 1, :]
        dg = jnp.concatenate([gcc[g] - gcr[:, g] for g in groups], axis=0)
        gamma = jnp.exp2(dg + bias)
        kbeta = k * beta
        kb = k.astype(BF16)
        pbs.append((grp_dot_nt(kbeta.astype(BF16), kb) * (gamma * offdiag)).astype(BF16))
        eg = jnp.exp2(gcc)
        rhss.append(jnp.concatenate([v * beta, kbeta * eg], axis=1))
        qs.append(q.astype(BF16))
        ks.append(kb)
        gammas.append(gamma)
        qgs.append(q * eg)
        khs.append(k * jnp.exp2(gtc - gcc))
        bg_mxu(1)
        bg_vpu(1)
    bg_mxu.drain()

    def solve(_, carry):
        pb = pbs
        tbs = [pb[h] + eye_b for h in heads]
        for lvl in range(5):
            pb = [grp_dot(pb[h], pb[h]).astype(BF16) for h in heads]
            bg_vpu(1)
            tnew = [grp_dot(tbs[h], pb[h] + eye_b) for h in heads]
            bg_vpu(1)
            tbs = [x.astype(BF16) for x in tnew]

        us, ws, qks = [], [], []
        for h in heads:
            uw = rhss[h] + grp_dot(tbs[h] - eye_b, rhss[h].astype(BF16))
            us.append(uw[:, :GDN_DV])
            ws.append(uw[:, GDN_DV:])
            qks.append((grp_dot_nt(qs[h], ks[h]) * gammas[h]).astype(BF16))
        bg_vpu(2)

        s = [s_scr[h] for h in heads]
        vn_parts = [[] for _ in heads]
        qs_parts = [[] for _ in heads]
        for c in range(nchunk):
            rows = slice(c * GDN_CHUNK, (c + 1) * GDN_CHUNK)
            for h in heads:
                wq = _mm(jnp.concatenate([ws[h][rows], qgs[h][rows]], axis=0), s[h])
                vn = us[h][rows] - wq[:GDN_CHUNK]
                qs_parts[h].append(wq[GDN_CHUNK:])
                vn_parts[h].append(vn)
                decay = jnp.exp2(gt_l[h:h + 1, c * 128:(c + 1) * 128])
                s[h] = decay * s[h] + _mm_tn(khs[h][rows], vn)
            bg_vpu(1)
        bg_vpu.drain()
        for h in heads:
            hs = slice(h * GDN_DV, (h + 1) * GDN_DV)
            s_scr[h] = s[h]
            vn = jnp.concatenate(vn_parts[h], axis=0).astype(BF16)
            o = jnp.concatenate(qs_parts[h], axis=0) + grp_dot(qks[h], vn)
            o_ref[0, :, hs] = (_rms(o, ng_ref[...]) * _silu(zz[:, hs])).astype(BF16)
        return carry

    lax.fori_loop(0, jnp.minimum(t + 1, 1), solve, 0)

    @pl.when(t == pl.num_programs(1) - 1)
    def _():
        s_out_ref[0] = s_scr[...]


def _gdn_prompt(x3, mix_g, w_rg, w_ba, w_bat, conv_w, alog_r, dt_r, alog_c, dt_c, norm_g, tb=256):
    b, t, _ = x3.shape
    nt = t // tb
    assert t % tb == 0 and nt >= 2, "a sequence's last block must be prepared during one of its own steps"
    lbd, ubd, obd, csel, bias, offdiag, eye = _gdn_tables(tb)
    vw = GDN_HEADS * GDN_DV
    c2 = lambda i, j: (0, 0)
    single = pl.Buffered(1)
    return pl.pallas_call(
        functools.partial(_gdn_body, tb),
        grid=(b, nt),
        in_specs=[
            pl.BlockSpec((1, tb, D_MODEL), lambda i, j: (0, 0, 0)),
            pl.BlockSpec((1, tb, D_MODEL), _next_block(nt, b * nt)),
            pl.BlockSpec((1, D_MODEL), c2),
            pl.BlockSpec((D_MODEL, CONV_CH), lambda i, j: (0, COL_GDN // CONV_CH), pipeline_mode=single),
            pl.BlockSpec((D_MODEL, vw), lambda i, j: (0, (COL_GDN + CONV_CH) // vw), pipeline_mode=single),
            pl.BlockSpec((D_MODEL, BA_PAD), c2),
            pl.BlockSpec((BA_PAD, D_MODEL), c2),
            pl.BlockSpec((CONV_W, CONV_CH), c2),
            pl.BlockSpec((1, BA_PAD), c2),
            pl.BlockSpec((1, BA_PAD), c2),
            pl.BlockSpec((BA_PAD, 1), c2),
            pl.BlockSpec((BA_PAD, 1), c2),
            pl.BlockSpec((1, GDN_DV), c2),
            pl.BlockSpec((tb, tb), c2),
            pl.BlockSpec((tb, tb), c2),
            pl.BlockSpec((tb, tb), c2),
            pl.BlockSpec((tb, (tb // GDN_CHUNK) * 128), c2),
            pl.BlockSpec((tb, 2 * GDN_CHUNK), c2),
            pl.BlockSpec((tb, 2 * GDN_CHUNK), c2),
            pl.BlockSpec((tb, 2 * GDN_CHUNK), c2),
        ],
        out_specs=[
            pl.BlockSpec((1, tb, vw), lambda i, j: (i, j, 0)),
            pl.BlockSpec((1, GDN_HEADS, GDN_DK, GDN_DV), lambda i, j: (i, 0, 0, 0)),
            pl.BlockSpec((1, CONV_W - 1, CONV_CH), lambda i, j: (i, 0, 0)),
        ],
        out_shape=[
            jax.ShapeDtypeStruct((b, t, vw), BF16),
            jax.ShapeDtypeStruct((b, GDN_HEADS, GDN_DK, GDN_DV), F32),
            jax.ShapeDtypeStruct((b, CONV_W - 1, CONV_CH), F32),
        ],
        scratch_shapes=[pltpu.VMEM((GDN_HEADS, GDN_DK, GDN_DV), F32), pltpu.VMEM((tb + 8, CONV_CH), F32)]
        + 2 * [pltpu.VMEM((tb, CONV_CH), F32), pltpu.VMEM((tb, vw), F32),
               pltpu.VMEM((tb, BA_PAD), F32), pltpu.VMEM((BA_PAD, tb), F32)],
        compiler_params=_params("arbitrary", "arbitrary"),
        name="gdn_prompt",
    )(x3, x3, mix_g, w_rg, w_rg, w_ba, w_bat, conv_w, alog_r, dt_r, alog_c, dt_c, norm_g, lbd, ubd, obd, csel,
      bias, offdiag, eye)


def _block_diag_rows(x, nblk):
    row = lax.broadcasted_iota(jnp.int32, (8, nblk * 128), 0)
    blk = lax.broadcasted_iota(jnp.int32, (8, nblk * 128), 1) // 128
    return jnp.where(row == blk, jnp.concatenate([x] * nblk, axis=1), 0.0)


def _sample_step_body(cdec, qk_ref, rv_ref, rg_ref, x_ref, z_ref, bcol_ref, sr_ref, sg_ref, sc_ref, cos_ref, sin_ref,
                      cw_ref, alog_ref, dt_ref, gn_ref, ng_ref,
                      oa_ref, ob_ref, sr_out_ref, sg_out_ref, sc_out_ref):
    nh = GDN_HEADS
    cos, sin = cos_ref[...], sin_ref[...]
    gam = jnp.where(lax.broadcasted_iota(jnp.int32, (8, 1), 0) == 0, cdec[0], 0.0)
    for h in range(1, RET_HEADS):
        gam = jnp.where(lax.broadcasted_iota(jnp.int32, (8, 1), 0) == h, cdec[h], gam)
    zero4 = jnp.zeros((RET_HEADS, RET_DV), F32)
    for g in range(x_ref.shape[0]):
        x_new = x_ref[g]
        buf = sc_ref[g]
        conv = x_new * cw_ref[CONV_W - 1]
        for i in range(CONV_W - 1):
            conv = conv + buf[i] * cw_ref[i]
        conv = _silu(conv)
        sc_out_ref[g, 0] = buf[1]
        sc_out_ref[g, 1] = buf[2]
        sc_out_ref[g, 2] = x_new
        q, k, v = conv[0:nh], conv[nh:2 * nh], conv[2 * nh:3 * nh]
        q = q * lax.rsqrt(jnp.sum(q * q, axis=-1, keepdims=True) + EPS) * (GDN_DK ** -0.5)
        k = k * lax.rsqrt(jnp.sum(k * k, axis=-1, keepdims=True) + EPS)
        col = bcol_ref[g]
        beta = jax.nn.sigmoid(col[0:nh])
        eg = jnp.exp(-jnp.exp(alog_ref[...]) * _softplus(col[nh:2 * nh] + dt_ref[...]))
        kbd, qbd = _block_diag_rows(k, nh), _block_diag_rows(q, nh)
        s_flat = sg_ref[g].reshape(nh * GDN_DK, GDN_DV)
        kq_s = _mm(jnp.concatenate([kbd, qbd], axis=0), s_flat)
        vn = beta * (v - eg * kq_s[0:nh])
        o = eg * kq_s[nh:2 * nh] + jnp.sum(q * k, axis=-1, keepdims=True) * vn
        ob_ref[g] = _rms(o, ng_ref[...]) * _silu(z_ref[g])
        upd = _mm_tn(kbd, vn)
        eg_l = jnp.broadcast_to(eg, (nh, GDN_DV))
        for h in range(nh):
            sg_out_ref[g, h] = eg_l[h:h + 1] * sg_ref[g, h] + upd[h * GDN_DK:(h + 1) * GDN_DK]

        qk = _rot(qk_ref[g], cos, sin)
        rq = jnp.concatenate([qk[0:RET_HEADS], qk[0:RET_HEADS]], axis=0)
        rk = jnp.concatenate([qk[RET_HEADS:], qk[RET_HEADS:]], axis=0) * (RET_DK ** -0.5)
        rv = jnp.concatenate([rv_ref[g], zero4], axis=0)
        qbd_r, kbd_r = _block_diag_rows(rq, RET_HEADS), _block_diag_rows(rk, RET_HEADS)
        sr_flat = sr_ref[g].reshape(RET_HEADS * RET_DK, RET_DV)
        q_s = _mm(qbd_r, sr_flat)
        o_r = gam * q_s + jnp.sum(rq * rk, axis=-1, keepdims=True) * rv
        oa_ref[g] = _group_norm_gate(o_r[0:RET_HEADS], rg_ref[g], gn_ref[...])
        upd_r = _mm_tn(kbd_r, rv)
        for h in range(RET_HEADS):
            sr_out_ref[g, h] = cdec[h] * sr_ref[g, h] + upd_r[h * RET_DK:(h + 1) * RET_DK]


def _sample_step(proj_ret, proj_gdn, ba, state_ret, state_gdn, state_conv, cos, sin, conv_w, a_log, dt_bias, gn,
                 norm_g, gs=8):
    ns = proj_ret.shape[0]
    nh, rh = GDN_HEADS, RET_HEADS
    nq = rh * RET_DK
    qk3 = proj_ret[:, :2 * nq].reshape(ns, 2 * rh, RET_DK)
    rv3 = proj_ret[:, 2 * nq:2 * nq + rh * RET_DV].reshape(ns, rh, RET_DV)
    rg3 = proj_ret[:, 2 * nq + rh * RET_DV:].reshape(ns, rh, RET_DV)
    x3 = proj_gdn[:, :CONV_CH].reshape(ns, 3 * nh, GDN_DK)
    z3 = proj_gdn[:, CONV_CH:].reshape(ns, nh, GDN_DV)
    bcol = ba[:, :2 * nh].reshape(ns, 2 * nh, 1)
    sc4 = state_conv.reshape(ns, CONV_W - 1, 3 * nh, GDN_DK)
    cw3 = conv_w.reshape(CONV_W, 3 * nh, GDN_DK)
    cdec = [float(v) for v in np.exp(_RET_LOG_G)]
    c2 = lambda i: (0, 0)
    c3 = lambda i: (0, 0, 0)
    b3 = lambda i: (i, 0, 0)
    b4 = lambda i: (i, 0, 0, 0)
    oa, ob, sr, sg, sc = pl.pallas_call(
        functools.partial(_sample_step_body, cdec),
        grid=(ns // gs,),
        in_specs=[
            pl.BlockSpec((gs, 2 * rh, RET_DK), b3),
            pl.BlockSpec((gs, rh, RET_DV), b3),
            pl.BlockSpec((gs, rh, RET_DV), b3),
            pl.BlockSpec((gs, 3 * nh, GDN_DK), b3),
            pl.BlockSpec((gs, nh, GDN_DV), b3),
            pl.BlockSpec((gs, 2 * nh, 1), b3),
            pl.BlockSpec((gs, rh, RET_DK, RET_DV), b4),
            pl.BlockSpec((gs, nh, GDN_DK, GDN_DV), b4),
            pl.BlockSpec((gs, CONV_W - 1, 3 * nh, GDN_DK), b4),
            pl.BlockSpec((1, RET_DK), c2),
            pl.BlockSpec((1, RET_DK), c2),
            pl.BlockSpec((CONV_W, 3 * nh, GDN_DK), c3),
            pl.BlockSpec((nh, 1), c2),
            pl.BlockSpec((nh, 1), c2),
            pl.BlockSpec((rh, RET_DV), c2),
            pl.BlockSpec((1, GDN_DV), c2),
        ],
        out_specs=[
            pl.BlockSpec((gs, rh, RET_DV), b3),
            pl.BlockSpec((gs, nh, GDN_DV), b3),
            pl.BlockSpec((gs, rh, RET_DK, RET_DV), b4),
            pl.BlockSpec((gs, nh, GDN_DK, GDN_DV), b4),
            pl.BlockSpec((gs, CONV_W - 1, 3 * nh, GDN_DK), b4),
        ],
        out_shape=[
            jax.ShapeDtypeStruct((ns, rh, RET_DV), F32),
            jax.ShapeDtypeStruct((ns, nh, GDN_DV), F32),
            jax.ShapeDtypeStruct(state_ret.shape, F32),
            jax.ShapeDtypeStruct(state_gdn.shape, F32),
            jax.ShapeDtypeStruct(sc4.shape, F32),
        ],
        compiler_params=_params("parallel"),
        name="sample_step",
    )(qk3, rv3, rg3, x3, z3, bcol, state_ret, state_gdn, sc4, cos, sin, cw3, a_log.reshape(nh, 1),
      dt_bias.reshape(nh, 1), gn.reshape(rh, RET_DV), norm_g)
    return (oa.reshape(ns, rh * RET_DV).astype(BF16), ob.reshape(ns, nh * GDN_DV).astype(BF16), sr, sg,
            sc.reshape(state_conv.shape))


def _merge_body(x_ref, nmg_ref, wgt_ref, oa_ref, ob_ref, wa_ref, wb_ref, wo_ref, ng_ref, wq_ref, x1_ref, q_ref):
    x = x_ref[...]
    gates = jnp.dot(_rms(x, nmg_ref[...]).astype(BF16), wgt_ref[...], preferred_element_type=F32)
    ya = jnp.dot(oa_ref[...], wa_ref[...], preferred_element_type=F32)
    yb = jnp.dot(ob_ref[...], wb_ref[...], preferred_element_type=F32)
    merged = jax.nn.sigmoid(gates[:, :D_MODEL]) * ya + jax.nn.sigmoid(gates[:, D_MODEL:]) * yb
    x1 = x + _mm(merged, wo_ref[...])
    x1_ref[...] = x1
    q_ref[...] = _mm(_rms(x1, ng_ref[...]), wq_ref[...]).astype(BF16)


def _merge(x, mix_g, w_gates, oa, ob, wa, wb, wo, ng, wq, tm):
    m = x.shape[0]
    row = lambda i: (i, 0)
    c2 = lambda i: (0, 0)
    wspec = pl.BlockSpec((D_MODEL, D_MODEL), c2, pipeline_mode=pl.Buffered(1))
    return pl.pallas_call(
        _merge_body,
        grid=(m // tm,),
        in_specs=[
            pl.BlockSpec((tm, D_MODEL), row),
            pl.BlockSpec((1, D_MODEL), c2),
            pl.BlockSpec((D_MODEL, 2 * D_MODEL), c2, pipeline_mode=pl.Buffered(1)),
            pl.BlockSpec((tm, D_MODEL), row),
            pl.BlockSpec((tm, D_MODEL), row),
            wspec, wspec, wspec,
            pl.BlockSpec((1, D_MODEL), c2),
            wspec,
        ],
        out_specs=[pl.BlockSpec((tm, D_MODEL), row), pl.BlockSpec((tm, D_MODEL), row)],
        out_shape=[jax.ShapeDtypeStruct((m, D_MODEL), F32), jax.ShapeDtypeStruct((m, D_MODEL), BF16)],
        compiler_params=_params("parallel"),
        name="merge",
    )(x, mix_g, w_gates, oa, ob, wa, wb, wo, ng, wq)


def _memkv_body(m_ref, g_ref, wk_ref, wv_ref, k_ref, v_ref, k4_ref, v4_ref):
    mn = _rms(m_ref[...], g_ref[...]).astype(BF16)
    k = jnp.dot(mn, wk_ref[...], preferred_element_type=F32)
    v = jnp.dot(mn, wv_ref[...], preferred_element_type=F32)
    k_ref[...] = k
    v_ref[...] = v
    for h in range(X_HEADS):
        k4_ref[:, h, :] = k[:, h * X_HD:(h + 1) * X_HD]
        v4_ref[:, h, :] = v[:, h * X_HD:(h + 1) * X_HD]


def _memkv(mem, g, wk, wv, tm=512):
    m = mem.shape[0]
    row = lambda i: (i, 0)
    row3 = lambda i: (i, 0, 0)
    c2 = lambda i: (0, 0)
    return pl.pallas_call(
        _memkv_body,
        grid=(m // tm,),
        in_specs=[pl.BlockSpec((tm, D_MODEL), row), pl.BlockSpec((1, D_MODEL), c2),
                  pl.BlockSpec((D_MODEL, D_MODEL), c2), pl.BlockSpec((D_MODEL, D_MODEL), c2)],
        out_specs=[pl.BlockSpec((tm, D_MODEL), row), pl.BlockSpec((tm, D_MODEL), row),
                   pl.BlockSpec((tm, X_HEADS, X_HD), row3), pl.BlockSpec((tm, X_HEADS, X_HD), row3)],
        out_shape=[jax.ShapeDtypeStruct((m, D_MODEL), F32)] * 2
        + [jax.ShapeDtypeStruct((m, X_HEADS, X_HD), F32)] * 2,
        compiler_params=_params("parallel"),
        name="memkv",
    )(mem, g, wk, wv)


def _xattn_body(q_ref, mk_ref, mv_ref, x1_ref, wo_ref, x2_ref):
    parts = []
    for h in range(X_HEADS):
        hs = slice(h * X_HD, (h + 1) * X_HD)
        s = _mm_nt(q_ref[0, :, hs], mk_ref[0, :, hs]) * (X_HD ** -0.5)
        p = jnp.exp(s - jnp.max(s, axis=-1, keepdims=True))
        o = _mm(p, mv_ref[0, :, hs]) / jnp.sum(p, axis=-1, keepdims=True)
        parts.append(o.astype(BF16))
    x2_ref[0] = x1_ref[0] + jnp.dot(jnp.concatenate(parts, axis=1), wo_ref[...], preferred_element_type=F32)


def _xattn_prompt(q3, mk3, mv3, x13, wo, tq=1024):
    b, t, _ = q3.shape
    tok = lambda i, j: (i, j, 0)
    mem = lambda i, j: (i, 0, 0)
    return pl.pallas_call(
        _xattn_body,
        grid=(b, t // tq),
        in_specs=[pl.BlockSpec((1, tq, D_MODEL), tok), pl.BlockSpec((1, N_MEM, D_MODEL), mem),
                  pl.BlockSpec((1, N_MEM, D_MODEL), mem), pl.BlockSpec((1, tq, D_MODEL), tok),
                  pl.BlockSpec((D_MODEL, D_MODEL), lambda i, j: (0, 0))],
        out_specs=pl.BlockSpec((1, tq, D_MODEL), tok),
        out_shape=jax.ShapeDtypeStruct((b, t, D_MODEL), F32),
        compiler_params=_params("parallel", "parallel"),
        name="xattn_prompt",
    )(q3, mk3, mv3, x13, wo)


def _xattn_sample_body(q_ref, mk_ref, mv_ref, o_ref):
    for g in range(q_ref.shape[0]):
        q = q_ref[g]
        s = jnp.sum(mk_ref[g] * q[None], axis=-1, keepdims=True) * (X_HD ** -0.5)
        p = jnp.exp(s - jnp.max(s, axis=0, keepdims=True))
        o_ref[g] = jnp.sum(p * mv_ref[g], axis=0) / jnp.sum(p, axis=0)


def _xattn_sample(q, mk4, mv4, gs=8):
    ns = q.shape[0]
    q3 = q.astype(F32).reshape(ns, X_HEADS, X_HD)
    row = lambda i: (i, 0, 0)
    mem = lambda i: (i, 0, 0, 0)
    return pl.pallas_call(
        _xattn_sample_body,
        grid=(ns // gs,),
        in_specs=[pl.BlockSpec((gs, X_HEADS, X_HD), row), pl.BlockSpec((gs, N_MEM, X_HEADS, X_HD), mem),
                  pl.BlockSpec((gs, N_MEM, X_HEADS, X_HD), mem)],
        out_specs=pl.BlockSpec((gs, X_HEADS, X_HD), row),
        out_shape=jax.ShapeDtypeStruct((ns, X_HEADS, X_HD), F32),
        compiler_params=_params("parallel"),
        name="xattn_sample",
    )(q3, mk4, mv4)


def _resid_mm_body(x_ref, a_ref, w_ref, o_ref):
    o_ref[...] = x_ref[...] + jnp.dot(a_ref[...], w_ref[...], preferred_element_type=F32)


def _resid_mm(x, a, w):
    m = x.shape[0]
    return pl.pallas_call(
        _resid_mm_body,
        out_shape=jax.ShapeDtypeStruct((m, D_MODEL), F32),
        compiler_params=pltpu.CompilerParams(vmem_limit_bytes=VMEM_LIMIT),
        name="resid_mm",
    )(x, a, w)


def _ffn_body(x_ref, ng_ref, wg_ref, wu_ref, wd_ref, nf_ref, y_ref):
    x = x_ref[...]
    h = _rms(x, ng_ref[...]).astype(BF16)
    gate = jnp.dot(h, wg_ref[...], preferred_element_type=F32)
    up = jnp.dot(h, wu_ref[...], preferred_element_type=F32)
    x3 = x + _mm(_silu(gate) * up, wd_ref[...])
    y_ref[...] = _rms(x3, nf_ref[...])


def _ffn(x, ng, wg, wu, wd, nf, tm):
    m = x.shape[0]
    dff = wg.shape[1]
    row = lambda i: (i, 0)
    c2 = lambda i: (0, 0)
    single = pl.Buffered(1)
    return pl.pallas_call(
        _ffn_body,
        grid=(m // tm,),
        in_specs=[pl.BlockSpec((tm, D_MODEL), row), pl.BlockSpec((1, D_MODEL), c2),
                  pl.BlockSpec((D_MODEL, dff), c2, pipeline_mode=single),
                  pl.BlockSpec((D_MODEL, dff), c2, pipeline_mode=single),
                  pl.BlockSpec((dff, D_MODEL), c2, pipeline_mode=single),
                  pl.BlockSpec((1, D_MODEL), c2)],
        out_specs=pl.BlockSpec((tm, D_MODEL), row),
        out_shape=jax.ShapeDtypeStruct((m, D_MODEL), F32),
        compiler_params=_params("parallel"),
        name="ffn",
    )(x, ng, wg, wu, wd, nf)


def _rope_tables(pos):
    half = RET_DK // 2
    inv = ROPE_BASE ** (-jnp.arange(half, dtype=F32) / half)
    ang = pos.astype(F32)[:, None] * inv[None, :]
    cos, sin = jnp.cos(ang), jnp.sin(ang)
    return jnp.concatenate([cos, cos], axis=-1), jnp.concatenate([-sin, sin], axis=-1)


def _pad_lanes(v, offset):
    return jnp.zeros((BA_PAD,), F32).at[offset:offset + v.shape[0]].set(v)


def kernel(x_prompt, x_sample, state_ret, state_gdn, state_conv, cache_mem_k, cache_mem_v, mem_prompt,
           norm_mix_g, w_in, ret_gn_g, w_branch_a, gdn_conv_w, gdn_a_log, gdn_dt_bias, gdn_norm_g,
           w_branch_b, w_out, norm_x_g, mem_norm_g, w_xq, w_xk, w_xv, w_xo, norm_ffn_g, w_gate, w_up,
           w_down, norm_final_g):
    depth = w_in.shape[0]
    assert depth == 1, "single-layer kernel"
    b, t, _ = x_prompt.shape
    ns = x_sample.shape[0]
    l = 0

    w = w_in[l]
    ba0, g0 = COL_GATE, COL_GATE + 2 * GDN_HEADS
    w_bf = w.astype(BF16)
    w_gates = w_bf[:, g0:]
    w_ba = jnp.pad(w[:, ba0:g0], ((0, 0), (0, BA_PAD - 2 * GDN_HEADS))).astype(BF16)
    w_bat = w_ba.T
    row = lambda v: v.reshape(1, -1)
    wa, wb, wo = w_branch_a[l].astype(BF16), w_branch_b[l].astype(BF16), w_out[l].astype(BF16)
    wq, wk, wv, wxo = w_xq[l].astype(BF16), w_xk[l].astype(BF16), w_xv[l].astype(BF16), w_xo[l].astype(BF16)
    wg, wu, wd = w_gate[l].astype(BF16), w_up[l].astype(BF16), w_down[l].astype(BF16)
    alog_r = _pad_lanes(gdn_a_log[l], GDN_HEADS).reshape(1, BA_PAD)
    dt_r = _pad_lanes(gdn_dt_bias[l], GDN_HEADS).reshape(1, BA_PAD)
    alog_c, dt_c = alog_r.reshape(BA_PAD, 1), dt_r.reshape(BA_PAD, 1)
    cos_p, sin_p = _rope_tables(jnp.arange(t))
    cos_s, sin_s = _rope_tables(PAST_LEN + jnp.arange(1))

    mix_g = row(norm_mix_g[l])
    xp = x_prompt.reshape(b * t, D_MODEL)
    oa_p, sr_p = _retention_prompt(x_prompt, mix_g, w_bf, cos_p, sin_p, row(ret_gn_g[l]))
    ob_p, sg_p, sc_p = _gdn_prompt(x_prompt, mix_g, w_bf, w_ba, w_bat, gdn_conv_w[l], alog_r, dt_r,
                                   alog_c, dt_c, row(gdn_norm_g[l]))
    x1_p, q_p = _merge(xp, mix_g, w_gates, oa_p.reshape(b * t, -1), ob_p.reshape(b * t, -1), wa, wb, wo,
                       row(norm_x_g[l]), wq, tm=1024)
    mk_p, mv_p, mk4_p, mv4_p = _memkv(mem_prompt.reshape(b * N_MEM, D_MODEL), row(mem_norm_g[l]), wk, wv)
    x2_p = _xattn_prompt(q_p.reshape(b, t, D_MODEL), mk_p.reshape(b, N_MEM, D_MODEL),
                         mv_p.reshape(b, N_MEM, D_MODEL), x1_p.reshape(b, t, D_MODEL), wxo)
    y_p = _ffn(x2_p.reshape(b * t, D_MODEL), row(norm_ffn_g[l]), wg, wu, wd, row(norm_final_g), tm=512)

    xs = x_sample.reshape(ns, D_MODEL)
    proj_s, ba_s = _inproj(xs, mix_g, w_bf, COL_GATE, w_ba)
    oa_s, ob_s, sr_s, sg_s, sc_s = _sample_step(proj_s[:, :COL_GDN], proj_s[:, COL_GDN:], ba_s, state_ret[l],
                                                state_gdn[l], state_conv[l],
                                                cos_s, sin_s, gdn_conv_w[l], gdn_a_log[l], gdn_dt_bias[l],
                                                ret_gn_g[l], row(gdn_norm_g[l]))
    x1_s, q_s = _merge(xs, mix_g, w_gates, oa_s.reshape(ns, -1), ob_s.reshape(ns, -1), wa, wb, wo,
                       row(norm_x_g[l]), wq, tm=ns)
    o_s = _xattn_sample(q_s, cache_mem_k[l], cache_mem_v[l])
    x2_s = _resid_mm(x1_s, o_s.reshape(ns, D_MODEL).astype(BF16), wxo)
    y_s = _ffn(x2_s, row(norm_ffn_g[l]), wg, wu, wd, row(norm_final_g), tm=ns)

    return (y_p.reshape(b, t, D_MODEL), y_s.reshape(ns, 1, D_MODEL),
            sr_p[None], sg_p[None], sc_p[None],
            mk4_p.reshape(1, b, N_MEM, X_HEADS, X_HD), mv4_p.reshape(1, b, N_MEM, X_HEADS, X_HD),
            sr_s[None], sg_s[None], sc_s[None])
```

```python
import functools

import numpy as np
import jax
import jax.numpy as jnp
from jax import lax
from jax.experimental import pallas as pl
from jax.experimental.pallas import tpu as pltpu

F32 = jnp.float32
BF16 = jnp.bfloat16

D_MODEL = 1024
RET_HEADS, RET_DK, RET_DV = 4, 128, 256
GDN_HEADS, GDN_DK, GDN_DV = 8, 128, 128
CONV_W = 4
CONV_CH = 3 * GDN_HEADS * GDN_DK
N_MEM, X_HEADS, X_HD = 256, 4, 256
PAST_LEN = 16384
ROPE_BASE = 10000.0
EPS = 1e-6
GDN_CHUNK = 64

COL_RET = 0
COL_GDN = 3072
COL_GATE = 7168
BA_PAD = 128

VMEM_LIMIT = 56 * 1024 * 1024

NT_DIMS = (((1,), (1,)), ((), ()))
TN_DIMS = (((0,), (0,)), ((), ()))


def _mm(a, b):
    return jnp.dot(a.astype(BF16), b.astype(BF16), preferred_element_type=F32)


def _mm_nt(a, b):
    return lax.dot_general(a.astype(BF16), b.astype(BF16), NT_DIMS, preferred_element_type=F32)


def _mm_tn(a, b):
    return lax.dot_general(a.astype(BF16), b.astype(BF16), TN_DIMS, preferred_element_type=F32)


def _split3(x):
    hi = x.astype(BF16)
    r = x - hi.astype(F32)
    mid = r.astype(BF16)
    return hi, mid, (r - mid.astype(F32)).astype(BF16)


def _mm_sel(sel, x):
    return sum(jnp.dot(sel, p, preferred_element_type=F32) for p in _split3(x))


def _mm_sel_r(x, sel):
    return sum(jnp.dot(p, sel, preferred_element_type=F32) for p in _split3(x))


def _rms(x, g):
    return x * lax.rsqrt(jnp.mean(x * x, axis=-1, keepdims=True) + EPS) * g


def _silu(x):
    h = 0.5 * x
    return h + h * jnp.tanh(h)


def _softplus(x):
    return jnp.maximum(x, 0.0) + jnp.log1p(jnp.exp(-jnp.abs(x)))


def _params(*sem):
    return pltpu.CompilerParams(dimension_semantics=sem, vmem_limit_bytes=VMEM_LIMIT)


def _inproj_body(x_ref, g_ref, w_ref, wba_ref, o_ref, oba_ref, h_scr):
    @pl.when(pl.program_id(0) == 0)
    def _():
        hb = _rms(x_ref[...], g_ref[...]).astype(BF16)
        h_scr[...] = hb
        oba_ref[...] = jnp.dot(hb, wba_ref[...], preferred_element_type=F32)

    o_ref[...] = jnp.dot(h_scr[...], w_ref[...], preferred_element_type=F32)


def _inproj(x, g, w_bf, n, w_ba, tn=1024):
    m = x.shape[0]
    assert n % tn == 0 and n <= w_bf.shape[1]
    return pl.pallas_call(
        _inproj_body,
        grid=(n // tn,),
        in_specs=[
            pl.BlockSpec((m, D_MODEL), lambda j: (0, 0)),
            pl.BlockSpec((1, D_MODEL), lambda j: (0, 0)),
            pl.BlockSpec((D_MODEL, tn), lambda j: (0, j)),
            pl.BlockSpec((D_MODEL, BA_PAD), lambda j: (0, 0)),
        ],
        out_specs=[
            pl.BlockSpec((m, tn), lambda j: (0, j)),
            pl.BlockSpec((m, BA_PAD), lambda j: (0, 0)),
        ],
        out_shape=[
            jax.ShapeDtypeStruct((m, n), F32),
            jax.ShapeDtypeStruct((m, BA_PAD), F32),
        ],
        scratch_shapes=[pltpu.VMEM((m, D_MODEL), BF16)],
        compiler_params=_params("arbitrary"),
        name="inproj",
    )(x, g, w_bf, w_ba)


_RET_LOG_G = np.log1p(-np.exp2(-5.0 - np.arange(RET_HEADS, dtype=np.float64)))


def _ret_tables(c):
    idx = np.arange(c, dtype=np.float64)
    diff = idx[:, None] - idx[None, :]
    dmat = np.where(diff >= 0, np.exp(np.maximum(diff, 0.0)[None] * _RET_LOG_G[:, None, None]), 0.0)
    qdec = np.exp((idx + 1.0)[None, :] * _RET_LOG_G[:, None])
    kdec = np.exp((c - 1.0 - idx)[None, :] * _RET_LOG_G[:, None])
    lane = np.ones((1, 1, RET_DK))
    return (jnp.asarray(dmat, F32), jnp.asarray(qdec[:, :, None] * lane, F32),
            jnp.asarray(kdec[:, :, None] * lane, F32), [float(v) for v in np.exp(c * _RET_LOG_G)])


def _rot(x, cos, sin):
    return x * cos + pltpu.roll(x, RET_DK // 2, 1) * sin


def _group_norm_gate(o, gate, gn):
    mu = jnp.mean(o, axis=-1, keepdims=True)
    d = o - mu
    var = jnp.mean(d * d, axis=-1, keepdims=True)
    return _silu(gate) * (d * lax.rsqrt(var + EPS) * gn)


def _proj_tiles(hb, w_ref, dst, width=512):
    for c0 in range(0, w_ref.shape[1], width):
        dst[:, c0:c0 + width] = jnp.dot(hb, w_ref[:, c0:c0 + width], preferred_element_type=F32)


def _next_block(nt, nblocks):
    def index_map(i, j):
        n1 = jnp.minimum(i * nt + j + 1, nblocks - 1)
        return (n1 // nt, n1 % nt, 0)
    return index_map


def _ret_body(cdec, x0_ref, xn_ref, nmg_ref, w_ref, *rest):
    *consts, o_ref, s_out_ref, s_scr, pa, pb = rest
    t = pl.program_id(1)
    n = pl.program_id(0) * pl.num_programs(1) + t
    bufs = (pa, pb)

    @pl.when(t == 0)
    def _():
        s_scr[...] = jnp.zeros_like(s_scr)

    @pl.when(n == 0)
    def _():
        _proj_tiles(_rms(x0_ref[0], nmg_ref[...]).astype(BF16), w_ref, pa)

    for slot in range(2):
        @pl.when(n % 2 == slot)
        def _(slot=slot):
            _ret_step(cdec, t, xn_ref, nmg_ref, w_ref, bufs[slot], bufs[1 - slot], *consts, o_ref, s_out_ref, s_scr)


def _ret_step(cdec, t, xn_ref, nmg_ref, w_ref, proj, proj_next, cos_ref, sin_ref, dmat_ref, qdec_ref, kdec_ref,
              gn_ref, o_ref, s_out_ref, s_scr):
    qw = RET_HEADS * RET_DK
    hb = _rms(xn_ref[0], nmg_ref[...]).astype(BF16)

    def tile(c0, width=512):
        def run():
            proj_next[:, c0:c0 + width] = jnp.dot(hb, w_ref[:, c0:c0 + width], preferred_element_type=F32)
        return run

    bg = _Background([tile(c0) for c0 in range(0, w_ref.shape[1], 512)])
    bg(1)

    chunk = dmat_ref.shape[1]
    for c0 in range(0, proj.shape[0], chunk):
        rows = slice(c0, c0 + chunk)
        cos, sin = cos_ref[rows, :], sin_ref[rows, :]
        for h in range(RET_HEADS):
            qk = slice(h * RET_DK, (h + 1) * RET_DK)
            kk = slice(qw + h * RET_DK, qw + (h + 1) * RET_DK)
            vv = slice(h * RET_DV, (h + 1) * RET_DV)
            q = _rot(proj[rows, qk], cos, sin)
            k = _rot(proj[rows, kk], cos, sin) * (RET_DK ** -0.5)
            v = proj[rows, 2 * qw + h * RET_DV:2 * qw + (h + 1) * RET_DV]
            gate = proj[rows, 2 * qw + (RET_HEADS + h) * RET_DV:2 * qw + (RET_HEADS + h + 1) * RET_DV]
            s = s_scr[h]
            scores = _mm_nt(q, k) * dmat_ref[h]
            o = _mm(scores, v) + _mm(q * qdec_ref[h], s)
            s_scr[h] = cdec[h] * s + _mm_tn(k * kdec_ref[h], v)
            o_ref[0, rows, vv] = _group_norm_gate(o, gate, gn_ref[:, vv]).astype(BF16)
            bg(1)
    bg.drain()

    @pl.when(t == pl.num_programs(1) - 1)
    def _():
        s_out_ref[0] = s_scr[...]


def _retention_prompt(x3, mix_g, w_rg, cos, sin, gn, tb=512, chunk=256):
    b, t, _ = x3.shape
    nt = t // tb
    dmat, qdec, kdec, cdec = _ret_tables(chunk)
    vw = RET_HEADS * RET_DV
    const3 = lambda i, j: (0, 0, 0)
    return pl.pallas_call(
        functools.partial(_ret_body, cdec),
        grid=(b, nt),
        in_specs=[
            pl.BlockSpec((1, tb, D_MODEL), const3),
            pl.BlockSpec((1, tb, D_MODEL), _next_block(nt, b * nt)),
            pl.BlockSpec((1, D_MODEL), lambda i, j: (0, 0)),
            pl.BlockSpec((D_MODEL, COL_GDN), lambda i, j: (0, COL_RET // COL_GDN), pipeline_mode=pl.Buffered(1)),
            pl.BlockSpec((tb, RET_DK), lambda i, j: (j, 0)),
            pl.BlockSpec((tb, RET_DK), lambda i, j: (j, 0)),
            pl.BlockSpec((RET_HEADS, chunk, chunk), const3),
            pl.BlockSpec((RET_HEADS, chunk, RET_DK), const3),
            pl.BlockSpec((RET_HEADS, chunk, RET_DK), const3),
            pl.BlockSpec((1, vw), lambda i, j: (0, 0)),
        ],
        out_specs=[
            pl.BlockSpec((1, tb, vw), lambda i, j: (i, j, 0)),
            pl.BlockSpec((1, RET_HEADS, RET_DK, RET_DV), lambda i, j: (i, 0, 0, 0)),
        ],
        out_shape=[
            jax.ShapeDtypeStruct((b, t, vw), BF16),
            jax.ShapeDtypeStruct((b, RET_HEADS, RET_DK, RET_DV), F32),
        ],
        scratch_shapes=[pltpu.VMEM((RET_HEADS, RET_DK, RET_DV), F32)]
        + 2 * [pltpu.VMEM((tb, COL_GDN), F32)],
        compiler_params=_params("arbitrary", "arbitrary"),
        name="retention_prompt",
    )(x3, x3, mix_g, w_rg, cos, sin, dmat, qdec, kdec, gn)


def _gdn_tables(tb):
    idx = np.arange(tb)
    same = (idx[:, None] // GDN_CHUNK) == (idx[None, :] // GDN_CHUNK)
    lower = same & (idx[:, None] >= idx[None, :])
    nchunk = tb // GDN_CHUNK
    chunk_sel = np.repeat((idx[:, None] // GDN_CHUNK) == np.arange(nchunk)[None, :], 128, axis=1)
    grp = 2 * GDN_CHUNK
    il = (idx % grp)[:, None]
    jl = np.arange(grp)[None, :]
    bias = np.where((il // GDN_CHUNK == jl // GDN_CHUNK) & (il >= jl), 0.0, -1e30)
    eye = (il == jl).astype(np.float64)
    return (jnp.asarray(lower, BF16), jnp.asarray(lower.T, BF16), jnp.asarray(same, BF16),
            jnp.asarray(chunk_sel, BF16), jnp.asarray(bias, F32), jnp.asarray(eye - 1.0, F32),
            jnp.asarray(eye, BF16))


def _gdn_prepare(x_ref, first_of_seq, nmg_ref, wqkv_ref, wz_ref, wba_ref, wbat_ref, cw_ref, xr, cs_buf, zs, bas,
                 bats, tb):
    st = {}

    def start():
        xr[0:8, :] = jnp.where(first_of_seq, 0.0, xr[tb:tb + 8, :])
        st["hb"] = _rms(x_ref[0], nmg_ref[...]).astype(BF16)

    def tile(w_ref, dst, c0, width=512):
        def run():
            dst[:, c0:c0 + width] = jnp.dot(st["hb"], w_ref[:, c0:c0 + width], preferred_element_type=F32)
        return run

    def logits():
        bas[...] = jnp.dot(st["hb"], wba_ref[...], preferred_element_type=F32)
        bats[...] = lax.dot_general(wbat_ref[...], st["hb"], NT_DIMS, preferred_element_type=F32)

    def conv(c0):
        def run():
            cs = slice(c0, c0 + GDN_DK)
            acc = xr[5:5 + tb, cs] * cw_ref[0:1, cs]
            for i in range(1, CONV_W):
                acc = acc + xr[5 + i:5 + i + tb, cs] * cw_ref[i:i + 1, cs]
            cs_buf[:, cs] = _silu(acc)
        return run

    mxu_items = ([start] + [tile(wqkv_ref, xr.at[8:8 + tb], c0) for c0 in range(0, CONV_CH, 512)]
                 + [tile(wz_ref, zs, c0) for c0 in range(0, wz_ref.shape[1], 512)] + [logits])
    vpu_items = [conv(c0) for c0 in range(0, CONV_CH, GDN_DK)]
    return mxu_items, vpu_items


class _Background:
    def __init__(self, items):
        self._items = list(items)

    def __call__(self, count):
        for item in self._items[:count]:
            item()
        del self._items[:count]

    def drain(self):
        self(len(self._items))


def _gdn_body(tb, x0_ref, xn_ref, nmg_ref, wqkv_ref, wz_ref, wba_ref, wbat_ref, cw_ref, *rest):
    *consts, o_ref, s_out_ref, conv_out_ref, s_scr, xr, ca, za, baa, bata, cb, zb, bab, batb = rest
    t = pl.program_id(1)
    nt = pl.num_programs(1)
    n = pl.program_id(0) * nt + t
    weights = (nmg_ref, wqkv_ref, wz_ref, wba_ref, wbat_ref, cw_ref)
    bufs = ((ca, za, baa, bata), (cb, zb, bab, batb))

    @pl.when(t == 0)
    def _():
        s_scr[...] = jnp.zeros_like(s_scr)

    @pl.when(n == 0)
    def _():
        mxu_items, vpu_items = _gdn_prepare(x0_ref, True, *weights, xr, *bufs[0], tb)
        _Background(mxu_items + vpu_items).drain()

    for slot in range(2):
        @pl.when(n % 2 == slot)
        def _(slot=slot):
            mxu_items, vpu_items = _gdn_prepare(xn_ref, t == nt - 1, *weights, xr, *bufs[1 - slot], tb)
            _gdn_step(tb, t, bufs[slot], *consts, o_ref, s_out_ref, s_scr,
                      _Background(mxu_items), _Background(vpu_items))

    @pl.when(t == nt - 2)
    def _():
        conv_out_ref[0] = xr[tb + 5:tb + 8, :]


def _gdn_step(tb, t, cur, alog_r_ref, dt_r_ref, alog_c_ref, dt_c_ref, ng_ref,
              lbd_ref, ubd_ref, obd_ref, csel_ref, bias_ref, offd_ref, eye_ref, o_ref, s_out_ref, s_scr,
              bg_mxu, bg_vpu):
    nchunk = tb // GDN_CHUNK
    hk = GDN_HEADS * GDN_DK
    c_scr, zz, ba_ref, bat_ref = cur
    nh = GDN_HEADS
    ba, bat = ba_ref[...], bat_ref[nh:2 * nh, :]

    bg_mxu(2)
    log2e = 1.4426950408889634
    beta_c = jax.nn.sigmoid(ba)
    g_c = (-log2e * jnp.exp(alog_r_ref[...])) * _softplus(ba + dt_r_ref[...])
    g_r = (-log2e * jnp.exp(alog_c_ref[nh:2 * nh, :])) * _softplus(bat + dt_c_ref[nh:2 * nh, :])
    gc_c = _mm_sel(lbd_ref[...], g_c)
    gt_c = _mm_sel(obd_ref[...], g_c)
    gc_r = _mm_sel_r(g_r, ubd_ref[...])
    gt_l = _mm_sel_r(g_r, csel_ref[...])

    heads = range(GDN_HEADS)
    grp = 2 * GDN_CHUNK
    groups = [slice(p * grp, (p + 1) * grp) for p in range(tb // grp)]

    def grp_dot(a, b):
        return jnp.concatenate([jnp.dot(a[g], b[g], preferred_element_type=F32) for g in groups], axis=0)

    def grp_dot_nt(a, b):
        return jnp.concatenate([lax.dot_general(a[g], b[g], NT_DIMS, preferred_element_type=F32)
                                for g in groups], axis=0)

    bias = bias_ref[...]
    offdiag = offd_ref[...]
    eye_b = eye_ref[...]
    qs, ks, gammas, pbs, rhss, qgs, khs = [], [], [], [], [], [], []
    for h in heads:
        q = c_scr[:, h * GDN_DK:(h + 1) * GDN_DK]
        k = c_scr[:, hk + h * GDN_DK:hk + (h + 1) * GDN_DK]
        v = c_scr[:, 2 * hk + h * GDN_DV:2 * hk + (h + 1) * GDN_DV]
        q = q * lax.rsqrt(jnp.sum(q * q, axis=-1, keepdims=True) + EPS) * (GDN_DK ** -0.5)
        k = k * lax.rsqrt(jnp.sum(k * k, axis=-1, keepdims=True) + EPS)
        beta = beta_c[:, h:h + 1]
        gcc = gc_c[:, 8 + h:9 + h]
        gtc = gt_c[:, 8 + h:9 + h]
        gcr = gc_r[h:h + 1, :]
        dg = jnp.concatenate([gcc[g] - gcr[:, g] for g in groups], axis=0)
        gamma = jnp.exp2(dg + bias)
        kbeta = k * beta
        kb = k.astype(BF16)
        pbs.append((grp_dot_nt(kbeta.astype(BF16), kb) * (gamma * offdiag)).astype(BF16))
        eg = jnp.exp2(gcc)
        rhss.append(jnp.concatenate([v * beta, kbeta * eg], axis=1))
        qs.append(q.astype(BF16))
        ks.append(kb)
        gammas.append(gamma)
        qgs.append(q * eg)
        khs.append(k * jnp.exp2(gtc - gcc))
        bg_mxu(1)
        bg_vpu(1)
    bg_mxu.drain()

    def solve(_, carry):
        pb = pbs
        tbs = [pb[h] + eye_b for h in heads]
        for lvl in range(5):
            pb = [grp_dot(pb[h], pb[h]).astype(BF16) for h in heads]
            bg_vpu(1)
            tnew = [grp_dot(tbs[h], pb[h] + eye_b) for h in heads]
            bg_vpu(1)
            tbs = [x.astype(BF16) for x in tnew]

        us, ws, qks = [], [], []
        for h in heads:
            uw = rhss[h] + grp_dot(tbs[h] - eye_b, rhss[h].astype(BF16))
            us.append(uw[:, :GDN_DV])
            ws.append(uw[:, GDN_DV:])
            qks.append((grp_dot_nt(qs[h], ks[h]) * gammas[h]).astype(BF16))
        bg_vpu(2)

        s = [s_scr[h] for h in heads]
        vn_parts = [[] for _ in heads]
        qs_parts = [[] for _ in heads]
        for c in range(nchunk):
            rows = slice(c * GDN_CHUNK, (c + 1) * GDN_CHUNK)
            for h in heads:
                wq = _mm(jnp.concatenate([ws[h][rows], qgs[h][rows]], axis=0), s[h])
                vn = us[h][rows] - wq[:GDN_CHUNK]
                qs_parts[h].append(wq[GDN_CHUNK:])
                vn_parts[h].append(vn)
                decay = jnp.exp2(gt_l[h:h + 1, c * 128:(c + 1) * 128])
                s[h] = decay * s[h] + _mm_tn(khs[h][rows], vn)
            bg_vpu(1)
        bg_vpu.drain()
        for h in heads:
            hs = slice(h * GDN_DV, (h + 1) * GDN_DV)
            s_scr[h] = s[h]
            vn = jnp.concatenate(vn_parts[h], axis=0).astype(BF16)
            o = jnp.concatenate(qs_parts[h], axis=0) + grp_dot(qks[h], vn)
            o_ref[0, :, hs] = (_rms(o, ng_ref[...]) * _silu(zz[:, hs])).astype(BF16)
        return carry

    lax.fori_loop(0, jnp.minimum(t + 1, 1), solve, 0)

    @pl.when(t == pl.num_programs(1) - 1)
    def _():
        s_out_ref[0] = s_scr[...]


def _gdn_prompt(x3, mix_g, w_rg, w_ba, w_bat, conv_w, alog_r, dt_r, alog_c, dt_c, norm_g, tb=256):
    b, t, _ = x3.shape
    nt = t // tb
    assert t % tb == 0 and nt >= 2, "a sequence's last block must be prepared during one of its own steps"
    lbd, ubd, obd, csel, bias, offdiag, eye = _gdn_tables(tb)
    vw = GDN_HEADS * GDN_DV
    c2 = lambda i, j: (0, 0)
    single = pl.Buffered(1)
    return pl.pallas_call(
        functools.partial(_gdn_body, tb),
        grid=(b, nt),
        in_specs=[
            pl.BlockSpec((1, tb, D_MODEL), lambda i, j: (0, 0, 0)),
            pl.BlockSpec((1, tb, D_MODEL), _next_block(nt, b * nt)),
            pl.BlockSpec((1, D_MODEL), c2),
            pl.BlockSpec((D_MODEL, CONV_CH), lambda i, j: (0, COL_GDN // CONV_CH), pipeline_mode=single),
            pl.BlockSpec((D_MODEL, vw), lambda i, j: (0, (COL_GDN + CONV_CH) // vw), pipeline_mode=single),
            pl.BlockSpec((D_MODEL, BA_PAD), c2),
            pl.BlockSpec((BA_PAD, D_MODEL), c2),
            pl.BlockSpec((CONV_W, CONV_CH), c2),
            pl.BlockSpec((1, BA_PAD), c2),
            pl.BlockSpec((1, BA_PAD), c2),
            pl.BlockSpec((BA_PAD, 1), c2),
            pl.BlockSpec((BA_PAD, 1), c2),
            pl.BlockSpec((1, GDN_DV), c2),
            pl.BlockSpec((tb, tb), c2),
            pl.BlockSpec((tb, tb), c2),
            pl.BlockSpec((tb, tb), c2),
            pl.BlockSpec((tb, (tb // GDN_CHUNK) * 128), c2),
            pl.BlockSpec((tb, 2 * GDN_CHUNK), c2),
            pl.BlockSpec((tb, 2 * GDN_CHUNK), c2),
            pl.BlockSpec((tb, 2 * GDN_CHUNK), c2),
        ],
        out_specs=[
            pl.BlockSpec((1, tb, vw), lambda i, j: (i, j, 0)),
            pl.BlockSpec((1, GDN_HEADS, GDN_DK, GDN_DV), lambda i, j: (i, 0, 0, 0)),
            pl.BlockSpec((1, CONV_W - 1, CONV_CH), lambda i, j: (i, 0, 0)),
        ],
        out_shape=[
            jax.ShapeDtypeStruct((b, t, vw), BF16),
            jax.ShapeDtypeStruct((b, GDN_HEADS, GDN_DK, GDN_DV), F32),
            jax.ShapeDtypeStruct((b, CONV_W - 1, CONV_CH), F32),
        ],
        scratch_shapes=[pltpu.VMEM((GDN_HEADS, GDN_DK, GDN_DV), F32), pltpu.VMEM((tb + 8, CONV_CH), F32)]
        + 2 * [pltpu.VMEM((tb, CONV_CH), F32), pltpu.VMEM((tb, vw), F32),
               pltpu.VMEM((tb, BA_PAD), F32), pltpu.VMEM((BA_PAD, tb), F32)],
        compiler_params=_params("arbitrary", "arbitrary"),
        name="gdn_prompt",
    )(x3, x3, mix_g, w_rg, w_rg, w_ba, w_bat, conv_w, alog_r, dt_r, alog_c, dt_c, norm_g, lbd, ubd, obd, csel,
      bias, offdiag, eye)


def _block_diag_rows(x, nblk):
    row = lax.broadcasted_iota(jnp.int32, (8, nblk * 128), 0)
    blk = lax.broadcasted_iota(jnp.int32, (8, nblk * 128), 1) // 128
    return jnp.where(row == blk, jnp.concatenate([x] * nblk, axis=1), 0.0)


def _sample_step_body(cdec, qk_ref, rv_ref, rg_ref, x_ref, z_ref, bcol_ref, sr_ref, sg_ref, sc_ref, cos_ref, sin_ref,
                      cw_ref, alog_ref, dt_ref, gn_ref, ng_ref,
                      oa_ref, ob_ref, sr_out_ref, sg_out_ref, sc_out_ref):
    nh = GDN_HEADS
    cos, sin = cos_ref[...], sin_ref[...]
    gam = jnp.where(lax.broadcasted_iota(jnp.int32, (8, 1), 0) == 0, cdec[0], 0.0)
    for h in range(1, RET_HEADS):
        gam = jnp.where(lax.broadcasted_iota(jnp.int32, (8, 1), 0) == h, cdec[h], gam)
    zero4 = jnp.zeros((RET_HEADS, RET_DV), F32)
    for g in range(x_ref.shape[0]):
        x_new = x_ref[g]
        buf = sc_ref[g]
        conv = x_new * cw_ref[CONV_W - 1]
        for i in range(CONV_W - 1):
            conv = conv + buf[i] * cw_ref[i]
        conv = _silu(conv)
        sc_out_ref[g, 0] = buf[1]
        sc_out_ref[g, 1] = buf[2]
        sc_out_ref[g, 2] = x_new
        q, k, v = conv[0:nh], conv[nh:2 * nh], conv[2 * nh:3 * nh]
        q = q * lax.rsqrt(jnp.sum(q * q, axis=-1, keepdims=True) + EPS) * (GDN_DK ** -0.5)
        k = k * lax.rsqrt(jnp.sum(k * k, axis=-1, keepdims=True) + EPS)
        col = bcol_ref[g]
        beta = jax.nn.sigmoid(col[0:nh])
        eg = jnp.exp(-jnp.exp(alog_ref[...]) * _softplus(col[nh:2 * nh] + dt_ref[...]))
        kbd, qbd = _block_diag_rows(k, nh), _block_diag_rows(q, nh)
        s_flat = sg_ref[g].reshape(nh * GDN_DK, GDN_DV)
        kq_s = _mm(jnp.concatenate([kbd, qbd], axis=0), s_flat)
        vn = beta * (v - eg * kq_s[0:nh])
        o = eg * kq_s[nh:2 * nh] + jnp.sum(q * k, axis=-1, keepdims=True) * vn
        ob_ref[g] = _rms(o, ng_ref[...]) * _silu(z_ref[g])
        upd = _mm_tn(kbd, vn)
        eg_l = jnp.broadcast_to(eg, (nh, GDN_DV))
        for h in range(nh):
            sg_out_ref[g, h] = eg_l[h:h + 1] * sg_ref[g, h] + upd[h * GDN_DK:(h + 1) * GDN_DK]

        qk = _rot(qk_ref[g], cos, sin)
        rq = jnp.concatenate([qk[0:RET_HEADS], qk[0:RET_HEADS]], axis=0)
        rk = jnp.concatenate([qk[RET_HEADS:], qk[RET_HEADS:]], axis=0) * (RET_DK ** -0.5)
        rv = jnp.concatenate([rv_ref[g], zero4], axis=0)
        qbd_r, kbd_r = _block_diag_rows(rq, RET_HEADS), _block_diag_rows(rk, RET_HEADS)
        sr_flat = sr_ref[g].reshape(RET_HEADS * RET_DK, RET_DV)
        q_s = _mm(qbd_r, sr_flat)
        o_r = gam * q_s + jnp.sum(rq * rk, axis=-1, keepdims=True) * rv
        oa_ref[g] = _group_norm_gate(o_r[0:RET_HEADS], rg_ref[g], gn_ref[...])
        upd_r = _mm_tn(kbd_r, rv)
        for h in range(RET_HEADS):
            sr_out_ref[g, h] = cdec[h] * sr_ref[g, h] + upd_r[h * RET_DK:(h + 1) * RET_DK]


def _sample_step(proj_ret, proj_gdn, ba, state_ret, state_gdn, state_conv, cos, sin, conv_w, a_log, dt_bias, gn,
                 norm_g, gs=8):
    ns = proj_ret.shape[0]
    nh, rh = GDN_HEADS, RET_HEADS
    nq = rh * RET_DK
    qk3 = proj_ret[:, :2 * nq].reshape(ns, 2 * rh, RET_DK)
    rv3 = proj_ret[:, 2 * nq:2 * nq + rh * RET_DV].reshape(ns, rh, RET_DV)
    rg3 = proj_ret[:, 2 * nq + rh * RET_DV:].reshape(ns, rh, RET_DV)
    x3 = proj_gdn[:, :CONV_CH].reshape(ns, 3 * nh, GDN_DK)
    z3 = proj_gdn[:, CONV_CH:].reshape(ns, nh, GDN_DV)
    bcol = ba[:, :2 * nh].reshape(ns, 2 * nh, 1)
    sc4 = state_conv.reshape(ns, CONV_W - 1, 3 * nh, GDN_DK)
    cw3 = conv_w.reshape(CONV_W, 3 * nh, GDN_DK)
    cdec = [float(v) for v in np.exp(_RET_LOG_G)]
    c2 = lambda i: (0, 0)
    c3 = lambda i: (0, 0, 0)
    b3 = lambda i: (i, 0, 0)
    b4 = lambda i: (i, 0, 0, 0)
    oa, ob, sr, sg, sc = pl.pallas_call(
        functools.partial(_sample_step_body, cdec),
        grid=(ns // gs,),
        in_specs=[
            pl.BlockSpec((gs, 2 * rh, RET_DK), b3),
            pl.BlockSpec((gs, rh, RET_DV), b3),
            pl.BlockSpec((gs, rh, RET_DV), b3),
            pl.BlockSpec((gs, 3 * nh, GDN_DK), b3),
            pl.BlockSpec((gs, nh, GDN_DV), b3),
            pl.BlockSpec((gs, 2 * nh, 1), b3),
            pl.BlockSpec((gs, rh, RET_DK, RET_DV), b4),
            pl.BlockSpec((gs, nh, GDN_DK, GDN_DV), b4),
            pl.BlockSpec((gs, CONV_W - 1, 3 * nh, GDN_DK), b4),
            pl.BlockSpec((1, RET_DK), c2),
            pl.BlockSpec((1, RET_DK), c2),
            pl.BlockSpec((CONV_W, 3 * nh, GDN_DK), c3),
            pl.BlockSpec((nh, 1), c2),
            pl.BlockSpec((nh, 1), c2),
            pl.BlockSpec((rh, RET_DV), c2),
            pl.BlockSpec((1, GDN_DV), c2),
        ],
        out_specs=[
            pl.BlockSpec((gs, rh, RET_DV), b3),
            pl.BlockSpec((gs, nh, GDN_DV), b3),
            pl.BlockSpec((gs, rh, RET_DK, RET_DV), b4),
            pl.BlockSpec((gs, nh, GDN_DK, GDN_DV), b4),
            pl.BlockSpec((gs, CONV_W - 1, 3 * nh, GDN_DK), b4),
        ],
        out_shape=[
            jax.ShapeDtypeStruct((ns, rh, RET_DV), F32),
            jax.ShapeDtypeStruct((ns, nh, GDN_DV), F32),
            jax.ShapeDtypeStruct(state_ret.shape, F32),
            jax.ShapeDtypeStruct(state_gdn.shape, F32),
            jax.ShapeDtypeStruct(sc4.shape, F32),
        ],
        compiler_params=_params("parallel"),
        name="sample_step",
    )(qk3, rv3, rg3, x3, z3, bcol, state_ret, state_gdn, sc4, cos, sin, cw3, a_log.reshape(nh, 1),
      dt_bias.reshape(nh, 1), gn.reshape(rh, RET_DV), norm_g)
    return (oa.reshape(ns, rh * RET_DV).astype(BF16), ob.reshape(ns, nh * GDN_DV).astype(BF16), sr, sg,
            sc.reshape(state_conv.shape))


def _merge_body(x_ref, nmg_ref, wgt_ref, oa_ref, ob_ref, wa_ref, wb_ref, wo_ref, ng_ref, wq_ref, x1_ref, q_ref):
    x = x_ref[...]
    gates = jnp.dot(_rms(x, nmg_ref[...]).astype(BF16), wgt_ref[...], preferred_element_type=F32)
    ya = jnp.dot(oa_ref[...], wa_ref[...], preferred_element_type=F32)
    yb = jnp.dot(ob_ref[...], wb_ref[...], preferred_element_type=F32)
    merged = jax.nn.sigmoid(gates[:, :D_MODEL]) * ya + jax.nn.sigmoid(gates[:, D_MODEL:]) * yb
    x1 = x + _mm(merged, wo_ref[...])
    x1_ref[...] = x1
    q_ref[...] = _mm(_rms(x1, ng_ref[...]), wq_ref[...]).astype(BF16)


def _merge(x, mix_g, w_gates, oa, ob, wa, wb, wo, ng, wq, tm):
    m = x.shape[0]
    row = lambda i: (i, 0)
    c2 = lambda i: (0, 0)
    wspec = pl.BlockSpec((D_MODEL, D_MODEL), c2, pipeline_mode=pl.Buffered(1))
    return pl.pallas_call(
        _merge_body,
        grid=(m // tm,),
        in_specs=[
            pl.BlockSpec((tm, D_MODEL), row),
            pl.BlockSpec((1, D_MODEL), c2),
            pl.BlockSpec((D_MODEL, 2 * D_MODEL), c2, pipeline_mode=pl.Buffered(1)),
            pl.BlockSpec((tm, D_MODEL), row),
            pl.BlockSpec((tm, D_MODEL), row),
            wspec, wspec, wspec,
            pl.BlockSpec((1, D_MODEL), c2),
            wspec,
        ],
        out_specs=[pl.BlockSpec((tm, D_MODEL), row), pl.BlockSpec((tm, D_MODEL), row)],
        out_shape=[jax.ShapeDtypeStruct((m, D_MODEL), F32), jax.ShapeDtypeStruct((m, D_MODEL), BF16)],
        compiler_params=_params("parallel"),
        name="merge",
    )(x, mix_g, w_gates, oa, ob, wa, wb, wo, ng, wq)


def _memkv_body(m_ref, g_ref, wk_ref, wv_ref, k_ref, v_ref, k4_ref, v4_ref):
    mn = _rms(m_ref[...], g_ref[...]).astype(BF16)
    k = jnp.dot(mn, wk_ref[...], preferred_element_type=F32)
    v = jnp.dot(mn, wv_ref[...], preferred_element_type=F32)
    k_ref[...] = k
    v_ref[...] = v
    for h in range(X_HEADS):
        k4_ref[:, h, :] = k[:, h * X_HD:(h + 1) * X_HD]
        v4_ref[:, h, :] = v[:, h * X_HD:(h + 1) * X_HD]


def _memkv(mem, g, wk, wv, tm=512):
    m = mem.shape[0]
    row = lambda i: (i, 0)
    row3 = lambda i: (i, 0, 0)
    c2 = lambda i: (0, 0)
    return pl.pallas_call(
        _memkv_body,
        grid=(m // tm,),
        in_specs=[pl.BlockSpec((tm, D_MODEL), row), pl.BlockSpec((1, D_MODEL), c2),
                  pl.BlockSpec((D_MODEL, D_MODEL), c2), pl.BlockSpec((D_MODEL, D_MODEL), c2)],
        out_specs=[pl.BlockSpec((tm, D_MODEL), row), pl.BlockSpec((tm, D_MODEL), row),
                   pl.BlockSpec((tm, X_HEADS, X_HD), row3), pl.BlockSpec((tm, X_HEADS, X_HD), row3)],
        out_shape=[jax.ShapeDtypeStruct((m, D_MODEL), F32)] * 2
        + [jax.ShapeDtypeStruct((m, X_HEADS, X_HD), F32)] * 2,
        compiler_params=_params("parallel"),
        name="memkv",
    )(mem, g, wk, wv)


def _xattn_body(q_ref, mk_ref, mv_ref, x1_ref, wo_ref, x2_ref):
    parts = []
    for h in range(X_HEADS):
        hs = slice(h * X_HD, (h + 1) * X_HD)
        s = _mm_nt(q_ref[0, :, hs], mk_ref[0, :, hs]) * (X_HD ** -0.5)
        p = jnp.exp(s - jnp.max(s, axis=-1, keepdims=True))
        o = _mm(p, mv_ref[0, :, hs]) / jnp.sum(p, axis=-1, keepdims=True)
        parts.append(o.astype(BF16))
    x2_ref[0] = x1_ref[0] + jnp.dot(jnp.concatenate(parts, axis=1), wo_ref[...], preferred_element_type=F32)


def _xattn_prompt(q3, mk3, mv3, x13, wo, tq=1024):
    b, t, _ = q3.shape
    tok = lambda i, j: (i, j, 0)
    mem = lambda i, j: (i, 0, 0)
    return pl.pallas_call(
        _xattn_body,
        grid=(b, t // tq),
        in_specs=[pl.BlockSpec((1, tq, D_MODEL), tok), pl.BlockSpec((1, N_MEM, D_MODEL), mem),
                  pl.BlockSpec((1, N_MEM, D_MODEL), mem), pl.BlockSpec((1, tq, D_MODEL), tok),
                  pl.BlockSpec((D_MODEL, D_MODEL), lambda i, j: (0, 0))],
        out_specs=pl.BlockSpec((1, tq, D_MODEL), tok),
        out_shape=jax.ShapeDtypeStruct((b, t, D_MODEL), F32),
        compiler_params=_params("parallel", "parallel"),
        name="xattn_prompt",
    )(q3, mk3, mv3, x13, wo)


def _xattn_sample_body(q_ref, mk_ref, mv_ref, o_ref):
    for g in range(q_ref.shape[0]):
        q = q_ref[g]
        s = jnp.sum(mk_ref[g] * q[None], axis=-1, keepdims=True) * (X_HD ** -0.5)
        p = jnp.exp(s - jnp.max(s, axis=0, keepdims=True))
        o_ref[g] = jnp.sum(p * mv_ref[g], axis=0) / jnp.sum(p, axis=0)


def _xattn_sample(q, mk4, mv4, gs=8):
    ns = q.shape[0]
    q3 = q.astype(F32).reshape(ns, X_HEADS, X_HD)
    row = lambda i: (i, 0, 0)
    mem = lambda i: (i, 0, 0, 0)
    return pl.pallas_call(
        _xattn_sample_body,
        grid=(ns // gs,),
        in_specs=[pl.BlockSpec((gs, X_HEADS, X_HD), row), pl.BlockSpec((gs, N_MEM, X_HEADS, X_HD), mem),
                  pl.BlockSpec((gs, N_MEM, X_HEADS, X_HD), mem)],
        out_specs=pl.BlockSpec((gs, X_HEADS, X_HD), row),
        out_shape=jax.ShapeDtypeStruct((ns, X_HEADS, X_HD), F32),
        compiler_params=_params("parallel"),
        name="xattn_sample",
    )(q3, mk4, mv4)


def _resid_mm_body(x_ref, a_ref, w_ref, o_ref):
    o_ref[...] = x_ref[...] + jnp.dot(a_ref[...], w_ref[...], preferred_element_type=F32)


def _resid_mm(x, a, w):
    m = x.shape[0]
    return pl.pallas_call(
        _resid_mm_body,
        out_shape=jax.ShapeDtypeStruct((m, D_MODEL), F32),
        compiler_params=pltpu.CompilerParams(vmem_limit_bytes=VMEM_LIMIT),
        name="resid_mm",
    )(x, a, w)


def _ffn_body(x_ref, ng_ref, wg_ref, wu_ref, wd_ref, nf_ref, y_ref):
    x = x_ref[...]
    h = _rms(x, ng_ref[...]).astype(BF16)
    gate = jnp.dot(h, wg_ref[...], preferred_element_type=F32)
    up = jnp.dot(h, wu_ref[...], preferred_element_type=F32)
    x3 = x + _mm(_silu(gate) * up, wd_ref[...])
    y_ref[...] = _rms(x3, nf_ref[...])


def _ffn(x, ng, wg, wu, wd, nf, tm):
    m = x.shape[0]
    dff = wg.shape[1]
    row = lambda i: (i, 0)
    c2 = lambda i: (0, 0)
    single = pl.Buffered(1)
    return pl.pallas_call(
        _ffn_body,
        grid=(m // tm,),
        in_specs=[pl.BlockSpec((tm, D_MODEL), row), pl.BlockSpec((1, D_MODEL), c2),
                  pl.BlockSpec((D_MODEL, dff), c2, pipeline_mode=single),
                  pl.BlockSpec((D_MODEL, dff), c2, pipeline_mode=single),
                  pl.BlockSpec((dff, D_MODEL), c2, pipeline_mode=single),
                  pl.BlockSpec((1, D_MODEL), c2)],
        out_specs=pl.BlockSpec((tm, D_MODEL), row),
        out_shape=jax.ShapeDtypeStruct((m, D_MODEL), F32),
        compiler_params=_params("parallel"),
        name="ffn",
    )(x, ng, wg, wu, wd, nf)


def _rope_tables(pos):
    half = RET_DK // 2
    inv = ROPE_BASE ** (-jnp.arange(half, dtype=F32) / half)
    ang = pos.astype(F32)[:, None] * inv[None, :]
    cos, sin = jnp.cos(ang), jnp.sin(ang)
    return jnp.concatenate([cos, cos], axis=-1), jnp.concatenate([-sin, sin], axis=-1)


def _pad_lanes(v, offset):
    return jnp.zeros((BA_PAD,), F32).at[offset:offset + v.shape[0]].set(v)


def kernel(x_prompt, x_sample, state_ret, state_gdn, state_conv, cache_mem_k, cache_mem_v, mem_prompt,
           norm_mix_g, w_in, ret_gn_g, w_branch_a, gdn_conv_w, gdn_a_log, gdn_dt_bias, gdn_norm_g,
           w_branch_b, w_out, norm_x_g, mem_norm_g, w_xq, w_xk, w_xv, w_xo, norm_ffn_g, w_gate, w_up,
           w_down, norm_final_g):
    depth = w_in.shape[0]
    assert depth == 1, "single-layer kernel"
    b, t, _ = x_prompt.shape
    ns = x_sample.shape[0]
    l = 0

    w = w_in[l]
    ba0, g0 = COL_GATE, COL_GATE + 2 * GDN_HEADS
    w_bf = w.astype(BF16)
    w_gates = w_bf[:, g0:]
    w_ba = jnp.pad(w[:, ba0:g0], ((0, 0), (0, BA_PAD - 2 * GDN_HEADS))).astype(BF16)
    w_bat = w_ba.T
    row = lambda v: v.reshape(1, -1)
    wa, wb, wo = w_branch_a[l].astype(BF16), w_branch_b[l].astype(BF16), w_out[l].astype(BF16)
    wq, wk, wv, wxo = w_xq[l].astype(BF16), w_xk[l].astype(BF16), w_xv[l].astype(BF16), w_xo[l].astype(BF16)
    wg, wu, wd = w_gate[l].astype(BF16), w_up[l].astype(BF16), w_down[l].astype(BF16)
    alog_r = _pad_lanes(gdn_a_log[l], GDN_HEADS).reshape(1, BA_PAD)
    dt_r = _pad_lanes(gdn_dt_bias[l], GDN_HEADS).reshape(1, BA_PAD)
    alog_c, dt_c = alog_r.reshape(BA_PAD, 1), dt_r.reshape(BA_PAD, 1)
    cos_p, sin_p = _rope_tables(jnp.arange(t))
    cos_s, sin_s = _rope_tables(PAST_LEN + jnp.arange(1))

    mix_g = row(norm_mix_g[l])
    xp = x_prompt.reshape(b * t, D_MODEL)
    oa_p, sr_p = _retention_prompt(x_prompt, mix_g, w_bf, cos_p, sin_p, row(ret_gn_g[l]))
    ob_p, sg_p, sc_p = _gdn_prompt(x_prompt, mix_g, w_bf, w_ba, w_bat, gdn_conv_w[l], alog_r, dt_r,
                                   alog_c, dt_c, row(gdn_norm_g[l]))
    x1_p, q_p = _merge(xp, mix_g, w_gates, oa_p.reshape(b * t, -1), ob_p.reshape(b * t, -1), wa, wb, wo,
                       row(norm_x_g[l]), wq, tm=1024)
    mk_p, mv_p, mk4_p, mv4_p = _memkv(mem_prompt.reshape(b * N_MEM, D_MODEL), row(mem_norm_g[l]), wk, wv)
    x2_p = _xattn_prompt(q_p.reshape(b, t, D_MODEL), mk_p.reshape(b, N_MEM, D_MODEL),
                         mv_p.reshape(b, N_MEM, D_MODEL), x1_p.reshape(b, t, D_MODEL), wxo)
    y_p = _ffn(x2_p.reshape(b * t, D_MODEL), row(norm_ffn_g[l]), wg, wu, wd, row(norm_final_g), tm=512)

    xs = x_sample.reshape(ns, D_MODEL)
    proj_s, ba_s = _inproj(xs, mix_g, w_bf, COL_GATE, w_ba)
    oa_s, ob_s, sr_s, sg_s, sc_s = _sample_step(proj_s[:, :COL_GDN], proj_s[:, COL_GDN:], ba_s, state_ret[l],
                                                state_gdn[l], state_conv[l],
                                                cos_s, sin_s, gdn_conv_w[l], gdn_a_log[l], gdn_dt_bias[l],
                                                ret_gn_g[l], row(gdn_norm_g[l]))
    x1_s, q_s = _merge(xs, mix_g, w_gates, oa_s.reshape(ns, -1), ob_s.reshape(ns, -1), wa, wb, wo,
                       row(norm_x_g[l]), wq, tm=ns)
    o_s = _xattn_sample(q_s, cache_mem_k[l], cache_mem_v[l])
    x2_s = _resid_mm(x1_s, o_s.reshape(ns, D_MODEL).astype(BF16), wxo)
    y_s = _ffn(x2_s, row(norm_ffn_g[l]), wg, wu, wd, row(norm_final_g), tm=ns)

    return (y_p.reshape(b, t, D_MODEL), y_s.reshape(ns, 1, D_MODEL),
            sr_p[None], sg_p[None], sc_p[None],
            mk4_p.reshape(1, b, N_MEM, X_HEADS, X_HD), mv4_p.reshape(1, b, N_MEM, X_HEADS, X_HD),
            sr_s[None], sg_s[None], sc_s[None])
```

```python
import functools

import numpy as np
import jax
import jax.numpy as jnp
from jax import lax
from jax.experimental import pallas as pl
from jax.experimental.pallas import tpu as pltpu

F32 = jnp.float32
BF16 = jnp.bfloat16

D_MODEL = 1024
RET_HEADS, RET_DK, RET_DV = 4, 128, 256
GDN_HEADS, GDN_DK, GDN_DV = 8, 128, 128
CONV_W = 4
CONV_CH = 3 * GDN_HEADS * GDN_DK
N_MEM, X_HEADS, X_HD = 256, 4, 256
PAST_LEN = 16384
ROPE_BASE = 10000.0
EPS = 1e-6
GDN_CHUNK = 64

COL_RET = 0
COL_GDN = 3072
COL_GATE = 7168
BA_PAD = 128

VMEM_LIMIT = 56 * 1024 * 1024

NT_DIMS = (((1,), (1,)), ((), ()))
TN_DIMS = (((0,), (0,)), ((), ()))


def _mm(a, b):
    return jnp.dot(a.astype(BF16), b.astype(BF16), preferred_element_type=F32)


def _mm_nt(a, b):
    return lax.dot_general(a.astype(BF16), b.astype(BF16), NT_DIMS, preferred_element_type=F32)


def _mm_tn(a, b):
    return lax.dot_general(a.astype(BF16), b.astype(BF16), TN_DIMS, preferred_element_type=F32)


def _split3(x):
    hi = x.astype(BF16)
    r = x - hi.astype(F32)
    mid = r.astype(BF16)
    return hi, mid, (r - mid.astype(F32)).astype(BF16)


def _mm_sel(sel, x):
    return sum(jnp.dot(sel, p, preferred_element_type=F32) for p in _split3(x))


def _mm_sel_r(x, sel):
    return sum(jnp.dot(p, sel, preferred_element_type=F32) for p in _split3(x))


def _rms(x, g):
    return x * lax.rsqrt(jnp.mean(x * x, axis=-1, keepdims=True) + EPS) * g


def _silu(x):
    h = 0.5 * x
    return h + h * jnp.tanh(h)


def _softplus(x):
    return jnp.maximum(x, 0.0) + jnp.log1p(jnp.exp(-jnp.abs(x)))


def _params(*sem):
    return pltpu.CompilerParams(dimension_semantics=sem, vmem_limit_bytes=VMEM_LIMIT)


def _inproj_body(x_ref, g_ref, w_ref, wba_ref, o_ref, oba_ref, h_scr):
    @pl.when(pl.program_id(0) == 0)
    def _():
        hb = _rms(x_ref[...], g_ref[...]).astype(BF16)
        h_scr[...] = hb
        oba_ref[...] = jnp.dot(hb, wba_ref[...], preferred_element_type=F32)

    o_ref[...] = jnp.dot(h_scr[...], w_ref[...], preferred_element_type=F32)


def _inproj(x, g, w_bf, n, w_ba, tn=1024):
    m = x.shape[0]
    assert n % tn == 0 and n <= w_bf.shape[1]
    return pl.pallas_call(
        _inproj_body,
        grid=(n // tn,),
        in_specs=[
            pl.BlockSpec((m, D_MODEL), lambda j: (0, 0)),
            pl.BlockSpec((1, D_MODEL), lambda j: (0, 0)),
            pl.BlockSpec((D_MODEL, tn), lambda j: (0, j)),
            pl.BlockSpec((D_MODEL, BA_PAD), lambda j: (0, 0)),
        ],
        out_specs=[
            pl.BlockSpec((m, tn), lambda j: (0, j)),
            pl.BlockSpec((m, BA_PAD), lambda j: (0, 0)),
        ],
        out_shape=[
            jax.ShapeDtypeStruct((m, n), F32),
            jax.ShapeDtypeStruct((m, BA_PAD), F32),
        ],
        scratch_shapes=[pltpu.VMEM((m, D_MODEL), BF16)],
        compiler_params=_params("arbitrary"),
        name="inproj",
    )(x, g, w_bf, w_ba)


_RET_LOG_G = np.log1p(-np.exp2(-5.0 - np.arange(RET_HEADS, dtype=np.float64)))


def _ret_tables(c):
    idx = np.arange(c, dtype=np.float64)
    diff = idx[:, None] - idx[None, :]
    dmat = np.where(diff >= 0, np.exp(np.maximum(diff, 0.0)[None] * _RET_LOG_G[:, None, None]), 0.0)
    qdec = np.exp((idx + 1.0)[None, :] * _RET_LOG_G[:, None])
    kdec = np.exp((c - 1.0 - idx)[None, :] * _RET_LOG_G[:, None])
    lane = np.ones((1, 1, RET_DK))
    return (jnp.asarray(dmat, F32), jnp.asarray(qdec[:, :, None] * lane, F32),
            jnp.asarray(kdec[:, :, None] * lane, F32), [float(v) for v in np.exp(c * _RET_LOG_G)])


def _rot(x, cos, sin):
    return x * cos + pltpu.roll(x, RET_DK // 2, 1) * sin


def _group_norm_gate(o, gate, gn):
    mu = jnp.mean(o, axis=-1, keepdims=True)
    d = o - mu
    var = jnp.mean(d * d, axis=-1, keepdims=True)
    return _silu(gate) * (d * lax.rsqrt(var + EPS) * gn)


def _proj_tiles(hb, w_ref, dst, width=512):
    for c0 in range(0, w_ref.shape[1], width):
        dst[:, c0:c0 + width] = jnp.dot(hb, w_ref[:, c0:c0 + width], preferred_element_type=F32)


def _next_block(nt, nblocks):
    def index_map(i, j):
        n1 = jnp.minimum(i * nt + j + 1, nblocks - 1)
        return (n1 // nt, n1 % nt, 0)
    return index_map


def _ret_body(cdec, x0_ref, xn_ref, nmg_ref, w_ref, *rest):
    *consts, o_ref, s_out_ref, s_scr, pa, pb = rest
    t = pl.program_id(1)
    n = pl.program_id(0) * pl.num_programs(1) + t
    bufs = (pa, pb)

    @pl.when(t == 0)
    def _():
        s_scr[...] = jnp.zeros_like(s_scr)

    @pl.when(n == 0)
    def _():
        _proj_tiles(_rms(x0_ref[0], nmg_ref[...]).astype(BF16), w_ref, pa)

    for slot in range(2):
        @pl.when(n % 2 == slot)
        def _(slot=slot):
            _ret_step(cdec, t, xn_ref, nmg_ref, w_ref, bufs[slot], bufs[1 - slot], *consts, o_ref, s_out_ref, s_scr)


def _ret_step(cdec, t, xn_ref, nmg_ref, w_ref, proj, proj_next, cos_ref, sin_ref, dmat_ref, qdec_ref, kdec_ref,
              gn_ref, o_ref, s_out_ref, s_scr):
    qw = RET_HEADS * RET_DK
    hb = _rms(xn_ref[0], nmg_ref[...]).astype(BF16)

    def tile(c0, width=512):
        def run():
            proj_next[:, c0:c0 + width] = jnp.dot(hb, w_ref[:, c0:c0 + width], preferred_element_type=F32)
        return run

    bg = _Background([tile(c0) for c0 in range(0, w_ref.shape[1], 512)])
    bg(1)

    chunk = dmat_ref.shape[1]
    for c0 in range(0, proj.shape[0], chunk):
        rows = slice(c0, c0 + chunk)
        cos, sin = cos_ref[rows, :], sin_ref[rows, :]
        for h in range(RET_HEADS):
            qk = slice(h * RET_DK, (h + 1) * RET_DK)
            kk = slice(qw + h * RET_DK, qw + (h + 1) * RET_DK)
            vv = slice(h * RET_DV, (h + 1) * RET_DV)
            q = _rot(proj[rows, qk], cos, sin)
            k = _rot(proj[rows, kk], cos, sin) * (RET_DK ** -0.5)
            v = proj[rows, 2 * qw + h * RET_DV:2 * qw + (h + 1) * RET_DV]
            gate = proj[rows, 2 * qw + (RET_HEADS + h) * RET_DV:2 * qw + (RET_HEADS + h + 1) * RET_DV]
            s = s_scr[h]
            scores = _mm_nt(q, k) * dmat_ref[h]
            o = _mm(scores, v) + _mm(q * qdec_ref[h], s)
            s_scr[h] = cdec[h] * s + _mm_tn(k * kdec_ref[h], v)
            o_ref[0, rows, vv] = _group_norm_gate(o, gate, gn_ref[:, vv]).astype(BF16)
            bg(1)
    bg.drain()

    @pl.when(t == pl.num_programs(1) - 1)
    def _():
        s_out_ref[0] = s_scr[...]


def _retention_prompt(x3, mix_g, w_rg, cos, sin, gn, tb=512, chunk=256):
    b, t, _ = x3.shape
    nt = t // tb
    dmat, qdec, kdec, cdec = _ret_tables(chunk)
    vw = RET_HEADS * RET_DV
    const3 = lambda i, j: (0, 0, 0)
    return pl.pallas_call(
        functools.partial(_ret_body, cdec),
        grid=(b, nt),
        in_specs=[
            pl.BlockSpec((1, tb, D_MODEL), const3),
            pl.BlockSpec((1, tb, D_MODEL), _next_block(nt, b * nt)),
            pl.BlockSpec((1, D_MODEL), lambda i, j: (0, 0)),
            pl.BlockSpec((D_MODEL, COL_GDN), lambda i, j: (0, COL_RET // COL_GDN), pipeline_mode=pl.Buffered(1)),
            pl.BlockSpec((tb, RET_DK), lambda i, j: (j, 0)),
            pl.BlockSpec((tb, RET_DK), lambda i, j: (j, 0)),
            pl.BlockSpec((RET_HEADS, chunk, chunk), const3),
            pl.BlockSpec((RET_HEADS, chunk, RET_DK), const3),
            pl.BlockSpec((RET_HEADS, chunk, RET_DK), const3),
            pl.BlockSpec((1, vw), lambda i, j: (0, 0)),
        ],
        out_specs=[
            pl.BlockSpec((1, tb, vw), lambda i, j: (i, j, 0)),
            pl.BlockSpec((1, RET_HEADS, RET_DK, RET_DV), lambda i, j: (i, 0, 0, 0)),
        ],
        out_shape=[
            jax.ShapeDtypeStruct((b, t, vw), BF16),
            jax.ShapeDtypeStruct((b, RET_HEADS, RET_DK, RET_DV), F32),
        ],
        scratch_shapes=[pltpu.VMEM((RET_HEADS, RET_DK, RET_DV), F32)]
        + 2 * [pltpu.VMEM((tb, COL_GDN), F32)],
        compiler_params=_params("arbitrary", "arbitrary"),
        name="retention_prompt",
    )(x3, x3, mix_g, w_rg, cos, sin, dmat, qdec, kdec, gn)


def _gdn_tables(tb):
    idx = np.arange(tb)
    same = (idx[:, None] // GDN_CHUNK) == (idx[None, :] // GDN_CHUNK)
    lower = same & (idx[:, None] >= idx[None, :])
    nchunk = tb // GDN_CHUNK
    chunk_sel = np.repeat((idx[:, None] // GDN_CHUNK) == np.arange(nchunk)[None, :], 128, axis=1)
    grp = 2 * GDN_CHUNK
    il = (idx % grp)[:, None]
    jl = np.arange(grp)[None, :]
    bias = np.where((il // GDN_CHUNK == jl // GDN_CHUNK) & (il >= jl), 0.0, -1e30)
    eye = (il == jl).astype(np.float64)
    return (jnp.asarray(lower, BF16), jnp.asarray(lower.T, BF16), jnp.asarray(same, BF16),
            jnp.asarray(chunk_sel, BF16), jnp.asarray(bias, F32), jnp.asarray(eye - 1.0, F32),
            jnp.asarray(eye, BF16))


def _gdn_prepare(x_ref, first_of_seq, nmg_ref, wqkv_ref, wz_ref, wba_ref, wbat_ref, cw_ref, xr, cs_buf, zs, bas,
                 bats, tb):
    st = {}

    def start():
        xr[0:8, :] = jnp.where(first_of_seq, 0.0, xr[tb:tb + 8, :])
        st["hb"] = _rms(x_ref[0], nmg_ref[...]).astype(BF16)

    def tile(w_ref, dst, c0, width=512):
        def run():
            dst[:, c0:c0 + width] = jnp.dot(st["hb"], w_ref[:, c0:c0 + width], preferred_element_type=F32)
        return run

    def logits():
        bas[...] = jnp.dot(st["hb"], wba_ref[...], preferred_element_type=F32)
        bats[...] = lax.dot_general(wbat_ref[...], st["hb"], NT_DIMS, preferred_element_type=F32)

    def conv(c0):
        def run():
            cs = slice(c0, c0 + GDN_DK)
            acc = xr[5:5 + tb, cs] * cw_ref[0:1, cs]
            for i in range(1, CONV_W):
                acc = acc + xr[5 + i:5 + i + tb, cs] * cw_ref[i:i + 1, cs]
            cs_buf[:, cs] = _silu(acc)
        return run

    mxu_items = ([start] + [tile(wqkv_ref, xr.at[8:8 + tb], c0) for c0 in range(0, CONV_CH, 512)]
                 + [tile(wz_ref, zs, c0) for c0 in range(0, wz_ref.shape[1], 512)] + [logits])
    vpu_items = [conv(c0) for c0 in range(0, CONV_CH, GDN_DK)]
    return mxu_items, vpu_items


class _Background:
    def __init__(self, items):
        self._items = list(items)

    def __call__(self, count):
        for item in self._items[:count]:
            item()
        del self._items[:count]

    def drain(self):
        self(len(self._items))


def _gdn_body(tb, x0_ref, xn_ref, nmg_ref, wqkv_ref, wz_ref, wba_ref, wbat_ref, cw_ref, *rest):
    *consts, o_ref, s_out_ref, conv_out_ref, s_scr, xr, ca, za, baa, bata, cb, zb, bab, batb = rest
    t = pl.program_id(1)
    nt = pl.num_programs(1)
    n = pl.program_id(0) * nt + t
    weights = (nmg_ref, wqkv_ref, wz_ref, wba_ref, wbat_ref, cw_ref)
    bufs = ((ca, za, baa, bata), (cb, zb, bab, batb))

    @pl.when(t == 0)
    def _():
        s_scr[...] = jnp.zeros_like(s_scr)

    @pl.when(n == 0)
    def _():
        mxu_items, vpu_items = _gdn_prepare(x0_ref, True, *weights, xr, *bufs[0], tb)
        _Background(mxu_items + vpu_items).drain()

    for slot in range(2):
        @pl.when(n % 2 == slot)
        def _(slot=slot):
            mxu_items, vpu_items = _gdn_prepare(xn_ref, t == nt - 1, *weights, xr, *bufs[1 - slot], tb)
            _gdn_step(tb, t, bufs[slot], *consts, o_ref, s_out_ref, s_scr,
                      _Background(mxu_items), _Background(vpu_items))

    @pl.when(t == nt - 2)
    def _():
        conv_out_ref[0] = xr[tb + 5:tb + 8, :]


def _gdn_step(tb, t, cur, alog_r_ref, dt_r_ref, alog_c_ref, dt_c_ref, ng_ref,
              lbd_ref, ubd_ref, obd_ref, csel_ref, bias_ref, offd_ref, eye_ref, o_ref, s_out_ref, s_scr,
              bg_mxu, bg_vpu):
    nchunk = tb // GDN_CHUNK
    hk = GDN_HEADS * GDN_DK
    c_scr, zz, ba_ref, bat_ref = cur
    nh = GDN_HEADS
    ba, bat = ba_ref[...], bat_ref[nh:2 * nh, :]

    bg_mxu(2)
    log2e = 1.4426950408889634
    beta_c = jax.nn.sigmoid(ba)
    g_c = (-log2e * jnp.exp(alog_r_ref[...])) * _softplus(ba + dt_r_ref[...])
    g_r = (-log2e * jnp.exp(alog_c_ref[nh:2 * nh, :])) * _softplus(bat + dt_c_ref[nh:2 * nh, :])
    gc_c = _mm_sel(lbd_ref[...], g_c)
    gt_c = _mm_sel(obd_ref[...], g_c)
    gc_r = _mm_sel_r(g_r, ubd_ref[...])
    gt_l = _mm_sel_r(g_r, csel_ref[...])

    heads = range(GDN_HEADS)
    grp = 2 * GDN_CHUNK
    groups = [slice(p * grp, (p + 1) * grp) for p in range(tb // grp)]

    def grp_dot(a, b):
        return jnp.concatenate([jnp.dot(a[g], b[g], preferred_element_type=F32) for g in groups], axis=0)

    def grp_dot_nt(a, b):
        return jnp.concatenate([lax.dot_general(a[g], b[g], NT_DIMS, preferred_element_type=F32)
                                for g in groups], axis=0)

    bias = bias_ref[...]
    offdiag = offd_ref[...]
    eye_b = eye_ref[...]
    qs, ks, gammas, pbs, rhss, qgs, khs = [], [], [], [], [], [], []
    for h in heads:
        q = c_scr[:, h * GDN_DK:(h + 1) * GDN_DK]
        k = c_scr[:, hk + h * GDN_DK:hk + (h + 1) * GDN_DK]
        v = c_scr[:, 2 * hk + h * GDN_DV:2 * hk + (h + 1) * GDN_DV]
        q = q * lax.rsqrt(jnp.sum(q * q, axis=-1, keepdims=True) + EPS) * (GDN_DK ** -0.5)
        k = k * lax.rsqrt(jnp.sum(k * k, axis=-1, keepdims=True) + EPS)
        beta = beta_c[:, h:h + 1]
        gcc = gc_c[:, 8 + h:9 + h]
        gtc = gt_c[:, 8 + h:9 + h]
        gcr = gc_r[h:h + 1, :]
        dg = jnp.concatenate([gcc[g] - gcr[:, g] for g in groups], axis=0)
        gamma = jnp.exp2(dg + bias)
        kbeta = k * beta
        kb = k.astype(BF16)
        pbs.append((grp_dot_nt(kbeta.astype(BF16), kb) * (gamma * offdiag)).astype(BF16))
        eg = jnp.exp2(gcc)
        rhss.append(jnp.concatenate([v * beta, kbeta * eg], axis=1))
        qs.append(q.astype(BF16))
        ks.append(kb)
        gammas.append(gamma)
        qgs.append(q * eg)
        khs.append(k * jnp.exp2(gtc - gcc))
        bg_mxu(1)
        bg_vpu(1)
    bg_mxu.drain()

    def solve(_, carry):
        pb = pbs
        tbs = [pb[h] + eye_b for h in heads]
        for lvl in range(5):
            pb = [grp_dot(pb[h], pb[h]).astype(BF16) for h in heads]
            bg_vpu(1)
            tnew = [grp_dot(tbs[h], pb[h] + eye_b) for h in heads]
            bg_vpu(1)
            tbs = [x.astype(BF16) for x in tnew]

        us, ws, qks = [], [], []
        for h in heads:
            uw = rhss[h] + grp_dot(tbs[h] - eye_b, rhss[h].astype(BF16))
            us.append(uw[:, :GDN_DV])
            ws.append(uw[:, GDN_DV:])
            qks.append((grp_dot_nt(qs[h], ks[h]) * gammas[h]).astype(BF16))
        bg_vpu(2)

        s = [s_scr[h] for h in heads]
        vn_parts = [[] for _ in heads]
        qs_parts = [[] for _ in heads]
        for c in range(nchunk):
            rows = slice(c * GDN_CHUNK, (c + 1) * GDN_CHUNK)
            for h in heads:
                wq = _mm(jnp.concatenate([ws[h][rows], qgs[h][rows]], axis=0), s[h])
                vn = us[h][rows] - wq[:GDN_CHUNK]
                qs_parts[h].append(wq[GDN_CHUNK:])
                vn_parts[h].append(vn)
                decay = jnp.exp2(gt_l[h:h + 1, c * 128:(c + 1) * 128])
                s[h] = decay * s[h] + _mm_tn(khs[h][rows], vn)
            bg_vpu(1)
        bg_vpu.drain()
        for h in heads:
            hs = slice(h * GDN_DV, (h + 1) * GDN_DV)
            s_scr[h] = s[h]
            vn = jnp.concatenate(vn_parts[h], axis=0).astype(BF16)
            o = jnp.concatenate(qs_parts[h], axis=0) + grp_dot(qks[h], vn)
            o_ref[0, :, hs] = (_rms(o, ng_ref[...]) * _silu(zz[:, hs])).astype(BF16)
        return carry

    lax.fori_loop(0, jnp.minimum(t + 1, 1), solve, 0)

    @pl.when(t == pl.num_programs(1) - 1)
    def _():
        s_out_ref[0] = s_scr[...]


def _gdn_prompt(x3, mix_g, w_rg, w_ba, w_bat, conv_w, alog_r, dt_r, alog_c, dt_c, norm_g, tb=256):
    b, t, _ = x3.shape
    nt = t // tb
    assert t % tb == 0 and nt >= 2, "a sequence's last block must be prepared during one of its own steps"
    lbd, ubd, obd, csel, bias, offdiag, eye = _gdn_tables(tb)
    vw = GDN_HEADS * GDN_DV
    c2 = lambda i, j: (0, 0)
    single = pl.Buffered(1)
    return pl.pallas_call(
        functools.partial(_gdn_body, tb),
        grid=(b, nt),
        in_specs=[
            pl.BlockSpec((1, tb, D_MODEL), lambda i, j: (0, 0, 0)),
            pl.BlockSpec((1, tb, D_MODEL), _next_block(nt, b * nt)),
            pl.BlockSpec((1, D_MODEL), c2),
            pl.BlockSpec((D_MODEL, CONV_CH), lambda i, j: (0, COL_GDN // CONV_CH), pipeline_mode=single),
            pl.BlockSpec((D_MODEL, vw), lambda i, j: (0, (COL_GDN + CONV_CH) // vw), pipeline_mode=single),
            pl.BlockSpec((D_MODEL, BA_PAD), c2),
            pl.BlockSpec((BA_PAD, D_MODEL), c2),
            pl.BlockSpec((CONV_W, CONV_CH), c2),
            pl.BlockSpec((1, BA_PAD), c2),
            pl.BlockSpec((1, BA_PAD), c2),
            pl.BlockSpec((BA_PAD, 1), c2),
            pl.BlockSpec((BA_PAD, 1), c2),
            pl.BlockSpec((1, GDN_DV), c2),
            pl.BlockSpec((tb, tb), c2),
            pl.BlockSpec((tb, tb), c2),
            pl.BlockSpec((tb, tb), c2),
            pl.BlockSpec((tb, (tb // GDN_CHUNK) * 128), c2),
            pl.BlockSpec((tb, 2 * GDN_CHUNK), c2),
            pl.BlockSpec((tb, 2 * GDN_CHUNK), c2),
            pl.BlockSpec((tb, 2 * GDN_CHUNK), c2),
        ],
        out_specs=[
            pl.BlockSpec((1, tb, vw), lambda i, j: (i, j, 0)),
            pl.BlockSpec((1, GDN_HEADS, GDN_DK, GDN_DV), lambda i, j: (i, 0, 0, 0)),
            pl.BlockSpec((1, CONV_W - 1, CONV_CH), lambda i, j: (i, 0, 0)),
        ],
        out_shape=[
            jax.ShapeDtypeStruct((b, t, vw), BF16),
            jax.ShapeDtypeStruct((b, GDN_HEADS, GDN_DK, GDN_DV), F32),
            jax.ShapeDtypeStruct((b, CONV_W - 1, CONV_CH), F32),
        ],
        scratch_shapes=[pltpu.VMEM((GDN_HEADS, GDN_DK, GDN_DV), F32), pltpu.VMEM((tb + 8, CONV_CH), F32)]
        + 2 * [pltpu.VMEM((tb, CONV_CH), F32), pltpu.VMEM((tb, vw), F32),
               pltpu.VMEM((tb, BA_PAD), F32), pltpu.VMEM((BA_PAD, tb), F32)],
        compiler_params=_params("arbitrary", "arbitrary"),
        name="gdn_prompt",
    )(x3, x3, mix_g, w_rg, w_rg, w_ba, w_bat, conv_w, alog_r, dt_r, alog_c, dt_c, norm_g, lbd, ubd, obd, csel,
      bias, offdiag, eye)


def _block_diag_rows(x, nblk):
    row = lax.broadcasted_iota(jnp.int32, (8, nblk * 128), 0)
    blk = lax.broadcasted_iota(jnp.int32, (8, nblk * 128), 1) // 128
    return jnp.where(row == blk, jnp.concatenate([x] * nblk, axis=1), 0.0)


def _sample_step_body(cdec, qk_ref, rv_ref, rg_ref, x_ref, z_ref, bcol_ref, sr_ref, sg_ref, sc_ref, cos_ref, sin_ref,
                      cw_ref, alog_ref, dt_ref, gn_ref, ng_ref,
                      oa_ref, ob_ref, sr_out_ref, sg_out_ref, sc_out_ref):
    nh = GDN_HEADS
    cos, sin = cos_ref[...], sin_ref[...]
    gam = jnp.where(lax.broadcasted_iota(jnp.int32, (8, 1), 0) == 0, cdec[0], 0.0)
    for h in range(1, RET_HEADS):
        gam = jnp.where(lax.broadcasted_iota(jnp.int32, (8, 1), 0) == h, cdec[h], gam)
    zero4 = jnp.zeros((RET_HEADS, RET_DV), F32)
    for g in range(x_ref.shape[0]):
        x_new = x_ref[g]
        buf = sc_ref[g]
        conv = x_new * cw_ref[CONV_W - 1]
        for i in range(CONV_W - 1):
            conv = conv + buf[i] * cw_ref[i]
        conv = _silu(conv)
        sc_out_ref[g, 0] = buf[1]
        sc_out_ref[g, 1] = buf[2]
        sc_out_ref[g, 2] = x_new
        q, k, v = conv[0:nh], conv[nh:2 * nh], conv[2 * nh:3 * nh]
        q = q * lax.rsqrt(jnp.sum(q * q, axis=-1, keepdims=True) + EPS) * (GDN_DK ** -0.5)
        k = k * lax.rsqrt(jnp.sum(k * k, axis=-1, keepdims=True) + EPS)
        col = bcol_ref[g]
        beta = jax.nn.sigmoid(col[0:nh])
        eg = jnp.exp(-jnp.exp(alog_ref[...]) * _softplus(col[nh:2 * nh] + dt_ref[...]))
        kbd, qbd = _block_diag_rows(k, nh), _block_diag_rows(q, nh)
        s_flat = sg_ref[g].reshape(nh * GDN_DK, GDN_DV)
        kq_s = _mm(jnp.concatenate([kbd, qbd], axis=0), s_flat)
        vn = beta * (v - eg * kq_s[0:nh])
        o = eg * kq_s[nh:2 * nh] + jnp.sum(q * k, axis=-1, keepdims=True) * vn
        ob_ref[g] = _rms(o, ng_ref[...]) * _silu(z_ref[g])
        upd = _mm_tn(kbd, vn)
        eg_l = jnp.broadcast_to(eg, (nh, GDN_DV))
        for h in range(nh):
            sg_out_ref[g, h] = eg_l[h:h + 1] * sg_ref[g, h] + upd[h * GDN_DK:(h + 1) * GDN_DK]

        qk = _rot(qk_ref[g], cos, sin)
        rq = jnp.concatenate([qk[0:RET_HEADS], qk[0:RET_HEADS]], axis=0)
        rk = jnp.concatenate([qk[RET_HEADS:], qk[RET_HEADS:]], axis=0) * (RET_DK ** -0.5)
        rv = jnp.concatenate([rv_ref[g], zero4], axis=0)
        qbd_r, kbd_r = _block_diag_rows(rq, RET_HEADS), _block_diag_rows(rk, RET_HEADS)
        sr_flat = sr_ref[g].reshape(RET_HEADS * RET_DK, RET_DV)
        q_s = _mm(qbd_r, sr_flat)
        o_r = gam * q_s + jnp.sum(rq * rk, axis=-1, keepdims=True) * rv
        oa_ref[g] = _group_norm_gate(o_r[0:RET_HEADS], rg_ref[g], gn_ref[...])
        upd_r = _mm_tn(kbd_r, rv)
        for h in range(RET_HEADS):
            sr_out_ref[g, h] = cdec[h] * sr_ref[g, h] + upd_r[h * RET_DK:(h + 1) * RET_DK]


def _sample_step(proj_ret, proj_gdn, ba, state_ret, state_gdn, state_conv, cos, sin, conv_w, a_log, dt_bias, gn,
                 norm_g, gs=8):
    ns = proj_ret.shape[0]
    nh, rh = GDN_HEADS, RET_HEADS
    nq = rh * RET_DK
    qk3 = proj_ret[:, :2 * nq].reshape(ns, 2 * rh, RET_DK)
    rv3 = proj_ret[:, 2 * nq:2 * nq + rh * RET_DV].reshape(ns, rh, RET_DV)
    rg3 = proj_ret[:, 2 * nq + rh * RET_DV:].reshape(ns, rh, RET_DV)
    x3 = proj_gdn[:, :CONV_CH].reshape(ns, 3 * nh, GDN_DK)
    z3 = proj_gdn[:, CONV_CH:].reshape(ns, nh, GDN_DV)
    bcol = ba[:, :2 * nh].reshape(ns, 2 * nh, 1)
    sc4 = state_conv.reshape(ns, CONV_W - 1, 3 * nh, GDN_DK)
    cw3 = conv_w.reshape(CONV_W, 3 * nh, GDN_DK)
    cdec = [float(v) for v in np.exp(_RET_LOG_G)]
    c2 = lambda i: (0, 0)
    c3 = lambda i: (0, 0, 0)
    b3 = lambda i: (i, 0, 0)
    b4 = lambda i: (i, 0, 0, 0)
    oa, ob, sr, sg, sc = pl.pallas_call(
        functools.partial(_sample_step_body, cdec),
        grid=(ns // gs,),
        in_specs=[
            pl.BlockSpec((gs, 2 * rh, RET_DK), b3),
            pl.BlockSpec((gs, rh, RET_DV), b3),
            pl.BlockSpec((gs, rh, RET_DV), b3),
            pl.BlockSpec((gs, 3 * nh, GDN_DK), b3),
            pl.BlockSpec((gs, nh, GDN_DV), b3),
            pl.BlockSpec((gs, 2 * nh, 1), b3),
            pl.BlockSpec((gs, rh, RET_DK, RET_DV), b4),
            pl.BlockSpec((gs, nh, GDN_DK, GDN_DV), b4),
            pl.BlockSpec((gs, CONV_W - 1, 3 * nh, GDN_DK), b4),
            pl.BlockSpec((1, RET_DK), c2),
            pl.BlockSpec((1, RET_DK), c2),
            pl.BlockSpec((CONV_W, 3 * nh, GDN_DK), c3),
            pl.BlockSpec((nh, 1), c2),
            pl.BlockSpec((nh, 1), c2),
            pl.BlockSpec((rh, RET_DV), c2),
            pl.BlockSpec((1, GDN_DV), c2),
        ],
        out_specs=[
            pl.BlockSpec((gs, rh, RET_DV), b3),
            pl.BlockSpec((gs, nh, GDN_DV), b3),
            pl.BlockSpec((gs, rh, RET_DK, RET_DV), b4),
            pl.BlockSpec((gs, nh, GDN_DK, GDN_DV), b4),
            pl.BlockSpec((gs, CONV_W - 1, 3 * nh, GDN_DK), b4),
        ],
        out_shape=[
            jax.ShapeDtypeStruct((ns, rh, RET_DV), F32),
            jax.ShapeDtypeStruct((ns, nh, GDN_DV), F32),
            jax.ShapeDtypeStruct(state_ret.shape, F32),
            jax.ShapeDtypeStruct(state_gdn.shape, F32),
            jax.ShapeDtypeStruct(sc4.shape, F32),
        ],
        compiler_params=_params("parallel"),
        name="sample_step",
    )(qk3, rv3, rg3, x3, z3, bcol, state_ret, state_gdn, sc4, cos, sin, cw3, a_log.reshape(nh, 1),
      dt_bias.reshape(nh, 1), gn.reshape(rh, RET_DV), norm_g)
    return (oa.reshape(ns, rh * RET_DV).astype(BF16), ob.reshape(ns, nh * GDN_DV).astype(BF16), sr, sg,
            sc.reshape(state_conv.shape))


def _merge_body(x_ref, nmg_ref, wgt_ref, oa_ref, ob_ref, wa_ref, wb_ref, wo_ref, ng_ref, wq_ref, x1_ref, q_ref):
    x = x_ref[...]
    gates = jnp.dot(_rms(x, nmg_ref[...]).astype(BF16), wgt_ref[...], preferred_element_type=F32)
    ya = jnp.dot(oa_ref[...], wa_ref[...], preferred_element_type=F32)
    yb = jnp.dot(ob_ref[...], wb_ref[...], preferred_element_type=F32)
    merged = jax.nn.sigmoid(gates[:, :D_MODEL]) * ya + jax.nn.sigmoid(gates[:, D_MODEL:]) * yb
    x1 = x + _mm(merged, wo_ref[...])
    x1_ref[...] = x1
    q_ref[...] = _mm(_rms(x1, ng_ref[...]), wq_ref[...]).astype(BF16)


def _merge(x, mix_g, w_gates, oa, ob, wa, wb, wo, ng, wq, tm):
    m = x.shape[0]
    row = lambda i: (i, 0)
    c2 = lambda i: (0, 0)
    wspec = pl.BlockSpec((D_MODEL, D_MODEL), c2, pipeline_mode=pl.Buffered(1))
    return pl.pallas_call(
        _merge_body,
        grid=(m // tm,),
        in_specs=[
            pl.BlockSpec((tm, D_MODEL), row),
            pl.BlockSpec((1, D_MODEL), c2),
            pl.BlockSpec((D_MODEL, 2 * D_MODEL), c2, pipeline_mode=pl.Buffered(1)),
            pl.BlockSpec((tm, D_MODEL), row),
            pl.BlockSpec((tm, D_MODEL), row),
            wspec, wspec, wspec,
            pl.BlockSpec((1, D_MODEL), c2),
            wspec,
        ],
        out_specs=[pl.BlockSpec((tm, D_MODEL), row), pl.BlockSpec((tm, D_MODEL), row)],
        out_shape=[jax.ShapeDtypeStruct((m, D_MODEL), F32), jax.ShapeDtypeStruct((m, D_MODEL), BF16)],
        compiler_params=_params("parallel"),
        name="merge",
    )(x, mix_g, w_gates, oa, ob, wa, wb, wo, ng, wq)


def _memkv_body(m_ref, g_ref, wk_ref, wv_ref, k_ref, v_ref, k4_ref, v4_ref):
    mn = _rms(m_ref[...], g_ref[...]).astype(BF16)
    k = jnp.dot(mn, wk_ref[...], preferred_element_type=F32)
    v = jnp.dot(mn, wv_ref[...], preferred_element_type=F32)
    k_ref[...] = k
    v_ref[...] = v
    for h in range(X_HEADS):
        k4_ref[:, h, :] = k[:, h * X_HD:(h + 1) * X_HD]
        v4_ref[:, h, :] = v[:, h * X_HD:(h + 1) * X_HD]


def _memkv(mem, g, wk, wv, tm=512):
    m = mem.shape[0]
    row = lambda i: (i, 0)
    row3 = lambda i: (i, 0, 0)
    c2 = lambda i: (0, 0)
    return pl.pallas_call(
        _memkv_body,
        grid=(m // tm,),
        in_specs=[pl.BlockSpec((tm, D_MODEL), row), pl.BlockSpec((1, D_MODEL), c2),
                  pl.BlockSpec((D_MODEL, D_MODEL), c2), pl.BlockSpec((D_MODEL, D_MODEL), c2)],
        out_specs=[pl.BlockSpec((tm, D_MODEL), row), pl.BlockSpec((tm, D_MODEL), row),
                   pl.BlockSpec((tm, X_HEADS, X_HD), row3), pl.BlockSpec((tm, X_HEADS, X_HD), row3)],
        out_shape=[jax.ShapeDtypeStruct((m, D_MODEL), F32)] * 2
        + [jax.ShapeDtypeStruct((m, X_HEADS, X_HD), F32)] * 2,
        compiler_params=_params("parallel"),
        name="memkv",
    )(mem, g, wk, wv)


def _xattn_body(q_ref, mem_ref, mg_ref, wk_ref, wv_ref, x1_ref, wo_ref, x2_ref, k4_ref, v4_ref, k_scr, v_scr):
    @pl.when(pl.program_id(1) == 0)
    def _():
        mn = _rms(mem_ref[0], mg_ref[...]).astype(BF16)
        k = jnp.dot(mn, wk_ref[...], preferred_element_type=F32)
        v = jnp.dot(mn, wv_ref[...], preferred_element_type=F32)
        k_scr[...] = k.astype(BF16)
        v_scr[...] = v.astype(BF16)
        for h in range(X_HEADS):
            k4_ref[0, :, h, :] = k[:, h * X_HD:(h + 1) * X_HD]
            v4_ref[0, :, h, :] = v[:, h * X_HD:(h + 1) * X_HD]

    parts = []
    for h in range(X_HEADS):
        hs = slice(h * X_HD, (h + 1) * X_HD)
        s = _mm_nt(q_ref[0, :, hs], k_scr[:, hs]) * (X_HD ** -0.5)
        p = jnp.exp(s - jnp.max(s, axis=-1, keepdims=True))
        o = _mm(p, v_scr[:, hs]) / jnp.sum(p, axis=-1, keepdims=True)
        parts.append(o.astype(BF16))
    x2_ref[0] = x1_ref[0] + jnp.dot(jnp.concatenate(parts, axis=1), wo_ref[...], preferred_element_type=F32)


def _xattn_prompt(q3, mem3, mem_g, wk, wv, x13, wo, tq=1024):
    b, t, _ = q3.shape
    tok = lambda i, j: (i, j, 0)
    mem = lambda i, j: (i, 0, 0)
    mem4 = lambda i, j: (i, 0, 0, 0)
    c2 = lambda i, j: (0, 0)
    wspec = pl.BlockSpec((D_MODEL, D_MODEL), c2, pipeline_mode=pl.Buffered(1))
    kv_shape = jax.ShapeDtypeStruct((b, N_MEM, X_HEADS, X_HD), F32)
    return pl.pallas_call(
        _xattn_body,
        grid=(b, t // tq),
        in_specs=[pl.BlockSpec((1, tq, D_MODEL), tok), pl.BlockSpec((1, N_MEM, D_MODEL), mem),
                  pl.BlockSpec((1, D_MODEL), c2), wspec, wspec, pl.BlockSpec((1, tq, D_MODEL), tok), wspec],
        out_specs=[pl.BlockSpec((1, tq, D_MODEL), tok), pl.BlockSpec((1, N_MEM, X_HEADS, X_HD), mem4),
                   pl.BlockSpec((1, N_MEM, X_HEADS, X_HD), mem4)],
        out_shape=[jax.ShapeDtypeStruct((b, t, D_MODEL), F32), kv_shape, kv_shape],
        scratch_shapes=[pltpu.VMEM((N_MEM, D_MODEL), BF16), pltpu.VMEM((N_MEM, D_MODEL), BF16)],
        compiler_params=_params("parallel", "arbitrary"),
        name="xattn_prompt",
    )(q3, mem3, mem_g, wk, wv, x13, wo)


def _xattn_sample_body(q_ref, mk_ref, mv_ref, o_ref):
    for g in range(q_ref.shape[0]):
        q = q_ref[g]
        s = jnp.sum(mk_ref[g] * q[None], axis=-1, keepdims=True) * (X_HD ** -0.5)
        p = jnp.exp(s - jnp.max(s, axis=0, keepdims=True))
        o_ref[g] = jnp.sum(p * mv_ref[g], axis=0) / jnp.sum(p, axis=0)


def _xattn_sample(q, mk4, mv4, gs=8):
    ns = q.shape[0]
    q3 = q.astype(F32).reshape(ns, X_HEADS, X_HD)
    row = lambda i: (i, 0, 0)
    mem = lambda i: (i, 0, 0, 0)
    return pl.pallas_call(
        _xattn_sample_body,
        grid=(ns // gs,),
        in_specs=[pl.BlockSpec((gs, X_HEADS, X_HD), row), pl.BlockSpec((gs, N_MEM, X_HEADS, X_HD), mem),
                  pl.BlockSpec((gs, N_MEM, X_HEADS, X_HD), mem)],
        out_specs=pl.BlockSpec((gs, X_HEADS, X_HD), row),
        out_shape=jax.ShapeDtypeStruct((ns, X_HEADS, X_HD), F32),
        compiler_params=_params("parallel"),
        name="xattn_sample",
    )(q3, mk4, mv4)


def _resid_mm_body(x_ref, a_ref, w_ref, o_ref):
    o_ref[...] = x_ref[...] + jnp.dot(a_ref[...], w_ref[...], preferred_element_type=F32)


def _resid_mm(x, a, w):
    m = x.shape[0]
    return pl.pallas_call(
        _resid_mm_body,
        out_shape=jax.ShapeDtypeStruct((m, D_MODEL), F32),
        compiler_params=pltpu.CompilerParams(vmem_limit_bytes=VMEM_LIMIT),
        name="resid_mm",
    )(x, a, w)


def _ffn_body(x_ref, ng_ref, wg_ref, wu_ref, wd_ref, nf_ref, y_ref):
    x = x_ref[...]
    h = _rms(x, ng_ref[...]).astype(BF16)
    gate = jnp.dot(h, wg_ref[...], preferred_element_type=F32)
    up = jnp.dot(h, wu_ref[...], preferred_element_type=F32)
    x3 = x + _mm(_silu(gate) * up, wd_ref[...])
    y_ref[...] = _rms(x3, nf_ref[...])


def _ffn(x, ng, wg, wu, wd, nf, tm):
    m = x.shape[0]
    dff = wg.shape[1]
    row = lambda i: (i, 0)
    c2 = lambda i: (0, 0)
    single = pl.Buffered(1)
    return pl.pallas_call(
        _ffn_body,
        grid=(m // tm,),
        in_specs=[pl.BlockSpec((tm, D_MODEL), row), pl.BlockSpec((1, D_MODEL), c2),
                  pl.BlockSpec((D_MODEL, dff), c2, pipeline_mode=single),
                  pl.BlockSpec((D_MODEL, dff), c2, pipeline_mode=single),
                  pl.BlockSpec((dff, D_MODEL), c2, pipeline_mode=single),
                  pl.BlockSpec((1, D_MODEL), c2)],
        out_specs=pl.BlockSpec((tm, D_MODEL), row),
        out_shape=jax.ShapeDtypeStruct((m, D_MODEL), F32),
        compiler_params=_params("parallel"),
        name="ffn",
    )(x, ng, wg, wu, wd, nf)


def _rope_tables(pos):
    half = RET_DK // 2
    inv = ROPE_BASE ** (-jnp.arange(half, dtype=F32) / half)
    ang = pos.astype(F32)[:, None] * inv[None, :]
    cos, sin = jnp.cos(ang), jnp.sin(ang)
    return jnp.concatenate([cos, cos], axis=-1), jnp.concatenate([-sin, sin], axis=-1)


def _pad_lanes(v, offset):
    return jnp.zeros((BA_PAD,), F32).at[offset:offset + v.shape[0]].set(v)


def kernel(x_prompt, x_sample, state_ret, state_gdn, state_conv, cache_mem_k, cache_mem_v, mem_prompt,
           norm_mix_g, w_in, ret_gn_g, w_branch_a, gdn_conv_w, gdn_a_log, gdn_dt_bias, gdn_norm_g,
           w_branch_b, w_out, norm_x_g, mem_norm_g, w_xq, w_xk, w_xv, w_xo, norm_ffn_g, w_gate, w_up,
           w_down, norm_final_g):
    depth = w_in.shape[0]
    assert depth == 1, "single-layer kernel"
    b, t, _ = x_prompt.shape
    ns = x_sample.shape[0]
    l = 0

    w = w_in[l]
    ba0, g0 = COL_GATE, COL_GATE + 2 * GDN_HEADS
    w_bf = w.astype(BF16)
    w_gates = w_bf[:, g0:]
    w_ba = jnp.pad(w[:, ba0:g0], ((0, 0), (0, BA_PAD - 2 * GDN_HEADS))).astype(BF16)
    w_bat = w_ba.T
    row = lambda v: v.reshape(1, -1)
    wa, wb, wo = w_branch_a[l].astype(BF16), w_branch_b[l].astype(BF16), w_out[l].astype(BF16)
    wq, wk, wv, wxo = w_xq[l].astype(BF16), w_xk[l].astype(BF16), w_xv[l].astype(BF16), w_xo[l].astype(BF16)
    wg, wu, wd = w_gate[l].astype(BF16), w_up[l].astype(BF16), w_down[l].astype(BF16)
    alog_r = _pad_lanes(gdn_a_log[l], GDN_HEADS).reshape(1, BA_PAD)
    dt_r = _pad_lanes(gdn_dt_bias[l], GDN_HEADS).reshape(1, BA_PAD)
    alog_c, dt_c = alog_r.reshape(BA_PAD, 1), dt_r.reshape(BA_PAD, 1)
    cos_p, sin_p = _rope_tables(jnp.arange(t))
    cos_s, sin_s = _rope_tables(PAST_LEN + jnp.arange(1))

    mix_g = row(norm_mix_g[l])
    xp = x_prompt.reshape(b * t, D_MODEL)
    oa_p, sr_p = _retention_prompt(x_prompt, mix_g, w_bf, cos_p, sin_p, row(ret_gn_g[l]))
    ob_p, sg_p, sc_p = _gdn_prompt(x_prompt, mix_g, w_bf, w_ba, w_bat, gdn_conv_w[l], alog_r, dt_r,
                                   alog_c, dt_c, row(gdn_norm_g[l]))
    x1_p, q_p = _merge(xp, mix_g, w_gates, oa_p.reshape(b * t, -1), ob_p.reshape(b * t, -1), wa, wb, wo,
                       row(norm_x_g[l]), wq, tm=1024)
    x2_p, mk4_p, mv4_p = _xattn_prompt(q_p.reshape(b, t, D_MODEL), mem_prompt, row(mem_norm_g[l]), wk, wv,
                                       x1_p.reshape(b, t, D_MODEL), wxo)
    y_p = _ffn(x2_p.reshape(b * t, D_MODEL), row(norm_ffn_g[l]), wg, wu, wd, row(norm_final_g), tm=512)

    xs = x_sample.reshape(ns, D_MODEL)
    proj_s, ba_s = _inproj(xs, mix_g, w_bf, COL_GATE, w_ba)
    oa_s, ob_s, sr_s, sg_s, sc_s = _sample_step(proj_s[:, :COL_GDN], proj_s[:, COL_GDN:], ba_s, state_ret[l],
                                                state_gdn[l], state_conv[l],
                                                cos_s, sin_s, gdn_conv_w[l], gdn_a_log[l], gdn_dt_bias[l],
                                                ret_gn_g[l], row(gdn_norm_g[l]))
    x1_s, q_s = _merge(xs, mix_g, w_gates, oa_s.reshape(ns, -1), ob_s.reshape(ns, -1), wa, wb, wo,
                       row(norm_x_g[l]), wq, tm=ns)
    o_s = _xattn_sample(q_s, cache_mem_k[l], cache_mem_v[l])
    x2_s = _resid_mm(x1_s, o_s.reshape(ns, D_MODEL).astype(BF16), wxo)
    y_s = _ffn(x2_s, row(norm_ffn_g[l]), wg, wu, wd, row(norm_final_g), tm=ns)

    return (y_p.reshape(b, t, D_MODEL), y_s.reshape(ns, 1, D_MODEL),
            sr_p[None], sg_p[None], sc_p[None],
            mk4_p.reshape(1, b, N_MEM, X_HEADS, X_HD), mv4_p.reshape(1, b, N_MEM, X_HEADS, X_HD),
            sr_s[None], sg_s[None], sc_s[None])
```
